```python
import math
import jax
import jax.numpy as jnp
from jax import lax
import numpy as np

D_MODEL = 2048
BATCH = 1
SEQ = 16384
DEPTH = 2

CTX_LEN = 256
GRID_W = 64
NORM_EPS = 1e-6

HY_C = D_MODEL // 2
HY_SHORT = 3
HY_EMB = 33
HY_BANDS = (HY_EMB - 1) // 2
HY_FH = 64
HY_DECAY_TARGET = 1e-2
HY_FAST_PCT = 0.3
HY_SLOW_PCT = 1.5
HY_MOD_SHIFT = 0.05

HG_HEADS = 8
HG_DK = 128
HG_DV = 128
HG_K = HG_HEADS * HG_DK
HG_V = HG_HEADS * HG_DV

GL_HEADS = 4
GL_DK = 128
GL_DV = 256
GL_K = GL_HEADS * GL_DK
GL_V = GL_HEADS * GL_DV
GL_RANK = 16
GL_TAU = 16.0

CHUNK = 64
N_BRANCH = 3
BR_W = D_MODEL // 2

HG_COLS = 3 * HG_K + 2 * HG_V
GL_COLS = 2 * GL_K + 2 * GL_V + 2 * GL_RANK
REC_COLS = HG_COLS + GL_COLS
HY_COLS = 3 * HY_C
MERGE_COLS = N_BRANCH * D_MODEL
N_COLS = REC_COLS + HY_COLS + MERGE_COLS

N_GROUPS = 4
EXP_PER_GROUP = 8
N_EXPERTS = N_GROUPS * EXP_PER_GROUP
TOP_K = 2
D_FF = D_MODEL // 4
MOE_BLOCK = 256

kernel_name = 'hybrid_hyena_hgrn2_gla_hmoe_dit'


def _split(a, sizes):
    return jnp.split(a, np.cumsum(sizes)[:-1].tolist(), axis=-1)


def rms_norm(x, w):
    x32 = x.astype(jnp.float32)
    y = x32 * lax.rsqrt(jnp.mean(x32 * x32, axis=-1, keepdims=True) + NORM_EPS)
    return (y * w.astype(jnp.float32)).astype(x.dtype)


def modulate(x, w, shift, scale):
    return rms_norm(x, w) * (1.0 + scale) + shift


def chunk_gated_scan(q, k, v, log_a, s0, with_out):
    B, L, H, dk = q.shape
    dv = v.shape[-1]
    n = L // CHUNK

    def chunks(a):
        return a.astype(jnp.float32).reshape(B, n, CHUNK, H, a.shape[-1]).transpose(1, 0, 3, 2, 4)

    lower = jnp.tril(jnp.ones((CHUNK, CHUNK), dtype=bool))[:, :, None]

    def step(s, inp):
        qc, kc, vc, gc = inp
        b = jnp.cumsum(gc, axis=2)
        b_last = b[:, :, -1:, :]
        s_new = jnp.exp(b_last[:, :, 0, :])[..., None] * s + jnp.einsum('bhsk,bhsv->bhkv', kc * jnp.exp(b_last - b), vc)
        if not with_out:
            return s_new, None
        o_inter = jnp.einsum('bhtk,bhkv->bhtv', qc * jnp.exp(b), s)
        rel = b[:, :, :, None, :] - b[:, :, None, :, :]
        decay = jnp.exp(jnp.where(lower, rel, -jnp.inf))
        scores = jnp.einsum('bhtk,bhsk,bhtsk->bhts', qc, kc, decay)
        return s_new, o_inter + jnp.einsum('bhts,bhsv->bhtv', scores, vc)

    s_fin, o = lax.scan(step, s0, (chunks(q), chunks(k), chunks(v), chunks(log_a)))
    if not with_out:
        return None, s_fin
    return o.transpose(1, 0, 3, 2, 4).reshape(B, L, H, dv), s_fin


def bidir_scan(q, v, k_f, g_f, k_b, g_b, s0_f, s0_b, with_out):
    flip = lambda a: jnp.flip(a, axis=1)
    o_f, s_f = chunk_gated_scan(q, k_f, v, g_f, s0_f, with_out)
    o_b, s_b = chunk_gated_scan(flip(q), flip(k_b), flip(v), flip(g_b), s0_b, with_out)
    o = (o_f + flip(o_b)) if with_out else None
    return o, s_f, s_b


def hgrn2_branch(z_hg, lb, norm_w, s0_f, s0_b, with_out):
    q, f_f, f_b, i, g = _split(z_hg, [HG_K, HG_K, HG_K, HG_V, HG_V])
    B, L, _ = q.shape
    heads = lambda a, d: a.reshape(B, L, HG_HEADS, d)

    def forget(zf, lbd):
        zf32 = zf.astype(jnp.float32)
        log_f = jnp.logaddexp(jnp.log(lbd), jnp.log1p(-lbd) + jax.nn.log_sigmoid(zf32))
        key = (1.0 - lbd) * jax.nn.sigmoid(-zf32)
        return heads(key, HG_DK), heads(log_f, HG_DK)

    k_f, g_f = forget(f_f, lb[0])
    k_b, g_b = forget(f_b, lb[1])
    o, s_f, s_b = bidir_scan(heads(jax.nn.silu(q), HG_DK), heads(i, HG_DV), k_f, g_f, k_b, g_b, s0_f, s0_b, with_out)
    if not with_out:
        return None, s_f, s_b
    y = rms_norm(o, norm_w).reshape(B, L, HG_V) * jax.nn.sigmoid(g.astype(jnp.float32))
    return y.astype(z_hg.dtype), s_f, s_b


def gla_branch(z_gl, w_a2, b_a, norm_w, s0_f, s0_b, with_out):
    q, k, v, r, a_f, a_b = _split(z_gl, [GL_K, GL_K, GL_V, GL_V, GL_RANK, GL_RANK])
    B, L, _ = q.shape
    heads = lambda a, d: a.reshape(B, L, GL_HEADS, d)

    def log_alpha(a, d):
        return heads(jax.nn.log_sigmoid((a @ w_a2[d] + b_a[d]).astype(jnp.float32)) / GL_TAU, GL_DK)

    kh = heads(k, GL_DK)
    o, s_f, s_b = bidir_scan(heads(q, GL_DK) * GL_DK ** -0.5, heads(v, GL_DV), kh, log_alpha(a_f, 0), kh, log_alpha(a_b, 1), s0_f, s0_b, with_out)
    if not with_out:
        return None, s_f, s_b
    y = rms_norm(o, norm_w).reshape(B, L, GL_V) * jax.nn.silu(r.astype(jnp.float32))
    return y.astype(z_gl.dtype), s_f, s_b


def hyena_filters(L, w1, b1, w2, b2, w3, b3, freq):
    t = jnp.linspace(0.0, 1.0, L, dtype=jnp.float32)[:, None]
    ang = 2.0 * math.pi * jnp.arange(L, dtype=jnp.float32)[:, None] / L
    bands = jnp.linspace(1e-4, HY_BANDS - 1, HY_BANDS, dtype=jnp.float32)[None, :]
    feats = jnp.concatenate([t, jnp.cos(bands * ang), -jnp.sin(bands * ang)], axis=-1)
    h = jnp.sin(freq * (feats @ w1 + b1))
    h = jnp.sin(freq * (h @ w2 + b2))
    h = (h @ w3 + b3).astype(jnp.float32)
    deltas = jnp.abs(jnp.linspace(math.log(HY_DECAY_TARGET) / HY_SLOW_PCT, math.log(HY_DECAY_TARGET) / HY_FAST_PCT, HY_C, dtype=jnp.float32))
    decay = jnp.exp(-t * deltas)
    h = h * (jnp.tile(decay, (1, 2)) + HY_MOD_SHIFT)
    h = h / jnp.sum(jnp.abs(h), axis=0, keepdims=True)
    return h[:, :HY_C], h[:, HY_C:]


def fft_long_conv(v, h):
    L = v.shape[1]
    n = 2 * L
    v_f = jnp.fft.rfft(v.astype(jnp.float32), n=n, axis=1)
    h_f = jnp.fft.rfft(h, n=n, axis=0)
    return jnp.fft.irfft(v_f * h_f[None], n=n, axis=1)[:, :L]


def short_conv_centered(x, w, b):
    L = x.shape[1]
    pad = HY_SHORT // 2
    xp = jnp.pad(x, ((0, 0), (pad, pad), (0, 0)))
    return sum(w[j] * xp[:, j:j + L] for j in range(HY_SHORT)) + b


def hyena_branch(z_hy, conv_w, conv_b, w1, b1, w2, b2, w3, b3, freq, skip):
    L = z_hy.shape[1]
    x0, x1, v = _split(short_conv_centered(z_hy, conv_w, conv_b), [HY_C, HY_C, HY_C])
    h_f, h_b = hyena_filters(L, w1, b1, w2, b2, w3, b3, freq)
    v = v * x1
    y = fft_long_conv(v, h_f) + jnp.flip(fft_long_conv(jnp.flip(v, axis=1), h_b), axis=1) + v * skip
    return y.astype(z_hy.dtype) * x0


def merge_branches(z_gate, ys, w_branch, w_out):
    B, L, _ = z_gate.shape
    gates = jax.nn.sigmoid(z_gate.reshape(B, L, N_BRANCH, D_MODEL))
    merged = sum(gates[:, :, br] * (ys[br] @ w_branch[br]) for br in range(N_BRANCH))
    return merged @ w_out


def token_mixer(u, uc, w_in, b_in, hy_params, lb, hg_norm_w, gl_w_a2, gl_b_a, gl_norm_w, w_branch, w_out, need_ctx):
    B = u.shape[0]
    z = u @ w_in + b_in
    zc = (uc @ w_in + b_in) if need_ctx else (uc @ w_in[:, :REC_COLS] + b_in[:REC_COLS])
    zeros = lambda h, dk, dv: jnp.zeros((B, h, dk, dv), jnp.float32)
    yc_hg, hg_sf, hg_sb = hgrn2_branch(zc[..., :HG_COLS], lb, hg_norm_w, zeros(HG_HEADS, HG_DK, HG_DV), zeros(HG_HEADS, HG_DK, HG_DV), need_ctx)
    yc_gl, gl_sf, gl_sb = gla_branch(zc[..., HG_COLS:REC_COLS], gl_w_a2, gl_b_a, gl_norm_w, zeros(GL_HEADS, GL_DK, GL_DV), zeros(GL_HEADS, GL_DK, GL_DV), need_ctx)
    y_hg, _, _ = hgrn2_branch(z[..., :HG_COLS], lb, hg_norm_w, hg_sf, hg_sb, True)
    y_gl, _, _ = gla_branch(z[..., HG_COLS:REC_COLS], gl_w_a2, gl_b_a, gl_norm_w, gl_sf, gl_sb, True)
    y_hy = hyena_branch(z[..., REC_COLS:REC_COLS + HY_COLS], *hy_params)
    out = merge_branches(z[..., REC_COLS + HY_COLS:], (y_hy, y_hg, y_gl), w_branch, w_out)
    if not need_ctx:
        return out, None
    yc_hy = hyena_branch(zc[..., REC_COLS:REC_COLS + HY_COLS], *hy_params)
    out_c = merge_branches(zc[..., REC_COLS + HY_COLS:], (yc_hy, yc_hg, yc_gl), w_branch, w_out)
    return out, out_c


def hier_moe(u, w_rg, b_rg, w_re, b_re, w_gate, w_up, w_down):
    B, L, D = u.shape
    n = B * L
    x = u.reshape(n, D)
    p_grp = jax.nn.softmax((x @ w_rg + b_rg).astype(jnp.float32), axis=-1)
    grp = jnp.argmax(p_grp, axis=-1).astype(jnp.int32)
    p_top = jnp.max(p_grp, axis=-1)
    le = (x @ w_re + b_re).reshape(n, N_GROUPS, EXP_PER_GROUP)
    le = jnp.take_along_axis(le, grp[:, None, None], axis=1)[:, 0]
    top_p, top_i = lax.top_k(jax.nn.softmax(le.astype(jnp.float32), axis=-1), TOP_K)
    weight = p_top[:, None] * top_p / jnp.sum(top_p, axis=-1, keepdims=True)
    expert = grp[:, None] * EXP_PER_GROUP + top_i.astype(jnp.int32)
    a = n * TOP_K
    e_flat = expert.reshape(a)
    tok_flat = jnp.repeat(jnp.arange(n, dtype=jnp.int32), TOP_K)
    order = jnp.argsort(e_flat)
    e_s, tok_s, w_s = e_flat[order], tok_flat[order], weight.reshape(a)[order]
    counts = jnp.zeros((N_EXPERTS,), jnp.int32).at[e_flat].add(1)
    off = jnp.cumsum(counts) - counts
    padded = (counts + MOE_BLOCK - 1) // MOE_BLOCK * MOE_BLOCK
    pad_end = jnp.cumsum(padded)
    pad_off = pad_end - padded
    pos = pad_off[e_s] + jnp.arange(a, dtype=jnp.int32) - off[e_s]
    p_len = a + N_EXPERTS * MOE_BLOCK
    n_blk = p_len // MOE_BLOCK
    buf_tok = jnp.zeros((p_len,), jnp.int32).at[pos].set(tok_s)
    buf_w = jnp.zeros((p_len,), x.dtype).at[pos].set(w_s.astype(x.dtype))
    blk_start = jnp.arange(n_blk, dtype=jnp.int32) * MOE_BLOCK
    blk_exp = jnp.minimum(jnp.sum(pad_end[None, :] <= blk_start[:, None], axis=1), N_EXPERTS - 1)

    def run_block(args):
        tok, e = args
        xb = x[tok]
        return (jax.nn.silu(xb @ w_gate[e]) * (xb @ w_up[e])) @ w_down[e]

    y = lax.map(run_block, (buf_tok.reshape(n_blk, MOE_BLOCK), blk_exp)).reshape(p_len, D) * buf_w[:, None]
    return jnp.zeros_like(x).at[buf_tok].add(y).reshape(B, L, D)


def setup_inputs(seed: int = 0) -> dict:
    key = jax.random.key(seed)
    ks = iter(jax.random.split(key, 48))
    D = D_MODEL

    def nrm(shape, scale):
        return jax.random.normal(next(ks), shape, jnp.float32) * scale

    def gain(shape):
        return 1.0 + nrm(shape, 0.05)

    return {
        'x': nrm((BATCH, SEQ, D), 1.0),
        'c': nrm((BATCH, D), 1.0),
        'ctx': nrm((BATCH, CTX_LEN, D), 1.0),
        'c_ctx': nrm((D,), 1.0),
        'w_mod': nrm((DEPTH, D, 6 * D), 0.5 * D ** -0.5),
        'b_mod': nrm((DEPTH, 6 * D), 0.02),
        'norm_mix_w': gain((DEPTH, D)),
        'norm_ffn_w': gain((DEPTH, D)),
        'w_in': nrm((DEPTH, D, N_COLS), D ** -0.5),
        'b_in': nrm((DEPTH, N_COLS), 0.02),
        'hy_conv_w': nrm((DEPTH, HY_SHORT, HY_COLS), HY_SHORT ** -0.5),
        'hy_conv_b': nrm((DEPTH, HY_COLS), 0.02),
        'hy_f_w1': nrm((DEPTH, HY_EMB, HY_FH), HY_EMB ** -0.5),
        'hy_f_b1': nrm((DEPTH, HY_FH), 0.02),
        'hy_f_w2': nrm((DEPTH, HY_FH, HY_FH), HY_FH ** -0.5),
        'hy_f_b2': nrm((DEPTH, HY_FH), 0.02),
        'hy_f_w3': nrm((DEPTH, HY_FH, 2 * HY_C), HY_FH ** -0.5),
        'hy_f_b3': nrm((DEPTH, 2 * HY_C), 0.02),
        'hy_f_freq': gain((DEPTH, HY_FH)),
        'hy_skip': nrm((DEPTH, HY_C), 0.5),
        'hg_lb_raw': nrm((DEPTH, 2, HG_K), 0.5),
        'hg_norm_w': gain((DEPTH, HG_DV)),
        'gl_w_a2': nrm((DEPTH, 2, GL_RANK, GL_K), GL_RANK ** -0.5),
        'gl_b_a': nrm((DEPTH, 2, GL_K), 0.1),
        'gl_norm_w': gain((DEPTH, GL_DV)),
        'w_branch': nrm((DEPTH, N_BRANCH, BR_W, D), BR_W ** -0.5),
        'w_out': nrm((DEPTH, D, D), D ** -0.5),
        'w_rg': nrm((DEPTH, D, N_GROUPS), D ** -0.5),
        'b_rg': nrm((DEPTH, N_GROUPS), 0.01),
        'w_re': nrm((DEPTH, D, N_EXPERTS), D ** -0.5),
        'b_re': nrm((DEPTH, N_EXPERTS), 0.01),
        'w_gate': nrm((DEPTH, N_EXPERTS, D, D_FF), D ** -0.5),
        'w_up': nrm((DEPTH, N_EXPERTS, D, D_FF), D ** -0.5),
        'w_down': nrm((DEPTH, N_EXPERTS, D_FF, D), D_FF ** -0.5),
        'final_norm_w': gain((D,)),
    }


def reference(x, c, ctx, c_ctx, w_mod, b_mod, norm_mix_w, norm_ffn_w, w_in, b_in, hy_conv_w, hy_conv_b, hy_f_w1, hy_f_b1, hy_f_w2, hy_f_b2, hy_f_w3, hy_f_b3, hy_f_freq, hy_skip, hg_lb_raw, hg_norm_w, gl_w_a2, gl_b_a, gl_norm_w, w_branch, w_out, w_rg, b_rg, w_re, b_re, w_gate, w_up, w_down, final_norm_w):
    lb_all = jnp.cumsum(jax.nn.softmax(hg_lb_raw.astype(jnp.float32), axis=0), axis=0)
    lb_all = lb_all - lb_all[:1]
    h, hc = x, ctx
    for l in range(DEPTH):
        need_ctx = l < DEPTH - 1
        mod = jax.nn.silu(c) @ w_mod[l] + b_mod[l]
        mod_c = jax.nn.silu(c_ctx) @ w_mod[l] + b_mod[l]
        sh1, sc1, gt1, sh2, sc2, gt2 = jnp.split(mod[:, None, :], 6, axis=-1)
        sh1c, sc1c, gt1c, sh2c, sc2c, gt2c = jnp.split(mod_c, 6)
        hy_params = (hy_conv_w[l], hy_conv_b[l], hy_f_w1[l], hy_f_b1[l], hy_f_w2[l], hy_f_b2[l], hy_f_w3[l], hy_f_b3[l], hy_f_freq[l], hy_skip[l])
        u = modulate(h, norm_mix_w[l], sh1, sc1)
        uc = modulate(hc, norm_mix_w[l], sh1c, sc1c)
        mix, mix_c = token_mixer(u, uc, w_in[l], b_in[l], hy_params, lb_all[l], hg_norm_w[l], gl_w_a2[l], gl_b_a[l], gl_norm_w[l], w_branch[l], w_out[l], need_ctx)
        h = h + gt1 * mix
        moe_args = (w_rg[l], b_rg[l], w_re[l], b_re[l], w_gate[l], w_up[l], w_down[l])
        h = h + gt2 * hier_moe(modulate(h, norm_ffn_w[l], sh2, sc2), *moe_args)
        if need_ctx:
            hc = hc + gt1c * mix_c
            hc = hc + gt2c * hier_moe(modulate(hc, norm_ffn_w[l], sh2c, sc2c), *moe_args)
    return rms_norm(h, final_norm_w)
```

```python
import functools
import math

import jax
import jax.numpy as jnp
import numpy as np
from jax import lax
from jax.experimental import pallas as pl
from jax.experimental.pallas import tpu as pltpu

D_MODEL = 2048
NORM_EPS = 1e-6

HY_C = D_MODEL // 2
HY_EMB = 33
HY_BANDS = (HY_EMB - 1) // 2
HY_DECAY_TARGET = 1e-2
HY_FAST_PCT = 0.3
HY_SLOW_PCT = 1.5
HY_MOD_SHIFT = 0.05

HG_HEADS = 8
HG_DK = 128
HG_DV = 128
HG_K = HG_HEADS * HG_DK
HG_V = HG_HEADS * HG_DV

GL_HEADS = 4
GL_DK = 128
GL_DV = 256
GL_K = GL_HEADS * GL_DK
GL_V = GL_HEADS * GL_DV
GL_RANK = 16
GL_TAU = 16.0

N_BRANCH = 3
HG_COLS = 3 * HG_K + 2 * HG_V
GL_COLS = 2 * GL_K + 2 * GL_V + 2 * GL_RANK
REC_COLS = HG_COLS + GL_COLS
HY_COLS = 3 * HY_C
MERGE_COLS = N_BRANCH * D_MODEL

N_GROUPS = 4
EXP_PER_GROUP = 8
N_EXPERTS = N_GROUPS * EXP_PER_GROUP
TOP_K = 2
D_FF = D_MODEL // 4
MOE_BLOCK = 256

SCAN_CHUNK = 128

VMEM_LIMIT_BYTES = 56 * 1024 * 1024


def _cparams(*sem):
    return pltpu.CompilerParams(dimension_semantics=sem, vmem_limit_bytes=VMEM_LIMIT_BYTES)


def _mm_kernel(x_ref, w_ref, o_ref):
    o_ref[...] = jnp.dot(x_ref[...], w_ref[...], preferred_element_type=jnp.float32)


def _matmul(x, w, tm=512, tn=512):
    m, k = x.shape
    n = w.shape[1]
    tm = min(tm, -(-m // 8) * 8)
    mp = -(-m // tm) * tm
    np_ = -(-n // tn) * tn
    if mp != m:
        x = jnp.pad(x, ((0, mp - m), (0, 0)))
    if np_ != n:
        w = jnp.pad(w, ((0, 0), (0, np_ - n)))
    out = pl.pallas_call(
        _mm_kernel,
        grid=(mp // tm, np_ // tn),
        in_specs=[pl.BlockSpec((tm, k), lambda i, j: (i, 0)),
                  pl.BlockSpec((k, tn), lambda i, j: (0, j))],
        out_specs=pl.BlockSpec((tm, tn), lambda i, j: (i, j)),
        out_shape=jax.ShapeDtypeStruct((mp, np_), jnp.float32),
        compiler_params=_cparams("parallel", "arbitrary"),
        name="dense_matmul",
    )(x, w)
    return out[:m, :n]


def _bf(a):
    return a.astype(jnp.bfloat16)


def _rms(x, w):
    return x * lax.rsqrt(jnp.mean(x * x, axis=-1, keepdims=True) + NORM_EPS) * w


def _modulate(x, w, shift, scale):
    return _rms(x, w) * (1.0 + scale) + shift


def _level_tables(c):
    t = np.arange(c)
    tabs = []
    for lvl in range(int(math.log2(c))):
        upper = ((t >> lvl) & 1).astype(bool)
        ref = np.minimum(((t >> lvl) | 1) << lvl, c - 1)
        same = (t[:, None] >> (lvl + 1)) == (t[None, :] >> (lvl + 1))
        tabs.append((upper, ref, same))
    return tabs


def _scan_dir(q, k, v, g, s0):
    L, H, dk = q.shape
    dv = v.shape[-1]
    c = min(SCAN_CHUNK, L)
    n = L // c
    ch = lambda a: a.reshape(n, c, H, a.shape[-1]).transpose(0, 2, 1, 3)
    q, k, v, g = ch(q), ch(k), ch(v), ch(g)
    b = jnp.cumsum(g, axis=2)
    b_last = b[:, :, -1:, :]
    u = jnp.einsum('nhck,nhcv->nhkv', k * jnp.exp(b_last - b), v)
    d = jnp.exp(b_last[:, :, 0, :])

    def step(s, inp):
        d_i, u_i = inp
        return d_i[..., None] * s + u_i, s

    s_fin, s_prev = lax.scan(step, s0, (d, u))
    o = jnp.einsum('nhck,nhkv->nhcv', q * jnp.exp(b), s_prev)
    eye = jnp.eye(c, dtype=bool)
    scores = jnp.where(eye, jnp.einsum('nhtk,nhsk->nhts', q, k), 0.0)
    for upper, ref, same in _level_tables(c):
        up = jnp.asarray(upper)[None, None, :, None]
        b_ref = b[:, :, ref, :]
        e = jnp.exp(jnp.where(up, b - b_ref, b_ref - b))
        qe = jnp.where(up, q * e, 0.0)
        ke = jnp.where(up, 0.0, k * e)
        scores = scores + jnp.where(jnp.asarray(same), jnp.einsum('nhtk,nhsk->nhts', qe, ke), 0.0)
    o = o + jnp.einsum('nhts,nhsv->nhtv', scores, v)
    return o.transpose(0, 2, 1, 3).reshape(L, H, dv), s_fin


def _bidir(q, v, k_f, g_f, k_b, g_b, s0_f, s0_b):
    fl = lambda a: jnp.flip(a, axis=0)
    o_f, s_f = _scan_dir(q, k_f, v, g_f, s0_f)
    o_b, s_b = _scan_dir(fl(q), fl(k_b), fl(v), fl(g_b), s0_b)
    return o_f + fl(o_b), s_f, s_b


def _log_sigmoid(z):
    return jnp.minimum(z, 0.0) - jnp.log1p(jnp.exp(-jnp.abs(z)))


def _hgrn2(z_hg, lb, norm_w, s0_f, s0_b):
    L = z_hg.shape[0]
    q, f_f, f_b, i, g = jnp.split(z_hg, [HG_K, 2 * HG_K, 3 * HG_K, 3 * HG_K + HG_V], axis=-1)
    hd = lambda a, d: a.reshape(L, HG_HEADS, d)

    def forget(zf, lbd):
        log_f = jnp.logaddexp(jnp.log(lbd), jnp.log1p(-lbd) + _log_sigmoid(zf))
        key = (1.0 - lbd) * jax.nn.sigmoid(-zf)
        return hd(key, HG_DK), hd(log_f, HG_DK)

    k_f, g_f = forget(f_f, lb[0])
    k_b, g_b = forget(f_b, lb[1])
    o, s_f, s_b = _bidir(hd(jax.nn.silu(q), HG_DK), hd(i, HG_DV), k_f, g_f, k_b, g_b, s0_f, s0_b)
    y = _rms(o, norm_w).reshape(L, HG_V) * jax.nn.sigmoid(g)
    return y, s_f, s_b


def _gla(z_gl, w_a2, b_a, norm_w, s0_f, s0_b):
    L = z_gl.shape[0]
    q, k, v, r, a_f, a_b = jnp.split(
        z_gl, [GL_K, 2 * GL_K, 2 * GL_K + GL_V, 2 * GL_K + 2 * GL_V, 2 * GL_K + 2 * GL_V + GL_RANK], axis=-1)
    hd = lambda a, d: a.reshape(L, GL_HEADS, d)
    la = lambda a, d: hd(_log_sigmoid(a @ w_a2[d] + b_a[d]) / GL_TAU, GL_DK)
    kh = hd(k, GL_DK)
    o, s_f, s_b = _bidir(hd(q, GL_DK) * GL_DK ** -0.5, hd(v, GL_DV), kh, la(a_f, 0), kh, la(a_b, 1), s0_f, s0_b)
    y = _rms(o, norm_w).reshape(L, GL_V) * jax.nn.silu(r)
    return y, s_f, s_b


def _hyena_filters(L, w1, b1, w2, b2, w3, b3, freq):
    t = jnp.linspace(0.0, 1.0, L, dtype=jnp.float32)[:, None]
    ang = 2.0 * math.pi * jnp.arange(L, dtype=jnp.float32)[:, None] / L
    bands = jnp.linspace(1e-4, HY_BANDS - 1, HY_BANDS, dtype=jnp.float32)[None, :]
    feats = jnp.concatenate([t, jnp.cos(bands * ang), -jnp.sin(bands * ang)], axis=-1)
    h = jnp.sin(freq * (feats @ w1 + b1))
    h = jnp.sin(freq * (h @ w2 + b2))
    h = h @ w3 + b3
    deltas = jnp.abs(jnp.linspace(math.log(HY_DECAY_TARGET) / HY_SLOW_PCT, math.log(HY_DECAY_TARGET) / HY_FAST_PCT,
                                  HY_C, dtype=jnp.float32))
    decay = jnp.exp(-t * deltas)
    h = h * (jnp.tile(decay, (1, 2)) + HY_MOD_SHIFT)
    h = h / jnp.sum(jnp.abs(h), axis=0, keepdims=True)
    return h[:, :HY_C], h[:, HY_C:]


def _two_sided_conv(v, h_f, h_b):
    L = v.shape[0]
    n = 2 * L
    kern = jnp.zeros((n, v.shape[1]), jnp.float32)
    kern = kern.at[:L].set(h_f)
    kern = kern.at[0].add(h_b[0])
    kern = kern.at[L + 1:].set(jnp.flip(h_b[1:], axis=0))
    y = jnp.fft.irfft(jnp.fft.rfft(v, n=n, axis=0) * jnp.fft.rfft(kern, axis=0), n=n, axis=0)
    return y[:L]


def _hyena(z_hy, conv_w, conv_b, fparams, skip):
    L = z_hy.shape[0]
    zp = jnp.pad(z_hy, ((1, 1), (0, 0)))
    zc = conv_w[0] * zp[0:L] + conv_w[1] * zp[1:L + 1] + conv_w[2] * zp[2:L + 2] + conv_b
    x0, x1, v = zc[:, :HY_C], zc[:, HY_C:2 * HY_C], zc[:, 2 * HY_C:]
    h_f, h_b = _hyena_filters(L, *fparams)
    v = v * x1
    return (_two_sided_conv(v, h_f, h_b) + v * skip) * x0


def _merge(z_gate, ys, w_branch, w_out):
    merged = 0.0
    for br in range(N_BRANCH):
        gate = jax.nn.sigmoid(z_gate[:, br * D_MODEL:(br + 1) * D_MODEL])
        merged = merged + gate * _matmul(_bf(ys[br]), _bf(w_branch[br]))
    return _matmul(_bf(merged), _bf(w_out))


def _mixer(u, uc, p, need_ctx):
    w_in = _bf(p['w_in'])
    z = _matmul(_bf(u), w_in) + p['b_in']
    if need_ctx:
        zc = _matmul(_bf(uc), w_in) + p['b_in']
    else:
        zc = _matmul(_bf(uc), w_in[:, :REC_COLS]) + p['b_in'][:REC_COLS]
    zeros = lambda h, dk, dv: jnp.zeros((h, dk, dv), jnp.float32)
    yc_hg, hg_sf, hg_sb = _hgrn2(zc[:, :HG_COLS], p['lb'], p['hg_norm_w'],
                                 zeros(HG_HEADS, HG_DK, HG_DV), zeros(HG_HEADS, HG_DK, HG_DV))
    yc_gl, gl_sf, gl_sb = _gla(zc[:, HG_COLS:REC_COLS], p['gl_w_a2'], p['gl_b_a'], p['gl_norm_w'],
                               zeros(GL_HEADS, GL_DK, GL_DV), zeros(GL_HEADS, GL_DK, GL_DV))
    y_hg, _, _ = _hgrn2(z[:, :HG_COLS], p['lb'], p['hg_norm_w'], hg_sf, hg_sb)
    y_gl, _, _ = _gla(z[:, HG_COLS:REC_COLS], p['gl_w_a2'], p['gl_b_a'], p['gl_norm_w'], gl_sf, gl_sb)
    hy = (p['hy_conv_w'], p['hy_conv_b'], p['hy_f'], p['hy_skip'])
    y_hy = _hyena(z[:, REC_COLS:REC_COLS + HY_COLS], *hy)
    out = _merge(z[:, REC_COLS + HY_COLS:], (y_hy, y_hg, y_gl), p['w_branch'], p['w_out'])
    if not need_ctx:
        return out, None
    yc_hy = _hyena(zc[:, REC_COLS:REC_COLS + HY_COLS], *hy)
    out_c = _merge(zc[:, REC_COLS + HY_COLS:], (yc_hy, yc_hg, yc_gl), p['w_branch'], p['w_out'])
    return out, out_c


def _ffn_kernel(blk_exp_ref, n_used_ref, x_ref, wg_ref, wu_ref, wd_ref, o_ref):
    i = pl.program_id(0)

    @pl.when(i < n_used_ref[0])
    def _():
        x = x_ref[...]
        hg = jnp.dot(x, wg_ref[0], preferred_element_type=jnp.float32)
        hu = jnp.dot(x, wu_ref[0], preferred_element_type=jnp.float32)
        act = (hg * jax.nn.sigmoid(hg) * hu).astype(jnp.bfloat16)
        o_ref[...] = jnp.dot(act, wd_ref[0], preferred_element_type=jnp.float32)

    @pl.when(i >= n_used_ref[0])
    def _():
        o_ref[...] = jnp.zeros_like(o_ref)


def _grouped_ffn(xg, blk_exp, n_used, w_gate, w_up, w_down):
    p_len, d = xg.shape
    n_blk = p_len // MOE_BLOCK
    grid_spec = pltpu.PrefetchScalarGridSpec(
        num_scalar_prefetch=2,
        grid=(n_blk,),
        in_specs=[pl.BlockSpec((MOE_BLOCK, d), lambda i, be, nu: (i, 0)),
                  pl.BlockSpec((1, d, D_FF), lambda i, be, nu: (be[i], 0, 0)),
                  pl.BlockSpec((1, d, D_FF), lambda i, be, nu: (be[i], 0, 0)),
                  pl.BlockSpec((1, D_FF, d), lambda i, be, nu: (be[i], 0, 0))],
        out_specs=pl.BlockSpec((MOE_BLOCK, d), lambda i, be, nu: (i, 0)),
    )
    return pl.pallas_call(
        _ffn_kernel,
        grid_spec=grid_spec,
        out_shape=jax.ShapeDtypeStruct((p_len, d), jnp.float32),
        compiler_params=_cparams("arbitrary"),
        name="moe_grouped_ffn",
    )(blk_exp, n_used, xg, w_gate, w_up, w_down)


def _hier_moe(x, p):
    n, d = x.shape
    xb = _bf(x)
    w_r = jnp.concatenate([p['w_rg'], p['w_re']], axis=1)
    logits = _matmul(xb, _bf(w_r), tn=128)
    lg = logits[:, :N_GROUPS] + p['b_rg']
    p_grp = jax.nn.softmax(lg, axis=-1)
    grp = jnp.argmax(p_grp, axis=-1).astype(jnp.int32)
    p_top = jnp.max(p_grp, axis=-1)
    le = (logits[:, N_GROUPS:] + p['b_re']).reshape(n, N_GROUPS, EXP_PER_GROUP)
    le = jnp.take_along_axis(le, grp[:, None, None], axis=1)[:, 0]
    top_p, top_i = lax.top_k(jax.nn.softmax(le, axis=-1), TOP_K)
    weight = p_top[:, None] * top_p / jnp.sum(top_p, axis=-1, keepdims=True)
    expert = grp[:, None] * EXP_PER_GROUP + top_i.astype(jnp.int32)
    a = n * TOP_K
    e_flat = expert.reshape(a)
    onehot = (e_flat[:, None] == jnp.arange(N_EXPERTS, dtype=jnp.int32)[None, :]).astype(jnp.int32)
    rank = jnp.take_along_axis(jnp.cumsum(onehot, axis=0) - onehot, e_flat[:, None], axis=1)[:, 0]
    counts = jnp.sum(onehot, axis=0)
    padded = (counts + MOE_BLOCK - 1) // MOE_BLOCK * MOE_BLOCK
    pad_end = jnp.cumsum(padded)
    pad_off = pad_end - padded
    pos = pad_off[e_flat] + rank
    p_len = (a + N_EXPERTS * MOE_BLOCK + MOE_BLOCK - 1) // MOE_BLOCK * MOE_BLOCK
    n_blk = p_len // MOE_BLOCK
    tok_flat = jnp.arange(a, dtype=jnp.int32) // TOP_K
    buf_tok = jnp.zeros((p_len,), jnp.int32).at[pos].set(tok_flat)
    blk_start = jnp.arange(n_blk, dtype=jnp.int32) * MOE_BLOCK
    blk_exp = jnp.minimum(jnp.sum(pad_end[None, :] <= blk_start[:, None], axis=1), N_EXPERTS - 1).astype(jnp.int32)
    n_used = (pad_end[-1:] // MOE_BLOCK).astype(jnp.int32)
    xg = xb[buf_tok]
    y = _grouped_ffn(xg, blk_exp, n_used, _bf(p['w_gate']), _bf(p['w_up']), _bf(p['w_down']))
    yk = y[pos].reshape(n, TOP_K, d)
    return jnp.sum(yk * weight[:, :, None], axis=1)


def _final_norm_kernel(x_ref, w_ref, o_ref):
    x = x_ref[...]
    o_ref[...] = x * lax.rsqrt(jnp.mean(x * x, axis=-1, keepdims=True) + NORM_EPS) * w_ref[...]


def _final_norm(x, w, tm=512):
    m, d = x.shape
    return pl.pallas_call(
        _final_norm_kernel,
        grid=(m // tm,),
        in_specs=[pl.BlockSpec((tm, d), lambda i: (i, 0)), pl.BlockSpec((1, d), lambda i: (0, 0))],
        out_specs=pl.BlockSpec((tm, d), lambda i: (i, 0)),
        out_shape=jax.ShapeDtypeStruct((m, d), jnp.float32),
        compiler_params=_cparams("parallel"),
        name="final_rmsnorm",
    )(x, w.reshape(1, d))


def kernel(x, c, ctx, c_ctx, w_mod, b_mod, norm_mix_w, norm_ffn_w, w_in, b_in, hy_conv_w, hy_conv_b, hy_f_w1, hy_f_b1, hy_f_w2, hy_f_b2, hy_f_w3, hy_f_b3, hy_f_freq, hy_skip, hg_lb_raw, hg_norm_w, gl_w_a2, gl_b_a, gl_norm_w, w_branch, w_out, w_rg, b_rg, w_re, b_re, w_gate, w_up, w_down, final_norm_w):
    assert x.shape[0] == 1
    depth = w_mod.shape[0]
    lb_all = jnp.cumsum(jax.nn.softmax(hg_lb_raw, axis=0), axis=0)
    lb_all = lb_all - lb_all[:1]
    h, hc = x[0], ctx[0]
    cc = jnp.concatenate([c, c_ctx[None, :]], axis=0)
    for l in range(depth):
        need_ctx = l < depth - 1
        mod = _matmul(_bf(jax.nn.silu(cc)), _bf(w_mod[l])) + b_mod[l]
        sh1, sc1, gt1, sh2, sc2, gt2 = jnp.split(mod[0:1], 6, axis=-1)
        sh1c, sc1c, gt1c, sh2c, sc2c, gt2c = jnp.split(mod[1:2], 6, axis=-1)
        p = dict(w_in=w_in[l], b_in=b_in[l], hy_conv_w=hy_conv_w[l], hy_conv_b=hy_conv_b[l],
                 hy_f=(hy_f_w1[l], hy_f_b1[l], hy_f_w2[l], hy_f_b2[l], hy_f_w3[l], hy_f_b3[l], hy_f_freq[l]),
                 hy_skip=hy_skip[l], lb=lb_all[l], hg_norm_w=hg_norm_w[l], gl_w_a2=gl_w_a2[l], gl_b_a=gl_b_a[l],
                 gl_norm_w=gl_norm_w[l], w_branch=w_branch[l], w_out=w_out[l], w_rg=w_rg[l], b_rg=b_rg[l],
                 w_re=w_re[l], b_re=b_re[l], w_gate=w_gate[l], w_up=w_up[l], w_down=w_down[l])
        u = _modulate(h, norm_mix_w[l], sh1, sc1)
        uc = _modulate(hc, norm_mix_w[l], sh1c, sc1c)
        mix, mix_c = _mixer(u, uc, p, need_ctx)
        h = h + gt1 * mix
        h = h + gt2 * _hier_moe(_modulate(h, norm_ffn_w[l], sh2, sc2), p)
        if need_ctx:
            hc = hc + gt1c * mix_c
            hc = hc + gt2c * _hier_moe(_modulate(hc, norm_ffn_w[l], sh2c, sc2c), p)
    return _final_norm(h, final_norm_w)[None]
```

```python
import functools
import math

import jax
import jax.numpy as jnp
import numpy as np
from jax import lax
from jax.experimental import pallas as pl
from jax.experimental.pallas import tpu as pltpu

D_MODEL = 2048
NORM_EPS = 1e-6

HY_C = D_MODEL // 2
HY_EMB = 33
HY_BANDS = (HY_EMB - 1) // 2
HY_DECAY_TARGET = 1e-2
HY_FAST_PCT = 0.3
HY_SLOW_PCT = 1.5
HY_MOD_SHIFT = 0.05

HG_HEADS = 8
HG_DK = 128
HG_DV = 128
HG_K = HG_HEADS * HG_DK
HG_V = HG_HEADS * HG_DV

GL_HEADS = 4
GL_DK = 128
GL_DV = 256
GL_K = GL_HEADS * GL_DK
GL_V = GL_HEADS * GL_DV
GL_RANK = 16
GL_TAU = 16.0

N_BRANCH = 3
HG_COLS = 3 * HG_K + 2 * HG_V
GL_COLS = 2 * GL_K + 2 * GL_V + 2 * GL_RANK
REC_COLS = HG_COLS + GL_COLS
HY_COLS = 3 * HY_C
MERGE_COLS = N_BRANCH * D_MODEL

COL_TILE = 512
GL_Q_OFF = HG_COLS
GL_V_OFF = GL_Q_OFF + 2 * GL_K
GL_A_OFF = GL_V_OFF + 2 * GL_V
HY_OFF = -(-(GL_A_OFF + 2 * GL_RANK) // COL_TILE) * COL_TILE
MG_OFF = HY_OFF + HY_COLS
Z_COLS = MG_OFF + MERGE_COLS
assert GL_Q_OFF % GL_K == 0 and GL_V_OFF % GL_V == 0 and GL_A_OFF % 128 == 0 and Z_COLS % COL_TILE == 0

N_GROUPS = 4
EXP_PER_GROUP = 8
N_EXPERTS = N_GROUPS * EXP_PER_GROUP
TOP_K = 2
D_FF = D_MODEL // 4
MOE_BLOCK = 256

SCAN_CHUNK = 128

VMEM_LIMIT_BYTES = 56 * 1024 * 1024


def _cparams(*sem):
    return pltpu.CompilerParams(dimension_semantics=sem, vmem_limit_bytes=VMEM_LIMIT_BYTES)


def _bf(a):
    return a.astype(jnp.bfloat16)


def _mm_kernel(x_ref, w_ref, b_ref, o_ref):
    o_ref[...] = jnp.dot(x_ref[...], w_ref[...], preferred_element_type=jnp.float32) + b_ref[...]


def _matmul(x, w, bias=None, tm=512, tn=COL_TILE):
    m, k = x.shape
    n = w.shape[1]
    tm = min(tm, -(-m // 8) * 8)
    mp = -(-m // tm) * tm
    np_ = -(-n // tn) * tn
    if bias is None:
        bias = jnp.zeros((n,), jnp.float32)
    if mp != m:
        x = jnp.pad(x, ((0, mp - m), (0, 0)))
    if np_ != n:
        w = jnp.pad(w, ((0, 0), (0, np_ - n)))
        bias = jnp.pad(bias, (0, np_ - n))
    out = pl.pallas_call(
        _mm_kernel,
        grid=(mp // tm, np_ // tn),
        in_specs=[pl.BlockSpec((tm, k), lambda i, j: (i, 0)),
                  pl.BlockSpec((k, tn), lambda i, j: (0, j)),
                  pl.BlockSpec((1, tn), lambda i, j: (0, j))],
        out_specs=pl.BlockSpec((tm, tn), lambda i, j: (i, j)),
        out_shape=jax.ShapeDtypeStruct((mp, np_), jnp.float32),
        compiler_params=_cparams("parallel", "arbitrary"),
        name="dense_matmul",
    )(x, w, bias.reshape(1, np_))
    if mp != m or np_ != n:
        out = out[:m, :n]
    return out


def _rms(x, w):
    return x * lax.rsqrt(jnp.mean(x * x, axis=-1, keepdims=True) + NORM_EPS) * w


def _modulate(x, w, shift, scale):
    return _rms(x, w) * (1.0 + scale) + shift


def _scan_masks(c, reverse):
    t = np.arange(c)
    ms = [np.eye(c, dtype=np.float32)]
    for lvl in range(int(math.log2(c))):
        upper = ((t >> lvl) & 1).astype(bool)
        same = (t[:, None] >> (lvl + 1)) == (t[None, :] >> (lvl + 1))
        m = same & upper[:, None] & (~upper)[None, :]
        ms.append((m.T if reverse else m).astype(np.float32))
    tri = t[None, :] >= t[:, None] if reverse else t[None, :] <= t[:, None]
    return jnp.asarray(np.stack(ms)), jnp.asarray(tri.astype(np.float32), dtype=jnp.bfloat16)


def _level_arg(cum, lvl, reverse):
    c = cum.shape[0]
    blk = 1 << lvl
    if blk >= 8:
        pieces = []
        for gs in range(0, c, 2 * blk):
            ref = cum[gs + blk:gs + blk + 1, :]
            pieces.append(ref - cum[gs:gs + blk, :])
            pieces.append(cum[gs + blk:gs + 2 * blk, :] - ref)
        arg = jnp.concatenate(pieces, axis=0)
    else:
        c3 = cum.reshape(c // 8, 8, cum.shape[1])
        sub = lax.broadcasted_iota(jnp.int32, c3.shape, 1)
        ref_row = ((sub >> lvl) | 1) << lvl
        ref = None
        for r in range(blk, 8, 2 * blk):
            cand = jnp.broadcast_to(c3[:, r:r + 1, :], c3.shape)
            ref = cand if ref is None else jnp.where(ref_row == r, cand, ref)
        upper = ((sub >> lvl) & 1) == 1
        arg = jnp.where(upper, c3 - ref, ref - c3).reshape(cum.shape)
    return -arg if reverse else arg


def _dot_nt(a, b):
    return lax.dot_general(a, b, (((1,), (1,)), ((), ())), preferred_element_type=jnp.float32)


def _dot_tn(a, b):
    return lax.dot_general(a, b, (((0,), (0,)), ((), ())), preferred_element_type=jnp.float32)


def _stable_log_sigmoid(z):
    return jnp.minimum(z, 0.0) - jnp.log1p(jnp.exp(-jnp.abs(z)))


def _scan_kernel(*refs, mode, reverse, final, heads, dk, dv, c):
    it = iter(refs)
    q_ref = next(it)
    k_ref = next(it)
    v_ref = next(it)
    if mode == "hg":
        lbp_ref = next(it)
    else:
        a_ref = next(it)
        wa_ref = next(it)
        ba_ref = next(it)
    s0_ref = next(it)
    masks_ref = next(it)
    tri_ref = next(it)
    if final:
        oprev_ref = next(it)
        gate_ref = next(it)
        nw_ref = next(it)
    o_ref = next(it)
    st_ref = next(it)

    @pl.when(pl.program_id(0) == 0)
    def _():
        st_ref[...] = s0_ref[...]

    if mode == "gl":
        la_all = jnp.dot(a_ref[...].astype(jnp.bfloat16), wa_ref[...],
                         preferred_element_type=jnp.float32) + ba_ref[...]
    tri = tri_ref[...]
    tot_row = 0 if reverse else c - 1
    n_lvl = int(math.log2(c))
    for h in range(heads):
        ks = slice(h * dk, (h + 1) * dk)
        vs = slice(h * dv, (h + 1) * dv)
        q = q_ref[:, ks]
        v = v_ref[:, vs].astype(jnp.bfloat16)
        if mode == "hg":
            zf = k_ref[:, ks]
            la = lbp_ref[0:1, ks]
            lbb = lbp_ref[1:2, ks] + _stable_log_sigmoid(zf)
            g = jnp.maximum(la, lbb) + jnp.log1p(jnp.exp(-jnp.abs(la - lbb)))
            k = lbp_ref[2:3, ks] * jax.nn.sigmoid(-zf)
            q = q * jax.nn.sigmoid(q)
        else:
            g = _stable_log_sigmoid(la_all[:, ks]) * (1.0 / GL_TAU)
            k = k_ref[:, ks]
            q = q * (dk ** -0.5)
        g1 = g.astype(jnp.bfloat16)
        r1 = g - g1.astype(jnp.float32)
        g2 = r1.astype(jnp.bfloat16)
        g3 = (r1 - g2.astype(jnp.float32)).astype(jnp.bfloat16)
        cum = (jnp.dot(tri, g1, preferred_element_type=jnp.float32)
               + jnp.dot(tri, g2, preferred_element_type=jnp.float32)
               + jnp.dot(tri, g3, preferred_element_type=jnp.float32))
        tot = cum[tot_row:tot_row + 1, :]
        st = st_ref[h]
        o = _dot_nt((q * jnp.exp(cum)).astype(jnp.bfloat16), st.astype(jnp.bfloat16))
        kt = (k * jnp.exp(tot - cum)).astype(jnp.bfloat16)
        st_ref[h] = st * jnp.exp(tot) + _dot_tn(v, kt)
        scores = masks_ref[0] * _dot_nt(q.astype(jnp.bfloat16), k.astype(jnp.bfloat16))
        for lvl in range(n_lvl):
            e = jnp.exp(_level_arg(cum, lvl, reverse))
            scores = scores + masks_ref[1 + lvl] * _dot_nt((q * e).astype(jnp.bfloat16),
                                                           (k * e).astype(jnp.bfloat16))
        o = o + jnp.dot(scores.astype(jnp.bfloat16), v, preferred_element_type=jnp.float32)
        if final:
            o = o + oprev_ref[:, vs]
            y = o * lax.rsqrt(jnp.mean(o * o, axis=-1, keepdims=True) + NORM_EPS) * nw_ref[...]
            gt = gate_ref[:, vs]
            act = jax.nn.sigmoid(gt) if mode == "hg" else gt * jax.nn.sigmoid(gt)
            o_ref[:, vs] = (y * act).astype(o_ref.dtype)
        else:
            o_ref[:, vs] = o


def _scan_pass(mode, reverse, final, L, srcs, s0, params, final_srcs=(), norm_w=None):
    heads, dk, dv = (HG_HEADS, HG_DK, HG_DV) if mode == "hg" else (GL_HEADS, GL_DK, GL_DV)
    c = min(SCAN_CHUNK, L)
    nb = L // c
    row = (lambda i: nb - 1 - i) if reverse else (lambda i: i)
    masks, tri = _scan_masks(c, reverse)

    def const(shape):
        return pl.BlockSpec(shape, lambda i: (0,) * len(shape))

    def rowblock(width, cb):
        return pl.BlockSpec((c, width), lambda i: (row(i), cb))

    args = [a for a, _, _ in srcs] + list(params) + [s0, masks, tri]
    specs = ([rowblock(w, cb) for _, w, cb in srcs] + [const(p.shape) for p in params]
             + [const(s0.shape), const(masks.shape), const(tri.shape)])
    if final:
        args += [a for a, _, _ in final_srcs] + [norm_w]
        specs += [rowblock(w, cb) for _, w, cb in final_srcs] + [const(norm_w.shape)]
    return pl.pallas_call(
        functools.partial(_scan_kernel, mode=mode, reverse=reverse, final=final, heads=heads, dk=dk, dv=dv, c=c),
        grid=(nb,),
        in_specs=specs,
        out_specs=[pl.BlockSpec((c, heads * dv), lambda i: (row(i), 0)), const((heads, dv, dk))],
        out_shape=[jax.ShapeDtypeStruct((L, heads * dv), jnp.bfloat16 if final else jnp.float32),
                   jax.ShapeDtypeStruct((heads, dv, dk), jnp.float32)],
        compiler_params=_cparams("arbitrary"),
        name=f"scan_{mode}_{'bwd' if reverse else 'fwd'}",
    )(*args)


def _hgrn2(z, L, lb, norm_w, s0_f, s0_b):
    lbp = lambda d: jnp.stack([jnp.log(lb[d]), jnp.log1p(-lb[d]), 1.0 - lb[d]])
    w = HG_K
    o_b, s_b = _scan_pass("hg", True, False, L, [(z, w, 0), (z, w, 2), (z, w, 3)], s0_b, [lbp(1)])
    y, s_f = _scan_pass("hg", False, True, L, [(z, w, 0), (z, w, 1), (z, w, 3)], s0_f, [lbp(0)],
                        final_srcs=[(o_b, HG_V, 0), (z, HG_V, 4)], norm_w=norm_w.reshape(1, HG_DV))
    return y, s_f, s_b


def _gla(z, L, w_a2, b_a, norm_w, s0_f, s0_b):
    def gate_params(d):
        wa = jnp.zeros((128, GL_K), jnp.float32).at[d * GL_RANK:(d + 1) * GL_RANK].set(w_a2[d])
        return [_bf(wa), b_a[d].reshape(1, GL_K)]

    srcs = [(z, GL_K, GL_Q_OFF // GL_K), (z, GL_K, GL_Q_OFF // GL_K + 1), (z, GL_V, GL_V_OFF // GL_V),
            (z, 128, GL_A_OFF // 128)]
    o_b, s_b = _scan_pass("gl", True, False, L, srcs, s0_b, gate_params(1))
    y, s_f = _scan_pass("gl", False, True, L, srcs, s0_f, gate_params(0),
                        final_srcs=[(o_b, GL_V, 0), (z, GL_V, GL_V_OFF // GL_V + 1)], norm_w=norm_w.reshape(1, GL_DV))
    return y, s_f, s_b


def _hyena_filters(L, w1, b1, w2, b2, w3, b3, freq):
    t = jnp.linspace(0.0, 1.0, L, dtype=jnp.float32)[:, None]
    ang = 2.0 * math.pi * jnp.arange(L, dtype=jnp.float32)[:, None] / L
    bands = jnp.linspace(1e-4, HY_BANDS - 1, HY_BANDS, dtype=jnp.float32)[None, :]
    feats = jnp.concatenate([t, jnp.cos(bands * ang), -jnp.sin(bands * ang)], axis=-1)
    h = jnp.sin(freq * (feats @ w1 + b1))
    h = jnp.sin(freq * (h @ w2 + b2))
    h = h @ w3 + b3
    deltas = jnp.abs(jnp.linspace(math.log(HY_DECAY_TARGET) / HY_SLOW_PCT, math.log(HY_DECAY_TARGET) / HY_FAST_PCT,
                                  HY_C, dtype=jnp.float32))
    decay = jnp.exp(-t * deltas)
    h = h * (jnp.tile(decay, (1, 2)) + HY_MOD_SHIFT)
    h = h / jnp.sum(jnp.abs(h), axis=0, keepdims=True)
    return h[:, :HY_C], h[:, HY_C:]


def _two_sided_conv(v, h_f, h_b):
    L = v.shape[0]
    n = 2 * L
    kern = jnp.zeros((n, v.shape[1]), jnp.float32)
    kern = kern.at[:L].set(h_f)
    kern = kern.at[0].add(h_b[0])
    kern = kern.at[L + 1:].set(jnp.flip(h_b[1:], axis=0))
    y = jnp.fft.irfft(jnp.fft.rfft(v, n=n, axis=0) * jnp.fft.rfft(kern, axis=0), n=n, axis=0)
    return y[:L]


def _hyena(z_hy, conv_w, conv_b, fparams, skip):
    L = z_hy.shape[0]
    zp = jnp.pad(z_hy, ((1, 1), (0, 0)))
    zc = conv_w[0] * zp[0:L] + conv_w[1] * zp[1:L + 1] + conv_w[2] * zp[2:L + 2] + conv_b
    x0, x1, v = zc[:, :HY_C], zc[:, HY_C:2 * HY_C], zc[:, 2 * HY_C:]
    h_f, h_b = _hyena_filters(L, *fparams)
    v = v * x1
    return (_two_sided_conv(v, h_f, h_b) + v * skip) * x0


def _merge(z_gate, ys, w_branch, w_out):
    merged = 0.0
    for br in range(N_BRANCH):
        gate = jax.nn.sigmoid(z_gate[:, br * D_MODEL:(br + 1) * D_MODEL])
        merged = merged + gate * _matmul(_bf(ys[br]), _bf(w_branch[br]))
    return _matmul(_bf(merged), _bf(w_out))


def _pad_cols(a):
    pad = lambda n: jnp.zeros(a.shape[:-1] + (n,), a.dtype)
    return jnp.concatenate([a[..., :REC_COLS], pad(HY_OFF - REC_COLS), a[..., REC_COLS:]], axis=-1)


def _mixer(u, uc, p, need_ctx):
    L, Lc = u.shape[0], uc.shape[0]
    w_in = _pad_cols(_bf(p['w_in']))
    b_in = _pad_cols(p['b_in'])
    z = _matmul(_bf(u), w_in, b_in)
    ncol = Z_COLS if need_ctx else HY_OFF
    zc = _matmul(_bf(uc), w_in[:, :ncol], b_in[:ncol])
    zeros = lambda h, dk, dv: jnp.zeros((h, dv, dk), jnp.float32)
    yc_hg, hg_sf, hg_sb = _hgrn2(zc, Lc, p['lb'], p['hg_norm_w'],
                                 zeros(HG_HEADS, HG_DK, HG_DV), zeros(HG_HEADS, HG_DK, HG_DV))
    yc_gl, gl_sf, gl_sb = _gla(zc, Lc, p['gl_w_a2'], p['gl_b_a'], p['gl_norm_w'],
                               zeros(GL_HEADS, GL_DK, GL_DV), zeros(GL_HEADS, GL_DK, GL_DV))
    y_hg, _, _ = _hgrn2(z, L, p['lb'], p['hg_norm_w'], hg_sf, hg_sb)
    y_gl, _, _ = _gla(z, L, p['gl_w_a2'], p['gl_b_a'], p['gl_norm_w'], gl_sf, gl_sb)
    hy = (p['hy_conv_w'], p['hy_conv_b'], p['hy_f'], p['hy_skip'])
    y_hy = _hyena(z[:, HY_OFF:MG_OFF], *hy)
    out = _merge(z[:, MG_OFF:], (y_hy, y_hg, y_gl), p['w_branch'], p['w_out'])
    if not need_ctx:
        return out, None
    yc_hy = _hyena(zc[:, HY_OFF:MG_OFF], *hy)
    out_c = _merge(zc[:, MG_OFF:], (yc_hy, yc_hg, yc_gl), p['w_branch'], p['w_out'])
    return out, out_c


def _ffn_kernel(blk_exp_ref, n_used_ref, x_ref, wg_ref, wu_ref, wd_ref, o_ref):
    i = pl.program_id(0)

    @pl.when(i < n_used_ref[0])
    def _():
        x = x_ref[...]
        hg = jnp.dot(x, wg_ref[0], preferred_element_type=jnp.float32)
        hu = jnp.dot(x, wu_ref[0], preferred_element_type=jnp.float32)
        act = (hg * jax.nn.sigmoid(hg) * hu).astype(jnp.bfloat16)
        o_ref[...] = jnp.dot(act, wd_ref[0], preferred_element_type=jnp.float32)

    @pl.when(i >= n_used_ref[0])
    def _():
        o_ref[...] = jnp.zeros_like(o_ref)


def _grouped_ffn(xg, blk_exp, n_used, w_gate, w_up, w_down):
    p_len, d = xg.shape
    n_blk = p_len // MOE_BLOCK
    grid_spec = pltpu.PrefetchScalarGridSpec(
        num_scalar_prefetch=2,
        grid=(n_blk,),
        in_specs=[pl.BlockSpec((MOE_BLOCK, d), lambda i, be, nu: (i, 0)),
                  pl.BlockSpec((1, d, D_FF), lambda i, be, nu: (be[i], 0, 0)),
                  pl.BlockSpec((1, d, D_FF), lambda i, be, nu: (be[i], 0, 0)),
                  pl.BlockSpec((1, D_FF, d), lambda i, be, nu: (be[i], 0, 0))],
        out_specs=pl.BlockSpec((MOE_BLOCK, d), lambda i, be, nu: (i, 0)),
    )
    return pl.pallas_call(
        _ffn_kernel,
        grid_spec=grid_spec,
        out_shape=jax.ShapeDtypeStruct((p_len, d), jnp.float32),
        compiler_params=_cparams("arbitrary"),
        name="moe_grouped_ffn",
    )(blk_exp, n_used, xg, w_gate, w_up, w_down)


def _hier_moe(x, p):
    n, d = x.shape
    xb = _bf(x)
    w_r = jnp.concatenate([p['w_rg'], p['w_re']], axis=1)
    logits = _matmul(xb, _bf(w_r), tn=128)
    lg = logits[:, :N_GROUPS] + p['b_rg']
    p_grp = jax.nn.softmax(lg, axis=-1)
    grp = jnp.argmax(p_grp, axis=-1).astype(jnp.int32)
    p_top = jnp.max(p_grp, axis=-1)
    le = (logits[:, N_GROUPS:] + p['b_re']).reshape(n, N_GROUPS, EXP_PER_GROUP)
    le = jnp.take_along_axis(le, grp[:, None, None], axis=1)[:, 0]
    top_p, top_i = lax.top_k(jax.nn.softmax(le, axis=-1), TOP_K)
    weight = p_top[:, None] * top_p / jnp.sum(top_p, axis=-1, keepdims=True)
    expert = grp[:, None] * EXP_PER_GROUP + top_i.astype(jnp.int32)
    a = n * TOP_K
    e_flat = expert.reshape(a)
    onehot = (e_flat[:, None] == jnp.arange(N_EXPERTS, dtype=jnp.int32)[None, :]).astype(jnp.int32)
    rank = jnp.take_along_axis(jnp.cumsum(onehot, axis=0) - onehot, e_flat[:, None], axis=1)[:, 0]
    counts = jnp.sum(onehot, axis=0)
    padded = (counts + MOE_BLOCK - 1) // MOE_BLOCK * MOE_BLOCK
    pad_end = jnp.cumsum(padded)
    pad_off = pad_end - padded
    pos = pad_off[e_flat] + rank
    p_len = (a + N_EXPERTS * MOE_BLOCK + MOE_BLOCK - 1) // MOE_BLOCK * MOE_BLOCK
    n_blk = p_len // MOE_BLOCK
    tok_flat = jnp.arange(a, dtype=jnp.int32) // TOP_K
    buf_tok = jnp.zeros((p_len,), jnp.int32).at[pos].set(tok_flat)
    blk_start = jnp.arange(n_blk, dtype=jnp.int32) * MOE_BLOCK
    blk_exp = jnp.minimum(jnp.sum(pad_end[None, :] <= blk_start[:, None], axis=1), N_EXPERTS - 1).astype(jnp.int32)
    n_used = (pad_end[-1:] // MOE_BLOCK).astype(jnp.int32)
    xg = xb[buf_tok]
    y = _grouped_ffn(xg, blk_exp, n_used, _bf(p['w_gate']), _bf(p['w_up']), _bf(p['w_down']))
    yk = y[pos].reshape(n, TOP_K, d)
    return jnp.sum(yk * weight[:, :, None], axis=1)


def _final_norm_kernel(x_ref, w_ref, o_ref):
    x = x_ref[...]
    o_ref[...] = x * lax.rsqrt(jnp.mean(x * x, axis=-1, keepdims=True) + NORM_EPS) * w_ref[...]


def _final_norm(x, w, tm=512):
    m, d = x.shape
    return pl.pallas_call(
        _final_norm_kernel,
        grid=(m // tm,),
        in_specs=[pl.BlockSpec((tm, d), lambda i: (i, 0)), pl.BlockSpec((1, d), lambda i: (0, 0))],
        out_specs=pl.BlockSpec((tm, d), lambda i: (i, 0)),
        out_shape=jax.ShapeDtypeStruct((m, d), jnp.float32),
        compiler_params=_cparams("parallel"),
        name="final_rmsnorm",
    )(x, w.reshape(1, d))


def kernel(x, c, ctx, c_ctx, w_mod, b_mod, norm_mix_w, norm_ffn_w, w_in, b_in, hy_conv_w, hy_conv_b, hy_f_w1, hy_f_b1, hy_f_w2, hy_f_b2, hy_f_w3, hy_f_b3, hy_f_freq, hy_skip, hg_lb_raw, hg_norm_w, gl_w_a2, gl_b_a, gl_norm_w, w_branch, w_out, w_rg, b_rg, w_re, b_re, w_gate, w_up, w_down, final_norm_w):
    assert x.shape[0] == 1
    depth = w_mod.shape[0]
    lb_all = jnp.cumsum(jax.nn.softmax(hg_lb_raw, axis=0), axis=0)
    lb_all = lb_all - lb_all[:1]
    h, hc = x[0], ctx[0]
    cc = jnp.concatenate([c, c_ctx[None, :]], axis=0)
    for l in range(depth):
        need_ctx = l < depth - 1
        mod = _matmul(_bf(jax.nn.silu(cc)), _bf(w_mod[l]), b_mod[l])
        sh1, sc1, gt1, sh2, sc2, gt2 = jnp.split(mod[0:1], 6, axis=-1)
        sh1c, sc1c, gt1c, sh2c, sc2c, gt2c = jnp.split(mod[1:2], 6, axis=-1)
        p = dict(w_in=w_in[l], b_in=b_in[l], hy_conv_w=hy_conv_w[l], hy_conv_b=hy_conv_b[l],
                 hy_f=(hy_f_w1[l], hy_f_b1[l], hy_f_w2[l], hy_f_b2[l], hy_f_w3[l], hy_f_b3[l], hy_f_freq[l]),
                 hy_skip=hy_skip[l], lb=lb_all[l], hg_norm_w=hg_norm_w[l], gl_w_a2=gl_w_a2[l], gl_b_a=gl_b_a[l],
                 gl_norm_w=gl_norm_w[l], w_branch=w_branch[l], w_out=w_out[l], w_rg=w_rg[l], b_rg=b_rg[l],
                 w_re=w_re[l], b_re=b_re[l], w_gate=w_gate[l], w_up=w_up[l], w_down=w_down[l])
        u = _modulate(h, norm_mix_w[l], sh1, sc1)
        uc = _modulate(hc, norm_mix_w[l], sh1c, sc1c)
        mix, mix_c = _mixer(u, uc, p, need_ctx)
        h = h + gt1 * mix
        h = h + gt2 * _hier_moe(_modulate(h, norm_ffn_w[l], sh2, sc2), p)
        if need_ctx:
            hc = hc + gt1c * mix_c
            hc = hc + gt2c * _hier_moe(_modulate(hc, norm_ffn_w[l], sh2c, sc2c), p)
    return _final_norm(h, final_norm_w)[None]
```

```python
import functools
import math

import jax
import jax.numpy as jnp
import numpy as np
from jax import lax
from jax.experimental import pallas as pl
from jax.experimental.pallas import tpu as pltpu

D_MODEL = 2048
NORM_EPS = 1e-6

HY_C = D_MODEL // 2
HY_EMB = 33
HY_BANDS = (HY_EMB - 1) // 2
HY_DECAY_TARGET = 1e-2
HY_FAST_PCT = 0.3
HY_SLOW_PCT = 1.5
HY_MOD_SHIFT = 0.05

HG_HEADS = 8
HG_DK = 128
HG_DV = 128
HG_K = HG_HEADS * HG_DK
HG_V = HG_HEADS * HG_DV

GL_HEADS = 4
GL_DK = 128
GL_DV = 256
GL_K = GL_HEADS * GL_DK
GL_V = GL_HEADS * GL_DV
GL_RANK = 16
GL_TAU = 16.0

N_BRANCH = 3
HG_COLS = 3 * HG_K + 2 * HG_V
GL_COLS = 2 * GL_K + 2 * GL_V + 2 * GL_RANK
REC_COLS = HG_COLS + GL_COLS
HY_COLS = 3 * HY_C
MERGE_COLS = N_BRANCH * D_MODEL

COL_TILE = 512
GL_Q_OFF = HG_COLS
GL_V_OFF = GL_Q_OFF + 2 * GL_K
GL_A_OFF = GL_V_OFF + 2 * GL_V
HY_OFF = -(-(GL_A_OFF + 2 * GL_RANK) // HY_C) * HY_C
MG_OFF = HY_OFF + HY_COLS
Z_COLS = MG_OFF + MERGE_COLS
assert GL_Q_OFF % GL_K == 0 and GL_V_OFF % GL_V == 0 and GL_A_OFF % 128 == 0 and Z_COLS % COL_TILE == 0
assert MG_OFF % D_MODEL == 0

N_GROUPS = 4
EXP_PER_GROUP = 8
N_EXPERTS = N_GROUPS * EXP_PER_GROUP
TOP_K = 2
D_FF = D_MODEL // 4
MOE_BLOCK = 256

SCAN_CHUNK = 128

VMEM_LIMIT_BYTES = 56 * 1024 * 1024


def _cparams(*sem):
    return pltpu.CompilerParams(dimension_semantics=sem, vmem_limit_bytes=VMEM_LIMIT_BYTES)


def _bf(a):
    return a.astype(jnp.bfloat16)


def _mm_kernel(x_ref, w_ref, b_ref, o_ref):
    acc = jnp.dot(x_ref[...], w_ref[...], preferred_element_type=jnp.float32) + b_ref[...]
    o_ref[...] = acc.astype(o_ref.dtype)


def _matmul(x, w, bias=None, tm=512, tn=COL_TILE, out_dtype=jnp.float32):
    m, k = x.shape
    n = w.shape[1]
    tm = min(tm, -(-m // 8) * 8)
    mp = -(-m // tm) * tm
    np_ = -(-n // tn) * tn
    if bias is None:
        bias = jnp.zeros((n,), jnp.float32)
    if mp != m:
        x = jnp.pad(x, ((0, mp - m), (0, 0)))
    if np_ != n:
        w = jnp.pad(w, ((0, 0), (0, np_ - n)))
        bias = jnp.pad(bias, (0, np_ - n))
    out = pl.pallas_call(
        _mm_kernel,
        grid=(mp // tm, np_ // tn),
        in_specs=[pl.BlockSpec((tm, k), lambda i, j: (i, 0)),
                  pl.BlockSpec((k, tn), lambda i, j: (0, j)),
                  pl.BlockSpec((1, tn), lambda i, j: (0, j))],
        out_specs=pl.BlockSpec((tm, tn), lambda i, j: (i, j)),
        out_shape=jax.ShapeDtypeStruct((mp, np_), out_dtype),
        compiler_params=_cparams("parallel", "arbitrary"),
        name="dense_matmul",
    )(x, w, bias.reshape(1, np_))
    if mp != m or np_ != n:
        out = out[:m, :n]
    return out


def _rms(x, w):
    return x * lax.rsqrt(jnp.mean(x * x, axis=-1, keepdims=True) + NORM_EPS) * w


def _modulate(x, w, shift, scale):
    return _rms(x, w) * (1.0 + scale) + shift


def _scan_masks(c, reverse):
    t = np.arange(c)
    ms = [np.eye(c, dtype=np.float32)]
    for lvl in range(int(math.log2(c))):
        upper = ((t >> lvl) & 1).astype(bool)
        same = (t[:, None] >> (lvl + 1)) == (t[None, :] >> (lvl + 1))
        m = same & upper[:, None] & (~upper)[None, :]
        ms.append((m.T if reverse else m).astype(np.float32))
    tri = t[None, :] >= t[:, None] if reverse else t[None, :] <= t[:, None]
    return jnp.asarray(np.stack(ms)), jnp.asarray(tri.astype(np.float32), dtype=jnp.bfloat16)


def _level_arg(cum, lvl, reverse):
    c = cum.shape[0]
    blk = 1 << lvl
    if blk >= 8:
        pieces = []
        for gs in range(0, c, 2 * blk):
            ref = cum[gs + blk:gs + blk + 1, :]
            pieces.append(ref - cum[gs:gs + blk, :])
            pieces.append(cum[gs + blk:gs + 2 * blk, :] - ref)
        arg = jnp.concatenate(pieces, axis=0)
    else:
        c3 = cum.reshape(c // 8, 8, cum.shape[1])
        sub = lax.broadcasted_iota(jnp.int32, c3.shape, 1)
        ref_row = ((sub >> lvl) | 1) << lvl
        ref = None
        for r in range(blk, 8, 2 * blk):
            cand = jnp.broadcast_to(c3[:, r:r + 1, :], c3.shape)
            ref = cand if ref is None else jnp.where(ref_row == r, cand, ref)
        upper = ((sub >> lvl) & 1) == 1
        arg = jnp.where(upper, c3 - ref, ref - c3).reshape(cum.shape)
    return -arg if reverse else arg


def _dot_nt(a, b):
    return lax.dot_general(a, b, (((1,), (1,)), ((), ())), preferred_element_type=jnp.float32)


def _dot_tn(a, b):
    return lax.dot_general(a, b, (((0,), (0,)), ((), ())), preferred_element_type=jnp.float32)


def _stable_log_sigmoid(z):
    return jnp.minimum(z, 0.0) - jnp.log1p(jnp.exp(-jnp.abs(z)))


def _scan_kernel(*refs, mode, reverse, final, heads, dk, dv, c):
    it = iter(refs)
    q_ref = next(it)
    k_ref = next(it)
    v_ref = next(it)
    if mode == "hg":
        lbp_ref = next(it)
    else:
        a_ref = next(it)
        wa_ref = next(it)
        ba_ref = next(it)
    s0_ref = next(it)
    masks_ref = next(it)
    tri_ref = next(it)
    if final:
        oprev_ref = next(it)
        gate_ref = next(it)
        nw_ref = next(it)
    o_ref = next(it)
    st_ref = next(it)

    @pl.when(pl.program_id(0) == 0)
    def _():
        st_ref[...] = s0_ref[...]

    if mode == "gl":
        la_all = jnp.dot(a_ref[...].astype(jnp.bfloat16), wa_ref[...],
                         preferred_element_type=jnp.float32) + ba_ref[...]
    tri = tri_ref[...]
    tot_row = 0 if reverse else c - 1
    n_lvl = int(math.log2(c))
    for h in range(heads):
        ks = slice(h * dk, (h + 1) * dk)
        vs = slice(h * dv, (h + 1) * dv)
        q = q_ref[:, ks]
        v = v_ref[:, vs].astype(jnp.bfloat16)
        if mode == "hg":
            zf = k_ref[:, ks]
            la = lbp_ref[0:1, ks]
            lbb = lbp_ref[1:2, ks] + _stable_log_sigmoid(zf)
            g = jnp.maximum(la, lbb) + jnp.log1p(jnp.exp(-jnp.abs(la - lbb)))
            k = lbp_ref[2:3, ks] * jax.nn.sigmoid(-zf)
            q = q * jax.nn.sigmoid(q)
        else:
            g = _stable_log_sigmoid(la_all[:, ks]) * (1.0 / GL_TAU)
            k = k_ref[:, ks]
            q = q * (dk ** -0.5)
        g1 = g.astype(jnp.bfloat16)
        r1 = g - g1.astype(jnp.float32)
        g2 = r1.astype(jnp.bfloat16)
        g3 = (r1 - g2.astype(jnp.float32)).astype(jnp.bfloat16)
        cum = (jnp.dot(tri, g1, preferred_element_type=jnp.float32)
               + jnp.dot(tri, g2, preferred_element_type=jnp.float32)
               + jnp.dot(tri, g3, preferred_element_type=jnp.float32))
        tot = cum[tot_row:tot_row + 1, :]
        st = st_ref[h]
        o = _dot_nt((q * jnp.exp(cum)).astype(jnp.bfloat16), st.astype(jnp.bfloat16))
        kt = (k * jnp.exp(tot - cum)).astype(jnp.bfloat16)
        st_ref[h] = st * jnp.exp(tot) + _dot_tn(v, kt)
        scores = masks_ref[0] * _dot_nt(q.astype(jnp.bfloat16), k.astype(jnp.bfloat16))
        for lvl in range(n_lvl):
            e = jnp.exp(_level_arg(cum, lvl, reverse))
            scores = scores + masks_ref[1 + lvl] * _dot_nt((q * e).astype(jnp.bfloat16),
                                                           (k * e).astype(jnp.bfloat16))
        o = o + jnp.dot(scores.astype(jnp.bfloat16), v, preferred_element_type=jnp.float32)
        if final:
            o = o + oprev_ref[:, vs]
            y = o * lax.rsqrt(jnp.mean(o * o, axis=-1, keepdims=True) + NORM_EPS) * nw_ref[...]
            gt = gate_ref[:, vs]
            act = jax.nn.sigmoid(gt) if mode == "hg" else gt * jax.nn.sigmoid(gt)
            o_ref[:, vs] = (y * act).astype(o_ref.dtype)
        else:
            o_ref[:, vs] = o


def _scan_pass(mode, reverse, final, L, srcs, s0, params, final_srcs=(), norm_w=None):
    heads, dk, dv = (HG_HEADS, HG_DK, HG_DV) if mode == "hg" else (GL_HEADS, GL_DK, GL_DV)
    c = min(SCAN_CHUNK, L)
    nb = L // c
    row = (lambda i: nb - 1 - i) if reverse else (lambda i: i)
    masks, tri = _scan_masks(c, reverse)

    def const(shape):
        return pl.BlockSpec(shape, lambda i: (0,) * len(shape))

    def rowblock(width, cb):
        return pl.BlockSpec((c, width), lambda i: (row(i), cb))

    args = [a for a, _, _ in srcs] + list(params) + [s0, masks, tri]
    specs = ([rowblock(w, cb) for _, w, cb in srcs] + [const(p.shape) for p in params]
             + [const(s0.shape), const(masks.shape), const(tri.shape)])
    if final:
        args += [a for a, _, _ in final_srcs] + [norm_w]
        specs += [rowblock(w, cb) for _, w, cb in final_srcs] + [const(norm_w.shape)]
    return pl.pallas_call(
        functools.partial(_scan_kernel, mode=mode, reverse=reverse, final=final, heads=heads, dk=dk, dv=dv, c=c),
        grid=(nb,),
        in_specs=specs,
        out_specs=[pl.BlockSpec((c, heads * dv), lambda i: (row(i), 0)), const((heads, dv, dk))],
        out_shape=[jax.ShapeDtypeStruct((L, heads * dv), jnp.bfloat16 if final else jnp.float32),
                   jax.ShapeDtypeStruct((heads, dv, dk), jnp.float32)],
        compiler_params=_cparams("arbitrary"),
        name=f"scan_{mode}_{'bwd' if reverse else 'fwd'}",
    )(*args)


def _hgrn2(z, L, lb, norm_w, s0_f, s0_b):
    lbp = lambda d: jnp.stack([jnp.log(lb[d]), jnp.log1p(-lb[d]), 1.0 - lb[d]])
    w = HG_K
    o_b, s_b = _scan_pass("hg", True, False, L, [(z, w, 0), (z, w, 2), (z, w, 3)], s0_b, [lbp(1)])
    y, s_f = _scan_pass("hg", False, True, L, [(z, w, 0), (z, w, 1), (z, w, 3)], s0_f, [lbp(0)],
                        final_srcs=[(o_b, HG_V, 0), (z, HG_V, 4)], norm_w=norm_w.reshape(1, HG_DV))
    return y, s_f, s_b


def _gla(z, L, w_a2, b_a, norm_w, s0_f, s0_b):
    def gate_params(d):
        wa = jnp.zeros((128, GL_K), jnp.float32).at[d * GL_RANK:(d + 1) * GL_RANK].set(w_a2[d])
        return [_bf(wa), b_a[d].reshape(1, GL_K)]

    srcs = [(z, GL_K, GL_Q_OFF // GL_K), (z, GL_K, GL_Q_OFF // GL_K + 1), (z, GL_V, GL_V_OFF // GL_V),
            (z, 128, GL_A_OFF // 128)]
    o_b, s_b = _scan_pass("gl", True, False, L, srcs, s0_b, gate_params(1))
    y, s_f = _scan_pass("gl", False, True, L, srcs, s0_f, gate_params(0),
                        final_srcs=[(o_b, GL_V, 0), (z, GL_V, GL_V_OFF // GL_V + 1)], norm_w=norm_w.reshape(1, GL_DV))
    return y, s_f, s_b


def _hyena_filters(L, w1, b1, w2, b2, w3, b3, freq):
    t = jnp.linspace(0.0, 1.0, L, dtype=jnp.float32)[:, None]
    ang = 2.0 * math.pi * jnp.arange(L, dtype=jnp.float32)[:, None] / L
    bands = jnp.linspace(1e-4, HY_BANDS - 1, HY_BANDS, dtype=jnp.float32)[None, :]
    feats = jnp.concatenate([t, jnp.cos(bands * ang), -jnp.sin(bands * ang)], axis=-1)
    h = jnp.sin(freq * (feats @ w1 + b1))
    h = jnp.sin(freq * (h @ w2 + b2))
    h = h @ w3 + b3
    deltas = jnp.abs(jnp.linspace(math.log(HY_DECAY_TARGET) / HY_SLOW_PCT, math.log(HY_DECAY_TARGET) / HY_FAST_PCT,
                                  HY_C, dtype=jnp.float32))
    decay = jnp.exp(-t * deltas)
    h = h * (jnp.tile(decay, (1, 2)) + HY_MOD_SHIFT)
    inorm = 1.0 / jnp.sum(jnp.abs(h), axis=0)
    return h, inorm.reshape(2, HY_C)


def _hy_pre_kernel(x0_ref, x1_ref, v_ref, x0p_ref, x1p_ref, vp_ref, x0n_ref, x1n_ref, vn_ref, w_ref, b_ref,
                   vo_ref, x0o_ref):
    i = pl.program_id(0)
    first = i == 0
    last = i == pl.num_programs(0) - 1
    tm = x0_ref.shape[0]
    row = lax.broadcasted_iota(jnp.int32, x0_ref.shape, 0)

    def conv(x_ref, p_ref, n_ref, g):
        x = x_ref[...]
        cs = slice(g * HY_C, (g + 1) * HY_C)
        prev_row = jnp.where(first, 0.0, p_ref[7:8, :])
        next_row = jnp.where(last, 0.0, n_ref[0:1, :])
        xp = jnp.where(row == 0, prev_row, pltpu.roll(x, 1, 0))
        xn = jnp.where(row == tm - 1, next_row, pltpu.roll(x, tm - 1, 0))
        return w_ref[0:1, cs] * xp + w_ref[1:2, cs] * x + w_ref[2:3, cs] * xn + b_ref[0:1, cs]

    x0 = conv(x0_ref, x0p_ref, x0n_ref, 0)
    x1 = conv(x1_ref, x1p_ref, x1n_ref, 1)
    v = conv(v_ref, vp_ref, vn_ref, 2)
    vo_ref[...] = (v * x1).astype(vo_ref.dtype)
    x0o_ref[...] = x0.astype(x0o_ref.dtype)


def _hy_pre(z, L, conv_w, conv_b):
    tm = min(256, L)
    nb8 = L // 8
    cb = HY_OFF // HY_C
    main = lambda g: pl.BlockSpec((tm, HY_C), lambda i: (i, cb + g))
    prev = lambda g: pl.BlockSpec((8, HY_C), lambda i: (jnp.maximum(i * (tm // 8) - 1, 0), cb + g))
    nxt = lambda g: pl.BlockSpec((8, HY_C), lambda i: (jnp.minimum((i + 1) * (tm // 8), nb8 - 1), cb + g))
    const = lambda a: pl.BlockSpec(a.shape, lambda i: (0, 0))
    cbias = conv_b.reshape(1, HY_COLS)
    out = jax.ShapeDtypeStruct((L, HY_C), jnp.bfloat16)
    return pl.pallas_call(
        _hy_pre_kernel,
        grid=(L // tm,),
        in_specs=[main(0), main(1), main(2), prev(0), prev(1), prev(2), nxt(0), nxt(1), nxt(2),
                  const(conv_w), const(cbias)],
        out_specs=[pl.BlockSpec((tm, HY_C), lambda i: (i, 0))] * 2,
        out_shape=[out, out],
        compiler_params=_cparams("parallel"),
        name="hyena_short_conv",
    )(z, z, z, z, z, z, z, z, z, conv_w, cbias)


HY_N1 = 256
HY_TWO_STAGE_MIN_L = 1024


def _dft_outer_table(n1, cols):
    ang = -2.0 * np.pi * np.outer(np.arange(n1 // 2) + 0.5, np.arange(cols)) / n1
    return jnp.asarray(np.concatenate([np.cos(ang), np.sin(ang)], axis=0), jnp.bfloat16)


def _dft_inner_table(n1, n2):
    j2 = np.arange(n2)
    f_ang = -2.0 * np.pi * np.outer(np.arange(n2), j2) / n2
    tw_ang = -2.0 * np.pi * np.outer(np.arange(n1 // 2) + 0.5, j2) / (n1 * n2)
    fr, fi = jnp.asarray(np.cos(f_ang), jnp.float32), jnp.asarray(np.sin(f_ang), jnp.float32)
    twr, twi = jnp.asarray(np.cos(tw_ang), jnp.float32), jnp.asarray(np.sin(tw_ang), jnp.float32)
    mr = fr[None] * twr[:, None, :] - fi[None] * twi[:, None, :]
    mi = fr[None] * twi[:, None, :] + fi[None] * twr[:, None, :]
    return _bf(jnp.concatenate([jnp.concatenate([mr, -mi], axis=2), jnp.concatenate([mi, mr], axis=2)], axis=1))


def _spectral_product(xv, xh, inorm, half):
    inf, inb = inorm[0:1, :], inorm[1:2, :]
    gr = xh[:half, :HY_C] * inf + xh[:half, HY_C:] * inb
    gi = xh[half:, :HY_C] * inf - xh[half:, HY_C:] * inb
    xr, xi = xv[:half], xv[half:]
    return jnp.concatenate([xr * gr - xi * gi, xr * gi + xi * gr], axis=0).astype(jnp.bfloat16)


def _hy_spec_kernel(r_ref, avr_ref, avi_ref, ahr_ref, ahi_ref, inorm_ref, br_ref, bi_ref):
    r = r_ref[0]
    n2 = avr_ref.shape[1]
    xv = jnp.dot(r, jnp.concatenate([avr_ref[0], avi_ref[0]], axis=0), preferred_element_type=jnp.float32)
    xh = jnp.dot(r, jnp.concatenate([ahr_ref[0], ahi_ref[0]], axis=0), preferred_element_type=jnp.float32)
    b = _dot_tn(r, _spectral_product(xv, xh, inorm_ref[...], n2))
    br_ref[0] = b[:n2].astype(br_ref.dtype)
    bi_ref[0] = b[n2:].astype(bi_ref.dtype)


def _hy_spec(r, av, ah, inorm, n1, n2):
    av3 = av.reshape(n1, n2, HY_C)
    ah3 = ah.reshape(n1, n2, 2 * HY_C)
    h1 = n1 // 2
    out = jax.ShapeDtypeStruct((h1, n2, HY_C), jnp.bfloat16)
    return pl.pallas_call(
        _hy_spec_kernel,
        grid=(h1,),
        in_specs=[pl.BlockSpec((1, 2 * n2, 2 * n2), lambda k: (k, 0, 0)),
                  pl.BlockSpec((1, n2, HY_C), lambda k: (k, 0, 0)),
                  pl.BlockSpec((1, n2, HY_C), lambda k: (k + h1, 0, 0)),
                  pl.BlockSpec((1, n2, 2 * HY_C), lambda k: (k, 0, 0)),
                  pl.BlockSpec((1, n2, 2 * HY_C), lambda k: (k + h1, 0, 0)),
                  pl.BlockSpec((2, HY_C), lambda k: (0, 0))],
        out_specs=[pl.BlockSpec((1, n2, HY_C), lambda k: (k, 0, 0))] * 2,
        out_shape=[out, out],
        compiler_params=_cparams("parallel"),
        name="hyena_spectral",
    )(r, av3, av3, ah3, ah3, inorm)


def _hy_spec_direct_kernel(xv_ref, xh_ref, inorm_ref, yr_ref, yi_ref):
    half = yr_ref.shape[0]
    y = _spectral_product(xv_ref[...].astype(jnp.float32), xh_ref[...].astype(jnp.float32), inorm_ref[...], half)
    yr_ref[...] = y[:half]
    yi_ref[...] = y[half:]


def _hy_spec_direct(xv, xh, inorm, L):
    full = lambda a: pl.BlockSpec(a.shape, lambda i: (0, 0))
    out = jax.ShapeDtypeStruct((L, HY_C), jnp.bfloat16)
    return pl.pallas_call(
        _hy_spec_direct_kernel,
        grid=(1,),
        in_specs=[full(xv), full(xh), full(inorm)],
        out_specs=[pl.BlockSpec((L, HY_C), lambda i: (0, 0))] * 2,
        out_shape=[out, out],
        compiler_params=_cparams("arbitrary"),
        name="hyena_spectral_direct",
    )(xv, xh, inorm)


def _hy_post_kernel(tr_ref, ti_ref, br_ref, bi_ref, v_ref, x0_ref, skip_ref, o_ref, *, scale):
    acc = (jnp.dot(tr_ref[...], br_ref[...], preferred_element_type=jnp.float32)
           + jnp.dot(ti_ref[...], bi_ref[...], preferred_element_type=jnp.float32))
    y = (acc * scale + v_ref[...].astype(jnp.float32) * skip_ref[...]) * x0_ref[...].astype(jnp.float32)
    o_ref[...] = y.astype(o_ref.dtype)


def _hy_post(t_fwd, b_r, b_i, v, x0, skip, L, h1, n2):
    ncol = n2 * HY_C
    tn = min(4096, ncol)
    tr_t = t_fwd[:h1].T
    ti_t = t_fwd[h1:].T
    skip_t = jnp.tile(skip, tn // HY_C).reshape(1, tn)
    col = lambda rows: pl.BlockSpec((rows, tn), lambda j: (0, j))
    rows_out = tr_t.shape[0]
    y = pl.pallas_call(
        functools.partial(_hy_post_kernel, scale=1.0 / L),
        grid=(ncol // tn,),
        in_specs=[pl.BlockSpec(tr_t.shape, lambda j: (0, 0)), pl.BlockSpec(ti_t.shape, lambda j: (0, 0)),
                  col(h1), col(h1), col(rows_out), col(rows_out), pl.BlockSpec((1, tn), lambda j: (0, 0))],
        out_specs=col(rows_out),
        out_shape=jax.ShapeDtypeStruct((rows_out, ncol), jnp.bfloat16),
        compiler_params=_cparams("parallel"),
        name="hyena_inverse",
    )(tr_t, ti_t, b_r.reshape(h1, ncol), b_i.reshape(h1, ncol), v.reshape(rows_out, ncol),
      x0.reshape(rows_out, ncol), skip_t)
    return y.reshape(L, HY_C)


def _hyena(z, L, conv_w, conv_b, fparams, skip):
    v, x0 = _hy_pre(z, L, conv_w, conv_b)
    hfil, inorm = _hyena_filters(L, *fparams)
    if L >= HY_TWO_STAGE_MIN_L:
        n1 = HY_N1
        n2 = 2 * L // n1
        h1 = n1 // 2
        t_fwd = _dft_outer_table(n1, h1)
        av = _matmul(t_fwd, v.reshape(h1, n2 * HY_C), tm=n1, tn=4096, out_dtype=jnp.bfloat16)
        ah = _matmul(t_fwd, _bf(hfil).reshape(h1, n2 * 2 * HY_C), tm=n1, tn=4096, out_dtype=jnp.bfloat16)
        b_r, b_i = _hy_spec(_dft_inner_table(n1, n2), av, ah, inorm, n1, n2)
        return _hy_post(t_fwd, b_r, b_i, v, x0, skip, L, h1, n2)
    t_fwd = _dft_outer_table(2 * L, L)
    xv = _matmul(t_fwd, v, tm=2 * L, tn=HY_C, out_dtype=jnp.bfloat16)
    xh = _matmul(t_fwd, _bf(hfil), tm=2 * L, tn=HY_C, out_dtype=jnp.bfloat16)
    y_r, y_i = _hy_spec_direct(xv, xh, inorm, L)
    return _hy_post(t_fwd, y_r, y_i, v, x0, skip, L, L, 1)


def _merge_kernel(yh_ref, yg_ref, yl_ref, gate_h_ref, gate_g_ref, gate_l_ref, wb_ref, o_ref):
    acc = None
    for br, (y_ref, g_ref) in enumerate(((yh_ref, gate_h_ref), (yg_ref, gate_g_ref), (yl_ref, gate_l_ref))):
        t = jnp.dot(y_ref[...], wb_ref[br], preferred_element_type=jnp.float32) * jax.nn.sigmoid(g_ref[...])
        acc = t if acc is None else acc + t
    o_ref[...] = acc.astype(o_ref.dtype)


def _proj_residual_kernel(m_ref, w_ref, h_ref, gt_ref, o_ref):
    o_ref[...] = h_ref[...] + gt_ref[...] * jnp.dot(m_ref[...], w_ref[...], preferred_element_type=jnp.float32)


def _merge(z, L, ys, w_branch, w_out, h, gt):
    tm = min(256, L)
    gb = MG_OFF // D_MODEL
    ybs = pl.BlockSpec((tm, HY_C), lambda i: (i, 0))
    gate = lambda br: pl.BlockSpec((tm, D_MODEL), lambda i: (i, gb + br))
    row = pl.BlockSpec((tm, D_MODEL), lambda i: (i, 0))
    merged = pl.pallas_call(
        _merge_kernel,
        grid=(L // tm,),
        in_specs=[ybs, ybs, ybs, gate(0), gate(1), gate(2),
                  pl.BlockSpec((N_BRANCH, HY_C, D_MODEL), lambda i: (0, 0, 0))],
        out_specs=row,
        out_shape=jax.ShapeDtypeStruct((L, D_MODEL), jnp.bfloat16),
        compiler_params=_cparams("parallel"),
        name="branch_merge",
    )(ys[0], ys[1], ys[2], z, z, z, _bf(w_branch))
    return pl.pallas_call(
        _proj_residual_kernel,
        grid=(L // tm,),
        in_specs=[row, pl.BlockSpec((D_MODEL, D_MODEL), lambda i: (0, 0)), row,
                  pl.BlockSpec((1, D_MODEL), lambda i: (0, 0))],
        out_specs=row,
        out_shape=jax.ShapeDtypeStruct((L, D_MODEL), jnp.float32),
        compiler_params=_cparams("parallel"),
        name="out_proj_residual",
    )(merged, _bf(w_out), h, gt)


def _pad_cols(a):
    pad = lambda n: jnp.zeros(a.shape[:-1] + (n,), a.dtype)
    return jnp.concatenate([a[..., :REC_COLS], pad(HY_OFF - REC_COLS), a[..., REC_COLS:]], axis=-1)


def _mixer(h, hc, u, uc, gt, gtc, p, need_ctx):
    L, Lc = u.shape[0], uc.shape[0]
    w_in = _pad_cols(_bf(p['w_in']))
    b_in = _pad_cols(p['b_in'])
    z = _matmul(_bf(u), w_in, b_in)
    ncol = Z_COLS if need_ctx else HY_OFF
    zc = _matmul(_bf(uc), w_in[:, :ncol], b_in[:ncol])
    zeros = lambda hd, dk, dv: jnp.zeros((hd, dv, dk), jnp.float32)
    yc_hg, hg_sf, hg_sb = _hgrn2(zc, Lc, p['lb'], p['hg_norm_w'],
                                 zeros(HG_HEADS, HG_DK, HG_DV), zeros(HG_HEADS, HG_DK, HG_DV))
    yc_gl, gl_sf, gl_sb = _gla(zc, Lc, p['gl_w_a2'], p['gl_b_a'], p['gl_norm_w'],
                               zeros(GL_HEADS, GL_DK, GL_DV), zeros(GL_HEADS, GL_DK, GL_DV))
    y_hg, _, _ = _hgrn2(z, L, p['lb'], p['hg_norm_w'], hg_sf, hg_sb)
    y_gl, _, _ = _gla(z, L, p['gl_w_a2'], p['gl_b_a'], p['gl_norm_w'], gl_sf, gl_sb)
    hy = (p['hy_conv_w'], p['hy_conv_b'], p['hy_f'], p['hy_skip'])
    y_hy = _hyena(z, L, *hy)
    h = _merge(z, L, (y_hy, y_hg, y_gl), p['w_branch'], p['w_out'], h, gt)
    if need_ctx:
        yc_hy = _hyena(zc, Lc, *hy)
        hc = _merge(zc, Lc, (yc_hy, yc_hg, yc_gl), p['w_branch'], p['w_out'], hc, gtc)
    return h, hc


def _ffn_kernel(blk_exp_ref, n_used_ref, x_ref, wg_ref, wu_ref, wd_ref, o_ref):
    i = pl.program_id(0)

    @pl.when(i < n_used_ref[0])
    def _():
        x = x_ref[...]
        hg = jnp.dot(x, wg_ref[0], preferred_element_type=jnp.float32)
        hu = jnp.dot(x, wu_ref[0], preferred_element_type=jnp.float32)
        act = (hg * jax.nn.sigmoid(hg) * hu).astype(jnp.bfloat16)
        o_ref[...] = jnp.dot(act, wd_ref[0], preferred_element_type=jnp.float32)

    @pl.when(i >= n_used_ref[0])
    def _():
        o_ref[...] = jnp.zeros_like(o_ref)


def _grouped_ffn(xg, blk_exp, n_used, w_gate, w_up, w_down):
    p_len, d = xg.shape
    n_blk = p_len // MOE_BLOCK
    grid_spec = pltpu.PrefetchScalarGridSpec(
        num_scalar_prefetch=2,
        grid=(n_blk,),
        in_specs=[pl.BlockSpec((MOE_BLOCK, d), lambda i, be, nu: (i, 0)),
                  pl.BlockSpec((1, d, D_FF), lambda i, be, nu: (be[i], 0, 0)),
                  pl.BlockSpec((1, d, D_FF), lambda i, be, nu: (be[i], 0, 0)),
                  pl.BlockSpec((1, D_FF, d), lambda i, be, nu: (be[i], 0, 0))],
        out_specs=pl.BlockSpec((MOE_BLOCK, d), lambda i, be, nu: (i, 0)),
    )
    return pl.pallas_call(
        _ffn_kernel,
        grid_spec=grid_spec,
        out_shape=jax.ShapeDtypeStruct((p_len, d), jnp.float32),
        compiler_params=_cparams("arbitrary"),
        name="moe_grouped_ffn",
    )(blk_exp, n_used, xg, w_gate, w_up, w_down)


def _hier_moe(x, p):
    n, d = x.shape
    xb = _bf(x)
    w_r = jnp.concatenate([p['w_rg'], p['w_re']], axis=1)
    logits = _matmul(xb, _bf(w_r), tn=128)
    lg = logits[:, :N_GROUPS] + p['b_rg']
    p_grp = jax.nn.softmax(lg, axis=-1)
    grp = jnp.argmax(p_grp, axis=-1).astype(jnp.int32)
    p_top = jnp.max(p_grp, axis=-1)
    le = (logits[:, N_GROUPS:] + p['b_re']).reshape(n, N_GROUPS, EXP_PER_GROUP)
    le = jnp.take_along_axis(le, grp[:, None, None], axis=1)[:, 0]
    top_p, top_i = lax.top_k(jax.nn.softmax(le, axis=-1), TOP_K)
    weight = p_top[:, None] * top_p / jnp.sum(top_p, axis=-1, keepdims=True)
    expert = grp[:, None] * EXP_PER_GROUP + top_i.astype(jnp.int32)
    a = n * TOP_K
    e_flat = expert.reshape(a)
    onehot = (e_flat[:, None] == jnp.arange(N_EXPERTS, dtype=jnp.int32)[None, :]).astype(jnp.int32)
    rank = jnp.take_along_axis(jnp.cumsum(onehot, axis=0) - onehot, e_flat[:, None], axis=1)[:, 0]
    counts = jnp.sum(onehot, axis=0)
    padded = (counts + MOE_BLOCK - 1) // MOE_BLOCK * MOE_BLOCK
    pad_end = jnp.cumsum(padded)
    pad_off = pad_end - padded
    pos = pad_off[e_flat] + rank
    p_len = (a + N_EXPERTS * MOE_BLOCK + MOE_BLOCK - 1) // MOE_BLOCK * MOE_BLOCK
    n_blk = p_len // MOE_BLOCK
    tok_flat = jnp.arange(a, dtype=jnp.int32) // TOP_K
    buf_tok = jnp.zeros((p_len,), jnp.int32).at[pos].set(tok_flat)
    blk_start = jnp.arange(n_blk, dtype=jnp.int32) * MOE_BLOCK
    blk_exp = jnp.minimum(jnp.sum(pad_end[None, :] <= blk_start[:, None], axis=1), N_EXPERTS - 1).astype(jnp.int32)
    n_used = (pad_end[-1:] // MOE_BLOCK).astype(jnp.int32)
    xg = xb[buf_tok]
    y = _grouped_ffn(xg, blk_exp, n_used, _bf(p['w_gate']), _bf(p['w_up']), _bf(p['w_down']))
    yk = y[pos].reshape(n, TOP_K, d)
    return jnp.sum(yk * weight[:, :, None], axis=1)


def _final_norm_kernel(x_ref, w_ref, o_ref):
    x = x_ref[...]
    o_ref[...] = x * lax.rsqrt(jnp.mean(x * x, axis=-1, keepdims=True) + NORM_EPS) * w_ref[...]


def _final_norm(x, w, tm=512):
    m, d = x.shape
    return pl.pallas_call(
        _final_norm_kernel,
        grid=(m // tm,),
        in_specs=[pl.BlockSpec((tm, d), lambda i: (i, 0)), pl.BlockSpec((1, d), lambda i: (0, 0))],
        out_specs=pl.BlockSpec((tm, d), lambda i: (i, 0)),
        out_shape=jax.ShapeDtypeStruct((m, d), jnp.float32),
        compiler_params=_cparams("parallel"),
        name="final_rmsnorm",
    )(x, w.reshape(1, d))


def kernel(x, c, ctx, c_ctx, w_mod, b_mod, norm_mix_w, norm_ffn_w, w_in, b_in, hy_conv_w, hy_conv_b, hy_f_w1, hy_f_b1, hy_f_w2, hy_f_b2, hy_f_w3, hy_f_b3, hy_f_freq, hy_skip, hg_lb_raw, hg_norm_w, gl_w_a2, gl_b_a, gl_norm_w, w_branch, w_out, w_rg, b_rg, w_re, b_re, w_gate, w_up, w_down, final_norm_w):
    assert x.shape[0] == 1
    depth = w_mod.shape[0]
    lb_all = jnp.cumsum(jax.nn.softmax(hg_lb_raw, axis=0), axis=0)
    lb_all = lb_all - lb_all[:1]
    h, hc = x[0], ctx[0]
    cc = jnp.concatenate([c, c_ctx[None, :]], axis=0)
    for l in range(depth):
        need_ctx = l < depth - 1
        mod = _matmul(_bf(jax.nn.silu(cc)), _bf(w_mod[l]), b_mod[l])
        sh1, sc1, gt1, sh2, sc2, gt2 = jnp.split(mod[0:1], 6, axis=-1)
        sh1c, sc1c, gt1c, sh2c, sc2c, gt2c = jnp.split(mod[1:2], 6, axis=-1)
        p = dict(w_in=w_in[l], b_in=b_in[l], hy_conv_w=hy_conv_w[l], hy_conv_b=hy_conv_b[l],
                 hy_f=(hy_f_w1[l], hy_f_b1[l], hy_f_w2[l], hy_f_b2[l], hy_f_w3[l], hy_f_b3[l], hy_f_freq[l]),
                 hy_skip=hy_skip[l], lb=lb_all[l], hg_norm_w=hg_norm_w[l], gl_w_a2=gl_w_a2[l], gl_b_a=gl_b_a[l],
                 gl_norm_w=gl_norm_w[l], w_branch=w_branch[l], w_out=w_out[l], w_rg=w_rg[l], b_rg=b_rg[l],
                 w_re=w_re[l], b_re=b_re[l], w_gate=w_gate[l], w_up=w_up[l], w_down=w_down[l])
        u = _modulate(h, norm_mix_w[l], sh1, sc1)
        uc = _modulate(hc, norm_mix_w[l], sh1c, sc1c)
        h, hc = _mixer(h, hc, u, uc, gt1, gt1c, p, need_ctx)
        h = h + gt2 * _hier_moe(_modulate(h, norm_ffn_w[l], sh2, sc2), p)
        if need_ctx:
            hc = hc + gt2c * _hier_moe(_modulate(hc, norm_ffn_w[l], sh2c, sc2c), p)
    return _final_norm(h, final_norm_w)[None]
```

```python
import functools
import math

import jax
import jax.numpy as jnp
import numpy as np
from jax import lax
from jax.experimental import pallas as pl
from jax.experimental.pallas import tpu as pltpu

D_MODEL = 2048
NORM_EPS = 1e-6

HY_C = D_MODEL // 2
HY_EMB = 33
HY_BANDS = (HY_EMB - 1) // 2
HY_DECAY_TARGET = 1e-2
HY_FAST_PCT = 0.3
HY_SLOW_PCT = 1.5
HY_MOD_SHIFT = 0.05

HG_HEADS = 8
HG_DK = 128
HG_DV = 128
HG_K = HG_HEADS * HG_DK
HG_V = HG_HEADS * HG_DV

GL_HEADS = 4
GL_DK = 128
GL_DV = 256
GL_K = GL_HEADS * GL_DK
GL_V = GL_HEADS * GL_DV
GL_RANK = 16
GL_TAU = 16.0

N_BRANCH = 3
HG_COLS = 3 * HG_K + 2 * HG_V
GL_COLS = 2 * GL_K + 2 * GL_V + 2 * GL_RANK
REC_COLS = HG_COLS + GL_COLS
HY_COLS = 3 * HY_C
MERGE_COLS = N_BRANCH * D_MODEL

COL_TILE = 512
GL_Q_OFF = HG_COLS
GL_V_OFF = GL_Q_OFF + 2 * GL_K
GL_A_OFF = GL_V_OFF + 2 * GL_V
HY_OFF = -(-(GL_A_OFF + 2 * GL_RANK) // HY_C) * HY_C
MG_OFF = HY_OFF + HY_COLS
Z_COLS = MG_OFF + MERGE_COLS
assert GL_Q_OFF % GL_K == 0 and GL_V_OFF % GL_V == 0 and GL_A_OFF % 128 == 0 and Z_COLS % COL_TILE == 0
assert MG_OFF % D_MODEL == 0

N_GROUPS = 4
EXP_PER_GROUP = 8
N_EXPERTS = N_GROUPS * EXP_PER_GROUP
TOP_K = 2
D_FF = D_MODEL // 4
MOE_BLOCK = 256

SCAN_CHUNK = 128

VMEM_LIMIT_BYTES = 56 * 1024 * 1024


def _cparams(*sem):
    return pltpu.CompilerParams(dimension_semantics=sem, vmem_limit_bytes=VMEM_LIMIT_BYTES)


def _bf(a):
    return a.astype(jnp.bfloat16)


def _mm_kernel(x_ref, w_ref, b_ref, o_ref):
    acc = jnp.dot(x_ref[...], w_ref[...], preferred_element_type=jnp.float32) + b_ref[...]
    o_ref[...] = acc.astype(o_ref.dtype)


def _matmul(x, w, bias=None, tm=512, tn=COL_TILE, out_dtype=jnp.float32):
    m, k = x.shape
    n = w.shape[1]
    tm = min(tm, -(-m // 8) * 8)
    mp = -(-m // tm) * tm
    np_ = -(-n // tn) * tn
    if bias is None:
        bias = jnp.zeros((n,), jnp.float32)
    if mp != m:
        x = jnp.pad(x, ((0, mp - m), (0, 0)))
    if np_ != n:
        w = jnp.pad(w, ((0, 0), (0, np_ - n)))
        bias = jnp.pad(bias, (0, np_ - n))
    out = pl.pallas_call(
        _mm_kernel,
        grid=(mp // tm, np_ // tn),
        in_specs=[pl.BlockSpec((tm, k), lambda i, j: (i, 0)),
                  pl.BlockSpec((k, tn), lambda i, j: (0, j)),
                  pl.BlockSpec((1, tn), lambda i, j: (0, j))],
        out_specs=pl.BlockSpec((tm, tn), lambda i, j: (i, j)),
        out_shape=jax.ShapeDtypeStruct((mp, np_), out_dtype),
        compiler_params=_cparams("parallel", "arbitrary"),
        name="dense_matmul",
    )(x, w, bias.reshape(1, np_))
    if mp != m or np_ != n:
        out = out[:m, :n]
    return out


def _norm_mod_kernel(*refs, with_router):
    if with_router:
        h_ref, w_ref, sh_ref, sc_ref, wr_ref, br_ref, o_ref, lg_ref = refs
    else:
        h_ref, w_ref, sh_ref, sc_ref, o_ref = refs
    x = h_ref[...]
    y = x * lax.rsqrt(jnp.mean(x * x, axis=-1, keepdims=True) + NORM_EPS) * w_ref[...]
    u = (y * (1.0 + sc_ref[...]) + sh_ref[...]).astype(jnp.bfloat16)
    o_ref[...] = u
    if with_router:
        lg_ref[...] = jnp.dot(u, wr_ref[...], preferred_element_type=jnp.float32) + br_ref[...]


def _norm_mod(h, w, shift, scale, router=None):
    m, d = h.shape
    tm = min(512, m)
    row = pl.BlockSpec((tm, d), lambda i: (i, 0))
    vec = pl.BlockSpec((1, d), lambda i: (0, 0))
    args = [h, w.reshape(1, d), shift, scale]
    specs = [row, vec, vec, vec]
    out_shape = [jax.ShapeDtypeStruct((m, d), jnp.bfloat16)]
    out_specs = [row]
    if router is not None:
        args += list(router)
        specs += [pl.BlockSpec(router[0].shape, lambda i: (0, 0)), pl.BlockSpec(router[1].shape, lambda i: (0, 0))]
        out_shape.append(jax.ShapeDtypeStruct((m, router[0].shape[1]), jnp.float32))
        out_specs.append(pl.BlockSpec((tm, router[0].shape[1]), lambda i: (i, 0)))
    out = pl.pallas_call(
        functools.partial(_norm_mod_kernel, with_router=router is not None),
        grid=(m // tm,),
        in_specs=specs,
        out_specs=out_specs,
        out_shape=out_shape,
        compiler_params=_cparams("parallel"),
        name="norm_modulate",
    )(*args)
    return out if router is not None else out[0]


def _scan_masks(c, reverse):
    t = np.arange(c)
    ms = [np.eye(c, dtype=np.float32)]
    for lvl in range(int(math.log2(c))):
        upper = ((t >> lvl) & 1).astype(bool)
        same = (t[:, None] >> (lvl + 1)) == (t[None, :] >> (lvl + 1))
        m = same & upper[:, None] & (~upper)[None, :]
        ms.append((m.T if reverse else m).astype(np.float32))
    tri = t[None, :] >= t[:, None] if reverse else t[None, :] <= t[:, None]
    return jnp.asarray(np.stack(ms)), jnp.asarray(tri.astype(np.float32), dtype=jnp.bfloat16)


def _level_arg(cum, lvl, reverse):
    c = cum.shape[0]
    blk = 1 << lvl
    if blk >= 8:
        pieces = []
        for gs in range(0, c, 2 * blk):
            ref = cum[gs + blk:gs + blk + 1, :]
            pieces.append(ref - cum[gs:gs + blk, :])
            pieces.append(cum[gs + blk:gs + 2 * blk, :] - ref)
        arg = jnp.concatenate(pieces, axis=0)
    else:
        c3 = cum.reshape(c // 8, 8, cum.shape[1])
        sub = lax.broadcasted_iota(jnp.int32, c3.shape, 1)
        ref_row = ((sub >> lvl) | 1) << lvl
        ref = None
        for r in range(blk, 8, 2 * blk):
            cand = jnp.broadcast_to(c3[:, r:r + 1, :], c3.shape)
            ref = cand if ref is None else jnp.where(ref_row == r, cand, ref)
        upper = ((sub >> lvl) & 1) == 1
        arg = jnp.where(upper, c3 - ref, ref - c3).reshape(cum.shape)
    return -arg if reverse else arg


def _dot_nt(a, b):
    return lax.dot_general(a, b, (((1,), (1,)), ((), ())), preferred_element_type=jnp.float32)


def _dot_tn(a, b):
    return lax.dot_general(a, b, (((0,), (0,)), ((), ())), preferred_element_type=jnp.float32)


def _stable_log_sigmoid(z):
    return jnp.minimum(z, 0.0) - jnp.log1p(jnp.exp(-jnp.abs(z)))


def _scan_kernel(*refs, mode, reverse, final, heads, dk, dv, c):
    it = iter(refs)
    q_ref = next(it)
    k_ref = next(it)
    v_ref = next(it)
    if mode == "hg":
        lbp_ref = next(it)
    else:
        a_ref = next(it)
        wa_ref = next(it)
        ba_ref = next(it)
    s0_ref = next(it)
    masks_ref = next(it)
    tri_ref = next(it)
    if final:
        oprev_ref = next(it)
        gate_ref = next(it)
        nw_ref = next(it)
    o_ref = next(it)
    st_ref = next(it)

    @pl.when(pl.program_id(0) == 0)
    def _():
        st_ref[...] = s0_ref[...]

    if mode == "gl":
        la_all = jnp.dot(a_ref[...].astype(jnp.bfloat16), wa_ref[...],
                         preferred_element_type=jnp.float32) + ba_ref[...]
    tri = tri_ref[...]
    tot_row = 0 if reverse else c - 1
    n_lvl = int(math.log2(c))
    for h in range(heads):
        ks = slice(h * dk, (h + 1) * dk)
        vs = slice(h * dv, (h + 1) * dv)
        q = q_ref[:, ks]
        v = v_ref[:, vs].astype(jnp.bfloat16)
        if mode == "hg":
            zf = k_ref[:, ks]
            la = lbp_ref[0:1, ks]
            lbb = lbp_ref[1:2, ks] + _stable_log_sigmoid(zf)
            g = jnp.maximum(la, lbb) + jnp.log1p(jnp.exp(-jnp.abs(la - lbb)))
            k = lbp_ref[2:3, ks] * jax.nn.sigmoid(-zf)
            q = q * jax.nn.sigmoid(q)
        else:
            g = _stable_log_sigmoid(la_all[:, ks]) * (1.0 / GL_TAU)
            k = k_ref[:, ks]
            q = q * (dk ** -0.5)
        g1 = g.astype(jnp.bfloat16)
        r1 = g - g1.astype(jnp.float32)
        g2 = r1.astype(jnp.bfloat16)
        g3 = (r1 - g2.astype(jnp.float32)).astype(jnp.bfloat16)
        cum = (jnp.dot(tri, g1, preferred_element_type=jnp.float32)
               + jnp.dot(tri, g2, preferred_element_type=jnp.float32)
               + jnp.dot(tri, g3, preferred_element_type=jnp.float32))
        tot = cum[tot_row:tot_row + 1, :]
        st = st_ref[h]
        o = _dot_nt((q * jnp.exp(cum)).astype(jnp.bfloat16), st.astype(jnp.bfloat16))
        kt = (k * jnp.exp(tot - cum)).astype(jnp.bfloat16)
        st_ref[h] = st * jnp.exp(tot) + _dot_tn(v, kt)
        scores = masks_ref[0] * _dot_nt(q.astype(jnp.bfloat16), k.astype(jnp.bfloat16))
        for lvl in range(n_lvl):
            e = jnp.exp(_level_arg(cum, lvl, reverse))
            scores = scores + masks_ref[1 + lvl] * _dot_nt((q * e).astype(jnp.bfloat16),
                                                           (k * e).astype(jnp.bfloat16))
        o = o + jnp.dot(scores.astype(jnp.bfloat16), v, preferred_element_type=jnp.float32)
        if final:
            o = o + oprev_ref[:, vs]
            y = o * lax.rsqrt(jnp.mean(o * o, axis=-1, keepdims=True) + NORM_EPS) * nw_ref[...]
            gt = gate_ref[:, vs]
            act = jax.nn.sigmoid(gt) if mode == "hg" else gt * jax.nn.sigmoid(gt)
            o_ref[:, vs] = (y * act).astype(o_ref.dtype)
        else:
            o_ref[:, vs] = o


def _scan_pass(mode, reverse, final, L, srcs, s0, params, final_srcs=(), norm_w=None):
    heads, dk, dv = (HG_HEADS, HG_DK, HG_DV) if mode == "hg" else (GL_HEADS, GL_DK, GL_DV)
    c = min(SCAN_CHUNK, L)
    nb = L // c
    row = (lambda i: nb - 1 - i) if reverse else (lambda i: i)
    masks, tri = _scan_masks(c, reverse)

    def const(shape):
        return pl.BlockSpec(shape, lambda i: (0,) * len(shape))

    def rowblock(width, cb):
        return pl.BlockSpec((c, width), lambda i: (row(i), cb))

    args = [a for a, _, _ in srcs] + list(params) + [s0, masks, tri]
    specs = ([rowblock(w, cb) for _, w, cb in srcs] + [const(p.shape) for p in params]
             + [const(s0.shape), const(masks.shape), const(tri.shape)])
    if final:
        args += [a for a, _, _ in final_srcs] + [norm_w]
        specs += [rowblock(w, cb) for _, w, cb in final_srcs] + [const(norm_w.shape)]
    return pl.pallas_call(
        functools.partial(_scan_kernel, mode=mode, reverse=reverse, final=final, heads=heads, dk=dk, dv=dv, c=c),
        grid=(nb,),
        in_specs=specs,
        out_specs=[pl.BlockSpec((c, heads * dv), lambda i: (row(i), 0)), const((heads, dv, dk))],
        out_shape=[jax.ShapeDtypeStruct((L, heads * dv), jnp.bfloat16 if final else jnp.float32),
                   jax.ShapeDtypeStruct((heads, dv, dk), jnp.float32)],
        compiler_params=_cparams("arbitrary"),
        name=f"scan_{mode}_{'bwd' if reverse else 'fwd'}",
    )(*args)


def _hgrn2(z, L, lb, norm_w, s0_f, s0_b):
    lbp = lambda d: jnp.stack([jnp.log(lb[d]), jnp.log1p(-lb[d]), 1.0 - lb[d]])
    w = HG_K
    o_b, s_b = _scan_pass("hg", True, False, L, [(z, w, 0), (z, w, 2), (z, w, 3)], s0_b, [lbp(1)])
    y, s_f = _scan_pass("hg", False, True, L, [(z, w, 0), (z, w, 1), (z, w, 3)], s0_f, [lbp(0)],
                        final_srcs=[(o_b, HG_V, 0), (z, HG_V, 4)], norm_w=norm_w.reshape(1, HG_DV))
    return y, s_f, s_b


def _gla(z, L, w_a2, b_a, norm_w, s0_f, s0_b):
    def gate_params(d):
        wa = jnp.zeros((128, GL_K), jnp.float32).at[d * GL_RANK:(d + 1) * GL_RANK].set(w_a2[d])
        return [_bf(wa), b_a[d].reshape(1, GL_K)]

    srcs = [(z, GL_K, GL_Q_OFF // GL_K), (z, GL_K, GL_Q_OFF // GL_K + 1), (z, GL_V, GL_V_OFF // GL_V),
            (z, 128, GL_A_OFF // 128)]
    o_b, s_b = _scan_pass("gl", True, False, L, srcs, s0_b, gate_params(1))
    y, s_f = _scan_pass("gl", False, True, L, srcs, s0_f, gate_params(0),
                        final_srcs=[(o_b, GL_V, 0), (z, GL_V, GL_V_OFF // GL_V + 1)], norm_w=norm_w.reshape(1, GL_DV))
    return y, s_f, s_b


def _hyena_filters(L, w1, b1, w2, b2, w3, b3, freq):
    t = jnp.linspace(0.0, 1.0, L, dtype=jnp.float32)[:, None]
    ang = 2.0 * math.pi * jnp.arange(L, dtype=jnp.float32)[:, None] / L
    bands = jnp.linspace(1e-4, HY_BANDS - 1, HY_BANDS, dtype=jnp.float32)[None, :]
    feats = jnp.concatenate([t, jnp.cos(bands * ang), -jnp.sin(bands * ang)], axis=-1)
    h = jnp.sin(freq * (feats @ w1 + b1))
    h = jnp.sin(freq * (h @ w2 + b2))
    h = h @ w3 + b3
    deltas = jnp.abs(jnp.linspace(math.log(HY_DECAY_TARGET) / HY_SLOW_PCT, math.log(HY_DECAY_TARGET) / HY_FAST_PCT,
                                  HY_C, dtype=jnp.float32))
    decay = jnp.exp(-t * deltas)
    h = h * (jnp.tile(decay, (1, 2)) + HY_MOD_SHIFT)
    inorm = 1.0 / jnp.sum(jnp.abs(h), axis=0)
    return h, inorm.reshape(2, HY_C)


def _hy_pre_kernel(x0_ref, x1_ref, v_ref, x0p_ref, x1p_ref, vp_ref, x0n_ref, x1n_ref, vn_ref, w_ref, b_ref,
                   vo_ref, x0o_ref):
    i = pl.program_id(0)
    first = i == 0
    last = i == pl.num_programs(0) - 1
    tm = x0_ref.shape[0]
    row = lax.broadcasted_iota(jnp.int32, x0_ref.shape, 0)

    def conv(x_ref, p_ref, n_ref, g):
        x = x_ref[...]
        cs = slice(g * HY_C, (g + 1) * HY_C)
        prev_row = jnp.where(first, 0.0, p_ref[7:8, :])
        next_row = jnp.where(last, 0.0, n_ref[0:1, :])
        xp = jnp.where(row == 0, prev_row, pltpu.roll(x, 1, 0))
        xn = jnp.where(row == tm - 1, next_row, pltpu.roll(x, tm - 1, 0))
        return w_ref[0:1, cs] * xp + w_ref[1:2, cs] * x + w_ref[2:3, cs] * xn + b_ref[0:1, cs]

    x0 = conv(x0_ref, x0p_ref, x0n_ref, 0)
    x1 = conv(x1_ref, x1p_ref, x1n_ref, 1)
    v = conv(v_ref, vp_ref, vn_ref, 2)
    vo_ref[...] = (v * x1).astype(vo_ref.dtype)
    x0o_ref[...] = x0.astype(x0o_ref.dtype)


def _hy_pre(z, L, conv_w, conv_b):
    tm = min(256, L)
    nb8 = L // 8
    cb = HY_OFF // HY_C
    main = lambda g: pl.BlockSpec((tm, HY_C), lambda i: (i, cb + g))
    prev = lambda g: pl.BlockSpec((8, HY_C), lambda i: (jnp.maximum(i * (tm // 8) - 1, 0), cb + g))
    nxt = lambda g: pl.BlockSpec((8, HY_C), lambda i: (jnp.minimum((i + 1) * (tm // 8), nb8 - 1), cb + g))
    const = lambda a: pl.BlockSpec(a.shape, lambda i: (0, 0))
    cbias = conv_b.reshape(1, HY_COLS)
    out = jax.ShapeDtypeStruct((L, HY_C), jnp.bfloat16)
    return pl.pallas_call(
        _hy_pre_kernel,
        grid=(L // tm,),
        in_specs=[main(0), main(1), main(2), prev(0), prev(1), prev(2), nxt(0), nxt(1), nxt(2),
                  const(conv_w), const(cbias)],
        out_specs=[pl.BlockSpec((tm, HY_C), lambda i: (i, 0))] * 2,
        out_shape=[out, out],
        compiler_params=_cparams("parallel"),
        name="hyena_short_conv",
    )(z, z, z, z, z, z, z, z, z, conv_w, cbias)


HY_N1 = 256
HY_TWO_STAGE_MIN_L = 1024


def _dft_outer_table(n1, cols):
    ang = -2.0 * np.pi * np.outer(np.arange(n1 // 2) + 0.5, np.arange(cols)) / n1
    return jnp.asarray(np.concatenate([np.cos(ang), np.sin(ang)], axis=0), jnp.bfloat16)


def _dft_inner_table(n1, n2):
    j2 = np.arange(n2)
    f_ang = -2.0 * np.pi * np.outer(np.arange(n2), j2) / n2
    tw_ang = -2.0 * np.pi * np.outer(np.arange(n1 // 2) + 0.5, j2) / (n1 * n2)
    fr, fi = jnp.asarray(np.cos(f_ang), jnp.float32), jnp.asarray(np.sin(f_ang), jnp.float32)
    twr, twi = jnp.asarray(np.cos(tw_ang), jnp.float32), jnp.asarray(np.sin(tw_ang), jnp.float32)
    mr = fr[None] * twr[:, None, :] - fi[None] * twi[:, None, :]
    mi = fr[None] * twi[:, None, :] + fi[None] * twr[:, None, :]
    return _bf(jnp.concatenate([jnp.concatenate([mr, -mi], axis=2), jnp.concatenate([mi, mr], axis=2)], axis=1))


def _spectral_product(xv, xh, inorm, half):
    inf, inb = inorm[0:1, :], inorm[1:2, :]
    gr = xh[:half, :HY_C] * inf + xh[:half, HY_C:] * inb
    gi = xh[half:, :HY_C] * inf - xh[half:, HY_C:] * inb
    xr, xi = xv[:half], xv[half:]
    return jnp.concatenate([xr * gr - xi * gi, xr * gi + xi * gr], axis=0).astype(jnp.bfloat16)


def _hy_spec_kernel(r_ref, avr_ref, avi_ref, ahr_ref, ahi_ref, inorm_ref, br_ref, bi_ref):
    r = r_ref[0]
    n2 = avr_ref.shape[1]
    xv = jnp.dot(r, jnp.concatenate([avr_ref[0], avi_ref[0]], axis=0), preferred_element_type=jnp.float32)
    xh = jnp.dot(r, jnp.concatenate([ahr_ref[0], ahi_ref[0]], axis=0), preferred_element_type=jnp.float32)
    b = _dot_tn(r, _spectral_product(xv, xh, inorm_ref[...], n2))
    br_ref[0] = b[:n2].astype(br_ref.dtype)
    bi_ref[0] = b[n2:].astype(bi_ref.dtype)


def _hy_spec(r, av, ah, inorm, n1, n2):
    av3 = av.reshape(n1, n2, HY_C)
    ah3 = ah.reshape(n1, n2, 2 * HY_C)
    h1 = n1 // 2
    out = jax.ShapeDtypeStruct((h1, n2, HY_C), jnp.bfloat16)
    return pl.pallas_call(
        _hy_spec_kernel,
        grid=(h1,),
        in_specs=[pl.BlockSpec((1, 2 * n2, 2 * n2), lambda k: (k, 0, 0)),
                  pl.BlockSpec((1, n2, HY_C), lambda k: (k, 0, 0)),
                  pl.BlockSpec((1, n2, HY_C), lambda k: (k + h1, 0, 0)),
                  pl.BlockSpec((1, n2, 2 * HY_C), lambda k: (k, 0, 0)),
                  pl.BlockSpec((1, n2, 2 * HY_C), lambda k: (k + h1, 0, 0)),
                  pl.BlockSpec((2, HY_C), lambda k: (0, 0))],
        out_specs=[pl.BlockSpec((1, n2, HY_C), lambda k: (k, 0, 0))] * 2,
        out_shape=[out, out],
        compiler_params=_cparams("parallel"),
        name="hyena_spectral",
    )(r, av3, av3, ah3, ah3, inorm)


def _hy_spec_direct_kernel(xv_ref, xh_ref, inorm_ref, yr_ref, yi_ref):
    half = yr_ref.shape[0]
    y = _spectral_product(xv_ref[...].astype(jnp.float32), xh_ref[...].astype(jnp.float32), inorm_ref[...], half)
    yr_ref[...] = y[:half]
    yi_ref[...] = y[half:]


def _hy_spec_direct(xv, xh, inorm, L):
    full = lambda a: pl.BlockSpec(a.shape, lambda i: (0, 0))
    out = jax.ShapeDtypeStruct((L, HY_C), jnp.bfloat16)
    return pl.pallas_call(
        _hy_spec_direct_kernel,
        grid=(1,),
        in_specs=[full(xv), full(xh), full(inorm)],
        out_specs=[pl.BlockSpec((L, HY_C), lambda i: (0, 0))] * 2,
        out_shape=[out, out],
        compiler_params=_cparams("arbitrary"),
        name="hyena_spectral_direct",
    )(xv, xh, inorm)


def _hy_post_kernel(tr_ref, ti_ref, br_ref, bi_ref, v_ref, x0_ref, skip_ref, o_ref, *, scale):
    acc = (jnp.dot(tr_ref[...], br_ref[...], preferred_element_type=jnp.float32)
           + jnp.dot(ti_ref[...], bi_ref[...], preferred_element_type=jnp.float32))
    y = (acc * scale + v_ref[...].astype(jnp.float32) * skip_ref[...]) * x0_ref[...].astype(jnp.float32)
    o_ref[...] = y.astype(o_ref.dtype)


def _hy_post(t_fwd, b_r, b_i, v, x0, skip, L, h1, n2):
    ncol = n2 * HY_C
    tn = min(4096, ncol)
    tr_t = t_fwd[:h1].T
    ti_t = t_fwd[h1:].T
    skip_t = jnp.tile(skip, tn // HY_C).reshape(1, tn)
    col = lambda rows: pl.BlockSpec((rows, tn), lambda j: (0, j))
    rows_out = tr_t.shape[0]
    y = pl.pallas_call(
        functools.partial(_hy_post_kernel, scale=1.0 / L),
        grid=(ncol // tn,),
        in_specs=[pl.BlockSpec(tr_t.shape, lambda j: (0, 0)), pl.BlockSpec(ti_t.shape, lambda j: (0, 0)),
                  col(h1), col(h1), col(rows_out), col(rows_out), pl.BlockSpec((1, tn), lambda j: (0, 0))],
        out_specs=col(rows_out),
        out_shape=jax.ShapeDtypeStruct((rows_out, ncol), jnp.bfloat16),
        compiler_params=_cparams("parallel"),
        name="hyena_inverse",
    )(tr_t, ti_t, b_r.reshape(h1, ncol), b_i.reshape(h1, ncol), v.reshape(rows_out, ncol),
      x0.reshape(rows_out, ncol), skip_t)
    return y.reshape(L, HY_C)


def _hyena(z, L, conv_w, conv_b, fparams, skip):
    v, x0 = _hy_pre(z, L, conv_w, conv_b)
    hfil, inorm = _hyena_filters(L, *fparams)
    if L >= HY_TWO_STAGE_MIN_L:
        n1 = HY_N1
        n2 = 2 * L // n1
        h1 = n1 // 2
        t_fwd = _dft_outer_table(n1, h1)
        av = _matmul(t_fwd, v.reshape(h1, n2 * HY_C), tm=n1, tn=4096, out_dtype=jnp.bfloat16)
        ah = _matmul(t_fwd, _bf(hfil).reshape(h1, n2 * 2 * HY_C), tm=n1, tn=4096, out_dtype=jnp.bfloat16)
        b_r, b_i = _hy_spec(_dft_inner_table(n1, n2), av, ah, inorm, n1, n2)
        return _hy_post(t_fwd, b_r, b_i, v, x0, skip, L, h1, n2)
    t_fwd = _dft_outer_table(2 * L, L)
    xv = _matmul(t_fwd, v, tm=2 * L, tn=HY_C, out_dtype=jnp.bfloat16)
    xh = _matmul(t_fwd, _bf(hfil), tm=2 * L, tn=HY_C, out_dtype=jnp.bfloat16)
    y_r, y_i = _hy_spec_direct(xv, xh, inorm, L)
    return _hy_post(t_fwd, y_r, y_i, v, x0, skip, L, L, 1)


def _merge_kernel(yh_ref, yg_ref, yl_ref, gate_h_ref, gate_g_ref, gate_l_ref, wb_ref, o_ref):
    acc = None
    for br, (y_ref, g_ref) in enumerate(((yh_ref, gate_h_ref), (yg_ref, gate_g_ref), (yl_ref, gate_l_ref))):
        t = jnp.dot(y_ref[...], wb_ref[br], preferred_element_type=jnp.float32) * jax.nn.sigmoid(g_ref[...])
        acc = t if acc is None else acc + t
    o_ref[...] = acc.astype(o_ref.dtype)


def _proj_residual_kernel(m_ref, w_ref, h_ref, gt_ref, o_ref):
    o_ref[...] = h_ref[...] + gt_ref[...] * jnp.dot(m_ref[...], w_ref[...], preferred_element_type=jnp.float32)


def _merge(z, L, ys, w_branch, w_out, h, gt):
    tm = min(256, L)
    gb = MG_OFF // D_MODEL
    ybs = pl.BlockSpec((tm, HY_C), lambda i: (i, 0))
    gate = lambda br: pl.BlockSpec((tm, D_MODEL), lambda i: (i, gb + br))
    row = pl.BlockSpec((tm, D_MODEL), lambda i: (i, 0))
    merged = pl.pallas_call(
        _merge_kernel,
        grid=(L // tm,),
        in_specs=[ybs, ybs, ybs, gate(0), gate(1), gate(2),
                  pl.BlockSpec((N_BRANCH, HY_C, D_MODEL), lambda i: (0, 0, 0))],
        out_specs=row,
        out_shape=jax.ShapeDtypeStruct((L, D_MODEL), jnp.bfloat16),
        compiler_params=_cparams("parallel"),
        name="branch_merge",
    )(ys[0], ys[1], ys[2], z, z, z, _bf(w_branch))
    return pl.pallas_call(
        _proj_residual_kernel,
        grid=(L // tm,),
        in_specs=[row, pl.BlockSpec((D_MODEL, D_MODEL), lambda i: (0, 0)), row,
                  pl.BlockSpec((1, D_MODEL), lambda i: (0, 0))],
        out_specs=row,
        out_shape=jax.ShapeDtypeStruct((L, D_MODEL), jnp.float32),
        compiler_params=_cparams("parallel"),
        name="out_proj_residual",
    )(merged, _bf(w_out), h, gt)


def _pad_cols(a):
    pad = lambda n: jnp.zeros(a.shape[:-1] + (n,), a.dtype)
    return jnp.concatenate([a[..., :REC_COLS], pad(HY_OFF - REC_COLS), a[..., REC_COLS:]], axis=-1)


def _mixer(h, hc, u, uc, gt, gtc, p, need_ctx):
    L, Lc = u.shape[0], uc.shape[0]
    w_in = _pad_cols(_bf(p['w_in']))
    b_in = _pad_cols(p['b_in'])
    z = _matmul(u, w_in, b_in, tm=1024, tn=1024)
    ncol = Z_COLS if need_ctx else HY_OFF
    zc = _matmul(uc, w_in[:, :ncol], b_in[:ncol], tm=1024, tn=1024)
    zeros = lambda hd, dk, dv: jnp.zeros((hd, dv, dk), jnp.float32)
    yc_hg, hg_sf, hg_sb = _hgrn2(zc, Lc, p['lb'], p['hg_norm_w'],
                                 zeros(HG_HEADS, HG_DK, HG_DV), zeros(HG_HEADS, HG_DK, HG_DV))
    yc_gl, gl_sf, gl_sb = _gla(zc, Lc, p['gl_w_a2'], p['gl_b_a'], p['gl_norm_w'],
                               zeros(GL_HEADS, GL_DK, GL_DV), zeros(GL_HEADS, GL_DK, GL_DV))
    y_hg, _, _ = _hgrn2(z, L, p['lb'], p['hg_norm_w'], hg_sf, hg_sb)
    y_gl, _, _ = _gla(z, L, p['gl_w_a2'], p['gl_b_a'], p['gl_norm_w'], gl_sf, gl_sb)
    hy = (p['hy_conv_w'], p['hy_conv_b'], p['hy_f'], p['hy_skip'])
    y_hy = _hyena(z, L, *hy)
    h = _merge(z, L, (y_hy, y_hg, y_gl), p['w_branch'], p['w_out'], h, gt)
    if need_ctx:
        yc_hy = _hyena(zc, Lc, *hy)
        hc = _merge(zc, Lc, (yc_hy, yc_hg, yc_gl), p['w_branch'], p['w_out'], hc, gtc)
    return h, hc


def _ffn_kernel(blk_exp_ref, n_used_ref, x_ref, rw_ref, wg_ref, wu_ref, wd_ref, o_ref, wg_s, wu_s, wd_s):
    i = pl.program_id(0)
    new_expert = jnp.logical_or(i == 0, blk_exp_ref[i] != blk_exp_ref[jnp.maximum(i - 1, 0)])

    @pl.when(jnp.logical_and(i < n_used_ref[0], new_expert))
    def _():
        wg_s[...] = wg_ref[0].astype(jnp.bfloat16)
        wu_s[...] = wu_ref[0].astype(jnp.bfloat16)
        wd_s[...] = wd_ref[0].astype(jnp.bfloat16)

    @pl.when(i < n_used_ref[0])
    def _():
        x = x_ref[...]
        hg = jnp.dot(x, wg_s[...], preferred_element_type=jnp.float32)
        hu = jnp.dot(x, wu_s[...], preferred_element_type=jnp.float32)
        act = (hg * jax.nn.sigmoid(hg) * hu).astype(jnp.bfloat16)
        y = jnp.dot(act, wd_s[...], preferred_element_type=jnp.float32)
        o_ref[...] = (y * rw_ref[...]).astype(o_ref.dtype)

    @pl.when(i >= n_used_ref[0])
    def _():
        o_ref[...] = jnp.zeros_like(o_ref)


def _grouped_ffn(xg, row_w, blk_exp, n_used, w_gate, w_up, w_down):
    p_len, d = xg.shape
    n_blk = p_len // MOE_BLOCK
    grid_spec = pltpu.PrefetchScalarGridSpec(
        num_scalar_prefetch=2,
        grid=(n_blk,),
        in_specs=[pl.BlockSpec((MOE_BLOCK, d), lambda i, be, nu: (i, 0)),
                  pl.BlockSpec((MOE_BLOCK, 1), lambda i, be, nu: (i, 0)),
                  pl.BlockSpec((1, d, D_FF), lambda i, be, nu: (be[i], 0, 0)),
                  pl.BlockSpec((1, d, D_FF), lambda i, be, nu: (be[i], 0, 0)),
                  pl.BlockSpec((1, D_FF, d), lambda i, be, nu: (be[i], 0, 0))],
        out_specs=pl.BlockSpec((MOE_BLOCK, d), lambda i, be, nu: (i, 0)),
        scratch_shapes=[pltpu.VMEM((d, D_FF), jnp.bfloat16), pltpu.VMEM((d, D_FF), jnp.bfloat16),
                        pltpu.VMEM((D_FF, d), jnp.bfloat16)],
    )
    return pl.pallas_call(
        _ffn_kernel,
        grid_spec=grid_spec,
        out_shape=jax.ShapeDtypeStruct((p_len, d), jnp.bfloat16),
        compiler_params=_cparams("arbitrary"),
        name="moe_grouped_ffn",
    )(blk_exp, n_used, xg, row_w, w_gate, w_up, w_down)


ROUTER_COLS = 128


def _hier_moe(h, norm_w, shift, scale, gt, p):
    n, d = h.shape
    pad = ROUTER_COLS - N_GROUPS - N_EXPERTS
    w_r = _bf(jnp.concatenate([p['w_rg'], p['w_re'], jnp.zeros((d, pad), jnp.float32)], axis=1))
    b_r = jnp.concatenate([p['b_rg'], p['b_re'], jnp.zeros((pad,), jnp.float32)]).reshape(1, ROUTER_COLS)
    xb, logits = _norm_mod(h, norm_w, shift, scale, router=(w_r, b_r))
    lg = logits[:, :N_GROUPS]
    p_grp = jax.nn.softmax(lg, axis=-1)
    grp = jnp.argmax(p_grp, axis=-1).astype(jnp.int32)
    p_top = jnp.max(p_grp, axis=-1)
    le = logits[:, N_GROUPS:N_GROUPS + N_EXPERTS].reshape(n, N_GROUPS, EXP_PER_GROUP)
    le = jnp.take_along_axis(le, grp[:, None, None], axis=1)[:, 0]
    top_p, top_i = lax.top_k(jax.nn.softmax(le, axis=-1), TOP_K)
    weight = p_top[:, None] * top_p / jnp.sum(top_p, axis=-1, keepdims=True)
    expert = grp[:, None] * EXP_PER_GROUP + top_i.astype(jnp.int32)
    a = n * TOP_K
    e_flat = expert.reshape(a)
    onehot = (e_flat[:, None] == jnp.arange(N_EXPERTS, dtype=jnp.int32)[None, :]).astype(jnp.int32)
    rank = jnp.take_along_axis(jnp.cumsum(onehot, axis=0) - onehot, e_flat[:, None], axis=1)[:, 0]
    counts = jnp.sum(onehot, axis=0)
    padded = (counts + MOE_BLOCK - 1) // MOE_BLOCK * MOE_BLOCK
    pad_end = jnp.cumsum(padded)
    pad_off = pad_end - padded
    pos = pad_off[e_flat] + rank
    p_len = (a + N_EXPERTS * MOE_BLOCK + MOE_BLOCK - 1) // MOE_BLOCK * MOE_BLOCK
    n_blk = p_len // MOE_BLOCK
    tok_flat = jnp.arange(a, dtype=jnp.int32) // TOP_K
    buf_tok = jnp.zeros((p_len,), jnp.int32).at[pos].set(tok_flat)
    row_w = jnp.zeros((p_len,), jnp.float32).at[pos].set(weight.reshape(a)).reshape(p_len, 1)
    blk_start = jnp.arange(n_blk, dtype=jnp.int32) * MOE_BLOCK
    blk_exp = jnp.minimum(jnp.sum(pad_end[None, :] <= blk_start[:, None], axis=1), N_EXPERTS - 1).astype(jnp.int32)
    n_used = (pad_end[-1:] // MOE_BLOCK).astype(jnp.int32)
    xg = xb[buf_tok]
    y = _grouped_ffn(xg, row_w, blk_exp, n_used, p['w_gate'], p['w_up'], p['w_down'])
    yk = y[pos].reshape(n, TOP_K, d).astype(jnp.float32)
    return h + gt * jnp.sum(yk, axis=1)


def _final_norm_kernel(x_ref, w_ref, o_ref):
    x = x_ref[...]
    o_ref[...] = x * lax.rsqrt(jnp.mean(x * x, axis=-1, keepdims=True) + NORM_EPS) * w_ref[...]


def _final_norm(x, w, tm=512):
    m, d = x.shape
    return pl.pallas_call(
        _final_norm_kernel,
        grid=(m // tm,),
        in_specs=[pl.BlockSpec((tm, d), lambda i: (i, 0)), pl.BlockSpec((1, d), lambda i: (0, 0))],
        out_specs=pl.BlockSpec((tm, d), lambda i: (i, 0)),
        out_shape=jax.ShapeDtypeStruct((m, d), jnp.float32),
        compiler_params=_cparams("parallel"),
        name="final_rmsnorm",
    )(x, w.reshape(1, d))


def kernel(x, c, ctx, c_ctx, w_mod, b_mod, norm_mix_w, norm_ffn_w, w_in, b_in, hy_conv_w, hy_conv_b, hy_f_w1, hy_f_b1, hy_f_w2, hy_f_b2, hy_f_w3, hy_f_b3, hy_f_freq, hy_skip, hg_lb_raw, hg_norm_w, gl_w_a2, gl_b_a, gl_norm_w, w_branch, w_out, w_rg, b_rg, w_re, b_re, w_gate, w_up, w_down, final_norm_w):
    assert x.shape[0] == 1
    depth = w_mod.shape[0]
    lb_all = jnp.cumsum(jax.nn.softmax(hg_lb_raw, axis=0), axis=0)
    lb_all = lb_all - lb_all[:1]
    h, hc = x[0], ctx[0]
    cc = jnp.concatenate([c, c_ctx[None, :]], axis=0)
    for l in range(depth):
        need_ctx = l < depth - 1
        mod = _matmul(_bf(jax.nn.silu(cc)), _bf(w_mod[l]), b_mod[l])
        sh1, sc1, gt1, sh2, sc2, gt2 = jnp.split(mod[0:1], 6, axis=-1)
        sh1c, sc1c, gt1c, sh2c, sc2c, gt2c = jnp.split(mod[1:2], 6, axis=-1)
        p = dict(w_in=w_in[l], b_in=b_in[l], hy_conv_w=hy_conv_w[l], hy_conv_b=hy_conv_b[l],
                 hy_f=(hy_f_w1[l], hy_f_b1[l], hy_f_w2[l], hy_f_b2[l], hy_f_w3[l], hy_f_b3[l], hy_f_freq[l]),
                 hy_skip=hy_skip[l], lb=lb_all[l], hg_norm_w=hg_norm_w[l], gl_w_a2=gl_w_a2[l], gl_b_a=gl_b_a[l],
                 gl_norm_w=gl_norm_w[l], w_branch=w_branch[l], w_out=w_out[l], w_rg=w_rg[l], b_rg=b_rg[l],
                 w_re=w_re[l], b_re=b_re[l], w_gate=w_gate[l], w_up=w_up[l], w_down=w_down[l])
        u = _norm_mod(h, norm_mix_w[l], sh1, sc1)
        uc = _norm_mod(hc, norm_mix_w[l], sh1c, sc1c)
        h, hc = _mixer(h, hc, u, uc, gt1, gt1c, p, need_ctx)
        h = _hier_moe(h, norm_ffn_w[l], sh2, sc2, gt2, p)
        if need_ctx:
            hc = _hier_moe(hc, norm_ffn_w[l], sh2c, sc2c, gt2c, p)
    return _final_norm(h, final_norm_w)[None]
```

```python
import functools
import math

import jax
import jax.numpy as jnp
import numpy as np
from jax import lax
from jax.experimental import pallas as pl
from jax.experimental.pallas import tpu as pltpu

D_MODEL = 2048
NORM_EPS = 1e-6

HY_C = D_MODEL // 2
HY_EMB = 33
HY_BANDS = (HY_EMB - 1) // 2
HY_DECAY_TARGET = 1e-2
HY_FAST_PCT = 0.3
HY_SLOW_PCT = 1.5
HY_MOD_SHIFT = 0.05

HG_HEADS = 8
HG_DK = 128
HG_DV = 128
HG_K = HG_HEADS * HG_DK
HG_V = HG_HEADS * HG_DV

GL_HEADS = 4
GL_DK = 128
GL_DV = 256
GL_K = GL_HEADS * GL_DK
GL_V = GL_HEADS * GL_DV
GL_RANK = 16
GL_TAU = 16.0

N_BRANCH = 3
HG_COLS = 3 * HG_K + 2 * HG_V
GL_COLS = 2 * GL_K + 2 * GL_V + 2 * GL_RANK
REC_COLS = HG_COLS + GL_COLS
HY_COLS = 3 * HY_C
MERGE_COLS = N_BRANCH * D_MODEL

COL_TILE = 512
GL_Q_OFF = HG_COLS
GL_V_OFF = GL_Q_OFF + 2 * GL_K
GL_A_OFF = GL_V_OFF + 2 * GL_V
HY_OFF = -(-(GL_A_OFF + 2 * GL_RANK) // HY_C) * HY_C
MG_OFF = HY_OFF + HY_COLS
Z_COLS = MG_OFF + MERGE_COLS
assert GL_Q_OFF % GL_K == 0 and GL_V_OFF % GL_V == 0 and GL_A_OFF % 128 == 0 and Z_COLS % COL_TILE == 0
assert MG_OFF % D_MODEL == 0

N_GROUPS = 4
EXP_PER_GROUP = 8
N_EXPERTS = N_GROUPS * EXP_PER_GROUP
TOP_K = 2
D_FF = D_MODEL // 4
MOE_BLOCK = 256

SCAN_CHUNK = 128

VMEM_LIMIT_BYTES = 56 * 1024 * 1024


def _cparams(*sem):
    return pltpu.CompilerParams(dimension_semantics=sem, vmem_limit_bytes=VMEM_LIMIT_BYTES)


def _bf(a):
    return a.astype(jnp.bfloat16)


def _mm_kernel(x_ref, w_ref, b_ref, o_ref):
    acc = jnp.dot(x_ref[...], w_ref[...], preferred_element_type=jnp.float32) + b_ref[...]
    o_ref[...] = acc.astype(o_ref.dtype)


def _matmul(x, w, bias=None, tm=512, tn=COL_TILE, out_dtype=jnp.float32):
    m, k = x.shape
    n = w.shape[1]
    tm = min(tm, -(-m // 8) * 8)
    mp = -(-m // tm) * tm
    np_ = -(-n // tn) * tn
    if bias is None:
        bias = jnp.zeros((n,), jnp.float32)
    if mp != m:
        x = jnp.pad(x, ((0, mp - m), (0, 0)))
    if np_ != n:
        w = jnp.pad(w, ((0, 0), (0, np_ - n)))
        bias = jnp.pad(bias, (0, np_ - n))
    out = pl.pallas_call(
        _mm_kernel,
        grid=(mp // tm, np_ // tn),
        in_specs=[pl.BlockSpec((tm, k), lambda i, j: (i, 0)),
                  pl.BlockSpec((k, tn), lambda i, j: (0, j)),
                  pl.BlockSpec((1, tn), lambda i, j: (0, j))],
        out_specs=pl.BlockSpec((tm, tn), lambda i, j: (i, j)),
        out_shape=jax.ShapeDtypeStruct((mp, np_), out_dtype),
        compiler_params=_cparams("parallel", "arbitrary"),
        name="dense_matmul",
    )(x, w, bias.reshape(1, np_))
    if mp != m or np_ != n:
        out = out[:m, :n]
    return out


def _norm_mod_kernel(*refs, with_router):
    if with_router:
        h_ref, w_ref, sh_ref, sc_ref, wr_ref, br_ref, o_ref, lg_ref = refs
    else:
        h_ref, w_ref, sh_ref, sc_ref, o_ref = refs
    x = h_ref[...]
    y = x * lax.rsqrt(jnp.mean(x * x, axis=-1, keepdims=True) + NORM_EPS) * w_ref[...]
    u = (y * (1.0 + sc_ref[...]) + sh_ref[...]).astype(jnp.bfloat16)
    o_ref[...] = u
    if with_router:
        lg_ref[...] = jnp.dot(u, wr_ref[...], preferred_element_type=jnp.float32) + br_ref[...]


def _norm_mod(h, w, shift, scale, router=None):
    m, d = h.shape
    tm = min(512, m)
    row = pl.BlockSpec((tm, d), lambda i: (i, 0))
    vec = pl.BlockSpec((1, d), lambda i: (0, 0))
    args = [h, w.reshape(1, d), shift, scale]
    specs = [row, vec, vec, vec]
    out_shape = [jax.ShapeDtypeStruct((m, d), jnp.bfloat16)]
    out_specs = [row]
    if router is not None:
        args += list(router)
        specs += [pl.BlockSpec(router[0].shape, lambda i: (0, 0)), pl.BlockSpec(router[1].shape, lambda i: (0, 0))]
        out_shape.append(jax.ShapeDtypeStruct((m, router[0].shape[1]), jnp.float32))
        out_specs.append(pl.BlockSpec((tm, router[0].shape[1]), lambda i: (i, 0)))
    out = pl.pallas_call(
        functools.partial(_norm_mod_kernel, with_router=router is not None),
        grid=(m // tm,),
        in_specs=specs,
        out_specs=out_specs,
        out_shape=out_shape,
        compiler_params=_cparams("parallel"),
        name="norm_modulate",
    )(*args)
    return out if router is not None else out[0]


def _scan_masks(c, reverse):
    t = np.arange(c)
    ms = [np.eye(c, dtype=np.float32)]
    for lvl in range(int(math.log2(c))):
        upper = ((t >> lvl) & 1).astype(bool)
        same = (t[:, None] >> (lvl + 1)) == (t[None, :] >> (lvl + 1))
        m = same & upper[:, None] & (~upper)[None, :]
        ms.append((m.T if reverse else m).astype(np.float32))
    tri = t[None, :] >= t[:, None] if reverse else t[None, :] <= t[:, None]
    return jnp.asarray(np.stack(ms)), jnp.asarray(tri.astype(np.float32), dtype=jnp.bfloat16)


def _level_arg(cum, lvl, reverse):
    c = cum.shape[0]
    blk = 1 << lvl
    if blk >= 8:
        pieces = []
        for gs in range(0, c, 2 * blk):
            ref = cum[gs + blk:gs + blk + 1, :]
            pieces.append(ref - cum[gs:gs + blk, :])
            pieces.append(cum[gs + blk:gs + 2 * blk, :] - ref)
        arg = jnp.concatenate(pieces, axis=0)
    else:
        c3 = cum.reshape(c // 8, 8, cum.shape[1])
        sub = lax.broadcasted_iota(jnp.int32, c3.shape, 1)
        ref_row = ((sub >> lvl) | 1) << lvl
        ref = None
        for r in range(blk, 8, 2 * blk):
            cand = jnp.broadcast_to(c3[:, r:r + 1, :], c3.shape)
            ref = cand if ref is None else jnp.where(ref_row == r, cand, ref)
        upper = ((sub >> lvl) & 1) == 1
        arg = jnp.where(upper, c3 - ref, ref - c3).reshape(cum.shape)
    return -arg if reverse else arg


def _dot_nt(a, b):
    return lax.dot_general(a, b, (((1,), (1,)), ((), ())), preferred_element_type=jnp.float32)


def _dot_tn(a, b):
    return lax.dot_general(a, b, (((0,), (0,)), ((), ())), preferred_element_type=jnp.float32)


def _stable_log_sigmoid(z):
    return jnp.minimum(z, 0.0) - jnp.log1p(jnp.exp(-jnp.abs(z)))


def _scan_kernel(*refs, mode, reverse, final, heads, dk, dv, c):
    it = iter(refs)
    q_ref = next(it)
    k_ref = next(it)
    v_ref = next(it)
    if mode == "hg":
        lbp_ref = next(it)
    else:
        a_ref = next(it)
        wa_ref = next(it)
        ba_ref = next(it)
    s0_ref = next(it)
    masks_ref = next(it)
    tri_ref = next(it)
    if final:
        oprev_ref = next(it)
        gate_ref = next(it)
        nw_ref = next(it)
    o_ref = next(it)
    st_ref = next(it)

    @pl.when(pl.program_id(0) == 0)
    def _():
        st_ref[...] = s0_ref[...]

    if mode == "gl":
        la_all = jnp.dot(a_ref[...].astype(jnp.bfloat16), wa_ref[...],
                         preferred_element_type=jnp.float32) + ba_ref[...]
    tri = tri_ref[...]
    tot_row = 0 if reverse else c - 1
    n_lvl = int(math.log2(c))
    for h in range(heads):
        ks = slice(h * dk, (h + 1) * dk)
        vs = slice(h * dv, (h + 1) * dv)
        q = q_ref[:, ks]
        v = v_ref[:, vs].astype(jnp.bfloat16)
        if mode == "hg":
            zf = k_ref[:, ks]
            la = lbp_ref[0:1, ks]
            lbb = lbp_ref[1:2, ks] + _stable_log_sigmoid(zf)
            g = jnp.maximum(la, lbb) + jnp.log1p(jnp.exp(-jnp.abs(la - lbb)))
            k = lbp_ref[2:3, ks] * jax.nn.sigmoid(-zf)
            q = q * jax.nn.sigmoid(q)
        else:
            g = _stable_log_sigmoid(la_all[:, ks]) * (1.0 / GL_TAU)
            k = k_ref[:, ks]
            q = q * (dk ** -0.5)
        g1 = g.astype(jnp.bfloat16)
        r1 = g - g1.astype(jnp.float32)
        g2 = r1.astype(jnp.bfloat16)
        g3 = (r1 - g2.astype(jnp.float32)).astype(jnp.bfloat16)
        cum = (jnp.dot(tri, g1, preferred_element_type=jnp.float32)
               + jnp.dot(tri, g2, preferred_element_type=jnp.float32)
               + jnp.dot(tri, g3, preferred_element_type=jnp.float32))
        tot = cum[tot_row:tot_row + 1, :]
        st = st_ref[h]
        o = _dot_nt((q * jnp.exp(cum)).astype(jnp.bfloat16), st.astype(jnp.bfloat16))
        kt = (k * jnp.exp(tot - cum)).astype(jnp.bfloat16)
        st_ref[h] = st * jnp.exp(tot) + _dot_tn(v, kt)
        scores = masks_ref[0] * _dot_nt(q.astype(jnp.bfloat16), k.astype(jnp.bfloat16))
        for lvl in range(n_lvl):
            e = jnp.exp(_level_arg(cum, lvl, reverse))
            scores = scores + masks_ref[1 + lvl] * _dot_nt((q * e).astype(jnp.bfloat16),
                                                           (k * e).astype(jnp.bfloat16))
        o = o + jnp.dot(scores.astype(jnp.bfloat16), v, preferred_element_type=jnp.float32)
        if final:
            o = o + oprev_ref[:, vs]
            y = o * lax.rsqrt(jnp.mean(o * o, axis=-1, keepdims=True) + NORM_EPS) * nw_ref[...]
            gt = gate_ref[:, vs]
            act = jax.nn.sigmoid(gt) if mode == "hg" else gt * jax.nn.sigmoid(gt)
            o_ref[:, vs] = (y * act).astype(o_ref.dtype)
        else:
            o_ref[:, vs] = o


def _scan_pass(mode, reverse, final, L, srcs, s0, params, final_srcs=(), norm_w=None):
    heads, dk, dv = (HG_HEADS, HG_DK, HG_DV) if mode == "hg" else (GL_HEADS, GL_DK, GL_DV)
    c = min(SCAN_CHUNK, L)
    nb = L // c
    row = (lambda i: nb - 1 - i) if reverse else (lambda i: i)
    masks, tri = _scan_masks(c, reverse)

    def const(shape):
        return pl.BlockSpec(shape, lambda i: (0,) * len(shape))

    def rowblock(width, cb):
        return pl.BlockSpec((c, width), lambda i: (row(i), cb))

    args = [a for a, _, _ in srcs] + list(params) + [s0, masks, tri]
    specs = ([rowblock(w, cb) for _, w, cb in srcs] + [const(p.shape) for p in params]
             + [const(s0.shape), const(masks.shape), const(tri.shape)])
    if final:
        args += [a for a, _, _ in final_srcs] + [norm_w]
        specs += [rowblock(w, cb) for _, w, cb in final_srcs] + [const(norm_w.shape)]
    return pl.pallas_call(
        functools.partial(_scan_kernel, mode=mode, reverse=reverse, final=final, heads=heads, dk=dk, dv=dv, c=c),
        grid=(nb,),
        in_specs=specs,
        out_specs=[pl.BlockSpec((c, heads * dv), lambda i: (row(i), 0)), const((heads, dv, dk))],
        out_shape=[jax.ShapeDtypeStruct((L, heads * dv), jnp.bfloat16 if final else jnp.float32),
                   jax.ShapeDtypeStruct((heads, dv, dk), jnp.float32)],
        compiler_params=_cparams("arbitrary"),
        name=f"scan_{mode}_{'bwd' if reverse else 'fwd'}",
    )(*args)


def _hgrn2(z, L, lb, norm_w, s0_f, s0_b):
    lbp = lambda d: jnp.stack([jnp.log(lb[d]), jnp.log1p(-lb[d]), 1.0 - lb[d]])
    w = HG_K
    o_b, s_b = _scan_pass("hg", True, False, L, [(z, w, 0), (z, w, 2), (z, w, 3)], s0_b, [lbp(1)])
    y, s_f = _scan_pass("hg", False, True, L, [(z, w, 0), (z, w, 1), (z, w, 3)], s0_f, [lbp(0)],
                        final_srcs=[(o_b, HG_V, 0), (z, HG_V, 4)], norm_w=norm_w.reshape(1, HG_DV))
    return y, s_f, s_b


def _gla(z, L, w_a2, b_a, norm_w, s0_f, s0_b):
    def gate_params(d):
        wa = jnp.zeros((128, GL_K), jnp.float32).at[d * GL_RANK:(d + 1) * GL_RANK].set(w_a2[d])
        return [_bf(wa), b_a[d].reshape(1, GL_K)]

    srcs = [(z, GL_K, GL_Q_OFF // GL_K), (z, GL_K, GL_Q_OFF // GL_K + 1), (z, GL_V, GL_V_OFF // GL_V),
            (z, 128, GL_A_OFF // 128)]
    o_b, s_b = _scan_pass("gl", True, False, L, srcs, s0_b, gate_params(1))
    y, s_f = _scan_pass("gl", False, True, L, srcs, s0_f, gate_params(0),
                        final_srcs=[(o_b, GL_V, 0), (z, GL_V, GL_V_OFF // GL_V + 1)], norm_w=norm_w.reshape(1, GL_DV))
    return y, s_f, s_b


def _hyena_filters(L, w1, b1, w2, b2, w3, b3, freq):
    t = jnp.linspace(0.0, 1.0, L, dtype=jnp.float32)[:, None]
    ang = 2.0 * math.pi * jnp.arange(L, dtype=jnp.float32)[:, None] / L
    bands = jnp.linspace(1e-4, HY_BANDS - 1, HY_BANDS, dtype=jnp.float32)[None, :]
    feats = jnp.concatenate([t, jnp.cos(bands * ang), -jnp.sin(bands * ang)], axis=-1)
    h = jnp.sin(freq * (feats @ w1 + b1))
    h = jnp.sin(freq * (h @ w2 + b2))
    h = h @ w3 + b3
    deltas = jnp.abs(jnp.linspace(math.log(HY_DECAY_TARGET) / HY_SLOW_PCT, math.log(HY_DECAY_TARGET) / HY_FAST_PCT,
                                  HY_C, dtype=jnp.float32))
    decay = jnp.exp(-t * deltas)
    h = h * (jnp.tile(decay, (1, 2)) + HY_MOD_SHIFT)
    inorm = 1.0 / jnp.sum(jnp.abs(h), axis=0)
    return h, inorm.reshape(2, HY_C)


def _hy_pre_kernel(x0_ref, x1_ref, v_ref, x0p_ref, x1p_ref, vp_ref, x0n_ref, x1n_ref, vn_ref, w_ref, b_ref,
                   vo_ref, x0o_ref):
    i = pl.program_id(0)
    first = i == 0
    last = i == pl.num_programs(0) - 1
    tm = x0_ref.shape[0]
    row = lax.broadcasted_iota(jnp.int32, x0_ref.shape, 0)

    def conv(x_ref, p_ref, n_ref, g):
        x = x_ref[...]
        cs = slice(g * HY_C, (g + 1) * HY_C)
        prev_row = jnp.where(first, 0.0, p_ref[7:8, :])
        next_row = jnp.where(last, 0.0, n_ref[0:1, :])
        xp = jnp.where(row == 0, prev_row, pltpu.roll(x, 1, 0))
        xn = jnp.where(row == tm - 1, next_row, pltpu.roll(x, tm - 1, 0))
        return w_ref[0:1, cs] * xp + w_ref[1:2, cs] * x + w_ref[2:3, cs] * xn + b_ref[0:1, cs]

    x0 = conv(x0_ref, x0p_ref, x0n_ref, 0)
    x1 = conv(x1_ref, x1p_ref, x1n_ref, 1)
    v = conv(v_ref, vp_ref, vn_ref, 2)
    vo_ref[...] = (v * x1).astype(vo_ref.dtype)
    x0o_ref[...] = x0.astype(x0o_ref.dtype)


def _hy_pre(z, L, conv_w, conv_b):
    tm = min(256, L)
    nb8 = L // 8
    cb = HY_OFF // HY_C
    main = lambda g: pl.BlockSpec((tm, HY_C), lambda i: (i, cb + g))
    prev = lambda g: pl.BlockSpec((8, HY_C), lambda i: (jnp.maximum(i * (tm // 8) - 1, 0), cb + g))
    nxt = lambda g: pl.BlockSpec((8, HY_C), lambda i: (jnp.minimum((i + 1) * (tm // 8), nb8 - 1), cb + g))
    const = lambda a: pl.BlockSpec(a.shape, lambda i: (0, 0))
    cbias = conv_b.reshape(1, HY_COLS)
    return pl.pallas_call(
        _hy_pre_kernel,
        grid=(L // tm,),
        in_specs=[main(0), main(1), main(2), prev(0), prev(1), prev(2), nxt(0), nxt(1), nxt(2),
                  const(conv_w), const(cbias)],
        out_specs=[pl.BlockSpec((tm, HY_C), lambda i: (i, 0))] * 2,
        out_shape=[jax.ShapeDtypeStruct((L, HY_C), jnp.float32), jax.ShapeDtypeStruct((L, HY_C), jnp.bfloat16)],
        compiler_params=_cparams("parallel"),
        name="hyena_short_conv",
    )(z, z, z, z, z, z, z, z, z, conv_w, cbias)


HY_N1 = 256
HY_TWO_STAGE_MIN_L = 1024


def _dft_outer_table(n1, cols):
    ang = -2.0 * np.pi * np.outer(np.arange(n1 // 2) + 0.5, np.arange(cols)) / n1
    return jnp.asarray(np.concatenate([np.cos(ang), np.sin(ang)], axis=0), jnp.bfloat16)


def _dft_inner_table(n1, n2):
    j2 = np.arange(n2)
    f_ang = -2.0 * np.pi * np.outer(np.arange(n2), j2) / n2
    tw_ang = -2.0 * np.pi * np.outer(np.arange(n1 // 2) + 0.5, j2) / (n1 * n2)
    fr, fi = jnp.asarray(np.cos(f_ang), jnp.float32), jnp.asarray(np.sin(f_ang), jnp.float32)
    twr, twi = jnp.asarray(np.cos(tw_ang), jnp.float32), jnp.asarray(np.sin(tw_ang), jnp.float32)
    mr = fr[None] * twr[:, None, :] - fi[None] * twi[:, None, :]
    mi = fr[None] * twi[:, None, :] + fi[None] * twr[:, None, :]
    return _bf(jnp.concatenate([jnp.concatenate([mr, -mi], axis=2), jnp.concatenate([mi, mr], axis=2)], axis=1))


def _spectral_product(xv, xh, inorm, half):
    inf, inb = inorm[0:1, :], inorm[1:2, :]
    gr = xh[:half, :HY_C] * inf + xh[:half, HY_C:] * inb
    gi = xh[half:, :HY_C] * inf - xh[half:, HY_C:] * inb
    xr, xi = xv[:half], xv[half:]
    return jnp.concatenate([xr * gr - xi * gi, xr * gi + xi * gr], axis=0).astype(jnp.bfloat16)


def _hy_spec_kernel(r_ref, avr_ref, avi_ref, ahr_ref, ahi_ref, inorm_ref, br_ref, bi_ref):
    r = r_ref[0]
    n2 = avr_ref.shape[1]
    xv = jnp.dot(r, jnp.concatenate([avr_ref[0], avi_ref[0]], axis=0), preferred_element_type=jnp.float32)
    xh = jnp.dot(r, jnp.concatenate([ahr_ref[0], ahi_ref[0]], axis=0), preferred_element_type=jnp.float32)
    b = _dot_tn(r, _spectral_product(xv, xh, inorm_ref[...], n2))
    br_ref[0] = b[:n2].astype(br_ref.dtype)
    bi_ref[0] = b[n2:].astype(bi_ref.dtype)


def _hy_spec(r, av, ah, inorm, n1, n2):
    av3 = av.reshape(n1, n2, HY_C)
    ah3 = ah.reshape(n1, n2, 2 * HY_C)
    h1 = n1 // 2
    out = jax.ShapeDtypeStruct((h1, n2, HY_C), jnp.bfloat16)
    return pl.pallas_call(
        _hy_spec_kernel,
        grid=(h1,),
        in_specs=[pl.BlockSpec((1, 2 * n2, 2 * n2), lambda k: (k, 0, 0)),
                  pl.BlockSpec((1, n2, HY_C), lambda k: (k, 0, 0)),
                  pl.BlockSpec((1, n2, HY_C), lambda k: (k + h1, 0, 0)),
                  pl.BlockSpec((1, n2, 2 * HY_C), lambda k: (k, 0, 0)),
                  pl.BlockSpec((1, n2, 2 * HY_C), lambda k: (k + h1, 0, 0)),
                  pl.BlockSpec((2, HY_C), lambda k: (0, 0))],
        out_specs=[pl.BlockSpec((1, n2, HY_C), lambda k: (k, 0, 0))] * 2,
        out_shape=[out, out],
        compiler_params=_cparams("parallel"),
        name="hyena_spectral",
    )(r, av3, av3, ah3, ah3, inorm)


def _hy_spec_direct_kernel(xv_ref, xh_ref, inorm_ref, yr_ref, yi_ref):
    half = yr_ref.shape[0]
    y = _spectral_product(xv_ref[...].astype(jnp.float32), xh_ref[...].astype(jnp.float32), inorm_ref[...], half)
    yr_ref[...] = y[:half]
    yi_ref[...] = y[half:]


def _hy_spec_direct(xv, xh, inorm, L):
    full = lambda a: pl.BlockSpec(a.shape, lambda i: (0, 0))
    out = jax.ShapeDtypeStruct((L, HY_C), jnp.bfloat16)
    return pl.pallas_call(
        _hy_spec_direct_kernel,
        grid=(1,),
        in_specs=[full(xv), full(xh), full(inorm)],
        out_specs=[pl.BlockSpec((L, HY_C), lambda i: (0, 0))] * 2,
        out_shape=[out, out],
        compiler_params=_cparams("arbitrary"),
        name="hyena_spectral_direct",
    )(xv, xh, inorm)


def _hy_post_kernel(tr_ref, ti_ref, br_ref, bi_ref, v_ref, x0_ref, skip_ref, o_ref, *, scale):
    acc = (jnp.dot(tr_ref[...], br_ref[...], preferred_element_type=jnp.float32)
           + jnp.dot(ti_ref[...], bi_ref[...], preferred_element_type=jnp.float32))
    y = (acc * scale + v_ref[...].astype(jnp.float32) * skip_ref[...]) * x0_ref[...].astype(jnp.float32)
    o_ref[...] = y.astype(o_ref.dtype)


def _hy_post(t_fwd, b_r, b_i, v, x0, skip, L, h1, n2):
    ncol = n2 * HY_C
    tn = min(4096, ncol)
    tr_t = t_fwd[:h1].T
    ti_t = t_fwd[h1:].T
    skip_t = jnp.tile(skip, tn // HY_C).reshape(1, tn)
    col = lambda rows: pl.BlockSpec((rows, tn), lambda j: (0, j))
    rows_out = tr_t.shape[0]
    y = pl.pallas_call(
        functools.partial(_hy_post_kernel, scale=1.0 / L),
        grid=(ncol // tn,),
        in_specs=[pl.BlockSpec(tr_t.shape, lambda j: (0, 0)), pl.BlockSpec(ti_t.shape, lambda j: (0, 0)),
                  col(h1), col(h1), col(rows_out), col(rows_out), pl.BlockSpec((1, tn), lambda j: (0, 0))],
        out_specs=col(rows_out),
        out_shape=jax.ShapeDtypeStruct((rows_out, ncol), jnp.bfloat16),
        compiler_params=_cparams("parallel"),
        name="hyena_inverse",
    )(tr_t, ti_t, b_r.reshape(h1, ncol), b_i.reshape(h1, ncol), v.reshape(rows_out, ncol),
      x0.reshape(rows_out, ncol), skip_t)
    return y.reshape(L, HY_C)


SUBLANE = 8
HY_COL_TILE = 512


def _dft_outer_kron(n1):
    h1 = n1 // 2
    ang = -2.0 * np.pi * np.outer(np.arange(h1) + 0.5, np.arange(h1)) / n1
    eye = np.eye(SUBLANE)
    t_r, t_i = np.cos(ang), np.sin(ang)
    fwd = np.kron(np.concatenate([t_r, t_i], axis=0), eye)
    inv = np.concatenate([np.kron(t_r.T, eye), np.kron(t_i.T, eye)], axis=1)
    return jnp.asarray(fwd, jnp.bfloat16), jnp.asarray(inv, jnp.bfloat16)


def _hy_outer_fwd_kernel(t_ref, x_ref, o_ref):
    x = x_ref[...]
    rows_in, rows_out = x.shape[0], t_ref.shape[0] // SUBLANE
    cw = x.shape[2]
    parts = []
    for s in range(0, x.shape[1], SUBLANE):
        xs = x[:, s:s + SUBLANE, :].reshape(rows_in * SUBLANE, cw).astype(jnp.bfloat16)
        r = jnp.dot(t_ref[...], xs, preferred_element_type=jnp.float32)
        parts.append(r.reshape(rows_out, SUBLANE, cw))
    o_ref[...] = jnp.concatenate(parts, axis=1).astype(o_ref.dtype)


def _hy_outer_fwd(t_kron, x3):
    h1, n2, w = x3.shape
    n1 = 2 * h1
    blk = 2 * SUBLANE
    return pl.pallas_call(
        _hy_outer_fwd_kernel,
        grid=(n2 // blk, w // HY_COL_TILE),
        in_specs=[pl.BlockSpec(t_kron.shape, lambda j, cc: (0, 0)),
                  pl.BlockSpec((h1, blk, HY_COL_TILE), lambda j, cc: (0, j, cc))],
        out_specs=pl.BlockSpec((n1, blk, HY_COL_TILE), lambda j, cc: (0, j, cc)),
        out_shape=jax.ShapeDtypeStruct((n1, n2, w), jnp.bfloat16),
        compiler_params=_cparams("parallel", "parallel"),
        name="hyena_outer_dft",
    )(t_kron, x3)


def _hy_outer_inv_kernel(t_ref, br_ref, bi_ref, v_ref, x0_ref, skip_ref, o_ref, *, scale):
    br = br_ref[...].astype(jnp.float32)
    bi = bi_ref[...].astype(jnp.float32)
    v = v_ref[...]
    x0 = x0_ref[...].astype(jnp.float32)
    h1, _, cw = br.shape
    parts = []
    for s in range(0, br.shape[1], SUBLANE):
        sl = slice(s, s + SUBLANE)
        b = jnp.concatenate([br[:, sl, :].reshape(h1 * SUBLANE, cw), bi[:, sl, :].reshape(h1 * SUBLANE, cw)], axis=0)
        r = jnp.dot(t_ref[...], b.astype(jnp.bfloat16), preferred_element_type=jnp.float32)
        parts.append((r.reshape(h1, SUBLANE, cw) * scale + v[:, sl, :] * skip_ref[...]) * x0[:, sl, :])
    o_ref[...] = jnp.concatenate(parts, axis=1).astype(o_ref.dtype)


def _hy_outer_inv(t_kron_inv, b_r, b_i, v3, x03, skip, L):
    h1, n2, w = b_r.shape
    blk = 2 * SUBLANE
    tile = pl.BlockSpec((h1, blk, HY_COL_TILE), lambda j, cc: (0, j, cc))
    return pl.pallas_call(
        functools.partial(_hy_outer_inv_kernel, scale=1.0 / L),
        grid=(n2 // blk, w // HY_COL_TILE),
        in_specs=[pl.BlockSpec(t_kron_inv.shape, lambda j, cc: (0, 0)), tile, tile, tile, tile,
                  pl.BlockSpec((1, 1, HY_COL_TILE), lambda j, cc: (0, 0, cc))],
        out_specs=tile,
        out_shape=jax.ShapeDtypeStruct((h1, n2, w), jnp.bfloat16),
        compiler_params=_cparams("parallel", "parallel"),
        name="hyena_outer_idft",
    )(t_kron_inv, b_r, b_i, v3, x03, skip.reshape(1, 1, w))


def _hyena(z, L, conv_w, conv_b, fparams, skip):
    v, x0 = _hy_pre(z, L, conv_w, conv_b)
    hfil, inorm = _hyena_filters(L, *fparams)
    if L >= HY_TWO_STAGE_MIN_L:
        n1 = HY_N1
        n2 = 2 * L // n1
        h1 = n1 // 2
        assert n2 % (2 * SUBLANE) == 0
        t_kron, t_kron_inv = _dft_outer_kron(n1)
        v3 = v.reshape(h1, n2, HY_C)
        av = _hy_outer_fwd(t_kron, v3)
        ah = _hy_outer_fwd(t_kron, hfil.reshape(h1, n2, 2 * HY_C))
        b_r, b_i = _hy_spec(_dft_inner_table(n1, n2), av, ah, inorm, n1, n2)
        return _hy_outer_inv(t_kron_inv, b_r, b_i, v3, x0.reshape(h1, n2, HY_C), skip, L).reshape(L, HY_C)
    t_fwd = _dft_outer_table(2 * L, L)
    xv = _matmul(t_fwd, _bf(v), tm=2 * L, tn=HY_C, out_dtype=jnp.bfloat16)
    xh = _matmul(t_fwd, _bf(hfil), tm=2 * L, tn=HY_C, out_dtype=jnp.bfloat16)
    y_r, y_i = _hy_spec_direct(xv, xh, inorm, L)
    return _hy_post(t_fwd, y_r, y_i, v, x0, skip, L, L, 1)


def _merge_kernel(yh_ref, yg_ref, yl_ref, gate_h_ref, gate_g_ref, gate_l_ref, wb_ref, o_ref):
    acc = None
    for br, (y_ref, g_ref) in enumerate(((yh_ref, gate_h_ref), (yg_ref, gate_g_ref), (yl_ref, gate_l_ref))):
        t = jnp.dot(y_ref[...], wb_ref[br], preferred_element_type=jnp.float32) * jax.nn.sigmoid(g_ref[...])
        acc = t if acc is None else acc + t
    o_ref[...] = acc.astype(o_ref.dtype)


def _proj_residual_kernel(m_ref, w_ref, h_ref, gt_ref, o_ref):
    o_ref[...] = h_ref[...] + gt_ref[...] * jnp.dot(m_ref[...], w_ref[...], preferred_element_type=jnp.float32)


def _merge(z, L, ys, w_branch, w_out, h, gt):
    tm = min(256, L)
    gb = MG_OFF // D_MODEL
    ybs = pl.BlockSpec((tm, HY_C), lambda i: (i, 0))
    gate = lambda br: pl.BlockSpec((tm, D_MODEL), lambda i: (i, gb + br))
    row = pl.BlockSpec((tm, D_MODEL), lambda i: (i, 0))
    merged = pl.pallas_call(
        _merge_kernel,
        grid=(L // tm,),
        in_specs=[ybs, ybs, ybs, gate(0), gate(1), gate(2),
                  pl.BlockSpec((N_BRANCH, HY_C, D_MODEL), lambda i: (0, 0, 0))],
        out_specs=row,
        out_shape=jax.ShapeDtypeStruct((L, D_MODEL), jnp.bfloat16),
        compiler_params=_cparams("parallel"),
        name="branch_merge",
    )(ys[0], ys[1], ys[2], z, z, z, _bf(w_branch))
    return pl.pallas_call(
        _proj_residual_kernel,
        grid=(L // tm,),
        in_specs=[row, pl.BlockSpec((D_MODEL, D_MODEL), lambda i: (0, 0)), row,
                  pl.BlockSpec((1, D_MODEL), lambda i: (0, 0))],
        out_specs=row,
        out_shape=jax.ShapeDtypeStruct((L, D_MODEL), jnp.float32),
        compiler_params=_cparams("parallel"),
        name="out_proj_residual",
    )(merged, _bf(w_out), h, gt)


def _pad_cols(a):
    pad = lambda n: jnp.zeros(a.shape[:-1] + (n,), a.dtype)
    return jnp.concatenate([a[..., :REC_COLS], pad(HY_OFF - REC_COLS), a[..., REC_COLS:]], axis=-1)


def _mixer(h, hc, u, uc, gt, gtc, p, need_ctx):
    L, Lc = u.shape[0], uc.shape[0]
    w_in = _pad_cols(_bf(p['w_in']))
    b_in = _pad_cols(p['b_in'])
    z = _matmul(u, w_in, b_in, tm=1024, tn=1024)
    ncol = Z_COLS if need_ctx else HY_OFF
    zc = _matmul(uc, w_in[:, :ncol], b_in[:ncol], tm=1024, tn=1024)
    zeros = lambda hd, dk, dv: jnp.zeros((hd, dv, dk), jnp.float32)
    yc_hg, hg_sf, hg_sb = _hgrn2(zc, Lc, p['lb'], p['hg_norm_w'],
                                 zeros(HG_HEADS, HG_DK, HG_DV), zeros(HG_HEADS, HG_DK, HG_DV))
    yc_gl, gl_sf, gl_sb = _gla(zc, Lc, p['gl_w_a2'], p['gl_b_a'], p['gl_norm_w'],
                               zeros(GL_HEADS, GL_DK, GL_DV), zeros(GL_HEADS, GL_DK, GL_DV))
    y_hg, _, _ = _hgrn2(z, L, p['lb'], p['hg_norm_w'], hg_sf, hg_sb)
    y_gl, _, _ = _gla(z, L, p['gl_w_a2'], p['gl_b_a'], p['gl_norm_w'], gl_sf, gl_sb)
    hy = (p['hy_conv_w'], p['hy_conv_b'], p['hy_f'], p['hy_skip'])
    y_hy = _hyena(z, L, *hy)
    h = _merge(z, L, (y_hy, y_hg, y_gl), p['w_branch'], p['w_out'], h, gt)
    if need_ctx:
        yc_hy = _hyena(zc, Lc, *hy)
        hc = _merge(zc, Lc, (yc_hy, yc_hg, yc_gl), p['w_branch'], p['w_out'], hc, gtc)
    return h, hc


def _ffn_kernel(blk_exp_ref, n_used_ref, x_ref, rw_ref, wg_ref, wu_ref, wd_ref, o_ref, wg_s, wu_s, wd_s):
    i = pl.program_id(0)
    new_expert = jnp.logical_or(i == 0, blk_exp_ref[i] != blk_exp_ref[jnp.maximum(i - 1, 0)])

    @pl.when(jnp.logical_and(i < n_used_ref[0], new_expert))
    def _():
        wg_s[...] = wg_ref[0].astype(jnp.bfloat16)
        wu_s[...] = wu_ref[0].astype(jnp.bfloat16)
        wd_s[...] = wd_ref[0].astype(jnp.bfloat16)

    @pl.when(i < n_used_ref[0])
    def _():
        x = x_ref[...]
        hg = jnp.dot(x, wg_s[...], preferred_element_type=jnp.float32)
        hu = jnp.dot(x, wu_s[...], preferred_element_type=jnp.float32)
        act = (hg * jax.nn.sigmoid(hg) * hu).astype(jnp.bfloat16)
        y = jnp.dot(act, wd_s[...], preferred_element_type=jnp.float32)
        o_ref[...] = (y * rw_ref[...]).astype(o_ref.dtype)

    @pl.when(i >= n_used_ref[0])
    def _():
        o_ref[...] = jnp.zeros_like(o_ref)


def _grouped_ffn(xg, row_w, blk_exp, n_used, layer, w_gate, w_up, w_down):
    p_len, d = xg.shape
    n_blk = p_len // MOE_BLOCK
    grid_spec = pltpu.PrefetchScalarGridSpec(
        num_scalar_prefetch=2,
        grid=(n_blk,),
        in_specs=[pl.BlockSpec((MOE_BLOCK, d), lambda i, be, nu: (i, 0)),
                  pl.BlockSpec((MOE_BLOCK, 1), lambda i, be, nu: (i, 0)),
                  pl.BlockSpec((None, 1, d, D_FF), lambda i, be, nu: (layer, be[i], 0, 0)),
                  pl.BlockSpec((None, 1, d, D_FF), lambda i, be, nu: (layer, be[i], 0, 0)),
                  pl.BlockSpec((None, 1, D_FF, d), lambda i, be, nu: (layer, be[i], 0, 0))],
        out_specs=pl.BlockSpec((MOE_BLOCK, d), lambda i, be, nu: (i, 0)),
        scratch_shapes=[pltpu.VMEM((d, D_FF), jnp.bfloat16), pltpu.VMEM((d, D_FF), jnp.bfloat16),
                        pltpu.VMEM((D_FF, d), jnp.bfloat16)],
    )
    return pl.pallas_call(
        _ffn_kernel,
        grid_spec=grid_spec,
        out_shape=jax.ShapeDtypeStruct((p_len, d), jnp.float32),
        compiler_params=_cparams("arbitrary"),
        name="moe_grouped_ffn",
    )(blk_exp, n_used, xg, row_w, w_gate, w_up, w_down)


ROUTER_COLS = 128


def _hier_moe(h, norm_w, shift, scale, gt, p):
    n, d = h.shape
    pad = ROUTER_COLS - N_GROUPS - N_EXPERTS
    w_r = _bf(jnp.concatenate([p['w_rg'], p['w_re'], jnp.zeros((d, pad), jnp.float32)], axis=1))
    b_r = jnp.concatenate([p['b_rg'], p['b_re'], jnp.zeros((pad,), jnp.float32)]).reshape(1, ROUTER_COLS)
    xb, logits = _norm_mod(h, norm_w, shift, scale, router=(w_r, b_r))
    lg = logits[:, :N_GROUPS]
    p_grp = jax.nn.softmax(lg, axis=-1)
    grp = jnp.argmax(p_grp, axis=-1).astype(jnp.int32)
    p_top = jnp.max(p_grp, axis=-1)
    le = logits[:, N_GROUPS:N_GROUPS + N_EXPERTS].reshape(n, N_GROUPS, EXP_PER_GROUP)
    le = jnp.take_along_axis(le, grp[:, None, None], axis=1)[:, 0]
    top_p, top_i = lax.top_k(jax.nn.softmax(le, axis=-1), TOP_K)
    weight = p_top[:, None] * top_p / jnp.sum(top_p, axis=-1, keepdims=True)
    expert = grp[:, None] * EXP_PER_GROUP + top_i.astype(jnp.int32)
    a = n * TOP_K
    e_flat = expert.reshape(a)
    onehot = (e_flat[:, None] == jnp.arange(N_EXPERTS, dtype=jnp.int32)[None, :]).astype(jnp.int32)
    rank = jnp.take_along_axis(jnp.cumsum(onehot, axis=0) - onehot, e_flat[:, None], axis=1)[:, 0]
    counts = jnp.sum(onehot, axis=0)
    padded = (counts + MOE_BLOCK - 1) // MOE_BLOCK * MOE_BLOCK
    pad_end = jnp.cumsum(padded)
    pad_off = pad_end - padded
    pos = pad_off[e_flat] + rank
    p_len = (a + N_EXPERTS * MOE_BLOCK + MOE_BLOCK - 1) // MOE_BLOCK * MOE_BLOCK
    n_blk = p_len // MOE_BLOCK
    tok_flat = jnp.arange(a, dtype=jnp.int32) // TOP_K
    buf_tok = jnp.zeros((p_len,), jnp.int32).at[pos].set(tok_flat)
    row_w = jnp.zeros((p_len,), jnp.float32).at[pos].set(weight.reshape(a)).reshape(p_len, 1)
    blk_start = jnp.arange(n_blk, dtype=jnp.int32) * MOE_BLOCK
    blk_exp = jnp.minimum(jnp.sum(pad_end[None, :] <= blk_start[:, None], axis=1), N_EXPERTS - 1).astype(jnp.int32)
    n_used = (pad_end[-1:] // MOE_BLOCK).astype(jnp.int32)
    xg = xb[buf_tok]
    y = _grouped_ffn(xg, row_w, blk_exp, n_used, p['layer'], p['w_gate'], p['w_up'], p['w_down'])
    return _moe_combine(y, pos, h, gt)


def _combine_kernel(pos_ref, y_hbm, h_ref, gt_ref, o_ref, buf, sem):
    i = pl.program_id(0)
    tokens = h_ref.shape[0]

    def row_copy(step, slot, r, k):
        src = pos_ref[(step * tokens + r) * TOP_K + k]
        return pltpu.make_async_copy(y_hbm.at[pl.ds(src, 1)], buf.at[slot, k, pl.ds(r, 1)], sem.at[slot])

    def issue(step, slot):
        def body(r, carry):
            for k in range(TOP_K):
                row_copy(step, slot, r, k).start()
            return carry

        lax.fori_loop(0, tokens, body, 0)

    @pl.when(i == 0)
    def _():
        issue(0, 0)

    @pl.when(i + 1 < pl.num_programs(0))
    def _():
        issue(i + 1, (i + 1) % 2)

    slot = i % 2
    pltpu.make_async_copy(buf.at[slot], buf.at[slot], sem.at[slot]).wait()
    rows = buf[slot]
    acc = rows[0]
    for k in range(1, TOP_K):
        acc = acc + rows[k]
    o_ref[...] = h_ref[...] + gt_ref[...] * acc


COMBINE_TOKENS = 128


def _moe_combine(y, pos, h, gt):
    n, d = h.shape
    tokens = min(COMBINE_TOKENS, n)
    grid_spec = pltpu.PrefetchScalarGridSpec(
        num_scalar_prefetch=1,
        grid=(n // tokens,),
        in_specs=[pl.BlockSpec(memory_space=pl.ANY),
                  pl.BlockSpec((tokens, d), lambda i, pos: (i, 0)),
                  pl.BlockSpec((1, d), lambda i, pos: (0, 0))],
        out_specs=pl.BlockSpec((tokens, d), lambda i, pos: (i, 0)),
        scratch_shapes=[pltpu.VMEM((2, TOP_K, tokens, d), jnp.float32), pltpu.SemaphoreType.DMA((2,))],
    )
    return pl.pallas_call(
        _combine_kernel,
        grid_spec=grid_spec,
        out_shape=jax.ShapeDtypeStruct((n, d), jnp.float32),
        compiler_params=_cparams("arbitrary"),
        name="moe_combine",
    )(pos, y, h, gt)


def _final_norm_kernel(x_ref, w_ref, o_ref):
    x = x_ref[...]
    o_ref[...] = x * lax.rsqrt(jnp.mean(x * x, axis=-1, keepdims=True) + NORM_EPS) * w_ref[...]


def _final_norm(x, w, tm=512):
    m, d = x.shape
    return pl.pallas_call(
        _final_norm_kernel,
        grid=(m // tm,),
        in_specs=[pl.BlockSpec((tm, d), lambda i: (i, 0)), pl.BlockSpec((1, d), lambda i: (0, 0))],
        out_specs=pl.BlockSpec((tm, d), lambda i: (i, 0)),
        out_shape=jax.ShapeDtypeStruct((m, d), jnp.float32),
        compiler_params=_cparams("parallel"),
        name="final_rmsnorm",
    )(x, w.reshape(1, d))


def kernel(x, c, ctx, c_ctx, w_mod, b_mod, norm_mix_w, norm_ffn_w, w_in, b_in, hy_conv_w, hy_conv_b, hy_f_w1, hy_f_b1, hy_f_w2, hy_f_b2, hy_f_w3, hy_f_b3, hy_f_freq, hy_skip, hg_lb_raw, hg_norm_w, gl_w_a2, gl_b_a, gl_norm_w, w_branch, w_out, w_rg, b_rg, w_re, b_re, w_gate, w_up, w_down, final_norm_w):
    assert x.shape[0] == 1
    depth = w_mod.shape[0]
    lb_all = jnp.cumsum(jax.nn.softmax(hg_lb_raw, axis=0), axis=0)
    lb_all = lb_all - lb_all[:1]
    h, hc = x[0], ctx[0]
    cc = jnp.concatenate([c, c_ctx[None, :]], axis=0)
    for l in range(depth):
        need_ctx = l < depth - 1
        mod = _matmul(_bf(jax.nn.silu(cc)), _bf(w_mod[l]), b_mod[l])
        sh1, sc1, gt1, sh2, sc2, gt2 = jnp.split(mod[0:1], 6, axis=-1)
        sh1c, sc1c, gt1c, sh2c, sc2c, gt2c = jnp.split(mod[1:2], 6, axis=-1)
        p = dict(w_in=w_in[l], b_in=b_in[l], hy_conv_w=hy_conv_w[l], hy_conv_b=hy_conv_b[l],
                 hy_f=(hy_f_w1[l], hy_f_b1[l], hy_f_w2[l], hy_f_b2[l], hy_f_w3[l], hy_f_b3[l], hy_f_freq[l]),
                 hy_skip=hy_skip[l], lb=lb_all[l], hg_norm_w=hg_norm_w[l], gl_w_a2=gl_w_a2[l], gl_b_a=gl_b_a[l],
                 gl_norm_w=gl_norm_w[l], w_branch=w_branch[l], w_out=w_out[l], w_rg=w_rg[l], b_rg=b_rg[l],
                 w_re=w_re[l], b_re=b_re[l], layer=l, w_gate=w_gate, w_up=w_up, w_down=w_down)
        u = _norm_mod(h, norm_mix_w[l], sh1, sc1)
        uc = _norm_mod(hc, norm_mix_w[l], sh1c, sc1c)
        h, hc = _mixer(h, hc, u, uc, gt1, gt1c, p, need_ctx)
        h = _hier_moe(h, norm_ffn_w[l], sh2, sc2, gt2, p)
        if need_ctx:
            hc = _hier_moe(hc, norm_ffn_w[l], sh2c, sc2c, gt2c, p)
    return _final_norm(h, final_norm_w)[None]
```

```python
import functools
import math

import jax
import jax.numpy as jnp
import numpy as np
from jax import lax
from jax.experimental import pallas as pl
from jax.experimental.pallas import tpu as pltpu

D_MODEL = 2048
NORM_EPS = 1e-6

HY_C = D_MODEL // 2
HY_EMB = 33
HY_BANDS = (HY_EMB - 1) // 2
HY_DECAY_TARGET = 1e-2
HY_FAST_PCT = 0.3
HY_SLOW_PCT = 1.5
HY_MOD_SHIFT = 0.05

HG_HEADS = 8
HG_DK = 128
HG_DV = 128
HG_K = HG_HEADS * HG_DK
HG_V = HG_HEADS * HG_DV

GL_HEADS = 4
GL_DK = 128
GL_DV = 256
GL_K = GL_HEADS * GL_DK
GL_V = GL_HEADS * GL_DV
GL_RANK = 16
GL_TAU = 16.0

N_BRANCH = 3
HG_COLS = 3 * HG_K + 2 * HG_V
GL_COLS = 2 * GL_K + 2 * GL_V + 2 * GL_RANK
REC_COLS = HG_COLS + GL_COLS
HY_COLS = 3 * HY_C
MERGE_COLS = N_BRANCH * D_MODEL

COL_TILE = 512
GL_Q_OFF = HG_COLS
GL_V_OFF = GL_Q_OFF + 2 * GL_K
GL_A_OFF = GL_V_OFF + 2 * GL_V
HY_OFF = -(-(GL_A_OFF + 2 * GL_RANK) // HY_C) * HY_C
MG_OFF = HY_OFF + HY_COLS
Z_COLS = MG_OFF + MERGE_COLS
assert GL_Q_OFF % GL_K == 0 and GL_V_OFF % GL_V == 0 and GL_A_OFF % 128 == 0 and Z_COLS % COL_TILE == 0
assert MG_OFF % D_MODEL == 0

N_GROUPS = 4
EXP_PER_GROUP = 8
N_EXPERTS = N_GROUPS * EXP_PER_GROUP
TOP_K = 2
D_FF = D_MODEL // 4
MOE_BLOCK = 256

SCAN_CHUNK = 128

VMEM_LIMIT_BYTES = 56 * 1024 * 1024


def _cparams(*sem):
    return pltpu.CompilerParams(dimension_semantics=sem, vmem_limit_bytes=VMEM_LIMIT_BYTES)


def _bf(a):
    return a.astype(jnp.bfloat16)


def _mm_kernel(x_ref, w_ref, b_ref, o_ref):
    acc = jnp.dot(x_ref[...], w_ref[...], preferred_element_type=jnp.float32) + b_ref[...]
    o_ref[...] = acc.astype(o_ref.dtype)


def _matmul(x, w, bias=None, tm=512, tn=COL_TILE, out_dtype=jnp.float32, n_cols=None):
    m, k = x.shape
    n = w.shape[1] if n_cols is None else n_cols
    assert n_cols is None or n_cols % tn == 0
    tm = min(tm, -(-m // 8) * 8)
    mp = -(-m // tm) * tm
    np_ = -(-n // tn) * tn
    if bias is None:
        bias = jnp.zeros((n,), jnp.float32)
    if mp != m:
        x = jnp.pad(x, ((0, mp - m), (0, 0)))
    if np_ != n:
        w = jnp.pad(w, ((0, 0), (0, np_ - n)))
        bias = jnp.pad(bias, (0, np_ - n))
    out = pl.pallas_call(
        _mm_kernel,
        grid=(mp // tm, np_ // tn),
        in_specs=[pl.BlockSpec((tm, k), lambda i, j: (i, 0)),
                  pl.BlockSpec((k, tn), lambda i, j: (0, j)),
                  pl.BlockSpec((1, tn), lambda i, j: (0, j))],
        out_specs=pl.BlockSpec((tm, tn), lambda i, j: (i, j)),
        out_shape=jax.ShapeDtypeStruct((mp, np_), out_dtype),
        compiler_params=_cparams("parallel", "arbitrary"),
        name="dense_matmul",
    )(x, w, bias.reshape(1, np_))
    if mp != m or np_ != n:
        out = out[:m, :n]
    return out


def _norm_mod_kernel(*refs, with_router):
    if with_router:
        h_ref, w_ref, sh_ref, sc_ref, wr_ref, br_ref, o_ref, lg_ref = refs
    else:
        h_ref, w_ref, sh_ref, sc_ref, o_ref = refs
    x = h_ref[...]
    y = x * lax.rsqrt(jnp.mean(x * x, axis=-1, keepdims=True) + NORM_EPS) * w_ref[...]
    u = (y * (1.0 + sc_ref[...]) + sh_ref[...]).astype(jnp.bfloat16)
    o_ref[...] = u
    if with_router:
        lg_ref[...] = jnp.dot(u, wr_ref[...], preferred_element_type=jnp.float32) + br_ref[...]


def _norm_mod(h, w, shift, scale, router=None):
    m, d = h.shape
    tm = min(512, m)
    row = pl.BlockSpec((tm, d), lambda i: (i, 0))
    vec = pl.BlockSpec((1, d), lambda i: (0, 0))
    args = [h, w.reshape(1, d), shift, scale]
    specs = [row, vec, vec, vec]
    out_shape = [jax.ShapeDtypeStruct((m, d), jnp.bfloat16)]
    out_specs = [row]
    if router is not None:
        args += list(router)
        specs += [pl.BlockSpec(router[0].shape, lambda i: (0, 0)), pl.BlockSpec(router[1].shape, lambda i: (0, 0))]
        out_shape.append(jax.ShapeDtypeStruct((m, router[0].shape[1]), jnp.float32))
        out_specs.append(pl.BlockSpec((tm, router[0].shape[1]), lambda i: (i, 0)))
    out = pl.pallas_call(
        functools.partial(_norm_mod_kernel, with_router=router is not None),
        grid=(m // tm,),
        in_specs=specs,
        out_specs=out_specs,
        out_shape=out_shape,
        compiler_params=_cparams("parallel"),
        name="norm_modulate",
    )(*args)
    return out if router is not None else out[0]


def _scan_masks(c, reverse):
    t = np.arange(c)
    ms = [np.eye(c, dtype=np.float32)]
    for lvl in range(int(math.log2(c))):
        upper = ((t >> lvl) & 1).astype(bool)
        same = (t[:, None] >> (lvl + 1)) == (t[None, :] >> (lvl + 1))
        m = same & upper[:, None] & (~upper)[None, :]
        ms.append((m.T if reverse else m).astype(np.float32))
    tri = t[None, :] >= t[:, None] if reverse else t[None, :] <= t[:, None]
    return jnp.asarray(np.stack(ms)), jnp.asarray(tri.astype(np.float32), dtype=jnp.bfloat16)


def _level_arg(cum, lvl, reverse):
    c = cum.shape[0]
    blk = 1 << lvl
    if blk >= 8:
        pieces = []
        for gs in range(0, c, 2 * blk):
            ref = cum[gs + blk:gs + blk + 1, :]
            pieces.append(ref - cum[gs:gs + blk, :])
            pieces.append(cum[gs + blk:gs + 2 * blk, :] - ref)
        arg = jnp.concatenate(pieces, axis=0)
    else:
        c3 = cum.reshape(c // 8, 8, cum.shape[1])
        sub = lax.broadcasted_iota(jnp.int32, c3.shape, 1)
        ref_row = ((sub >> lvl) | 1) << lvl
        ref = None
        for r in range(blk, 8, 2 * blk):
            cand = jnp.broadcast_to(c3[:, r:r + 1, :], c3.shape)
            ref = cand if ref is None else jnp.where(ref_row == r, cand, ref)
        upper = ((sub >> lvl) & 1) == 1
        arg = jnp.where(upper, c3 - ref, ref - c3).reshape(cum.shape)
    return -arg if reverse else arg


def _dot_nt(a, b):
    return lax.dot_general(a, b, (((1,), (1,)), ((), ())), preferred_element_type=jnp.float32)


def _dot_tn(a, b):
    return lax.dot_general(a, b, (((0,), (0,)), ((), ())), preferred_element_type=jnp.float32)


def _sigmoid_parts(z):
    e = jnp.exp(-jnp.abs(z))
    r = 1.0 / (1.0 + e)
    return jnp.minimum(z, 0.0) - jnp.log(1.0 + e), jnp.where(z >= 0.0, e * r, r)


def _scan_kernel(*refs, mode, reverse, final, heads, dk, dv, c):
    it = iter(refs)
    q_ref = next(it)
    k_ref = next(it)
    v_ref = next(it)
    if mode == "hg":
        lbp_ref = next(it)
    else:
        a_ref = next(it)
        wa_ref = next(it)
        ba_ref = next(it)
    s0_ref = next(it)
    masks_ref = next(it)
    tri_ref = next(it)
    if final:
        oprev_ref = next(it)
        gate_ref = next(it)
        nw_ref = next(it)
    o_ref = next(it)
    st_ref = next(it)

    @pl.when(pl.program_id(0) == 0)
    def _():
        st_ref[...] = s0_ref[...]

    if mode == "gl":
        la_all = jnp.dot(a_ref[...].astype(jnp.bfloat16), wa_ref[...],
                         preferred_element_type=jnp.float32) + ba_ref[...]
    tri = tri_ref[...]
    tot_row = 0 if reverse else c - 1
    n_lvl = int(math.log2(c))
    for h in range(heads):
        ks = slice(h * dk, (h + 1) * dk)
        vs = slice(h * dv, (h + 1) * dv)
        q = q_ref[:, ks]
        v = v_ref[:, vs].astype(jnp.bfloat16)
        if mode == "hg":
            log_sig, sig_neg = _sigmoid_parts(k_ref[:, ks])
            la = lbp_ref[0:1, ks]
            lbb = lbp_ref[1:2, ks] + log_sig
            g = jnp.maximum(la, lbb) + jnp.log(1.0 + jnp.exp(-jnp.abs(la - lbb)))
            k = lbp_ref[2:3, ks] * sig_neg
            q = q * jax.nn.sigmoid(q)
        else:
            g = _sigmoid_parts(la_all[:, ks])[0] * (1.0 / GL_TAU)
            k = k_ref[:, ks]
            q = q * (dk ** -0.5)
        g1 = g.astype(jnp.bfloat16)
        r1 = g - g1.astype(jnp.float32)
        g2 = r1.astype(jnp.bfloat16)
        g3 = (r1 - g2.astype(jnp.float32)).astype(jnp.bfloat16)
        cum = (jnp.dot(tri, g1, preferred_element_type=jnp.float32)
               + jnp.dot(tri, g2, preferred_element_type=jnp.float32)
               + jnp.dot(tri, g3, preferred_element_type=jnp.float32))
        tot = cum[tot_row:tot_row + 1, :]
        st = st_ref[h]
        o = _dot_nt((q * jnp.exp(cum)).astype(jnp.bfloat16), st.astype(jnp.bfloat16))
        kt = (k * jnp.exp(tot - cum)).astype(jnp.bfloat16)
        st_ref[h] = st * jnp.exp(tot) + _dot_tn(v, kt)
        scores = masks_ref[0] * _dot_nt(q.astype(jnp.bfloat16), k.astype(jnp.bfloat16))
        for lvl in range(n_lvl):
            e = jnp.exp(_level_arg(cum, lvl, reverse))
            scores = scores + masks_ref[1 + lvl] * _dot_nt((q * e).astype(jnp.bfloat16),
                                                           (k * e).astype(jnp.bfloat16))
        o = o + jnp.dot(scores.astype(jnp.bfloat16), v, preferred_element_type=jnp.float32)
        if final:
            o = o + oprev_ref[:, vs]
            y = o * lax.rsqrt(jnp.mean(o * o, axis=-1, keepdims=True) + NORM_EPS) * nw_ref[...]
            gt = gate_ref[:, vs]
            act = jax.nn.sigmoid(gt) if mode == "hg" else gt * jax.nn.sigmoid(gt)
            o_ref[:, vs] = (y * act).astype(o_ref.dtype)
        else:
            o_ref[:, vs] = o


def _scan_pass(mode, reverse, final, L, srcs, s0, params, final_srcs=(), norm_w=None):
    heads, dk, dv = (HG_HEADS, HG_DK, HG_DV) if mode == "hg" else (GL_HEADS, GL_DK, GL_DV)
    c = min(SCAN_CHUNK, L)
    nb = L // c
    row = (lambda i: nb - 1 - i) if reverse else (lambda i: i)
    masks, tri = _scan_masks(c, reverse)

    def const(shape):
        return pl.BlockSpec(shape, lambda i: (0,) * len(shape))

    def rowblock(width, cb):
        return pl.BlockSpec((c, width), lambda i: (row(i), cb))

    args = [a for a, _, _ in srcs] + list(params) + [s0, masks, tri]
    specs = ([rowblock(w, cb) for _, w, cb in srcs] + [const(p.shape) for p in params]
             + [const(s0.shape), const(masks.shape), const(tri.shape)])
    if final:
        args += [a for a, _, _ in final_srcs] + [norm_w]
        specs += [rowblock(w, cb) for _, w, cb in final_srcs] + [const(norm_w.shape)]
    return pl.pallas_call(
        functools.partial(_scan_kernel, mode=mode, reverse=reverse, final=final, heads=heads, dk=dk, dv=dv, c=c),
        grid=(nb,),
        in_specs=specs,
        out_specs=[pl.BlockSpec((c, heads * dv), lambda i: (row(i), 0)), const((heads, dv, dk))],
        out_shape=[jax.ShapeDtypeStruct((L, heads * dv), jnp.bfloat16 if final else jnp.float32),
                   jax.ShapeDtypeStruct((heads, dv, dk), jnp.float32)],
        compiler_params=_cparams("arbitrary"),
        name=f"scan_{mode}_{'bwd' if reverse else 'fwd'}",
    )(*args)


def _hgrn2(z, L, lb, norm_w, s0_f, s0_b):
    lbp = lambda d: jnp.stack([jnp.log(lb[d]), jnp.log1p(-lb[d]), 1.0 - lb[d]])
    w = HG_K
    o_b, s_b = _scan_pass("hg", True, False, L, [(z, w, 0), (z, w, 2), (z, w, 3)], s0_b, [lbp(1)])
    y, s_f = _scan_pass("hg", False, True, L, [(z, w, 0), (z, w, 1), (z, w, 3)], s0_f, [lbp(0)],
                        final_srcs=[(o_b, HG_V, 0), (z, HG_V, 4)], norm_w=norm_w.reshape(1, HG_DV))
    return y, s_f, s_b


def _gla(z, L, w_a2, b_a, norm_w, s0_f, s0_b):
    def gate_params(d):
        wa = jnp.zeros((128, GL_K), jnp.float32).at[d * GL_RANK:(d + 1) * GL_RANK].set(w_a2[d])
        return [_bf(wa), b_a[d].reshape(1, GL_K)]

    srcs = [(z, GL_K, GL_Q_OFF // GL_K), (z, GL_K, GL_Q_OFF // GL_K + 1), (z, GL_V, GL_V_OFF // GL_V),
            (z, 128, GL_A_OFF // 128)]
    o_b, s_b = _scan_pass("gl", True, False, L, srcs, s0_b, gate_params(1))
    y, s_f = _scan_pass("gl", False, True, L, srcs, s0_f, gate_params(0),
                        final_srcs=[(o_b, GL_V, 0), (z, GL_V, GL_V_OFF // GL_V + 1)], norm_w=norm_w.reshape(1, GL_DV))
    return y, s_f, s_b


def _hyena_filters(L, w1, b1, w2, b2, w3, b3, freq):
    t = jnp.linspace(0.0, 1.0, L, dtype=jnp.float32)[:, None]
    ang = 2.0 * math.pi * jnp.arange(L, dtype=jnp.float32)[:, None] / L
    bands = jnp.linspace(1e-4, HY_BANDS - 1, HY_BANDS, dtype=jnp.float32)[None, :]
    feats = jnp.concatenate([t, jnp.cos(bands * ang), -jnp.sin(bands * ang)], axis=-1)
    h = jnp.sin(freq * (feats @ w1 + b1))
    h = jnp.sin(freq * (h @ w2 + b2))
    h = h @ w3 + b3
    deltas = jnp.abs(jnp.linspace(math.log(HY_DECAY_TARGET) / HY_SLOW_PCT, math.log(HY_DECAY_TARGET) / HY_FAST_PCT,
                                  HY_C, dtype=jnp.float32))
    decay = jnp.exp(-t * deltas)
    h = h * (jnp.tile(decay, (1, 2)) + HY_MOD_SHIFT)
    inorm = 1.0 / jnp.sum(jnp.abs(h), axis=0)
    return h, inorm.reshape(2, HY_C)


def _hy_pre_kernel(x0_ref, x1_ref, v_ref, x0p_ref, x1p_ref, vp_ref, x0n_ref, x1n_ref, vn_ref, w_ref, b_ref,
                   vo_ref, x0o_ref):
    i = pl.program_id(0)
    first = i == 0
    last = i == pl.num_programs(0) - 1
    tm = x0_ref.shape[0]
    row = lax.broadcasted_iota(jnp.int32, x0_ref.shape, 0)

    def conv(x_ref, p_ref, n_ref, g):
        x = x_ref[...]
        cs = slice(g * HY_C, (g + 1) * HY_C)
        prev_row = jnp.where(first, 0.0, p_ref[7:8, :])
        next_row = jnp.where(last, 0.0, n_ref[0:1, :])
        xp = jnp.where(row == 0, prev_row, pltpu.roll(x, 1, 0))
        xn = jnp.where(row == tm - 1, next_row, pltpu.roll(x, tm - 1, 0))
        return w_ref[0:1, cs] * xp + w_ref[1:2, cs] * x + w_ref[2:3, cs] * xn + b_ref[0:1, cs]

    x0 = conv(x0_ref, x0p_ref, x0n_ref, 0)
    x1 = conv(x1_ref, x1p_ref, x1n_ref, 1)
    v = conv(v_ref, vp_ref, vn_ref, 2)
    vo_ref[...] = (v * x1).astype(vo_ref.dtype)
    x0o_ref[...] = x0.astype(x0o_ref.dtype)


def _hy_pre(z, L, conv_w, conv_b):
    tm = min(256, L)
    nb8 = L // 8
    cb = HY_OFF // HY_C
    main = lambda g: pl.BlockSpec((tm, HY_C), lambda i: (i, cb + g))
    prev = lambda g: pl.BlockSpec((8, HY_C), lambda i: (jnp.maximum(i * (tm // 8) - 1, 0), cb + g))
    nxt = lambda g: pl.BlockSpec((8, HY_C), lambda i: (jnp.minimum((i + 1) * (tm // 8), nb8 - 1), cb + g))
    const = lambda a: pl.BlockSpec(a.shape, lambda i: (0, 0))
    cbias = conv_b.reshape(1, HY_COLS)
    return pl.pallas_call(
        _hy_pre_kernel,
        grid=(L // tm,),
        in_specs=[main(0), main(1), main(2), prev(0), prev(1), prev(2), nxt(0), nxt(1), nxt(2),
                  const(conv_w), const(cbias)],
        out_specs=[pl.BlockSpec((tm, HY_C), lambda i: (i, 0))] * 2,
        out_shape=[jax.ShapeDtypeStruct((L, HY_C), jnp.float32), jax.ShapeDtypeStruct((L, HY_C), jnp.bfloat16)],
        compiler_params=_cparams("parallel"),
        name="hyena_short_conv",
    )(z, z, z, z, z, z, z, z, z, conv_w, cbias)


HY_N1 = 256
HY_TWO_STAGE_MIN_L = 1024


def _dft_outer_table(n1, cols):
    ang = -2.0 * np.pi * np.outer(np.arange(n1 // 2) + 0.5, np.arange(cols)) / n1
    return jnp.asarray(np.concatenate([np.cos(ang), np.sin(ang)], axis=0), jnp.bfloat16)


def _dft_inner_table(n1, n2):
    j2 = np.arange(n2)
    f_ang = -2.0 * np.pi * np.outer(np.arange(n2), j2) / n2
    tw_ang = -2.0 * np.pi * np.outer(np.arange(n1 // 2) + 0.5, j2) / (n1 * n2)
    fr, fi = jnp.asarray(np.cos(f_ang), jnp.float32), jnp.asarray(np.sin(f_ang), jnp.float32)
    twr, twi = jnp.asarray(np.cos(tw_ang), jnp.float32), jnp.asarray(np.sin(tw_ang), jnp.float32)
    mr = fr[None] * twr[:, None, :] - fi[None] * twi[:, None, :]
    mi = fr[None] * twi[:, None, :] + fi[None] * twr[:, None, :]
    return _bf(jnp.concatenate([jnp.concatenate([mr, -mi], axis=2), jnp.concatenate([mi, mr], axis=2)], axis=1))


def _spectral_product(xv, xh, inorm, half):
    inf, inb = inorm[0:1, :], inorm[1:2, :]
    gr = xh[:half, :HY_C] * inf + xh[:half, HY_C:] * inb
    gi = xh[half:, :HY_C] * inf - xh[half:, HY_C:] * inb
    xr, xi = xv[:half], xv[half:]
    return jnp.concatenate([xr * gr - xi * gi, xr * gi + xi * gr], axis=0).astype(jnp.bfloat16)


def _hy_spec_kernel(r_ref, avr_ref, avi_ref, ahr_ref, ahi_ref, inorm_ref, br_ref, bi_ref):
    r = r_ref[0]
    n2 = avr_ref.shape[1]
    xv = jnp.dot(r, jnp.concatenate([avr_ref[0], avi_ref[0]], axis=0), preferred_element_type=jnp.float32)
    xh = jnp.dot(r, jnp.concatenate([ahr_ref[0], ahi_ref[0]], axis=0), preferred_element_type=jnp.float32)
    b = _dot_tn(r, _spectral_product(xv, xh, inorm_ref[...], n2))
    br_ref[0] = b[:n2].astype(br_ref.dtype)
    bi_ref[0] = b[n2:].astype(bi_ref.dtype)


def _hy_spec(r, av, ah, inorm, n1, n2):
    av3 = av.reshape(n1, n2, HY_C)
    ah3 = ah.reshape(n1, n2, 2 * HY_C)
    h1 = n1 // 2
    out = jax.ShapeDtypeStruct((h1, n2, HY_C), jnp.bfloat16)
    return pl.pallas_call(
        _hy_spec_kernel,
        grid=(h1,),
        in_specs=[pl.BlockSpec((1, 2 * n2, 2 * n2), lambda k: (k, 0, 0)),
                  pl.BlockSpec((1, n2, HY_C), lambda k: (k, 0, 0)),
                  pl.BlockSpec((1, n2, HY_C), lambda k: (k + h1, 0, 0)),
                  pl.BlockSpec((1, n2, 2 * HY_C), lambda k: (k, 0, 0)),
                  pl.BlockSpec((1, n2, 2 * HY_C), lambda k: (k + h1, 0, 0)),
                  pl.BlockSpec((2, HY_C), lambda k: (0, 0))],
        out_specs=[pl.BlockSpec((1, n2, HY_C), lambda k: (k, 0, 0))] * 2,
        out_shape=[out, out],
        compiler_params=_cparams("parallel"),
        name="hyena_spectral",
    )(r, av3, av3, ah3, ah3, inorm)


def _hy_spec_direct_kernel(xv_ref, xh_ref, inorm_ref, yr_ref, yi_ref):
    half = yr_ref.shape[0]
    y = _spectral_product(xv_ref[...].astype(jnp.float32), xh_ref[...].astype(jnp.float32), inorm_ref[...], half)
    yr_ref[...] = y[:half]
    yi_ref[...] = y[half:]


def _hy_spec_direct(xv, xh, inorm, L):
    full = lambda a: pl.BlockSpec(a.shape, lambda i: (0, 0))
    out = jax.ShapeDtypeStruct((L, HY_C), jnp.bfloat16)
    return pl.pallas_call(
        _hy_spec_direct_kernel,
        grid=(1,),
        in_specs=[full(xv), full(xh), full(inorm)],
        out_specs=[pl.BlockSpec((L, HY_C), lambda i: (0, 0))] * 2,
        out_shape=[out, out],
        compiler_params=_cparams("arbitrary"),
        name="hyena_spectral_direct",
    )(xv, xh, inorm)


def _hy_post_kernel(tr_ref, ti_ref, br_ref, bi_ref, v_ref, x0_ref, skip_ref, o_ref, *, scale):
    acc = (jnp.dot(tr_ref[...], br_ref[...], preferred_element_type=jnp.float32)
           + jnp.dot(ti_ref[...], bi_ref[...], preferred_element_type=jnp.float32))
    y = (acc * scale + v_ref[...].astype(jnp.float32) * skip_ref[...]) * x0_ref[...].astype(jnp.float32)
    o_ref[...] = y.astype(o_ref.dtype)


def _hy_post(t_fwd, b_r, b_i, v, x0, skip, L, h1, n2):
    ncol = n2 * HY_C
    tn = min(4096, ncol)
    tr_t = t_fwd[:h1].T
    ti_t = t_fwd[h1:].T
    skip_t = jnp.tile(skip, tn // HY_C).reshape(1, tn)
    col = lambda rows: pl.BlockSpec((rows, tn), lambda j: (0, j))
    rows_out = tr_t.shape[0]
    y = pl.pallas_call(
        functools.partial(_hy_post_kernel, scale=1.0 / L),
        grid=(ncol // tn,),
        in_specs=[pl.BlockSpec(tr_t.shape, lambda j: (0, 0)), pl.BlockSpec(ti_t.shape, lambda j: (0, 0)),
                  col(h1), col(h1), col(rows_out), col(rows_out), pl.BlockSpec((1, tn), lambda j: (0, 0))],
        out_specs=col(rows_out),
        out_shape=jax.ShapeDtypeStruct((rows_out, ncol), jnp.bfloat16),
        compiler_params=_cparams("parallel"),
        name="hyena_inverse",
    )(tr_t, ti_t, b_r.reshape(h1, ncol), b_i.reshape(h1, ncol), v.reshape(rows_out, ncol),
      x0.reshape(rows_out, ncol), skip_t)
    return y.reshape(L, HY_C)


SUBLANE = 8
HY_COL_TILE = 512


def _dft_outer_kron(n1):
    h1 = n1 // 2
    ang = -2.0 * np.pi * np.outer(np.arange(h1) + 0.5, np.arange(h1)) / n1
    eye = np.eye(SUBLANE)
    t_r, t_i = np.cos(ang), np.sin(ang)
    fwd = np.kron(np.concatenate([t_r, t_i], axis=0), eye)
    inv = np.concatenate([np.kron(t_r.T, eye), np.kron(t_i.T, eye)], axis=1)
    return jnp.asarray(fwd, jnp.bfloat16), jnp.asarray(inv, jnp.bfloat16)


def _hy_outer_fwd_kernel(t_ref, x_ref, o_ref):
    x = x_ref[...]
    rows_in, rows_out = x.shape[0], t_ref.shape[0] // SUBLANE
    cw = x.shape[2]
    parts = []
    for s in range(0, x.shape[1], SUBLANE):
        xs = x[:, s:s + SUBLANE, :].reshape(rows_in * SUBLANE, cw).astype(jnp.bfloat16)
        r = jnp.dot(t_ref[...], xs, preferred_element_type=jnp.float32)
        parts.append(r.reshape(rows_out, SUBLANE, cw))
    o_ref[...] = jnp.concatenate(parts, axis=1).astype(o_ref.dtype)


def _hy_outer_fwd(t_kron, x3):
    h1, n2, w = x3.shape
    n1 = 2 * h1
    blk = 2 * SUBLANE
    return pl.pallas_call(
        _hy_outer_fwd_kernel,
        grid=(n2 // blk, w // HY_COL_TILE),
        in_specs=[pl.BlockSpec(t_kron.shape, lambda j, cc: (0, 0)),
                  pl.BlockSpec((h1, blk, HY_COL_TILE), lambda j, cc: (0, j, cc))],
        out_specs=pl.BlockSpec((n1, blk, HY_COL_TILE), lambda j, cc: (0, j, cc)),
        out_shape=jax.ShapeDtypeStruct((n1, n2, w), jnp.bfloat16),
        compiler_params=_cparams("parallel", "parallel"),
        name="hyena_outer_dft",
    )(t_kron, x3)


def _hy_outer_inv_kernel(t_ref, br_ref, bi_ref, v_ref, x0_ref, skip_ref, o_ref, *, scale):
    br = br_ref[...].astype(jnp.float32)
    bi = bi_ref[...].astype(jnp.float32)
    v = v_ref[...]
    x0 = x0_ref[...].astype(jnp.float32)
    h1, _, cw = br.shape
    parts = []
    for s in range(0, br.shape[1], SUBLANE):
        sl = slice(s, s + SUBLANE)
        b = jnp.concatenate([br[:, sl, :].reshape(h1 * SUBLANE, cw), bi[:, sl, :].reshape(h1 * SUBLANE, cw)], axis=0)
        r = jnp.dot(t_ref[...], b.astype(jnp.bfloat16), preferred_element_type=jnp.float32)
        parts.append((r.reshape(h1, SUBLANE, cw) * scale + v[:, sl, :] * skip_ref[...]) * x0[:, sl, :])
    o_ref[...] = jnp.concatenate(parts, axis=1).astype(o_ref.dtype)


def _hy_outer_inv(t_kron_inv, b_r, b_i, v3, x03, skip, L):
    h1, n2, w = b_r.shape
    blk = 2 * SUBLANE
    tile = pl.BlockSpec((h1, blk, HY_COL_TILE), lambda j, cc: (0, j, cc))
    return pl.pallas_call(
        functools.partial(_hy_outer_inv_kernel, scale=1.0 / L),
        grid=(n2 // blk, w // HY_COL_TILE),
        in_specs=[pl.BlockSpec(t_kron_inv.shape, lambda j, cc: (0, 0)), tile, tile, tile, tile,
                  pl.BlockSpec((1, 1, HY_COL_TILE), lambda j, cc: (0, 0, cc))],
        out_specs=tile,
        out_shape=jax.ShapeDtypeStruct((h1, n2, w), jnp.bfloat16),
        compiler_params=_cparams("parallel", "parallel"),
        name="hyena_outer_idft",
    )(t_kron_inv, b_r, b_i, v3, x03, skip.reshape(1, 1, w))


def _hyena(z, L, conv_w, conv_b, fparams, skip):
    v, x0 = _hy_pre(z, L, conv_w, conv_b)
    hfil, inorm = _hyena_filters(L, *fparams)
    if L >= HY_TWO_STAGE_MIN_L:
        n1 = HY_N1
        n2 = 2 * L // n1
        h1 = n1 // 2
        assert n2 % (2 * SUBLANE) == 0
        t_kron, t_kron_inv = _dft_outer_kron(n1)
        v3 = v.reshape(h1, n2, HY_C)
        av = _hy_outer_fwd(t_kron, v3)
        ah = _hy_outer_fwd(t_kron, hfil.reshape(h1, n2, 2 * HY_C))
        b_r, b_i = _hy_spec(_dft_inner_table(n1, n2), av, ah, inorm, n1, n2)
        return _hy_outer_inv(t_kron_inv, b_r, b_i, v3, x0.reshape(h1, n2, HY_C), skip, L).reshape(L, HY_C)
    t_fwd = _dft_outer_table(2 * L, L)
    xv = _matmul(t_fwd, _bf(v), tm=2 * L, tn=HY_C, out_dtype=jnp.bfloat16)
    xh = _matmul(t_fwd, _bf(hfil), tm=2 * L, tn=HY_C, out_dtype=jnp.bfloat16)
    y_r, y_i = _hy_spec_direct(xv, xh, inorm, L)
    return _hy_post(t_fwd, y_r, y_i, v, x0, skip, L, L, 1)


def _merge_kernel(yh_ref, yg_ref, yl_ref, gate_h_ref, gate_g_ref, gate_l_ref, wb_ref, o_ref):
    acc = None
    for br, (y_ref, g_ref) in enumerate(((yh_ref, gate_h_ref), (yg_ref, gate_g_ref), (yl_ref, gate_l_ref))):
        t = jnp.dot(y_ref[...], wb_ref[br], preferred_element_type=jnp.float32) * jax.nn.sigmoid(g_ref[...])
        acc = t if acc is None else acc + t
    o_ref[...] = acc.astype(o_ref.dtype)


def _proj_residual_kernel(m_ref, w_ref, h_ref, gt_ref, o_ref):
    o_ref[...] = h_ref[...] + gt_ref[...] * jnp.dot(m_ref[...], w_ref[...], preferred_element_type=jnp.float32)


def _merge(z, L, ys, w_branch, w_out, h, gt):
    tm = min(256, L)
    gb = MG_OFF // D_MODEL
    ybs = pl.BlockSpec((tm, HY_C), lambda i: (i, 0))
    gate = lambda br: pl.BlockSpec((tm, D_MODEL), lambda i: (i, gb + br))
    row = pl.BlockSpec((tm, D_MODEL), lambda i: (i, 0))
    merged = pl.pallas_call(
        _merge_kernel,
        grid=(L // tm,),
        in_specs=[ybs, ybs, ybs, gate(0), gate(1), gate(2),
                  pl.BlockSpec((N_BRANCH, HY_C, D_MODEL), lambda i: (0, 0, 0))],
        out_specs=row,
        out_shape=jax.ShapeDtypeStruct((L, D_MODEL), jnp.bfloat16),
        compiler_params=_cparams("parallel"),
        name="branch_merge",
    )(ys[0], ys[1], ys[2], z, z, z, _bf(w_branch))
    return pl.pallas_call(
        _proj_residual_kernel,
        grid=(L // tm,),
        in_specs=[row, pl.BlockSpec((D_MODEL, D_MODEL), lambda i: (0, 0)), row,
                  pl.BlockSpec((1, D_MODEL), lambda i: (0, 0))],
        out_specs=row,
        out_shape=jax.ShapeDtypeStruct((L, D_MODEL), jnp.float32),
        compiler_params=_cparams("parallel"),
        name="out_proj_residual",
    )(merged, _bf(w_out), h, gt)


def _pad_cols(a):
    pad = lambda n: jnp.zeros(a.shape[:-1] + (n,), a.dtype)
    return jnp.concatenate([a[..., :REC_COLS], pad(HY_OFF - REC_COLS), a[..., REC_COLS:]], axis=-1)


W_TILE = 1024
W_SHIFT = HY_OFF - REC_COLS
W_LANE_OFF = (-W_SHIFT) % 128
assert 0 < W_SHIFT <= W_TILE and HY_OFF % W_TILE == 0 and Z_COLS % W_TILE == 0


def _w_in_prep_kernel(a_ref, b_ref, o_ref):
    j = pl.program_id(0)
    shifted = j >= HY_OFF // W_TILE

    @pl.when(jnp.logical_not(shifted))
    def _():
        o_ref[...] = a_ref[...].astype(o_ref.dtype)

    @pl.when(shifted)
    def _():
        window = jnp.concatenate([a_ref[...], b_ref[...]], axis=1)
        o_ref[...] = window[:, W_LANE_OFF:W_LANE_OFF + W_TILE].astype(o_ref.dtype)


def _w_in_prep(w_in, layer):
    _, k, n = w_in.shape
    first_shifted = HY_OFF // W_TILE
    a_idx = lambda j: jnp.where(j < first_shifted, j, j - (W_SHIFT + W_TILE - 1) // W_TILE)
    return pl.pallas_call(
        _w_in_prep_kernel,
        grid=(Z_COLS // W_TILE,),
        in_specs=[pl.BlockSpec((None, k, W_TILE), lambda j: (layer, 0, a_idx(j))),
                  pl.BlockSpec((None, k, 128), lambda j: (layer, 0, (a_idx(j) + 1) * (W_TILE // 128)))],
        out_specs=pl.BlockSpec((k, W_TILE), lambda j: (0, j)),
        out_shape=jax.ShapeDtypeStruct((k, Z_COLS), jnp.bfloat16),
        compiler_params=_cparams("parallel"),
        name="w_in_relayout",
    )(w_in, w_in)


def _mixer(h, hc, u, uc, gt, gtc, p, need_ctx):
    L, Lc = u.shape[0], uc.shape[0]
    w_in = _w_in_prep(p['w_in'], p['layer'])
    b_in = _pad_cols(p['b_in'])
    z = _matmul(u, w_in, b_in, tm=1024, tn=W_TILE)
    ncol = Z_COLS if need_ctx else HY_OFF
    zc = _matmul(uc, w_in, b_in[:ncol], tm=1024, tn=W_TILE, n_cols=ncol)
    zeros = lambda hd, dk, dv: jnp.zeros((hd, dv, dk), jnp.float32)
    yc_hg, hg_sf, hg_sb = _hgrn2(zc, Lc, p['lb'], p['hg_norm_w'],
                                 zeros(HG_HEADS, HG_DK, HG_DV), zeros(HG_HEADS, HG_DK, HG_DV))
    yc_gl, gl_sf, gl_sb = _gla(zc, Lc, p['gl_w_a2'], p['gl_b_a'], p['gl_norm_w'],
                               zeros(GL_HEADS, GL_DK, GL_DV), zeros(GL_HEADS, GL_DK, GL_DV))
    y_hg, _, _ = _hgrn2(z, L, p['lb'], p['hg_norm_w'], hg_sf, hg_sb)
    y_gl, _, _ = _gla(z, L, p['gl_w_a2'], p['gl_b_a'], p['gl_norm_w'], gl_sf, gl_sb)
    hy = (p['hy_conv_w'], p['hy_conv_b'], p['hy_f'], p['hy_skip'])
    y_hy = _hyena(z, L, *hy)
    h = _merge(z, L, (y_hy, y_hg, y_gl), p['w_branch'], p['w_out'], h, gt)
    if need_ctx:
        yc_hy = _hyena(zc, Lc, *hy)
        hc = _merge(zc, Lc, (yc_hy, yc_hg, yc_gl), p['w_branch'], p['w_out'], hc, gtc)
    return h, hc


def _ffn_kernel(blk_exp_ref, n_used_ref, x_ref, wg_ref, wu_ref, wd_ref, o_ref, wg_s, wu_s, wd_s):
    i = pl.program_id(0)
    new_expert = jnp.logical_or(i == 0, blk_exp_ref[i] != blk_exp_ref[jnp.maximum(i - 1, 0)])

    @pl.when(jnp.logical_and(i < n_used_ref[0], new_expert))
    def _():
        wg_s[...] = wg_ref[0].astype(jnp.bfloat16)
        wu_s[...] = wu_ref[0].astype(jnp.bfloat16)
        wd_s[...] = wd_ref[0].astype(jnp.bfloat16)

    @pl.when(i < n_used_ref[0])
    def _():
        x = x_ref[...]
        hg = jnp.dot(x, wg_s[...], preferred_element_type=jnp.float32)
        hu = jnp.dot(x, wu_s[...], preferred_element_type=jnp.float32)
        act = (hg * jax.nn.sigmoid(hg) * hu).astype(jnp.bfloat16)
        o_ref[...] = jnp.dot(act, wd_s[...], preferred_element_type=jnp.float32)

    @pl.when(i >= n_used_ref[0])
    def _():
        o_ref[...] = jnp.zeros_like(o_ref)


def _grouped_ffn(xg, blk_exp, n_used, layer, w_gate, w_up, w_down):
    p_len, d = xg.shape
    n_blk = p_len // MOE_BLOCK
    grid_spec = pltpu.PrefetchScalarGridSpec(
        num_scalar_prefetch=2,
        grid=(n_blk,),
        in_specs=[pl.BlockSpec((MOE_BLOCK, d), lambda i, be, nu: (i, 0)),
                  pl.BlockSpec((None, 1, d, D_FF), lambda i, be, nu: (layer, be[i], 0, 0)),
                  pl.BlockSpec((None, 1, d, D_FF), lambda i, be, nu: (layer, be[i], 0, 0)),
                  pl.BlockSpec((None, 1, D_FF, d), lambda i, be, nu: (layer, be[i], 0, 0))],
        out_specs=pl.BlockSpec((MOE_BLOCK, d), lambda i, be, nu: (i, 0)),
        scratch_shapes=[pltpu.VMEM((d, D_FF), jnp.bfloat16), pltpu.VMEM((d, D_FF), jnp.bfloat16),
                        pltpu.VMEM((D_FF, d), jnp.bfloat16)],
    )
    return pl.pallas_call(
        _ffn_kernel,
        grid_spec=grid_spec,
        out_shape=jax.ShapeDtypeStruct((p_len, d), jnp.float32),
        compiler_params=_cparams("arbitrary"),
        name="moe_grouped_ffn",
    )(blk_exp, n_used, xg, w_gate, w_up, w_down)


ROUTER_COLS = 128


def _hier_moe(h, norm_w, shift, scale, gt, p):
    n, d = h.shape
    pad = ROUTER_COLS - N_GROUPS - N_EXPERTS
    w_r = _bf(jnp.concatenate([p['w_rg'], p['w_re'], jnp.zeros((d, pad), jnp.float32)], axis=1))
    b_r = jnp.concatenate([p['b_rg'], p['b_re'], jnp.zeros((pad,), jnp.float32)]).reshape(1, ROUTER_COLS)
    xb, logits = _norm_mod(h, norm_w, shift, scale, router=(w_r, b_r))
    lg = logits[:, :N_GROUPS]
    p_grp = jax.nn.softmax(lg, axis=-1)
    grp = jnp.argmax(p_grp, axis=-1).astype(jnp.int32)
    p_top = jnp.max(p_grp, axis=-1)
    le = logits[:, N_GROUPS:N_GROUPS + N_EXPERTS].reshape(n, N_GROUPS, EXP_PER_GROUP)
    le = jnp.take_along_axis(le, grp[:, None, None], axis=1)[:, 0]
    top_p, top_i = lax.top_k(jax.nn.softmax(le, axis=-1), TOP_K)
    weight = p_top[:, None] * top_p / jnp.sum(top_p, axis=-1, keepdims=True)
    expert = grp[:, None] * EXP_PER_GROUP + top_i.astype(jnp.int32)
    a = n * TOP_K
    e_flat = expert.reshape(a)
    onehot = (e_flat[:, None] == jnp.arange(N_EXPERTS, dtype=jnp.int32)[None, :]).astype(jnp.int32)
    rank = jnp.take_along_axis(jnp.cumsum(onehot, axis=0) - onehot, e_flat[:, None], axis=1)[:, 0]
    counts = jnp.sum(onehot, axis=0)
    padded = (counts + MOE_BLOCK - 1) // MOE_BLOCK * MOE_BLOCK
    pad_end = jnp.cumsum(padded)
    pad_off = pad_end - padded
    pos = pad_off[e_flat] + rank
    p_len = (a + N_EXPERTS * MOE_BLOCK + MOE_BLOCK - 1) // MOE_BLOCK * MOE_BLOCK
    n_blk = p_len // MOE_BLOCK
    tok_flat = jnp.arange(a, dtype=jnp.int32) // TOP_K
    buf_tok = jnp.zeros((p_len,), jnp.int32).at[pos].set(tok_flat)
    blk_start = jnp.arange(n_blk, dtype=jnp.int32) * MOE_BLOCK
    blk_exp = jnp.minimum(jnp.sum(pad_end[None, :] <= blk_start[:, None], axis=1), N_EXPERTS - 1).astype(jnp.int32)
    n_used = (pad_end[-1:] // MOE_BLOCK).astype(jnp.int32)
    xg = xb[buf_tok]
    y = _grouped_ffn(xg, blk_exp, n_used, p['layer'], p['w_gate'], p['w_up'], p['w_down'])
    return _moe_combine(y, pos, weight, h, gt)


def _combine_kernel(pos_ref, y_hbm, wts_ref, h_ref, gt_ref, o_ref, buf, sem):
    i = pl.program_id(0)
    tokens = h_ref.shape[0]

    def row_copy(step, slot, r, k):
        src = pos_ref[(step * tokens + r) * TOP_K + k]
        return pltpu.make_async_copy(y_hbm.at[pl.ds(src, 1)], buf.at[slot, k, pl.ds(r, 1)], sem.at[slot])

    def issue(step, slot):
        def body(r, carry):
            for k in range(TOP_K):
                row_copy(step, slot, r, k).start()
            return carry

        lax.fori_loop(0, tokens, body, 0)

    @pl.when(i == 0)
    def _():
        issue(0, 0)

    @pl.when(i + 1 < pl.num_programs(0))
    def _():
        issue(i + 1, (i + 1) % 2)

    slot = i % 2
    pltpu.make_async_copy(buf.at[slot], buf.at[slot], sem.at[slot]).wait()
    rows = buf[slot]
    wts = wts_ref[...]
    acc = rows[0] * wts[:, 0:1]
    for k in range(1, TOP_K):
        acc = acc + rows[k] * wts[:, k:k + 1]
    o_ref[...] = h_ref[...] + gt_ref[...] * acc


COMBINE_TOKENS = 128


def _moe_combine(y, pos, wts, h, gt):
    n, d = h.shape
    tokens = min(COMBINE_TOKENS, n)
    grid_spec = pltpu.PrefetchScalarGridSpec(
        num_scalar_prefetch=1,
        grid=(n // tokens,),
        in_specs=[pl.BlockSpec(memory_space=pl.ANY),
                  pl.BlockSpec((tokens, TOP_K), lambda i, pos: (i, 0)),
                  pl.BlockSpec((tokens, d), lambda i, pos: (i, 0)),
                  pl.BlockSpec((1, d), lambda i, pos: (0, 0))],
        out_specs=pl.BlockSpec((tokens, d), lambda i, pos: (i, 0)),
        scratch_shapes=[pltpu.VMEM((2, TOP_K, tokens, d), jnp.float32), pltpu.SemaphoreType.DMA((2,))],
    )
    return pl.pallas_call(
        _combine_kernel,
        grid_spec=grid_spec,
        out_shape=jax.ShapeDtypeStruct((n, d), jnp.float32),
        compiler_params=_cparams("arbitrary"),
        name="moe_combine",
    )(pos, y, wts, h, gt)


def _final_norm_kernel(x_ref, w_ref, o_ref):
    x = x_ref[...]
    o_ref[...] = x * lax.rsqrt(jnp.mean(x * x, axis=-1, keepdims=True) + NORM_EPS) * w_ref[...]


def _final_norm(x, w, tm=512):
    m, d = x.shape
    return pl.pallas_call(
        _final_norm_kernel,
        grid=(m // tm,),
        in_specs=[pl.BlockSpec((tm, d), lambda i: (i, 0)), pl.BlockSpec((1, d), lambda i: (0, 0))],
        out_specs=pl.BlockSpec((tm, d), lambda i: (i, 0)),
        out_shape=jax.ShapeDtypeStruct((m, d), jnp.float32),
        compiler_params=_cparams("parallel"),
        name="final_rmsnorm",
    )(x, w.reshape(1, d))


def kernel(x, c, ctx, c_ctx, w_mod, b_mod, norm_mix_w, norm_ffn_w, w_in, b_in, hy_conv_w, hy_conv_b, hy_f_w1, hy_f_b1, hy_f_w2, hy_f_b2, hy_f_w3, hy_f_b3, hy_f_freq, hy_skip, hg_lb_raw, hg_norm_w, gl_w_a2, gl_b_a, gl_norm_w, w_branch, w_out, w_rg, b_rg, w_re, b_re, w_gate, w_up, w_down, final_norm_w):
    assert x.shape[0] == 1
    depth = w_mod.shape[0]
    lb_all = jnp.cumsum(jax.nn.softmax(hg_lb_raw, axis=0), axis=0)
    lb_all = lb_all - lb_all[:1]
    h, hc = x[0], ctx[0]
    cc = jnp.concatenate([c, c_ctx[None, :]], axis=0)
    for l in range(depth):
        need_ctx = l < depth - 1
        mod = _matmul(_bf(jax.nn.silu(cc)), _bf(w_mod[l]), b_mod[l])
        sh1, sc1, gt1, sh2, sc2, gt2 = jnp.split(mod[0:1], 6, axis=-1)
        sh1c, sc1c, gt1c, sh2c, sc2c, gt2c = jnp.split(mod[1:2], 6, axis=-1)
        p = dict(w_in=w_in, b_in=b_in[l], hy_conv_w=hy_conv_w[l], hy_conv_b=hy_conv_b[l],
                 hy_f=(hy_f_w1[l], hy_f_b1[l], hy_f_w2[l], hy_f_b2[l], hy_f_w3[l], hy_f_b3[l], hy_f_freq[l]),
                 hy_skip=hy_skip[l], lb=lb_all[l], hg_norm_w=hg_norm_w[l], gl_w_a2=gl_w_a2[l], gl_b_a=gl_b_a[l],
                 gl_norm_w=gl_norm_w[l], w_branch=w_branch[l], w_out=w_out[l], w_rg=w_rg[l], b_rg=b_rg[l],
                 w_re=w_re[l], b_re=b_re[l], layer=l, w_gate=w_gate, w_up=w_up, w_down=w_down)
        u = _norm_mod(h, norm_mix_w[l], sh1, sc1)
        uc = _norm_mod(hc, norm_mix_w[l], sh1c, sc1c)
        h, hc = _mixer(h, hc, u, uc, gt1, gt1c, p, need_ctx)
        h = _hier_moe(h, norm_ffn_w[l], sh2, sc2, gt2, p)
        if need_ctx:
            hc = _hier_moe(hc, norm_ffn_w[l], sh2c, sc2c, gt2c, p)
    return _final_norm(h, final_norm_w)[None]
```

```python
import functools
import math

import jax
import jax.numpy as jnp
import numpy as np
from jax import lax
from jax.experimental import pallas as pl
from jax.experimental.pallas import tpu as pltpu

D_MODEL = 2048
NORM_EPS = 1e-6

HY_C = D_MODEL // 2
HY_EMB = 33
HY_BANDS = (HY_EMB - 1) // 2
HY_DECAY_TARGET = 1e-2
HY_FAST_PCT = 0.3
HY_SLOW_PCT = 1.5
HY_MOD_SHIFT = 0.05

HG_HEADS = 8
HG_DK = 128
HG_DV = 128
HG_K = HG_HEADS * HG_DK
HG_V = HG_HEADS * HG_DV

GL_HEADS = 4
GL_DK = 128
GL_DV = 256
GL_K = GL_HEADS * GL_DK
GL_V = GL_HEADS * GL_DV
GL_RANK = 16
GL_TAU = 16.0

N_BRANCH = 3
HG_COLS = 3 * HG_K + 2 * HG_V
GL_COLS = 2 * GL_K + 2 * GL_V + 2 * GL_RANK
REC_COLS = HG_COLS + GL_COLS
HY_COLS = 3 * HY_C
MERGE_COLS = N_BRANCH * D_MODEL

COL_TILE = 512
GL_Q_OFF = HG_COLS
GL_V_OFF = GL_Q_OFF + 2 * GL_K
GL_A_OFF = GL_V_OFF + 2 * GL_V
HY_OFF = -(-(GL_A_OFF + 2 * GL_RANK) // HY_C) * HY_C
MG_OFF = HY_OFF + HY_COLS
Z_COLS = MG_OFF + MERGE_COLS
assert GL_Q_OFF % GL_K == 0 and GL_V_OFF % GL_V == 0 and GL_A_OFF % 128 == 0 and Z_COLS % COL_TILE == 0
assert MG_OFF % D_MODEL == 0

N_GROUPS = 4
EXP_PER_GROUP = 8
N_EXPERTS = N_GROUPS * EXP_PER_GROUP
TOP_K = 2
D_FF = D_MODEL // 4
MOE_BLOCK = 256

SCAN_CHUNK = 128
LOG2_E = 1.4426950408889634

VMEM_LIMIT_BYTES = 56 * 1024 * 1024


def _cparams(*sem):
    return pltpu.CompilerParams(dimension_semantics=sem, vmem_limit_bytes=VMEM_LIMIT_BYTES)


def _bf(a):
    return a.astype(jnp.bfloat16)


def _mm_kernel(x_ref, w_ref, b_ref, o_ref):
    acc = jnp.dot(x_ref[...], w_ref[...].astype(jnp.bfloat16), preferred_element_type=jnp.float32) + b_ref[...]
    o_ref[...] = acc.astype(o_ref.dtype)


def _matmul(x, w, bias=None, tm=512, tn=COL_TILE, out_dtype=jnp.float32, n_cols=None, layer=None):
    m, k = x.shape
    n = w.shape[-1] if n_cols is None else n_cols
    assert n_cols is None or n_cols % tn == 0
    assert layer is None or n % tn == 0
    tm = min(tm, -(-m // 8) * 8)
    mp = -(-m // tm) * tm
    np_ = -(-n // tn) * tn
    if bias is None:
        bias = jnp.zeros((n,), jnp.float32)
    if mp != m:
        x = jnp.pad(x, ((0, mp - m), (0, 0)))
    if np_ != n:
        w = jnp.pad(w, ((0, 0), (0, np_ - n)))
        bias = jnp.pad(bias, (0, np_ - n))
    if layer is None:
        w_spec = pl.BlockSpec((k, tn), lambda i, j: (0, j))
    else:
        w_spec = pl.BlockSpec((None, k, tn), lambda i, j: (layer, 0, j))
    out = pl.pallas_call(
        _mm_kernel,
        grid=(mp // tm, np_ // tn),
        in_specs=[pl.BlockSpec((tm, k), lambda i, j: (i, 0)),
                  w_spec,
                  pl.BlockSpec((1, tn), lambda i, j: (0, j))],
        out_specs=pl.BlockSpec((tm, tn), lambda i, j: (i, j)),
        out_shape=jax.ShapeDtypeStruct((mp, np_), out_dtype),
        compiler_params=_cparams("parallel", "arbitrary"),
        name="dense_matmul",
    )(x, w, bias.reshape(1, np_))
    if mp != m or np_ != n:
        out = out[:m, :n]
    return out


def _norm_mod_kernel(*refs, with_router):
    if with_router:
        h_ref, w_ref, sh_ref, sc_ref, wr_ref, br_ref, o_ref, lg_ref = refs
    else:
        h_ref, w_ref, sh_ref, sc_ref, o_ref = refs
    x = h_ref[...]
    y = x * lax.rsqrt(jnp.mean(x * x, axis=-1, keepdims=True) + NORM_EPS) * w_ref[...]
    u = (y * (1.0 + sc_ref[...]) + sh_ref[...]).astype(jnp.bfloat16)
    o_ref[...] = u
    if with_router:
        lg_ref[...] = jnp.dot(u, wr_ref[...], preferred_element_type=jnp.float32) + br_ref[...]


def _norm_mod(h, w, shift, scale, router=None):
    m, d = h.shape
    tm = min(512, m)
    row = pl.BlockSpec((tm, d), lambda i: (i, 0))
    vec = pl.BlockSpec((1, d), lambda i: (0, 0))
    args = [h, w.reshape(1, d), shift, scale]
    specs = [row, vec, vec, vec]
    out_shape = [jax.ShapeDtypeStruct((m, d), jnp.bfloat16)]
    out_specs = [row]
    if router is not None:
        args += list(router)
        specs += [pl.BlockSpec(router[0].shape, lambda i: (0, 0)), pl.BlockSpec(router[1].shape, lambda i: (0, 0))]
        out_shape.append(jax.ShapeDtypeStruct((m, router[0].shape[1]), jnp.float32))
        out_specs.append(pl.BlockSpec((tm, router[0].shape[1]), lambda i: (i, 0)))
    out = pl.pallas_call(
        functools.partial(_norm_mod_kernel, with_router=router is not None),
        grid=(m // tm,),
        in_specs=specs,
        out_specs=out_specs,
        out_shape=out_shape,
        compiler_params=_cparams("parallel"),
        name="norm_modulate",
    )(*args)
    return out if router is not None else out[0]


def _scan_masks(c, reverse):
    t = np.arange(c)
    ms = [np.eye(c, dtype=np.float32)]
    for lvl in range(int(math.log2(c))):
        upper = ((t >> lvl) & 1).astype(bool)
        same = (t[:, None] >> (lvl + 1)) == (t[None, :] >> (lvl + 1))
        m = same & upper[:, None] & (~upper)[None, :]
        ms.append((m.T if reverse else m).astype(np.float32))
    tri = t[None, :] >= t[:, None] if reverse else t[None, :] <= t[:, None]
    return jnp.asarray(np.stack(ms)), jnp.asarray(tri.astype(np.float32), dtype=jnp.bfloat16)


def _level_arg(cum, lvl, reverse):
    c = cum.shape[0]
    blk = 1 << lvl
    if blk >= 8:
        pieces = []
        for gs in range(0, c, 2 * blk):
            ref = cum[gs + blk:gs + blk + 1, :]
            pieces.append(ref - cum[gs:gs + blk, :])
            pieces.append(cum[gs + blk:gs + 2 * blk, :] - ref)
        arg = jnp.concatenate(pieces, axis=0)
    else:
        c3 = cum.reshape(c // 8, 8, cum.shape[1])
        sub = lax.broadcasted_iota(jnp.int32, c3.shape, 1)
        ref_row = ((sub >> lvl) | 1) << lvl
        ref = None
        for r in range(blk, 8, 2 * blk):
            cand = jnp.broadcast_to(c3[:, r:r + 1, :], c3.shape)
            ref = cand if ref is None else jnp.where(ref_row == r, cand, ref)
        upper = ((sub >> lvl) & 1) == 1
        arg = jnp.where(upper, c3 - ref, ref - c3).reshape(cum.shape)
    return -arg if reverse else arg


def _dot_nt(a, b):
    return lax.dot_general(a, b, (((1,), (1,)), ((), ())), preferred_element_type=jnp.float32)


def _dot_tn(a, b):
    return lax.dot_general(a, b, (((0,), (0,)), ((), ())), preferred_element_type=jnp.float32)


def _sigmoid_parts(z):
    e = jnp.exp(-jnp.abs(z))
    r = 1.0 / (1.0 + e)
    return jnp.minimum(z, 0.0) - jnp.log(1.0 + e), jnp.where(z >= 0.0, e * r, r)


def _scan_kernel(*refs, mode, reverse, final, heads, dk, dv, c):
    it = iter(refs)
    q_ref = next(it)
    k_ref = next(it)
    v_ref = next(it)
    if mode == "hg":
        lbp_ref = next(it)
    else:
        a_ref = next(it)
        wa_ref = next(it)
        ba_ref = next(it)
    s0_ref = next(it)
    masks_ref = next(it)
    tri_ref = next(it)
    if final:
        oprev_ref = next(it)
        gate_ref = next(it)
        nw_ref = next(it)
    o_ref = next(it)
    st_ref = next(it)

    @pl.when(pl.program_id(0) == 0)
    def _():
        st_ref[...] = s0_ref[...]

    if mode == "gl":
        la_all = jnp.dot(a_ref[...].astype(jnp.bfloat16), wa_ref[...],
                         preferred_element_type=jnp.float32) + ba_ref[...]
    tri = tri_ref[...]
    tot_row = 0 if reverse else c - 1
    n_lvl = int(math.log2(c))
    for h in range(heads):
        ks = slice(h * dk, (h + 1) * dk)
        vs = slice(h * dv, (h + 1) * dv)
        q = q_ref[:, ks]
        v = v_ref[:, vs].astype(jnp.bfloat16)
        if mode == "hg":
            log_sig, sig_neg = _sigmoid_parts(k_ref[:, ks])
            la = lbp_ref[0:1, ks]
            lbb = lbp_ref[1:2, ks] + log_sig
            g = jnp.maximum(la, lbb) + jnp.log(1.0 + jnp.exp(-jnp.abs(la - lbb)))
            k = lbp_ref[2:3, ks] * sig_neg
            q = q * jax.nn.sigmoid(q)
        else:
            g = _sigmoid_parts(la_all[:, ks])[0] * (1.0 / GL_TAU)
            k = k_ref[:, ks]
            q = q * (dk ** -0.5)
        g = g * LOG2_E
        g1 = g.astype(jnp.bfloat16)
        r1 = g - g1.astype(jnp.float32)
        g2 = r1.astype(jnp.bfloat16)
        g3 = (r1 - g2.astype(jnp.float32)).astype(jnp.bfloat16)
        cum = (jnp.dot(tri, g1, preferred_element_type=jnp.float32)
               + jnp.dot(tri, g2, preferred_element_type=jnp.float32)
               + jnp.dot(tri, g3, preferred_element_type=jnp.float32))
        tot = cum[tot_row:tot_row + 1, :]
        st = st_ref[h]
        o = _dot_nt((q * jnp.exp2(cum)).astype(jnp.bfloat16), st.astype(jnp.bfloat16))
        kt = (k * jnp.exp2(tot - cum)).astype(jnp.bfloat16)
        st_ref[h] = st * jnp.exp2(tot) + _dot_tn(v, kt)
        qb = q.astype(jnp.bfloat16)
        kb = k.astype(jnp.bfloat16)
        scores = masks_ref[0] * _dot_nt(qb, kb)
        for lvl in range(n_lvl):
            e = jnp.exp2(_level_arg(cum, lvl, reverse)).astype(jnp.bfloat16)
            scores = scores + masks_ref[1 + lvl] * _dot_nt(qb * e, kb * e)
        o = o + jnp.dot(scores.astype(jnp.bfloat16), v, preferred_element_type=jnp.float32)
        if final:
            o = o + oprev_ref[:, vs]
            y = o * lax.rsqrt(jnp.mean(o * o, axis=-1, keepdims=True) + NORM_EPS) * nw_ref[...]
            gt = gate_ref[:, vs]
            act = jax.nn.sigmoid(gt) if mode == "hg" else gt * jax.nn.sigmoid(gt)
            o_ref[:, vs] = (y * act).astype(o_ref.dtype)
        else:
            o_ref[:, vs] = o


def _scan_pass(mode, reverse, final, L, srcs, s0, params, final_srcs=(), norm_w=None):
    heads, dk, dv = (HG_HEADS, HG_DK, HG_DV) if mode == "hg" else (GL_HEADS, GL_DK, GL_DV)
    c = min(SCAN_CHUNK, L)
    nb = L // c
    row = (lambda i: nb - 1 - i) if reverse else (lambda i: i)
    masks, tri = _scan_masks(c, reverse)

    def const(shape):
        return pl.BlockSpec(shape, lambda i: (0,) * len(shape))

    def rowblock(width, cb):
        return pl.BlockSpec((c, width), lambda i: (row(i), cb))

    args = [a for a, _, _ in srcs] + list(params) + [s0, masks, tri]
    specs = ([rowblock(w, cb) for _, w, cb in srcs] + [const(p.shape) for p in params]
             + [const(s0.shape), const(masks.shape), const(tri.shape)])
    if final:
        args += [a for a, _, _ in final_srcs] + [norm_w]
        specs += [rowblock(w, cb) for _, w, cb in final_srcs] + [const(norm_w.shape)]
    return pl.pallas_call(
        functools.partial(_scan_kernel, mode=mode, reverse=reverse, final=final, heads=heads, dk=dk, dv=dv, c=c),
        grid=(nb,),
        in_specs=specs,
        out_specs=[pl.BlockSpec((c, heads * dv), lambda i: (row(i), 0)), const((heads, dv, dk))],
        out_shape=[jax.ShapeDtypeStruct((L, heads * dv), jnp.bfloat16 if final else jnp.float32),
                   jax.ShapeDtypeStruct((heads, dv, dk), jnp.float32)],
        compiler_params=_cparams("arbitrary"),
        name=f"scan_{mode}_{'bwd' if reverse else 'fwd'}",
    )(*args)


def _hgrn2(z, L, lb, norm_w, s0_f, s0_b):
    lbp = lambda d: jnp.stack([jnp.log(lb[d]), jnp.log1p(-lb[d]), 1.0 - lb[d]])
    w = HG_K
    o_b, s_b = _scan_pass("hg", True, False, L, [(z, w, 0), (z, w, 2), (z, w, 3)], s0_b, [lbp(1)])
    y, s_f = _scan_pass("hg", False, True, L, [(z, w, 0), (z, w, 1), (z, w, 3)], s0_f, [lbp(0)],
                        final_srcs=[(o_b, HG_V, 0), (z, HG_V, 4)], norm_w=norm_w.reshape(1, HG_DV))
    return y, s_f, s_b


def _gla(z, L, w_a2, b_a, norm_w, s0_f, s0_b):
    def gate_params(d):
        wa = jnp.zeros((128, GL_K), jnp.float32).at[d * GL_RANK:(d + 1) * GL_RANK].set(w_a2[d])
        return [_bf(wa), b_a[d].reshape(1, GL_K)]

    srcs = [(z, GL_K, GL_Q_OFF // GL_K), (z, GL_K, GL_Q_OFF // GL_K + 1), (z, GL_V, GL_V_OFF // GL_V),
            (z, 128, GL_A_OFF // 128)]
    o_b, s_b = _scan_pass("gl", True, False, L, srcs, s0_b, gate_params(1))
    y, s_f = _scan_pass("gl", False, True, L, srcs, s0_f, gate_params(0),
                        final_srcs=[(o_b, GL_V, 0), (z, GL_V, GL_V_OFF // GL_V + 1)], norm_w=norm_w.reshape(1, GL_DV))
    return y, s_f, s_b


HY_FEAT_PAD = 128


def _hy_filter_kernel(f_ref, w1_ref, b1_ref, fq_ref, w2_ref, b2_ref, w3_ref, b3_ref, dl_ref, h_ref, s_ref):
    i = pl.program_id(0)
    f = f_ref[...]
    fq = fq_ref[...]
    a = jnp.sin(fq * (jnp.dot(f.astype(jnp.bfloat16), w1_ref[...], preferred_element_type=jnp.float32) + b1_ref[...]))
    a = jnp.sin(fq * (jnp.dot(a.astype(jnp.bfloat16), w2_ref[...], preferred_element_type=jnp.float32) + b2_ref[...]))
    hh = jnp.dot(a.astype(jnp.bfloat16), w3_ref[...], preferred_element_type=jnp.float32) + b3_ref[...]
    hh = hh * (jnp.exp(-f[:, 0:1] * dl_ref[...]) + HY_MOD_SHIFT)
    h_ref[...] = hh
    part = jnp.sum(jnp.abs(hh).reshape(hh.shape[0] // 8, 8, hh.shape[1]), axis=0)

    @pl.when(i == 0)
    def _():
        s_ref[...] = part

    @pl.when(i > 0)
    def _():
        s_ref[...] += part


def _hyena_filters(L, w1, b1, w2, b2, w3, b3, freq):
    t = jnp.linspace(0.0, 1.0, L, dtype=jnp.float32)[:, None]
    ang = 2.0 * math.pi * jnp.arange(L, dtype=jnp.float32)[:, None] / L
    bands = jnp.linspace(1e-4, HY_BANDS - 1, HY_BANDS, dtype=jnp.float32)[None, :]
    feats = jnp.concatenate([t, jnp.cos(bands * ang), -jnp.sin(bands * ang),
                             jnp.zeros((L, HY_FEAT_PAD - HY_EMB), jnp.float32)], axis=-1)
    deltas = jnp.abs(jnp.linspace(math.log(HY_DECAY_TARGET) / HY_SLOW_PCT, math.log(HY_DECAY_TARGET) / HY_FAST_PCT,
                                  HY_C, dtype=jnp.float32))
    fh = w1.shape[1]
    padm = lambda a, r, c: _bf(jnp.pad(a, ((0, r - a.shape[0]), (0, c - a.shape[1]))))
    padv = lambda a: jnp.pad(a, (0, HY_FEAT_PAD - a.shape[0])).reshape(1, HY_FEAT_PAD)
    tm = min(512, L)
    const = lambda r, c: pl.BlockSpec((r, c), lambda i: (0, 0))
    hfil, sums = pl.pallas_call(
        _hy_filter_kernel,
        grid=(L // tm,),
        in_specs=[pl.BlockSpec((tm, HY_FEAT_PAD), lambda i: (i, 0)),
                  const(HY_FEAT_PAD, HY_FEAT_PAD), const(1, HY_FEAT_PAD), const(1, HY_FEAT_PAD),
                  const(HY_FEAT_PAD, HY_FEAT_PAD), const(1, HY_FEAT_PAD),
                  const(HY_FEAT_PAD, 2 * HY_C), const(1, 2 * HY_C), const(1, 2 * HY_C)],
        out_specs=[pl.BlockSpec((tm, 2 * HY_C), lambda i: (i, 0)), const(8, 2 * HY_C)],
        out_shape=[jax.ShapeDtypeStruct((L, 2 * HY_C), jnp.float32), jax.ShapeDtypeStruct((8, 2 * HY_C), jnp.float32)],
        compiler_params=_cparams("arbitrary"),
        name="hyena_filters",
    )(feats, padm(w1, HY_FEAT_PAD, HY_FEAT_PAD), padv(b1), padv(freq), padm(w2, HY_FEAT_PAD, HY_FEAT_PAD), padv(b2),
      padm(w3, HY_FEAT_PAD, 2 * HY_C), b3.reshape(1, 2 * HY_C), jnp.tile(deltas, 2).reshape(1, 2 * HY_C))
    assert fh <= HY_FEAT_PAD
    inorm = 1.0 / jnp.sum(sums, axis=0)
    return hfil, inorm.reshape(2, HY_C)


def _hy_pre_kernel(x0_ref, x1_ref, v_ref, x0p_ref, x1p_ref, vp_ref, x0n_ref, x1n_ref, vn_ref, w_ref, b_ref,
                   vo_ref, x0o_ref):
    i = pl.program_id(0)
    first = i == 0
    last = i == pl.num_programs(0) - 1
    tm = x0_ref.shape[0]
    row = lax.broadcasted_iota(jnp.int32, x0_ref.shape, 0)

    def conv(x_ref, p_ref, n_ref, g):
        x = x_ref[...]
        cs = slice(g * HY_C, (g + 1) * HY_C)
        prev_row = jnp.where(first, 0.0, p_ref[7:8, :])
        next_row = jnp.where(last, 0.0, n_ref[0:1, :])
        xp = jnp.where(row == 0, prev_row, pltpu.roll(x, 1, 0))
        xn = jnp.where(row == tm - 1, next_row, pltpu.roll(x, tm - 1, 0))
        return w_ref[0:1, cs] * xp + w_ref[1:2, cs] * x + w_ref[2:3, cs] * xn + b_ref[0:1, cs]

    x0 = conv(x0_ref, x0p_ref, x0n_ref, 0)
    x1 = conv(x1_ref, x1p_ref, x1n_ref, 1)
    v = conv(v_ref, vp_ref, vn_ref, 2)
    vo_ref[...] = (v * x1).astype(vo_ref.dtype)
    x0o_ref[...] = x0.astype(x0o_ref.dtype)


def _hy_pre(z, L, conv_w, conv_b):
    tm = min(256, L)
    nb8 = L // 8
    cb = HY_OFF // HY_C
    main = lambda g: pl.BlockSpec((tm, HY_C), lambda i: (i, cb + g))
    prev = lambda g: pl.BlockSpec((8, HY_C), lambda i: (jnp.maximum(i * (tm // 8) - 1, 0), cb + g))
    nxt = lambda g: pl.BlockSpec((8, HY_C), lambda i: (jnp.minimum((i + 1) * (tm // 8), nb8 - 1), cb + g))
    const = lambda a: pl.BlockSpec(a.shape, lambda i: (0, 0))
    cbias = conv_b.reshape(1, HY_COLS)
    return pl.pallas_call(
        _hy_pre_kernel,
        grid=(L // tm,),
        in_specs=[main(0), main(1), main(2), prev(0), prev(1), prev(2), nxt(0), nxt(1), nxt(2),
                  const(conv_w), const(cbias)],
        out_specs=[pl.BlockSpec((tm, HY_C), lambda i: (i, 0))] * 2,
        out_shape=[jax.ShapeDtypeStruct((L, HY_C), jnp.float32), jax.ShapeDtypeStruct((L, HY_C), jnp.bfloat16)],
        compiler_params=_cparams("parallel"),
        name="hyena_short_conv",
    )(z, z, z, z, z, z, z, z, z, conv_w, cbias)


HY_N1 = 256
HY_TWO_STAGE_MIN_L = 1024


def _dft_outer_table(n1, cols):
    ang = -2.0 * np.pi * np.outer(np.arange(n1 // 2) + 0.5, np.arange(cols)) / n1
    return jnp.asarray(np.concatenate([np.cos(ang), np.sin(ang)], axis=0), jnp.bfloat16)


def _dft_inner_table(n1, n2):
    j2 = np.arange(n2)
    f_ang = -2.0 * np.pi * np.outer(np.arange(n2), j2) / n2
    tw_ang = -2.0 * np.pi * np.outer(np.arange(n1 // 2) + 0.5, j2) / (n1 * n2)
    fr, fi = jnp.asarray(np.cos(f_ang), jnp.float32), jnp.asarray(np.sin(f_ang), jnp.float32)
    twr, twi = jnp.asarray(np.cos(tw_ang), jnp.float32), jnp.asarray(np.sin(tw_ang), jnp.float32)
    mr = fr[None] * twr[:, None, :] - fi[None] * twi[:, None, :]
    mi = fr[None] * twi[:, None, :] + fi[None] * twr[:, None, :]
    return _bf(jnp.concatenate([jnp.concatenate([mr, -mi], axis=2), jnp.concatenate([mi, mr], axis=2)], axis=1))


def _spectral_product(xv, xh, inorm, half):
    inf, inb = inorm[0:1, :], inorm[1:2, :]
    gr = xh[:half, :HY_C] * inf + xh[:half, HY_C:] * inb
    gi = xh[half:, :HY_C] * inf - xh[half:, HY_C:] * inb
    xr, xi = xv[:half], xv[half:]
    return jnp.concatenate([xr * gr - xi * gi, xr * gi + xi * gr], axis=0).astype(jnp.bfloat16)


def _hy_spec_kernel(r_ref, avr_ref, avi_ref, ahr_ref, ahi_ref, inorm_ref, br_ref, bi_ref):
    r = r_ref[0]
    n2 = avr_ref.shape[1]
    xv = jnp.dot(r, jnp.concatenate([avr_ref[0], avi_ref[0]], axis=0), preferred_element_type=jnp.float32)
    xh = jnp.dot(r, jnp.concatenate([ahr_ref[0], ahi_ref[0]], axis=0), preferred_element_type=jnp.float32)
    b = _dot_tn(r, _spectral_product(xv, xh, inorm_ref[...], n2))
    br_ref[0] = b[:n2].astype(br_ref.dtype)
    bi_ref[0] = b[n2:].astype(bi_ref.dtype)


def _hy_spec(r, av, ah, inorm, n1, n2):
    av3 = av.reshape(n1, n2, HY_C)
    ah3 = ah.reshape(n1, n2, 2 * HY_C)
    h1 = n1 // 2
    out = jax.ShapeDtypeStruct((h1, n2, HY_C), jnp.bfloat16)
    return pl.pallas_call(
        _hy_spec_kernel,
        grid=(h1,),
        in_specs=[pl.BlockSpec((1, 2 * n2, 2 * n2), lambda k: (k, 0, 0)),
                  pl.BlockSpec((1, n2, HY_C), lambda k: (k, 0, 0)),
                  pl.BlockSpec((1, n2, HY_C), lambda k: (k + h1, 0, 0)),
                  pl.BlockSpec((1, n2, 2 * HY_C), lambda k: (k, 0, 0)),
                  pl.BlockSpec((1, n2, 2 * HY_C), lambda k: (k + h1, 0, 0)),
                  pl.BlockSpec((2, HY_C), lambda k: (0, 0))],
        out_specs=[pl.BlockSpec((1, n2, HY_C), lambda k: (k, 0, 0))] * 2,
        out_shape=[out, out],
        compiler_params=_cparams("parallel"),
        name="hyena_spectral",
    )(r, av3, av3, ah3, ah3, inorm)


def _hy_spec_direct_kernel(xv_ref, xh_ref, inorm_ref, yr_ref, yi_ref):
    half = yr_ref.shape[0]
    y = _spectral_product(xv_ref[...].astype(jnp.float32), xh_ref[...].astype(jnp.float32), inorm_ref[...], half)
    yr_ref[...] = y[:half]
    yi_ref[...] = y[half:]


def _hy_spec_direct(xv, xh, inorm, L):
    full = lambda a: pl.BlockSpec(a.shape, lambda i: (0, 0))
    out = jax.ShapeDtypeStruct((L, HY_C), jnp.bfloat16)
    return pl.pallas_call(
        _hy_spec_direct_kernel,
        grid=(1,),
        in_specs=[full(xv), full(xh), full(inorm)],
        out_specs=[pl.BlockSpec((L, HY_C), lambda i: (0, 0))] * 2,
        out_shape=[out, out],
        compiler_params=_cparams("arbitrary"),
        name="hyena_spectral_direct",
    )(xv, xh, inorm)


def _hy_post_kernel(tr_ref, ti_ref, br_ref, bi_ref, v_ref, x0_ref, skip_ref, o_ref, *, scale):
    acc = (jnp.dot(tr_ref[...], br_ref[...], preferred_element_type=jnp.float32)
           + jnp.dot(ti_ref[...], bi_ref[...], preferred_element_type=jnp.float32))
    y = (acc * scale + v_ref[...].astype(jnp.float32) * skip_ref[...]) * x0_ref[...].astype(jnp.float32)
    o_ref[...] = y.astype(o_ref.dtype)


def _hy_post(t_fwd, b_r, b_i, v, x0, skip, L, h1, n2):
    ncol = n2 * HY_C
    tn = min(4096, ncol)
    tr_t = t_fwd[:h1].T
    ti_t = t_fwd[h1:].T
    skip_t = jnp.tile(skip, tn // HY_C).reshape(1, tn)
    col = lambda rows: pl.BlockSpec((rows, tn), lambda j: (0, j))
    rows_out = tr_t.shape[0]
    y = pl.pallas_call(
        functools.partial(_hy_post_kernel, scale=1.0 / L),
        grid=(ncol // tn,),
        in_specs=[pl.BlockSpec(tr_t.shape, lambda j: (0, 0)), pl.BlockSpec(ti_t.shape, lambda j: (0, 0)),
                  col(h1), col(h1), col(rows_out), col(rows_out), pl.BlockSpec((1, tn), lambda j: (0, 0))],
        out_specs=col(rows_out),
        out_shape=jax.ShapeDtypeStruct((rows_out, ncol), jnp.bfloat16),
        compiler_params=_cparams("parallel"),
        name="hyena_inverse",
    )(tr_t, ti_t, b_r.reshape(h1, ncol), b_i.reshape(h1, ncol), v.reshape(rows_out, ncol),
      x0.reshape(rows_out, ncol), skip_t)
    return y.reshape(L, HY_C)


SUBLANE = 8
HY_COL_TILE = 512


def _dft_outer_kron(n1):
    h1 = n1 // 2
    ang = -2.0 * np.pi * np.outer(np.arange(h1) + 0.5, np.arange(h1)) / n1
    eye = np.eye(SUBLANE)
    t_r, t_i = np.cos(ang), np.sin(ang)
    fwd = np.kron(np.concatenate([t_r, t_i], axis=0), eye)
    inv = np.concatenate([np.kron(t_r.T, eye), np.kron(t_i.T, eye)], axis=1)
    return jnp.asarray(fwd, jnp.bfloat16), jnp.asarray(inv, jnp.bfloat16)


def _hy_outer_fwd_kernel(t_ref, x_ref, o_ref):
    x = x_ref[...]
    rows_in, rows_out = x.shape[0], t_ref.shape[0] // SUBLANE
    cw = x.shape[2]
    parts = []
    for s in range(0, x.shape[1], SUBLANE):
        xs = x[:, s:s + SUBLANE, :].reshape(rows_in * SUBLANE, cw).astype(jnp.bfloat16)
        r = jnp.dot(t_ref[...], xs, preferred_element_type=jnp.float32)
        parts.append(r.reshape(rows_out, SUBLANE, cw))
    o_ref[...] = jnp.concatenate(parts, axis=1).astype(o_ref.dtype)


def _hy_outer_fwd(t_kron, x3):
    h1, n2, w = x3.shape
    n1 = 2 * h1
    blk = 2 * SUBLANE
    return pl.pallas_call(
        _hy_outer_fwd_kernel,
        grid=(n2 // blk, w // HY_COL_TILE),
        in_specs=[pl.BlockSpec(t_kron.shape, lambda j, cc: (0, 0)),
                  pl.BlockSpec((h1, blk, HY_COL_TILE), lambda j, cc: (0, j, cc))],
        out_specs=pl.BlockSpec((n1, blk, HY_COL_TILE), lambda j, cc: (0, j, cc)),
        out_shape=jax.ShapeDtypeStruct((n1, n2, w), jnp.bfloat16),
        compiler_params=_cparams("parallel", "parallel"),
        name="hyena_outer_dft",
    )(t_kron, x3)


def _hy_outer_inv_kernel(t_ref, br_ref, bi_ref, v_ref, x0_ref, skip_ref, o_ref, *, scale):
    br = br_ref[...].astype(jnp.float32)
    bi = bi_ref[...].astype(jnp.float32)
    v = v_ref[...]
    x0 = x0_ref[...].astype(jnp.float32)
    h1, _, cw = br.shape
    parts = []
    for s in range(0, br.shape[1], SUBLANE):
        sl = slice(s, s + SUBLANE)
        b = jnp.concatenate([br[:, sl, :].reshape(h1 * SUBLANE, cw), bi[:, sl, :].reshape(h1 * SUBLANE, cw)], axis=0)
        r = jnp.dot(t_ref[...], b.astype(jnp.bfloat16), preferred_element_type=jnp.float32)
        parts.append((r.reshape(h1, SUBLANE, cw) * scale + v[:, sl, :] * skip_ref[...]) * x0[:, sl, :])
    o_ref[...] = jnp.concatenate(parts, axis=1).astype(o_ref.dtype)


def _hy_outer_inv(t_kron_inv, b_r, b_i, v3, x03, skip, L):
    h1, n2, w = b_r.shape
    blk = 2 * SUBLANE
    tile = pl.BlockSpec((h1, blk, HY_COL_TILE), lambda j, cc: (0, j, cc))
    return pl.pallas_call(
        functools.partial(_hy_outer_inv_kernel, scale=1.0 / L),
        grid=(n2 // blk, w // HY_COL_TILE),
        in_specs=[pl.BlockSpec(t_kron_inv.shape, lambda j, cc: (0, 0)), tile, tile, tile, tile,
                  pl.BlockSpec((1, 1, HY_COL_TILE), lambda j, cc: (0, 0, cc))],
        out_specs=tile,
        out_shape=jax.ShapeDtypeStruct((h1, n2, w), jnp.bfloat16),
        compiler_params=_cparams("parallel", "parallel"),
        name="hyena_outer_idft",
    )(t_kron_inv, b_r, b_i, v3, x03, skip.reshape(1, 1, w))


def _hyena(z, L, conv_w, conv_b, fparams, skip):
    v, x0 = _hy_pre(z, L, conv_w, conv_b)
    hfil, inorm = _hyena_filters(L, *fparams)
    if L >= HY_TWO_STAGE_MIN_L:
        n1 = HY_N1
        n2 = 2 * L // n1
        h1 = n1 // 2
        assert n2 % (2 * SUBLANE) == 0
        t_kron, t_kron_inv = _dft_outer_kron(n1)
        v3 = v.reshape(h1, n2, HY_C)
        av = _hy_outer_fwd(t_kron, v3)
        ah = _hy_outer_fwd(t_kron, hfil.reshape(h1, n2, 2 * HY_C))
        b_r, b_i = _hy_spec(_dft_inner_table(n1, n2), av, ah, inorm, n1, n2)
        return _hy_outer_inv(t_kron_inv, b_r, b_i, v3, x0.reshape(h1, n2, HY_C), skip, L).reshape(L, HY_C)
    t_fwd = _dft_outer_table(2 * L, L)
    xv = _matmul(t_fwd, _bf(v), tm=2 * L, tn=HY_C, out_dtype=jnp.bfloat16)
    xh = _matmul(t_fwd, _bf(hfil), tm=2 * L, tn=HY_C, out_dtype=jnp.bfloat16)
    y_r, y_i = _hy_spec_direct(xv, xh, inorm, L)
    return _hy_post(t_fwd, y_r, y_i, v, x0, skip, L, L, 1)


def _merge_kernel(yh_ref, yg_ref, yl_ref, gate_h_ref, gate_g_ref, gate_l_ref, wb_ref, o_ref):
    acc = None
    for br, (y_ref, g_ref) in enumerate(((yh_ref, gate_h_ref), (yg_ref, gate_g_ref), (yl_ref, gate_l_ref))):
        t = jnp.dot(y_ref[...], wb_ref[br], preferred_element_type=jnp.float32) * jax.nn.sigmoid(g_ref[...])
        acc = t if acc is None else acc + t
    o_ref[...] = acc.astype(o_ref.dtype)


def _proj_residual_kernel(m_ref, w_ref, h_ref, gt_ref, o_ref):
    o_ref[...] = h_ref[...] + gt_ref[...] * jnp.dot(m_ref[...], w_ref[...], preferred_element_type=jnp.float32)


def _merge(z, L, ys, w_branch, w_out, h, gt):
    tm = min(256, L)
    gb = MG_OFF // D_MODEL
    ybs = pl.BlockSpec((tm, HY_C), lambda i: (i, 0))
    gate = lambda br: pl.BlockSpec((tm, D_MODEL), lambda i: (i, gb + br))
    row = pl.BlockSpec((tm, D_MODEL), lambda i: (i, 0))
    merged = pl.pallas_call(
        _merge_kernel,
        grid=(L // tm,),
        in_specs=[ybs, ybs, ybs, gate(0), gate(1), gate(2),
                  pl.BlockSpec((N_BRANCH, HY_C, D_MODEL), lambda i: (0, 0, 0))],
        out_specs=row,
        out_shape=jax.ShapeDtypeStruct((L, D_MODEL), jnp.bfloat16),
        compiler_params=_cparams("parallel"),
        name="branch_merge",
    )(ys[0], ys[1], ys[2], z, z, z, _bf(w_branch))
    return pl.pallas_call(
        _proj_residual_kernel,
        grid=(L // tm,),
        in_specs=[row, pl.BlockSpec((D_MODEL, D_MODEL), lambda i: (0, 0)), row,
                  pl.BlockSpec((1, D_MODEL), lambda i: (0, 0))],
        out_specs=row,
        out_shape=jax.ShapeDtypeStruct((L, D_MODEL), jnp.float32),
        compiler_params=_cparams("parallel"),
        name="out_proj_residual",
    )(merged, _bf(w_out), h, gt)


def _pad_cols(a):
    pad = lambda n: jnp.zeros(a.shape[:-1] + (n,), a.dtype)
    return jnp.concatenate([a[..., :REC_COLS], pad(HY_OFF - REC_COLS), a[..., REC_COLS:]], axis=-1)


W_TILE = 1024
W_SHIFT = HY_OFF - REC_COLS
W_LANE_OFF = (-W_SHIFT) % 128
assert 0 < W_SHIFT <= W_TILE and HY_OFF % W_TILE == 0 and Z_COLS % W_TILE == 0


def _w_in_prep_kernel(a_ref, b_ref, o_ref):
    j = pl.program_id(0)
    shifted = j >= HY_OFF // W_TILE

    @pl.when(jnp.logical_not(shifted))
    def _():
        o_ref[...] = a_ref[...].astype(o_ref.dtype)

    @pl.when(shifted)
    def _():
        window = jnp.concatenate([a_ref[...], b_ref[...]], axis=1)
        o_ref[...] = window[:, W_LANE_OFF:W_LANE_OFF + W_TILE].astype(o_ref.dtype)


def _w_in_prep(w_in, layer):
    _, k, n = w_in.shape
    first_shifted = HY_OFF // W_TILE
    n_tiles = Z_COLS // W_TILE
    a_idx = lambda j: jnp.where(j < first_shifted, j, j - (W_SHIFT + W_TILE - 1) // W_TILE)
    tails = []
    for j in range(first_shifted, n_tiles):
        start = (j - (W_SHIFT + W_TILE - 1) // W_TILE + 1) * W_TILE
        piece = w_in[layer, :, start:min(start + 128, n)]
        tails.append(jnp.pad(piece, ((0, 0), (0, 128 - piece.shape[1]))))
    tail_cols = jnp.concatenate(tails, axis=1)
    return pl.pallas_call(
        _w_in_prep_kernel,
        grid=(n_tiles,),
        in_specs=[pl.BlockSpec((None, k, W_TILE), lambda j: (layer, 0, a_idx(j))),
                  pl.BlockSpec((k, 128), lambda j: (0, jnp.maximum(j - first_shifted, 0)))],
        out_specs=pl.BlockSpec((k, W_TILE), lambda j: (0, j)),
        out_shape=jax.ShapeDtypeStruct((k, Z_COLS), jnp.bfloat16),
        compiler_params=_cparams("parallel"),
        name="w_in_relayout",
    )(w_in, tail_cols)


def _mixer(h, hc, u, uc, gt, gtc, p, need_ctx):
    L, Lc = u.shape[0], uc.shape[0]
    w_in = _w_in_prep(p['w_in'], p['layer'])
    b_in = _pad_cols(p['b_in'])
    z = _matmul(u, w_in, b_in, tm=1024, tn=W_TILE)
    ncol = Z_COLS if need_ctx else HY_OFF
    zc = _matmul(uc, w_in, b_in[:ncol], tm=1024, tn=W_TILE, n_cols=ncol)
    zeros = lambda hd, dk, dv: jnp.zeros((hd, dv, dk), jnp.float32)
    yc_hg, hg_sf, hg_sb = _hgrn2(zc, Lc, p['lb'], p['hg_norm_w'],
                                 zeros(HG_HEADS, HG_DK, HG_DV), zeros(HG_HEADS, HG_DK, HG_DV))
    yc_gl, gl_sf, gl_sb = _gla(zc, Lc, p['gl_w_a2'], p['gl_b_a'], p['gl_norm_w'],
                               zeros(GL_HEADS, GL_DK, GL_DV), zeros(GL_HEADS, GL_DK, GL_DV))
    y_hg, _, _ = _hgrn2(z, L, p['lb'], p['hg_norm_w'], hg_sf, hg_sb)
    y_gl, _, _ = _gla(z, L, p['gl_w_a2'], p['gl_b_a'], p['gl_norm_w'], gl_sf, gl_sb)
    hy = (p['hy_conv_w'], p['hy_conv_b'], p['hy_f'], p['hy_skip'])
    y_hy = _hyena(z, L, *hy)
    h = _merge(z, L, (y_hy, y_hg, y_gl), p['w_branch'], p['w_out'], h, gt)
    if need_ctx:
        yc_hy = _hyena(zc, Lc, *hy)
        hc = _merge(zc, Lc, (yc_hy, yc_hg, yc_gl), p['w_branch'], p['w_out'], hc, gtc)
    return h, hc


def _ffn_kernel(blk_exp_ref, n_used_ref, x_ref, wg_ref, wu_ref, wd_ref, o_ref, wg_s, wu_s, wd_s):
    i = pl.program_id(0)
    new_expert = jnp.logical_or(i == 0, blk_exp_ref[i] != blk_exp_ref[jnp.maximum(i - 1, 0)])

    @pl.when(jnp.logical_and(i < n_used_ref[0], new_expert))
    def _():
        wg_s[...] = wg_ref[0].astype(jnp.bfloat16)
        wu_s[...] = wu_ref[0].astype(jnp.bfloat16)
        wd_s[...] = wd_ref[0].astype(jnp.bfloat16)

    @pl.when(i < n_used_ref[0])
    def _():
        x = x_ref[...]
        hg = jnp.dot(x, wg_s[...], preferred_element_type=jnp.float32)
        hu = jnp.dot(x, wu_s[...], preferred_element_type=jnp.float32)
        act = (hg * jax.nn.sigmoid(hg) * hu).astype(jnp.bfloat16)
        o_ref[...] = jnp.dot(act, wd_s[...], preferred_element_type=jnp.float32)

    @pl.when(i >= n_used_ref[0])
    def _():
        o_ref[...] = jnp.zeros_like(o_ref)


def _grouped_ffn(xg, blk_exp, n_used, layer, w_gate, w_up, w_down):
    p_len, d = xg.shape
    n_blk = p_len // MOE_BLOCK
    grid_spec = pltpu.PrefetchScalarGridSpec(
        num_scalar_prefetch=2,
        grid=(n_blk,),
        in_specs=[pl.BlockSpec((MOE_BLOCK, d), lambda i, be, nu: (i, 0)),
                  pl.BlockSpec((None, 1, d, D_FF), lambda i, be, nu: (layer, be[i], 0, 0)),
                  pl.BlockSpec((None, 1, d, D_FF), lambda i, be, nu: (layer, be[i], 0, 0)),
                  pl.BlockSpec((None, 1, D_FF, d), lambda i, be, nu: (layer, be[i], 0, 0))],
        out_specs=pl.BlockSpec((MOE_BLOCK, d), lambda i, be, nu: (i, 0)),
        scratch_shapes=[pltpu.VMEM((d, D_FF), jnp.bfloat16), pltpu.VMEM((d, D_FF), jnp.bfloat16),
                        pltpu.VMEM((D_FF, d), jnp.bfloat16)],
    )
    return pl.pallas_call(
        _ffn_kernel,
        grid_spec=grid_spec,
        out_shape=jax.ShapeDtypeStruct((p_len, d), jnp.float32),
        compiler_params=_cparams("arbitrary"),
        name="moe_grouped_ffn",
    )(blk_exp, n_used, xg, w_gate, w_up, w_down)


ROUTER_COLS = 128


def _hier_moe(h, norm_w, shift, scale, gt, p):
    n, d = h.shape
    pad = ROUTER_COLS - N_GROUPS - N_EXPERTS
    w_r = _bf(jnp.concatenate([p['w_rg'], p['w_re'], jnp.zeros((d, pad), jnp.float32)], axis=1))
    b_r = jnp.concatenate([p['b_rg'], p['b_re'], jnp.zeros((pad,), jnp.float32)]).reshape(1, ROUTER_COLS)
    xb, logits = _norm_mod(h, norm_w, shift, scale, router=(w_r, b_r))
    lg = logits[:, :N_GROUPS]
    p_grp = jax.nn.softmax(lg, axis=-1)
    grp = jnp.argmax(p_grp, axis=-1).astype(jnp.int32)
    p_top = jnp.max(p_grp, axis=-1)
    le = logits[:, N_GROUPS:N_GROUPS + N_EXPERTS].reshape(n, N_GROUPS, EXP_PER_GROUP)
    le = jnp.take_along_axis(le, grp[:, None, None], axis=1)[:, 0]
    top_p, top_i = lax.top_k(jax.nn.softmax(le, axis=-1), TOP_K)
    weight = p_top[:, None] * top_p / jnp.sum(top_p, axis=-1, keepdims=True)
    expert = grp[:, None] * EXP_PER_GROUP + top_i.astype(jnp.int32)
    a = n * TOP_K
    e_flat = expert.reshape(a)
    onehot = (e_flat[:, None] == jnp.arange(N_EXPERTS, dtype=jnp.int32)[None, :]).astype(jnp.int32)
    rank = jnp.take_along_axis(jnp.cumsum(onehot, axis=0) - onehot, e_flat[:, None], axis=1)[:, 0]
    counts = jnp.sum(onehot, axis=0)
    padded = (counts + MOE_BLOCK - 1) // MOE_BLOCK * MOE_BLOCK
    pad_end = jnp.cumsum(padded)
    pad_off = pad_end - padded
    pos = pad_off[e_flat] + rank
    p_len = (a + N_EXPERTS * MOE_BLOCK + MOE_BLOCK - 1) // MOE_BLOCK * MOE_BLOCK
    n_blk = p_len // MOE_BLOCK
    tok_flat = jnp.arange(a, dtype=jnp.int32) // TOP_K
    buf_tok = jnp.zeros((p_len,), jnp.int32).at[pos].set(tok_flat)
    blk_start = jnp.arange(n_blk, dtype=jnp.int32) * MOE_BLOCK
    blk_exp = jnp.minimum(jnp.sum(pad_end[None, :] <= blk_start[:, None], axis=1), N_EXPERTS - 1).astype(jnp.int32)
    n_used = (pad_end[-1:] // MOE_BLOCK).astype(jnp.int32)
    xg = xb[buf_tok]
    y = _grouped_ffn(xg, blk_exp, n_used, p['layer'], p['w_gate'], p['w_up'], p['w_down'])
    return _moe_combine(y, pos, weight, h, gt)


def _combine_kernel(pos_ref, y_hbm, wts_ref, h_ref, gt_ref, o_ref, buf, sem):
    i = pl.program_id(0)
    tokens = h_ref.shape[0]

    def row_copy(step, slot, r, k):
        src = pos_ref[(step * tokens + r) * TOP_K + k]
        return pltpu.make_async_copy(y_hbm.at[pl.ds(src, 1)], buf.at[slot, k, pl.ds(r, 1)], sem.at[slot])

    def issue(step, slot):
        def body(r, carry):
            for k in range(TOP_K):
                row_copy(step, slot, r, k).start()
            return carry

        lax.fori_loop(0, tokens, body, 0)

    @pl.when(i == 0)
    def _():
        issue(0, 0)

    @pl.when(i + 1 < pl.num_programs(0))
    def _():
        issue(i + 1, (i + 1) % 2)

    slot = i % 2
    pltpu.make_async_copy(buf.at[slot], buf.at[slot], sem.at[slot]).wait()
    rows = buf[slot]
    wts = wts_ref[...]
    acc = rows[0] * wts[:, 0:1]
    for k in range(1, TOP_K):
        acc = acc + rows[k] * wts[:, k:k + 1]
    o_ref[...] = h_ref[...] + gt_ref[...] * acc


COMBINE_TOKENS = 128


def _moe_combine(y, pos, wts, h, gt):
    n, d = h.shape
    tokens = min(COMBINE_TOKENS, n)
    grid_spec = pltpu.PrefetchScalarGridSpec(
        num_scalar_prefetch=1,
        grid=(n // tokens,),
        in_specs=[pl.BlockSpec(memory_space=pl.ANY),
                  pl.BlockSpec((tokens, TOP_K), lambda i, pos: (i, 0)),
                  pl.BlockSpec((tokens, d), lambda i, pos: (i, 0)),
                  pl.BlockSpec((1, d), lambda i, pos: (0, 0))],
        out_specs=pl.BlockSpec((tokens, d), lambda i, pos: (i, 0)),
        scratch_shapes=[pltpu.VMEM((2, TOP_K, tokens, d), jnp.float32), pltpu.SemaphoreType.DMA((2,))],
    )
    return pl.pallas_call(
        _combine_kernel,
        grid_spec=grid_spec,
        out_shape=jax.ShapeDtypeStruct((n, d), jnp.float32),
        compiler_params=_cparams("arbitrary"),
        name="moe_combine",
    )(pos, y, wts, h, gt)


def _final_norm_kernel(x_ref, w_ref, o_ref):
    x = x_ref[...]
    o_ref[...] = x * lax.rsqrt(jnp.mean(x * x, axis=-1, keepdims=True) + NORM_EPS) * w_ref[...]


def _final_norm(x, w, tm=512):
    m, d = x.shape
    return pl.pallas_call(
        _final_norm_kernel,
        grid=(m // tm,),
        in_specs=[pl.BlockSpec((tm, d), lambda i: (i, 0)), pl.BlockSpec((1, d), lambda i: (0, 0))],
        out_specs=pl.BlockSpec((tm, d), lambda i: (i, 0)),
        out_shape=jax.ShapeDtypeStruct((m, d), jnp.float32),
        compiler_params=_cparams("parallel"),
        name="final_rmsnorm",
    )(x, w.reshape(1, d))


def kernel(x, c, ctx, c_ctx, w_mod, b_mod, norm_mix_w, norm_ffn_w, w_in, b_in, hy_conv_w, hy_conv_b, hy_f_w1, hy_f_b1, hy_f_w2, hy_f_b2, hy_f_w3, hy_f_b3, hy_f_freq, hy_skip, hg_lb_raw, hg_norm_w, gl_w_a2, gl_b_a, gl_norm_w, w_branch, w_out, w_rg, b_rg, w_re, b_re, w_gate, w_up, w_down, final_norm_w):
    assert x.shape[0] == 1
    depth = w_mod.shape[0]
    lb_all = jnp.cumsum(jax.nn.softmax(hg_lb_raw, axis=0), axis=0)
    lb_all = lb_all - lb_all[:1]
    h, hc = x[0], ctx[0]
    cc = jnp.concatenate([c, c_ctx[None, :]], axis=0)
    for l in range(depth):
        need_ctx = l < depth - 1
        mod = _matmul(_bf(jax.nn.silu(cc)), w_mod, b_mod[l], layer=l)
        sh1, sc1, gt1, sh2, sc2, gt2 = jnp.split(mod[0:1], 6, axis=-1)
        sh1c, sc1c, gt1c, sh2c, sc2c, gt2c = jnp.split(mod[1:2], 6, axis=-1)
        p = dict(w_in=w_in, b_in=b_in[l], hy_conv_w=hy_conv_w[l], hy_conv_b=hy_conv_b[l],
                 hy_f=(hy_f_w1[l], hy_f_b1[l], hy_f_w2[l], hy_f_b2[l], hy_f_w3[l], hy_f_b3[l], hy_f_freq[l]),
                 hy_skip=hy_skip[l], lb=lb_all[l], hg_norm_w=hg_norm_w[l], gl_w_a2=gl_w_a2[l], gl_b_a=gl_b_a[l],
                 gl_norm_w=gl_norm_w[l], w_branch=w_branch[l], w_out=w_out[l], w_rg=w_rg[l], b_rg=b_rg[l],
                 w_re=w_re[l], b_re=b_re[l], layer=l, w_gate=w_gate, w_up=w_up, w_down=w_down)
        u = _norm_mod(h, norm_mix_w[l], sh1, sc1)
        uc = _norm_mod(hc, norm_mix_w[l], sh1c, sc1c)
        h, hc = _mixer(h, hc, u, uc, gt1, gt1c, p, need_ctx)
        h = _hier_moe(h, norm_ffn_w[l], sh2, sc2, gt2, p)
        if need_ctx:
            hc = _hier_moe(hc, norm_ffn_w[l], sh2c, sc2c, gt2c, p)
    return _final_norm(h, final_norm_w)[None]
```

```python
import functools
import math

import jax
import jax.numpy as jnp
import numpy as np
from jax import lax
from jax.experimental import pallas as pl
from jax.experimental.pallas import tpu as pltpu

D_MODEL = 2048
NORM_EPS = 1e-6

HY_C = D_MODEL // 2
HY_EMB = 33
HY_BANDS = (HY_EMB - 1) // 2
HY_DECAY_TARGET = 1e-2
HY_FAST_PCT = 0.3
HY_SLOW_PCT = 1.5
HY_MOD_SHIFT = 0.05

HG_HEADS = 8
HG_DK = 128
HG_DV = 128
HG_K = HG_HEADS * HG_DK
HG_V = HG_HEADS * HG_DV

GL_HEADS = 4
GL_DK = 128
GL_DV = 256
GL_K = GL_HEADS * GL_DK
GL_V = GL_HEADS * GL_DV
GL_RANK = 16
GL_TAU = 16.0

N_BRANCH = 3
HG_COLS = 3 * HG_K + 2 * HG_V
GL_COLS = 2 * GL_K + 2 * GL_V + 2 * GL_RANK
REC_COLS = HG_COLS + GL_COLS
HY_COLS = 3 * HY_C
MERGE_COLS = N_BRANCH * D_MODEL

COL_TILE = 512
GL_Q_OFF = HG_COLS
GL_V_OFF = GL_Q_OFF + 2 * GL_K
GL_A_OFF = GL_V_OFF + 2 * GL_V
HY_OFF = -(-(GL_A_OFF + 2 * GL_RANK) // HY_C) * HY_C
MG_OFF = HY_OFF + HY_COLS
Z_COLS = MG_OFF + MERGE_COLS
assert GL_Q_OFF % GL_K == 0 and GL_V_OFF % GL_V == 0 and GL_A_OFF % 128 == 0 and Z_COLS % COL_TILE == 0
assert MG_OFF % D_MODEL == 0

N_GROUPS = 4
EXP_PER_GROUP = 8
N_EXPERTS = N_GROUPS * EXP_PER_GROUP
TOP_K = 2
D_FF = D_MODEL // 4
MOE_BLOCK = 256

SCAN_CHUNK = 128
LOG2_E = 1.4426950408889634

VMEM_LIMIT_BYTES = 56 * 1024 * 1024


def _cparams(*sem):
    return pltpu.CompilerParams(dimension_semantics=sem, vmem_limit_bytes=VMEM_LIMIT_BYTES)


def _bf(a):
    return a.astype(jnp.bfloat16)


def _mm_kernel(x_ref, w_ref, b_ref, o_ref):
    acc = jnp.dot(x_ref[...], w_ref[...].astype(jnp.bfloat16), preferred_element_type=jnp.float32) + b_ref[...]
    o_ref[...] = acc.astype(o_ref.dtype)


def _matmul(x, w, bias=None, tm=512, tn=COL_TILE, out_dtype=jnp.float32, n_cols=None, layer=None):
    m, k = x.shape
    n = w.shape[-1] if n_cols is None else n_cols
    assert n_cols is None or n_cols % tn == 0
    assert layer is None or n % tn == 0
    tm = min(tm, -(-m // 8) * 8)
    mp = -(-m // tm) * tm
    np_ = -(-n // tn) * tn
    if bias is None:
        bias = jnp.zeros((n,), jnp.float32)
    if mp != m:
        x = jnp.pad(x, ((0, mp - m), (0, 0)))
    if np_ != n:
        w = jnp.pad(w, ((0, 0), (0, np_ - n)))
        bias = jnp.pad(bias, (0, np_ - n))
    if layer is None:
        w_spec = pl.BlockSpec((k, tn), lambda i, j: (0, j))
    else:
        w_spec = pl.BlockSpec((None, k, tn), lambda i, j: (layer, 0, j))
    out = pl.pallas_call(
        _mm_kernel,
        grid=(mp // tm, np_ // tn),
        in_specs=[pl.BlockSpec((tm, k), lambda i, j: (i, 0)),
                  w_spec,
                  pl.BlockSpec((1, tn), lambda i, j: (0, j))],
        out_specs=pl.BlockSpec((tm, tn), lambda i, j: (i, j)),
        out_shape=jax.ShapeDtypeStruct((mp, np_), out_dtype),
        compiler_params=_cparams("parallel", "arbitrary"),
        name="dense_matmul",
    )(x, w, bias.reshape(1, np_))
    if mp != m or np_ != n:
        out = out[:m, :n]
    return out


def _norm_mod_kernel(*refs, with_router):
    if with_router:
        h_ref, w_ref, sh_ref, sc_ref, wr_ref, br_ref, o_ref, lg_ref = refs
    else:
        h_ref, w_ref, sh_ref, sc_ref, o_ref = refs
    x = h_ref[...]
    y = x * lax.rsqrt(jnp.mean(x * x, axis=-1, keepdims=True) + NORM_EPS) * w_ref[...]
    u = (y * (1.0 + sc_ref[...]) + sh_ref[...]).astype(jnp.bfloat16)
    o_ref[...] = u.astype(o_ref.dtype)
    if with_router:
        lg_ref[...] = jnp.dot(u, wr_ref[...], preferred_element_type=jnp.float32) + br_ref[...]


def _norm_mod(h, w, shift, scale, router=None, out_dtype=jnp.bfloat16):
    m, d = h.shape
    tm = min(512, m)
    row = pl.BlockSpec((tm, d), lambda i: (i, 0))
    vec = pl.BlockSpec((1, d), lambda i: (0, 0))
    args = [h, w.reshape(1, d), shift, scale]
    specs = [row, vec, vec, vec]
    out_shape = [jax.ShapeDtypeStruct((m, d), out_dtype)]
    out_specs = [row]
    if router is not None:
        args += list(router)
        specs += [pl.BlockSpec(router[0].shape, lambda i: (0, 0)), pl.BlockSpec(router[1].shape, lambda i: (0, 0))]
        out_shape.append(jax.ShapeDtypeStruct((m, router[0].shape[1]), jnp.float32))
        out_specs.append(pl.BlockSpec((tm, router[0].shape[1]), lambda i: (i, 0)))
    out = pl.pallas_call(
        functools.partial(_norm_mod_kernel, with_router=router is not None),
        grid=(m // tm,),
        in_specs=specs,
        out_specs=out_specs,
        out_shape=out_shape,
        compiler_params=_cparams("parallel"),
        name="norm_modulate",
    )(*args)
    return out if router is not None else out[0]


def _scan_masks(c, reverse):
    t = np.arange(c)
    ms = [np.eye(c, dtype=np.float32)]
    for lvl in range(int(math.log2(c))):
        upper = ((t >> lvl) & 1).astype(bool)
        same = (t[:, None] >> (lvl + 1)) == (t[None, :] >> (lvl + 1))
        m = same & upper[:, None] & (~upper)[None, :]
        ms.append((m.T if reverse else m).astype(np.float32))
    tri = t[None, :] >= t[:, None] if reverse else t[None, :] <= t[:, None]
    return jnp.asarray(np.stack(ms)), jnp.asarray(tri.astype(np.float32), dtype=jnp.bfloat16)


def _level_arg(cum, lvl, reverse):
    c = cum.shape[0]
    blk = 1 << lvl
    if blk >= 8:
        pieces = []
        for gs in range(0, c, 2 * blk):
            ref = cum[gs + blk:gs + blk + 1, :]
            pieces.append(ref - cum[gs:gs + blk, :])
            pieces.append(cum[gs + blk:gs + 2 * blk, :] - ref)
        arg = jnp.concatenate(pieces, axis=0)
    else:
        c3 = cum.reshape(c // 8, 8, cum.shape[1])
        sub = lax.broadcasted_iota(jnp.int32, c3.shape, 1)
        ref_row = ((sub >> lvl) | 1) << lvl
        ref = None
        for r in range(blk, 8, 2 * blk):
            cand = jnp.broadcast_to(c3[:, r:r + 1, :], c3.shape)
            ref = cand if ref is None else jnp.where(ref_row == r, cand, ref)
        upper = ((sub >> lvl) & 1) == 1
        arg = jnp.where(upper, c3 - ref, ref - c3).reshape(cum.shape)
    return -arg if reverse else arg


def _dot_nt(a, b):
    return lax.dot_general(a, b, (((1,), (1,)), ((), ())), preferred_element_type=jnp.float32)


def _dot_tn(a, b):
    return lax.dot_general(a, b, (((0,), (0,)), ((), ())), preferred_element_type=jnp.float32)


def _sigmoid_parts(z):
    e = jnp.exp(-jnp.abs(z))
    r = 1.0 / (1.0 + e)
    return jnp.minimum(z, 0.0) - jnp.log(1.0 + e), jnp.where(z >= 0.0, e * r, r)


def _scan_kernel(*refs, mode, reverse, final, heads, dk, dv, c):
    it = iter(refs)
    q_ref = next(it)
    k_ref = next(it)
    v_ref = next(it)
    if mode == "hg":
        lbp_ref = next(it)
    else:
        a_ref = next(it)
        wa_ref = next(it)
        ba_ref = next(it)
    s0_ref = next(it)
    masks_ref = next(it)
    tri_ref = next(it)
    if final:
        oprev_ref = next(it)
        gate_ref = next(it)
        nw_ref = next(it)
    o_ref = next(it)
    st_ref = next(it)

    @pl.when(pl.program_id(0) == 0)
    def _():
        st_ref[...] = s0_ref[...]

    if mode == "gl":
        la_all = jnp.dot(a_ref[...].astype(jnp.bfloat16), wa_ref[...],
                         preferred_element_type=jnp.float32) + ba_ref[...]
    tri = tri_ref[...]
    tot_row = 0 if reverse else c - 1
    n_lvl = int(math.log2(c))
    for h in range(heads):
        ks = slice(h * dk, (h + 1) * dk)
        vs = slice(h * dv, (h + 1) * dv)
        q = q_ref[:, ks]
        v = v_ref[:, vs].astype(jnp.bfloat16)
        if mode == "hg":
            log_sig, sig_neg = _sigmoid_parts(k_ref[:, ks])
            la = lbp_ref[0:1, ks]
            lbb = lbp_ref[1:2, ks] + log_sig
            g = jnp.maximum(la, lbb) + jnp.log(1.0 + jnp.exp(-jnp.abs(la - lbb)))
            k = lbp_ref[2:3, ks] * sig_neg
            q = q * jax.nn.sigmoid(q)
        else:
            g = _sigmoid_parts(la_all[:, ks])[0] * (1.0 / GL_TAU)
            k = k_ref[:, ks]
            q = q * (dk ** -0.5)
        g = g * LOG2_E
        g1 = g.astype(jnp.bfloat16)
        r1 = g - g1.astype(jnp.float32)
        g2 = r1.astype(jnp.bfloat16)
        g3 = (r1 - g2.astype(jnp.float32)).astype(jnp.bfloat16)
        cum = (jnp.dot(tri, g1, preferred_element_type=jnp.float32)
               + jnp.dot(tri, g2, preferred_element_type=jnp.float32)
               + jnp.dot(tri, g3, preferred_element_type=jnp.float32))
        tot = cum[tot_row:tot_row + 1, :]
        st = st_ref[h]
        o = _dot_nt((q * jnp.exp2(cum)).astype(jnp.bfloat16), st.astype(jnp.bfloat16))
        kt = (k * jnp.exp2(tot - cum)).astype(jnp.bfloat16)
        st_ref[h] = st * jnp.exp2(tot) + _dot_tn(v, kt)
        qb = q.astype(jnp.bfloat16)
        kb = k.astype(jnp.bfloat16)
        scores = masks_ref[0] * _dot_nt(qb, kb)
        for lvl in range(n_lvl):
            e = jnp.exp2(_level_arg(cum, lvl, reverse)).astype(jnp.bfloat16)
            scores = scores + masks_ref[1 + lvl] * _dot_nt(qb * e, kb * e)
        o = o + jnp.dot(scores.astype(jnp.bfloat16), v, preferred_element_type=jnp.float32)
        if final:
            o = o + oprev_ref[:, vs]
            y = o * lax.rsqrt(jnp.mean(o * o, axis=-1, keepdims=True) + NORM_EPS) * nw_ref[...]
            gt = gate_ref[:, vs]
            act = jax.nn.sigmoid(gt) if mode == "hg" else gt * jax.nn.sigmoid(gt)
            o_ref[:, vs] = (y * act).astype(o_ref.dtype)
        else:
            o_ref[:, vs] = o


def _scan_pass(mode, reverse, final, L, srcs, s0, params, final_srcs=(), norm_w=None):
    heads, dk, dv = (HG_HEADS, HG_DK, HG_DV) if mode == "hg" else (GL_HEADS, GL_DK, GL_DV)
    c = min(SCAN_CHUNK, L)
    nb = L // c
    row = (lambda i: nb - 1 - i) if reverse else (lambda i: i)
    masks, tri = _scan_masks(c, reverse)

    def const(shape):
        return pl.BlockSpec(shape, lambda i: (0,) * len(shape))

    def rowblock(width, cb):
        return pl.BlockSpec((c, width), lambda i: (row(i), cb))

    args = [a for a, _, _ in srcs] + list(params) + [s0, masks, tri]
    specs = ([rowblock(w, cb) for _, w, cb in srcs] + [const(p.shape) for p in params]
             + [const(s0.shape), const(masks.shape), const(tri.shape)])
    if final:
        args += [a for a, _, _ in final_srcs] + [norm_w]
        specs += [rowblock(w, cb) for _, w, cb in final_srcs] + [const(norm_w.shape)]
    return pl.pallas_call(
        functools.partial(_scan_kernel, mode=mode, reverse=reverse, final=final, heads=heads, dk=dk, dv=dv, c=c),
        grid=(nb,),
        in_specs=specs,
        out_specs=[pl.BlockSpec((c, heads * dv), lambda i: (row(i), 0)), const((heads, dv, dk))],
        out_shape=[jax.ShapeDtypeStruct((L, heads * dv), jnp.bfloat16 if final else jnp.float32),
                   jax.ShapeDtypeStruct((heads, dv, dk), jnp.float32)],
        compiler_params=_cparams("arbitrary"),
        name=f"scan_{mode}_{'bwd' if reverse else 'fwd'}",
    )(*args)


def _hgrn2(z, L, lb, norm_w, s0_f, s0_b):
    lbp = lambda d: jnp.stack([jnp.log(lb[d]), jnp.log1p(-lb[d]), 1.0 - lb[d]])
    w = HG_K
    o_b, s_b = _scan_pass("hg", True, False, L, [(z, w, 0), (z, w, 2), (z, w, 3)], s0_b, [lbp(1)])
    y, s_f = _scan_pass("hg", False, True, L, [(z, w, 0), (z, w, 1), (z, w, 3)], s0_f, [lbp(0)],
                        final_srcs=[(o_b, HG_V, 0), (z, HG_V, 4)], norm_w=norm_w.reshape(1, HG_DV))
    return y, s_f, s_b


def _gla(z, L, w_a2, b_a, norm_w, s0_f, s0_b):
    def gate_params(d):
        wa = jnp.zeros((128, GL_K), jnp.float32).at[d * GL_RANK:(d + 1) * GL_RANK].set(w_a2[d])
        return [_bf(wa), b_a[d].reshape(1, GL_K)]

    srcs = [(z, GL_K, GL_Q_OFF // GL_K), (z, GL_K, GL_Q_OFF // GL_K + 1), (z, GL_V, GL_V_OFF // GL_V),
            (z, 128, GL_A_OFF // 128)]
    o_b, s_b = _scan_pass("gl", True, False, L, srcs, s0_b, gate_params(1))
    y, s_f = _scan_pass("gl", False, True, L, srcs, s0_f, gate_params(0),
                        final_srcs=[(o_b, GL_V, 0), (z, GL_V, GL_V_OFF // GL_V + 1)], norm_w=norm_w.reshape(1, GL_DV))
    return y, s_f, s_b


HY_FEAT_PAD = 128


def _hy_filter_kernel(f_ref, w1_ref, b1_ref, fq_ref, w2_ref, b2_ref, w3_ref, b3_ref, dl_ref, h_ref, s_ref):
    i = pl.program_id(0)
    f = f_ref[...]
    fq = fq_ref[...]
    a = jnp.sin(fq * (jnp.dot(f.astype(jnp.bfloat16), w1_ref[...], preferred_element_type=jnp.float32) + b1_ref[...]))
    a = jnp.sin(fq * (jnp.dot(a.astype(jnp.bfloat16), w2_ref[...], preferred_element_type=jnp.float32) + b2_ref[...]))
    hh = jnp.dot(a.astype(jnp.bfloat16), w3_ref[...], preferred_element_type=jnp.float32) + b3_ref[...]
    hh = hh * (jnp.exp(-f[:, 0:1] * dl_ref[...]) + HY_MOD_SHIFT)
    h_ref[...] = hh
    part = jnp.sum(jnp.abs(hh).reshape(hh.shape[0] // 8, 8, hh.shape[1]), axis=0)

    @pl.when(i == 0)
    def _():
        s_ref[...] = part

    @pl.when(i > 0)
    def _():
        s_ref[...] += part


def _hyena_filters(L, w1, b1, w2, b2, w3, b3, freq):
    t = jnp.linspace(0.0, 1.0, L, dtype=jnp.float32)[:, None]
    ang = 2.0 * math.pi * jnp.arange(L, dtype=jnp.float32)[:, None] / L
    bands = jnp.linspace(1e-4, HY_BANDS - 1, HY_BANDS, dtype=jnp.float32)[None, :]
    feats = jnp.concatenate([t, jnp.cos(bands * ang), -jnp.sin(bands * ang),
                             jnp.zeros((L, HY_FEAT_PAD - HY_EMB), jnp.float32)], axis=-1)
    deltas = jnp.abs(jnp.linspace(math.log(HY_DECAY_TARGET) / HY_SLOW_PCT, math.log(HY_DECAY_TARGET) / HY_FAST_PCT,
                                  HY_C, dtype=jnp.float32))
    fh = w1.shape[1]
    padm = lambda a, r, c: _bf(jnp.pad(a, ((0, r - a.shape[0]), (0, c - a.shape[1]))))
    padv = lambda a: jnp.pad(a, (0, HY_FEAT_PAD - a.shape[0])).reshape(1, HY_FEAT_PAD)
    tm = min(512, L)
    const = lambda r, c: pl.BlockSpec((r, c), lambda i: (0, 0))
    hfil, sums = pl.pallas_call(
        _hy_filter_kernel,
        grid=(L // tm,),
        in_specs=[pl.BlockSpec((tm, HY_FEAT_PAD), lambda i: (i, 0)),
                  const(HY_FEAT_PAD, HY_FEAT_PAD), const(1, HY_FEAT_PAD), const(1, HY_FEAT_PAD),
                  const(HY_FEAT_PAD, HY_FEAT_PAD), const(1, HY_FEAT_PAD),
                  const(HY_FEAT_PAD, 2 * HY_C), const(1, 2 * HY_C), const(1, 2 * HY_C)],
        out_specs=[pl.BlockSpec((tm, 2 * HY_C), lambda i: (i, 0)), const(8, 2 * HY_C)],
        out_shape=[jax.ShapeDtypeStruct((L, 2 * HY_C), jnp.float32), jax.ShapeDtypeStruct((8, 2 * HY_C), jnp.float32)],
        compiler_params=_cparams("arbitrary"),
        name="hyena_filters",
    )(feats, padm(w1, HY_FEAT_PAD, HY_FEAT_PAD), padv(b1), padv(freq), padm(w2, HY_FEAT_PAD, HY_FEAT_PAD), padv(b2),
      padm(w3, HY_FEAT_PAD, 2 * HY_C), b3.reshape(1, 2 * HY_C), jnp.tile(deltas, 2).reshape(1, 2 * HY_C))
    assert fh <= HY_FEAT_PAD
    inorm = 1.0 / jnp.sum(sums, axis=0)
    return hfil, inorm.reshape(2, HY_C)


def _hy_pre_kernel(x0_ref, x1_ref, v_ref, x0p_ref, x1p_ref, vp_ref, x0n_ref, x1n_ref, vn_ref, w_ref, b_ref,
                   vo_ref, x0o_ref):
    i = pl.program_id(0)
    first = i == 0
    last = i == pl.num_programs(0) - 1
    tm = x0_ref.shape[0]
    row = lax.broadcasted_iota(jnp.int32, x0_ref.shape, 0)

    def conv(x_ref, p_ref, n_ref, g):
        x = x_ref[...]
        cs = slice(g * HY_C, (g + 1) * HY_C)
        prev_row = jnp.where(first, 0.0, p_ref[7:8, :])
        next_row = jnp.where(last, 0.0, n_ref[0:1, :])
        xp = jnp.where(row == 0, prev_row, pltpu.roll(x, 1, 0))
        xn = jnp.where(row == tm - 1, next_row, pltpu.roll(x, tm - 1, 0))
        return w_ref[0:1, cs] * xp + w_ref[1:2, cs] * x + w_ref[2:3, cs] * xn + b_ref[0:1, cs]

    x0 = conv(x0_ref, x0p_ref, x0n_ref, 0)
    x1 = conv(x1_ref, x1p_ref, x1n_ref, 1)
    v = conv(v_ref, vp_ref, vn_ref, 2)
    vo_ref[...] = (v * x1).astype(vo_ref.dtype)
    x0o_ref[...] = x0.astype(x0o_ref.dtype)


def _hy_pre(z, L, conv_w, conv_b):
    tm = min(256, L)
    nb8 = L // 8
    cb = HY_OFF // HY_C
    main = lambda g: pl.BlockSpec((tm, HY_C), lambda i: (i, cb + g))
    prev = lambda g: pl.BlockSpec((8, HY_C), lambda i: (jnp.maximum(i * (tm // 8) - 1, 0), cb + g))
    nxt = lambda g: pl.BlockSpec((8, HY_C), lambda i: (jnp.minimum((i + 1) * (tm // 8), nb8 - 1), cb + g))
    const = lambda a: pl.BlockSpec(a.shape, lambda i: (0, 0))
    cbias = conv_b.reshape(1, HY_COLS)
    return pl.pallas_call(
        _hy_pre_kernel,
        grid=(L // tm,),
        in_specs=[main(0), main(1), main(2), prev(0), prev(1), prev(2), nxt(0), nxt(1), nxt(2),
                  const(conv_w), const(cbias)],
        out_specs=[pl.BlockSpec((tm, HY_C), lambda i: (i, 0))] * 2,
        out_shape=[jax.ShapeDtypeStruct((L, HY_C), jnp.float32), jax.ShapeDtypeStruct((L, HY_C), jnp.bfloat16)],
        compiler_params=_cparams("parallel"),
        name="hyena_short_conv",
    )(z, z, z, z, z, z, z, z, z, conv_w, cbias)


HY_N1 = 256
HY_TWO_STAGE_MIN_L = 1024


def _dft_outer_table(n1, cols):
    ang = -2.0 * np.pi * np.outer(np.arange(n1 // 2) + 0.5, np.arange(cols)) / n1
    return jnp.asarray(np.concatenate([np.cos(ang), np.sin(ang)], axis=0), jnp.bfloat16)


def _dft_inner_table(n1, n2):
    j2 = np.arange(n2)
    f_ang = -2.0 * np.pi * np.outer(np.arange(n2), j2) / n2
    tw_ang = -2.0 * np.pi * np.outer(np.arange(n1 // 2) + 0.5, j2) / (n1 * n2)
    fr, fi = jnp.asarray(np.cos(f_ang), jnp.float32), jnp.asarray(np.sin(f_ang), jnp.float32)
    twr, twi = jnp.asarray(np.cos(tw_ang), jnp.float32), jnp.asarray(np.sin(tw_ang), jnp.float32)
    mr = fr[None] * twr[:, None, :] - fi[None] * twi[:, None, :]
    mi = fr[None] * twi[:, None, :] + fi[None] * twr[:, None, :]
    return _bf(jnp.concatenate([jnp.concatenate([mr, -mi], axis=2), jnp.concatenate([mi, mr], axis=2)], axis=1))


def _spectral_product(xv, xh, inorm, half):
    inf, inb = inorm[0:1, :], inorm[1:2, :]
    gr = xh[:half, :HY_C] * inf + xh[:half, HY_C:] * inb
    gi = xh[half:, :HY_C] * inf - xh[half:, HY_C:] * inb
    xr, xi = xv[:half], xv[half:]
    return jnp.concatenate([xr * gr - xi * gi, xr * gi + xi * gr], axis=0).astype(jnp.bfloat16)


def _hy_spec_kernel(r_ref, avr_ref, avi_ref, ahr_ref, ahi_ref, inorm_ref, br_ref, bi_ref):
    r = r_ref[0]
    n2 = avr_ref.shape[1]
    xv = jnp.dot(r, jnp.concatenate([avr_ref[0], avi_ref[0]], axis=0), preferred_element_type=jnp.float32)
    xh = jnp.dot(r, jnp.concatenate([ahr_ref[0], ahi_ref[0]], axis=0), preferred_element_type=jnp.float32)
    b = _dot_tn(r, _spectral_product(xv, xh, inorm_ref[...], n2))
    br_ref[0] = b[:n2].astype(br_ref.dtype)
    bi_ref[0] = b[n2:].astype(bi_ref.dtype)


def _hy_spec(r, av, ah, inorm, n1, n2):
    av3 = av.reshape(n1, n2, HY_C)
    ah3 = ah.reshape(n1, n2, 2 * HY_C)
    h1 = n1 // 2
    out = jax.ShapeDtypeStruct((h1, n2, HY_C), jnp.bfloat16)
    return pl.pallas_call(
        _hy_spec_kernel,
        grid=(h1,),
        in_specs=[pl.BlockSpec((1, 2 * n2, 2 * n2), lambda k: (k, 0, 0)),
                  pl.BlockSpec((1, n2, HY_C), lambda k: (k, 0, 0)),
                  pl.BlockSpec((1, n2, HY_C), lambda k: (k + h1, 0, 0)),
                  pl.BlockSpec((1, n2, 2 * HY_C), lambda k: (k, 0, 0)),
                  pl.BlockSpec((1, n2, 2 * HY_C), lambda k: (k + h1, 0, 0)),
                  pl.BlockSpec((2, HY_C), lambda k: (0, 0))],
        out_specs=[pl.BlockSpec((1, n2, HY_C), lambda k: (k, 0, 0))] * 2,
        out_shape=[out, out],
        compiler_params=_cparams("parallel"),
        name="hyena_spectral",
    )(r, av3, av3, ah3, ah3, inorm)


def _hy_spec_direct_kernel(xv_ref, xh_ref, inorm_ref, yr_ref, yi_ref):
    half = yr_ref.shape[0]
    y = _spectral_product(xv_ref[...].astype(jnp.float32), xh_ref[...].astype(jnp.float32), inorm_ref[...], half)
    yr_ref[...] = y[:half]
    yi_ref[...] = y[half:]


def _hy_spec_direct(xv, xh, inorm, L):
    full = lambda a: pl.BlockSpec(a.shape, lambda i: (0, 0))
    out = jax.ShapeDtypeStruct((L, HY_C), jnp.bfloat16)
    return pl.pallas_call(
        _hy_spec_direct_kernel,
        grid=(1,),
        in_specs=[full(xv), full(xh), full(inorm)],
        out_specs=[pl.BlockSpec((L, HY_C), lambda i: (0, 0))] * 2,
        out_shape=[out, out],
        compiler_params=_cparams("arbitrary"),
        name="hyena_spectral_direct",
    )(xv, xh, inorm)


def _hy_post_kernel(tr_ref, ti_ref, br_ref, bi_ref, v_ref, x0_ref, skip_ref, o_ref, *, scale):
    acc = (jnp.dot(tr_ref[...], br_ref[...], preferred_element_type=jnp.float32)
           + jnp.dot(ti_ref[...], bi_ref[...], preferred_element_type=jnp.float32))
    y = (acc * scale + v_ref[...].astype(jnp.float32) * skip_ref[...]) * x0_ref[...].astype(jnp.float32)
    o_ref[...] = y.astype(o_ref.dtype)


def _hy_post(t_fwd, b_r, b_i, v, x0, skip, L, h1, n2):
    ncol = n2 * HY_C
    tn = min(4096, ncol)
    tr_t = t_fwd[:h1].T
    ti_t = t_fwd[h1:].T
    skip_t = jnp.tile(skip, tn // HY_C).reshape(1, tn)
    col = lambda rows: pl.BlockSpec((rows, tn), lambda j: (0, j))
    rows_out = tr_t.shape[0]
    y = pl.pallas_call(
        functools.partial(_hy_post_kernel, scale=1.0 / L),
        grid=(ncol // tn,),
        in_specs=[pl.BlockSpec(tr_t.shape, lambda j: (0, 0)), pl.BlockSpec(ti_t.shape, lambda j: (0, 0)),
                  col(h1), col(h1), col(rows_out), col(rows_out), pl.BlockSpec((1, tn), lambda j: (0, 0))],
        out_specs=col(rows_out),
        out_shape=jax.ShapeDtypeStruct((rows_out, ncol), jnp.bfloat16),
        compiler_params=_cparams("parallel"),
        name="hyena_inverse",
    )(tr_t, ti_t, b_r.reshape(h1, ncol), b_i.reshape(h1, ncol), v.reshape(rows_out, ncol),
      x0.reshape(rows_out, ncol), skip_t)
    return y.reshape(L, HY_C)


SUBLANE = 8
HY_COL_TILE = 512


def _dft_outer_kron(n1):
    h1 = n1 // 2
    ang = -2.0 * np.pi * np.outer(np.arange(h1) + 0.5, np.arange(h1)) / n1
    eye = np.eye(SUBLANE)
    t_r, t_i = np.cos(ang), np.sin(ang)
    fwd = np.kron(np.concatenate([t_r, t_i], axis=0), eye)
    inv = np.concatenate([np.kron(t_r.T, eye), np.kron(t_i.T, eye)], axis=1)
    return jnp.asarray(fwd, jnp.bfloat16), jnp.asarray(inv, jnp.bfloat16)


def _hy_outer_fwd_kernel(t_ref, x_ref, o_ref):
    x = x_ref[...]
    rows_in, rows_out = x.shape[0], t_ref.shape[0] // SUBLANE
    cw = x.shape[2]
    parts = []
    for s in range(0, x.shape[1], SUBLANE):
        xs = x[:, s:s + SUBLANE, :].reshape(rows_in * SUBLANE, cw).astype(jnp.bfloat16)
        r = jnp.dot(t_ref[...], xs, preferred_element_type=jnp.float32)
        parts.append(r.reshape(rows_out, SUBLANE, cw))
    o_ref[...] = jnp.concatenate(parts, axis=1).astype(o_ref.dtype)


def _hy_outer_fwd(t_kron, x3):
    h1, n2, w = x3.shape
    n1 = 2 * h1
    blk = 2 * SUBLANE
    return pl.pallas_call(
        _hy_outer_fwd_kernel,
        grid=(n2 // blk, w // HY_COL_TILE),
        in_specs=[pl.BlockSpec(t_kron.shape, lambda j, cc: (0, 0)),
                  pl.BlockSpec((h1, blk, HY_COL_TILE), lambda j, cc: (0, j, cc))],
        out_specs=pl.BlockSpec((n1, blk, HY_COL_TILE), lambda j, cc: (0, j, cc)),
        out_shape=jax.ShapeDtypeStruct((n1, n2, w), jnp.bfloat16),
        compiler_params=_cparams("parallel", "parallel"),
        name="hyena_outer_dft",
    )(t_kron, x3)


def _hy_outer_inv_kernel(t_ref, br_ref, bi_ref, v_ref, x0_ref, skip_ref, o_ref, *, scale):
    br = br_ref[...].astype(jnp.float32)
    bi = bi_ref[...].astype(jnp.float32)
    v = v_ref[...]
    x0 = x0_ref[...].astype(jnp.float32)
    h1, _, cw = br.shape
    parts = []
    for s in range(0, br.shape[1], SUBLANE):
        sl = slice(s, s + SUBLANE)
        b = jnp.concatenate([br[:, sl, :].reshape(h1 * SUBLANE, cw), bi[:, sl, :].reshape(h1 * SUBLANE, cw)], axis=0)
        r = jnp.dot(t_ref[...], b.astype(jnp.bfloat16), preferred_element_type=jnp.float32)
        parts.append((r.reshape(h1, SUBLANE, cw) * scale + v[:, sl, :] * skip_ref[...]) * x0[:, sl, :])
    o_ref[...] = jnp.concatenate(parts, axis=1).astype(o_ref.dtype)


def _hy_outer_inv(t_kron_inv, b_r, b_i, v3, x03, skip, L):
    h1, n2, w = b_r.shape
    blk = 2 * SUBLANE
    tile = pl.BlockSpec((h1, blk, HY_COL_TILE), lambda j, cc: (0, j, cc))
    return pl.pallas_call(
        functools.partial(_hy_outer_inv_kernel, scale=1.0 / L),
        grid=(n2 // blk, w // HY_COL_TILE),
        in_specs=[pl.BlockSpec(t_kron_inv.shape, lambda j, cc: (0, 0)), tile, tile, tile, tile,
                  pl.BlockSpec((1, 1, HY_COL_TILE), lambda j, cc: (0, 0, cc))],
        out_specs=tile,
        out_shape=jax.ShapeDtypeStruct((h1, n2, w), jnp.bfloat16),
        compiler_params=_cparams("parallel", "parallel"),
        name="hyena_outer_idft",
    )(t_kron_inv, b_r, b_i, v3, x03, skip.reshape(1, 1, w))


def _hyena(z, L, conv_w, conv_b, fparams, skip):
    v, x0 = _hy_pre(z, L, conv_w, conv_b)
    hfil, inorm = _hyena_filters(L, *fparams)
    if L >= HY_TWO_STAGE_MIN_L:
        n1 = HY_N1
        n2 = 2 * L // n1
        h1 = n1 // 2
        assert n2 % (2 * SUBLANE) == 0
        t_kron, t_kron_inv = _dft_outer_kron(n1)
        v3 = v.reshape(h1, n2, HY_C)
        av = _hy_outer_fwd(t_kron, v3)
        ah = _hy_outer_fwd(t_kron, hfil.reshape(h1, n2, 2 * HY_C))
        b_r, b_i = _hy_spec(_dft_inner_table(n1, n2), av, ah, inorm, n1, n2)
        return _hy_outer_inv(t_kron_inv, b_r, b_i, v3, x0.reshape(h1, n2, HY_C), skip, L).reshape(L, HY_C)
    t_fwd = _dft_outer_table(2 * L, L)
    xv = _matmul(t_fwd, _bf(v), tm=2 * L, tn=HY_C, out_dtype=jnp.bfloat16)
    xh = _matmul(t_fwd, _bf(hfil), tm=2 * L, tn=HY_C, out_dtype=jnp.bfloat16)
    y_r, y_i = _hy_spec_direct(xv, xh, inorm, L)
    return _hy_post(t_fwd, y_r, y_i, v, x0, skip, L, L, 1)


def _merge_kernel(yh_ref, yg_ref, yl_ref, gate_h_ref, gate_g_ref, gate_l_ref, wb_ref, o_ref):
    acc = None
    for br, (y_ref, g_ref) in enumerate(((yh_ref, gate_h_ref), (yg_ref, gate_g_ref), (yl_ref, gate_l_ref))):
        t = jnp.dot(y_ref[...], wb_ref[br], preferred_element_type=jnp.float32) * jax.nn.sigmoid(g_ref[...])
        acc = t if acc is None else acc + t
    o_ref[...] = acc.astype(o_ref.dtype)


def _proj_residual_kernel(m_ref, w_ref, h_ref, gt_ref, o_ref):
    o_ref[...] = h_ref[...] + gt_ref[...] * jnp.dot(m_ref[...], w_ref[...], preferred_element_type=jnp.float32)


def _merge(z, L, ys, w_branch, w_out, h, gt):
    tm = min(256, L)
    gb = MG_OFF // D_MODEL
    ybs = pl.BlockSpec((tm, HY_C), lambda i: (i, 0))
    gate = lambda br: pl.BlockSpec((tm, D_MODEL), lambda i: (i, gb + br))
    row = pl.BlockSpec((tm, D_MODEL), lambda i: (i, 0))
    merged = pl.pallas_call(
        _merge_kernel,
        grid=(L // tm,),
        in_specs=[ybs, ybs, ybs, gate(0), gate(1), gate(2),
                  pl.BlockSpec((N_BRANCH, HY_C, D_MODEL), lambda i: (0, 0, 0))],
        out_specs=row,
        out_shape=jax.ShapeDtypeStruct((L, D_MODEL), jnp.bfloat16),
        compiler_params=_cparams("parallel"),
        name="branch_merge",
    )(ys[0], ys[1], ys[2], z, z, z, _bf(w_branch))
    return pl.pallas_call(
        _proj_residual_kernel,
        grid=(L // tm,),
        in_specs=[row, pl.BlockSpec((D_MODEL, D_MODEL), lambda i: (0, 0)), row,
                  pl.BlockSpec((1, D_MODEL), lambda i: (0, 0))],
        out_specs=row,
        out_shape=jax.ShapeDtypeStruct((L, D_MODEL), jnp.float32),
        compiler_params=_cparams("parallel"),
        name="out_proj_residual",
    )(merged, _bf(w_out), h, gt)


def _pad_cols(a):
    pad = lambda n: jnp.zeros(a.shape[:-1] + (n,), a.dtype)
    return jnp.concatenate([a[..., :REC_COLS], pad(HY_OFF - REC_COLS), a[..., REC_COLS:]], axis=-1)


W_TILE = 1024
W_SHIFT = HY_OFF - REC_COLS
W_LANE_OFF = (-W_SHIFT) % 128
assert 0 < W_SHIFT <= W_TILE and HY_OFF % W_TILE == 0 and Z_COLS % W_TILE == 0


def _w_in_prep_kernel(a_ref, b_ref, o_ref):
    j = pl.program_id(0)
    shifted = j >= HY_OFF // W_TILE

    @pl.when(jnp.logical_not(shifted))
    def _():
        o_ref[...] = a_ref[...].astype(o_ref.dtype)

    @pl.when(shifted)
    def _():
        window = jnp.concatenate([a_ref[...], b_ref[...]], axis=1)
        o_ref[...] = window[:, W_LANE_OFF:W_LANE_OFF + W_TILE].astype(o_ref.dtype)


def _tail_copy_kernel(x_ref, o_ref, *, first_col_block, n_cols):
    col = (pl.program_id(0) + first_col_block) * W_TILE + lax.broadcasted_iota(jnp.int32, o_ref.shape, 1)
    o_ref[...] = jnp.where(col < n_cols, x_ref[...], 0.0)


def _w_in_prep(w_in, layer):
    _, k, n = w_in.shape
    first_shifted = HY_OFF // W_TILE
    n_tiles = Z_COLS // W_TILE
    a_idx = lambda j: jnp.where(j < first_shifted, j, j - (W_SHIFT + W_TILE - 1) // W_TILE)
    n_shift = n_tiles - first_shifted
    assert (W_SHIFT + W_TILE - 1) // W_TILE == 1
    tail_cols = pl.pallas_call(
        functools.partial(_tail_copy_kernel, first_col_block=first_shifted, n_cols=n),
        grid=(n_shift,),
        in_specs=[pl.BlockSpec((None, k, 128), lambda t: (layer, 0, (a_idx(t + first_shifted) + 1) * (W_TILE // 128)))],
        out_specs=pl.BlockSpec((k, 128), lambda t: (0, t)),
        out_shape=jax.ShapeDtypeStruct((k, n_shift * 128), w_in.dtype),
        compiler_params=_cparams("parallel"),
        name="w_in_tail_columns",
    )(w_in)
    return pl.pallas_call(
        _w_in_prep_kernel,
        grid=(n_tiles,),
        in_specs=[pl.BlockSpec((None, k, W_TILE), lambda j: (layer, 0, a_idx(j))),
                  pl.BlockSpec((k, 128), lambda j: (0, jnp.maximum(j - first_shifted, 0)))],
        out_specs=pl.BlockSpec((k, W_TILE), lambda j: (0, j)),
        out_shape=jax.ShapeDtypeStruct((k, Z_COLS), jnp.bfloat16),
        compiler_params=_cparams("parallel"),
        name="w_in_relayout",
    )(w_in, tail_cols)


def _mixer(h, hc, u, uc, gt, gtc, p, need_ctx):
    L, Lc = u.shape[0], uc.shape[0]
    w_in = _w_in_prep(p['w_in'], p['layer'])
    b_in = _pad_cols(p['b_in'])
    z = _matmul(u, w_in, b_in, tm=1024, tn=W_TILE)
    ncol = Z_COLS if need_ctx else HY_OFF
    zc = _matmul(uc, w_in, b_in[:ncol], tm=1024, tn=W_TILE, n_cols=ncol)
    zeros = lambda hd, dk, dv: jnp.zeros((hd, dv, dk), jnp.float32)
    yc_hg, hg_sf, hg_sb = _hgrn2(zc, Lc, p['lb'], p['hg_norm_w'],
                                 zeros(HG_HEADS, HG_DK, HG_DV), zeros(HG_HEADS, HG_DK, HG_DV))
    yc_gl, gl_sf, gl_sb = _gla(zc, Lc, p['gl_w_a2'], p['gl_b_a'], p['gl_norm_w'],
                               zeros(GL_HEADS, GL_DK, GL_DV), zeros(GL_HEADS, GL_DK, GL_DV))
    y_hg, _, _ = _hgrn2(z, L, p['lb'], p['hg_norm_w'], hg_sf, hg_sb)
    y_gl, _, _ = _gla(z, L, p['gl_w_a2'], p['gl_b_a'], p['gl_norm_w'], gl_sf, gl_sb)
    hy = (p['hy_conv_w'], p['hy_conv_b'], p['hy_f'], p['hy_skip'])
    y_hy = _hyena(z, L, *hy)
    h = _merge(z, L, (y_hy, y_hg, y_gl), p['w_branch'], p['w_out'], h, gt)
    if need_ctx:
        yc_hy = _hyena(zc, Lc, *hy)
        hc = _merge(zc, Lc, (yc_hy, yc_hg, yc_gl), p['w_branch'], p['w_out'], hc, gtc)
    return h, hc


GATHER_UNROLL = 8


def _ffn_kernel(blk_exp_ref, n_used_ref, tok_ref, x_hbm, wg_ref, wu_ref, wd_ref, o_ref, xbuf, sem, wg_s, wu_s, wd_s):
    i = pl.program_id(0)
    n_used = n_used_ref[0]
    rows = o_ref.shape[0]

    def issue(step, slot):
        def body(r, carry):
            src = tok_ref[step * rows + r]
            pltpu.make_async_copy(x_hbm.at[pl.ds(src, 1)], xbuf.at[slot, pl.ds(r, 1)], sem.at[slot]).start()
            return carry

        lax.fori_loop(0, rows, body, 0, unroll=GATHER_UNROLL)

    @pl.when(jnp.logical_and(i == 0, n_used > 0))
    def _():
        issue(0, 0)

    @pl.when(i + 1 < n_used)
    def _():
        issue(i + 1, (i + 1) % 2)

    new_expert = jnp.logical_or(i == 0, blk_exp_ref[i] != blk_exp_ref[jnp.maximum(i - 1, 0)])

    @pl.when(jnp.logical_and(i < n_used, new_expert))
    def _():
        wg_s[...] = wg_ref[0].astype(jnp.bfloat16)
        wu_s[...] = wu_ref[0].astype(jnp.bfloat16)
        wd_s[...] = wd_ref[0].astype(jnp.bfloat16)

    @pl.when(i < n_used)
    def _():
        slot = i % 2
        pltpu.make_async_copy(xbuf.at[slot], xbuf.at[slot], sem.at[slot]).wait()
        x = xbuf[slot].astype(jnp.bfloat16)
        hg = jnp.dot(x, wg_s[...], preferred_element_type=jnp.float32)
        hu = jnp.dot(x, wu_s[...], preferred_element_type=jnp.float32)
        act = (hg * jax.nn.sigmoid(hg) * hu).astype(jnp.bfloat16)
        o_ref[...] = jnp.dot(act, wd_s[...], preferred_element_type=jnp.float32)

    @pl.when(i >= n_used)
    def _():
        o_ref[...] = jnp.zeros_like(o_ref)


def _grouped_ffn(x, buf_tok, blk_exp, n_used, layer, w_gate, w_up, w_down):
    p_len = buf_tok.shape[0]
    d = x.shape[1]
    n_blk = p_len // MOE_BLOCK
    grid_spec = pltpu.PrefetchScalarGridSpec(
        num_scalar_prefetch=3,
        grid=(n_blk,),
        in_specs=[pl.BlockSpec(memory_space=pl.ANY),
                  pl.BlockSpec((None, 1, d, D_FF), lambda i, be, nu, tk: (layer, be[i], 0, 0)),
                  pl.BlockSpec((None, 1, d, D_FF), lambda i, be, nu, tk: (layer, be[i], 0, 0)),
                  pl.BlockSpec((None, 1, D_FF, d), lambda i, be, nu, tk: (layer, be[i], 0, 0))],
        out_specs=pl.BlockSpec((MOE_BLOCK, d), lambda i, be, nu, tk: (i, 0)),
        scratch_shapes=[pltpu.VMEM((2, MOE_BLOCK, d), jnp.float32), pltpu.SemaphoreType.DMA((2,)),
                        pltpu.VMEM((d, D_FF), jnp.bfloat16), pltpu.VMEM((d, D_FF), jnp.bfloat16),
                        pltpu.VMEM((D_FF, d), jnp.bfloat16)],
    )
    return pl.pallas_call(
        _ffn_kernel,
        grid_spec=grid_spec,
        out_shape=jax.ShapeDtypeStruct((p_len, d), jnp.float32),
        compiler_params=_cparams("arbitrary"),
        name="moe_grouped_ffn",
    )(blk_exp, n_used, buf_tok, x, w_gate, w_up, w_down)


ROUTER_COLS = 128


def _hier_moe(h, norm_w, shift, scale, gt, p):
    n, d = h.shape
    pad = ROUTER_COLS - N_GROUPS - N_EXPERTS
    w_r = _bf(jnp.concatenate([p['w_rg'], p['w_re'], jnp.zeros((d, pad), jnp.float32)], axis=1))
    b_r = jnp.concatenate([p['b_rg'], p['b_re'], jnp.zeros((pad,), jnp.float32)]).reshape(1, ROUTER_COLS)
    xb, logits = _norm_mod(h, norm_w, shift, scale, router=(w_r, b_r), out_dtype=jnp.float32)
    lg = logits[:, :N_GROUPS]
    p_grp = jax.nn.softmax(lg, axis=-1)
    grp = jnp.argmax(p_grp, axis=-1).astype(jnp.int32)
    p_top = jnp.max(p_grp, axis=-1)
    le = logits[:, N_GROUPS:N_GROUPS + N_EXPERTS].reshape(n, N_GROUPS, EXP_PER_GROUP)
    le = jnp.take_along_axis(le, grp[:, None, None], axis=1)[:, 0]
    top_p, top_i = lax.top_k(jax.nn.softmax(le, axis=-1), TOP_K)
    weight = p_top[:, None] * top_p / jnp.sum(top_p, axis=-1, keepdims=True)
    expert = grp[:, None] * EXP_PER_GROUP + top_i.astype(jnp.int32)
    a = n * TOP_K
    e_flat = expert.reshape(a)
    onehot = (e_flat[:, None] == jnp.arange(N_EXPERTS, dtype=jnp.int32)[None, :]).astype(jnp.int32)
    rank = jnp.take_along_axis(jnp.cumsum(onehot, axis=0) - onehot, e_flat[:, None], axis=1)[:, 0]
    counts = jnp.sum(onehot, axis=0)
    padded = (counts + MOE_BLOCK - 1) // MOE_BLOCK * MOE_BLOCK
    pad_end = jnp.cumsum(padded)
    pad_off = pad_end - padded
    pos = pad_off[e_flat] + rank
    p_len = (a + N_EXPERTS * MOE_BLOCK + MOE_BLOCK - 1) // MOE_BLOCK * MOE_BLOCK
    n_blk = p_len // MOE_BLOCK
    tok_flat = jnp.arange(a, dtype=jnp.int32) // TOP_K
    buf_tok = jnp.zeros((p_len,), jnp.int32).at[pos].set(tok_flat)
    blk_start = jnp.arange(n_blk, dtype=jnp.int32) * MOE_BLOCK
    blk_exp = jnp.minimum(jnp.sum(pad_end[None, :] <= blk_start[:, None], axis=1), N_EXPERTS - 1).astype(jnp.int32)
    n_used = (pad_end[-1:] // MOE_BLOCK).astype(jnp.int32)
    y = _grouped_ffn(xb, buf_tok, blk_exp, n_used, p['layer'], p['w_gate'], p['w_up'], p['w_down'])
    return _moe_combine(y, pos, weight, h, gt)


def _combine_kernel(pos_ref, y_hbm, wts_ref, h_ref, gt_ref, o_ref, buf, sem):
    i = pl.program_id(0)
    tokens = h_ref.shape[0]

    def row_copy(step, slot, r, k):
        src = pos_ref[(step * tokens + r) * TOP_K + k]
        return pltpu.make_async_copy(y_hbm.at[pl.ds(src, 1)], buf.at[slot, k, pl.ds(r, 1)], sem.at[slot])

    def issue(step, slot):
        def body(r, carry):
            for k in range(TOP_K):
                row_copy(step, slot, r, k).start()
            return carry

        lax.fori_loop(0, tokens, body, 0)

    @pl.when(i == 0)
    def _():
        issue(0, 0)

    @pl.when(i + 1 < pl.num_programs(0))
    def _():
        issue(i + 1, (i + 1) % 2)

    slot = i % 2
    pltpu.make_async_copy(buf.at[slot], buf.at[slot], sem.at[slot]).wait()
    rows = buf[slot]
    wts = wts_ref[...]
    acc = rows[0] * wts[:, 0:1]
    for k in range(1, TOP_K):
        acc = acc + rows[k] * wts[:, k:k + 1]
    o_ref[...] = h_ref[...] + gt_ref[...] * acc


COMBINE_TOKENS = 128


def _moe_combine(y, pos, wts, h, gt):
    n, d = h.shape
    tokens = min(COMBINE_TOKENS, n)
    grid_spec = pltpu.PrefetchScalarGridSpec(
        num_scalar_prefetch=1,
        grid=(n // tokens,),
        in_specs=[pl.BlockSpec(memory_space=pl.ANY),
                  pl.BlockSpec((tokens, TOP_K), lambda i, pos: (i, 0)),
                  pl.BlockSpec((tokens, d), lambda i, pos: (i, 0)),
                  pl.BlockSpec((1, d), lambda i, pos: (0, 0))],
        out_specs=pl.BlockSpec((tokens, d), lambda i, pos: (i, 0)),
        scratch_shapes=[pltpu.VMEM((2, TOP_K, tokens, d), jnp.float32), pltpu.SemaphoreType.DMA((2,))],
    )
    return pl.pallas_call(
        _combine_kernel,
        grid_spec=grid_spec,
        out_shape=jax.ShapeDtypeStruct((n, d), jnp.float32),
        compiler_params=_cparams("arbitrary"),
        name="moe_combine",
    )(pos, y, wts, h, gt)


def _final_norm_kernel(x_ref, w_ref, o_ref):
    x = x_ref[...]
    o_ref[...] = x * lax.rsqrt(jnp.mean(x * x, axis=-1, keepdims=True) + NORM_EPS) * w_ref[...]


def _final_norm(x, w, tm=512):
    m, d = x.shape
    return pl.pallas_call(
        _final_norm_kernel,
        grid=(m // tm,),
        in_specs=[pl.BlockSpec((tm, d), lambda i: (i, 0)), pl.BlockSpec((1, d), lambda i: (0, 0))],
        out_specs=pl.BlockSpec((tm, d), lambda i: (i, 0)),
        out_shape=jax.ShapeDtypeStruct((m, d), jnp.float32),
        compiler_params=_cparams("parallel"),
        name="final_rmsnorm",
    )(x, w.reshape(1, d))


def kernel(x, c, ctx, c_ctx, w_mod, b_mod, norm_mix_w, norm_ffn_w, w_in, b_in, hy_conv_w, hy_conv_b, hy_f_w1, hy_f_b1, hy_f_w2, hy_f_b2, hy_f_w3, hy_f_b3, hy_f_freq, hy_skip, hg_lb_raw, hg_norm_w, gl_w_a2, gl_b_a, gl_norm_w, w_branch, w_out, w_rg, b_rg, w_re, b_re, w_gate, w_up, w_down, final_norm_w):
    assert x.shape[0] == 1
    depth = w_mod.shape[0]
    lb_all = jnp.cumsum(jax.nn.softmax(hg_lb_raw, axis=0), axis=0)
    lb_all = lb_all - lb_all[:1]
    h, hc = x[0], ctx[0]
    cc = jnp.concatenate([c, c_ctx[None, :]], axis=0)
    for l in range(depth):
        need_ctx = l < depth - 1
        mod = _matmul(_bf(jax.nn.silu(cc)), w_mod, b_mod[l], layer=l)
        sh1, sc1, gt1, sh2, sc2, gt2 = jnp.split(mod[0:1], 6, axis=-1)
        sh1c, sc1c, gt1c, sh2c, sc2c, gt2c = jnp.split(mod[1:2], 6, axis=-1)
        p = dict(w_in=w_in, b_in=b_in[l], hy_conv_w=hy_conv_w[l], hy_conv_b=hy_conv_b[l],
                 hy_f=(hy_f_w1[l], hy_f_b1[l], hy_f_w2[l], hy_f_b2[l], hy_f_w3[l], hy_f_b3[l], hy_f_freq[l]),
                 hy_skip=hy_skip[l], lb=lb_all[l], hg_norm_w=hg_norm_w[l], gl_w_a2=gl_w_a2[l], gl_b_a=gl_b_a[l],
                 gl_norm_w=gl_norm_w[l], w_branch=w_branch[l], w_out=w_out[l], w_rg=w_rg[l], b_rg=b_rg[l],
                 w_re=w_re[l], b_re=b_re[l], layer=l, w_gate=w_gate, w_up=w_up, w_down=w_down)
        u = _norm_mod(h, norm_mix_w[l], sh1, sc1)
        uc = _norm_mod(hc, norm_mix_w[l], sh1c, sc1c)
        h, hc = _mixer(h, hc, u, uc, gt1, gt1c, p, need_ctx)
        h = _hier_moe(h, norm_ffn_w[l], sh2, sc2, gt2, p)
        if need_ctx:
            hc = _hier_moe(hc, norm_ffn_w[l], sh2c, sc2c, gt2c, p)
    return _final_norm(h, final_norm_w)[None]
```

```python
import functools
import math

import jax
import jax.numpy as jnp
import numpy as np
from jax import lax
from jax.experimental import pallas as pl
from jax.experimental.pallas import tpu as pltpu

D_MODEL = 2048
NORM_EPS = 1e-6

HY_C = D_MODEL // 2
HY_EMB = 33
HY_BANDS = (HY_EMB - 1) // 2
HY_DECAY_TARGET = 1e-2
HY_FAST_PCT = 0.3
HY_SLOW_PCT = 1.5
HY_MOD_SHIFT = 0.05

HG_HEADS = 8
HG_DK = 128
HG_DV = 128
HG_K = HG_HEADS * HG_DK
HG_V = HG_HEADS * HG_DV

GL_HEADS = 4
GL_DK = 128
GL_DV = 256
GL_K = GL_HEADS * GL_DK
GL_V = GL_HEADS * GL_DV
GL_RANK = 16
GL_TAU = 16.0

N_BRANCH = 3
HG_COLS = 3 * HG_K + 2 * HG_V
GL_COLS = 2 * GL_K + 2 * GL_V + 2 * GL_RANK
REC_COLS = HG_COLS + GL_COLS
HY_COLS = 3 * HY_C
MERGE_COLS = N_BRANCH * D_MODEL

COL_TILE = 512
GL_Q_OFF = HG_COLS
GL_V_OFF = GL_Q_OFF + 2 * GL_K
GL_A_OFF = GL_V_OFF + 2 * GL_V
HY_OFF = -(-(GL_A_OFF + 2 * GL_RANK) // HY_C) * HY_C
MG_OFF = HY_OFF + HY_COLS
Z_COLS = MG_OFF + MERGE_COLS
assert GL_Q_OFF % GL_K == 0 and GL_V_OFF % GL_V == 0 and GL_A_OFF % 128 == 0 and Z_COLS % COL_TILE == 0
assert MG_OFF % D_MODEL == 0

N_GROUPS = 4
EXP_PER_GROUP = 8
N_EXPERTS = N_GROUPS * EXP_PER_GROUP
TOP_K = 2
D_FF = D_MODEL // 4
MOE_BLOCK = 256

SCAN_CHUNK = 128
LOG2_E = 1.4426950408889634

VMEM_LIMIT_BYTES = 56 * 1024 * 1024


def _cparams(*sem):
    return pltpu.CompilerParams(dimension_semantics=sem, vmem_limit_bytes=VMEM_LIMIT_BYTES)


def _bf(a):
    return a.astype(jnp.bfloat16)


def _mm_kernel(x_ref, w_ref, b_ref, o_ref):
    acc = jnp.dot(x_ref[...], w_ref[...].astype(jnp.bfloat16), preferred_element_type=jnp.float32) + b_ref[...]
    o_ref[...] = acc.astype(o_ref.dtype)


def _matmul(x, w, bias=None, tm=512, tn=COL_TILE, out_dtype=jnp.float32, n_cols=None, layer=None):
    m, k = x.shape
    n = w.shape[-1] if n_cols is None else n_cols
    assert n_cols is None or n_cols % tn == 0
    assert layer is None or n % tn == 0
    tm = min(tm, -(-m // 8) * 8)
    mp = -(-m // tm) * tm
    np_ = -(-n // tn) * tn
    if bias is None:
        bias = jnp.zeros((n,), jnp.float32)
    if mp != m:
        x = jnp.pad(x, ((0, mp - m), (0, 0)))
    if np_ != n:
        w = jnp.pad(w, ((0, 0), (0, np_ - n)))
        bias = jnp.pad(bias, (0, np_ - n))
    if layer is None:
        w_spec = pl.BlockSpec((k, tn), lambda i, j: (0, j))
    else:
        w_spec = pl.BlockSpec((None, k, tn), lambda i, j: (layer, 0, j))
    out = pl.pallas_call(
        _mm_kernel,
        grid=(mp // tm, np_ // tn),
        in_specs=[pl.BlockSpec((tm, k), lambda i, j: (i, 0)),
                  w_spec,
                  pl.BlockSpec((1, tn), lambda i, j: (0, j))],
        out_specs=pl.BlockSpec((tm, tn), lambda i, j: (i, j)),
        out_shape=jax.ShapeDtypeStruct((mp, np_), out_dtype),
        compiler_params=_cparams("parallel", "arbitrary"),
        name="dense_matmul",
    )(x, w, bias.reshape(1, np_))
    if mp != m or np_ != n:
        out = out[:m, :n]
    return out


def _norm_mod_kernel(*refs, with_router):
    if with_router:
        h_ref, w_ref, sh_ref, sc_ref, wr_ref, br_ref, o_ref, lg_ref = refs
    else:
        h_ref, w_ref, sh_ref, sc_ref, o_ref = refs
    x = h_ref[...]
    y = x * lax.rsqrt(jnp.mean(x * x, axis=-1, keepdims=True) + NORM_EPS) * w_ref[...]
    u = (y * (1.0 + sc_ref[...]) + sh_ref[...]).astype(jnp.bfloat16)
    o_ref[...] = u.astype(o_ref.dtype)
    if with_router:
        lg_ref[...] = jnp.dot(u, wr_ref[...], preferred_element_type=jnp.float32) + br_ref[...]


def _norm_mod(h, w, shift, scale, router=None, out_dtype=jnp.bfloat16):
    m, d = h.shape
    tm = min(512, m)
    row = pl.BlockSpec((tm, d), lambda i: (i, 0))
    vec = pl.BlockSpec((1, d), lambda i: (0, 0))
    args = [h, w.reshape(1, d), shift, scale]
    specs = [row, vec, vec, vec]
    out_shape = [jax.ShapeDtypeStruct((m, d), out_dtype)]
    out_specs = [row]
    if router is not None:
        args += list(router)
        specs += [pl.BlockSpec(router[0].shape, lambda i: (0, 0)), pl.BlockSpec(router[1].shape, lambda i: (0, 0))]
        out_shape.append(jax.ShapeDtypeStruct((m, router[0].shape[1]), jnp.float32))
        out_specs.append(pl.BlockSpec((tm, router[0].shape[1]), lambda i: (i, 0)))
    out = pl.pallas_call(
        functools.partial(_norm_mod_kernel, with_router=router is not None),
        grid=(m // tm,),
        in_specs=specs,
        out_specs=out_specs,
        out_shape=out_shape,
        compiler_params=_cparams("parallel"),
        name="norm_modulate",
    )(*args)
    return out if router is not None else out[0]


def _scan_masks(c, reverse):
    t = np.arange(c)
    ms = [np.eye(c, dtype=np.float32)]
    for lvl in range(int(math.log2(c))):
        upper = ((t >> lvl) & 1).astype(bool)
        same = (t[:, None] >> (lvl + 1)) == (t[None, :] >> (lvl + 1))
        m = same & upper[:, None] & (~upper)[None, :]
        ms.append((m.T if reverse else m).astype(np.float32))
    tri = t[None, :] >= t[:, None] if reverse else t[None, :] <= t[:, None]
    return jnp.asarray(np.stack(ms)), jnp.asarray(tri.astype(np.float32), dtype=jnp.bfloat16)


def _level_arg(cum, lvl, reverse):
    c = cum.shape[0]
    blk = 1 << lvl
    if blk >= 8:
        pieces = []
        for gs in range(0, c, 2 * blk):
            ref = cum[gs + blk:gs + blk + 1, :]
            pieces.append(ref - cum[gs:gs + blk, :])
            pieces.append(cum[gs + blk:gs + 2 * blk, :] - ref)
        arg = jnp.concatenate(pieces, axis=0)
    else:
        c3 = cum.reshape(c // 8, 8, cum.shape[1])
        sub = lax.broadcasted_iota(jnp.int32, c3.shape, 1)
        ref_row = ((sub >> lvl) | 1) << lvl
        ref = None
        for r in range(blk, 8, 2 * blk):
            cand = jnp.broadcast_to(c3[:, r:r + 1, :], c3.shape)
            ref = cand if ref is None else jnp.where(ref_row == r, cand, ref)
        upper = ((sub >> lvl) & 1) == 1
        arg = jnp.where(upper, c3 - ref, ref - c3).reshape(cum.shape)
    return -arg if reverse else arg


def _dot_nt(a, b):
    return lax.dot_general(a, b, (((1,), (1,)), ((), ())), preferred_element_type=jnp.float32)


def _dot_tn(a, b):
    return lax.dot_general(a, b, (((0,), (0,)), ((), ())), preferred_element_type=jnp.float32)


def _sigmoid_parts(z):
    e = jnp.exp(-jnp.abs(z))
    r = 1.0 / (1.0 + e)
    return jnp.minimum(z, 0.0) - jnp.log(1.0 + e), jnp.where(z >= 0.0, e * r, r)


def _scan_kernel(*refs, mode, reverse, final, heads, dk, dv, c):
    it = iter(refs)
    q_ref = next(it)
    k_ref = next(it)
    v_ref = next(it)
    if mode == "hg":
        lbp_ref = next(it)
    else:
        a_ref = next(it)
        wa_ref = next(it)
        ba_ref = next(it)
    s0_ref = next(it)
    masks_ref = next(it)
    tri_ref = next(it)
    if final:
        oprev_ref = next(it)
        gate_ref = next(it)
        nw_ref = next(it)
    o_ref = next(it)
    st_ref = next(it)

    @pl.when(pl.program_id(0) == 0)
    def _():
        st_ref[...] = s0_ref[...]

    if mode == "gl":
        la_all = jnp.dot(a_ref[...].astype(jnp.bfloat16), wa_ref[...],
                         preferred_element_type=jnp.float32) + ba_ref[...]
    tri = tri_ref[...]
    tot_row = 0 if reverse else c - 1
    n_lvl = int(math.log2(c))
    for h in range(heads):
        ks = slice(h * dk, (h + 1) * dk)
        vs = slice(h * dv, (h + 1) * dv)
        q = q_ref[:, ks]
        v = v_ref[:, vs].astype(jnp.bfloat16)
        if mode == "hg":
            log_sig, sig_neg = _sigmoid_parts(k_ref[:, ks])
            la = lbp_ref[0:1, ks]
            lbb = lbp_ref[1:2, ks] + log_sig
            g = jnp.maximum(la, lbb) + jnp.log(1.0 + jnp.exp(-jnp.abs(la - lbb)))
            k = lbp_ref[2:3, ks] * sig_neg
            q = q * jax.nn.sigmoid(q)
        else:
            g = _sigmoid_parts(la_all[:, ks])[0] * (1.0 / GL_TAU)
            k = k_ref[:, ks]
            q = q * (dk ** -0.5)
        g = g * LOG2_E
        g1 = g.astype(jnp.bfloat16)
        r1 = g - g1.astype(jnp.float32)
        g2 = r1.astype(jnp.bfloat16)
        g3 = (r1 - g2.astype(jnp.float32)).astype(jnp.bfloat16)
        cum = (jnp.dot(tri, g1, preferred_element_type=jnp.float32)
               + jnp.dot(tri, g2, preferred_element_type=jnp.float32)
               + jnp.dot(tri, g3, preferred_element_type=jnp.float32))
        tot = cum[tot_row:tot_row + 1, :]
        st = st_ref[h]
        o = _dot_nt((q * jnp.exp2(cum)).astype(jnp.bfloat16), st.astype(jnp.bfloat16))
        kt = (k * jnp.exp2(tot - cum)).astype(jnp.bfloat16)
        st_ref[h] = st * jnp.exp2(tot) + _dot_tn(v, kt)
        qb = q.astype(jnp.bfloat16)
        kb = k.astype(jnp.bfloat16)
        scores = masks_ref[0] * _dot_nt(qb, kb)
        for lvl in range(n_lvl):
            e = jnp.exp2(_level_arg(cum, lvl, reverse)).astype(jnp.bfloat16)
            scores = scores + masks_ref[1 + lvl] * _dot_nt(qb * e, kb * e)
        o = o + jnp.dot(scores.astype(jnp.bfloat16), v, preferred_element_type=jnp.float32)
        if final:
            o = o + oprev_ref[:, vs]
            y = o * lax.rsqrt(jnp.mean(o * o, axis=-1, keepdims=True) + NORM_EPS) * nw_ref[...]
            gt = gate_ref[:, vs]
            act = jax.nn.sigmoid(gt) if mode == "hg" else gt * jax.nn.sigmoid(gt)
            o_ref[:, vs] = (y * act).astype(o_ref.dtype)
        else:
            o_ref[:, vs] = o


def _scan_pass(mode, reverse, final, L, srcs, s0, params, final_srcs=(), norm_w=None):
    heads, dk, dv = (HG_HEADS, HG_DK, HG_DV) if mode == "hg" else (GL_HEADS, GL_DK, GL_DV)
    c = min(SCAN_CHUNK, L)
    nb = L // c
    row = (lambda i: nb - 1 - i) if reverse else (lambda i: i)
    masks, tri = _scan_masks(c, reverse)

    def const(shape):
        return pl.BlockSpec(shape, lambda i: (0,) * len(shape))

    def rowblock(width, cb):
        return pl.BlockSpec((c, width), lambda i: (row(i), cb))

    args = [a for a, _, _ in srcs] + list(params) + [s0, masks, tri]
    specs = ([rowblock(w, cb) for _, w, cb in srcs] + [const(p.shape) for p in params]
             + [const(s0.shape), const(masks.shape), const(tri.shape)])
    if final:
        args += [a for a, _, _ in final_srcs] + [norm_w]
        specs += [rowblock(w, cb) for _, w, cb in final_srcs] + [const(norm_w.shape)]
    return pl.pallas_call(
        functools.partial(_scan_kernel, mode=mode, reverse=reverse, final=final, heads=heads, dk=dk, dv=dv, c=c),
        grid=(nb,),
        in_specs=specs,
        out_specs=[pl.BlockSpec((c, heads * dv), lambda i: (row(i), 0)), const((heads, dv, dk))],
        out_shape=[jax.ShapeDtypeStruct((L, heads * dv), jnp.bfloat16 if final else jnp.float32),
                   jax.ShapeDtypeStruct((heads, dv, dk), jnp.float32)],
        compiler_params=_cparams("arbitrary"),
        name=f"scan_{mode}_{'bwd' if reverse else 'fwd'}",
    )(*args)


def _hgrn2(z, L, lb, norm_w, s0_f, s0_b):
    lbp = lambda d: jnp.stack([jnp.log(lb[d]), jnp.log1p(-lb[d]), 1.0 - lb[d]])
    w = HG_K
    o_b, s_b = _scan_pass("hg", True, False, L, [(z, w, 0), (z, w, 2), (z, w, 3)], s0_b, [lbp(1)])
    y, s_f = _scan_pass("hg", False, True, L, [(z, w, 0), (z, w, 1), (z, w, 3)], s0_f, [lbp(0)],
                        final_srcs=[(o_b, HG_V, 0), (z, HG_V, 4)], norm_w=norm_w.reshape(1, HG_DV))
    return y, s_f, s_b


def _gla(z, L, w_a2, b_a, norm_w, s0_f, s0_b):
    def gate_params(d):
        wa = jnp.zeros((128, GL_K), jnp.float32).at[d * GL_RANK:(d + 1) * GL_RANK].set(w_a2[d])
        return [_bf(wa), b_a[d].reshape(1, GL_K)]

    srcs = [(z, GL_K, GL_Q_OFF // GL_K), (z, GL_K, GL_Q_OFF // GL_K + 1), (z, GL_V, GL_V_OFF // GL_V),
            (z, 128, GL_A_OFF // 128)]
    o_b, s_b = _scan_pass("gl", True, False, L, srcs, s0_b, gate_params(1))
    y, s_f = _scan_pass("gl", False, True, L, srcs, s0_f, gate_params(0),
                        final_srcs=[(o_b, GL_V, 0), (z, GL_V, GL_V_OFF // GL_V + 1)], norm_w=norm_w.reshape(1, GL_DV))
    return y, s_f, s_b


HY_FEAT_PAD = 128


def _hy_filter_kernel(f_ref, w1_ref, b1_ref, fq_ref, w2_ref, b2_ref, w3_ref, b3_ref, dl_ref, h_ref, s_ref):
    i = pl.program_id(0)
    f = f_ref[...]
    fq = fq_ref[...]
    a = jnp.sin(fq * (jnp.dot(f.astype(jnp.bfloat16), w1_ref[...], preferred_element_type=jnp.float32) + b1_ref[...]))
    a = jnp.sin(fq * (jnp.dot(a.astype(jnp.bfloat16), w2_ref[...], preferred_element_type=jnp.float32) + b2_ref[...]))
    hh = jnp.dot(a.astype(jnp.bfloat16), w3_ref[...], preferred_element_type=jnp.float32) + b3_ref[...]
    hh = hh * (jnp.exp(-f[:, 0:1] * dl_ref[...]) + HY_MOD_SHIFT)
    h_ref[...] = hh
    part = jnp.sum(jnp.abs(hh).reshape(hh.shape[0] // 8, 8, hh.shape[1]), axis=0)

    @pl.when(i == 0)
    def _():
        s_ref[...] = part

    @pl.when(i > 0)
    def _():
        s_ref[...] += part


def _hyena_filters(L, w1, b1, w2, b2, w3, b3, freq):
    t = jnp.linspace(0.0, 1.0, L, dtype=jnp.float32)[:, None]
    ang = 2.0 * math.pi * jnp.arange(L, dtype=jnp.float32)[:, None] / L
    bands = jnp.linspace(1e-4, HY_BANDS - 1, HY_BANDS, dtype=jnp.float32)[None, :]
    feats = jnp.concatenate([t, jnp.cos(bands * ang), -jnp.sin(bands * ang),
                             jnp.zeros((L, HY_FEAT_PAD - HY_EMB), jnp.float32)], axis=-1)
    deltas = jnp.abs(jnp.linspace(math.log(HY_DECAY_TARGET) / HY_SLOW_PCT, math.log(HY_DECAY_TARGET) / HY_FAST_PCT,
                                  HY_C, dtype=jnp.float32))
    fh = w1.shape[1]
    padm = lambda a, r, c: _bf(jnp.pad(a, ((0, r - a.shape[0]), (0, c - a.shape[1]))))
    padv = lambda a: jnp.pad(a, (0, HY_FEAT_PAD - a.shape[0])).reshape(1, HY_FEAT_PAD)
    tm = min(512, L)
    const = lambda r, c: pl.BlockSpec((r, c), lambda i: (0, 0))
    hfil, sums = pl.pallas_call(
        _hy_filter_kernel,
        grid=(L // tm,),
        in_specs=[pl.BlockSpec((tm, HY_FEAT_PAD), lambda i: (i, 0)),
                  const(HY_FEAT_PAD, HY_FEAT_PAD), const(1, HY_FEAT_PAD), const(1, HY_FEAT_PAD),
                  const(HY_FEAT_PAD, HY_FEAT_PAD), const(1, HY_FEAT_PAD),
                  const(HY_FEAT_PAD, 2 * HY_C), const(1, 2 * HY_C), const(1, 2 * HY_C)],
        out_specs=[pl.BlockSpec((tm, 2 * HY_C), lambda i: (i, 0)), const(8, 2 * HY_C)],
        out_shape=[jax.ShapeDtypeStruct((L, 2 * HY_C), jnp.float32), jax.ShapeDtypeStruct((8, 2 * HY_C), jnp.float32)],
        compiler_params=_cparams("arbitrary"),
        name="hyena_filters",
    )(feats, padm(w1, HY_FEAT_PAD, HY_FEAT_PAD), padv(b1), padv(freq), padm(w2, HY_FEAT_PAD, HY_FEAT_PAD), padv(b2),
      padm(w3, HY_FEAT_PAD, 2 * HY_C), b3.reshape(1, 2 * HY_C), jnp.tile(deltas, 2).reshape(1, 2 * HY_C))
    assert fh <= HY_FEAT_PAD
    inorm = 1.0 / jnp.sum(sums, axis=0)
    return hfil, inorm.reshape(2, HY_C)


def _hy_pre_kernel(x0_ref, x1_ref, v_ref, x0p_ref, x1p_ref, vp_ref, x0n_ref, x1n_ref, vn_ref, w_ref, b_ref,
                   vo_ref, x0o_ref):
    i = pl.program_id(0)
    first = i == 0
    last = i == pl.num_programs(0) - 1
    tm = x0_ref.shape[0]
    row = lax.broadcasted_iota(jnp.int32, x0_ref.shape, 0)

    def conv(x_ref, p_ref, n_ref, g):
        x = x_ref[...]
        cs = slice(g * HY_C, (g + 1) * HY_C)
        prev_row = jnp.where(first, 0.0, p_ref[7:8, :])
        next_row = jnp.where(last, 0.0, n_ref[0:1, :])
        xp = jnp.where(row == 0, prev_row, pltpu.roll(x, 1, 0))
        xn = jnp.where(row == tm - 1, next_row, pltpu.roll(x, tm - 1, 0))
        return w_ref[0:1, cs] * xp + w_ref[1:2, cs] * x + w_ref[2:3, cs] * xn + b_ref[0:1, cs]

    x0 = conv(x0_ref, x0p_ref, x0n_ref, 0)
    x1 = conv(x1_ref, x1p_ref, x1n_ref, 1)
    v = conv(v_ref, vp_ref, vn_ref, 2)
    vo_ref[...] = (v * x1).astype(vo_ref.dtype)
    x0o_ref[...] = x0.astype(x0o_ref.dtype)


def _hy_pre(z, L, conv_w, conv_b):
    tm = min(256, L)
    nb8 = L // 8
    cb = HY_OFF // HY_C
    main = lambda g: pl.BlockSpec((tm, HY_C), lambda i: (i, cb + g))
    prev = lambda g: pl.BlockSpec((8, HY_C), lambda i: (jnp.maximum(i * (tm // 8) - 1, 0), cb + g))
    nxt = lambda g: pl.BlockSpec((8, HY_C), lambda i: (jnp.minimum((i + 1) * (tm // 8), nb8 - 1), cb + g))
    const = lambda a: pl.BlockSpec(a.shape, lambda i: (0, 0))
    cbias = conv_b.reshape(1, HY_COLS)
    return pl.pallas_call(
        _hy_pre_kernel,
        grid=(L // tm,),
        in_specs=[main(0), main(1), main(2), prev(0), prev(1), prev(2), nxt(0), nxt(1), nxt(2),
                  const(conv_w), const(cbias)],
        out_specs=[pl.BlockSpec((tm, HY_C), lambda i: (i, 0))] * 2,
        out_shape=[jax.ShapeDtypeStruct((L, HY_C), jnp.float32), jax.ShapeDtypeStruct((L, HY_C), jnp.bfloat16)],
        compiler_params=_cparams("parallel"),
        name="hyena_short_conv",
    )(z, z, z, z, z, z, z, z, z, conv_w, cbias)


HY_N1 = 256
HY_TWO_STAGE_MIN_L = 1024


def _dft_outer_table(n1, cols):
    ang = -2.0 * np.pi * np.outer(np.arange(n1 // 2) + 0.5, np.arange(cols)) / n1
    return jnp.asarray(np.concatenate([np.cos(ang), np.sin(ang)], axis=0), jnp.bfloat16)


def _dft_inner_table(n1, n2):
    j2 = np.arange(n2)
    f_ang = -2.0 * np.pi * np.outer(np.arange(n2), j2) / n2
    tw_ang = -2.0 * np.pi * np.outer(np.arange(n1 // 2) + 0.5, j2) / (n1 * n2)
    fr, fi = jnp.asarray(np.cos(f_ang), jnp.float32), jnp.asarray(np.sin(f_ang), jnp.float32)
    twr, twi = jnp.asarray(np.cos(tw_ang), jnp.float32), jnp.asarray(np.sin(tw_ang), jnp.float32)
    mr = fr[None] * twr[:, None, :] - fi[None] * twi[:, None, :]
    mi = fr[None] * twi[:, None, :] + fi[None] * twr[:, None, :]
    return _bf(jnp.concatenate([jnp.concatenate([mr, -mi], axis=2), jnp.concatenate([mi, mr], axis=2)], axis=1))


def _spectral_product(xv, xh, inorm, half):
    inf, inb = inorm[0:1, :], inorm[1:2, :]
    gr = xh[:half, :HY_C] * inf + xh[:half, HY_C:] * inb
    gi = xh[half:, :HY_C] * inf - xh[half:, HY_C:] * inb
    xr, xi = xv[:half], xv[half:]
    return jnp.concatenate([xr * gr - xi * gi, xr * gi + xi * gr], axis=0).astype(jnp.bfloat16)


def _hy_spec_kernel(r_ref, avr_ref, avi_ref, ahr_ref, ahi_ref, inorm_ref, br_ref, bi_ref):
    r = r_ref[0]
    n2 = avr_ref.shape[1]
    xv = jnp.dot(r, jnp.concatenate([avr_ref[0], avi_ref[0]], axis=0), preferred_element_type=jnp.float32)
    xh = jnp.dot(r, jnp.concatenate([ahr_ref[0], ahi_ref[0]], axis=0), preferred_element_type=jnp.float32)
    b = _dot_tn(r, _spectral_product(xv, xh, inorm_ref[...], n2))
    br_ref[0] = b[:n2].astype(br_ref.dtype)
    bi_ref[0] = b[n2:].astype(bi_ref.dtype)


def _hy_spec(r, av, ah, inorm, n1, n2):
    av3 = av.reshape(n1, n2, HY_C)
    ah3 = ah.reshape(n1, n2, 2 * HY_C)
    h1 = n1 // 2
    out = jax.ShapeDtypeStruct((h1, n2, HY_C), jnp.bfloat16)
    return pl.pallas_call(
        _hy_spec_kernel,
        grid=(h1,),
        in_specs=[pl.BlockSpec((1, 2 * n2, 2 * n2), lambda k: (k, 0, 0)),
                  pl.BlockSpec((1, n2, HY_C), lambda k: (k, 0, 0)),
                  pl.BlockSpec((1, n2, HY_C), lambda k: (k + h1, 0, 0)),
                  pl.BlockSpec((1, n2, 2 * HY_C), lambda k: (k, 0, 0)),
                  pl.BlockSpec((1, n2, 2 * HY_C), lambda k: (k + h1, 0, 0)),
                  pl.BlockSpec((2, HY_C), lambda k: (0, 0))],
        out_specs=[pl.BlockSpec((1, n2, HY_C), lambda k: (k, 0, 0))] * 2,
        out_shape=[out, out],
        compiler_params=_cparams("parallel"),
        name="hyena_spectral",
    )(r, av3, av3, ah3, ah3, inorm)


def _hy_spec_direct_kernel(xv_ref, xh_ref, inorm_ref, yr_ref, yi_ref):
    half = yr_ref.shape[0]
    y = _spectral_product(xv_ref[...].astype(jnp.float32), xh_ref[...].astype(jnp.float32), inorm_ref[...], half)
    yr_ref[...] = y[:half]
    yi_ref[...] = y[half:]


def _hy_spec_direct(xv, xh, inorm, L):
    full = lambda a: pl.BlockSpec(a.shape, lambda i: (0, 0))
    out = jax.ShapeDtypeStruct((L, HY_C), jnp.bfloat16)
    return pl.pallas_call(
        _hy_spec_direct_kernel,
        grid=(1,),
        in_specs=[full(xv), full(xh), full(inorm)],
        out_specs=[pl.BlockSpec((L, HY_C), lambda i: (0, 0))] * 2,
        out_shape=[out, out],
        compiler_params=_cparams("arbitrary"),
        name="hyena_spectral_direct",
    )(xv, xh, inorm)


def _hy_post_kernel(tr_ref, ti_ref, br_ref, bi_ref, v_ref, x0_ref, skip_ref, o_ref, *, scale):
    acc = (jnp.dot(tr_ref[...], br_ref[...], preferred_element_type=jnp.float32)
           + jnp.dot(ti_ref[...], bi_ref[...], preferred_element_type=jnp.float32))
    y = (acc * scale + v_ref[...].astype(jnp.float32) * skip_ref[...]) * x0_ref[...].astype(jnp.float32)
    o_ref[...] = y.astype(o_ref.dtype)


def _hy_post(t_fwd, b_r, b_i, v, x0, skip, L, h1, n2):
    ncol = n2 * HY_C
    tn = min(4096, ncol)
    tr_t = t_fwd[:h1].T
    ti_t = t_fwd[h1:].T
    skip_t = jnp.tile(skip, tn // HY_C).reshape(1, tn)
    col = lambda rows: pl.BlockSpec((rows, tn), lambda j: (0, j))
    rows_out = tr_t.shape[0]
    y = pl.pallas_call(
        functools.partial(_hy_post_kernel, scale=1.0 / L),
        grid=(ncol // tn,),
        in_specs=[pl.BlockSpec(tr_t.shape, lambda j: (0, 0)), pl.BlockSpec(ti_t.shape, lambda j: (0, 0)),
                  col(h1), col(h1), col(rows_out), col(rows_out), pl.BlockSpec((1, tn), lambda j: (0, 0))],
        out_specs=col(rows_out),
        out_shape=jax.ShapeDtypeStruct((rows_out, ncol), jnp.bfloat16),
        compiler_params=_cparams("parallel"),
        name="hyena_inverse",
    )(tr_t, ti_t, b_r.reshape(h1, ncol), b_i.reshape(h1, ncol), v.reshape(rows_out, ncol),
      x0.reshape(rows_out, ncol), skip_t)
    return y.reshape(L, HY_C)


SUBLANE = 8
HY_COL_TILE = 512


def _dft_outer_kron(n1):
    h1 = n1 // 2
    ang = -2.0 * np.pi * np.outer(np.arange(h1) + 0.5, np.arange(h1)) / n1
    eye = np.eye(SUBLANE)
    t_r, t_i = np.cos(ang), np.sin(ang)
    fwd = np.kron(np.concatenate([t_r, t_i], axis=0), eye)
    inv = np.concatenate([np.kron(t_r.T, eye), np.kron(t_i.T, eye)], axis=1)
    return jnp.asarray(fwd, jnp.bfloat16), jnp.asarray(inv, jnp.bfloat16)


def _hy_outer_fwd_kernel(t_ref, x_ref, o_ref):
    x = x_ref[...]
    rows_in, rows_out = x.shape[0], t_ref.shape[0] // SUBLANE
    cw = x.shape[2]
    parts = []
    for s in range(0, x.shape[1], SUBLANE):
        xs = x[:, s:s + SUBLANE, :].reshape(rows_in * SUBLANE, cw).astype(jnp.bfloat16)
        r = jnp.dot(t_ref[...], xs, preferred_element_type=jnp.float32)
        parts.append(r.reshape(rows_out, SUBLANE, cw))
    o_ref[...] = jnp.concatenate(parts, axis=1).astype(o_ref.dtype)


def _hy_outer_fwd(t_kron, x3):
    h1, n2, w = x3.shape
    n1 = 2 * h1
    blk = 2 * SUBLANE
    return pl.pallas_call(
        _hy_outer_fwd_kernel,
        grid=(n2 // blk, w // HY_COL_TILE),
        in_specs=[pl.BlockSpec(t_kron.shape, lambda j, cc: (0, 0)),
                  pl.BlockSpec((h1, blk, HY_COL_TILE), lambda j, cc: (0, j, cc))],
        out_specs=pl.BlockSpec((n1, blk, HY_COL_TILE), lambda j, cc: (0, j, cc)),
        out_shape=jax.ShapeDtypeStruct((n1, n2, w), jnp.bfloat16),
        compiler_params=_cparams("parallel", "parallel"),
        name="hyena_outer_dft",
    )(t_kron, x3)


def _hy_outer_inv_kernel(t_ref, br_ref, bi_ref, v_ref, x0_ref, skip_ref, o_ref, *, scale):
    br = br_ref[...].astype(jnp.float32)
    bi = bi_ref[...].astype(jnp.float32)
    v = v_ref[...]
    x0 = x0_ref[...].astype(jnp.float32)
    h1, _, cw = br.shape
    parts = []
    for s in range(0, br.shape[1], SUBLANE):
        sl = slice(s, s + SUBLANE)
        b = jnp.concatenate([br[:, sl, :].reshape(h1 * SUBLANE, cw), bi[:, sl, :].reshape(h1 * SUBLANE, cw)], axis=0)
        r = jnp.dot(t_ref[...], b.astype(jnp.bfloat16), preferred_element_type=jnp.float32)
        parts.append((r.reshape(h1, SUBLANE, cw) * scale + v[:, sl, :] * skip_ref[...]) * x0[:, sl, :])
    o_ref[...] = jnp.concatenate(parts, axis=1).astype(o_ref.dtype)


def _hy_outer_inv(t_kron_inv, b_r, b_i, v3, x03, skip, L):
    h1, n2, w = b_r.shape
    blk = 2 * SUBLANE
    tile = pl.BlockSpec((h1, blk, HY_COL_TILE), lambda j, cc: (0, j, cc))
    return pl.pallas_call(
        functools.partial(_hy_outer_inv_kernel, scale=1.0 / L),
        grid=(n2 // blk, w // HY_COL_TILE),
        in_specs=[pl.BlockSpec(t_kron_inv.shape, lambda j, cc: (0, 0)), tile, tile, tile, tile,
                  pl.BlockSpec((1, 1, HY_COL_TILE), lambda j, cc: (0, 0, cc))],
        out_specs=tile,
        out_shape=jax.ShapeDtypeStruct((h1, n2, w), jnp.bfloat16),
        compiler_params=_cparams("parallel", "parallel"),
        name="hyena_outer_idft",
    )(t_kron_inv, b_r, b_i, v3, x03, skip.reshape(1, 1, w))


def _hyena(z, L, conv_w, conv_b, fparams, skip):
    v, x0 = _hy_pre(z, L, conv_w, conv_b)
    hfil, inorm = _hyena_filters(L, *fparams)
    if L >= HY_TWO_STAGE_MIN_L:
        n1 = HY_N1
        n2 = 2 * L // n1
        h1 = n1 // 2
        assert n2 % (2 * SUBLANE) == 0
        t_kron, t_kron_inv = _dft_outer_kron(n1)
        v3 = v.reshape(h1, n2, HY_C)
        av = _hy_outer_fwd(t_kron, v3)
        ah = _hy_outer_fwd(t_kron, hfil.reshape(h1, n2, 2 * HY_C))
        b_r, b_i = _hy_spec(_dft_inner_table(n1, n2), av, ah, inorm, n1, n2)
        return _hy_outer_inv(t_kron_inv, b_r, b_i, v3, x0.reshape(h1, n2, HY_C), skip, L).reshape(L, HY_C)
    t_fwd = _dft_outer_table(2 * L, L)
    xv = _matmul(t_fwd, _bf(v), tm=2 * L, tn=HY_C, out_dtype=jnp.bfloat16)
    xh = _matmul(t_fwd, _bf(hfil), tm=2 * L, tn=HY_C, out_dtype=jnp.bfloat16)
    y_r, y_i = _hy_spec_direct(xv, xh, inorm, L)
    return _hy_post(t_fwd, y_r, y_i, v, x0, skip, L, L, 1)


def _merge_kernel(yh_ref, yg_ref, yl_ref, gate_h_ref, gate_g_ref, gate_l_ref, wb_ref, o_ref):
    acc = None
    for br, (y_ref, g_ref) in enumerate(((yh_ref, gate_h_ref), (yg_ref, gate_g_ref), (yl_ref, gate_l_ref))):
        t = jnp.dot(y_ref[...], wb_ref[br], preferred_element_type=jnp.float32) * jax.nn.sigmoid(g_ref[...])
        acc = t if acc is None else acc + t
    o_ref[...] = acc.astype(o_ref.dtype)


def _proj_residual_kernel(m_ref, w_ref, h_ref, gt_ref, o_ref):
    o_ref[...] = h_ref[...] + gt_ref[...] * jnp.dot(m_ref[...], w_ref[...], preferred_element_type=jnp.float32)


MERGE_ROWS = 512


def _merge(z, L, ys, w_branch, w_out, h, gt):
    tm = min(MERGE_ROWS, L)
    gb = MG_OFF // D_MODEL
    ybs = pl.BlockSpec((tm, HY_C), lambda i: (i, 0))
    gate = lambda br: pl.BlockSpec((tm, D_MODEL), lambda i: (i, gb + br))
    row = pl.BlockSpec((tm, D_MODEL), lambda i: (i, 0))
    merged = pl.pallas_call(
        _merge_kernel,
        grid=(L // tm,),
        in_specs=[ybs, ybs, ybs, gate(0), gate(1), gate(2),
                  pl.BlockSpec((N_BRANCH, HY_C, D_MODEL), lambda i: (0, 0, 0), pipeline_mode=pl.Buffered(1))],
        out_specs=row,
        out_shape=jax.ShapeDtypeStruct((L, D_MODEL), jnp.bfloat16),
        compiler_params=_cparams("parallel"),
        name="branch_merge",
    )(ys[0], ys[1], ys[2], z, z, z, _bf(w_branch))
    return pl.pallas_call(
        _proj_residual_kernel,
        grid=(L // tm,),
        in_specs=[row, pl.BlockSpec((D_MODEL, D_MODEL), lambda i: (0, 0), pipeline_mode=pl.Buffered(1)), row,
                  pl.BlockSpec((1, D_MODEL), lambda i: (0, 0))],
        out_specs=row,
        out_shape=jax.ShapeDtypeStruct((L, D_MODEL), jnp.float32),
        compiler_params=_cparams("parallel"),
        name="out_proj_residual",
    )(merged, _bf(w_out), h, gt)


def _pad_cols(a):
    pad = lambda n: jnp.zeros(a.shape[:-1] + (n,), a.dtype)
    return jnp.concatenate([a[..., :REC_COLS], pad(HY_OFF - REC_COLS), a[..., REC_COLS:]], axis=-1)


W_TILE = 1024
W_SHIFT = HY_OFF - REC_COLS
W_ROW_OFF = W_TILE - W_SHIFT
assert 0 < W_SHIFT <= W_TILE and W_ROW_OFF % 8 == 0 and HY_OFF % W_TILE == 0 and Z_COLS % W_TILE == 0


def _w_in_prep_kernel(a_ref, b_ref, o_ref):
    j = pl.program_id(0)
    shifted = j >= HY_OFF // W_TILE

    @pl.when(jnp.logical_not(shifted))
    def _():
        o_ref[...] = a_ref[...].T.astype(o_ref.dtype)

    @pl.when(shifted)
    def _():
        window = jnp.concatenate([a_ref[...], b_ref[...]], axis=0)
        o_ref[...] = window[W_ROW_OFF:W_ROW_OFF + W_TILE].T.astype(o_ref.dtype)


def _w_in_prep(w_in, layer):
    w_t = jnp.swapaxes(w_in, 1, 2)
    _, n, k = w_t.shape
    first_shifted = HY_OFF // W_TILE
    a_idx = lambda j: jnp.where(j < first_shifted, j, j - 1)
    tail_blocks = W_TILE // W_ROW_OFF
    assert W_TILE % W_ROW_OFF == 0 and n % W_ROW_OFF == 0
    return pl.pallas_call(
        _w_in_prep_kernel,
        grid=(Z_COLS // W_TILE,),
        in_specs=[pl.BlockSpec((None, W_TILE, k), lambda j: (layer, a_idx(j), 0)),
                  pl.BlockSpec((None, W_ROW_OFF, k), lambda j: (layer, (a_idx(j) + 1) * tail_blocks, 0))],
        out_specs=pl.BlockSpec((k, W_TILE), lambda j: (0, j)),
        out_shape=jax.ShapeDtypeStruct((k, Z_COLS), jnp.bfloat16),
        compiler_params=_cparams("parallel"),
        name="w_in_relayout",
    )(w_t, w_t)


def _mixer(h, hc, u, uc, gt, gtc, p, need_ctx):
    L, Lc = u.shape[0], uc.shape[0]
    w_in = _w_in_prep(p['w_in'], p['layer'])
    b_in = _pad_cols(p['b_in'])
    z = _matmul(u, w_in, b_in, tm=1024, tn=W_TILE)
    ncol = Z_COLS if need_ctx else HY_OFF
    zc = _matmul(uc, w_in, b_in[:ncol], tm=1024, tn=W_TILE, n_cols=ncol)
    zeros = lambda hd, dk, dv: jnp.zeros((hd, dv, dk), jnp.float32)
    yc_hg, hg_sf, hg_sb = _hgrn2(zc, Lc, p['lb'], p['hg_norm_w'],
                                 zeros(HG_HEADS, HG_DK, HG_DV), zeros(HG_HEADS, HG_DK, HG_DV))
    yc_gl, gl_sf, gl_sb = _gla(zc, Lc, p['gl_w_a2'], p['gl_b_a'], p['gl_norm_w'],
                               zeros(GL_HEADS, GL_DK, GL_DV), zeros(GL_HEADS, GL_DK, GL_DV))
    y_hg, _, _ = _hgrn2(z, L, p['lb'], p['hg_norm_w'], hg_sf, hg_sb)
    y_gl, _, _ = _gla(z, L, p['gl_w_a2'], p['gl_b_a'], p['gl_norm_w'], gl_sf, gl_sb)
    hy = (p['hy_conv_w'], p['hy_conv_b'], p['hy_f'], p['hy_skip'])
    y_hy = _hyena(z, L, *hy)
    h = _merge(z, L, (y_hy, y_hg, y_gl), p['w_branch'], p['w_out'], h, gt)
    if need_ctx:
        yc_hy = _hyena(zc, Lc, *hy)
        hc = _merge(zc, Lc, (yc_hy, yc_hg, yc_gl), p['w_branch'], p['w_out'], hc, gtc)
    return h, hc


GATHER_UNROLL = 8


def _ffn_kernel(blk_exp_ref, n_used_ref, tok_ref, x_hbm, wg_ref, wu_ref, wd_ref, o_ref, xbuf, sem, wg_s, wu_s, wd_s):
    i = pl.program_id(0)
    n_used = n_used_ref[0]
    rows = o_ref.shape[0]

    def issue(step, slot):
        def body(r, carry):
            src = tok_ref[step * rows + r]
            pltpu.make_async_copy(x_hbm.at[pl.ds(src, 1)], xbuf.at[slot, pl.ds(r, 1)], sem.at[slot]).start()
            return carry

        lax.fori_loop(0, rows, body, 0, unroll=GATHER_UNROLL)

    @pl.when(jnp.logical_and(i == 0, n_used > 0))
    def _():
        issue(0, 0)

    @pl.when(i + 1 < n_used)
    def _():
        issue(i + 1, (i + 1) % 2)

    new_expert = jnp.logical_or(i == 0, blk_exp_ref[i] != blk_exp_ref[jnp.maximum(i - 1, 0)])

    @pl.when(jnp.logical_and(i < n_used, new_expert))
    def _():
        wg_s[...] = wg_ref[0].astype(jnp.bfloat16)
        wu_s[...] = wu_ref[0].astype(jnp.bfloat16)
        wd_s[...] = wd_ref[0].astype(jnp.bfloat16)

    @pl.when(i < n_used)
    def _():
        slot = i % 2
        pltpu.make_async_copy(xbuf.at[slot], xbuf.at[slot], sem.at[slot]).wait()
        x = xbuf[slot].astype(jnp.bfloat16)
        hg = jnp.dot(x, wg_s[...], preferred_element_type=jnp.float32)
        hu = jnp.dot(x, wu_s[...], preferred_element_type=jnp.float32)
        act = (hg * jax.nn.sigmoid(hg) * hu).astype(jnp.bfloat16)
        o_ref[...] = jnp.dot(act, wd_s[...], preferred_element_type=jnp.float32)

    @pl.when(i >= n_used)
    def _():
        o_ref[...] = jnp.zeros_like(o_ref)


def _grouped_ffn(x, buf_tok, blk_exp, n_used, layer, w_gate, w_up, w_down):
    p_len = buf_tok.shape[0]
    d = x.shape[1]
    n_blk = p_len // MOE_BLOCK
    grid_spec = pltpu.PrefetchScalarGridSpec(
        num_scalar_prefetch=3,
        grid=(n_blk,),
        in_specs=[pl.BlockSpec(memory_space=pl.ANY),
                  pl.BlockSpec((None, 1, d, D_FF), lambda i, be, nu, tk: (layer, be[i], 0, 0)),
                  pl.BlockSpec((None, 1, d, D_FF), lambda i, be, nu, tk: (layer, be[i], 0, 0)),
                  pl.BlockSpec((None, 1, D_FF, d), lambda i, be, nu, tk: (layer, be[i], 0, 0))],
        out_specs=pl.BlockSpec((MOE_BLOCK, d), lambda i, be, nu, tk: (i, 0)),
        scratch_shapes=[pltpu.VMEM((2, MOE_BLOCK, d), jnp.float32), pltpu.SemaphoreType.DMA((2,)),
                        pltpu.VMEM((d, D_FF), jnp.bfloat16), pltpu.VMEM((d, D_FF), jnp.bfloat16),
                        pltpu.VMEM((D_FF, d), jnp.bfloat16)],
    )
    return pl.pallas_call(
        _ffn_kernel,
        grid_spec=grid_spec,
        out_shape=jax.ShapeDtypeStruct((p_len, d), jnp.float32),
        compiler_params=_cparams("arbitrary"),
        name="moe_grouped_ffn",
    )(blk_exp, n_used, buf_tok, x, w_gate, w_up, w_down)


ROUTER_COLS = 128


def _hier_moe(h, norm_w, shift, scale, gt, p):
    n, d = h.shape
    pad = ROUTER_COLS - N_GROUPS - N_EXPERTS
    w_r = _bf(jnp.concatenate([p['w_rg'], p['w_re'], jnp.zeros((d, pad), jnp.float32)], axis=1))
    b_r = jnp.concatenate([p['b_rg'], p['b_re'], jnp.zeros((pad,), jnp.float32)]).reshape(1, ROUTER_COLS)
    xb, logits = _norm_mod(h, norm_w, shift, scale, router=(w_r, b_r), out_dtype=jnp.float32)
    lg = logits[:, :N_GROUPS]
    p_grp = jax.nn.softmax(lg, axis=-1)
    grp = jnp.argmax(p_grp, axis=-1).astype(jnp.int32)
    p_top = jnp.max(p_grp, axis=-1)
    le = logits[:, N_GROUPS:N_GROUPS + N_EXPERTS].reshape(n, N_GROUPS, EXP_PER_GROUP)
    le = jnp.take_along_axis(le, grp[:, None, None], axis=1)[:, 0]
    top_p, top_i = lax.top_k(jax.nn.softmax(le, axis=-1), TOP_K)
    weight = p_top[:, None] * top_p / jnp.sum(top_p, axis=-1, keepdims=True)
    expert = grp[:, None] * EXP_PER_GROUP + top_i.astype(jnp.int32)
    a = n * TOP_K
    e_flat = expert.reshape(a)
    onehot = (e_flat[:, None] == jnp.arange(N_EXPERTS, dtype=jnp.int32)[None, :]).astype(jnp.int32)
    rank = jnp.take_along_axis(jnp.cumsum(onehot, axis=0) - onehot, e_flat[:, None], axis=1)[:, 0]
    counts = jnp.sum(onehot, axis=0)
    padded = (counts + MOE_BLOCK - 1) // MOE_BLOCK * MOE_BLOCK
    pad_end = jnp.cumsum(padded)
    pad_off = pad_end - padded
    pos = pad_off[e_flat] + rank
    p_len = (a + N_EXPERTS * MOE_BLOCK + MOE_BLOCK - 1) // MOE_BLOCK * MOE_BLOCK
    n_blk = p_len // MOE_BLOCK
    tok_flat = jnp.arange(a, dtype=jnp.int32) // TOP_K
    buf_tok = (jnp.arange(p_len, dtype=jnp.int32) % n).at[pos].set(tok_flat)
    blk_start = jnp.arange(n_blk, dtype=jnp.int32) * MOE_BLOCK
    blk_exp = jnp.minimum(jnp.sum(pad_end[None, :] <= blk_start[:, None], axis=1), N_EXPERTS - 1).astype(jnp.int32)
    n_used = (pad_end[-1:] // MOE_BLOCK).astype(jnp.int32)
    y = _grouped_ffn(xb, buf_tok, blk_exp, n_used, p['layer'], p['w_gate'], p['w_up'], p['w_down'])
    return _moe_combine(y, pos, weight, h, gt)


def _combine_kernel(pos_ref, y_hbm, wts_ref, h_ref, gt_ref, o_ref, buf, sem):
    i = pl.program_id(0)
    tokens = h_ref.shape[0]

    def row_copy(step, slot, r, k):
        src = pos_ref[(step * tokens + r) * TOP_K + k]
        return pltpu.make_async_copy(y_hbm.at[pl.ds(src, 1)], buf.at[slot, k, pl.ds(r, 1)], sem.at[slot])

    def issue(step, slot):
        def body(r, carry):
            for k in range(TOP_K):
                row_copy(step, slot, r, k).start()
            return carry

        lax.fori_loop(0, tokens, body, 0, unroll=GATHER_UNROLL // TOP_K)

    @pl.when(i == 0)
    def _():
        issue(0, 0)

    @pl.when(i + 1 < pl.num_programs(0))
    def _():
        issue(i + 1, (i + 1) % 2)

    slot = i % 2
    pltpu.make_async_copy(buf.at[slot], buf.at[slot], sem.at[slot]).wait()
    rows = buf[slot]
    wts = wts_ref[...]
    acc = rows[0] * wts[:, 0:1]
    for k in range(1, TOP_K):
        acc = acc + rows[k] * wts[:, k:k + 1]
    o_ref[...] = h_ref[...] + gt_ref[...] * acc


COMBINE_TOKENS = 128


def _moe_combine(y, pos, wts, h, gt):
    n, d = h.shape
    tokens = min(COMBINE_TOKENS, n)
    grid_spec = pltpu.PrefetchScalarGridSpec(
        num_scalar_prefetch=1,
        grid=(n // tokens,),
        in_specs=[pl.BlockSpec(memory_space=pl.ANY),
                  pl.BlockSpec((tokens, TOP_K), lambda i, pos: (i, 0)),
                  pl.BlockSpec((tokens, d), lambda i, pos: (i, 0)),
                  pl.BlockSpec((1, d), lambda i, pos: (0, 0))],
        out_specs=pl.BlockSpec((tokens, d), lambda i, pos: (i, 0)),
        scratch_shapes=[pltpu.VMEM((2, TOP_K, tokens, d), jnp.float32), pltpu.SemaphoreType.DMA((2,))],
    )
    return pl.pallas_call(
        _combine_kernel,
        grid_spec=grid_spec,
        out_shape=jax.ShapeDtypeStruct((n, d), jnp.float32),
        compiler_params=_cparams("arbitrary"),
        name="moe_combine",
    )(pos, y, wts, h, gt)


def _final_norm_kernel(x_ref, w_ref, o_ref):
    x = x_ref[...]
    o_ref[...] = x * lax.rsqrt(jnp.mean(x * x, axis=-1, keepdims=True) + NORM_EPS) * w_ref[...]


def _final_norm(x, w, tm=512):
    m, d = x.shape
    return pl.pallas_call(
        _final_norm_kernel,
        grid=(m // tm,),
        in_specs=[pl.BlockSpec((tm, d), lambda i: (i, 0)), pl.BlockSpec((1, d), lambda i: (0, 0))],
        out_specs=pl.BlockSpec((tm, d), lambda i: (i, 0)),
        out_shape=jax.ShapeDtypeStruct((m, d), jnp.float32),
        compiler_params=_cparams("parallel"),
        name="final_rmsnorm",
    )(x, w.reshape(1, d))


def kernel(x, c, ctx, c_ctx, w_mod, b_mod, norm_mix_w, norm_ffn_w, w_in, b_in, hy_conv_w, hy_conv_b, hy_f_w1, hy_f_b1, hy_f_w2, hy_f_b2, hy_f_w3, hy_f_b3, hy_f_freq, hy_skip, hg_lb_raw, hg_norm_w, gl_w_a2, gl_b_a, gl_norm_w, w_branch, w_out, w_rg, b_rg, w_re, b_re, w_gate, w_up, w_down, final_norm_w):
    assert x.shape[0] == 1
    depth = w_mod.shape[0]
    lb_all = jnp.cumsum(jax.nn.softmax(hg_lb_raw, axis=0), axis=0)
    lb_all = lb_all - lb_all[:1]
    h, hc = x[0], ctx[0]
    cc = jnp.concatenate([c, c_ctx[None, :]], axis=0)
    for l in range(depth):
        need_ctx = l < depth - 1
        mod = _matmul(_bf(jax.nn.silu(cc)), w_mod, b_mod[l], layer=l)
        sh1, sc1, gt1, sh2, sc2, gt2 = jnp.split(mod[0:1], 6, axis=-1)
        sh1c, sc1c, gt1c, sh2c, sc2c, gt2c = jnp.split(mod[1:2], 6, axis=-1)
        p = dict(w_in=w_in, b_in=b_in[l], hy_conv_w=hy_conv_w[l], hy_conv_b=hy_conv_b[l],
                 hy_f=(hy_f_w1[l], hy_f_b1[l], hy_f_w2[l], hy_f_b2[l], hy_f_w3[l], hy_f_b3[l], hy_f_freq[l]),
                 hy_skip=hy_skip[l], lb=lb_all[l], hg_norm_w=hg_norm_w[l], gl_w_a2=gl_w_a2[l], gl_b_a=gl_b_a[l],
                 gl_norm_w=gl_norm_w[l], w_branch=w_branch[l], w_out=w_out[l], w_rg=w_rg[l], b_rg=b_rg[l],
                 w_re=w_re[l], b_re=b_re[l], layer=l, w_gate=w_gate, w_up=w_up, w_down=w_down)
        u = _norm_mod(h, norm_mix_w[l], sh1, sc1)
        uc = _norm_mod(hc, norm_mix_w[l], sh1c, sc1c)
        h, hc = _mixer(h, hc, u, uc, gt1, gt1c, p, need_ctx)
        h = _hier_moe(h, norm_ffn_w[l], sh2, sc2, gt2, p)
        if need_ctx:
            hc = _hier_moe(hc, norm_ffn_w[l], sh2c, sc2c, gt2c, p)
    return _final_norm(h, final_norm_w)[None]
```

```python
import functools
import math

import jax
import jax.numpy as jnp
import numpy as np
from jax import lax
from jax.experimental import pallas as pl
from jax.experimental.pallas import tpu as pltpu

D_MODEL = 2048
NORM_EPS = 1e-6

HY_C = D_MODEL // 2
HY_EMB = 33
HY_BANDS = (HY_EMB - 1) // 2
HY_DECAY_TARGET = 1e-2
HY_FAST_PCT = 0.3
HY_SLOW_PCT = 1.5
HY_MOD_SHIFT = 0.05

HG_HEADS = 8
HG_DK = 128
HG_DV = 128
HG_K = HG_HEADS * HG_DK
HG_V = HG_HEADS * HG_DV

GL_HEADS = 4
GL_DK = 128
GL_DV = 256
GL_K = GL_HEADS * GL_DK
GL_V = GL_HEADS * GL_DV
GL_RANK = 16
GL_TAU = 16.0

N_BRANCH = 3
HG_COLS = 3 * HG_K + 2 * HG_V
GL_COLS = 2 * GL_K + 2 * GL_V + 2 * GL_RANK
REC_COLS = HG_COLS + GL_COLS
HY_COLS = 3 * HY_C
MERGE_COLS = N_BRANCH * D_MODEL

COL_TILE = 512
GL_Q_OFF = HG_COLS
GL_V_OFF = GL_Q_OFF + 2 * GL_K
GL_A_OFF = GL_V_OFF + 2 * GL_V
HY_OFF = -(-(GL_A_OFF + 2 * GL_RANK) // HY_C) * HY_C
MG_OFF = HY_OFF + HY_COLS
Z_COLS = MG_OFF + MERGE_COLS
assert GL_Q_OFF % GL_K == 0 and GL_V_OFF % GL_V == 0 and GL_A_OFF % 128 == 0 and Z_COLS % COL_TILE == 0
assert MG_OFF % D_MODEL == 0

N_GROUPS = 4
EXP_PER_GROUP = 8
N_EXPERTS = N_GROUPS * EXP_PER_GROUP
TOP_K = 2
D_FF = D_MODEL // 4
MOE_BLOCK = 256

SCAN_CHUNK = 128
LOG2_E = 1.4426950408889634

VMEM_LIMIT_BYTES = 56 * 1024 * 1024


def _cparams(*sem):
    return pltpu.CompilerParams(dimension_semantics=sem, vmem_limit_bytes=VMEM_LIMIT_BYTES)


def _bf(a):
    return a.astype(jnp.bfloat16)


def _mm_kernel(x_ref, w_ref, b_ref, o_ref):
    acc = jnp.dot(x_ref[...], w_ref[...].astype(jnp.bfloat16), preferred_element_type=jnp.float32) + b_ref[...]
    o_ref[...] = acc.astype(o_ref.dtype)


def _matmul(x, w, bias=None, tm=512, tn=COL_TILE, out_dtype=jnp.float32, n_cols=None, layer=None):
    m, k = x.shape
    n = w.shape[-1] if n_cols is None else n_cols
    assert n_cols is None or n_cols % tn == 0
    assert layer is None or n % tn == 0
    tm = min(tm, -(-m // 8) * 8)
    mp = -(-m // tm) * tm
    np_ = -(-n // tn) * tn
    if bias is None:
        bias = jnp.zeros((n,), jnp.float32)
    if mp != m:
        x = jnp.pad(x, ((0, mp - m), (0, 0)))
    if np_ != n:
        w = jnp.pad(w, ((0, 0), (0, np_ - n)))
        bias = jnp.pad(bias, (0, np_ - n))
    if layer is None:
        w_spec = pl.BlockSpec((k, tn), lambda i, j: (0, j))
    else:
        w_spec = pl.BlockSpec((None, k, tn), lambda i, j: (layer, 0, j))
    out = pl.pallas_call(
        _mm_kernel,
        grid=(mp // tm, np_ // tn),
        in_specs=[pl.BlockSpec((tm, k), lambda i, j: (i, 0)),
                  w_spec,
                  pl.BlockSpec((1, tn), lambda i, j: (0, j))],
        out_specs=pl.BlockSpec((tm, tn), lambda i, j: (i, j)),
        out_shape=jax.ShapeDtypeStruct((mp, np_), out_dtype),
        compiler_params=_cparams("parallel", "arbitrary"),
        name="dense_matmul",
    )(x, w, bias.reshape(1, np_))
    if mp != m or np_ != n:
        out = out[:m, :n]
    return out


def _norm_mod_kernel(*refs, with_router):
    if with_router:
        h_ref, w_ref, sh_ref, sc_ref, wr_ref, br_ref, o_ref, lg_ref = refs
    else:
        h_ref, w_ref, sh_ref, sc_ref, o_ref = refs
    x = h_ref[...]
    y = x * lax.rsqrt(jnp.mean(x * x, axis=-1, keepdims=True) + NORM_EPS) * w_ref[...]
    u = (y * (1.0 + sc_ref[...]) + sh_ref[...]).astype(jnp.bfloat16)
    o_ref[...] = u.astype(o_ref.dtype)
    if with_router:
        lg_ref[...] = jnp.dot(u, wr_ref[...], preferred_element_type=jnp.float32) + br_ref[...]


def _norm_mod(h, w, shift, scale, router=None, out_dtype=jnp.bfloat16):
    m, d = h.shape
    tm = min(512, m)
    row = pl.BlockSpec((tm, d), lambda i: (i, 0))
    vec = pl.BlockSpec((1, d), lambda i: (0, 0))
    args = [h, w.reshape(1, d), shift, scale]
    specs = [row, vec, vec, vec]
    out_shape = [jax.ShapeDtypeStruct((m, d), out_dtype)]
    out_specs = [row]
    if router is not None:
        args += list(router)
        specs += [pl.BlockSpec(router[0].shape, lambda i: (0, 0)), pl.BlockSpec(router[1].shape, lambda i: (0, 0))]
        out_shape.append(jax.ShapeDtypeStruct((m, router[0].shape[1]), jnp.float32))
        out_specs.append(pl.BlockSpec((tm, router[0].shape[1]), lambda i: (i, 0)))
    out = pl.pallas_call(
        functools.partial(_norm_mod_kernel, with_router=router is not None),
        grid=(m // tm,),
        in_specs=specs,
        out_specs=out_specs,
        out_shape=out_shape,
        compiler_params=_cparams("parallel"),
        name="norm_modulate",
    )(*args)
    return out if router is not None else out[0]


def _scan_masks(c, reverse):
    t = np.arange(c)
    ms = [np.eye(c, dtype=np.float32)]
    for lvl in range(int(math.log2(c))):
        upper = ((t >> lvl) & 1).astype(bool)
        same = (t[:, None] >> (lvl + 1)) == (t[None, :] >> (lvl + 1))
        m = same & upper[:, None] & (~upper)[None, :]
        ms.append((m.T if reverse else m).astype(np.float32))
    tri = t[None, :] >= t[:, None] if reverse else t[None, :] <= t[:, None]
    return jnp.asarray(np.stack(ms)), jnp.asarray(tri.astype(np.float32), dtype=jnp.bfloat16)


def _level_arg(cum, lvl, reverse):
    c = cum.shape[0]
    blk = 1 << lvl
    if blk >= 8:
        pieces = []
        for gs in range(0, c, 2 * blk):
            ref = cum[gs + blk:gs + blk + 1, :]
            pieces.append(ref - cum[gs:gs + blk, :])
            pieces.append(cum[gs + blk:gs + 2 * blk, :] - ref)
        arg = jnp.concatenate(pieces, axis=0)
    else:
        c3 = cum.reshape(c // 8, 8, cum.shape[1])
        sub = lax.broadcasted_iota(jnp.int32, c3.shape, 1)
        ref_row = ((sub >> lvl) | 1) << lvl
        ref = None
        for r in range(blk, 8, 2 * blk):
            cand = jnp.broadcast_to(c3[:, r:r + 1, :], c3.shape)
            ref = cand if ref is None else jnp.where(ref_row == r, cand, ref)
        upper = ((sub >> lvl) & 1) == 1
        arg = jnp.where(upper, c3 - ref, ref - c3).reshape(cum.shape)
    return -arg if reverse else arg


def _dot_nt(a, b):
    return lax.dot_general(a, b, (((1,), (1,)), ((), ())), preferred_element_type=jnp.float32)


def _dot_tn(a, b):
    return lax.dot_general(a, b, (((0,), (0,)), ((), ())), preferred_element_type=jnp.float32)


def _sigmoid_parts(z):
    e = jnp.exp(-jnp.abs(z))
    r = 1.0 / (1.0 + e)
    return jnp.minimum(z, 0.0) - jnp.log(1.0 + e), jnp.where(z >= 0.0, e * r, r)


def _scan_kernel(*refs, mode, reverse, final, heads, dk, dv, c):
    it = iter(refs)
    q_ref = next(it)
    k_ref = next(it)
    v_ref = next(it)
    if mode == "hg":
        lbp_ref = next(it)
    else:
        a_ref = next(it)
        wa_ref = next(it)
        ba_ref = next(it)
    s0_ref = next(it)
    masks_ref = next(it)
    tri_ref = next(it)
    if final:
        oprev_ref = next(it)
        gate_ref = next(it)
        nw_ref = next(it)
    o_ref = next(it)
    st_ref = next(it)

    @pl.when(pl.program_id(0) == 0)
    def _():
        st_ref[...] = s0_ref[...]

    if mode == "gl":
        la_all = jnp.dot(a_ref[...].astype(jnp.bfloat16), wa_ref[...],
                         preferred_element_type=jnp.float32) + ba_ref[...]
    tri = tri_ref[...]
    tot_row = 0 if reverse else c - 1
    n_lvl = int(math.log2(c))
    for h in range(heads):
        ks = slice(h * dk, (h + 1) * dk)
        vs = slice(h * dv, (h + 1) * dv)
        q = q_ref[:, ks]
        v = v_ref[:, vs].astype(jnp.bfloat16)
        if mode == "hg":
            log_sig, sig_neg = _sigmoid_parts(k_ref[:, ks])
            la = lbp_ref[0:1, ks]
            lbb = lbp_ref[1:2, ks] + log_sig
            g = jnp.maximum(la, lbb) + jnp.log(1.0 + jnp.exp(-jnp.abs(la - lbb)))
            k = lbp_ref[2:3, ks] * sig_neg
            q = q * jax.nn.sigmoid(q)
        else:
            g = _sigmoid_parts(la_all[:, ks])[0] * (1.0 / GL_TAU)
            k = k_ref[:, ks]
            q = q * (dk ** -0.5)
        g = g * LOG2_E
        g1 = g.astype(jnp.bfloat16)
        r1 = g - g1.astype(jnp.float32)
        g2 = r1.astype(jnp.bfloat16)
        g3 = (r1 - g2.astype(jnp.float32)).astype(jnp.bfloat16)
        cum = (jnp.dot(tri, g1, preferred_element_type=jnp.float32)
               + jnp.dot(tri, g2, preferred_element_type=jnp.float32)
               + jnp.dot(tri, g3, preferred_element_type=jnp.float32))
        tot = cum[tot_row:tot_row + 1, :]
        st = st_ref[h]
        o = _dot_nt((q * jnp.exp2(cum)).astype(jnp.bfloat16), st.astype(jnp.bfloat16))
        kt = (k * jnp.exp2(tot - cum)).astype(jnp.bfloat16)
        st_ref[h] = st * jnp.exp2(tot) + _dot_tn(v, kt)
        qb = q.astype(jnp.bfloat16)
        kb = k.astype(jnp.bfloat16)
        scores = masks_ref[0] * _dot_nt(qb, kb)
        for lvl in range(n_lvl):
            e = jnp.exp2(_level_arg(cum, lvl, reverse)).astype(jnp.bfloat16)
            scores = scores + masks_ref[1 + lvl] * _dot_nt(qb * e, kb * e)
        o = o + jnp.dot(scores.astype(jnp.bfloat16), v, preferred_element_type=jnp.float32)
        if final:
            o = o + oprev_ref[:, vs]
            y = o * lax.rsqrt(jnp.mean(o * o, axis=-1, keepdims=True) + NORM_EPS) * nw_ref[...]
            gt = gate_ref[:, vs]
            act = jax.nn.sigmoid(gt) if mode == "hg" else gt * jax.nn.sigmoid(gt)
            o_ref[:, vs] = (y * act).astype(o_ref.dtype)
        else:
            o_ref[:, vs] = o


def _scan_pass(mode, reverse, final, L, srcs, s0, params, final_srcs=(), norm_w=None):
    heads, dk, dv = (HG_HEADS, HG_DK, HG_DV) if mode == "hg" else (GL_HEADS, GL_DK, GL_DV)
    c = min(SCAN_CHUNK, L)
    nb = L // c
    row = (lambda i: nb - 1 - i) if reverse else (lambda i: i)
    masks, tri = _scan_masks(c, reverse)

    def const(shape):
        return pl.BlockSpec(shape, lambda i: (0,) * len(shape))

    def rowblock(width, cb):
        return pl.BlockSpec((c, width), lambda i: (row(i), cb))

    args = [a for a, _, _ in srcs] + list(params) + [s0, masks, tri]
    specs = ([rowblock(w, cb) for _, w, cb in srcs] + [const(p.shape) for p in params]
             + [const(s0.shape), const(masks.shape), const(tri.shape)])
    if final:
        args += [a for a, _, _ in final_srcs] + [norm_w]
        specs += [rowblock(w, cb) for _, w, cb in final_srcs] + [const(norm_w.shape)]
    return pl.pallas_call(
        functools.partial(_scan_kernel, mode=mode, reverse=reverse, final=final, heads=heads, dk=dk, dv=dv, c=c),
        grid=(nb,),
        in_specs=specs,
        out_specs=[pl.BlockSpec((c, heads * dv), lambda i: (row(i), 0)), const((heads, dv, dk))],
        out_shape=[jax.ShapeDtypeStruct((L, heads * dv), jnp.bfloat16 if final else jnp.float32),
                   jax.ShapeDtypeStruct((heads, dv, dk), jnp.float32)],
        compiler_params=_cparams("arbitrary"),
        name=f"scan_{mode}_{'bwd' if reverse else 'fwd'}",
    )(*args)


def _hgrn2(z, L, lb, norm_w, s0_f, s0_b):
    lbp = lambda d: jnp.stack([jnp.log(lb[d]), jnp.log1p(-lb[d]), 1.0 - lb[d]])
    w = HG_K
    o_b, s_b = _scan_pass("hg", True, False, L, [(z, w, 0), (z, w, 2), (z, w, 3)], s0_b, [lbp(1)])
    y, s_f = _scan_pass("hg", False, True, L, [(z, w, 0), (z, w, 1), (z, w, 3)], s0_f, [lbp(0)],
                        final_srcs=[(o_b, HG_V, 0), (z, HG_V, 4)], norm_w=norm_w.reshape(1, HG_DV))
    return y, s_f, s_b


def _gla(z, L, w_a2, b_a, norm_w, s0_f, s0_b):
    def gate_params(d):
        wa = jnp.zeros((128, GL_K), jnp.float32).at[d * GL_RANK:(d + 1) * GL_RANK].set(w_a2[d])
        return [_bf(wa), b_a[d].reshape(1, GL_K)]

    srcs = [(z, GL_K, GL_Q_OFF // GL_K), (z, GL_K, GL_Q_OFF // GL_K + 1), (z, GL_V, GL_V_OFF // GL_V),
            (z, 128, GL_A_OFF // 128)]
    o_b, s_b = _scan_pass("gl", True, False, L, srcs, s0_b, gate_params(1))
    y, s_f = _scan_pass("gl", False, True, L, srcs, s0_f, gate_params(0),
                        final_srcs=[(o_b, GL_V, 0), (z, GL_V, GL_V_OFF // GL_V + 1)], norm_w=norm_w.reshape(1, GL_DV))
    return y, s_f, s_b


HY_FEAT_PAD = 128


def _hy_filter_kernel(f_ref, w1_ref, b1_ref, fq_ref, w2_ref, b2_ref, w3_ref, b3_ref, dl_ref, h_ref, s_ref):
    i = pl.program_id(0)
    f = f_ref[...]
    fq = fq_ref[...]
    a = jnp.sin(fq * (jnp.dot(f.astype(jnp.bfloat16), w1_ref[...], preferred_element_type=jnp.float32) + b1_ref[...]))
    a = jnp.sin(fq * (jnp.dot(a.astype(jnp.bfloat16), w2_ref[...], preferred_element_type=jnp.float32) + b2_ref[...]))
    hh = jnp.dot(a.astype(jnp.bfloat16), w3_ref[...], preferred_element_type=jnp.float32) + b3_ref[...]
    hh = hh * (jnp.exp(-f[:, 0:1] * dl_ref[...]) + HY_MOD_SHIFT)
    h_ref[...] = hh
    part = jnp.sum(jnp.abs(hh).reshape(hh.shape[0] // 8, 8, hh.shape[1]), axis=0)

    @pl.when(i == 0)
    def _():
        s_ref[...] = part

    @pl.when(i > 0)
    def _():
        s_ref[...] += part


def _hyena_filters(L, w1, b1, w2, b2, w3, b3, freq):
    t = jnp.linspace(0.0, 1.0, L, dtype=jnp.float32)[:, None]
    ang = 2.0 * math.pi * jnp.arange(L, dtype=jnp.float32)[:, None] / L
    bands = jnp.linspace(1e-4, HY_BANDS - 1, HY_BANDS, dtype=jnp.float32)[None, :]
    feats = jnp.concatenate([t, jnp.cos(bands * ang), -jnp.sin(bands * ang),
                             jnp.zeros((L, HY_FEAT_PAD - HY_EMB), jnp.float32)], axis=-1)
    deltas = jnp.abs(jnp.linspace(math.log(HY_DECAY_TARGET) / HY_SLOW_PCT, math.log(HY_DECAY_TARGET) / HY_FAST_PCT,
                                  HY_C, dtype=jnp.float32))
    fh = w1.shape[1]
    padm = lambda a, r, c: _bf(jnp.pad(a, ((0, r - a.shape[0]), (0, c - a.shape[1]))))
    padv = lambda a: jnp.pad(a, (0, HY_FEAT_PAD - a.shape[0])).reshape(1, HY_FEAT_PAD)
    tm = min(512, L)
    const = lambda r, c: pl.BlockSpec((r, c), lambda i: (0, 0))
    hfil, sums = pl.pallas_call(
        _hy_filter_kernel,
        grid=(L // tm,),
        in_specs=[pl.BlockSpec((tm, HY_FEAT_PAD), lambda i: (i, 0)),
                  const(HY_FEAT_PAD, HY_FEAT_PAD), const(1, HY_FEAT_PAD), const(1, HY_FEAT_PAD),
                  const(HY_FEAT_PAD, HY_FEAT_PAD), const(1, HY_FEAT_PAD),
                  const(HY_FEAT_PAD, 2 * HY_C), const(1, 2 * HY_C), const(1, 2 * HY_C)],
        out_specs=[pl.BlockSpec((tm, 2 * HY_C), lambda i: (i, 0)), const(8, 2 * HY_C)],
        out_shape=[jax.ShapeDtypeStruct((L, 2 * HY_C), jnp.float32), jax.ShapeDtypeStruct((8, 2 * HY_C), jnp.float32)],
        compiler_params=_cparams("arbitrary"),
        name="hyena_filters",
    )(feats, padm(w1, HY_FEAT_PAD, HY_FEAT_PAD), padv(b1), padv(freq), padm(w2, HY_FEAT_PAD, HY_FEAT_PAD), padv(b2),
      padm(w3, HY_FEAT_PAD, 2 * HY_C), b3.reshape(1, 2 * HY_C), jnp.tile(deltas, 2).reshape(1, 2 * HY_C))
    assert fh <= HY_FEAT_PAD
    inorm = 1.0 / jnp.sum(sums, axis=0)
    return hfil, inorm.reshape(2, HY_C)


def _hy_pre_kernel(x0_ref, x1_ref, v_ref, x0p_ref, x1p_ref, vp_ref, x0n_ref, x1n_ref, vn_ref, w_ref, b_ref,
                   vo_ref, x0o_ref):
    i = pl.program_id(0)
    first = i == 0
    last = i == pl.num_programs(0) - 1
    tm = x0_ref.shape[0]
    row = lax.broadcasted_iota(jnp.int32, x0_ref.shape, 0)

    def conv(x_ref, p_ref, n_ref, g):
        x = x_ref[...]
        cs = slice(g * HY_C, (g + 1) * HY_C)
        prev_row = jnp.where(first, 0.0, p_ref[7:8, :])
        next_row = jnp.where(last, 0.0, n_ref[0:1, :])
        xp = jnp.where(row == 0, prev_row, pltpu.roll(x, 1, 0))
        xn = jnp.where(row == tm - 1, next_row, pltpu.roll(x, tm - 1, 0))
        return w_ref[0:1, cs] * xp + w_ref[1:2, cs] * x + w_ref[2:3, cs] * xn + b_ref[0:1, cs]

    x0 = conv(x0_ref, x0p_ref, x0n_ref, 0)
    x1 = conv(x1_ref, x1p_ref, x1n_ref, 1)
    v = conv(v_ref, vp_ref, vn_ref, 2)
    vo_ref[...] = (v * x1).astype(vo_ref.dtype)
    x0o_ref[...] = x0.astype(x0o_ref.dtype)


def _hy_pre(z, L, conv_w, conv_b):
    tm = min(256, L)
    nb8 = L // 8
    cb = HY_OFF // HY_C
    main = lambda g: pl.BlockSpec((tm, HY_C), lambda i: (i, cb + g))
    prev = lambda g: pl.BlockSpec((8, HY_C), lambda i: (jnp.maximum(i * (tm // 8) - 1, 0), cb + g))
    nxt = lambda g: pl.BlockSpec((8, HY_C), lambda i: (jnp.minimum((i + 1) * (tm // 8), nb8 - 1), cb + g))
    const = lambda a: pl.BlockSpec(a.shape, lambda i: (0, 0))
    cbias = conv_b.reshape(1, HY_COLS)
    return pl.pallas_call(
        _hy_pre_kernel,
        grid=(L // tm,),
        in_specs=[main(0), main(1), main(2), prev(0), prev(1), prev(2), nxt(0), nxt(1), nxt(2),
                  const(conv_w), const(cbias)],
        out_specs=[pl.BlockSpec((tm, HY_C), lambda i: (i, 0))] * 2,
        out_shape=[jax.ShapeDtypeStruct((L, HY_C), jnp.float32), jax.ShapeDtypeStruct((L, HY_C), jnp.bfloat16)],
        compiler_params=_cparams("parallel"),
        name="hyena_short_conv",
    )(z, z, z, z, z, z, z, z, z, conv_w, cbias)


HY_N1 = 128
HY_TWO_STAGE_MIN_L = 1024


def _dft_outer_table(n1, cols):
    ang = -2.0 * np.pi * np.outer(np.arange(n1 // 2) + 0.5, np.arange(cols)) / n1
    return jnp.asarray(np.concatenate([np.cos(ang), np.sin(ang)], axis=0), jnp.bfloat16)


def _dft_inner_table(n1, n2):
    j2 = np.arange(n2)
    f_ang = -2.0 * np.pi * np.outer(np.arange(n2), j2) / n2
    tw_ang = -2.0 * np.pi * np.outer(np.arange(n1 // 2) + 0.5, j2) / (n1 * n2)
    fr, fi = jnp.asarray(np.cos(f_ang), jnp.float32), jnp.asarray(np.sin(f_ang), jnp.float32)
    twr, twi = jnp.asarray(np.cos(tw_ang), jnp.float32), jnp.asarray(np.sin(tw_ang), jnp.float32)
    mr = fr[None] * twr[:, None, :] - fi[None] * twi[:, None, :]
    mi = fr[None] * twi[:, None, :] + fi[None] * twr[:, None, :]
    return _bf(jnp.concatenate([jnp.concatenate([mr, -mi], axis=2), jnp.concatenate([mi, mr], axis=2)], axis=1))


def _spectral_product(xv, xh, inorm, half):
    inf, inb = inorm[0:1, :], inorm[1:2, :]
    gr = xh[:half, :HY_C] * inf + xh[:half, HY_C:] * inb
    gi = xh[half:, :HY_C] * inf - xh[half:, HY_C:] * inb
    xr, xi = xv[:half], xv[half:]
    return jnp.concatenate([xr * gr - xi * gi, xr * gi + xi * gr], axis=0).astype(jnp.bfloat16)


def _hy_spec_kernel(r_ref, avr_ref, avi_ref, ahr_ref, ahi_ref, inorm_ref, br_ref, bi_ref):
    r = r_ref[0]
    n2 = avr_ref.shape[1]
    xv = jnp.dot(r, jnp.concatenate([avr_ref[0], avi_ref[0]], axis=0), preferred_element_type=jnp.float32)
    xh = jnp.dot(r, jnp.concatenate([ahr_ref[0], ahi_ref[0]], axis=0), preferred_element_type=jnp.float32)
    b = _dot_tn(r, _spectral_product(xv, xh, inorm_ref[...], n2))
    br_ref[0] = b[:n2].astype(br_ref.dtype)
    bi_ref[0] = b[n2:].astype(bi_ref.dtype)


def _hy_spec(r, av, ah, inorm, n1, n2):
    av3 = av.reshape(n1, n2, HY_C)
    ah3 = ah.reshape(n1, n2, 2 * HY_C)
    h1 = n1 // 2
    out = jax.ShapeDtypeStruct((h1, n2, HY_C), jnp.bfloat16)
    return pl.pallas_call(
        _hy_spec_kernel,
        grid=(h1,),
        in_specs=[pl.BlockSpec((1, 2 * n2, 2 * n2), lambda k: (k, 0, 0)),
                  pl.BlockSpec((1, n2, HY_C), lambda k: (k, 0, 0)),
                  pl.BlockSpec((1, n2, HY_C), lambda k: (k + h1, 0, 0)),
                  pl.BlockSpec((1, n2, 2 * HY_C), lambda k: (k, 0, 0)),
                  pl.BlockSpec((1, n2, 2 * HY_C), lambda k: (k + h1, 0, 0)),
                  pl.BlockSpec((2, HY_C), lambda k: (0, 0))],
        out_specs=[pl.BlockSpec((1, n2, HY_C), lambda k: (k, 0, 0))] * 2,
        out_shape=[out, out],
        compiler_params=_cparams("parallel"),
        name="hyena_spectral",
    )(r, av3, av3, ah3, ah3, inorm)


def _hy_spec_direct_kernel(xv_ref, xh_ref, inorm_ref, yr_ref, yi_ref):
    half = yr_ref.shape[0]
    y = _spectral_product(xv_ref[...].astype(jnp.float32), xh_ref[...].astype(jnp.float32), inorm_ref[...], half)
    yr_ref[...] = y[:half]
    yi_ref[...] = y[half:]


def _hy_spec_direct(xv, xh, inorm, L):
    full = lambda a: pl.BlockSpec(a.shape, lambda i: (0, 0))
    out = jax.ShapeDtypeStruct((L, HY_C), jnp.bfloat16)
    return pl.pallas_call(
        _hy_spec_direct_kernel,
        grid=(1,),
        in_specs=[full(xv), full(xh), full(inorm)],
        out_specs=[pl.BlockSpec((L, HY_C), lambda i: (0, 0))] * 2,
        out_shape=[out, out],
        compiler_params=_cparams("arbitrary"),
        name="hyena_spectral_direct",
    )(xv, xh, inorm)


def _hy_post_kernel(tr_ref, ti_ref, br_ref, bi_ref, v_ref, x0_ref, skip_ref, o_ref, *, scale):
    acc = (jnp.dot(tr_ref[...], br_ref[...], preferred_element_type=jnp.float32)
           + jnp.dot(ti_ref[...], bi_ref[...], preferred_element_type=jnp.float32))
    y = (acc * scale + v_ref[...].astype(jnp.float32) * skip_ref[...]) * x0_ref[...].astype(jnp.float32)
    o_ref[...] = y.astype(o_ref.dtype)


def _hy_post(t_fwd, b_r, b_i, v, x0, skip, L, h1, n2):
    ncol = n2 * HY_C
    tn = min(4096, ncol)
    tr_t = t_fwd[:h1].T
    ti_t = t_fwd[h1:].T
    skip_t = jnp.tile(skip, tn // HY_C).reshape(1, tn)
    col = lambda rows: pl.BlockSpec((rows, tn), lambda j: (0, j))
    rows_out = tr_t.shape[0]
    y = pl.pallas_call(
        functools.partial(_hy_post_kernel, scale=1.0 / L),
        grid=(ncol // tn,),
        in_specs=[pl.BlockSpec(tr_t.shape, lambda j: (0, 0)), pl.BlockSpec(ti_t.shape, lambda j: (0, 0)),
                  col(h1), col(h1), col(rows_out), col(rows_out), pl.BlockSpec((1, tn), lambda j: (0, 0))],
        out_specs=col(rows_out),
        out_shape=jax.ShapeDtypeStruct((rows_out, ncol), jnp.bfloat16),
        compiler_params=_cparams("parallel"),
        name="hyena_inverse",
    )(tr_t, ti_t, b_r.reshape(h1, ncol), b_i.reshape(h1, ncol), v.reshape(rows_out, ncol),
      x0.reshape(rows_out, ncol), skip_t)
    return y.reshape(L, HY_C)


SUBLANE = 8
HY_COL_TILE = 512


def _dft_outer_kron(n1):
    h1 = n1 // 2
    ang = -2.0 * np.pi * np.outer(np.arange(h1) + 0.5, np.arange(h1)) / n1
    eye = np.eye(SUBLANE)
    t_r, t_i = np.cos(ang), np.sin(ang)
    fwd = np.kron(np.concatenate([t_r, t_i], axis=0), eye)
    inv = np.concatenate([np.kron(t_r.T, eye), np.kron(t_i.T, eye)], axis=1)
    return jnp.asarray(fwd, jnp.bfloat16), jnp.asarray(inv, jnp.bfloat16)


def _hy_outer_fwd_kernel(t_ref, x_ref, o_ref):
    x = x_ref[...]
    rows_in, rows_out = x.shape[0], t_ref.shape[0] // SUBLANE
    cw = x.shape[2]
    parts = []
    for s in range(0, x.shape[1], SUBLANE):
        xs = x[:, s:s + SUBLANE, :].reshape(rows_in * SUBLANE, cw).astype(jnp.bfloat16)
        r = jnp.dot(t_ref[...], xs, preferred_element_type=jnp.float32)
        parts.append(r.reshape(rows_out, SUBLANE, cw))
    o_ref[...] = jnp.concatenate(parts, axis=1).astype(o_ref.dtype)


def _hy_outer_fwd(t_kron, x3):
    h1, n2, w = x3.shape
    n1 = 2 * h1
    blk = 2 * SUBLANE
    return pl.pallas_call(
        _hy_outer_fwd_kernel,
        grid=(n2 // blk, w // HY_COL_TILE),
        in_specs=[pl.BlockSpec(t_kron.shape, lambda j, cc: (0, 0)),
                  pl.BlockSpec((h1, blk, HY_COL_TILE), lambda j, cc: (0, j, cc))],
        out_specs=pl.BlockSpec((n1, blk, HY_COL_TILE), lambda j, cc: (0, j, cc)),
        out_shape=jax.ShapeDtypeStruct((n1, n2, w), jnp.bfloat16),
        compiler_params=_cparams("parallel", "parallel"),
        name="hyena_outer_dft",
    )(t_kron, x3)


def _hy_outer_inv_kernel(t_ref, br_ref, bi_ref, v_ref, x0_ref, skip_ref, o_ref, *, scale):
    br = br_ref[...].astype(jnp.float32)
    bi = bi_ref[...].astype(jnp.float32)
    v = v_ref[...]
    x0 = x0_ref[...].astype(jnp.float32)
    h1, _, cw = br.shape
    parts = []
    for s in range(0, br.shape[1], SUBLANE):
        sl = slice(s, s + SUBLANE)
        b = jnp.concatenate([br[:, sl, :].reshape(h1 * SUBLANE, cw), bi[:, sl, :].reshape(h1 * SUBLANE, cw)], axis=0)
        r = jnp.dot(t_ref[...], b.astype(jnp.bfloat16), preferred_element_type=jnp.float32)
        parts.append((r.reshape(h1, SUBLANE, cw) * scale + v[:, sl, :] * skip_ref[...]) * x0[:, sl, :])
    o_ref[...] = jnp.concatenate(parts, axis=1).astype(o_ref.dtype)


def _hy_outer_inv(t_kron_inv, b_r, b_i, v3, x03, skip, L):
    h1, n2, w = b_r.shape
    blk = 2 * SUBLANE
    tile = pl.BlockSpec((h1, blk, HY_COL_TILE), lambda j, cc: (0, j, cc))
    return pl.pallas_call(
        functools.partial(_hy_outer_inv_kernel, scale=1.0 / L),
        grid=(n2 // blk, w // HY_COL_TILE),
        in_specs=[pl.BlockSpec(t_kron_inv.shape, lambda j, cc: (0, 0)), tile, tile, tile, tile,
                  pl.BlockSpec((1, 1, HY_COL_TILE), lambda j, cc: (0, 0, cc))],
        out_specs=tile,
        out_shape=jax.ShapeDtypeStruct((h1, n2, w), jnp.bfloat16),
        compiler_params=_cparams("parallel", "parallel"),
        name="hyena_outer_idft",
    )(t_kron_inv, b_r, b_i, v3, x03, skip.reshape(1, 1, w))


def _hyena(z, L, conv_w, conv_b, fparams, skip):
    v, x0 = _hy_pre(z, L, conv_w, conv_b)
    hfil, inorm = _hyena_filters(L, *fparams)
    if L >= HY_TWO_STAGE_MIN_L:
        n1 = HY_N1
        n2 = 2 * L // n1
        h1 = n1 // 2
        assert n2 % (2 * SUBLANE) == 0
        t_kron, t_kron_inv = _dft_outer_kron(n1)
        v3 = v.reshape(h1, n2, HY_C)
        av = _hy_outer_fwd(t_kron, v3)
        ah = _hy_outer_fwd(t_kron, hfil.reshape(h1, n2, 2 * HY_C))
        b_r, b_i = _hy_spec(_dft_inner_table(n1, n2), av, ah, inorm, n1, n2)
        return _hy_outer_inv(t_kron_inv, b_r, b_i, v3, x0.reshape(h1, n2, HY_C), skip, L).reshape(L, HY_C)
    t_fwd = _dft_outer_table(2 * L, L)
    xv = _matmul(t_fwd, _bf(v), tm=2 * L, tn=HY_C, out_dtype=jnp.bfloat16)
    xh = _matmul(t_fwd, _bf(hfil), tm=2 * L, tn=HY_C, out_dtype=jnp.bfloat16)
    y_r, y_i = _hy_spec_direct(xv, xh, inorm, L)
    return _hy_post(t_fwd, y_r, y_i, v, x0, skip, L, L, 1)


def _merge_kernel(yh_ref, yg_ref, yl_ref, gate_h_ref, gate_g_ref, gate_l_ref, wb_ref, o_ref):
    acc = None
    for br, (y_ref, g_ref) in enumerate(((yh_ref, gate_h_ref), (yg_ref, gate_g_ref), (yl_ref, gate_l_ref))):
        t = jnp.dot(y_ref[...], wb_ref[br], preferred_element_type=jnp.float32) * jax.nn.sigmoid(g_ref[...])
        acc = t if acc is None else acc + t
    o_ref[...] = acc.astype(o_ref.dtype)


def _proj_residual_kernel(m_ref, w_ref, h_ref, gt_ref, o_ref):
    o_ref[...] = h_ref[...] + gt_ref[...] * jnp.dot(m_ref[...], w_ref[...], preferred_element_type=jnp.float32)


MERGE_ROWS = 512


def _merge(z, L, ys, w_branch, w_out, h, gt):
    tm = min(MERGE_ROWS, L)
    gb = MG_OFF // D_MODEL
    ybs = pl.BlockSpec((tm, HY_C), lambda i: (i, 0))
    gate = lambda br: pl.BlockSpec((tm, D_MODEL), lambda i: (i, gb + br))
    row = pl.BlockSpec((tm, D_MODEL), lambda i: (i, 0))
    merged = pl.pallas_call(
        _merge_kernel,
        grid=(L // tm,),
        in_specs=[ybs, ybs, ybs, gate(0), gate(1), gate(2),
                  pl.BlockSpec((N_BRANCH, HY_C, D_MODEL), lambda i: (0, 0, 0), pipeline_mode=pl.Buffered(1))],
        out_specs=row,
        out_shape=jax.ShapeDtypeStruct((L, D_MODEL), jnp.bfloat16),
        compiler_params=_cparams("parallel"),
        name="branch_merge",
    )(ys[0], ys[1], ys[2], z, z, z, _bf(w_branch))
    return pl.pallas_call(
        _proj_residual_kernel,
        grid=(L // tm,),
        in_specs=[row, pl.BlockSpec((D_MODEL, D_MODEL), lambda i: (0, 0), pipeline_mode=pl.Buffered(1)), row,
                  pl.BlockSpec((1, D_MODEL), lambda i: (0, 0))],
        out_specs=row,
        out_shape=jax.ShapeDtypeStruct((L, D_MODEL), jnp.float32),
        compiler_params=_cparams("parallel"),
        name="out_proj_residual",
    )(merged, _bf(w_out), h, gt)


def _pad_cols(a):
    pad = lambda n: jnp.zeros(a.shape[:-1] + (n,), a.dtype)
    return jnp.concatenate([a[..., :REC_COLS], pad(HY_OFF - REC_COLS), a[..., REC_COLS:]], axis=-1)


W_TILE = 1024
W_SHIFT = HY_OFF - REC_COLS
W_ROW_OFF = W_TILE - W_SHIFT
assert 0 < W_SHIFT <= W_TILE and W_ROW_OFF % 8 == 0 and HY_OFF % W_TILE == 0 and Z_COLS % W_TILE == 0


def _w_in_prep_kernel(a_ref, b_ref, o_ref):
    j = pl.program_id(0)
    shifted = j >= HY_OFF // W_TILE

    @pl.when(jnp.logical_not(shifted))
    def _():
        o_ref[...] = a_ref[...].T.astype(o_ref.dtype)

    @pl.when(shifted)
    def _():
        window = jnp.concatenate([a_ref[...], b_ref[...]], axis=0)
        o_ref[...] = window[W_ROW_OFF:W_ROW_OFF + W_TILE].T.astype(o_ref.dtype)


def _w_in_prep(w_in, layer):
    w_t = jnp.swapaxes(w_in, 1, 2)
    _, n, k = w_t.shape
    first_shifted = HY_OFF // W_TILE
    a_idx = lambda j: jnp.where(j < first_shifted, j, j - 1)
    tail_blocks = W_TILE // W_ROW_OFF
    assert W_TILE % W_ROW_OFF == 0 and n % W_ROW_OFF == 0
    return pl.pallas_call(
        _w_in_prep_kernel,
        grid=(Z_COLS // W_TILE,),
        in_specs=[pl.BlockSpec((None, W_TILE, k), lambda j: (layer, a_idx(j), 0)),
                  pl.BlockSpec((None, W_ROW_OFF, k), lambda j: (layer, (a_idx(j) + 1) * tail_blocks, 0))],
        out_specs=pl.BlockSpec((k, W_TILE), lambda j: (0, j)),
        out_shape=jax.ShapeDtypeStruct((k, Z_COLS), jnp.bfloat16),
        compiler_params=_cparams("parallel"),
        name="w_in_relayout",
    )(w_t, w_t)


def _mixer(h, hc, u, uc, gt, gtc, p, need_ctx):
    L, Lc = u.shape[0], uc.shape[0]
    w_in = _w_in_prep(p['w_in'], p['layer'])
    b_in = _pad_cols(p['b_in'])
    z = _matmul(u, w_in, b_in, tm=1024, tn=W_TILE)
    ncol = Z_COLS if need_ctx else HY_OFF
    zc = _matmul(uc, w_in, b_in[:ncol], tm=1024, tn=W_TILE, n_cols=ncol)
    zeros = lambda hd, dk, dv: jnp.zeros((hd, dv, dk), jnp.float32)
    yc_hg, hg_sf, hg_sb = _hgrn2(zc, Lc, p['lb'], p['hg_norm_w'],
                                 zeros(HG_HEADS, HG_DK, HG_DV), zeros(HG_HEADS, HG_DK, HG_DV))
    yc_gl, gl_sf, gl_sb = _gla(zc, Lc, p['gl_w_a2'], p['gl_b_a'], p['gl_norm_w'],
                               zeros(GL_HEADS, GL_DK, GL_DV), zeros(GL_HEADS, GL_DK, GL_DV))
    y_hg, _, _ = _hgrn2(z, L, p['lb'], p['hg_norm_w'], hg_sf, hg_sb)
    y_gl, _, _ = _gla(z, L, p['gl_w_a2'], p['gl_b_a'], p['gl_norm_w'], gl_sf, gl_sb)
    hy = (p['hy_conv_w'], p['hy_conv_b'], p['hy_f'], p['hy_skip'])
    y_hy = _hyena(z, L, *hy)
    h = _merge(z, L, (y_hy, y_hg, y_gl), p['w_branch'], p['w_out'], h, gt)
    if need_ctx:
        yc_hy = _hyena(zc, Lc, *hy)
        hc = _merge(zc, Lc, (yc_hy, yc_hg, yc_gl), p['w_branch'], p['w_out'], hc, gtc)
    return h, hc


GATHER_UNROLL = 8
GATHER_AHEAD = 2
GATHER_SLOTS = GATHER_AHEAD + 1


def _ffn_kernel(blk_exp_ref, n_used_ref, tok_ref, x_hbm, wg_ref, wu_ref, wd_ref, o_ref, xbuf, sem, wg_s, wu_s, wd_s):
    i = pl.program_id(0)
    n_used = n_used_ref[0]
    rows = o_ref.shape[0]

    def issue(step, slot):
        def body(r, carry):
            src = tok_ref[step * rows + r]
            pltpu.make_async_copy(x_hbm.at[pl.ds(src, 1)], xbuf.at[slot, pl.ds(r, 1)], sem.at[slot]).start()
            return carry

        lax.fori_loop(0, rows, body, 0, unroll=GATHER_UNROLL)

    for s in range(GATHER_AHEAD):
        @pl.when(jnp.logical_and(i == 0, s < n_used))
        def _(s=s):
            issue(s, s % GATHER_SLOTS)

    @pl.when(i + GATHER_AHEAD < n_used)
    def _():
        issue(i + GATHER_AHEAD, (i + GATHER_AHEAD) % GATHER_SLOTS)

    new_expert = jnp.logical_or(i == 0, blk_exp_ref[i] != blk_exp_ref[jnp.maximum(i - 1, 0)])

    @pl.when(jnp.logical_and(i < n_used, new_expert))
    def _():
        wg_s[...] = wg_ref[0].astype(jnp.bfloat16)
        wu_s[...] = wu_ref[0].astype(jnp.bfloat16)
        wd_s[...] = wd_ref[0].astype(jnp.bfloat16)

    @pl.when(i < n_used)
    def _():
        slot = i % GATHER_SLOTS
        pltpu.make_async_copy(xbuf.at[slot], xbuf.at[slot], sem.at[slot]).wait()
        x = xbuf[slot].astype(jnp.bfloat16)
        hg = jnp.dot(x, wg_s[...], preferred_element_type=jnp.float32)
        hu = jnp.dot(x, wu_s[...], preferred_element_type=jnp.float32)
        act = (hg * jax.nn.sigmoid(hg) * hu).astype(jnp.bfloat16)
        o_ref[...] = jnp.dot(act, wd_s[...], preferred_element_type=jnp.float32)

    @pl.when(i >= n_used)
    def _():
        o_ref[...] = jnp.zeros_like(o_ref)


def _grouped_ffn(x, buf_tok, blk_exp, n_used, layer, w_gate, w_up, w_down):
    p_len = buf_tok.shape[0]
    d = x.shape[1]
    n_blk = p_len // MOE_BLOCK
    grid_spec = pltpu.PrefetchScalarGridSpec(
        num_scalar_prefetch=3,
        grid=(n_blk,),
        in_specs=[pl.BlockSpec(memory_space=pl.ANY),
                  pl.BlockSpec((None, 1, d, D_FF), lambda i, be, nu, tk: (layer, be[i], 0, 0)),
                  pl.BlockSpec((None, 1, d, D_FF), lambda i, be, nu, tk: (layer, be[i], 0, 0)),
                  pl.BlockSpec((None, 1, D_FF, d), lambda i, be, nu, tk: (layer, be[i], 0, 0))],
        out_specs=pl.BlockSpec((MOE_BLOCK, d), lambda i, be, nu, tk: (i, 0)),
        scratch_shapes=[pltpu.VMEM((GATHER_SLOTS, MOE_BLOCK, d), jnp.float32), pltpu.SemaphoreType.DMA((GATHER_SLOTS,)),
                        pltpu.VMEM((d, D_FF), jnp.bfloat16), pltpu.VMEM((d, D_FF), jnp.bfloat16),
                        pltpu.VMEM((D_FF, d), jnp.bfloat16)],
    )
    return pl.pallas_call(
        _ffn_kernel,
        grid_spec=grid_spec,
        out_shape=jax.ShapeDtypeStruct((p_len, d), jnp.float32),
        compiler_params=_cparams("arbitrary"),
        name="moe_grouped_ffn",
    )(blk_exp, n_used, buf_tok, x, w_gate, w_up, w_down)


ROUTER_COLS = 128


def _hier_moe(h, norm_w, shift, scale, gt, p):
    n, d = h.shape
    pad = ROUTER_COLS - N_GROUPS - N_EXPERTS
    w_r = _bf(jnp.concatenate([p['w_rg'], p['w_re'], jnp.zeros((d, pad), jnp.float32)], axis=1))
    b_r = jnp.concatenate([p['b_rg'], p['b_re'], jnp.zeros((pad,), jnp.float32)]).reshape(1, ROUTER_COLS)
    xb, logits = _norm_mod(h, norm_w, shift, scale, router=(w_r, b_r), out_dtype=jnp.float32)
    lg = logits[:, :N_GROUPS]
    p_grp = jax.nn.softmax(lg, axis=-1)
    grp = jnp.argmax(p_grp, axis=-1).astype(jnp.int32)
    p_top = jnp.max(p_grp, axis=-1)
    le = logits[:, N_GROUPS:N_GROUPS + N_EXPERTS].reshape(n, N_GROUPS, EXP_PER_GROUP)
    le = jnp.take_along_axis(le, grp[:, None, None], axis=1)[:, 0]
    top_p, top_i = lax.top_k(jax.nn.softmax(le, axis=-1), TOP_K)
    weight = p_top[:, None] * top_p / jnp.sum(top_p, axis=-1, keepdims=True)
    expert = grp[:, None] * EXP_PER_GROUP + top_i.astype(jnp.int32)
    a = n * TOP_K
    e_flat = expert.reshape(a)
    onehot = (e_flat[:, None] == jnp.arange(N_EXPERTS, dtype=jnp.int32)[None, :]).astype(jnp.int32)
    rank = jnp.take_along_axis(jnp.cumsum(onehot, axis=0) - onehot, e_flat[:, None], axis=1)[:, 0]
    counts = jnp.sum(onehot, axis=0)
    padded = (counts + MOE_BLOCK - 1) // MOE_BLOCK * MOE_BLOCK
    pad_end = jnp.cumsum(padded)
    pad_off = pad_end - padded
    pos = pad_off[e_flat] + rank
    p_len = (a + N_EXPERTS * MOE_BLOCK + MOE_BLOCK - 1) // MOE_BLOCK * MOE_BLOCK
    n_blk = p_len // MOE_BLOCK
    tok_flat = jnp.arange(a, dtype=jnp.int32) // TOP_K
    buf_tok = (jnp.arange(p_len, dtype=jnp.int32) % n).at[pos].set(tok_flat)
    blk_start = jnp.arange(n_blk, dtype=jnp.int32) * MOE_BLOCK
    blk_exp = jnp.minimum(jnp.sum(pad_end[None, :] <= blk_start[:, None], axis=1), N_EXPERTS - 1).astype(jnp.int32)
    n_used = (pad_end[-1:] // MOE_BLOCK).astype(jnp.int32)
    y = _grouped_ffn(xb, buf_tok, blk_exp, n_used, p['layer'], p['w_gate'], p['w_up'], p['w_down'])
    return _moe_combine(y, pos, weight, h, gt)


def _combine_kernel(pos_ref, y_hbm, wts_ref, h_ref, gt_ref, o_ref, buf, sem):
    i = pl.program_id(0)
    tokens = h_ref.shape[0]

    def row_copy(step, slot, r, k):
        src = pos_ref[(step * tokens + r) * TOP_K + k]
        return pltpu.make_async_copy(y_hbm.at[pl.ds(src, 1)], buf.at[slot, k, pl.ds(r, 1)], sem.at[slot])

    def issue(step, slot):
        def body(r, carry):
            for k in range(TOP_K):
                row_copy(step, slot, r, k).start()
            return carry

        lax.fori_loop(0, tokens, body, 0, unroll=GATHER_UNROLL // TOP_K)

    @pl.when(i == 0)
    def _():
        issue(0, 0)

    @pl.when(i + 1 < pl.num_programs(0))
    def _():
        issue(i + 1, (i + 1) % 2)

    slot = i % 2
    pltpu.make_async_copy(buf.at[slot], buf.at[slot], sem.at[slot]).wait()
    rows = buf[slot]
    wts = wts_ref[...]
    acc = rows[0] * wts[:, 0:1]
    for k in range(1, TOP_K):
        acc = acc + rows[k] * wts[:, k:k + 1]
    o_ref[...] = h_ref[...] + gt_ref[...] * acc


COMBINE_TOKENS = 128


def _moe_combine(y, pos, wts, h, gt):
    n, d = h.shape
    tokens = min(COMBINE_TOKENS, n)
    grid_spec = pltpu.PrefetchScalarGridSpec(
        num_scalar_prefetch=1,
        grid=(n // tokens,),
        in_specs=[pl.BlockSpec(memory_space=pl.ANY),
                  pl.BlockSpec((tokens, TOP_K), lambda i, pos: (i, 0)),
                  pl.BlockSpec((tokens, d), lambda i, pos: (i, 0)),
                  pl.BlockSpec((1, d), lambda i, pos: (0, 0))],
        out_specs=pl.BlockSpec((tokens, d), lambda i, pos: (i, 0)),
        scratch_shapes=[pltpu.VMEM((2, TOP_K, tokens, d), jnp.float32), pltpu.SemaphoreType.DMA((2,))],
    )
    return pl.pallas_call(
        _combine_kernel,
        grid_spec=grid_spec,
        out_shape=jax.ShapeDtypeStruct((n, d), jnp.float32),
        compiler_params=_cparams("arbitrary"),
        name="moe_combine",
    )(pos, y, wts, h, gt)


def _final_norm_kernel(x_ref, w_ref, o_ref):
    x = x_ref[...]
    o_ref[...] = x * lax.rsqrt(jnp.mean(x * x, axis=-1, keepdims=True) + NORM_EPS) * w_ref[...]


def _final_norm(x, w, tm=512):
    m, d = x.shape
    return pl.pallas_call(
        _final_norm_kernel,
        grid=(m // tm,),
        in_specs=[pl.BlockSpec((tm, d), lambda i: (i, 0)), pl.BlockSpec((1, d), lambda i: (0, 0))],
        out_specs=pl.BlockSpec((tm, d), lambda i: (i, 0)),
        out_shape=jax.ShapeDtypeStruct((m, d), jnp.float32),
        compiler_params=_cparams("parallel"),
        name="final_rmsnorm",
    )(x, w.reshape(1, d))


def kernel(x, c, ctx, c_ctx, w_mod, b_mod, norm_mix_w, norm_ffn_w, w_in, b_in, hy_conv_w, hy_conv_b, hy_f_w1, hy_f_b1, hy_f_w2, hy_f_b2, hy_f_w3, hy_f_b3, hy_f_freq, hy_skip, hg_lb_raw, hg_norm_w, gl_w_a2, gl_b_a, gl_norm_w, w_branch, w_out, w_rg, b_rg, w_re, b_re, w_gate, w_up, w_down, final_norm_w):
    assert x.shape[0] == 1
    depth = w_mod.shape[0]
    lb_all = jnp.cumsum(jax.nn.softmax(hg_lb_raw, axis=0), axis=0)
    lb_all = lb_all - lb_all[:1]
    h, hc = x[0], ctx[0]
    cc = jnp.concatenate([c, c_ctx[None, :]], axis=0)
    for l in range(depth):
        need_ctx = l < depth - 1
        mod = _matmul(_bf(jax.nn.silu(cc)), w_mod, b_mod[l], layer=l)
        sh1, sc1, gt1, sh2, sc2, gt2 = jnp.split(mod[0:1], 6, axis=-1)
        sh1c, sc1c, gt1c, sh2c, sc2c, gt2c = jnp.split(mod[1:2], 6, axis=-1)
        p = dict(w_in=w_in, b_in=b_in[l], hy_conv_w=hy_conv_w[l], hy_conv_b=hy_conv_b[l],
                 hy_f=(hy_f_w1[l], hy_f_b1[l], hy_f_w2[l], hy_f_b2[l], hy_f_w3[l], hy_f_b3[l], hy_f_freq[l]),
                 hy_skip=hy_skip[l], lb=lb_all[l], hg_norm_w=hg_norm_w[l], gl_w_a2=gl_w_a2[l], gl_b_a=gl_b_a[l],
                 gl_norm_w=gl_norm_w[l], w_branch=w_branch[l], w_out=w_out[l], w_rg=w_rg[l], b_rg=b_rg[l],
                 w_re=w_re[l], b_re=b_re[l], layer=l, w_gate=w_gate, w_up=w_up, w_down=w_down)
        u = _norm_mod(h, norm_mix_w[l], sh1, sc1)
        uc = _norm_mod(hc, norm_mix_w[l], sh1c, sc1c)
        h, hc = _mixer(h, hc, u, uc, gt1, gt1c, p, need_ctx)
        h = _hier_moe(h, norm_ffn_w[l], sh2, sc2, gt2, p)
        if need_ctx:
            hc = _hier_moe(hc, norm_ffn_w[l], sh2c, sc2c, gt2c, p)
    return _final_norm(h, final_norm_w)[None]
```

```python
import functools
import math

import jax
import jax.numpy as jnp
import numpy as np
from jax import lax
from jax.experimental import pallas as pl
from jax.experimental.pallas import tpu as pltpu

D_MODEL = 2048
NORM_EPS = 1e-6

HY_C = D_MODEL // 2
HY_EMB = 33
HY_BANDS = (HY_EMB - 1) // 2
HY_DECAY_TARGET = 1e-2
HY_FAST_PCT = 0.3
HY_SLOW_PCT = 1.5
HY_MOD_SHIFT = 0.05

HG_HEADS = 8
HG_DK = 128
HG_DV = 128
HG_K = HG_HEADS * HG_DK
HG_V = HG_HEADS * HG_DV

GL_HEADS = 4
GL_DK = 128
GL_DV = 256
GL_K = GL_HEADS * GL_DK
GL_V = GL_HEADS * GL_DV
GL_RANK = 16
GL_TAU = 16.0

N_BRANCH = 3
HG_COLS = 3 * HG_K + 2 * HG_V
GL_COLS = 2 * GL_K + 2 * GL_V + 2 * GL_RANK
REC_COLS = HG_COLS + GL_COLS
HY_COLS = 3 * HY_C
MERGE_COLS = N_BRANCH * D_MODEL

COL_TILE = 512
GL_Q_OFF = HG_COLS
GL_V_OFF = GL_Q_OFF + 2 * GL_K
GL_A_OFF = GL_V_OFF + 2 * GL_V
HY_OFF = -(-(GL_A_OFF + 2 * GL_RANK) // HY_C) * HY_C
MG_OFF = HY_OFF + HY_COLS
Z_COLS = MG_OFF + MERGE_COLS
assert GL_Q_OFF % GL_K == 0 and GL_V_OFF % GL_V == 0 and GL_A_OFF % 128 == 0 and Z_COLS % COL_TILE == 0
assert MG_OFF % D_MODEL == 0

N_GROUPS = 4
EXP_PER_GROUP = 8
N_EXPERTS = N_GROUPS * EXP_PER_GROUP
TOP_K = 2
D_FF = D_MODEL // 4
MOE_BLOCK = 256

SCAN_CHUNK = 128
SCAN_HEAD_GROUP = 8
LOG2_E = 1.4426950408889634

VMEM_LIMIT_BYTES = 56 * 1024 * 1024


def _cparams(*sem):
    return pltpu.CompilerParams(dimension_semantics=sem, vmem_limit_bytes=VMEM_LIMIT_BYTES)


def _bf(a):
    return a.astype(jnp.bfloat16)


def _mm_kernel(x_ref, w_ref, b_ref, o_ref):
    acc = jnp.dot(x_ref[...], w_ref[...].astype(jnp.bfloat16), preferred_element_type=jnp.float32) + b_ref[...]
    o_ref[...] = acc.astype(o_ref.dtype)


def _matmul(x, w, bias=None, tm=512, tn=COL_TILE, out_dtype=jnp.float32, n_cols=None, layer=None):
    m, k = x.shape
    n = w.shape[-1] if n_cols is None else n_cols
    assert n_cols is None or n_cols % tn == 0
    assert layer is None or n % tn == 0
    tm = min(tm, -(-m // 8) * 8)
    mp = -(-m // tm) * tm
    np_ = -(-n // tn) * tn
    if bias is None:
        bias = jnp.zeros((n,), jnp.float32)
    if mp != m:
        x = jnp.pad(x, ((0, mp - m), (0, 0)))
    if np_ != n:
        w = jnp.pad(w, ((0, 0), (0, np_ - n)))
        bias = jnp.pad(bias, (0, np_ - n))
    if layer is None:
        w_spec = pl.BlockSpec((k, tn), lambda i, j: (0, j))
    else:
        w_spec = pl.BlockSpec((None, k, tn), lambda i, j: (layer, 0, j))
    out = pl.pallas_call(
        _mm_kernel,
        grid=(mp // tm, np_ // tn),
        in_specs=[pl.BlockSpec((tm, k), lambda i, j: (i, 0)),
                  w_spec,
                  pl.BlockSpec((1, tn), lambda i, j: (0, j))],
        out_specs=pl.BlockSpec((tm, tn), lambda i, j: (i, j)),
        out_shape=jax.ShapeDtypeStruct((mp, np_), out_dtype),
        compiler_params=_cparams("parallel", "arbitrary"),
        name="dense_matmul",
    )(x, w, bias.reshape(1, np_))
    if mp != m or np_ != n:
        out = out[:m, :n]
    return out


def _norm_mod_kernel(*refs, with_router):
    if with_router:
        h_ref, w_ref, sh_ref, sc_ref, wr_ref, br_ref, o_ref, lg_ref = refs
    else:
        h_ref, w_ref, sh_ref, sc_ref, o_ref = refs
    x = h_ref[...]
    y = x * lax.rsqrt(jnp.mean(x * x, axis=-1, keepdims=True) + NORM_EPS) * w_ref[...]
    u = (y * (1.0 + sc_ref[...]) + sh_ref[...]).astype(jnp.bfloat16)
    o_ref[...] = u.astype(o_ref.dtype)
    if with_router:
        lg_ref[...] = jnp.dot(u, wr_ref[...], preferred_element_type=jnp.float32) + br_ref[...]


def _norm_mod(h, w, shift, scale, router=None, out_dtype=jnp.bfloat16):
    m, d = h.shape
    tm = min(512, m)
    row = pl.BlockSpec((tm, d), lambda i: (i, 0))
    vec = pl.BlockSpec((1, d), lambda i: (0, 0))
    args = [h, w.reshape(1, d), shift, scale]
    specs = [row, vec, vec, vec]
    out_shape = [jax.ShapeDtypeStruct((m, d), out_dtype)]
    out_specs = [row]
    if router is not None:
        args += list(router)
        specs += [pl.BlockSpec(router[0].shape, lambda i: (0, 0)), pl.BlockSpec(router[1].shape, lambda i: (0, 0))]
        out_shape.append(jax.ShapeDtypeStruct((m, router[0].shape[1]), jnp.float32))
        out_specs.append(pl.BlockSpec((tm, router[0].shape[1]), lambda i: (i, 0)))
    out = pl.pallas_call(
        functools.partial(_norm_mod_kernel, with_router=router is not None),
        grid=(m // tm,),
        in_specs=specs,
        out_specs=out_specs,
        out_shape=out_shape,
        compiler_params=_cparams("parallel"),
        name="norm_modulate",
    )(*args)
    return out if router is not None else out[0]


def _scan_masks(c, reverse):
    t = np.arange(c)
    ms = [np.eye(c, dtype=np.float32)]
    for lvl in range(int(math.log2(c))):
        upper = ((t >> lvl) & 1).astype(bool)
        same = (t[:, None] >> (lvl + 1)) == (t[None, :] >> (lvl + 1))
        m = same & upper[:, None] & (~upper)[None, :]
        ms.append((m.T if reverse else m).astype(np.float32))
    tri = t[None, :] >= t[:, None] if reverse else t[None, :] <= t[:, None]
    return jnp.asarray(np.stack(ms)), jnp.asarray(tri.astype(np.float32), dtype=jnp.bfloat16)


def _level_arg(cum, lvl, reverse):
    c = cum.shape[0]
    blk = 1 << lvl
    if blk >= 8:
        pieces = []
        for gs in range(0, c, 2 * blk):
            ref = cum[gs + blk:gs + blk + 1, :]
            pieces.append(ref - cum[gs:gs + blk, :])
            pieces.append(cum[gs + blk:gs + 2 * blk, :] - ref)
        arg = jnp.concatenate(pieces, axis=0)
    else:
        c3 = cum.reshape(c // 8, 8, cum.shape[1])
        sub = lax.broadcasted_iota(jnp.int32, c3.shape, 1)
        ref_row = ((sub >> lvl) | 1) << lvl
        ref = None
        for r in range(blk, 8, 2 * blk):
            cand = jnp.broadcast_to(c3[:, r:r + 1, :], c3.shape)
            ref = cand if ref is None else jnp.where(ref_row == r, cand, ref)
        upper = ((sub >> lvl) & 1) == 1
        arg = jnp.where(upper, c3 - ref, ref - c3).reshape(cum.shape)
    return -arg if reverse else arg


def _dot_nt(a, b):
    return lax.dot_general(a, b, (((1,), (1,)), ((), ())), preferred_element_type=jnp.float32)


def _dot_tn(a, b):
    return lax.dot_general(a, b, (((0,), (0,)), ((), ())), preferred_element_type=jnp.float32)


def _sigmoid_parts(z):
    e = jnp.exp(-jnp.abs(z))
    r = 1.0 / (1.0 + e)
    return jnp.minimum(z, 0.0) - jnp.log(1.0 + e), jnp.where(z >= 0.0, e * r, r)


def _scan_kernel(*refs, mode, reverse, final, heads, dk, dv, c):
    it = iter(refs)
    q_ref = next(it)
    k_ref = next(it)
    v_ref = next(it)
    if mode == "hg":
        lbp_ref = next(it)
    else:
        a_ref = next(it)
        wa_ref = next(it)
        ba_ref = next(it)
    s0_ref = next(it)
    masks_ref = next(it)
    tri_ref = next(it)
    if final:
        oprev_ref = next(it)
        gate_ref = next(it)
        nw_ref = next(it)
    o_ref = next(it)
    st_ref = next(it)

    @pl.when(pl.program_id(0) == 0)
    def _():
        st_ref[...] = s0_ref[...]

    if mode == "gl":
        la_all = jnp.dot(a_ref[...].astype(jnp.bfloat16), wa_ref[...],
                         preferred_element_type=jnp.float32) + ba_ref[...]
    tri = tri_ref[...]
    tot_row = 0 if reverse else c - 1
    n_lvl = int(math.log2(c))
    ksl = lambda h: slice(h * dk, (h + 1) * dk)
    vsl = lambda h: slice(h * dv, (h + 1) * dv)
    for h0 in range(0, heads, SCAN_HEAD_GROUP):
        group = range(h0, min(h0 + SCAN_HEAD_GROUP, heads))
        q_, k_, cum_, qb_, kb_, o_, sc_ = {}, {}, {}, {}, {}, {}, {}
        for h in group:
            q = q_ref[:, ksl(h)]
            if mode == "hg":
                log_sig, sig_neg = _sigmoid_parts(k_ref[:, ksl(h)])
                la = lbp_ref[0:1, ksl(h)]
                lbb = lbp_ref[1:2, ksl(h)] + log_sig
                g = jnp.maximum(la, lbb) + jnp.log(1.0 + jnp.exp(-jnp.abs(la - lbb)))
                k = lbp_ref[2:3, ksl(h)] * sig_neg
                q = q * jax.nn.sigmoid(q)
            else:
                g = _sigmoid_parts(la_all[:, ksl(h)])[0] * (1.0 / GL_TAU)
                k = k_ref[:, ksl(h)]
                q = q * (dk ** -0.5)
            g = g * LOG2_E
            g1 = g.astype(jnp.bfloat16)
            r1 = g - g1.astype(jnp.float32)
            g2 = r1.astype(jnp.bfloat16)
            g3 = (r1 - g2.astype(jnp.float32)).astype(jnp.bfloat16)
            cum_[h] = (jnp.dot(tri, g1, preferred_element_type=jnp.float32)
                       + jnp.dot(tri, g2, preferred_element_type=jnp.float32)
                       + jnp.dot(tri, g3, preferred_element_type=jnp.float32))
            q_[h], k_[h] = q, k
        for h in group:
            cum = cum_[h]
            tot = cum[tot_row:tot_row + 1, :]
            st = st_ref[h]
            v = v_ref[:, vsl(h)].astype(jnp.bfloat16)
            o_[h] = _dot_nt((q_[h] * jnp.exp2(cum)).astype(jnp.bfloat16), st.astype(jnp.bfloat16))
            kt = (k_[h] * jnp.exp2(tot - cum)).astype(jnp.bfloat16)
            st_ref[h] = st * jnp.exp2(tot) + _dot_tn(v, kt)
            qb_[h] = q_[h].astype(jnp.bfloat16)
            kb_[h] = k_[h].astype(jnp.bfloat16)
            sc_[h] = masks_ref[0] * _dot_nt(qb_[h], kb_[h])
        for lvl in range(n_lvl):
            for h in group:
                e = jnp.exp2(_level_arg(cum_[h], lvl, reverse)).astype(jnp.bfloat16)
                sc_[h] = sc_[h] + masks_ref[1 + lvl] * _dot_nt(qb_[h] * e, kb_[h] * e)
        for h in group:
            v = v_ref[:, vsl(h)].astype(jnp.bfloat16)
            o = o_[h] + jnp.dot(sc_[h].astype(jnp.bfloat16), v, preferred_element_type=jnp.float32)
            if final:
                o = o + oprev_ref[:, vsl(h)]
                y = o * lax.rsqrt(jnp.mean(o * o, axis=-1, keepdims=True) + NORM_EPS) * nw_ref[...]
                gt = gate_ref[:, vsl(h)]
                act = jax.nn.sigmoid(gt) if mode == "hg" else gt * jax.nn.sigmoid(gt)
                o_ref[:, vsl(h)] = (y * act).astype(o_ref.dtype)
            else:
                o_ref[:, vsl(h)] = o


def _scan_pass(mode, reverse, final, L, srcs, s0, params, final_srcs=(), norm_w=None):
    heads, dk, dv = (HG_HEADS, HG_DK, HG_DV) if mode == "hg" else (GL_HEADS, GL_DK, GL_DV)
    c = min(SCAN_CHUNK, L)
    nb = L // c
    row = (lambda i: nb - 1 - i) if reverse else (lambda i: i)
    masks, tri = _scan_masks(c, reverse)

    def const(shape):
        return pl.BlockSpec(shape, lambda i: (0,) * len(shape))

    def rowblock(width, cb):
        return pl.BlockSpec((c, width), lambda i: (row(i), cb))

    args = [a for a, _, _ in srcs] + list(params) + [s0, masks, tri]
    specs = ([rowblock(w, cb) for _, w, cb in srcs] + [const(p.shape) for p in params]
             + [const(s0.shape), const(masks.shape), const(tri.shape)])
    if final:
        args += [a for a, _, _ in final_srcs] + [norm_w]
        specs += [rowblock(w, cb) for _, w, cb in final_srcs] + [const(norm_w.shape)]
    return pl.pallas_call(
        functools.partial(_scan_kernel, mode=mode, reverse=reverse, final=final, heads=heads, dk=dk, dv=dv, c=c),
        grid=(nb,),
        in_specs=specs,
        out_specs=[pl.BlockSpec((c, heads * dv), lambda i: (row(i), 0)), const((heads, dv, dk))],
        out_shape=[jax.ShapeDtypeStruct((L, heads * dv), jnp.bfloat16 if final else jnp.float32),
                   jax.ShapeDtypeStruct((heads, dv, dk), jnp.float32)],
        compiler_params=_cparams("arbitrary"),
        name=f"scan_{mode}_{'bwd' if reverse else 'fwd'}",
    )(*args)


def _hgrn2(z, L, lb, norm_w, s0_f, s0_b):
    lbp = lambda d: jnp.stack([jnp.log(lb[d]), jnp.log1p(-lb[d]), 1.0 - lb[d]])
    w = HG_K
    o_b, s_b = _scan_pass("hg", True, False, L, [(z, w, 0), (z, w, 2), (z, w, 3)], s0_b, [lbp(1)])
    y, s_f = _scan_pass("hg", False, True, L, [(z, w, 0), (z, w, 1), (z, w, 3)], s0_f, [lbp(0)],
                        final_srcs=[(o_b, HG_V, 0), (z, HG_V, 4)], norm_w=norm_w.reshape(1, HG_DV))
    return y, s_f, s_b


def _gla(z, L, w_a2, b_a, norm_w, s0_f, s0_b):
    def gate_params(d):
        wa = jnp.zeros((128, GL_K), jnp.float32).at[d * GL_RANK:(d + 1) * GL_RANK].set(w_a2[d])
        return [_bf(wa), b_a[d].reshape(1, GL_K)]

    srcs = [(z, GL_K, GL_Q_OFF // GL_K), (z, GL_K, GL_Q_OFF // GL_K + 1), (z, GL_V, GL_V_OFF // GL_V),
            (z, 128, GL_A_OFF // 128)]
    o_b, s_b = _scan_pass("gl", True, False, L, srcs, s0_b, gate_params(1))
    y, s_f = _scan_pass("gl", False, True, L, srcs, s0_f, gate_params(0),
                        final_srcs=[(o_b, GL_V, 0), (z, GL_V, GL_V_OFF // GL_V + 1)], norm_w=norm_w.reshape(1, GL_DV))
    return y, s_f, s_b


HY_FEAT_PAD = 128


def _hy_filter_kernel(f_ref, w1_ref, b1_ref, fq_ref, w2_ref, b2_ref, w3_ref, b3_ref, dl_ref, h_ref, s_ref):
    i = pl.program_id(0)
    f = f_ref[...]
    fq = fq_ref[...]
    a = jnp.sin(fq * (jnp.dot(f.astype(jnp.bfloat16), w1_ref[...], preferred_element_type=jnp.float32) + b1_ref[...]))
    a = jnp.sin(fq * (jnp.dot(a.astype(jnp.bfloat16), w2_ref[...], preferred_element_type=jnp.float32) + b2_ref[...]))
    hh = jnp.dot(a.astype(jnp.bfloat16), w3_ref[...], preferred_element_type=jnp.float32) + b3_ref[...]
    hh = hh * (jnp.exp(-f[:, 0:1] * dl_ref[...]) + HY_MOD_SHIFT)
    h_ref[...] = hh
    part = jnp.sum(jnp.abs(hh).reshape(hh.shape[0] // 8, 8, hh.shape[1]), axis=0)

    @pl.when(i == 0)
    def _():
        s_ref[...] = part

    @pl.when(i > 0)
    def _():
        s_ref[...] += part


def _hyena_filters(L, w1, b1, w2, b2, w3, b3, freq):
    t = jnp.linspace(0.0, 1.0, L, dtype=jnp.float32)[:, None]
    ang = 2.0 * math.pi * jnp.arange(L, dtype=jnp.float32)[:, None] / L
    bands = jnp.linspace(1e-4, HY_BANDS - 1, HY_BANDS, dtype=jnp.float32)[None, :]
    feats = jnp.concatenate([t, jnp.cos(bands * ang), -jnp.sin(bands * ang),
                             jnp.zeros((L, HY_FEAT_PAD - HY_EMB), jnp.float32)], axis=-1)
    deltas = jnp.abs(jnp.linspace(math.log(HY_DECAY_TARGET) / HY_SLOW_PCT, math.log(HY_DECAY_TARGET) / HY_FAST_PCT,
                                  HY_C, dtype=jnp.float32))
    fh = w1.shape[1]
    padm = lambda a, r, c: _bf(jnp.pad(a, ((0, r - a.shape[0]), (0, c - a.shape[1]))))
    padv = lambda a: jnp.pad(a, (0, HY_FEAT_PAD - a.shape[0])).reshape(1, HY_FEAT_PAD)
    tm = min(512, L)
    const = lambda r, c: pl.BlockSpec((r, c), lambda i: (0, 0))
    hfil, sums = pl.pallas_call(
        _hy_filter_kernel,
        grid=(L // tm,),
        in_specs=[pl.BlockSpec((tm, HY_FEAT_PAD), lambda i: (i, 0)),
                  const(HY_FEAT_PAD, HY_FEAT_PAD), const(1, HY_FEAT_PAD), const(1, HY_FEAT_PAD),
                  const(HY_FEAT_PAD, HY_FEAT_PAD), const(1, HY_FEAT_PAD),
                  const(HY_FEAT_PAD, 2 * HY_C), const(1, 2 * HY_C), const(1, 2 * HY_C)],
        out_specs=[pl.BlockSpec((tm, 2 * HY_C), lambda i: (i, 0)), const(8, 2 * HY_C)],
        out_shape=[jax.ShapeDtypeStruct((L, 2 * HY_C), jnp.float32), jax.ShapeDtypeStruct((8, 2 * HY_C), jnp.float32)],
        compiler_params=_cparams("arbitrary"),
        name="hyena_filters",
    )(feats, padm(w1, HY_FEAT_PAD, HY_FEAT_PAD), padv(b1), padv(freq), padm(w2, HY_FEAT_PAD, HY_FEAT_PAD), padv(b2),
      padm(w3, HY_FEAT_PAD, 2 * HY_C), b3.reshape(1, 2 * HY_C), jnp.tile(deltas, 2).reshape(1, 2 * HY_C))
    assert fh <= HY_FEAT_PAD
    inorm = 1.0 / jnp.sum(sums, axis=0)
    return hfil, inorm.reshape(2, HY_C)


def _hy_pre_kernel(x0_ref, x1_ref, v_ref, x0p_ref, x1p_ref, vp_ref, x0n_ref, x1n_ref, vn_ref, w_ref, b_ref,
                   vo_ref, x0o_ref):
    i = pl.program_id(0)
    first = i == 0
    last = i == pl.num_programs(0) - 1
    tm = x0_ref.shape[0]
    row = lax.broadcasted_iota(jnp.int32, x0_ref.shape, 0)

    def conv(x_ref, p_ref, n_ref, g):
        x = x_ref[...]
        cs = slice(g * HY_C, (g + 1) * HY_C)
        prev_row = jnp.where(first, 0.0, p_ref[7:8, :])
        next_row = jnp.where(last, 0.0, n_ref[0:1, :])
        xp = jnp.where(row == 0, prev_row, pltpu.roll(x, 1, 0))
        xn = jnp.where(row == tm - 1, next_row, pltpu.roll(x, tm - 1, 0))
        return w_ref[0:1, cs] * xp + w_ref[1:2, cs] * x + w_ref[2:3, cs] * xn + b_ref[0:1, cs]

    x0 = conv(x0_ref, x0p_ref, x0n_ref, 0)
    x1 = conv(x1_ref, x1p_ref, x1n_ref, 1)
    v = conv(v_ref, vp_ref, vn_ref, 2)
    vo_ref[...] = (v * x1).astype(vo_ref.dtype)
    x0o_ref[...] = x0.astype(x0o_ref.dtype)


def _hy_pre(z, L, conv_w, conv_b):
    tm = min(256, L)
    nb8 = L // 8
    cb = HY_OFF // HY_C
    main = lambda g: pl.BlockSpec((tm, HY_C), lambda i: (i, cb + g))
    prev = lambda g: pl.BlockSpec((8, HY_C), lambda i: (jnp.maximum(i * (tm // 8) - 1, 0), cb + g))
    nxt = lambda g: pl.BlockSpec((8, HY_C), lambda i: (jnp.minimum((i + 1) * (tm // 8), nb8 - 1), cb + g))
    const = lambda a: pl.BlockSpec(a.shape, lambda i: (0, 0))
    cbias = conv_b.reshape(1, HY_COLS)
    return pl.pallas_call(
        _hy_pre_kernel,
        grid=(L // tm,),
        in_specs=[main(0), main(1), main(2), prev(0), prev(1), prev(2), nxt(0), nxt(1), nxt(2),
                  const(conv_w), const(cbias)],
        out_specs=[pl.BlockSpec((tm, HY_C), lambda i: (i, 0))] * 2,
        out_shape=[jax.ShapeDtypeStruct((L, HY_C), jnp.float32), jax.ShapeDtypeStruct((L, HY_C), jnp.bfloat16)],
        compiler_params=_cparams("parallel"),
        name="hyena_short_conv",
    )(z, z, z, z, z, z, z, z, z, conv_w, cbias)


HY_N1 = 128
HY_TWO_STAGE_MIN_L = 1024


def _dft_outer_table(n1, cols):
    ang = -2.0 * np.pi * np.outer(np.arange(n1 // 2) + 0.5, np.arange(cols)) / n1
    return jnp.asarray(np.concatenate([np.cos(ang), np.sin(ang)], axis=0), jnp.bfloat16)


def _dft_inner_table(n1, n2):
    j2 = np.arange(n2)
    f_ang = -2.0 * np.pi * np.outer(np.arange(n2), j2) / n2
    tw_ang = -2.0 * np.pi * np.outer(np.arange(n1 // 2) + 0.5, j2) / (n1 * n2)
    fr, fi = jnp.asarray(np.cos(f_ang), jnp.float32), jnp.asarray(np.sin(f_ang), jnp.float32)
    twr, twi = jnp.asarray(np.cos(tw_ang), jnp.float32), jnp.asarray(np.sin(tw_ang), jnp.float32)
    mr = fr[None] * twr[:, None, :] - fi[None] * twi[:, None, :]
    mi = fr[None] * twi[:, None, :] + fi[None] * twr[:, None, :]
    return _bf(jnp.concatenate([jnp.concatenate([mr, -mi], axis=2), jnp.concatenate([mi, mr], axis=2)], axis=1))


def _spectral_product(xv, xh, inorm, half):
    inf, inb = inorm[0:1, :], inorm[1:2, :]
    gr = xh[:half, :HY_C] * inf + xh[:half, HY_C:] * inb
    gi = xh[half:, :HY_C] * inf - xh[half:, HY_C:] * inb
    xr, xi = xv[:half], xv[half:]
    return jnp.concatenate([xr * gr - xi * gi, xr * gi + xi * gr], axis=0).astype(jnp.bfloat16)


def _hy_spec_kernel(r_ref, avr_ref, avi_ref, ahr_ref, ahi_ref, inorm_ref, br_ref, bi_ref):
    r = r_ref[0]
    n2 = avr_ref.shape[1]
    xv = jnp.dot(r, jnp.concatenate([avr_ref[0], avi_ref[0]], axis=0), preferred_element_type=jnp.float32)
    xh = jnp.dot(r, jnp.concatenate([ahr_ref[0], ahi_ref[0]], axis=0), preferred_element_type=jnp.float32)
    b = _dot_tn(r, _spectral_product(xv, xh, inorm_ref[...], n2))
    br_ref[0] = b[:n2].astype(br_ref.dtype)
    bi_ref[0] = b[n2:].astype(bi_ref.dtype)


def _hy_spec(r, av, ah, inorm, n1, n2):
    av3 = av.reshape(n1, n2, HY_C)
    ah3 = ah.reshape(n1, n2, 2 * HY_C)
    h1 = n1 // 2
    out = jax.ShapeDtypeStruct((h1, n2, HY_C), jnp.bfloat16)
    return pl.pallas_call(
        _hy_spec_kernel,
        grid=(h1,),
        in_specs=[pl.BlockSpec((1, 2 * n2, 2 * n2), lambda k: (k, 0, 0)),
                  pl.BlockSpec((1, n2, HY_C), lambda k: (k, 0, 0)),
                  pl.BlockSpec((1, n2, HY_C), lambda k: (k + h1, 0, 0)),
                  pl.BlockSpec((1, n2, 2 * HY_C), lambda k: (k, 0, 0)),
                  pl.BlockSpec((1, n2, 2 * HY_C), lambda k: (k + h1, 0, 0)),
                  pl.BlockSpec((2, HY_C), lambda k: (0, 0))],
        out_specs=[pl.BlockSpec((1, n2, HY_C), lambda k: (k, 0, 0))] * 2,
        out_shape=[out, out],
        compiler_params=_cparams("parallel"),
        name="hyena_spectral",
    )(r, av3, av3, ah3, ah3, inorm)


def _hy_spec_direct_kernel(xv_ref, xh_ref, inorm_ref, yr_ref, yi_ref):
    half = yr_ref.shape[0]
    y = _spectral_product(xv_ref[...].astype(jnp.float32), xh_ref[...].astype(jnp.float32), inorm_ref[...], half)
    yr_ref[...] = y[:half]
    yi_ref[...] = y[half:]


def _hy_spec_direct(xv, xh, inorm, L):
    full = lambda a: pl.BlockSpec(a.shape, lambda i: (0, 0))
    out = jax.ShapeDtypeStruct((L, HY_C), jnp.bfloat16)
    return pl.pallas_call(
        _hy_spec_direct_kernel,
        grid=(1,),
        in_specs=[full(xv), full(xh), full(inorm)],
        out_specs=[pl.BlockSpec((L, HY_C), lambda i: (0, 0))] * 2,
        out_shape=[out, out],
        compiler_params=_cparams("arbitrary"),
        name="hyena_spectral_direct",
    )(xv, xh, inorm)


def _hy_post_kernel(tr_ref, ti_ref, br_ref, bi_ref, v_ref, x0_ref, skip_ref, o_ref, *, scale):
    acc = (jnp.dot(tr_ref[...], br_ref[...], preferred_element_type=jnp.float32)
           + jnp.dot(ti_ref[...], bi_ref[...], preferred_element_type=jnp.float32))
    y = (acc * scale + v_ref[...].astype(jnp.float32) * skip_ref[...]) * x0_ref[...].astype(jnp.float32)
    o_ref[...] = y.astype(o_ref.dtype)


def _hy_post(t_fwd, b_r, b_i, v, x0, skip, L, h1, n2):
    ncol = n2 * HY_C
    tn = min(4096, ncol)
    tr_t = t_fwd[:h1].T
    ti_t = t_fwd[h1:].T
    skip_t = jnp.tile(skip, tn // HY_C).reshape(1, tn)
    col = lambda rows: pl.BlockSpec((rows, tn), lambda j: (0, j))
    rows_out = tr_t.shape[0]
    y = pl.pallas_call(
        functools.partial(_hy_post_kernel, scale=1.0 / L),
        grid=(ncol // tn,),
        in_specs=[pl.BlockSpec(tr_t.shape, lambda j: (0, 0)), pl.BlockSpec(ti_t.shape, lambda j: (0, 0)),
                  col(h1), col(h1), col(rows_out), col(rows_out), pl.BlockSpec((1, tn), lambda j: (0, 0))],
        out_specs=col(rows_out),
        out_shape=jax.ShapeDtypeStruct((rows_out, ncol), jnp.bfloat16),
        compiler_params=_cparams("parallel"),
        name="hyena_inverse",
    )(tr_t, ti_t, b_r.reshape(h1, ncol), b_i.reshape(h1, ncol), v.reshape(rows_out, ncol),
      x0.reshape(rows_out, ncol), skip_t)
    return y.reshape(L, HY_C)


SUBLANE = 8
HY_COL_TILE = 512


def _dft_outer_kron(n1):
    h1 = n1 // 2
    ang = -2.0 * np.pi * np.outer(np.arange(h1) + 0.5, np.arange(h1)) / n1
    eye = np.eye(SUBLANE)
    t_r, t_i = np.cos(ang), np.sin(ang)
    fwd = np.kron(np.concatenate([t_r, t_i], axis=0), eye)
    inv = np.concatenate([np.kron(t_r.T, eye), np.kron(t_i.T, eye)], axis=1)
    return jnp.asarray(fwd, jnp.bfloat16), jnp.asarray(inv, jnp.bfloat16)


def _hy_outer_fwd_kernel(t_ref, x_ref, o_ref):
    x = x_ref[...]
    rows_in, rows_out = x.shape[0], t_ref.shape[0] // SUBLANE
    cw = x.shape[2]
    parts = []
    for s in range(0, x.shape[1], SUBLANE):
        xs = x[:, s:s + SUBLANE, :].reshape(rows_in * SUBLANE, cw).astype(jnp.bfloat16)
        r = jnp.dot(t_ref[...], xs, preferred_element_type=jnp.float32)
        parts.append(r.reshape(rows_out, SUBLANE, cw))
    o_ref[...] = jnp.concatenate(parts, axis=1).astype(o_ref.dtype)


def _hy_outer_fwd(t_kron, x3):
    h1, n2, w = x3.shape
    n1 = 2 * h1
    blk = 2 * SUBLANE
    return pl.pallas_call(
        _hy_outer_fwd_kernel,
        grid=(n2 // blk, w // HY_COL_TILE),
        in_specs=[pl.BlockSpec(t_kron.shape, lambda j, cc: (0, 0)),
                  pl.BlockSpec((h1, blk, HY_COL_TILE), lambda j, cc: (0, j, cc))],
        out_specs=pl.BlockSpec((n1, blk, HY_COL_TILE), lambda j, cc: (0, j, cc)),
        out_shape=jax.ShapeDtypeStruct((n1, n2, w), jnp.bfloat16),
        compiler_params=_cparams("parallel", "parallel"),
        name="hyena_outer_dft",
    )(t_kron, x3)


def _hy_outer_inv_kernel(t_ref, br_ref, bi_ref, v_ref, x0_ref, skip_ref, o_ref, *, scale):
    br = br_ref[...].astype(jnp.float32)
    bi = bi_ref[...].astype(jnp.float32)
    v = v_ref[...]
    x0 = x0_ref[...].astype(jnp.float32)
    h1, _, cw = br.shape
    parts = []
    for s in range(0, br.shape[1], SUBLANE):
        sl = slice(s, s + SUBLANE)
        b = jnp.concatenate([br[:, sl, :].reshape(h1 * SUBLANE, cw), bi[:, sl, :].reshape(h1 * SUBLANE, cw)], axis=0)
        r = jnp.dot(t_ref[...], b.astype(jnp.bfloat16), preferred_element_type=jnp.float32)
        parts.append((r.reshape(h1, SUBLANE, cw) * scale + v[:, sl, :] * skip_ref[...]) * x0[:, sl, :])
    o_ref[...] = jnp.concatenate(parts, axis=1).astype(o_ref.dtype)


def _hy_outer_inv(t_kron_inv, b_r, b_i, v3, x03, skip, L):
    h1, n2, w = b_r.shape
    blk = 2 * SUBLANE
    tile = pl.BlockSpec((h1, blk, HY_COL_TILE), lambda j, cc: (0, j, cc))
    return pl.pallas_call(
        functools.partial(_hy_outer_inv_kernel, scale=1.0 / L),
        grid=(n2 // blk, w // HY_COL_TILE),
        in_specs=[pl.BlockSpec(t_kron_inv.shape, lambda j, cc: (0, 0)), tile, tile, tile, tile,
                  pl.BlockSpec((1, 1, HY_COL_TILE), lambda j, cc: (0, 0, cc))],
        out_specs=tile,
        out_shape=jax.ShapeDtypeStruct((h1, n2, w), jnp.bfloat16),
        compiler_params=_cparams("parallel", "parallel"),
        name="hyena_outer_idft",
    )(t_kron_inv, b_r, b_i, v3, x03, skip.reshape(1, 1, w))


def _hyena(z, L, conv_w, conv_b, fparams, skip):
    v, x0 = _hy_pre(z, L, conv_w, conv_b)
    hfil, inorm = _hyena_filters(L, *fparams)
    if L >= HY_TWO_STAGE_MIN_L:
        n1 = HY_N1
        n2 = 2 * L // n1
        h1 = n1 // 2
        assert n2 % (2 * SUBLANE) == 0
        t_kron, t_kron_inv = _dft_outer_kron(n1)
        v3 = v.reshape(h1, n2, HY_C)
        av = _hy_outer_fwd(t_kron, v3)
        ah = _hy_outer_fwd(t_kron, hfil.reshape(h1, n2, 2 * HY_C))
        b_r, b_i = _hy_spec(_dft_inner_table(n1, n2), av, ah, inorm, n1, n2)
        return _hy_outer_inv(t_kron_inv, b_r, b_i, v3, x0.reshape(h1, n2, HY_C), skip, L).reshape(L, HY_C)
    t_fwd = _dft_outer_table(2 * L, L)
    xv = _matmul(t_fwd, _bf(v), tm=2 * L, tn=HY_C, out_dtype=jnp.bfloat16)
    xh = _matmul(t_fwd, _bf(hfil), tm=2 * L, tn=HY_C, out_dtype=jnp.bfloat16)
    y_r, y_i = _hy_spec_direct(xv, xh, inorm, L)
    return _hy_post(t_fwd, y_r, y_i, v, x0, skip, L, L, 1)


def _merge_kernel(yh_ref, yg_ref, yl_ref, gate_h_ref, gate_g_ref, gate_l_ref, wb_ref, o_ref):
    acc = None
    for br, (y_ref, g_ref) in enumerate(((yh_ref, gate_h_ref), (yg_ref, gate_g_ref), (yl_ref, gate_l_ref))):
        t = jnp.dot(y_ref[...], wb_ref[br], preferred_element_type=jnp.float32) * jax.nn.sigmoid(g_ref[...])
        acc = t if acc is None else acc + t
    o_ref[...] = acc.astype(o_ref.dtype)


def _proj_residual_kernel(m_ref, w_ref, h_ref, gt_ref, o_ref):
    o_ref[...] = h_ref[...] + gt_ref[...] * jnp.dot(m_ref[...], w_ref[...], preferred_element_type=jnp.float32)


MERGE_ROWS = 512


def _merge(z, L, ys, w_branch, w_out, h, gt):
    tm = min(MERGE_ROWS, L)
    gb = MG_OFF // D_MODEL
    ybs = pl.BlockSpec((tm, HY_C), lambda i: (i, 0))
    gate = lambda br: pl.BlockSpec((tm, D_MODEL), lambda i: (i, gb + br))
    row = pl.BlockSpec((tm, D_MODEL), lambda i: (i, 0))
    merged = pl.pallas_call(
        _merge_kernel,
        grid=(L // tm,),
        in_specs=[ybs, ybs, ybs, gate(0), gate(1), gate(2),
                  pl.BlockSpec((N_BRANCH, HY_C, D_MODEL), lambda i: (0, 0, 0), pipeline_mode=pl.Buffered(1))],
        out_specs=row,
        out_shape=jax.ShapeDtypeStruct((L, D_MODEL), jnp.bfloat16),
        compiler_params=_cparams("parallel"),
        name="branch_merge",
    )(ys[0], ys[1], ys[2], z, z, z, _bf(w_branch))
    return pl.pallas_call(
        _proj_residual_kernel,
        grid=(L // tm,),
        in_specs=[row, pl.BlockSpec((D_MODEL, D_MODEL), lambda i: (0, 0), pipeline_mode=pl.Buffered(1)), row,
                  pl.BlockSpec((1, D_MODEL), lambda i: (0, 0))],
        out_specs=row,
        out_shape=jax.ShapeDtypeStruct((L, D_MODEL), jnp.float32),
        compiler_params=_cparams("parallel"),
        name="out_proj_residual",
    )(merged, _bf(w_out), h, gt)


def _pad_cols(a):
    pad = lambda n: jnp.zeros(a.shape[:-1] + (n,), a.dtype)
    return jnp.concatenate([a[..., :REC_COLS], pad(HY_OFF - REC_COLS), a[..., REC_COLS:]], axis=-1)


W_TILE = 1024
W_SHIFT = HY_OFF - REC_COLS
W_ROW_OFF = W_TILE - W_SHIFT
assert 0 < W_SHIFT <= W_TILE and W_ROW_OFF % 8 == 0 and HY_OFF % W_TILE == 0 and Z_COLS % W_TILE == 0


def _w_in_prep_kernel(a_ref, b_ref, o_ref):
    j = pl.program_id(0)
    shifted = j >= HY_OFF // W_TILE

    @pl.when(jnp.logical_not(shifted))
    def _():
        o_ref[...] = a_ref[...].T.astype(o_ref.dtype)

    @pl.when(shifted)
    def _():
        window = jnp.concatenate([a_ref[...], b_ref[...]], axis=0)
        o_ref[...] = window[W_ROW_OFF:W_ROW_OFF + W_TILE].T.astype(o_ref.dtype)


def _w_in_prep(w_in, layer):
    w_t = jnp.swapaxes(w_in, 1, 2)
    _, n, k = w_t.shape
    first_shifted = HY_OFF // W_TILE
    a_idx = lambda j: jnp.where(j < first_shifted, j, j - 1)
    tail_blocks = W_TILE // W_ROW_OFF
    assert W_TILE % W_ROW_OFF == 0 and n % W_ROW_OFF == 0
    return pl.pallas_call(
        _w_in_prep_kernel,
        grid=(Z_COLS // W_TILE,),
        in_specs=[pl.BlockSpec((None, W_TILE, k), lambda j: (layer, a_idx(j), 0)),
                  pl.BlockSpec((None, W_ROW_OFF, k), lambda j: (layer, (a_idx(j) + 1) * tail_blocks, 0))],
        out_specs=pl.BlockSpec((k, W_TILE), lambda j: (0, j)),
        out_shape=jax.ShapeDtypeStruct((k, Z_COLS), jnp.bfloat16),
        compiler_params=_cparams("parallel"),
        name="w_in_relayout",
    )(w_t, w_t)


def _mixer(h, hc, u, uc, gt, gtc, p, need_ctx):
    L, Lc = u.shape[0], uc.shape[0]
    w_in = _w_in_prep(p['w_in'], p['layer'])
    b_in = _pad_cols(p['b_in'])
    z = _matmul(u, w_in, b_in, tm=1024, tn=W_TILE)
    ncol = Z_COLS if need_ctx else HY_OFF
    zc = _matmul(uc, w_in, b_in[:ncol], tm=1024, tn=W_TILE, n_cols=ncol)
    zeros = lambda hd, dk, dv: jnp.zeros((hd, dv, dk), jnp.float32)
    yc_hg, hg_sf, hg_sb = _hgrn2(zc, Lc, p['lb'], p['hg_norm_w'],
                                 zeros(HG_HEADS, HG_DK, HG_DV), zeros(HG_HEADS, HG_DK, HG_DV))
    yc_gl, gl_sf, gl_sb = _gla(zc, Lc, p['gl_w_a2'], p['gl_b_a'], p['gl_norm_w'],
                               zeros(GL_HEADS, GL_DK, GL_DV), zeros(GL_HEADS, GL_DK, GL_DV))
    y_hg, _, _ = _hgrn2(z, L, p['lb'], p['hg_norm_w'], hg_sf, hg_sb)
    y_gl, _, _ = _gla(z, L, p['gl_w_a2'], p['gl_b_a'], p['gl_norm_w'], gl_sf, gl_sb)
    hy = (p['hy_conv_w'], p['hy_conv_b'], p['hy_f'], p['hy_skip'])
    y_hy = _hyena(z, L, *hy)
    h = _merge(z, L, (y_hy, y_hg, y_gl), p['w_branch'], p['w_out'], h, gt)
    if need_ctx:
        yc_hy = _hyena(zc, Lc, *hy)
        hc = _merge(zc, Lc, (yc_hy, yc_hg, yc_gl), p['w_branch'], p['w_out'], hc, gtc)
    return h, hc


GATHER_UNROLL = 8
GATHER_AHEAD = 2
GATHER_SLOTS = GATHER_AHEAD + 1


def _ffn_kernel(blk_exp_ref, n_used_ref, tok_ref, x_hbm, wg_ref, wu_ref, wd_ref, o_ref, xbuf, sem, wg_s, wu_s, wd_s):
    i = pl.program_id(0)
    n_used = n_used_ref[0]
    rows = o_ref.shape[0]

    def issue(step, slot):
        def body(r, carry):
            src = tok_ref[step * rows + r]
            pltpu.make_async_copy(x_hbm.at[pl.ds(src, 1)], xbuf.at[slot, pl.ds(r, 1)], sem.at[slot]).start()
            return carry

        lax.fori_loop(0, rows, body, 0, unroll=GATHER_UNROLL)

    for s in range(GATHER_AHEAD):
        @pl.when(jnp.logical_and(i == 0, s < n_used))
        def _(s=s):
            issue(s, s % GATHER_SLOTS)

    @pl.when(i + GATHER_AHEAD < n_used)
    def _():
        issue(i + GATHER_AHEAD, (i + GATHER_AHEAD) % GATHER_SLOTS)

    new_expert = jnp.logical_or(i == 0, blk_exp_ref[i] != blk_exp_ref[jnp.maximum(i - 1, 0)])

    @pl.when(jnp.logical_and(i < n_used, new_expert))
    def _():
        wg_s[...] = wg_ref[0].astype(jnp.bfloat16)
        wu_s[...] = wu_ref[0].astype(jnp.bfloat16)
        wd_s[...] = wd_ref[0].astype(jnp.bfloat16)

    @pl.when(i < n_used)
    def _():
        slot = i % GATHER_SLOTS
        pltpu.make_async_copy(xbuf.at[slot], xbuf.at[slot], sem.at[slot]).wait()
        x = xbuf[slot].astype(jnp.bfloat16)
        hg = jnp.dot(x, wg_s[...], preferred_element_type=jnp.float32)
        hu = jnp.dot(x, wu_s[...], preferred_element_type=jnp.float32)
        act = (hg * jax.nn.sigmoid(hg) * hu).astype(jnp.bfloat16)
        o_ref[...] = jnp.dot(act, wd_s[...], preferred_element_type=jnp.float32)

    @pl.when(i >= n_used)
    def _():
        o_ref[...] = jnp.zeros_like(o_ref)


def _grouped_ffn(x, buf_tok, blk_exp, n_used, layer, w_gate, w_up, w_down):
    p_len = buf_tok.shape[0]
    d = x.shape[1]
    n_blk = p_len // MOE_BLOCK
    grid_spec = pltpu.PrefetchScalarGridSpec(
        num_scalar_prefetch=3,
        grid=(n_blk,),
        in_specs=[pl.BlockSpec(memory_space=pl.ANY),
                  pl.BlockSpec((None, 1, d, D_FF), lambda i, be, nu, tk: (layer, be[i], 0, 0)),
                  pl.BlockSpec((None, 1, d, D_FF), lambda i, be, nu, tk: (layer, be[i], 0, 0)),
                  pl.BlockSpec((None, 1, D_FF, d), lambda i, be, nu, tk: (layer, be[i], 0, 0))],
        out_specs=pl.BlockSpec((MOE_BLOCK, d), lambda i, be, nu, tk: (i, 0)),
        scratch_shapes=[pltpu.VMEM((GATHER_SLOTS, MOE_BLOCK, d), jnp.float32), pltpu.SemaphoreType.DMA((GATHER_SLOTS,)),
                        pltpu.VMEM((d, D_FF), jnp.bfloat16), pltpu.VMEM((d, D_FF), jnp.bfloat16),
                        pltpu.VMEM((D_FF, d), jnp.bfloat16)],
    )
    return pl.pallas_call(
        _ffn_kernel,
        grid_spec=grid_spec,
        out_shape=jax.ShapeDtypeStruct((p_len, d), jnp.float32),
        compiler_params=_cparams("arbitrary"),
        name="moe_grouped_ffn",
    )(blk_exp, n_used, buf_tok, x, w_gate, w_up, w_down)


ROUTER_COLS = 128


def _hier_moe(h, norm_w, shift, scale, gt, p):
    n, d = h.shape
    pad = ROUTER_COLS - N_GROUPS - N_EXPERTS
    w_r = _bf(jnp.concatenate([p['w_rg'], p['w_re'], jnp.zeros((d, pad), jnp.float32)], axis=1))
    b_r = jnp.concatenate([p['b_rg'], p['b_re'], jnp.zeros((pad,), jnp.float32)]).reshape(1, ROUTER_COLS)
    xb, logits = _norm_mod(h, norm_w, shift, scale, router=(w_r, b_r), out_dtype=jnp.float32)
    lg = logits[:, :N_GROUPS]
    p_grp = jax.nn.softmax(lg, axis=-1)
    grp = jnp.argmax(p_grp, axis=-1).astype(jnp.int32)
    p_top = jnp.max(p_grp, axis=-1)
    le = logits[:, N_GROUPS:N_GROUPS + N_EXPERTS].reshape(n, N_GROUPS, EXP_PER_GROUP)
    le = jnp.take_along_axis(le, grp[:, None, None], axis=1)[:, 0]
    top_p, top_i = lax.top_k(jax.nn.softmax(le, axis=-1), TOP_K)
    weight = p_top[:, None] * top_p / jnp.sum(top_p, axis=-1, keepdims=True)
    expert = grp[:, None] * EXP_PER_GROUP + top_i.astype(jnp.int32)
    a = n * TOP_K
    e_flat = expert.reshape(a)
    onehot = (e_flat[:, None] == jnp.arange(N_EXPERTS, dtype=jnp.int32)[None, :]).astype(jnp.int32)
    rank = jnp.take_along_axis(jnp.cumsum(onehot, axis=0) - onehot, e_flat[:, None], axis=1)[:, 0]
    counts = jnp.sum(onehot, axis=0)
    padded = (counts + MOE_BLOCK - 1) // MOE_BLOCK * MOE_BLOCK
    pad_end = jnp.cumsum(padded)
    pad_off = pad_end - padded
    pos = pad_off[e_flat] + rank
    p_len = (a + N_EXPERTS * MOE_BLOCK + MOE_BLOCK - 1) // MOE_BLOCK * MOE_BLOCK
    n_blk = p_len // MOE_BLOCK
    tok_flat = jnp.arange(a, dtype=jnp.int32) // TOP_K
    buf_tok = (jnp.arange(p_len, dtype=jnp.int32) % n).at[pos].set(tok_flat)
    blk_start = jnp.arange(n_blk, dtype=jnp.int32) * MOE_BLOCK
    blk_exp = jnp.minimum(jnp.sum(pad_end[None, :] <= blk_start[:, None], axis=1), N_EXPERTS - 1).astype(jnp.int32)
    n_used = (pad_end[-1:] // MOE_BLOCK).astype(jnp.int32)
    y = _grouped_ffn(xb, buf_tok, blk_exp, n_used, p['layer'], p['w_gate'], p['w_up'], p['w_down'])
    return _moe_combine(y, pos, weight, h, gt)


def _combine_kernel(pos_ref, y_hbm, wts_ref, h_ref, gt_ref, o_ref, buf, sem):
    i = pl.program_id(0)
    tokens = h_ref.shape[0]

    def row_copy(step, slot, r, k):
        src = pos_ref[(step * tokens + r) * TOP_K + k]
        return pltpu.make_async_copy(y_hbm.at[pl.ds(src, 1)], buf.at[slot, k, pl.ds(r, 1)], sem.at[slot])

    def issue(step, slot):
        def body(r, carry):
            for k in range(TOP_K):
                row_copy(step, slot, r, k).start()
            return carry

        lax.fori_loop(0, tokens, body, 0, unroll=GATHER_UNROLL // TOP_K)

    @pl.when(i == 0)
    def _():
        issue(0, 0)

    @pl.when(i + 1 < pl.num_programs(0))
    def _():
        issue(i + 1, (i + 1) % 2)

    slot = i % 2
    pltpu.make_async_copy(buf.at[slot], buf.at[slot], sem.at[slot]).wait()
    rows = buf[slot]
    wts = wts_ref[...]
    acc = rows[0] * wts[:, 0:1]
    for k in range(1, TOP_K):
        acc = acc + rows[k] * wts[:, k:k + 1]
    o_ref[...] = h_ref[...] + gt_ref[...] * acc


COMBINE_TOKENS = 128


def _moe_combine(y, pos, wts, h, gt):
    n, d = h.shape
    tokens = min(COMBINE_TOKENS, n)
    grid_spec = pltpu.PrefetchScalarGridSpec(
        num_scalar_prefetch=1,
        grid=(n // tokens,),
        in_specs=[pl.BlockSpec(memory_space=pl.ANY),
                  pl.BlockSpec((tokens, TOP_K), lambda i, pos: (i, 0)),
                  pl.BlockSpec((tokens, d), lambda i, pos: (i, 0)),
                  pl.BlockSpec((1, d), lambda i, pos: (0, 0))],
        out_specs=pl.BlockSpec((tokens, d), lambda i, pos: (i, 0)),
        scratch_shapes=[pltpu.VMEM((2, TOP_K, tokens, d), jnp.float32), pltpu.SemaphoreType.DMA((2,))],
    )
    return pl.pallas_call(
        _combine_kernel,
        grid_spec=grid_spec,
        out_shape=jax.ShapeDtypeStruct((n, d), jnp.float32),
        compiler_params=_cparams("arbitrary"),
        name="moe_combine",
    )(pos, y, wts, h, gt)


def _final_norm_kernel(x_ref, w_ref, o_ref):
    x = x_ref[...]
    o_ref[...] = x * lax.rsqrt(jnp.mean(x * x, axis=-1, keepdims=True) + NORM_EPS) * w_ref[...]


def _final_norm(x, w, tm=512):
    m, d = x.shape
    return pl.pallas_call(
        _final_norm_kernel,
        grid=(m // tm,),
        in_specs=[pl.BlockSpec((tm, d), lambda i: (i, 0)), pl.BlockSpec((1, d), lambda i: (0, 0))],
        out_specs=pl.BlockSpec((tm, d), lambda i: (i, 0)),
        out_shape=jax.ShapeDtypeStruct((m, d), jnp.float32),
        compiler_params=_cparams("parallel"),
        name="final_rmsnorm",
    )(x, w.reshape(1, d))


def kernel(x, c, ctx, c_ctx, w_mod, b_mod, norm_mix_w, norm_ffn_w, w_in, b_in, hy_conv_w, hy_conv_b, hy_f_w1, hy_f_b1, hy_f_w2, hy_f_b2, hy_f_w3, hy_f_b3, hy_f_freq, hy_skip, hg_lb_raw, hg_norm_w, gl_w_a2, gl_b_a, gl_norm_w, w_branch, w_out, w_rg, b_rg, w_re, b_re, w_gate, w_up, w_down, final_norm_w):
    assert x.shape[0] == 1
    depth = w_mod.shape[0]
    lb_all = jnp.cumsum(jax.nn.softmax(hg_lb_raw, axis=0), axis=0)
    lb_all = lb_all - lb_all[:1]
    h, hc = x[0], ctx[0]
    cc = jnp.concatenate([c, c_ctx[None, :]], axis=0)
    for l in range(depth):
        need_ctx = l < depth - 1
        mod = _matmul(_bf(jax.nn.silu(cc)), w_mod, b_mod[l], layer=l)
        sh1, sc1, gt1, sh2, sc2, gt2 = jnp.split(mod[0:1], 6, axis=-1)
        sh1c, sc1c, gt1c, sh2c, sc2c, gt2c = jnp.split(mod[1:2], 6, axis=-1)
        p = dict(w_in=w_in, b_in=b_in[l], hy_conv_w=hy_conv_w[l], hy_conv_b=hy_conv_b[l],
                 hy_f=(hy_f_w1[l], hy_f_b1[l], hy_f_w2[l], hy_f_b2[l], hy_f_w3[l], hy_f_b3[l], hy_f_freq[l]),
                 hy_skip=hy_skip[l], lb=lb_all[l], hg_norm_w=hg_norm_w[l], gl_w_a2=gl_w_a2[l], gl_b_a=gl_b_a[l],
                 gl_norm_w=gl_norm_w[l], w_branch=w_branch[l], w_out=w_out[l], w_rg=w_rg[l], b_rg=b_rg[l],
                 w_re=w_re[l], b_re=b_re[l], layer=l, w_gate=w_gate, w_up=w_up, w_down=w_down)
        u = _norm_mod(h, norm_mix_w[l], sh1, sc1)
        uc = _norm_mod(hc, norm_mix_w[l], sh1c, sc1c)
        h, hc = _mixer(h, hc, u, uc, gt1, gt1c, p, need_ctx)
        h = _hier_moe(h, norm_ffn_w[l], sh2, sc2, gt2, p)
        if need_ctx:
            hc = _hier_moe(hc, norm_ffn_w[l], sh2c, sc2c, gt2c, p)
    return _final_norm(h, final_norm_w)[None]
```

```python
import functools
import math

import jax
import jax.numpy as jnp
import numpy as np
from jax import lax
from jax.experimental import pallas as pl
from jax.experimental.pallas import tpu as pltpu

D_MODEL = 2048
NORM_EPS = 1e-6

HY_C = D_MODEL // 2
HY_EMB = 33
HY_BANDS = (HY_EMB - 1) // 2
HY_DECAY_TARGET = 1e-2
HY_FAST_PCT = 0.3
HY_SLOW_PCT = 1.5
HY_MOD_SHIFT = 0.05

HG_HEADS = 8
HG_DK = 128
HG_DV = 128
HG_K = HG_HEADS * HG_DK
HG_V = HG_HEADS * HG_DV

GL_HEADS = 4
GL_DK = 128
GL_DV = 256
GL_K = GL_HEADS * GL_DK
GL_V = GL_HEADS * GL_DV
GL_RANK = 16
GL_TAU = 16.0

N_BRANCH = 3
HG_COLS = 3 * HG_K + 2 * HG_V
GL_COLS = 2 * GL_K + 2 * GL_V + 2 * GL_RANK
REC_COLS = HG_COLS + GL_COLS
HY_COLS = 3 * HY_C
MERGE_COLS = N_BRANCH * D_MODEL

COL_TILE = 512
GL_Q_OFF = HG_COLS
GL_V_OFF = GL_Q_OFF + 2 * GL_K
GL_A_OFF = GL_V_OFF + 2 * GL_V
HY_OFF = -(-(GL_A_OFF + 2 * GL_RANK) // HY_C) * HY_C
MG_OFF = HY_OFF + HY_COLS
Z_COLS = MG_OFF + MERGE_COLS
assert GL_Q_OFF % GL_K == 0 and GL_V_OFF % GL_V == 0 and GL_A_OFF % 128 == 0 and Z_COLS % COL_TILE == 0
assert MG_OFF % D_MODEL == 0

N_GROUPS = 4
EXP_PER_GROUP = 8
N_EXPERTS = N_GROUPS * EXP_PER_GROUP
TOP_K = 2
D_FF = D_MODEL // 4
MOE_BLOCK = 256

SCAN_CHUNK = 128
SCAN_HEAD_GROUP = 8
LOG2_E = 1.4426950408889634

VMEM_LIMIT_BYTES = 56 * 1024 * 1024


def _cparams(*sem):
    return pltpu.CompilerParams(dimension_semantics=sem, vmem_limit_bytes=VMEM_LIMIT_BYTES)


def _bf(a):
    return a.astype(jnp.bfloat16)


def _mm_kernel(x_ref, w_ref, b_ref, o_ref):
    acc = jnp.dot(x_ref[...], w_ref[...].astype(jnp.bfloat16), preferred_element_type=jnp.float32) + b_ref[...]
    o_ref[...] = acc.astype(o_ref.dtype)


def _matmul(x, w, bias=None, tm=512, tn=COL_TILE, out_dtype=jnp.float32, n_cols=None, layer=None):
    m, k = x.shape
    n = w.shape[-1] if n_cols is None else n_cols
    assert n_cols is None or n_cols % tn == 0
    assert layer is None or n % tn == 0
    tm = min(tm, -(-m // 8) * 8)
    mp = -(-m // tm) * tm
    np_ = -(-n // tn) * tn
    if bias is None:
        bias = jnp.zeros((n,), jnp.float32)
    if mp != m:
        x = jnp.pad(x, ((0, mp - m), (0, 0)))
    if np_ != n:
        w = jnp.pad(w, ((0, 0), (0, np_ - n)))
        bias = jnp.pad(bias, (0, np_ - n))
    if layer is None:
        w_spec = pl.BlockSpec((k, tn), lambda i, j: (0, j))
    else:
        w_spec = pl.BlockSpec((None, k, tn), lambda i, j: (layer, 0, j))
    out = pl.pallas_call(
        _mm_kernel,
        grid=(mp // tm, np_ // tn),
        in_specs=[pl.BlockSpec((tm, k), lambda i, j: (i, 0)),
                  w_spec,
                  pl.BlockSpec((1, tn), lambda i, j: (0, j))],
        out_specs=pl.BlockSpec((tm, tn), lambda i, j: (i, j)),
        out_shape=jax.ShapeDtypeStruct((mp, np_), out_dtype),
        compiler_params=_cparams("parallel", "arbitrary"),
        name="dense_matmul",
    )(x, w, bias.reshape(1, np_))
    if mp != m or np_ != n:
        out = out[:m, :n]
    return out


def _norm_mod_kernel(*refs, with_router):
    if with_router:
        h_ref, w_ref, sh_ref, sc_ref, wr_ref, br_ref, o_ref, lg_ref = refs
    else:
        h_ref, w_ref, sh_ref, sc_ref, o_ref = refs
    x = h_ref[...]
    y = x * lax.rsqrt(jnp.mean(x * x, axis=-1, keepdims=True) + NORM_EPS) * w_ref[...]
    u = (y * (1.0 + sc_ref[...]) + sh_ref[...]).astype(jnp.bfloat16)
    o_ref[...] = u.astype(o_ref.dtype)
    if with_router:
        lg_ref[...] = jnp.dot(u, wr_ref[...], preferred_element_type=jnp.float32) + br_ref[...]


def _norm_mod(h, w, shift, scale, router=None, out_dtype=jnp.bfloat16):
    m, d = h.shape
    tm = min(512, m)
    row = pl.BlockSpec((tm, d), lambda i: (i, 0))
    vec = pl.BlockSpec((1, d), lambda i: (0, 0))
    args = [h, w.reshape(1, d), shift, scale]
    specs = [row, vec, vec, vec]
    out_shape = [jax.ShapeDtypeStruct((m, d), out_dtype)]
    out_specs = [row]
    if router is not None:
        args += list(router)
        specs += [pl.BlockSpec(router[0].shape, lambda i: (0, 0)), pl.BlockSpec(router[1].shape, lambda i: (0, 0))]
        out_shape.append(jax.ShapeDtypeStruct((m, router[0].shape[1]), jnp.float32))
        out_specs.append(pl.BlockSpec((tm, router[0].shape[1]), lambda i: (i, 0)))
    out = pl.pallas_call(
        functools.partial(_norm_mod_kernel, with_router=router is not None),
        grid=(m // tm,),
        in_specs=specs,
        out_specs=out_specs,
        out_shape=out_shape,
        compiler_params=_cparams("parallel"),
        name="norm_modulate",
    )(*args)
    return out if router is not None else out[0]


def _scan_masks(c, reverse):
    t = np.arange(c)
    ms = [np.eye(c, dtype=np.float32)]
    for lvl in range(int(math.log2(c))):
        upper = ((t >> lvl) & 1).astype(bool)
        same = (t[:, None] >> (lvl + 1)) == (t[None, :] >> (lvl + 1))
        m = same & upper[:, None] & (~upper)[None, :]
        ms.append((m.T if reverse else m).astype(np.float32))
    tri = t[None, :] >= t[:, None] if reverse else t[None, :] <= t[:, None]
    return jnp.asarray(np.stack(ms)), jnp.asarray(tri.astype(np.float32), dtype=jnp.bfloat16)


def _level_arg(cum, lvl, reverse):
    c = cum.shape[0]
    blk = 1 << lvl
    if blk >= 8:
        pieces = []
        for gs in range(0, c, 2 * blk):
            ref = cum[gs + blk:gs + blk + 1, :]
            pieces.append(ref - cum[gs:gs + blk, :])
            pieces.append(cum[gs + blk:gs + 2 * blk, :] - ref)
        arg = jnp.concatenate(pieces, axis=0)
    else:
        c3 = cum.reshape(c // 8, 8, cum.shape[1])
        sub = lax.broadcasted_iota(jnp.int32, c3.shape, 1)
        ref_row = ((sub >> lvl) | 1) << lvl
        ref = None
        for r in range(blk, 8, 2 * blk):
            cand = jnp.broadcast_to(c3[:, r:r + 1, :], c3.shape)
            ref = cand if ref is None else jnp.where(ref_row == r, cand, ref)
        upper = ((sub >> lvl) & 1) == 1
        arg = jnp.where(upper, c3 - ref, ref - c3).reshape(cum.shape)
    return -arg if reverse else arg


def _dot_nt(a, b):
    return lax.dot_general(a, b, (((1,), (1,)), ((), ())), preferred_element_type=jnp.float32)


def _dot_tn(a, b):
    return lax.dot_general(a, b, (((0,), (0,)), ((), ())), preferred_element_type=jnp.float32)


def _sigmoid_parts(z):
    e = jnp.exp(-jnp.abs(z))
    r = 1.0 / (1.0 + e)
    return jnp.minimum(z, 0.0) - jnp.log(1.0 + e), jnp.where(z >= 0.0, e * r, r)


def _scan_kernel(*refs, mode, reverse, final, heads, dk, dv, c):
    it = iter(refs)
    q_ref = next(it)
    k_ref = next(it)
    v_ref = next(it)
    if mode == "hg":
        lbp_ref = next(it)
    else:
        a_ref = next(it)
        wa_ref = next(it)
        ba_ref = next(it)
    s0_ref = next(it)
    masks_ref = next(it)
    tri_ref = next(it)
    if final:
        oprev_ref = next(it)
        gate_ref = next(it)
        nw_ref = next(it)
    o_ref = next(it)
    st_ref = next(it)

    @pl.when(pl.program_id(0) == 0)
    def _():
        st_ref[...] = s0_ref[...]

    if mode == "gl":
        la_all = jnp.dot(a_ref[...].astype(jnp.bfloat16), wa_ref[...],
                         preferred_element_type=jnp.float32) + ba_ref[...]
    tri = tri_ref[...]
    tot_row = 0 if reverse else c - 1
    n_lvl = int(math.log2(c))
    ksl = lambda h: slice(h * dk, (h + 1) * dk)
    vsl = lambda h: slice(h * dv, (h + 1) * dv)
    for h0 in range(0, heads, SCAN_HEAD_GROUP):
        group = range(h0, min(h0 + SCAN_HEAD_GROUP, heads))
        q_, k_, cum_, qb_, kb_, o_, sc_ = {}, {}, {}, {}, {}, {}, {}
        for h in group:
            q = q_ref[:, ksl(h)]
            if mode == "hg":
                log_sig, sig_neg = _sigmoid_parts(k_ref[:, ksl(h)])
                la = lbp_ref[0:1, ksl(h)]
                lbb = lbp_ref[1:2, ksl(h)] + log_sig
                g = jnp.maximum(la, lbb) + jnp.log(1.0 + jnp.exp(-jnp.abs(la - lbb)))
                k = lbp_ref[2:3, ksl(h)] * sig_neg
                q = q * jax.nn.sigmoid(q)
            else:
                g = _sigmoid_parts(la_all[:, ksl(h)])[0] * (1.0 / GL_TAU)
                k = k_ref[:, ksl(h)]
                q = q * (dk ** -0.5)
            g = g * LOG2_E
            g1 = g.astype(jnp.bfloat16)
            r1 = g - g1.astype(jnp.float32)
            g2 = r1.astype(jnp.bfloat16)
            g3 = (r1 - g2.astype(jnp.float32)).astype(jnp.bfloat16)
            cum_[h] = (jnp.dot(tri, g1, preferred_element_type=jnp.float32)
                       + jnp.dot(tri, g2, preferred_element_type=jnp.float32)
                       + jnp.dot(tri, g3, preferred_element_type=jnp.float32))
            q_[h], k_[h] = q, k
        for h in group:
            cum = cum_[h]
            tot = cum[tot_row:tot_row + 1, :]
            st = st_ref[h]
            v = v_ref[:, vsl(h)].astype(jnp.bfloat16)
            o_[h] = _dot_nt((q_[h] * jnp.exp2(cum)).astype(jnp.bfloat16), st.astype(jnp.bfloat16))
            kt = (k_[h] * jnp.exp2(tot - cum)).astype(jnp.bfloat16)
            st_ref[h] = st * jnp.exp2(tot) + _dot_tn(v, kt)
            qb_[h] = q_[h].astype(jnp.bfloat16)
            kb_[h] = k_[h].astype(jnp.bfloat16)
            sc_[h] = masks_ref[0] * _dot_nt(qb_[h], kb_[h])
        for lvl in range(n_lvl):
            for h in group:
                e = jnp.exp2(_level_arg(cum_[h], lvl, reverse)).astype(jnp.bfloat16)
                sc_[h] = sc_[h] + masks_ref[1 + lvl] * _dot_nt(qb_[h] * e, kb_[h] * e)
        for h in group:
            v = v_ref[:, vsl(h)].astype(jnp.bfloat16)
            o = o_[h] + jnp.dot(sc_[h].astype(jnp.bfloat16), v, preferred_element_type=jnp.float32)
            if final:
                o = o + oprev_ref[:, vsl(h)]
                y = o * lax.rsqrt(jnp.mean(o * o, axis=-1, keepdims=True) + NORM_EPS) * nw_ref[...]
                gt = gate_ref[:, vsl(h)]
                act = jax.nn.sigmoid(gt) if mode == "hg" else gt * jax.nn.sigmoid(gt)
                o_ref[:, vsl(h)] = (y * act).astype(o_ref.dtype)
            else:
                o_ref[:, vsl(h)] = o


def _scan_pass(mode, reverse, final, L, srcs, s0, params, final_srcs=(), norm_w=None):
    heads, dk, dv = (HG_HEADS, HG_DK, HG_DV) if mode == "hg" else (GL_HEADS, GL_DK, GL_DV)
    c = min(SCAN_CHUNK, L)
    nb = L // c
    row = (lambda i: nb - 1 - i) if reverse else (lambda i: i)
    masks, tri = _scan_masks(c, reverse)

    def const(shape):
        return pl.BlockSpec(shape, lambda i: (0,) * len(shape))

    def rowblock(width, cb):
        return pl.BlockSpec((c, width), lambda i: (row(i), cb))

    args = [a for a, _, _ in srcs] + list(params) + [s0, masks, tri]
    specs = ([rowblock(w, cb) for _, w, cb in srcs] + [const(p.shape) for p in params]
             + [const(s0.shape), const(masks.shape), const(tri.shape)])
    if final:
        args += [a for a, _, _ in final_srcs] + [norm_w]
        specs += [rowblock(w, cb) for _, w, cb in final_srcs] + [const(norm_w.shape)]
    return pl.pallas_call(
        functools.partial(_scan_kernel, mode=mode, reverse=reverse, final=final, heads=heads, dk=dk, dv=dv, c=c),
        grid=(nb,),
        in_specs=specs,
        out_specs=[pl.BlockSpec((c, heads * dv), lambda i: (row(i), 0)), const((heads, dv, dk))],
        out_shape=[jax.ShapeDtypeStruct((L, heads * dv), jnp.bfloat16 if final else jnp.float32),
                   jax.ShapeDtypeStruct((heads, dv, dk), jnp.float32)],
        compiler_params=_cparams("arbitrary"),
        name=f"scan_{mode}_{'bwd' if reverse else 'fwd'}",
    )(*args)


def _hgrn2(z, L, lb, norm_w, s0_f, s0_b):
    lbp = lambda d: jnp.stack([jnp.log(lb[d]), jnp.log1p(-lb[d]), 1.0 - lb[d]])
    w = HG_K
    o_b, s_b = _scan_pass("hg", True, False, L, [(z, w, 0), (z, w, 2), (z, w, 3)], s0_b, [lbp(1)])
    y, s_f = _scan_pass("hg", False, True, L, [(z, w, 0), (z, w, 1), (z, w, 3)], s0_f, [lbp(0)],
                        final_srcs=[(o_b, HG_V, 0), (z, HG_V, 4)], norm_w=norm_w.reshape(1, HG_DV))
    return y, s_f, s_b


def _gla(z, L, w_a2, b_a, norm_w, s0_f, s0_b):
    def gate_params(d):
        wa = jnp.zeros((128, GL_K), jnp.float32).at[d * GL_RANK:(d + 1) * GL_RANK].set(w_a2[d])
        return [_bf(wa), b_a[d].reshape(1, GL_K)]

    srcs = [(z, GL_K, GL_Q_OFF // GL_K), (z, GL_K, GL_Q_OFF // GL_K + 1), (z, GL_V, GL_V_OFF // GL_V),
            (z, 128, GL_A_OFF // 128)]
    o_b, s_b = _scan_pass("gl", True, False, L, srcs, s0_b, gate_params(1))
    y, s_f = _scan_pass("gl", False, True, L, srcs, s0_f, gate_params(0),
                        final_srcs=[(o_b, GL_V, 0), (z, GL_V, GL_V_OFF // GL_V + 1)], norm_w=norm_w.reshape(1, GL_DV))
    return y, s_f, s_b


HY_FEAT_PAD = 128


def _hy_filter_kernel(f_ref, w1_ref, b1_ref, fq_ref, w2_ref, b2_ref, w3_ref, b3_ref, dl_ref, h_ref, s_ref):
    i = pl.program_id(0)
    f = f_ref[...]
    fq = fq_ref[...]
    a = jnp.sin(fq * (jnp.dot(f.astype(jnp.bfloat16), w1_ref[...], preferred_element_type=jnp.float32) + b1_ref[...]))
    a = jnp.sin(fq * (jnp.dot(a.astype(jnp.bfloat16), w2_ref[...], preferred_element_type=jnp.float32) + b2_ref[...]))
    hh = jnp.dot(a.astype(jnp.bfloat16), w3_ref[...], preferred_element_type=jnp.float32) + b3_ref[...]
    hh = hh * (jnp.exp(-f[:, 0:1] * dl_ref[...]) + HY_MOD_SHIFT)
    h_ref[...] = hh
    part = jnp.sum(jnp.abs(hh).reshape(hh.shape[0] // 8, 8, hh.shape[1]), axis=0)

    @pl.when(i == 0)
    def _():
        s_ref[...] = part

    @pl.when(i > 0)
    def _():
        s_ref[...] += part


def _hyena_filters(L, w1, b1, w2, b2, w3, b3, freq):
    t = jnp.linspace(0.0, 1.0, L, dtype=jnp.float32)[:, None]
    ang = 2.0 * math.pi * jnp.arange(L, dtype=jnp.float32)[:, None] / L
    bands = jnp.linspace(1e-4, HY_BANDS - 1, HY_BANDS, dtype=jnp.float32)[None, :]
    feats = jnp.concatenate([t, jnp.cos(bands * ang), -jnp.sin(bands * ang),
                             jnp.zeros((L, HY_FEAT_PAD - HY_EMB), jnp.float32)], axis=-1)
    deltas = jnp.abs(jnp.linspace(math.log(HY_DECAY_TARGET) / HY_SLOW_PCT, math.log(HY_DECAY_TARGET) / HY_FAST_PCT,
                                  HY_C, dtype=jnp.float32))
    fh = w1.shape[1]
    padm = lambda a, r, c: _bf(jnp.pad(a, ((0, r - a.shape[0]), (0, c - a.shape[1]))))
    padv = lambda a: jnp.pad(a, (0, HY_FEAT_PAD - a.shape[0])).reshape(1, HY_FEAT_PAD)
    tm = min(512, L)
    const = lambda r, c: pl.BlockSpec((r, c), lambda i: (0, 0))
    hfil, sums = pl.pallas_call(
        _hy_filter_kernel,
        grid=(L // tm,),
        in_specs=[pl.BlockSpec((tm, HY_FEAT_PAD), lambda i: (i, 0)),
                  const(HY_FEAT_PAD, HY_FEAT_PAD), const(1, HY_FEAT_PAD), const(1, HY_FEAT_PAD),
                  const(HY_FEAT_PAD, HY_FEAT_PAD), const(1, HY_FEAT_PAD),
                  const(HY_FEAT_PAD, 2 * HY_C), const(1, 2 * HY_C), const(1, 2 * HY_C)],
        out_specs=[pl.BlockSpec((tm, 2 * HY_C), lambda i: (i, 0)), const(8, 2 * HY_C)],
        out_shape=[jax.ShapeDtypeStruct((L, 2 * HY_C), jnp.float32), jax.ShapeDtypeStruct((8, 2 * HY_C), jnp.float32)],
        compiler_params=_cparams("arbitrary"),
        name="hyena_filters",
    )(feats, padm(w1, HY_FEAT_PAD, HY_FEAT_PAD), padv(b1), padv(freq), padm(w2, HY_FEAT_PAD, HY_FEAT_PAD), padv(b2),
      padm(w3, HY_FEAT_PAD, 2 * HY_C), b3.reshape(1, 2 * HY_C), jnp.tile(deltas, 2).reshape(1, 2 * HY_C))
    assert fh <= HY_FEAT_PAD
    inorm = 1.0 / jnp.sum(sums, axis=0)
    return hfil, inorm.reshape(2, HY_C)


def _hy_pre_kernel(x0_ref, x1_ref, v_ref, x0p_ref, x1p_ref, vp_ref, x0n_ref, x1n_ref, vn_ref, w_ref, b_ref,
                   vo_ref, x0o_ref):
    i = pl.program_id(0)
    first = i == 0
    last = i == pl.num_programs(0) - 1
    tm = x0_ref.shape[0]
    row = lax.broadcasted_iota(jnp.int32, x0_ref.shape, 0)

    def conv(x_ref, p_ref, n_ref, g):
        x = x_ref[...]
        cs = slice(g * HY_C, (g + 1) * HY_C)
        prev_row = jnp.where(first, 0.0, p_ref[7:8, :])
        next_row = jnp.where(last, 0.0, n_ref[0:1, :])
        xp = jnp.where(row == 0, prev_row, pltpu.roll(x, 1, 0))
        xn = jnp.where(row == tm - 1, next_row, pltpu.roll(x, tm - 1, 0))
        return w_ref[0:1, cs] * xp + w_ref[1:2, cs] * x + w_ref[2:3, cs] * xn + b_ref[0:1, cs]

    x0 = conv(x0_ref, x0p_ref, x0n_ref, 0)
    x1 = conv(x1_ref, x1p_ref, x1n_ref, 1)
    v = conv(v_ref, vp_ref, vn_ref, 2)
    vo_ref[...] = (v * x1).astype(vo_ref.dtype)
    x0o_ref[...] = x0.astype(x0o_ref.dtype)


def _hy_pre(z, L, conv_w, conv_b):
    tm = min(256, L)
    nb8 = L // 8
    cb = HY_OFF // HY_C
    main = lambda g: pl.BlockSpec((tm, HY_C), lambda i: (i, cb + g))
    prev = lambda g: pl.BlockSpec((8, HY_C), lambda i: (jnp.maximum(i * (tm // 8) - 1, 0), cb + g))
    nxt = lambda g: pl.BlockSpec((8, HY_C), lambda i: (jnp.minimum((i + 1) * (tm // 8), nb8 - 1), cb + g))
    const = lambda a: pl.BlockSpec(a.shape, lambda i: (0, 0))
    cbias = conv_b.reshape(1, HY_COLS)
    return pl.pallas_call(
        _hy_pre_kernel,
        grid=(L // tm,),
        in_specs=[main(0), main(1), main(2), prev(0), prev(1), prev(2), nxt(0), nxt(1), nxt(2),
                  const(conv_w), const(cbias)],
        out_specs=[pl.BlockSpec((tm, HY_C), lambda i: (i, 0))] * 2,
        out_shape=[jax.ShapeDtypeStruct((L, HY_C), jnp.float32), jax.ShapeDtypeStruct((L, HY_C), jnp.bfloat16)],
        compiler_params=_cparams("parallel"),
        name="hyena_short_conv",
    )(z, z, z, z, z, z, z, z, z, conv_w, cbias)


HY_N1 = 128
HY_TWO_STAGE_MIN_L = 1024


def _dft_outer_table(n1, cols):
    ang = -2.0 * np.pi * np.outer(np.arange(n1 // 2) + 0.5, np.arange(cols)) / n1
    return jnp.asarray(np.concatenate([np.cos(ang), np.sin(ang)], axis=0), jnp.bfloat16)


def _dft_inner_table(n1, n2):
    j2 = np.arange(n2)
    f_ang = -2.0 * np.pi * np.outer(np.arange(n2), j2) / n2
    tw_ang = -2.0 * np.pi * np.outer(np.arange(n1 // 2) + 0.5, j2) / (n1 * n2)
    fr, fi = jnp.asarray(np.cos(f_ang), jnp.float32), jnp.asarray(np.sin(f_ang), jnp.float32)
    twr, twi = jnp.asarray(np.cos(tw_ang), jnp.float32), jnp.asarray(np.sin(tw_ang), jnp.float32)
    mr = fr[None] * twr[:, None, :] - fi[None] * twi[:, None, :]
    mi = fr[None] * twi[:, None, :] + fi[None] * twr[:, None, :]
    return _bf(jnp.concatenate([jnp.concatenate([mr, -mi], axis=2), jnp.concatenate([mi, mr], axis=2)], axis=1))


def _spectral_product(xv, xh, inorm, half):
    inf, inb = inorm[0:1, :], inorm[1:2, :]
    gr = xh[:half, :HY_C] * inf + xh[:half, HY_C:] * inb
    gi = xh[half:, :HY_C] * inf - xh[half:, HY_C:] * inb
    xr, xi = xv[:half], xv[half:]
    return jnp.concatenate([xr * gr - xi * gi, xr * gi + xi * gr], axis=0).astype(jnp.bfloat16)


def _hy_spec_kernel(r_ref, avr_ref, avi_ref, ahr_ref, ahi_ref, inorm_ref, br_ref, bi_ref):
    r = r_ref[0]
    n2 = avr_ref.shape[1]
    xv = jnp.dot(r, jnp.concatenate([avr_ref[0], avi_ref[0]], axis=0), preferred_element_type=jnp.float32)
    xh = jnp.dot(r, jnp.concatenate([ahr_ref[0], ahi_ref[0]], axis=0), preferred_element_type=jnp.float32)
    b = _dot_tn(r, _spectral_product(xv, xh, inorm_ref[...], n2))
    br_ref[0] = b[:n2].astype(br_ref.dtype)
    bi_ref[0] = b[n2:].astype(bi_ref.dtype)


def _hy_spec(r, av, ah, inorm, n1, n2):
    av3 = av.reshape(n1, n2, HY_C)
    ah3 = ah.reshape(n1, n2, 2 * HY_C)
    h1 = n1 // 2
    out = jax.ShapeDtypeStruct((h1, n2, HY_C), jnp.bfloat16)
    return pl.pallas_call(
        _hy_spec_kernel,
        grid=(h1,),
        in_specs=[pl.BlockSpec((1, 2 * n2, 2 * n2), lambda k: (k, 0, 0)),
                  pl.BlockSpec((1, n2, HY_C), lambda k: (k, 0, 0)),
                  pl.BlockSpec((1, n2, HY_C), lambda k: (k + h1, 0, 0)),
                  pl.BlockSpec((1, n2, 2 * HY_C), lambda k: (k, 0, 0)),
                  pl.BlockSpec((1, n2, 2 * HY_C), lambda k: (k + h1, 0, 0)),
                  pl.BlockSpec((2, HY_C), lambda k: (0, 0))],
        out_specs=[pl.BlockSpec((1, n2, HY_C), lambda k: (k, 0, 0))] * 2,
        out_shape=[out, out],
        compiler_params=_cparams("parallel"),
        name="hyena_spectral",
    )(r, av3, av3, ah3, ah3, inorm)


def _hy_spec_direct_kernel(xv_ref, xh_ref, inorm_ref, yr_ref, yi_ref):
    half = yr_ref.shape[0]
    y = _spectral_product(xv_ref[...].astype(jnp.float32), xh_ref[...].astype(jnp.float32), inorm_ref[...], half)
    yr_ref[...] = y[:half]
    yi_ref[...] = y[half:]


def _hy_spec_direct(xv, xh, inorm, L):
    full = lambda a: pl.BlockSpec(a.shape, lambda i: (0, 0))
    out = jax.ShapeDtypeStruct((L, HY_C), jnp.bfloat16)
    return pl.pallas_call(
        _hy_spec_direct_kernel,
        grid=(1,),
        in_specs=[full(xv), full(xh), full(inorm)],
        out_specs=[pl.BlockSpec((L, HY_C), lambda i: (0, 0))] * 2,
        out_shape=[out, out],
        compiler_params=_cparams("arbitrary"),
        name="hyena_spectral_direct",
    )(xv, xh, inorm)


def _hy_post_kernel(tr_ref, ti_ref, br_ref, bi_ref, v_ref, x0_ref, skip_ref, o_ref, *, scale):
    acc = (jnp.dot(tr_ref[...], br_ref[...], preferred_element_type=jnp.float32)
           + jnp.dot(ti_ref[...], bi_ref[...], preferred_element_type=jnp.float32))
    y = (acc * scale + v_ref[...].astype(jnp.float32) * skip_ref[...]) * x0_ref[...].astype(jnp.float32)
    o_ref[...] = y.astype(o_ref.dtype)


def _hy_post(t_fwd, b_r, b_i, v, x0, skip, L, h1, n2):
    ncol = n2 * HY_C
    tn = min(4096, ncol)
    tr_t = t_fwd[:h1].T
    ti_t = t_fwd[h1:].T
    skip_t = jnp.tile(skip, tn // HY_C).reshape(1, tn)
    col = lambda rows: pl.BlockSpec((rows, tn), lambda j: (0, j))
    rows_out = tr_t.shape[0]
    y = pl.pallas_call(
        functools.partial(_hy_post_kernel, scale=1.0 / L),
        grid=(ncol // tn,),
        in_specs=[pl.BlockSpec(tr_t.shape, lambda j: (0, 0)), pl.BlockSpec(ti_t.shape, lambda j: (0, 0)),
                  col(h1), col(h1), col(rows_out), col(rows_out), pl.BlockSpec((1, tn), lambda j: (0, 0))],
        out_specs=col(rows_out),
        out_shape=jax.ShapeDtypeStruct((rows_out, ncol), jnp.bfloat16),
        compiler_params=_cparams("parallel"),
        name="hyena_inverse",
    )(tr_t, ti_t, b_r.reshape(h1, ncol), b_i.reshape(h1, ncol), v.reshape(rows_out, ncol),
      x0.reshape(rows_out, ncol), skip_t)
    return y.reshape(L, HY_C)


SUBLANE = 8
HY_COL_TILE = 512


def _dft_outer_kron(n1):
    h1 = n1 // 2
    ang = -2.0 * np.pi * np.outer(np.arange(h1) + 0.5, np.arange(h1)) / n1
    eye = np.eye(SUBLANE)
    t_r, t_i = np.cos(ang), np.sin(ang)
    fwd = np.kron(np.concatenate([t_r, t_i], axis=0), eye)
    inv = np.concatenate([np.kron(t_r.T, eye), np.kron(t_i.T, eye)], axis=1)
    return jnp.asarray(fwd, jnp.bfloat16), jnp.asarray(inv, jnp.bfloat16)


def _hy_outer_fwd_kernel(t_ref, x_ref, o_ref):
    x = x_ref[...]
    rows_in, rows_out = x.shape[0], t_ref.shape[0] // SUBLANE
    cw = x.shape[2]
    parts = []
    for s in range(0, x.shape[1], SUBLANE):
        xs = x[:, s:s + SUBLANE, :].reshape(rows_in * SUBLANE, cw).astype(jnp.bfloat16)
        r = jnp.dot(t_ref[...], xs, preferred_element_type=jnp.float32)
        parts.append(r.reshape(rows_out, SUBLANE, cw))
    o_ref[...] = jnp.concatenate(parts, axis=1).astype(o_ref.dtype)


def _hy_outer_fwd(t_kron, x3):
    h1, n2, w = x3.shape
    n1 = 2 * h1
    blk = 2 * SUBLANE
    return pl.pallas_call(
        _hy_outer_fwd_kernel,
        grid=(n2 // blk, w // HY_COL_TILE),
        in_specs=[pl.BlockSpec(t_kron.shape, lambda j, cc: (0, 0)),
                  pl.BlockSpec((h1, blk, HY_COL_TILE), lambda j, cc: (0, j, cc))],
        out_specs=pl.BlockSpec((n1, blk, HY_COL_TILE), lambda j, cc: (0, j, cc)),
        out_shape=jax.ShapeDtypeStruct((n1, n2, w), jnp.bfloat16),
        compiler_params=_cparams("parallel", "parallel"),
        name="hyena_outer_dft",
    )(t_kron, x3)


def _hy_outer_inv_kernel(t_ref, br_ref, bi_ref, v_ref, x0_ref, skip_ref, o_ref, *, scale):
    br = br_ref[...].astype(jnp.float32)
    bi = bi_ref[...].astype(jnp.float32)
    v = v_ref[...]
    x0 = x0_ref[...].astype(jnp.float32)
    h1, _, cw = br.shape
    parts = []
    for s in range(0, br.shape[1], SUBLANE):
        sl = slice(s, s + SUBLANE)
        b = jnp.concatenate([br[:, sl, :].reshape(h1 * SUBLANE, cw), bi[:, sl, :].reshape(h1 * SUBLANE, cw)], axis=0)
        r = jnp.dot(t_ref[...], b.astype(jnp.bfloat16), preferred_element_type=jnp.float32)
        parts.append((r.reshape(h1, SUBLANE, cw) * scale + v[:, sl, :] * skip_ref[...]) * x0[:, sl, :])
    o_ref[...] = jnp.concatenate(parts, axis=1).astype(o_ref.dtype)


def _hy_outer_inv(t_kron_inv, b_r, b_i, v3, x03, skip, L):
    h1, n2, w = b_r.shape
    blk = 2 * SUBLANE
    tile = pl.BlockSpec((h1, blk, HY_COL_TILE), lambda j, cc: (0, j, cc))
    return pl.pallas_call(
        functools.partial(_hy_outer_inv_kernel, scale=1.0 / L),
        grid=(n2 // blk, w // HY_COL_TILE),
        in_specs=[pl.BlockSpec(t_kron_inv.shape, lambda j, cc: (0, 0)), tile, tile, tile, tile,
                  pl.BlockSpec((1, 1, HY_COL_TILE), lambda j, cc: (0, 0, cc))],
        out_specs=tile,
        out_shape=jax.ShapeDtypeStruct((h1, n2, w), jnp.bfloat16),
        compiler_params=_cparams("parallel", "parallel"),
        name="hyena_outer_idft",
    )(t_kron_inv, b_r, b_i, v3, x03, skip.reshape(1, 1, w))


def _hyena(z, L, conv_w, conv_b, fparams, skip):
    v, x0 = _hy_pre(z, L, conv_w, conv_b)
    hfil, inorm = _hyena_filters(L, *fparams)
    if L >= HY_TWO_STAGE_MIN_L:
        n1 = HY_N1
        n2 = 2 * L // n1
        h1 = n1 // 2
        assert n2 % (2 * SUBLANE) == 0
        t_kron, t_kron_inv = _dft_outer_kron(n1)
        v3 = v.reshape(h1, n2, HY_C)
        av = _hy_outer_fwd(t_kron, v3)
        ah = _hy_outer_fwd(t_kron, hfil.reshape(h1, n2, 2 * HY_C))
        b_r, b_i = _hy_spec(_dft_inner_table(n1, n2), av, ah, inorm, n1, n2)
        return _hy_outer_inv(t_kron_inv, b_r, b_i, v3, x0.reshape(h1, n2, HY_C), skip, L).reshape(L, HY_C)
    t_fwd = _dft_outer_table(2 * L, L)
    xv = _matmul(t_fwd, _bf(v), tm=2 * L, tn=HY_C, out_dtype=jnp.bfloat16)
    xh = _matmul(t_fwd, _bf(hfil), tm=2 * L, tn=HY_C, out_dtype=jnp.bfloat16)
    y_r, y_i = _hy_spec_direct(xv, xh, inorm, L)
    return _hy_post(t_fwd, y_r, y_i, v, x0, skip, L, L, 1)


def _merge_kernel(yh_ref, yg_ref, yl_ref, gate_h_ref, gate_g_ref, gate_l_ref, wb_ref, o_ref):
    acc = None
    for br, (y_ref, g_ref) in enumerate(((yh_ref, gate_h_ref), (yg_ref, gate_g_ref), (yl_ref, gate_l_ref))):
        t = jnp.dot(y_ref[...], wb_ref[br], preferred_element_type=jnp.float32) * jax.nn.sigmoid(g_ref[...])
        acc = t if acc is None else acc + t
    o_ref[...] = acc.astype(o_ref.dtype)


def _proj_residual_kernel(m_ref, w_ref, h_ref, gt_ref, o_ref):
    o_ref[...] = h_ref[...] + gt_ref[...] * jnp.dot(m_ref[...], w_ref[...], preferred_element_type=jnp.float32)


MERGE_ROWS = 512


def _merge(z, L, ys, w_branch, w_out, h, gt):
    tm = min(MERGE_ROWS, L)
    gb = MG_OFF // D_MODEL
    ybs = pl.BlockSpec((tm, HY_C), lambda i: (i, 0))
    gate = lambda br: pl.BlockSpec((tm, D_MODEL), lambda i: (i, gb + br))
    row = pl.BlockSpec((tm, D_MODEL), lambda i: (i, 0))
    merged = pl.pallas_call(
        _merge_kernel,
        grid=(L // tm,),
        in_specs=[ybs, ybs, ybs, gate(0), gate(1), gate(2),
                  pl.BlockSpec((N_BRANCH, HY_C, D_MODEL), lambda i: (0, 0, 0), pipeline_mode=pl.Buffered(1))],
        out_specs=row,
        out_shape=jax.ShapeDtypeStruct((L, D_MODEL), jnp.bfloat16),
        compiler_params=_cparams("parallel"),
        name="branch_merge",
    )(ys[0], ys[1], ys[2], z, z, z, _bf(w_branch))
    return pl.pallas_call(
        _proj_residual_kernel,
        grid=(L // tm,),
        in_specs=[row, pl.BlockSpec((D_MODEL, D_MODEL), lambda i: (0, 0), pipeline_mode=pl.Buffered(1)), row,
                  pl.BlockSpec((1, D_MODEL), lambda i: (0, 0))],
        out_specs=row,
        out_shape=jax.ShapeDtypeStruct((L, D_MODEL), jnp.float32),
        compiler_params=_cparams("parallel"),
        name="out_proj_residual",
    )(merged, _bf(w_out), h, gt)


def _pad_cols(a):
    pad = lambda n: jnp.zeros(a.shape[:-1] + (n,), a.dtype)
    return jnp.concatenate([a[..., :REC_COLS], pad(HY_OFF - REC_COLS), a[..., REC_COLS:]], axis=-1)


W_TILE = 1024
W_SHIFT = HY_OFF - REC_COLS
W_ROW_OFF = W_TILE - W_SHIFT
assert 0 < W_SHIFT <= W_TILE and W_ROW_OFF % 8 == 0 and HY_OFF % W_TILE == 0 and Z_COLS % W_TILE == 0


def _w_in_prep_kernel(a_ref, b_ref, o_ref):
    j = pl.program_id(0)
    shifted = j >= HY_OFF // W_TILE

    @pl.when(jnp.logical_not(shifted))
    def _():
        o_ref[...] = a_ref[...].T.astype(o_ref.dtype)

    @pl.when(shifted)
    def _():
        window = jnp.concatenate([a_ref[...], b_ref[...]], axis=0)
        o_ref[...] = window[W_ROW_OFF:W_ROW_OFF + W_TILE].T.astype(o_ref.dtype)


def _w_in_prep(w_in, layer):
    w_t = jnp.swapaxes(w_in, 1, 2)
    _, n, k = w_t.shape
    first_shifted = HY_OFF // W_TILE
    a_idx = lambda j: jnp.where(j < first_shifted, j, j - 1)
    tail_blocks = W_TILE // W_ROW_OFF
    assert W_TILE % W_ROW_OFF == 0 and n % W_ROW_OFF == 0
    return pl.pallas_call(
        _w_in_prep_kernel,
        grid=(Z_COLS // W_TILE,),
        in_specs=[pl.BlockSpec((None, W_TILE, k), lambda j: (layer, a_idx(j), 0)),
                  pl.BlockSpec((None, W_ROW_OFF, k), lambda j: (layer, (a_idx(j) + 1) * tail_blocks, 0))],
        out_specs=pl.BlockSpec((k, W_TILE), lambda j: (0, j)),
        out_shape=jax.ShapeDtypeStruct((k, Z_COLS), jnp.bfloat16),
        compiler_params=_cparams("parallel"),
        name="w_in_relayout",
    )(w_t, w_t)


def _mixer(h, hc, u, uc, gt, gtc, p, need_ctx):
    L, Lc = u.shape[0], uc.shape[0]
    w_in = _w_in_prep(p['w_in'], p['layer'])
    b_in = _pad_cols(p['b_in'])
    z = _matmul(u, w_in, b_in, tm=1024, tn=W_TILE)
    ncol = Z_COLS if need_ctx else HY_OFF
    zc = _matmul(uc, w_in, b_in[:ncol], tm=1024, tn=W_TILE, n_cols=ncol)
    zeros = lambda hd, dk, dv: jnp.zeros((hd, dv, dk), jnp.float32)
    yc_hg, hg_sf, hg_sb = _hgrn2(zc, Lc, p['lb'], p['hg_norm_w'],
                                 zeros(HG_HEADS, HG_DK, HG_DV), zeros(HG_HEADS, HG_DK, HG_DV))
    yc_gl, gl_sf, gl_sb = _gla(zc, Lc, p['gl_w_a2'], p['gl_b_a'], p['gl_norm_w'],
                               zeros(GL_HEADS, GL_DK, GL_DV), zeros(GL_HEADS, GL_DK, GL_DV))
    y_hg, _, _ = _hgrn2(z, L, p['lb'], p['hg_norm_w'], hg_sf, hg_sb)
    y_gl, _, _ = _gla(z, L, p['gl_w_a2'], p['gl_b_a'], p['gl_norm_w'], gl_sf, gl_sb)
    hy = (p['hy_conv_w'], p['hy_conv_b'], p['hy_f'], p['hy_skip'])
    y_hy = _hyena(z, L, *hy)
    h = _merge(z, L, (y_hy, y_hg, y_gl), p['w_branch'], p['w_out'], h, gt)
    if need_ctx:
        yc_hy = _hyena(zc, Lc, *hy)
        hc = _merge(zc, Lc, (yc_hy, yc_hg, yc_gl), p['w_branch'], p['w_out'], hc, gtc)
    return h, hc


GATHER_UNROLL = 8
GATHER_AHEAD = 2
GATHER_SLOTS = GATHER_AHEAD + 1


def _ffn_kernel(blk_exp_ref, n_used_ref, tok_ref, x_hbm, wg_ref, wu_ref, wd_ref, o_ref, xbuf, sem, wg_s, wu_s, wd_s):
    i = pl.program_id(0)
    n_used = n_used_ref[0]
    rows = o_ref.shape[0]

    def issue(step, slot):
        def body(r, carry):
            src = tok_ref[step * rows + r]
            pltpu.make_async_copy(x_hbm.at[pl.ds(src, 1)], xbuf.at[slot, pl.ds(r, 1)], sem.at[slot]).start()
            return carry

        lax.fori_loop(0, rows, body, 0, unroll=GATHER_UNROLL)

    for s in range(GATHER_AHEAD):
        @pl.when(jnp.logical_and(i == 0, s < n_used))
        def _(s=s):
            issue(s, s % GATHER_SLOTS)

    @pl.when(i + GATHER_AHEAD < n_used)
    def _():
        issue(i + GATHER_AHEAD, (i + GATHER_AHEAD) % GATHER_SLOTS)

    new_expert = jnp.logical_or(i == 0, blk_exp_ref[i] != blk_exp_ref[jnp.maximum(i - 1, 0)])

    @pl.when(jnp.logical_and(i < n_used, new_expert))
    def _():
        wg_s[...] = wg_ref[0].astype(jnp.bfloat16)
        wu_s[...] = wu_ref[0].astype(jnp.bfloat16)
        wd_s[...] = wd_ref[0].astype(jnp.bfloat16)

    @pl.when(i < n_used)
    def _():
        slot = i % GATHER_SLOTS
        pltpu.make_async_copy(xbuf.at[slot], xbuf.at[slot], sem.at[slot]).wait()
        x = xbuf[slot].astype(jnp.bfloat16)
        hg = jnp.dot(x, wg_s[...], preferred_element_type=jnp.float32)
        hu = jnp.dot(x, wu_s[...], preferred_element_type=jnp.float32)
        act = (hg * jax.nn.sigmoid(hg) * hu).astype(jnp.bfloat16)
        o_ref[...] = jnp.dot(act, wd_s[...], preferred_element_type=jnp.float32)

    @pl.when(i >= n_used)
    def _():
        o_ref[...] = jnp.zeros_like(o_ref)


def _grouped_ffn(x, buf_tok, blk_exp, n_used, layer, w_gate, w_up, w_down):
    p_len = buf_tok.shape[0]
    d = x.shape[1]
    n_blk = p_len // MOE_BLOCK
    grid_spec = pltpu.PrefetchScalarGridSpec(
        num_scalar_prefetch=3,
        grid=(n_blk,),
        in_specs=[pl.BlockSpec(memory_space=pl.ANY),
                  pl.BlockSpec((None, 1, d, D_FF), lambda i, be, nu, tk: (layer, be[i], 0, 0)),
                  pl.BlockSpec((None, 1, d, D_FF), lambda i, be, nu, tk: (layer, be[i], 0, 0)),
                  pl.BlockSpec((None, 1, D_FF, d), lambda i, be, nu, tk: (layer, be[i], 0, 0))],
        out_specs=pl.BlockSpec((MOE_BLOCK, d), lambda i, be, nu, tk: (i, 0)),
        scratch_shapes=[pltpu.VMEM((GATHER_SLOTS, MOE_BLOCK, d), jnp.float32), pltpu.SemaphoreType.DMA((GATHER_SLOTS,)),
                        pltpu.VMEM((d, D_FF), jnp.bfloat16), pltpu.VMEM((d, D_FF), jnp.bfloat16),
                        pltpu.VMEM((D_FF, d), jnp.bfloat16)],
    )
    return pl.pallas_call(
        _ffn_kernel,
        grid_spec=grid_spec,
        out_shape=jax.ShapeDtypeStruct((p_len, d), jnp.float32),
        compiler_params=_cparams("arbitrary"),
        name="moe_grouped_ffn",
    )(blk_exp, n_used, buf_tok, x, w_gate, w_up, w_down)


ROUTER_COLS = 128


ROUTE_TOKENS = 256
ROUTE_OUT = (0, 1, 2, 3, 4, 5)


def _route_kernel(lg_ref, tril_ref, o_ref, cnt_ref, carry):
    i = pl.program_id(0)

    @pl.when(i == 0)
    def _():
        carry[...] = jnp.zeros_like(carry)

    x = lg_ref[...]
    lane = lax.broadcasted_iota(jnp.int32, x.shape, 1).astype(jnp.float32)
    neg = jnp.float32(-jnp.inf)
    far = jnp.float32(ROUTER_COLS)
    red_max = lambda a: jnp.max(a, axis=1, keepdims=True)
    red_min = lambda a: jnp.min(a, axis=1, keepdims=True)
    red_sum = lambda a: jnp.sum(a, axis=1, keepdims=True)
    gmask = lane < N_GROUPS
    gl = jnp.where(gmask, x, neg)
    gmax = red_max(gl)
    p_top = 1.0 / red_sum(jnp.where(gmask, jnp.exp(x - gmax), 0.0))
    grp = red_min(jnp.where(gl == gmax, lane, far))
    lo = N_GROUPS + EXP_PER_GROUP * grp
    emask = jnp.logical_and(lane >= lo, lane < lo + EXP_PER_GROUP)
    el = jnp.where(emask, x, neg)
    ee = jnp.where(emask, jnp.exp(x - red_max(el)), 0.0)
    prob = ee / red_sum(ee)
    p1 = red_max(prob)
    i1 = red_min(jnp.where(jnp.logical_and(emask, prob == p1), lane, far))
    rest = jnp.where(jnp.logical_and(emask, lane != i1), prob, -1.0)
    p2 = red_max(rest)
    i2 = red_min(jnp.where(rest == p2, lane, far))
    w1 = p_top * p1 / (p1 + p2)
    w2 = p_top * p2 / (p1 + p2)
    pick1 = (lane == i1).astype(jnp.float32)
    pick2 = (lane == i2).astype(jnp.float32)
    picks = pick1 + pick2
    before = jnp.dot(tril_ref[...], picks.astype(jnp.bfloat16), preferred_element_type=jnp.float32) + carry[...]
    r1 = red_sum(pick1 * before)
    r2 = red_sum(pick2 * before)
    carry[...] += jnp.sum(picks, axis=0, keepdims=True)
    cnt_ref[...] = carry[...]
    out = jnp.zeros_like(x)
    for col, val in zip(ROUTE_OUT, (i1 - N_GROUPS, i2 - N_GROUPS, w1, w2, r1, r2)):
        out = jnp.where(lane == col, val, out)
    o_ref[...] = out


def _route(logits):
    assert TOP_K == 2
    n = logits.shape[0]
    t = min(ROUTE_TOKENS, n)
    tril = jnp.asarray(np.tril(np.ones((t, t), np.float32), -1), jnp.bfloat16)
    out, cnt = pl.pallas_call(
        _route_kernel,
        grid=(n // t,),
        in_specs=[pl.BlockSpec((t, ROUTER_COLS), lambda i: (i, 0)), pl.BlockSpec((t, t), lambda i: (0, 0))],
        out_specs=[pl.BlockSpec((t, ROUTER_COLS), lambda i: (i, 0)), pl.BlockSpec((1, ROUTER_COLS), lambda i: (0, 0))],
        out_shape=[jax.ShapeDtypeStruct((n, ROUTER_COLS), jnp.float32),
                   jax.ShapeDtypeStruct((1, ROUTER_COLS), jnp.float32)],
        scratch_shapes=[pltpu.VMEM((1, ROUTER_COLS), jnp.float32)],
        compiler_params=_cparams("arbitrary"),
        name="moe_route",
    )(logits, tril)
    expert = out[:, 0:2].astype(jnp.int32)
    weight = out[:, 2:4]
    rank = out[:, 4:6].astype(jnp.int32)
    counts = cnt[0, N_GROUPS:N_GROUPS + N_EXPERTS].astype(jnp.int32)
    return expert, weight, rank, counts


def _hier_moe(h, norm_w, shift, scale, gt, p):
    n, d = h.shape
    pad = ROUTER_COLS - N_GROUPS - N_EXPERTS
    w_r = _bf(jnp.concatenate([p['w_rg'], p['w_re'], jnp.zeros((d, pad), jnp.float32)], axis=1))
    b_r = jnp.concatenate([p['b_rg'], p['b_re'], jnp.zeros((pad,), jnp.float32)]).reshape(1, ROUTER_COLS)
    xb, logits = _norm_mod(h, norm_w, shift, scale, router=(w_r, b_r), out_dtype=jnp.float32)
    expert, weight, rank, counts = _route(logits)
    a = n * TOP_K
    padded = (counts + MOE_BLOCK - 1) // MOE_BLOCK * MOE_BLOCK
    pad_end = jnp.cumsum(padded)
    pad_off = pad_end - padded
    pos = (pad_off[expert] + rank).reshape(a)
    p_len = (a + N_EXPERTS * MOE_BLOCK + MOE_BLOCK - 1) // MOE_BLOCK * MOE_BLOCK
    n_blk = p_len // MOE_BLOCK
    tok_flat = jnp.arange(a, dtype=jnp.int32) // TOP_K
    buf_tok = (jnp.arange(p_len, dtype=jnp.int32) % n).at[pos].set(tok_flat)
    blk_start = jnp.arange(n_blk, dtype=jnp.int32) * MOE_BLOCK
    blk_exp = jnp.minimum(jnp.sum(pad_end[None, :] <= blk_start[:, None], axis=1), N_EXPERTS - 1).astype(jnp.int32)
    n_used = (pad_end[-1:] // MOE_BLOCK).astype(jnp.int32)
    y = _grouped_ffn(xb, buf_tok, blk_exp, n_used, p['layer'], p['w_gate'], p['w_up'], p['w_down'])
    return _moe_combine(y, pos, weight, h, gt)


def _combine_kernel(pos_ref, y_hbm, wts_ref, h_ref, gt_ref, o_ref, buf, sem):
    i = pl.program_id(0)
    tokens = h_ref.shape[0]

    def row_copy(step, slot, r, k):
        src = pos_ref[(step * tokens + r) * TOP_K + k]
        return pltpu.make_async_copy(y_hbm.at[pl.ds(src, 1)], buf.at[slot, k, pl.ds(r, 1)], sem.at[slot])

    def issue(step, slot):
        def body(r, carry):
            for k in range(TOP_K):
                row_copy(step, slot, r, k).start()
            return carry

        lax.fori_loop(0, tokens, body, 0, unroll=GATHER_UNROLL // TOP_K)

    @pl.when(i == 0)
    def _():
        issue(0, 0)

    @pl.when(i + 1 < pl.num_programs(0))
    def _():
        issue(i + 1, (i + 1) % 2)

    slot = i % 2
    pltpu.make_async_copy(buf.at[slot], buf.at[slot], sem.at[slot]).wait()
    rows = buf[slot]
    wts = wts_ref[...]
    acc = rows[0] * wts[:, 0:1]
    for k in range(1, TOP_K):
        acc = acc + rows[k] * wts[:, k:k + 1]
    o_ref[...] = h_ref[...] + gt_ref[...] * acc


COMBINE_TOKENS = 128


def _moe_combine(y, pos, wts, h, gt):
    n, d = h.shape
    tokens = min(COMBINE_TOKENS, n)
    grid_spec = pltpu.PrefetchScalarGridSpec(
        num_scalar_prefetch=1,
        grid=(n // tokens,),
        in_specs=[pl.BlockSpec(memory_space=pl.ANY),
                  pl.BlockSpec((tokens, TOP_K), lambda i, pos: (i, 0)),
                  pl.BlockSpec((tokens, d), lambda i, pos: (i, 0)),
                  pl.BlockSpec((1, d), lambda i, pos: (0, 0))],
        out_specs=pl.BlockSpec((tokens, d), lambda i, pos: (i, 0)),
        scratch_shapes=[pltpu.VMEM((2, TOP_K, tokens, d), jnp.float32), pltpu.SemaphoreType.DMA((2,))],
    )
    return pl.pallas_call(
        _combine_kernel,
        grid_spec=grid_spec,
        out_shape=jax.ShapeDtypeStruct((n, d), jnp.float32),
        compiler_params=_cparams("arbitrary"),
        name="moe_combine",
    )(pos, y, wts, h, gt)


def _final_norm_kernel(x_ref, w_ref, o_ref):
    x = x_ref[...]
    o_ref[...] = x * lax.rsqrt(jnp.mean(x * x, axis=-1, keepdims=True) + NORM_EPS) * w_ref[...]


def _final_norm(x, w, tm=512):
    m, d = x.shape
    return pl.pallas_call(
        _final_norm_kernel,
        grid=(m // tm,),
        in_specs=[pl.BlockSpec((tm, d), lambda i: (i, 0)), pl.BlockSpec((1, d), lambda i: (0, 0))],
        out_specs=pl.BlockSpec((tm, d), lambda i: (i, 0)),
        out_shape=jax.ShapeDtypeStruct((m, d), jnp.float32),
        compiler_params=_cparams("parallel"),
        name="final_rmsnorm",
    )(x, w.reshape(1, d))


def kernel(x, c, ctx, c_ctx, w_mod, b_mod, norm_mix_w, norm_ffn_w, w_in, b_in, hy_conv_w, hy_conv_b, hy_f_w1, hy_f_b1, hy_f_w2, hy_f_b2, hy_f_w3, hy_f_b3, hy_f_freq, hy_skip, hg_lb_raw, hg_norm_w, gl_w_a2, gl_b_a, gl_norm_w, w_branch, w_out, w_rg, b_rg, w_re, b_re, w_gate, w_up, w_down, final_norm_w):
    assert x.shape[0] == 1
    depth = w_mod.shape[0]
    lb_all = jnp.cumsum(jax.nn.softmax(hg_lb_raw, axis=0), axis=0)
    lb_all = lb_all - lb_all[:1]
    h, hc = x[0], ctx[0]
    cc = jnp.concatenate([c, c_ctx[None, :]], axis=0)
    for l in range(depth):
        need_ctx = l < depth - 1
        mod = _matmul(_bf(jax.nn.silu(cc)), w_mod, b_mod[l], layer=l)
        sh1, sc1, gt1, sh2, sc2, gt2 = jnp.split(mod[0:1], 6, axis=-1)
        sh1c, sc1c, gt1c, sh2c, sc2c, gt2c = jnp.split(mod[1:2], 6, axis=-1)
        p = dict(w_in=w_in, b_in=b_in[l], hy_conv_w=hy_conv_w[l], hy_conv_b=hy_conv_b[l],
                 hy_f=(hy_f_w1[l], hy_f_b1[l], hy_f_w2[l], hy_f_b2[l], hy_f_w3[l], hy_f_b3[l], hy_f_freq[l]),
                 hy_skip=hy_skip[l], lb=lb_all[l], hg_norm_w=hg_norm_w[l], gl_w_a2=gl_w_a2[l], gl_b_a=gl_b_a[l],
                 gl_norm_w=gl_norm_w[l], w_branch=w_branch[l], w_out=w_out[l], w_rg=w_rg[l], b_rg=b_rg[l],
                 w_re=w_re[l], b_re=b_re[l], layer=l, w_gate=w_gate, w_up=w_up, w_down=w_down)
        u = _norm_mod(h, norm_mix_w[l], sh1, sc1)
        uc = _norm_mod(hc, norm_mix_w[l], sh1c, sc1c)
        h, hc = _mixer(h, hc, u, uc, gt1, gt1c, p, need_ctx)
        h = _hier_moe(h, norm_ffn_w[l], sh2, sc2, gt2, p)
        if need_ctx:
            hc = _hier_moe(hc, norm_ffn_w[l], sh2c, sc2c, gt2c, p)
    return _final_norm(h, final_norm_w)[None]
```

```python
import functools
import math

import jax
import jax.numpy as jnp
import numpy as np
from jax import lax
from jax.experimental import pallas as pl
from jax.experimental.pallas import tpu as pltpu

D_MODEL = 2048
NORM_EPS = 1e-6

HY_C = D_MODEL // 2
HY_EMB = 33
HY_BANDS = (HY_EMB - 1) // 2
HY_DECAY_TARGET = 1e-2
HY_FAST_PCT = 0.3
HY_SLOW_PCT = 1.5
HY_MOD_SHIFT = 0.05

HG_HEADS = 8
HG_DK = 128
HG_DV = 128
HG_K = HG_HEADS * HG_DK
HG_V = HG_HEADS * HG_DV

GL_HEADS = 4
GL_DK = 128
GL_DV = 256
GL_K = GL_HEADS * GL_DK
GL_V = GL_HEADS * GL_DV
GL_RANK = 16
GL_TAU = 16.0

N_BRANCH = 3
HG_COLS = 3 * HG_K + 2 * HG_V
GL_COLS = 2 * GL_K + 2 * GL_V + 2 * GL_RANK
REC_COLS = HG_COLS + GL_COLS
HY_COLS = 3 * HY_C
MERGE_COLS = N_BRANCH * D_MODEL

COL_TILE = 512
GL_Q_OFF = HG_COLS
GL_V_OFF = GL_Q_OFF + 2 * GL_K
GL_A_OFF = GL_V_OFF + 2 * GL_V
HY_OFF = -(-(GL_A_OFF + 2 * GL_RANK) // HY_C) * HY_C
MG_OFF = HY_OFF + HY_COLS
Z_COLS = MG_OFF + MERGE_COLS
assert GL_Q_OFF % GL_K == 0 and GL_V_OFF % GL_V == 0 and GL_A_OFF % 128 == 0 and Z_COLS % COL_TILE == 0
assert MG_OFF % D_MODEL == 0

N_GROUPS = 4
EXP_PER_GROUP = 8
N_EXPERTS = N_GROUPS * EXP_PER_GROUP
TOP_K = 2
D_FF = D_MODEL // 4
MOE_BLOCK = 256

SCAN_CHUNK = 128
SCAN_HEAD_GROUP = 8
SCAN_CHUNKS_PER_STEP = 2
LOG2_E = 1.4426950408889634

VMEM_LIMIT_BYTES = 56 * 1024 * 1024


def _cparams(*sem):
    return pltpu.CompilerParams(dimension_semantics=sem, vmem_limit_bytes=VMEM_LIMIT_BYTES)


def _bf(a):
    return a.astype(jnp.bfloat16)


def _mm_kernel(x_ref, w_ref, b_ref, o_ref):
    acc = jnp.dot(x_ref[...], w_ref[...].astype(jnp.bfloat16), preferred_element_type=jnp.float32) + b_ref[...]
    o_ref[...] = acc.astype(o_ref.dtype)


def _matmul(x, w, bias=None, tm=512, tn=COL_TILE, out_dtype=jnp.float32, n_cols=None, layer=None):
    m, k = x.shape
    n = w.shape[-1] if n_cols is None else n_cols
    assert n_cols is None or n_cols % tn == 0
    assert layer is None or n % tn == 0
    tm = min(tm, -(-m // 8) * 8)
    mp = -(-m // tm) * tm
    np_ = -(-n // tn) * tn
    if bias is None:
        bias = jnp.zeros((n,), jnp.float32)
    if mp != m:
        x = jnp.pad(x, ((0, mp - m), (0, 0)))
    if np_ != n:
        w = jnp.pad(w, ((0, 0), (0, np_ - n)))
        bias = jnp.pad(bias, (0, np_ - n))
    if layer is None:
        w_spec = pl.BlockSpec((k, tn), lambda i, j: (0, j))
    else:
        w_spec = pl.BlockSpec((None, k, tn), lambda i, j: (layer, 0, j))
    out = pl.pallas_call(
        _mm_kernel,
        grid=(mp // tm, np_ // tn),
        in_specs=[pl.BlockSpec((tm, k), lambda i, j: (i, 0)),
                  w_spec,
                  pl.BlockSpec((1, tn), lambda i, j: (0, j))],
        out_specs=pl.BlockSpec((tm, tn), lambda i, j: (i, j)),
        out_shape=jax.ShapeDtypeStruct((mp, np_), out_dtype),
        compiler_params=_cparams("parallel", "arbitrary"),
        name="dense_matmul",
    )(x, w, bias.reshape(1, np_))
    if mp != m or np_ != n:
        out = out[:m, :n]
    return out


def _norm_mod_kernel(*refs, with_router):
    if with_router:
        h_ref, w_ref, sh_ref, sc_ref, wr_ref, br_ref, o_ref, lg_ref = refs
    else:
        h_ref, w_ref, sh_ref, sc_ref, o_ref = refs
    x = h_ref[...]
    y = x * lax.rsqrt(jnp.mean(x * x, axis=-1, keepdims=True) + NORM_EPS) * w_ref[...]
    u = (y * (1.0 + sc_ref[...]) + sh_ref[...]).astype(jnp.bfloat16)
    o_ref[...] = u.astype(o_ref.dtype)
    if with_router:
        lg_ref[...] = jnp.dot(u, wr_ref[...], preferred_element_type=jnp.float32) + br_ref[...]


def _norm_mod(h, w, shift, scale, router=None, out_dtype=jnp.bfloat16):
    m, d = h.shape
    tm = min(512, m)
    row = pl.BlockSpec((tm, d), lambda i: (i, 0))
    vec = pl.BlockSpec((1, d), lambda i: (0, 0))
    args = [h, w.reshape(1, d), shift, scale]
    specs = [row, vec, vec, vec]
    out_shape = [jax.ShapeDtypeStruct((m, d), out_dtype)]
    out_specs = [row]
    if router is not None:
        args += list(router)
        specs += [pl.BlockSpec(router[0].shape, lambda i: (0, 0)), pl.BlockSpec(router[1].shape, lambda i: (0, 0))]
        out_shape.append(jax.ShapeDtypeStruct((m, router[0].shape[1]), jnp.float32))
        out_specs.append(pl.BlockSpec((tm, router[0].shape[1]), lambda i: (i, 0)))
    out = pl.pallas_call(
        functools.partial(_norm_mod_kernel, with_router=router is not None),
        grid=(m // tm,),
        in_specs=specs,
        out_specs=out_specs,
        out_shape=out_shape,
        compiler_params=_cparams("parallel"),
        name="norm_modulate",
    )(*args)
    return out if router is not None else out[0]


def _scan_masks(c, reverse):
    t = np.arange(c)
    ms = [np.eye(c, dtype=np.float32)]
    for lvl in range(int(math.log2(c))):
        upper = ((t >> lvl) & 1).astype(bool)
        same = (t[:, None] >> (lvl + 1)) == (t[None, :] >> (lvl + 1))
        m = same & upper[:, None] & (~upper)[None, :]
        ms.append((m.T if reverse else m).astype(np.float32))
    tri = t[None, :] >= t[:, None] if reverse else t[None, :] <= t[:, None]
    return jnp.asarray(np.stack(ms)), jnp.asarray(tri.astype(np.float32), dtype=jnp.bfloat16)


def _level_arg(cum, lvl, reverse):
    c = cum.shape[0]
    blk = 1 << lvl
    if blk >= 8:
        pieces = []
        for gs in range(0, c, 2 * blk):
            ref = cum[gs + blk:gs + blk + 1, :]
            pieces.append(ref - cum[gs:gs + blk, :])
            pieces.append(cum[gs + blk:gs + 2 * blk, :] - ref)
        arg = jnp.concatenate(pieces, axis=0)
    else:
        c3 = cum.reshape(c // 8, 8, cum.shape[1])
        sub = lax.broadcasted_iota(jnp.int32, c3.shape, 1)
        ref_row = ((sub >> lvl) | 1) << lvl
        ref = None
        for r in range(blk, 8, 2 * blk):
            cand = jnp.broadcast_to(c3[:, r:r + 1, :], c3.shape)
            ref = cand if ref is None else jnp.where(ref_row == r, cand, ref)
        upper = ((sub >> lvl) & 1) == 1
        arg = jnp.where(upper, c3 - ref, ref - c3).reshape(cum.shape)
    return -arg if reverse else arg


def _dot_nt(a, b):
    return lax.dot_general(a, b, (((1,), (1,)), ((), ())), preferred_element_type=jnp.float32)


def _dot_tn(a, b):
    return lax.dot_general(a, b, (((0,), (0,)), ((), ())), preferred_element_type=jnp.float32)


def _sigmoid_parts(z):
    e = jnp.exp(-jnp.abs(z))
    r = 1.0 / (1.0 + e)
    return jnp.minimum(z, 0.0) - jnp.log(1.0 + e), jnp.where(z >= 0.0, e * r, r)


def _scan_kernel(*refs, mode, reverse, final, heads, dk, dv, c, cps):
    it = iter(refs)
    q_ref = next(it)
    k_ref = next(it)
    v_ref = next(it)
    if mode == "hg":
        lbp_ref = next(it)
    else:
        a_ref = next(it)
        wa_ref = next(it)
        ba_ref = next(it)
    s0_ref = next(it)
    masks_ref = next(it)
    tri_ref = next(it)
    if final:
        oprev_ref = next(it)
        gate_ref = next(it)
        nw_ref = next(it)
    o_ref = next(it)
    st_ref = next(it)

    @pl.when(pl.program_id(0) == 0)
    def _():
        st_ref[...] = s0_ref[...]

    for ci in (reversed(range(cps)) if reverse else range(cps)):
        rs = slice(ci * c, (ci + 1) * c)
        if mode == "gl":
            la_all = jnp.dot(a_ref[rs, :].astype(jnp.bfloat16), wa_ref[...],
                             preferred_element_type=jnp.float32) + ba_ref[...]
        tri = tri_ref[...]
        tot_row = 0 if reverse else c - 1
        n_lvl = int(math.log2(c))
        ksl = lambda h: slice(h * dk, (h + 1) * dk)
        vsl = lambda h: slice(h * dv, (h + 1) * dv)
        for h0 in range(0, heads, SCAN_HEAD_GROUP):
            group = range(h0, min(h0 + SCAN_HEAD_GROUP, heads))
            q_, k_, cum_, qb_, kb_, o_, sc_ = {}, {}, {}, {}, {}, {}, {}
            for h in group:
                q = q_ref[rs, ksl(h)]
                if mode == "hg":
                    log_sig, sig_neg = _sigmoid_parts(k_ref[rs, ksl(h)])
                    la = lbp_ref[0:1, ksl(h)]
                    lbb = lbp_ref[1:2, ksl(h)] + log_sig
                    g = jnp.maximum(la, lbb) + jnp.log(1.0 + jnp.exp(-jnp.abs(la - lbb)))
                    k = lbp_ref[2:3, ksl(h)] * sig_neg
                    q = q * jax.nn.sigmoid(q)
                else:
                    g = _sigmoid_parts(la_all[:, ksl(h)])[0] * (1.0 / GL_TAU)
                    k = k_ref[rs, ksl(h)]
                    q = q * (dk ** -0.5)
                g = g * LOG2_E
                g1 = g.astype(jnp.bfloat16)
                r1 = g - g1.astype(jnp.float32)
                g2 = r1.astype(jnp.bfloat16)
                g3 = (r1 - g2.astype(jnp.float32)).astype(jnp.bfloat16)
                cum_[h] = (jnp.dot(tri, g1, preferred_element_type=jnp.float32)
                           + jnp.dot(tri, g2, preferred_element_type=jnp.float32)
                           + jnp.dot(tri, g3, preferred_element_type=jnp.float32))
                q_[h], k_[h] = q, k
            for h in group:
                cum = cum_[h]
                tot = cum[tot_row:tot_row + 1, :]
                st = st_ref[h]
                v = v_ref[rs, vsl(h)].astype(jnp.bfloat16)
                o_[h] = _dot_nt((q_[h] * jnp.exp2(cum)).astype(jnp.bfloat16), st.astype(jnp.bfloat16))
                kt = (k_[h] * jnp.exp2(tot - cum)).astype(jnp.bfloat16)
                st_ref[h] = st * jnp.exp2(tot) + _dot_tn(v, kt)
                qb_[h] = q_[h].astype(jnp.bfloat16)
                kb_[h] = k_[h].astype(jnp.bfloat16)
                sc_[h] = masks_ref[0] * _dot_nt(qb_[h], kb_[h])
            for lvl in range(n_lvl):
                for h in group:
                    e = jnp.exp2(_level_arg(cum_[h], lvl, reverse)).astype(jnp.bfloat16)
                    sc_[h] = sc_[h] + masks_ref[1 + lvl] * _dot_nt(qb_[h] * e, kb_[h] * e)
            for h in group:
                v = v_ref[rs, vsl(h)].astype(jnp.bfloat16)
                o = o_[h] + jnp.dot(sc_[h].astype(jnp.bfloat16), v, preferred_element_type=jnp.float32)
                if final:
                    o = o + oprev_ref[rs, vsl(h)]
                    y = o * lax.rsqrt(jnp.mean(o * o, axis=-1, keepdims=True) + NORM_EPS) * nw_ref[...]
                    gt = gate_ref[rs, vsl(h)]
                    act = jax.nn.sigmoid(gt) if mode == "hg" else gt * jax.nn.sigmoid(gt)
                    o_ref[rs, vsl(h)] = (y * act).astype(o_ref.dtype)
                else:
                    o_ref[rs, vsl(h)] = o


def _scan_pass(mode, reverse, final, L, srcs, s0, params, final_srcs=(), norm_w=None):
    heads, dk, dv = (HG_HEADS, HG_DK, HG_DV) if mode == "hg" else (GL_HEADS, GL_DK, GL_DV)
    c = min(SCAN_CHUNK, L)
    cps = min(SCAN_CHUNKS_PER_STEP, L // c)
    rows = c * cps
    nb = L // rows
    row = (lambda i: nb - 1 - i) if reverse else (lambda i: i)
    masks, tri = _scan_masks(c, reverse)

    def const(shape):
        return pl.BlockSpec(shape, lambda i: (0,) * len(shape))

    def rowblock(width, cb):
        return pl.BlockSpec((rows, width), lambda i: (row(i), cb))

    args = [a for a, _, _ in srcs] + list(params) + [s0, masks, tri]
    specs = ([rowblock(w, cb) for _, w, cb in srcs] + [const(p.shape) for p in params]
             + [const(s0.shape), const(masks.shape), const(tri.shape)])
    if final:
        args += [a for a, _, _ in final_srcs] + [norm_w]
        specs += [rowblock(w, cb) for _, w, cb in final_srcs] + [const(norm_w.shape)]
    return pl.pallas_call(
        functools.partial(_scan_kernel, mode=mode, reverse=reverse, final=final, heads=heads, dk=dk, dv=dv, c=c,
                          cps=cps),
        grid=(nb,),
        in_specs=specs,
        out_specs=[pl.BlockSpec((rows, heads * dv), lambda i: (row(i), 0)), const((heads, dv, dk))],
        out_shape=[jax.ShapeDtypeStruct((L, heads * dv), jnp.bfloat16 if final else jnp.float32),
                   jax.ShapeDtypeStruct((heads, dv, dk), jnp.float32)],
        compiler_params=_cparams("arbitrary"),
        name=f"scan_{mode}_{'bwd' if reverse else 'fwd'}",
    )(*args)


def _hgrn2(z, L, lb, norm_w, s0_f, s0_b):
    lbp = lambda d: jnp.stack([jnp.log(lb[d]), jnp.log1p(-lb[d]), 1.0 - lb[d]])
    w = HG_K
    o_b, s_b = _scan_pass("hg", True, False, L, [(z, w, 0), (z, w, 2), (z, w, 3)], s0_b, [lbp(1)])
    y, s_f = _scan_pass("hg", False, True, L, [(z, w, 0), (z, w, 1), (z, w, 3)], s0_f, [lbp(0)],
                        final_srcs=[(o_b, HG_V, 0), (z, HG_V, 4)], norm_w=norm_w.reshape(1, HG_DV))
    return y, s_f, s_b


def _gla(z, L, w_a2, b_a, norm_w, s0_f, s0_b):
    def gate_params(d):
        wa = jnp.zeros((128, GL_K), jnp.float32).at[d * GL_RANK:(d + 1) * GL_RANK].set(w_a2[d])
        return [_bf(wa), b_a[d].reshape(1, GL_K)]

    srcs = [(z, GL_K, GL_Q_OFF // GL_K), (z, GL_K, GL_Q_OFF // GL_K + 1), (z, GL_V, GL_V_OFF // GL_V),
            (z, 128, GL_A_OFF // 128)]
    o_b, s_b = _scan_pass("gl", True, False, L, srcs, s0_b, gate_params(1))
    y, s_f = _scan_pass("gl", False, True, L, srcs, s0_f, gate_params(0),
                        final_srcs=[(o_b, GL_V, 0), (z, GL_V, GL_V_OFF // GL_V + 1)], norm_w=norm_w.reshape(1, GL_DV))
    return y, s_f, s_b


HY_FEAT_PAD = 128


def _hy_filter_kernel(f_ref, w1_ref, b1_ref, fq_ref, w2_ref, b2_ref, w3_ref, b3_ref, dl_ref, h_ref, s_ref):
    i = pl.program_id(0)
    f = f_ref[...]
    fq = fq_ref[...]
    a = jnp.sin(fq * (jnp.dot(f.astype(jnp.bfloat16), w1_ref[...], preferred_element_type=jnp.float32) + b1_ref[...]))
    a = jnp.sin(fq * (jnp.dot(a.astype(jnp.bfloat16), w2_ref[...], preferred_element_type=jnp.float32) + b2_ref[...]))
    hh = jnp.dot(a.astype(jnp.bfloat16), w3_ref[...], preferred_element_type=jnp.float32) + b3_ref[...]
    hh = hh * (jnp.exp(-f[:, 0:1] * dl_ref[...]) + HY_MOD_SHIFT)
    h_ref[...] = hh
    part = jnp.sum(jnp.abs(hh).reshape(hh.shape[0] // 8, 8, hh.shape[1]), axis=0)

    @pl.when(i == 0)
    def _():
        s_ref[...] = part

    @pl.when(i > 0)
    def _():
        s_ref[...] += part


def _hyena_filters(L, w1, b1, w2, b2, w3, b3, freq):
    t = jnp.linspace(0.0, 1.0, L, dtype=jnp.float32)[:, None]
    ang = 2.0 * math.pi * jnp.arange(L, dtype=jnp.float32)[:, None] / L
    bands = jnp.linspace(1e-4, HY_BANDS - 1, HY_BANDS, dtype=jnp.float32)[None, :]
    feats = jnp.concatenate([t, jnp.cos(bands * ang), -jnp.sin(bands * ang),
                             jnp.zeros((L, HY_FEAT_PAD - HY_EMB), jnp.float32)], axis=-1)
    deltas = jnp.abs(jnp.linspace(math.log(HY_DECAY_TARGET) / HY_SLOW_PCT, math.log(HY_DECAY_TARGET) / HY_FAST_PCT,
                                  HY_C, dtype=jnp.float32))
    fh = w1.shape[1]
    padm = lambda a, r, c: _bf(jnp.pad(a, ((0, r - a.shape[0]), (0, c - a.shape[1]))))
    padv = lambda a: jnp.pad(a, (0, HY_FEAT_PAD - a.shape[0])).reshape(1, HY_FEAT_PAD)
    tm = min(512, L)
    const = lambda r, c: pl.BlockSpec((r, c), lambda i: (0, 0))
    hfil, sums = pl.pallas_call(
        _hy_filter_kernel,
        grid=(L // tm,),
        in_specs=[pl.BlockSpec((tm, HY_FEAT_PAD), lambda i: (i, 0)),
                  const(HY_FEAT_PAD, HY_FEAT_PAD), const(1, HY_FEAT_PAD), const(1, HY_FEAT_PAD),
                  const(HY_FEAT_PAD, HY_FEAT_PAD), const(1, HY_FEAT_PAD),
                  const(HY_FEAT_PAD, 2 * HY_C), const(1, 2 * HY_C), const(1, 2 * HY_C)],
        out_specs=[pl.BlockSpec((tm, 2 * HY_C), lambda i: (i, 0)), const(8, 2 * HY_C)],
        out_shape=[jax.ShapeDtypeStruct((L, 2 * HY_C), jnp.float32), jax.ShapeDtypeStruct((8, 2 * HY_C), jnp.float32)],
        compiler_params=_cparams("arbitrary"),
        name="hyena_filters",
    )(feats, padm(w1, HY_FEAT_PAD, HY_FEAT_PAD), padv(b1), padv(freq), padm(w2, HY_FEAT_PAD, HY_FEAT_PAD), padv(b2),
      padm(w3, HY_FEAT_PAD, 2 * HY_C), b3.reshape(1, 2 * HY_C), jnp.tile(deltas, 2).reshape(1, 2 * HY_C))
    assert fh <= HY_FEAT_PAD
    inorm = 1.0 / jnp.sum(sums, axis=0)
    return hfil, inorm.reshape(2, HY_C)


def _hy_pre_kernel(x0_ref, x1_ref, v_ref, x0p_ref, x1p_ref, vp_ref, x0n_ref, x1n_ref, vn_ref, w_ref, b_ref,
                   vo_ref, x0o_ref):
    i = pl.program_id(0)
    first = i == 0
    last = i == pl.num_programs(0) - 1
    tm = x0_ref.shape[0]
    row = lax.broadcasted_iota(jnp.int32, x0_ref.shape, 0)

    def conv(x_ref, p_ref, n_ref, g):
        x = x_ref[...]
        cs = slice(g * HY_C, (g + 1) * HY_C)
        prev_row = jnp.where(first, 0.0, p_ref[7:8, :])
        next_row = jnp.where(last, 0.0, n_ref[0:1, :])
        xp = jnp.where(row == 0, prev_row, pltpu.roll(x, 1, 0))
        xn = jnp.where(row == tm - 1, next_row, pltpu.roll(x, tm - 1, 0))
        return w_ref[0:1, cs] * xp + w_ref[1:2, cs] * x + w_ref[2:3, cs] * xn + b_ref[0:1, cs]

    x0 = conv(x0_ref, x0p_ref, x0n_ref, 0)
    x1 = conv(x1_ref, x1p_ref, x1n_ref, 1)
    v = conv(v_ref, vp_ref, vn_ref, 2)
    vo_ref[...] = (v * x1).astype(vo_ref.dtype)
    x0o_ref[...] = x0.astype(x0o_ref.dtype)


def _hy_pre(z, L, conv_w, conv_b):
    tm = min(256, L)
    nb8 = L // 8
    cb = HY_OFF // HY_C
    main = lambda g: pl.BlockSpec((tm, HY_C), lambda i: (i, cb + g))
    prev = lambda g: pl.BlockSpec((8, HY_C), lambda i: (jnp.maximum(i * (tm // 8) - 1, 0), cb + g))
    nxt = lambda g: pl.BlockSpec((8, HY_C), lambda i: (jnp.minimum((i + 1) * (tm // 8), nb8 - 1), cb + g))
    const = lambda a: pl.BlockSpec(a.shape, lambda i: (0, 0))
    cbias = conv_b.reshape(1, HY_COLS)
    return pl.pallas_call(
        _hy_pre_kernel,
        grid=(L // tm,),
        in_specs=[main(0), main(1), main(2), prev(0), prev(1), prev(2), nxt(0), nxt(1), nxt(2),
                  const(conv_w), const(cbias)],
        out_specs=[pl.BlockSpec((tm, HY_C), lambda i: (i, 0))] * 2,
        out_shape=[jax.ShapeDtypeStruct((L, HY_C), jnp.float32), jax.ShapeDtypeStruct((L, HY_C), jnp.bfloat16)],
        compiler_params=_cparams("parallel"),
        name="hyena_short_conv",
    )(z, z, z, z, z, z, z, z, z, conv_w, cbias)


HY_N1 = 128
HY_TWO_STAGE_MIN_L = 1024


def _dft_outer_table(n1, cols):
    ang = -2.0 * np.pi * np.outer(np.arange(n1 // 2) + 0.5, np.arange(cols)) / n1
    return jnp.asarray(np.concatenate([np.cos(ang), np.sin(ang)], axis=0), jnp.bfloat16)


def _dft_inner_table(n1, n2):
    j2 = np.arange(n2)
    f_ang = -2.0 * np.pi * np.outer(np.arange(n2), j2) / n2
    tw_ang = -2.0 * np.pi * np.outer(np.arange(n1 // 2) + 0.5, j2) / (n1 * n2)
    fr, fi = jnp.asarray(np.cos(f_ang), jnp.float32), jnp.asarray(np.sin(f_ang), jnp.float32)
    twr, twi = jnp.asarray(np.cos(tw_ang), jnp.float32), jnp.asarray(np.sin(tw_ang), jnp.float32)
    mr = fr[None] * twr[:, None, :] - fi[None] * twi[:, None, :]
    mi = fr[None] * twi[:, None, :] + fi[None] * twr[:, None, :]
    return _bf(jnp.concatenate([jnp.concatenate([mr, -mi], axis=2), jnp.concatenate([mi, mr], axis=2)], axis=1))


def _spectral_product(xv, xh, inorm, half):
    inf, inb = inorm[0:1, :], inorm[1:2, :]
    gr = xh[:half, :HY_C] * inf + xh[:half, HY_C:] * inb
    gi = xh[half:, :HY_C] * inf - xh[half:, HY_C:] * inb
    xr, xi = xv[:half], xv[half:]
    return jnp.concatenate([xr * gr - xi * gi, xr * gi + xi * gr], axis=0).astype(jnp.bfloat16)


def _hy_spec_kernel(r_ref, avr_ref, avi_ref, ahr_ref, ahi_ref, inorm_ref, br_ref, bi_ref):
    r = r_ref[0]
    n2 = avr_ref.shape[1]
    xv = jnp.dot(r, jnp.concatenate([avr_ref[0], avi_ref[0]], axis=0), preferred_element_type=jnp.float32)
    xh = jnp.dot(r, jnp.concatenate([ahr_ref[0], ahi_ref[0]], axis=0), preferred_element_type=jnp.float32)
    b = _dot_tn(r, _spectral_product(xv, xh, inorm_ref[...], n2))
    br_ref[0] = b[:n2].astype(br_ref.dtype)
    bi_ref[0] = b[n2:].astype(bi_ref.dtype)


def _hy_spec(r, av, ah, inorm, n1, n2):
    av3 = av.reshape(n1, n2, HY_C)
    ah3 = ah.reshape(n1, n2, 2 * HY_C)
    h1 = n1 // 2
    out = jax.ShapeDtypeStruct((h1, n2, HY_C), jnp.bfloat16)
    return pl.pallas_call(
        _hy_spec_kernel,
        grid=(h1,),
        in_specs=[pl.BlockSpec((1, 2 * n2, 2 * n2), lambda k: (k, 0, 0)),
                  pl.BlockSpec((1, n2, HY_C), lambda k: (k, 0, 0)),
                  pl.BlockSpec((1, n2, HY_C), lambda k: (k + h1, 0, 0)),
                  pl.BlockSpec((1, n2, 2 * HY_C), lambda k: (k, 0, 0)),
                  pl.BlockSpec((1, n2, 2 * HY_C), lambda k: (k + h1, 0, 0)),
                  pl.BlockSpec((2, HY_C), lambda k: (0, 0))],
        out_specs=[pl.BlockSpec((1, n2, HY_C), lambda k: (k, 0, 0))] * 2,
        out_shape=[out, out],
        compiler_params=_cparams("parallel"),
        name="hyena_spectral",
    )(r, av3, av3, ah3, ah3, inorm)


def _hy_spec_direct_kernel(xv_ref, xh_ref, inorm_ref, yr_ref, yi_ref):
    half = yr_ref.shape[0]
    y = _spectral_product(xv_ref[...].astype(jnp.float32), xh_ref[...].astype(jnp.float32), inorm_ref[...], half)
    yr_ref[...] = y[:half]
    yi_ref[...] = y[half:]


def _hy_spec_direct(xv, xh, inorm, L):
    full = lambda a: pl.BlockSpec(a.shape, lambda i: (0, 0))
    out = jax.ShapeDtypeStruct((L, HY_C), jnp.bfloat16)
    return pl.pallas_call(
        _hy_spec_direct_kernel,
        grid=(1,),
        in_specs=[full(xv), full(xh), full(inorm)],
        out_specs=[pl.BlockSpec((L, HY_C), lambda i: (0, 0))] * 2,
        out_shape=[out, out],
        compiler_params=_cparams("arbitrary"),
        name="hyena_spectral_direct",
    )(xv, xh, inorm)


def _hy_post_kernel(tr_ref, ti_ref, br_ref, bi_ref, v_ref, x0_ref, skip_ref, o_ref, *, scale):
    acc = (jnp.dot(tr_ref[...], br_ref[...], preferred_element_type=jnp.float32)
           + jnp.dot(ti_ref[...], bi_ref[...], preferred_element_type=jnp.float32))
    y = (acc * scale + v_ref[...].astype(jnp.float32) * skip_ref[...]) * x0_ref[...].astype(jnp.float32)
    o_ref[...] = y.astype(o_ref.dtype)


def _hy_post(t_fwd, b_r, b_i, v, x0, skip, L, h1, n2):
    ncol = n2 * HY_C
    tn = min(4096, ncol)
    tr_t = t_fwd[:h1].T
    ti_t = t_fwd[h1:].T
    skip_t = jnp.tile(skip, tn // HY_C).reshape(1, tn)
    col = lambda rows: pl.BlockSpec((rows, tn), lambda j: (0, j))
    rows_out = tr_t.shape[0]
    y = pl.pallas_call(
        functools.partial(_hy_post_kernel, scale=1.0 / L),
        grid=(ncol // tn,),
        in_specs=[pl.BlockSpec(tr_t.shape, lambda j: (0, 0)), pl.BlockSpec(ti_t.shape, lambda j: (0, 0)),
                  col(h1), col(h1), col(rows_out), col(rows_out), pl.BlockSpec((1, tn), lambda j: (0, 0))],
        out_specs=col(rows_out),
        out_shape=jax.ShapeDtypeStruct((rows_out, ncol), jnp.bfloat16),
        compiler_params=_cparams("parallel"),
        name="hyena_inverse",
    )(tr_t, ti_t, b_r.reshape(h1, ncol), b_i.reshape(h1, ncol), v.reshape(rows_out, ncol),
      x0.reshape(rows_out, ncol), skip_t)
    return y.reshape(L, HY_C)


SUBLANE = 8
HY_COL_TILE = 512


def _dft_outer_kron(n1):
    h1 = n1 // 2
    ang = -2.0 * np.pi * np.outer(np.arange(h1) + 0.5, np.arange(h1)) / n1
    eye = np.eye(SUBLANE)
    t_r, t_i = np.cos(ang), np.sin(ang)
    fwd = np.kron(np.concatenate([t_r, t_i], axis=0), eye)
    inv = np.concatenate([np.kron(t_r.T, eye), np.kron(t_i.T, eye)], axis=1)
    return jnp.asarray(fwd, jnp.bfloat16), jnp.asarray(inv, jnp.bfloat16)


def _hy_outer_fwd_kernel(t_ref, x_ref, o_ref):
    x = x_ref[...]
    rows_in, rows_out = x.shape[0], t_ref.shape[0] // SUBLANE
    cw = x.shape[2]
    parts = []
    for s in range(0, x.shape[1], SUBLANE):
        xs = x[:, s:s + SUBLANE, :].reshape(rows_in * SUBLANE, cw).astype(jnp.bfloat16)
        r = jnp.dot(t_ref[...], xs, preferred_element_type=jnp.float32)
        parts.append(r.reshape(rows_out, SUBLANE, cw))
    o_ref[...] = jnp.concatenate(parts, axis=1).astype(o_ref.dtype)


def _hy_outer_fwd(t_kron, x3):
    h1, n2, w = x3.shape
    n1 = 2 * h1
    blk = 2 * SUBLANE
    return pl.pallas_call(
        _hy_outer_fwd_kernel,
        grid=(n2 // blk, w // HY_COL_TILE),
        in_specs=[pl.BlockSpec(t_kron.shape, lambda j, cc: (0, 0)),
                  pl.BlockSpec((h1, blk, HY_COL_TILE), lambda j, cc: (0, j, cc))],
        out_specs=pl.BlockSpec((n1, blk, HY_COL_TILE), lambda j, cc: (0, j, cc)),
        out_shape=jax.ShapeDtypeStruct((n1, n2, w), jnp.bfloat16),
        compiler_params=_cparams("parallel", "parallel"),
        name="hyena_outer_dft",
    )(t_kron, x3)


def _hy_outer_inv_kernel(t_ref, br_ref, bi_ref, v_ref, x0_ref, skip_ref, o_ref, *, scale):
    br = br_ref[...].astype(jnp.float32)
    bi = bi_ref[...].astype(jnp.float32)
    v = v_ref[...]
    x0 = x0_ref[...].astype(jnp.float32)
    h1, _, cw = br.shape
    parts = []
    for s in range(0, br.shape[1], SUBLANE):
        sl = slice(s, s + SUBLANE)
        b = jnp.concatenate([br[:, sl, :].reshape(h1 * SUBLANE, cw), bi[:, sl, :].reshape(h1 * SUBLANE, cw)], axis=0)
        r = jnp.dot(t_ref[...], b.astype(jnp.bfloat16), preferred_element_type=jnp.float32)
        parts.append((r.reshape(h1, SUBLANE, cw) * scale + v[:, sl, :] * skip_ref[...]) * x0[:, sl, :])
    o_ref[...] = jnp.concatenate(parts, axis=1).astype(o_ref.dtype)


def _hy_outer_inv(t_kron_inv, b_r, b_i, v3, x03, skip, L):
    h1, n2, w = b_r.shape
    blk = 2 * SUBLANE
    tile = pl.BlockSpec((h1, blk, HY_COL_TILE), lambda j, cc: (0, j, cc))
    return pl.pallas_call(
        functools.partial(_hy_outer_inv_kernel, scale=1.0 / L),
        grid=(n2 // blk, w // HY_COL_TILE),
        in_specs=[pl.BlockSpec(t_kron_inv.shape, lambda j, cc: (0, 0)), tile, tile, tile, tile,
                  pl.BlockSpec((1, 1, HY_COL_TILE), lambda j, cc: (0, 0, cc))],
        out_specs=tile,
        out_shape=jax.ShapeDtypeStruct((h1, n2, w), jnp.bfloat16),
        compiler_params=_cparams("parallel", "parallel"),
        name="hyena_outer_idft",
    )(t_kron_inv, b_r, b_i, v3, x03, skip.reshape(1, 1, w))


def _hyena(z, L, conv_w, conv_b, fparams, skip):
    v, x0 = _hy_pre(z, L, conv_w, conv_b)
    hfil, inorm = _hyena_filters(L, *fparams)
    if L >= HY_TWO_STAGE_MIN_L:
        n1 = HY_N1
        n2 = 2 * L // n1
        h1 = n1 // 2
        assert n2 % (2 * SUBLANE) == 0
        t_kron, t_kron_inv = _dft_outer_kron(n1)
        v3 = v.reshape(h1, n2, HY_C)
        av = _hy_outer_fwd(t_kron, v3)
        ah = _hy_outer_fwd(t_kron, hfil.reshape(h1, n2, 2 * HY_C))
        b_r, b_i = _hy_spec(_dft_inner_table(n1, n2), av, ah, inorm, n1, n2)
        return _hy_outer_inv(t_kron_inv, b_r, b_i, v3, x0.reshape(h1, n2, HY_C), skip, L).reshape(L, HY_C)
    t_fwd = _dft_outer_table(2 * L, L)
    xv = _matmul(t_fwd, _bf(v), tm=2 * L, tn=HY_C, out_dtype=jnp.bfloat16)
    xh = _matmul(t_fwd, _bf(hfil), tm=2 * L, tn=HY_C, out_dtype=jnp.bfloat16)
    y_r, y_i = _hy_spec_direct(xv, xh, inorm, L)
    return _hy_post(t_fwd, y_r, y_i, v, x0, skip, L, L, 1)


def _merge_kernel(yh_ref, yg_ref, yl_ref, gate_h_ref, gate_g_ref, gate_l_ref, wb_ref, o_ref):
    acc = None
    for br, (y_ref, g_ref) in enumerate(((yh_ref, gate_h_ref), (yg_ref, gate_g_ref), (yl_ref, gate_l_ref))):
        t = jnp.dot(y_ref[...], wb_ref[br], preferred_element_type=jnp.float32) * jax.nn.sigmoid(g_ref[...])
        acc = t if acc is None else acc + t
    o_ref[...] = acc.astype(o_ref.dtype)


def _proj_residual_kernel(m_ref, w_ref, h_ref, gt_ref, o_ref):
    o_ref[...] = h_ref[...] + gt_ref[...] * jnp.dot(m_ref[...], w_ref[...], preferred_element_type=jnp.float32)


MERGE_ROWS = 512


def _merge(z, L, ys, w_branch, w_out, h, gt):
    tm = min(MERGE_ROWS, L)
    gb = MG_OFF // D_MODEL
    ybs = pl.BlockSpec((tm, HY_C), lambda i: (i, 0))
    gate = lambda br: pl.BlockSpec((tm, D_MODEL), lambda i: (i, gb + br))
    row = pl.BlockSpec((tm, D_MODEL), lambda i: (i, 0))
    merged = pl.pallas_call(
        _merge_kernel,
        grid=(L // tm,),
        in_specs=[ybs, ybs, ybs, gate(0), gate(1), gate(2),
                  pl.BlockSpec((N_BRANCH, HY_C, D_MODEL), lambda i: (0, 0, 0), pipeline_mode=pl.Buffered(1))],
        out_specs=row,
        out_shape=jax.ShapeDtypeStruct((L, D_MODEL), jnp.bfloat16),
        compiler_params=_cparams("parallel"),
        name="branch_merge",
    )(ys[0], ys[1], ys[2], z, z, z, _bf(w_branch))
    return pl.pallas_call(
        _proj_residual_kernel,
        grid=(L // tm,),
        in_specs=[row, pl.BlockSpec((D_MODEL, D_MODEL), lambda i: (0, 0), pipeline_mode=pl.Buffered(1)), row,
                  pl.BlockSpec((1, D_MODEL), lambda i: (0, 0))],
        out_specs=row,
        out_shape=jax.ShapeDtypeStruct((L, D_MODEL), jnp.float32),
        compiler_params=_cparams("parallel"),
        name="out_proj_residual",
    )(merged, _bf(w_out), h, gt)


def _pad_cols(a):
    pad = lambda n: jnp.zeros(a.shape[:-1] + (n,), a.dtype)
    return jnp.concatenate([a[..., :REC_COLS], pad(HY_OFF - REC_COLS), a[..., REC_COLS:]], axis=-1)


W_TILE = 1024
W_SHIFT = HY_OFF - REC_COLS
W_ROW_OFF = W_TILE - W_SHIFT
assert 0 < W_SHIFT <= W_TILE and W_ROW_OFF % 8 == 0 and HY_OFF % W_TILE == 0 and Z_COLS % W_TILE == 0


def _w_in_prep_kernel(a_ref, b_ref, o_ref):
    j = pl.program_id(0)
    shifted = j >= HY_OFF // W_TILE

    @pl.when(jnp.logical_not(shifted))
    def _():
        o_ref[...] = a_ref[...].T.astype(o_ref.dtype)

    @pl.when(shifted)
    def _():
        window = jnp.concatenate([a_ref[...], b_ref[...]], axis=0)
        o_ref[...] = window[W_ROW_OFF:W_ROW_OFF + W_TILE].T.astype(o_ref.dtype)


def _w_in_prep(w_in, layer):
    w_t = jnp.swapaxes(w_in, 1, 2)
    _, n, k = w_t.shape
    first_shifted = HY_OFF // W_TILE
    a_idx = lambda j: jnp.where(j < first_shifted, j, j - 1)
    tail_blocks = W_TILE // W_ROW_OFF
    assert W_TILE % W_ROW_OFF == 0 and n % W_ROW_OFF == 0
    return pl.pallas_call(
        _w_in_prep_kernel,
        grid=(Z_COLS // W_TILE,),
        in_specs=[pl.BlockSpec((None, W_TILE, k), lambda j: (layer, a_idx(j), 0)),
                  pl.BlockSpec((None, W_ROW_OFF, k), lambda j: (layer, (a_idx(j) + 1) * tail_blocks, 0))],
        out_specs=pl.BlockSpec((k, W_TILE), lambda j: (0, j)),
        out_shape=jax.ShapeDtypeStruct((k, Z_COLS), jnp.bfloat16),
        compiler_params=_cparams("parallel"),
        name="w_in_relayout",
    )(w_t, w_t)


def _mixer(h, hc, u, uc, gt, gtc, p, need_ctx):
    L, Lc = u.shape[0], uc.shape[0]
    w_in = _w_in_prep(p['w_in'], p['layer'])
    b_in = _pad_cols(p['b_in'])
    z = _matmul(u, w_in, b_in, tm=1024, tn=W_TILE)
    ncol = Z_COLS if need_ctx else HY_OFF
    zc = _matmul(uc, w_in, b_in[:ncol], tm=1024, tn=W_TILE, n_cols=ncol)
    zeros = lambda hd, dk, dv: jnp.zeros((hd, dv, dk), jnp.float32)
    yc_hg, hg_sf, hg_sb = _hgrn2(zc, Lc, p['lb'], p['hg_norm_w'],
                                 zeros(HG_HEADS, HG_DK, HG_DV), zeros(HG_HEADS, HG_DK, HG_DV))
    yc_gl, gl_sf, gl_sb = _gla(zc, Lc, p['gl_w_a2'], p['gl_b_a'], p['gl_norm_w'],
                               zeros(GL_HEADS, GL_DK, GL_DV), zeros(GL_HEADS, GL_DK, GL_DV))
    y_hg, _, _ = _hgrn2(z, L, p['lb'], p['hg_norm_w'], hg_sf, hg_sb)
    y_gl, _, _ = _gla(z, L, p['gl_w_a2'], p['gl_b_a'], p['gl_norm_w'], gl_sf, gl_sb)
    hy = (p['hy_conv_w'], p['hy_conv_b'], p['hy_f'], p['hy_skip'])
    y_hy = _hyena(z, L, *hy)
    h = _merge(z, L, (y_hy, y_hg, y_gl), p['w_branch'], p['w_out'], h, gt)
    if need_ctx:
        yc_hy = _hyena(zc, Lc, *hy)
        hc = _merge(zc, Lc, (yc_hy, yc_hg, yc_gl), p['w_branch'], p['w_out'], hc, gtc)
    return h, hc


GATHER_UNROLL = 8
GATHER_AHEAD = 2
GATHER_SLOTS = GATHER_AHEAD + 1


def _ffn_kernel(blk_exp_ref, n_used_ref, tok_ref, x_hbm, wg_ref, wu_ref, wd_ref, o_ref, xbuf, sem, wg_s, wu_s, wd_s):
    i = pl.program_id(0)
    n_used = n_used_ref[0]
    rows = o_ref.shape[0]

    def issue(step, slot):
        def body(r, carry):
            src = tok_ref[step * rows + r]
            pltpu.make_async_copy(x_hbm.at[pl.ds(src, 1)], xbuf.at[slot, pl.ds(r, 1)], sem.at[slot]).start()
            return carry

        lax.fori_loop(0, rows, body, 0, unroll=GATHER_UNROLL)

    for s in range(GATHER_AHEAD):
        @pl.when(jnp.logical_and(i == 0, s < n_used))
        def _(s=s):
            issue(s, s % GATHER_SLOTS)

    @pl.when(i + GATHER_AHEAD < n_used)
    def _():
        issue(i + GATHER_AHEAD, (i + GATHER_AHEAD) % GATHER_SLOTS)

    new_expert = jnp.logical_or(i == 0, blk_exp_ref[i] != blk_exp_ref[jnp.maximum(i - 1, 0)])

    @pl.when(jnp.logical_and(i < n_used, new_expert))
    def _():
        wg_s[...] = wg_ref[0].astype(jnp.bfloat16)
        wu_s[...] = wu_ref[0].astype(jnp.bfloat16)
        wd_s[...] = wd_ref[0].astype(jnp.bfloat16)

    @pl.when(i < n_used)
    def _():
        slot = i % GATHER_SLOTS
        pltpu.make_async_copy(xbuf.at[slot], xbuf.at[slot], sem.at[slot]).wait()
        x = xbuf[slot].astype(jnp.bfloat16)
        hg = jnp.dot(x, wg_s[...], preferred_element_type=jnp.float32)
        hu = jnp.dot(x, wu_s[...], preferred_element_type=jnp.float32)
        act = (hg * jax.nn.sigmoid(hg) * hu).astype(jnp.bfloat16)
        o_ref[...] = jnp.dot(act, wd_s[...], preferred_element_type=jnp.float32)

    @pl.when(i >= n_used)
    def _():
        o_ref[...] = jnp.zeros_like(o_ref)


def _grouped_ffn(x, buf_tok, blk_exp, n_used, layer, w_gate, w_up, w_down):
    p_len = buf_tok.shape[0]
    d = x.shape[1]
    n_blk = p_len // MOE_BLOCK
    grid_spec = pltpu.PrefetchScalarGridSpec(
        num_scalar_prefetch=3,
        grid=(n_blk,),
        in_specs=[pl.BlockSpec(memory_space=pl.ANY),
                  pl.BlockSpec((None, 1, d, D_FF), lambda i, be, nu, tk: (layer, be[i], 0, 0)),
                  pl.BlockSpec((None, 1, d, D_FF), lambda i, be, nu, tk: (layer, be[i], 0, 0)),
                  pl.BlockSpec((None, 1, D_FF, d), lambda i, be, nu, tk: (layer, be[i], 0, 0))],
        out_specs=pl.BlockSpec((MOE_BLOCK, d), lambda i, be, nu, tk: (i, 0)),
        scratch_shapes=[pltpu.VMEM((GATHER_SLOTS, MOE_BLOCK, d), jnp.float32), pltpu.SemaphoreType.DMA((GATHER_SLOTS,)),
                        pltpu.VMEM((d, D_FF), jnp.bfloat16), pltpu.VMEM((d, D_FF), jnp.bfloat16),
                        pltpu.VMEM((D_FF, d), jnp.bfloat16)],
    )
    return pl.pallas_call(
        _ffn_kernel,
        grid_spec=grid_spec,
        out_shape=jax.ShapeDtypeStruct((p_len, d), jnp.float32),
        compiler_params=_cparams("arbitrary"),
        name="moe_grouped_ffn",
    )(blk_exp, n_used, buf_tok, x, w_gate, w_up, w_down)


ROUTER_COLS = 128


ROUTE_TOKENS = 256
ROUTE_OUT = (0, 1, 2, 3, 4, 5)


def _route_kernel(lg_ref, tril_ref, o_ref, cnt_ref, carry):
    i = pl.program_id(0)

    @pl.when(i == 0)
    def _():
        carry[...] = jnp.zeros_like(carry)

    x = lg_ref[...]
    lane = lax.broadcasted_iota(jnp.int32, x.shape, 1).astype(jnp.float32)
    neg = jnp.float32(-jnp.inf)
    far = jnp.float32(ROUTER_COLS)
    red_max = lambda a: jnp.max(a, axis=1, keepdims=True)
    red_min = lambda a: jnp.min(a, axis=1, keepdims=True)
    red_sum = lambda a: jnp.sum(a, axis=1, keepdims=True)
    gmask = lane < N_GROUPS
    gl = jnp.where(gmask, x, neg)
    gmax = red_max(gl)
    p_top = 1.0 / red_sum(jnp.where(gmask, jnp.exp(x - gmax), 0.0))
    grp = red_min(jnp.where(gl == gmax, lane, far))
    lo = N_GROUPS + EXP_PER_GROUP * grp
    emask = jnp.logical_and(lane >= lo, lane < lo + EXP_PER_GROUP)
    el = jnp.where(emask, x, neg)
    ee = jnp.where(emask, jnp.exp(x - red_max(el)), 0.0)
    prob = ee / red_sum(ee)
    p1 = red_max(prob)
    i1 = red_min(jnp.where(jnp.logical_and(emask, prob == p1), lane, far))
    rest = jnp.where(jnp.logical_and(emask, lane != i1), prob, -1.0)
    p2 = red_max(rest)
    i2 = red_min(jnp.where(rest == p2, lane, far))
    w1 = p_top * p1 / (p1 + p2)
    w2 = p_top * p2 / (p1 + p2)
    pick1 = (lane == i1).astype(jnp.float32)
    pick2 = (lane == i2).astype(jnp.float32)
    picks = pick1 + pick2
    before = jnp.dot(tril_ref[...], picks.astype(jnp.bfloat16), preferred_element_type=jnp.float32) + carry[...]
    r1 = red_sum(pick1 * before)
    r2 = red_sum(pick2 * before)
    carry[...] += jnp.sum(picks, axis=0, keepdims=True)
    cnt_ref[...] = carry[...]
    out = jnp.zeros_like(x)
    for col, val in zip(ROUTE_OUT, (i1 - N_GROUPS, i2 - N_GROUPS, w1, w2, r1, r2)):
        out = jnp.where(lane == col, val, out)
    o_ref[...] = out


def _route(logits):
    assert TOP_K == 2
    n = logits.shape[0]
    t = min(ROUTE_TOKENS, n)
    tril = jnp.asarray(np.tril(np.ones((t, t), np.float32), -1), jnp.bfloat16)
    out, cnt = pl.pallas_call(
        _route_kernel,
        grid=(n // t,),
        in_specs=[pl.BlockSpec((t, ROUTER_COLS), lambda i: (i, 0)), pl.BlockSpec((t, t), lambda i: (0, 0))],
        out_specs=[pl.BlockSpec((t, ROUTER_COLS), lambda i: (i, 0)), pl.BlockSpec((1, ROUTER_COLS), lambda i: (0, 0))],
        out_shape=[jax.ShapeDtypeStruct((n, ROUTER_COLS), jnp.float32),
                   jax.ShapeDtypeStruct((1, ROUTER_COLS), jnp.float32)],
        scratch_shapes=[pltpu.VMEM((1, ROUTER_COLS), jnp.float32)],
        compiler_params=_cparams("arbitrary"),
        name="moe_route",
    )(logits, tril)
    expert = out[:, 0:2].astype(jnp.int32)
    weight = out[:, 2:4]
    rank = out[:, 4:6].astype(jnp.int32)
    counts = cnt[0, N_GROUPS:N_GROUPS + N_EXPERTS].astype(jnp.int32)
    return expert, weight, rank, counts


def _hier_moe(h, norm_w, shift, scale, gt, p):
    n, d = h.shape
    pad = ROUTER_COLS - N_GROUPS - N_EXPERTS
    w_r = _bf(jnp.concatenate([p['w_rg'], p['w_re'], jnp.zeros((d, pad), jnp.float32)], axis=1))
    b_r = jnp.concatenate([p['b_rg'], p['b_re'], jnp.zeros((pad,), jnp.float32)]).reshape(1, ROUTER_COLS)
    xb, logits = _norm_mod(h, norm_w, shift, scale, router=(w_r, b_r), out_dtype=jnp.float32)
    expert, weight, rank, counts = _route(logits)
    a = n * TOP_K
    padded = (counts + MOE_BLOCK - 1) // MOE_BLOCK * MOE_BLOCK
    pad_end = jnp.cumsum(padded)
    pad_off = pad_end - padded
    pos = (pad_off[expert] + rank).reshape(a)
    p_len = (a + N_EXPERTS * MOE_BLOCK + MOE_BLOCK - 1) // MOE_BLOCK * MOE_BLOCK
    n_blk = p_len // MOE_BLOCK
    tok_flat = jnp.arange(a, dtype=jnp.int32) // TOP_K
    buf_tok = (jnp.arange(p_len, dtype=jnp.int32) % n).at[pos].set(tok_flat)
    blk_start = jnp.arange(n_blk, dtype=jnp.int32) * MOE_BLOCK
    blk_exp = jnp.minimum(jnp.sum(pad_end[None, :] <= blk_start[:, None], axis=1), N_EXPERTS - 1).astype(jnp.int32)
    n_used = (pad_end[-1:] // MOE_BLOCK).astype(jnp.int32)
    y = _grouped_ffn(xb, buf_tok, blk_exp, n_used, p['layer'], p['w_gate'], p['w_up'], p['w_down'])
    return _moe_combine(y, pos, weight, h, gt)


def _combine_kernel(pos_ref, y_hbm, wts_ref, h_ref, gt_ref, o_ref, buf, sem):
    i = pl.program_id(0)
    tokens = h_ref.shape[0]

    def row_copy(step, slot, r, k):
        src = pos_ref[(step * tokens + r) * TOP_K + k]
        return pltpu.make_async_copy(y_hbm.at[pl.ds(src, 1)], buf.at[slot, k, pl.ds(r, 1)], sem.at[slot])

    def issue(step, slot):
        def body(r, carry):
            for k in range(TOP_K):
                row_copy(step, slot, r, k).start()
            return carry

        lax.fori_loop(0, tokens, body, 0, unroll=GATHER_UNROLL // TOP_K)

    @pl.when(i == 0)
    def _():
        issue(0, 0)

    @pl.when(i + 1 < pl.num_programs(0))
    def _():
        issue(i + 1, (i + 1) % 2)

    slot = i % 2
    pltpu.make_async_copy(buf.at[slot], buf.at[slot], sem.at[slot]).wait()
    rows = buf[slot]
    wts = wts_ref[...]
    acc = rows[0] * wts[:, 0:1]
    for k in range(1, TOP_K):
        acc = acc + rows[k] * wts[:, k:k + 1]
    o_ref[...] = h_ref[...] + gt_ref[...] * acc


COMBINE_TOKENS = 128


def _moe_combine(y, pos, wts, h, gt):
    n, d = h.shape
    tokens = min(COMBINE_TOKENS, n)
    grid_spec = pltpu.PrefetchScalarGridSpec(
        num_scalar_prefetch=1,
        grid=(n // tokens,),
        in_specs=[pl.BlockSpec(memory_space=pl.ANY),
                  pl.BlockSpec((tokens, TOP_K), lambda i, pos: (i, 0)),
                  pl.BlockSpec((tokens, d), lambda i, pos: (i, 0)),
                  pl.BlockSpec((1, d), lambda i, pos: (0, 0))],
        out_specs=pl.BlockSpec((tokens, d), lambda i, pos: (i, 0)),
        scratch_shapes=[pltpu.VMEM((2, TOP_K, tokens, d), jnp.float32), pltpu.SemaphoreType.DMA((2,))],
    )
    return pl.pallas_call(
        _combine_kernel,
        grid_spec=grid_spec,
        out_shape=jax.ShapeDtypeStruct((n, d), jnp.float32),
        compiler_params=_cparams("arbitrary"),
        name="moe_combine",
    )(pos, y, wts, h, gt)


def _final_norm_kernel(x_ref, w_ref, o_ref):
    x = x_ref[...]
    o_ref[...] = x * lax.rsqrt(jnp.mean(x * x, axis=-1, keepdims=True) + NORM_EPS) * w_ref[...]


def _final_norm(x, w, tm=512):
    m, d = x.shape
    return pl.pallas_call(
        _final_norm_kernel,
        grid=(m // tm,),
        in_specs=[pl.BlockSpec((tm, d), lambda i: (i, 0)), pl.BlockSpec((1, d), lambda i: (0, 0))],
        out_specs=pl.BlockSpec((tm, d), lambda i: (i, 0)),
        out_shape=jax.ShapeDtypeStruct((m, d), jnp.float32),
        compiler_params=_cparams("parallel"),
        name="final_rmsnorm",
    )(x, w.reshape(1, d))


def kernel(x, c, ctx, c_ctx, w_mod, b_mod, norm_mix_w, norm_ffn_w, w_in, b_in, hy_conv_w, hy_conv_b, hy_f_w1, hy_f_b1, hy_f_w2, hy_f_b2, hy_f_w3, hy_f_b3, hy_f_freq, hy_skip, hg_lb_raw, hg_norm_w, gl_w_a2, gl_b_a, gl_norm_w, w_branch, w_out, w_rg, b_rg, w_re, b_re, w_gate, w_up, w_down, final_norm_w):
    assert x.shape[0] == 1
    depth = w_mod.shape[0]
    lb_all = jnp.cumsum(jax.nn.softmax(hg_lb_raw, axis=0), axis=0)
    lb_all = lb_all - lb_all[:1]
    h, hc = x[0], ctx[0]
    cc = jnp.concatenate([c, c_ctx[None, :]], axis=0)
    for l in range(depth):
        need_ctx = l < depth - 1
        mod = _matmul(_bf(jax.nn.silu(cc)), w_mod, b_mod[l], layer=l)
        sh1, sc1, gt1, sh2, sc2, gt2 = jnp.split(mod[0:1], 6, axis=-1)
        sh1c, sc1c, gt1c, sh2c, sc2c, gt2c = jnp.split(mod[1:2], 6, axis=-1)
        p = dict(w_in=w_in, b_in=b_in[l], hy_conv_w=hy_conv_w[l], hy_conv_b=hy_conv_b[l],
                 hy_f=(hy_f_w1[l], hy_f_b1[l], hy_f_w2[l], hy_f_b2[l], hy_f_w3[l], hy_f_b3[l], hy_f_freq[l]),
                 hy_skip=hy_skip[l], lb=lb_all[l], hg_norm_w=hg_norm_w[l], gl_w_a2=gl_w_a2[l], gl_b_a=gl_b_a[l],
                 gl_norm_w=gl_norm_w[l], w_branch=w_branch[l], w_out=w_out[l], w_rg=w_rg[l], b_rg=b_rg[l],
                 w_re=w_re[l], b_re=b_re[l], layer=l, w_gate=w_gate, w_up=w_up, w_down=w_down)
        u = _norm_mod(h, norm_mix_w[l], sh1, sc1)
        uc = _norm_mod(hc, norm_mix_w[l], sh1c, sc1c)
        h, hc = _mixer(h, hc, u, uc, gt1, gt1c, p, need_ctx)
        h = _hier_moe(h, norm_ffn_w[l], sh2, sc2, gt2, p)
        if need_ctx:
            hc = _hier_moe(hc, norm_ffn_w[l], sh2c, sc2c, gt2c, p)
    return _final_norm(h, final_norm_w)[None]
```

```python
import functools
import math

import jax
import jax.numpy as jnp
import numpy as np
from jax import lax
from jax.experimental import pallas as pl
from jax.experimental.pallas import tpu as pltpu

D_MODEL = 2048
NORM_EPS = 1e-6

HY_C = D_MODEL // 2
HY_EMB = 33
HY_BANDS = (HY_EMB - 1) // 2
HY_DECAY_TARGET = 1e-2
HY_FAST_PCT = 0.3
HY_SLOW_PCT = 1.5
HY_MOD_SHIFT = 0.05

HG_HEADS = 8
HG_DK = 128
HG_DV = 128
HG_K = HG_HEADS * HG_DK
HG_V = HG_HEADS * HG_DV

GL_HEADS = 4
GL_DK = 128
GL_DV = 256
GL_K = GL_HEADS * GL_DK
GL_V = GL_HEADS * GL_DV
GL_RANK = 16
GL_TAU = 16.0

N_BRANCH = 3
HG_COLS = 3 * HG_K + 2 * HG_V
GL_COLS = 2 * GL_K + 2 * GL_V + 2 * GL_RANK
REC_COLS = HG_COLS + GL_COLS
HY_COLS = 3 * HY_C
MERGE_COLS = N_BRANCH * D_MODEL

COL_TILE = 512
GL_Q_OFF = HG_COLS
GL_V_OFF = GL_Q_OFF + 2 * GL_K
GL_A_OFF = GL_V_OFF + 2 * GL_V
HY_OFF = -(-(GL_A_OFF + 2 * GL_RANK) // HY_C) * HY_C
MG_OFF = HY_OFF + HY_COLS
Z_COLS = MG_OFF + MERGE_COLS
assert GL_Q_OFF % GL_K == 0 and GL_V_OFF % GL_V == 0 and GL_A_OFF % 128 == 0 and Z_COLS % COL_TILE == 0
assert MG_OFF % D_MODEL == 0
ZB_MG_OFF = 0
ZB_HY_OFF = MERGE_COLS

N_GROUPS = 4
EXP_PER_GROUP = 8
N_EXPERTS = N_GROUPS * EXP_PER_GROUP
TOP_K = 2
D_FF = D_MODEL // 4
MOE_BLOCK = 256

SCAN_CHUNK = 128
SCAN_HEAD_GROUP = 8
SCAN_CHUNKS_PER_STEP = 2
LOG2_E = 1.4426950408889634

VMEM_LIMIT_BYTES = 56 * 1024 * 1024


def _cparams(*sem):
    return pltpu.CompilerParams(dimension_semantics=sem, vmem_limit_bytes=VMEM_LIMIT_BYTES)


def _bf(a):
    return a.astype(jnp.bfloat16)


def _mm_kernel(x_ref, w_ref, b_ref, o_ref):
    acc = jnp.dot(x_ref[...], w_ref[...].astype(jnp.bfloat16), preferred_element_type=jnp.float32) + b_ref[...]
    o_ref[...] = acc.astype(o_ref.dtype)


def _matmul(x, w, bias=None, tm=512, tn=COL_TILE, out_dtype=jnp.float32, n_cols=None, layer=None, col_off=0,
            out_map=None):
    m, k = x.shape
    n = w.shape[-1] if n_cols is None else n_cols
    assert n_cols is None or n_cols % tn == 0
    assert layer is None or n % tn == 0
    tm = min(tm, -(-m // 8) * 8)
    mp = -(-m // tm) * tm
    np_ = -(-n // tn) * tn
    if bias is None:
        bias = jnp.zeros((n,), jnp.float32)
    if mp != m:
        x = jnp.pad(x, ((0, mp - m), (0, 0)))
    if np_ != n:
        w = jnp.pad(w, ((0, 0), (0, np_ - n)))
        bias = jnp.pad(bias, (0, np_ - n))
    if out_map is None:
        out_map = lambda j: j
    if layer is None:
        w_spec = pl.BlockSpec((k, tn), lambda i, j: (0, j + col_off))
    else:
        assert col_off == 0
        w_spec = pl.BlockSpec((None, k, tn), lambda i, j: (layer, 0, j))
    out = pl.pallas_call(
        _mm_kernel,
        grid=(mp // tm, np_ // tn),
        in_specs=[pl.BlockSpec((tm, k), lambda i, j: (i, 0)),
                  w_spec,
                  pl.BlockSpec((1, tn), lambda i, j: (0, j + col_off))],
        out_specs=pl.BlockSpec((tm, tn), lambda i, j: (i, out_map(j))),
        out_shape=jax.ShapeDtypeStruct((mp, np_), out_dtype),
        compiler_params=_cparams("parallel", "arbitrary"),
        name="dense_matmul",
    )(x, w, bias.reshape(1, -1))
    if mp != m or np_ != n:
        out = out[:m, :n]
    return out


def _norm_mod_kernel(*refs, with_router):
    if with_router:
        h_ref, w_ref, sh_ref, sc_ref, wr_ref, br_ref, o_ref, lg_ref = refs
    else:
        h_ref, w_ref, sh_ref, sc_ref, o_ref = refs
    x = h_ref[...]
    y = x * lax.rsqrt(jnp.mean(x * x, axis=-1, keepdims=True) + NORM_EPS) * w_ref[...]
    u = (y * (1.0 + sc_ref[...]) + sh_ref[...]).astype(jnp.bfloat16)
    o_ref[...] = u.astype(o_ref.dtype)
    if with_router:
        lg_ref[...] = jnp.dot(u, wr_ref[...], preferred_element_type=jnp.float32) + br_ref[...]


def _norm_mod(h, w, shift, scale, router=None, out_dtype=jnp.bfloat16):
    m, d = h.shape
    tm = min(512, m)
    row = pl.BlockSpec((tm, d), lambda i: (i, 0))
    vec = pl.BlockSpec((1, d), lambda i: (0, 0))
    args = [h, w.reshape(1, d), shift, scale]
    specs = [row, vec, vec, vec]
    out_shape = [jax.ShapeDtypeStruct((m, d), out_dtype)]
    out_specs = [row]
    if router is not None:
        args += list(router)
        specs += [pl.BlockSpec(router[0].shape, lambda i: (0, 0)), pl.BlockSpec(router[1].shape, lambda i: (0, 0))]
        out_shape.append(jax.ShapeDtypeStruct((m, router[0].shape[1]), jnp.float32))
        out_specs.append(pl.BlockSpec((tm, router[0].shape[1]), lambda i: (i, 0)))
    out = pl.pallas_call(
        functools.partial(_norm_mod_kernel, with_router=router is not None),
        grid=(m // tm,),
        in_specs=specs,
        out_specs=out_specs,
        out_shape=out_shape,
        compiler_params=_cparams("parallel"),
        name="norm_modulate",
    )(*args)
    return out if router is not None else out[0]


def _scan_masks(c, reverse):
    t = np.arange(c)
    ms = [np.eye(c, dtype=np.float32)]
    for lvl in range(int(math.log2(c))):
        upper = ((t >> lvl) & 1).astype(bool)
        same = (t[:, None] >> (lvl + 1)) == (t[None, :] >> (lvl + 1))
        m = same & upper[:, None] & (~upper)[None, :]
        ms.append((m.T if reverse else m).astype(np.float32))
    tri = t[None, :] >= t[:, None] if reverse else t[None, :] <= t[:, None]
    return jnp.asarray(np.stack(ms)), jnp.asarray(tri.astype(np.float32), dtype=jnp.bfloat16)


def _level_arg(cum, lvl, reverse):
    c = cum.shape[0]
    blk = 1 << lvl
    if blk >= 8:
        pieces = []
        for gs in range(0, c, 2 * blk):
            ref = cum[gs + blk:gs + blk + 1, :]
            pieces.append(ref - cum[gs:gs + blk, :])
            pieces.append(cum[gs + blk:gs + 2 * blk, :] - ref)
        arg = jnp.concatenate(pieces, axis=0)
    else:
        c3 = cum.reshape(c // 8, 8, cum.shape[1])
        sub = lax.broadcasted_iota(jnp.int32, c3.shape, 1)
        ref_row = ((sub >> lvl) | 1) << lvl
        ref = None
        for r in range(blk, 8, 2 * blk):
            cand = jnp.broadcast_to(c3[:, r:r + 1, :], c3.shape)
            ref = cand if ref is None else jnp.where(ref_row == r, cand, ref)
        upper = ((sub >> lvl) & 1) == 1
        arg = jnp.where(upper, c3 - ref, ref - c3).reshape(cum.shape)
    return -arg if reverse else arg


def _dot_nt(a, b):
    return lax.dot_general(a, b, (((1,), (1,)), ((), ())), preferred_element_type=jnp.float32)


def _dot_tn(a, b):
    return lax.dot_general(a, b, (((0,), (0,)), ((), ())), preferred_element_type=jnp.float32)


def _sigmoid_parts(z):
    e = jnp.exp(-jnp.abs(z))
    r = 1.0 / (1.0 + e)
    return jnp.minimum(z, 0.0) - jnp.log(1.0 + e), jnp.where(z >= 0.0, e * r, r)


def _scan_kernel(*refs, mode, reverse, final, heads, dk, dv, c, cps):
    it = iter(refs)
    q_ref = next(it)
    k_ref = next(it)
    v_ref = next(it)
    if mode == "hg":
        lbp_ref = next(it)
    else:
        a_ref = next(it)
        wa_ref = next(it)
        ba_ref = next(it)
    s0_ref = next(it)
    masks_ref = next(it)
    tri_ref = next(it)
    if final:
        oprev_ref = next(it)
        gate_ref = next(it)
        nw_ref = next(it)
    o_ref = next(it)
    st_ref = next(it)

    @pl.when(pl.program_id(0) == 0)
    def _():
        st_ref[...] = s0_ref[...]

    for ci in (reversed(range(cps)) if reverse else range(cps)):
        rs = slice(ci * c, (ci + 1) * c)
        if mode == "gl":
            la_all = jnp.dot(a_ref[rs, :].astype(jnp.bfloat16), wa_ref[...],
                             preferred_element_type=jnp.float32) + ba_ref[...]
        tri = tri_ref[...]
        tot_row = 0 if reverse else c - 1
        n_lvl = int(math.log2(c))
        ksl = lambda h: slice(h * dk, (h + 1) * dk)
        vsl = lambda h: slice(h * dv, (h + 1) * dv)
        for h0 in range(0, heads, SCAN_HEAD_GROUP):
            group = range(h0, min(h0 + SCAN_HEAD_GROUP, heads))
            q_, k_, cum_, qb_, kb_, o_, sc_ = {}, {}, {}, {}, {}, {}, {}
            for h in group:
                q = q_ref[rs, ksl(h)]
                if mode == "hg":
                    log_sig, sig_neg = _sigmoid_parts(k_ref[rs, ksl(h)])
                    la = lbp_ref[0:1, ksl(h)]
                    lbb = lbp_ref[1:2, ksl(h)] + log_sig
                    g = jnp.maximum(la, lbb) + jnp.log(1.0 + jnp.exp(-jnp.abs(la - lbb)))
                    k = lbp_ref[2:3, ksl(h)] * sig_neg
                    q = q * jax.nn.sigmoid(q)
                else:
                    g = _sigmoid_parts(la_all[:, ksl(h)])[0] * (1.0 / GL_TAU)
                    k = k_ref[rs, ksl(h)]
                    q = q * (dk ** -0.5)
                g = g * LOG2_E
                g1 = g.astype(jnp.bfloat16)
                r1 = g - g1.astype(jnp.float32)
                g2 = r1.astype(jnp.bfloat16)
                g3 = (r1 - g2.astype(jnp.float32)).astype(jnp.bfloat16)
                cum_[h] = (jnp.dot(tri, g1, preferred_element_type=jnp.float32)
                           + jnp.dot(tri, g2, preferred_element_type=jnp.float32)
                           + jnp.dot(tri, g3, preferred_element_type=jnp.float32))
                q_[h], k_[h] = q, k
            for h in group:
                cum = cum_[h]
                tot = cum[tot_row:tot_row + 1, :]
                st = st_ref[h]
                v = v_ref[rs, vsl(h)].astype(jnp.bfloat16)
                o_[h] = _dot_nt((q_[h] * jnp.exp2(cum)).astype(jnp.bfloat16), st.astype(jnp.bfloat16))
                kt = (k_[h] * jnp.exp2(tot - cum)).astype(jnp.bfloat16)
                st_ref[h] = st * jnp.exp2(tot) + _dot_tn(v, kt)
                qb_[h] = q_[h].astype(jnp.bfloat16)
                kb_[h] = k_[h].astype(jnp.bfloat16)
                sc_[h] = masks_ref[0] * _dot_nt(qb_[h], kb_[h])
            for lvl in range(n_lvl):
                for h in group:
                    e = jnp.exp2(_level_arg(cum_[h], lvl, reverse)).astype(jnp.bfloat16)
                    sc_[h] = sc_[h] + masks_ref[1 + lvl] * _dot_nt(qb_[h] * e, kb_[h] * e)
            for h in group:
                v = v_ref[rs, vsl(h)].astype(jnp.bfloat16)
                o = o_[h] + jnp.dot(sc_[h].astype(jnp.bfloat16), v, preferred_element_type=jnp.float32)
                if final:
                    o = o + oprev_ref[rs, vsl(h)]
                    y = o * lax.rsqrt(jnp.mean(o * o, axis=-1, keepdims=True) + NORM_EPS) * nw_ref[...]
                    gt = gate_ref[rs, vsl(h)]
                    act = jax.nn.sigmoid(gt) if mode == "hg" else gt * jax.nn.sigmoid(gt)
                    o_ref[rs, vsl(h)] = (y * act).astype(o_ref.dtype)
                else:
                    o_ref[rs, vsl(h)] = o


def _scan_pass(mode, reverse, final, L, srcs, s0, params, final_srcs=(), norm_w=None):
    heads, dk, dv = (HG_HEADS, HG_DK, HG_DV) if mode == "hg" else (GL_HEADS, GL_DK, GL_DV)
    c = min(SCAN_CHUNK, L)
    cps = min(SCAN_CHUNKS_PER_STEP, L // c)
    rows = c * cps
    nb = L // rows
    row = (lambda i: nb - 1 - i) if reverse else (lambda i: i)
    masks, tri = _scan_masks(c, reverse)

    def const(shape):
        return pl.BlockSpec(shape, lambda i: (0,) * len(shape))

    def rowblock(width, cb):
        return pl.BlockSpec((rows, width), lambda i: (row(i), cb))

    args = [a for a, _, _ in srcs] + list(params) + [s0, masks, tri]
    specs = ([rowblock(w, cb) for _, w, cb in srcs] + [const(p.shape) for p in params]
             + [const(s0.shape), const(masks.shape), const(tri.shape)])
    if final:
        args += [a for a, _, _ in final_srcs] + [norm_w]
        specs += [rowblock(w, cb) for _, w, cb in final_srcs] + [const(norm_w.shape)]
    return pl.pallas_call(
        functools.partial(_scan_kernel, mode=mode, reverse=reverse, final=final, heads=heads, dk=dk, dv=dv, c=c,
                          cps=cps),
        grid=(nb,),
        in_specs=specs,
        out_specs=[pl.BlockSpec((rows, heads * dv), lambda i: (row(i), 0)), const((heads, dv, dk))],
        out_shape=[jax.ShapeDtypeStruct((L, heads * dv), jnp.bfloat16 if final else jnp.float32),
                   jax.ShapeDtypeStruct((heads, dv, dk), jnp.float32)],
        compiler_params=_cparams("arbitrary"),
        name=f"scan_{mode}_{'bwd' if reverse else 'fwd'}",
    )(*args)


def _hgrn2(z, L, lb, norm_w, s0_f, s0_b):
    lbp = lambda d: jnp.stack([jnp.log(lb[d]), jnp.log1p(-lb[d]), 1.0 - lb[d]])
    w = HG_K
    o_b, s_b = _scan_pass("hg", True, False, L, [(z, w, 0), (z, w, 2), (z, w, 3)], s0_b, [lbp(1)])
    y, s_f = _scan_pass("hg", False, True, L, [(z, w, 0), (z, w, 1), (z, w, 3)], s0_f, [lbp(0)],
                        final_srcs=[(o_b, HG_V, 0), (z, HG_V, 4)], norm_w=norm_w.reshape(1, HG_DV))
    return y, s_f, s_b


def _gla(z, L, w_a2, b_a, norm_w, s0_f, s0_b):
    def gate_params(d):
        wa = jnp.zeros((128, GL_K), jnp.float32).at[d * GL_RANK:(d + 1) * GL_RANK].set(w_a2[d])
        return [_bf(wa), b_a[d].reshape(1, GL_K)]

    srcs = [(z, GL_K, GL_Q_OFF // GL_K), (z, GL_K, GL_Q_OFF // GL_K + 1), (z, GL_V, GL_V_OFF // GL_V),
            (z, 128, GL_A_OFF // 128)]
    o_b, s_b = _scan_pass("gl", True, False, L, srcs, s0_b, gate_params(1))
    y, s_f = _scan_pass("gl", False, True, L, srcs, s0_f, gate_params(0),
                        final_srcs=[(o_b, GL_V, 0), (z, GL_V, GL_V_OFF // GL_V + 1)], norm_w=norm_w.reshape(1, GL_DV))
    return y, s_f, s_b


HY_FEAT_PAD = 128


def _hy_filter_kernel(f_ref, w1_ref, b1_ref, fq_ref, w2_ref, b2_ref, w3_ref, b3_ref, dl_ref, h_ref, s_ref):
    i = pl.program_id(0)
    f = f_ref[...]
    fq = fq_ref[...]
    a = jnp.sin(fq * (jnp.dot(f.astype(jnp.bfloat16), w1_ref[...], preferred_element_type=jnp.float32) + b1_ref[...]))
    a = jnp.sin(fq * (jnp.dot(a.astype(jnp.bfloat16), w2_ref[...], preferred_element_type=jnp.float32) + b2_ref[...]))
    hh = jnp.dot(a.astype(jnp.bfloat16), w3_ref[...], preferred_element_type=jnp.float32) + b3_ref[...]
    hh = hh * (jnp.exp(-f[:, 0:1] * dl_ref[...]) + HY_MOD_SHIFT)
    h_ref[...] = hh
    part = jnp.sum(jnp.abs(hh).reshape(hh.shape[0] // 8, 8, hh.shape[1]), axis=0)

    @pl.when(i == 0)
    def _():
        s_ref[...] = part

    @pl.when(i > 0)
    def _():
        s_ref[...] += part


def _hyena_filters(L, w1, b1, w2, b2, w3, b3, freq):
    t = jnp.linspace(0.0, 1.0, L, dtype=jnp.float32)[:, None]
    ang = 2.0 * math.pi * jnp.arange(L, dtype=jnp.float32)[:, None] / L
    bands = jnp.linspace(1e-4, HY_BANDS - 1, HY_BANDS, dtype=jnp.float32)[None, :]
    feats = jnp.concatenate([t, jnp.cos(bands * ang), -jnp.sin(bands * ang),
                             jnp.zeros((L, HY_FEAT_PAD - HY_EMB), jnp.float32)], axis=-1)
    deltas = jnp.abs(jnp.linspace(math.log(HY_DECAY_TARGET) / HY_SLOW_PCT, math.log(HY_DECAY_TARGET) / HY_FAST_PCT,
                                  HY_C, dtype=jnp.float32))
    fh = w1.shape[1]
    padm = lambda a, r, c: _bf(jnp.pad(a, ((0, r - a.shape[0]), (0, c - a.shape[1]))))
    padv = lambda a: jnp.pad(a, (0, HY_FEAT_PAD - a.shape[0])).reshape(1, HY_FEAT_PAD)
    tm = min(512, L)
    const = lambda r, c: pl.BlockSpec((r, c), lambda i: (0, 0))
    hfil, sums = pl.pallas_call(
        _hy_filter_kernel,
        grid=(L // tm,),
        in_specs=[pl.BlockSpec((tm, HY_FEAT_PAD), lambda i: (i, 0)),
                  const(HY_FEAT_PAD, HY_FEAT_PAD), const(1, HY_FEAT_PAD), const(1, HY_FEAT_PAD),
                  const(HY_FEAT_PAD, HY_FEAT_PAD), const(1, HY_FEAT_PAD),
                  const(HY_FEAT_PAD, 2 * HY_C), const(1, 2 * HY_C), const(1, 2 * HY_C)],
        out_specs=[pl.BlockSpec((tm, 2 * HY_C), lambda i: (i, 0)), const(8, 2 * HY_C)],
        out_shape=[jax.ShapeDtypeStruct((L, 2 * HY_C), jnp.float32), jax.ShapeDtypeStruct((8, 2 * HY_C), jnp.float32)],
        compiler_params=_cparams("arbitrary"),
        name="hyena_filters",
    )(feats, padm(w1, HY_FEAT_PAD, HY_FEAT_PAD), padv(b1), padv(freq), padm(w2, HY_FEAT_PAD, HY_FEAT_PAD), padv(b2),
      padm(w3, HY_FEAT_PAD, 2 * HY_C), b3.reshape(1, 2 * HY_C), jnp.tile(deltas, 2).reshape(1, 2 * HY_C))
    assert fh <= HY_FEAT_PAD
    inorm = 1.0 / jnp.sum(sums, axis=0)
    return hfil, inorm.reshape(2, HY_C)


def _hy_pre_kernel(x0_ref, x1_ref, v_ref, x0p_ref, x1p_ref, vp_ref, x0n_ref, x1n_ref, vn_ref, w_ref, b_ref,
                   vo_ref, x0o_ref):
    i = pl.program_id(0)
    first = i == 0
    last = i == pl.num_programs(0) - 1
    tm = x0_ref.shape[0]
    row = lax.broadcasted_iota(jnp.int32, x0_ref.shape, 0)

    def conv(x_ref, p_ref, n_ref, g):
        x = x_ref[...].astype(jnp.float32)
        cs = slice(g * HY_C, (g + 1) * HY_C)
        prev_row = jnp.where(first, 0.0, p_ref[...].astype(jnp.float32)[HALO_ROWS - 1:HALO_ROWS, :])
        next_row = jnp.where(last, 0.0, n_ref[...].astype(jnp.float32)[0:1, :])
        xp = jnp.where(row == 0, prev_row, pltpu.roll(x, 1, 0))
        xn = jnp.where(row == tm - 1, next_row, pltpu.roll(x, tm - 1, 0))
        return w_ref[0:1, cs] * xp + w_ref[1:2, cs] * x + w_ref[2:3, cs] * xn + b_ref[0:1, cs]

    x0 = conv(x0_ref, x0p_ref, x0n_ref, 0)
    x1 = conv(x1_ref, x1p_ref, x1n_ref, 1)
    v = conv(v_ref, vp_ref, vn_ref, 2)
    vo_ref[...] = (v * x1).astype(vo_ref.dtype)
    x0o_ref[...] = x0.astype(x0o_ref.dtype)


HALO_ROWS = 16


def _hy_pre(z, L, conv_w, conv_b):
    tm = min(256, L)
    nbh = L // HALO_ROWS
    cb = ZB_HY_OFF // HY_C
    main = lambda g: pl.BlockSpec((tm, HY_C), lambda i: (i, cb + g))
    prev = lambda g: pl.BlockSpec((HALO_ROWS, HY_C), lambda i: (jnp.maximum(i * (tm // HALO_ROWS) - 1, 0), cb + g))
    nxt = lambda g: pl.BlockSpec((HALO_ROWS, HY_C),
                                 lambda i: (jnp.minimum((i + 1) * (tm // HALO_ROWS), nbh - 1), cb + g))
    const = lambda a: pl.BlockSpec(a.shape, lambda i: (0, 0))
    cbias = conv_b.reshape(1, HY_COLS)
    return pl.pallas_call(
        _hy_pre_kernel,
        grid=(L // tm,),
        in_specs=[main(0), main(1), main(2), prev(0), prev(1), prev(2), nxt(0), nxt(1), nxt(2),
                  const(conv_w), const(cbias)],
        out_specs=[pl.BlockSpec((tm, HY_C), lambda i: (i, 0))] * 2,
        out_shape=[jax.ShapeDtypeStruct((L, HY_C), jnp.float32), jax.ShapeDtypeStruct((L, HY_C), jnp.bfloat16)],
        compiler_params=_cparams("parallel"),
        name="hyena_short_conv",
    )(z, z, z, z, z, z, z, z, z, conv_w, cbias)


HY_N1 = 128
HY_TWO_STAGE_MIN_L = 1024


def _dft_outer_table(n1, cols):
    ang = -2.0 * np.pi * np.outer(np.arange(n1 // 2) + 0.5, np.arange(cols)) / n1
    return jnp.asarray(np.concatenate([np.cos(ang), np.sin(ang)], axis=0), jnp.bfloat16)


def _dft_inner_table(n1, n2):
    j2 = np.arange(n2)
    f_ang = -2.0 * np.pi * np.outer(np.arange(n2), j2) / n2
    tw_ang = -2.0 * np.pi * np.outer(np.arange(n1 // 2) + 0.5, j2) / (n1 * n2)
    fr, fi = jnp.asarray(np.cos(f_ang), jnp.float32), jnp.asarray(np.sin(f_ang), jnp.float32)
    twr, twi = jnp.asarray(np.cos(tw_ang), jnp.float32), jnp.asarray(np.sin(tw_ang), jnp.float32)
    mr = fr[None] * twr[:, None, :] - fi[None] * twi[:, None, :]
    mi = fr[None] * twi[:, None, :] + fi[None] * twr[:, None, :]
    return _bf(jnp.concatenate([jnp.concatenate([mr, -mi], axis=2), jnp.concatenate([mi, mr], axis=2)], axis=1))


def _spectral_product(xv, xh, inorm, half):
    inf, inb = inorm[0:1, :], inorm[1:2, :]
    gr = xh[:half, :HY_C] * inf + xh[:half, HY_C:] * inb
    gi = xh[half:, :HY_C] * inf - xh[half:, HY_C:] * inb
    xr, xi = xv[:half], xv[half:]
    return jnp.concatenate([xr * gr - xi * gi, xr * gi + xi * gr], axis=0).astype(jnp.bfloat16)


def _hy_spec_kernel(r_ref, avr_ref, avi_ref, ahr_ref, ahi_ref, inorm_ref, br_ref, bi_ref):
    r = r_ref[0]
    n2 = avr_ref.shape[1]
    xv = jnp.dot(r, jnp.concatenate([avr_ref[0], avi_ref[0]], axis=0), preferred_element_type=jnp.float32)
    xh = jnp.dot(r, jnp.concatenate([ahr_ref[0], ahi_ref[0]], axis=0), preferred_element_type=jnp.float32)
    b = _dot_tn(r, _spectral_product(xv, xh, inorm_ref[...], n2))
    br_ref[0] = b[:n2].astype(br_ref.dtype)
    bi_ref[0] = b[n2:].astype(bi_ref.dtype)


def _hy_spec(r, av, ah, inorm, n1, n2):
    av3 = av.reshape(n1, n2, HY_C)
    ah3 = ah.reshape(n1, n2, 2 * HY_C)
    h1 = n1 // 2
    out = jax.ShapeDtypeStruct((h1, n2, HY_C), jnp.bfloat16)
    return pl.pallas_call(
        _hy_spec_kernel,
        grid=(h1,),
        in_specs=[pl.BlockSpec((1, 2 * n2, 2 * n2), lambda k: (k, 0, 0)),
                  pl.BlockSpec((1, n2, HY_C), lambda k: (k, 0, 0)),
                  pl.BlockSpec((1, n2, HY_C), lambda k: (k + h1, 0, 0)),
                  pl.BlockSpec((1, n2, 2 * HY_C), lambda k: (k, 0, 0)),
                  pl.BlockSpec((1, n2, 2 * HY_C), lambda k: (k + h1, 0, 0)),
                  pl.BlockSpec((2, HY_C), lambda k: (0, 0))],
        out_specs=[pl.BlockSpec((1, n2, HY_C), lambda k: (k, 0, 0))] * 2,
        out_shape=[out, out],
        compiler_params=_cparams("parallel"),
        name="hyena_spectral",
    )(r, av3, av3, ah3, ah3, inorm)


def _hy_spec_direct_kernel(xv_ref, xh_ref, inorm_ref, yr_ref, yi_ref):
    half = yr_ref.shape[0]
    y = _spectral_product(xv_ref[...].astype(jnp.float32), xh_ref[...].astype(jnp.float32), inorm_ref[...], half)
    yr_ref[...] = y[:half]
    yi_ref[...] = y[half:]


def _hy_spec_direct(xv, xh, inorm, L):
    full = lambda a: pl.BlockSpec(a.shape, lambda i: (0, 0))
    out = jax.ShapeDtypeStruct((L, HY_C), jnp.bfloat16)
    return pl.pallas_call(
        _hy_spec_direct_kernel,
        grid=(1,),
        in_specs=[full(xv), full(xh), full(inorm)],
        out_specs=[pl.BlockSpec((L, HY_C), lambda i: (0, 0))] * 2,
        out_shape=[out, out],
        compiler_params=_cparams("arbitrary"),
        name="hyena_spectral_direct",
    )(xv, xh, inorm)


def _hy_post_kernel(tr_ref, ti_ref, br_ref, bi_ref, v_ref, x0_ref, skip_ref, o_ref, *, scale):
    acc = (jnp.dot(tr_ref[...], br_ref[...], preferred_element_type=jnp.float32)
           + jnp.dot(ti_ref[...], bi_ref[...], preferred_element_type=jnp.float32))
    y = (acc * scale + v_ref[...].astype(jnp.float32) * skip_ref[...]) * x0_ref[...].astype(jnp.float32)
    o_ref[...] = y.astype(o_ref.dtype)


def _hy_post(t_fwd, b_r, b_i, v, x0, skip, L, h1, n2):
    ncol = n2 * HY_C
    tn = min(4096, ncol)
    tr_t = t_fwd[:h1].T
    ti_t = t_fwd[h1:].T
    skip_t = jnp.tile(skip, tn // HY_C).reshape(1, tn)
    col = lambda rows: pl.BlockSpec((rows, tn), lambda j: (0, j))
    rows_out = tr_t.shape[0]
    y = pl.pallas_call(
        functools.partial(_hy_post_kernel, scale=1.0 / L),
        grid=(ncol // tn,),
        in_specs=[pl.BlockSpec(tr_t.shape, lambda j: (0, 0)), pl.BlockSpec(ti_t.shape, lambda j: (0, 0)),
                  col(h1), col(h1), col(rows_out), col(rows_out), pl.BlockSpec((1, tn), lambda j: (0, 0))],
        out_specs=col(rows_out),
        out_shape=jax.ShapeDtypeStruct((rows_out, ncol), jnp.bfloat16),
        compiler_params=_cparams("parallel"),
        name="hyena_inverse",
    )(tr_t, ti_t, b_r.reshape(h1, ncol), b_i.reshape(h1, ncol), v.reshape(rows_out, ncol),
      x0.reshape(rows_out, ncol), skip_t)
    return y.reshape(L, HY_C)


SUBLANE = 8
HY_COL_TILE = 512


def _dft_outer_kron(n1):
    h1 = n1 // 2
    ang = -2.0 * np.pi * np.outer(np.arange(h1) + 0.5, np.arange(h1)) / n1
    eye = np.eye(SUBLANE)
    t_r, t_i = np.cos(ang), np.sin(ang)
    fwd = np.kron(np.concatenate([t_r, t_i], axis=0), eye)
    inv = np.concatenate([np.kron(t_r.T, eye), np.kron(t_i.T, eye)], axis=1)
    return jnp.asarray(fwd, jnp.bfloat16), jnp.asarray(inv, jnp.bfloat16)


def _hy_outer_fwd_kernel(t_ref, x_ref, o_ref):
    x = x_ref[...]
    rows_in, rows_out = x.shape[0], t_ref.shape[0] // SUBLANE
    cw = x.shape[2]
    parts = []
    for s in range(0, x.shape[1], SUBLANE):
        xs = x[:, s:s + SUBLANE, :].reshape(rows_in * SUBLANE, cw).astype(jnp.bfloat16)
        r = jnp.dot(t_ref[...], xs, preferred_element_type=jnp.float32)
        parts.append(r.reshape(rows_out, SUBLANE, cw))
    o_ref[...] = jnp.concatenate(parts, axis=1).astype(o_ref.dtype)


def _hy_outer_fwd(t_kron, x3):
    h1, n2, w = x3.shape
    n1 = 2 * h1
    blk = 2 * SUBLANE
    return pl.pallas_call(
        _hy_outer_fwd_kernel,
        grid=(n2 // blk, w // HY_COL_TILE),
        in_specs=[pl.BlockSpec(t_kron.shape, lambda j, cc: (0, 0)),
                  pl.BlockSpec((h1, blk, HY_COL_TILE), lambda j, cc: (0, j, cc))],
        out_specs=pl.BlockSpec((n1, blk, HY_COL_TILE), lambda j, cc: (0, j, cc)),
        out_shape=jax.ShapeDtypeStruct((n1, n2, w), jnp.bfloat16),
        compiler_params=_cparams("parallel", "parallel"),
        name="hyena_outer_dft",
    )(t_kron, x3)


def _hy_outer_inv_kernel(t_ref, br_ref, bi_ref, v_ref, x0_ref, skip_ref, o_ref, *, scale):
    br = br_ref[...].astype(jnp.float32)
    bi = bi_ref[...].astype(jnp.float32)
    v = v_ref[...]
    x0 = x0_ref[...].astype(jnp.float32)
    h1, _, cw = br.shape
    parts = []
    for s in range(0, br.shape[1], SUBLANE):
        sl = slice(s, s + SUBLANE)
        b = jnp.concatenate([br[:, sl, :].reshape(h1 * SUBLANE, cw), bi[:, sl, :].reshape(h1 * SUBLANE, cw)], axis=0)
        r = jnp.dot(t_ref[...], b.astype(jnp.bfloat16), preferred_element_type=jnp.float32)
        parts.append((r.reshape(h1, SUBLANE, cw) * scale + v[:, sl, :] * skip_ref[...]) * x0[:, sl, :])
    o_ref[...] = jnp.concatenate(parts, axis=1).astype(o_ref.dtype)


def _hy_outer_inv(t_kron_inv, b_r, b_i, v3, x03, skip, L):
    h1, n2, w = b_r.shape
    blk = 2 * SUBLANE
    tile = pl.BlockSpec((h1, blk, HY_COL_TILE), lambda j, cc: (0, j, cc))
    return pl.pallas_call(
        functools.partial(_hy_outer_inv_kernel, scale=1.0 / L),
        grid=(n2 // blk, w // HY_COL_TILE),
        in_specs=[pl.BlockSpec(t_kron_inv.shape, lambda j, cc: (0, 0)), tile, tile, tile, tile,
                  pl.BlockSpec((1, 1, HY_COL_TILE), lambda j, cc: (0, 0, cc))],
        out_specs=tile,
        out_shape=jax.ShapeDtypeStruct((h1, n2, w), jnp.bfloat16),
        compiler_params=_cparams("parallel", "parallel"),
        name="hyena_outer_idft",
    )(t_kron_inv, b_r, b_i, v3, x03, skip.reshape(1, 1, w))


def _hyena(z, L, conv_w, conv_b, fparams, skip):
    v, x0 = _hy_pre(z, L, conv_w, conv_b)
    hfil, inorm = _hyena_filters(L, *fparams)
    if L >= HY_TWO_STAGE_MIN_L:
        n1 = HY_N1
        n2 = 2 * L // n1
        h1 = n1 // 2
        assert n2 % (2 * SUBLANE) == 0
        t_kron, t_kron_inv = _dft_outer_kron(n1)
        v3 = v.reshape(h1, n2, HY_C)
        av = _hy_outer_fwd(t_kron, v3)
        ah = _hy_outer_fwd(t_kron, hfil.reshape(h1, n2, 2 * HY_C))
        b_r, b_i = _hy_spec(_dft_inner_table(n1, n2), av, ah, inorm, n1, n2)
        return _hy_outer_inv(t_kron_inv, b_r, b_i, v3, x0.reshape(h1, n2, HY_C), skip, L).reshape(L, HY_C)
    t_fwd = _dft_outer_table(2 * L, L)
    xv = _matmul(t_fwd, _bf(v), tm=2 * L, tn=HY_C, out_dtype=jnp.bfloat16)
    xh = _matmul(t_fwd, _bf(hfil), tm=2 * L, tn=HY_C, out_dtype=jnp.bfloat16)
    y_r, y_i = _hy_spec_direct(xv, xh, inorm, L)
    return _hy_post(t_fwd, y_r, y_i, v, x0, skip, L, L, 1)


def _merge_kernel(yh_ref, yg_ref, yl_ref, gate_h_ref, gate_g_ref, gate_l_ref, wb_ref, o_ref):
    acc = None
    for br, (y_ref, g_ref) in enumerate(((yh_ref, gate_h_ref), (yg_ref, gate_g_ref), (yl_ref, gate_l_ref))):
        t = (jnp.dot(y_ref[...], wb_ref[br], preferred_element_type=jnp.float32)
             * jax.nn.sigmoid(g_ref[...].astype(jnp.float32)))
        acc = t if acc is None else acc + t
    o_ref[...] = acc.astype(o_ref.dtype)


def _proj_residual_kernel(m_ref, w_ref, h_ref, gt_ref, o_ref):
    o_ref[...] = h_ref[...] + gt_ref[...] * jnp.dot(m_ref[...], w_ref[...], preferred_element_type=jnp.float32)


MERGE_ROWS = 512


def _merge(z, L, ys, w_branch, w_out, h, gt):
    tm = min(MERGE_ROWS, L)
    gb = ZB_MG_OFF // D_MODEL
    ybs = pl.BlockSpec((tm, HY_C), lambda i: (i, 0))
    gate = lambda br: pl.BlockSpec((tm, D_MODEL), lambda i: (i, gb + br))
    row = pl.BlockSpec((tm, D_MODEL), lambda i: (i, 0))
    merged = pl.pallas_call(
        _merge_kernel,
        grid=(L // tm,),
        in_specs=[ybs, ybs, ybs, gate(0), gate(1), gate(2),
                  pl.BlockSpec((N_BRANCH, HY_C, D_MODEL), lambda i: (0, 0, 0), pipeline_mode=pl.Buffered(1))],
        out_specs=row,
        out_shape=jax.ShapeDtypeStruct((L, D_MODEL), jnp.bfloat16),
        compiler_params=_cparams("parallel"),
        name="branch_merge",
    )(ys[0], ys[1], ys[2], z, z, z, _bf(w_branch))
    return pl.pallas_call(
        _proj_residual_kernel,
        grid=(L // tm,),
        in_specs=[row, pl.BlockSpec((D_MODEL, D_MODEL), lambda i: (0, 0), pipeline_mode=pl.Buffered(1)), row,
                  pl.BlockSpec((1, D_MODEL), lambda i: (0, 0))],
        out_specs=row,
        out_shape=jax.ShapeDtypeStruct((L, D_MODEL), jnp.float32),
        compiler_params=_cparams("parallel"),
        name="out_proj_residual",
    )(merged, _bf(w_out), h, gt)


def _pad_cols(a):
    pad = lambda n: jnp.zeros(a.shape[:-1] + (n,), a.dtype)
    return jnp.concatenate([a[..., :REC_COLS], pad(HY_OFF - REC_COLS), a[..., REC_COLS:]], axis=-1)


W_TILE = 1024
W_SHIFT = HY_OFF - REC_COLS
W_ROW_OFF = W_TILE - W_SHIFT
assert 0 < W_SHIFT <= W_TILE and W_ROW_OFF % 8 == 0 and HY_OFF % W_TILE == 0 and Z_COLS % W_TILE == 0


def _w_in_prep_kernel(a_ref, b_ref, o_ref):
    j = pl.program_id(0)
    shifted = j >= HY_OFF // W_TILE

    @pl.when(jnp.logical_not(shifted))
    def _():
        o_ref[...] = a_ref[...].T.astype(o_ref.dtype)

    @pl.when(shifted)
    def _():
        window = jnp.concatenate([a_ref[...], b_ref[...]], axis=0)
        o_ref[...] = window[W_ROW_OFF:W_ROW_OFF + W_TILE].T.astype(o_ref.dtype)


def _w_in_prep(w_in, layer):
    w_t = jnp.swapaxes(w_in, 1, 2)
    _, n, k = w_t.shape
    first_shifted = HY_OFF // W_TILE
    a_idx = lambda j: jnp.where(j < first_shifted, j, j - 1)
    tail_blocks = W_TILE // W_ROW_OFF
    assert W_TILE % W_ROW_OFF == 0 and n % W_ROW_OFF == 0
    return pl.pallas_call(
        _w_in_prep_kernel,
        grid=(Z_COLS // W_TILE,),
        in_specs=[pl.BlockSpec((None, W_TILE, k), lambda j: (layer, a_idx(j), 0)),
                  pl.BlockSpec((None, W_ROW_OFF, k), lambda j: (layer, (a_idx(j) + 1) * tail_blocks, 0))],
        out_specs=pl.BlockSpec((k, W_TILE), lambda j: (0, j)),
        out_shape=jax.ShapeDtypeStruct((k, Z_COLS), jnp.bfloat16),
        compiler_params=_cparams("parallel"),
        name="w_in_relayout",
    )(w_t, w_t)


def _mixer(h, hc, u, uc, gt, gtc, p, need_ctx):
    L, Lc = u.shape[0], uc.shape[0]
    w_in = _w_in_prep(p['w_in'], p['layer'])
    b_in = _pad_cols(p['b_in'])
    rest = dict(tm=1024, tn=W_TILE, n_cols=Z_COLS - HY_OFF, col_off=HY_OFF // W_TILE, out_dtype=jnp.bfloat16,
                out_map=lambda j: jnp.where(j < HY_COLS // W_TILE, j + MERGE_COLS // W_TILE, j - HY_COLS // W_TILE))
    z = _matmul(u, w_in, b_in, tm=1024, tn=W_TILE, n_cols=HY_OFF)
    zb = _matmul(u, w_in, b_in, **rest)
    zc = _matmul(uc, w_in, b_in, tm=1024, tn=W_TILE, n_cols=HY_OFF)
    zcb = _matmul(uc, w_in, b_in, **rest) if need_ctx else None
    zeros = lambda hd, dk, dv: jnp.zeros((hd, dv, dk), jnp.float32)
    yc_hg, hg_sf, hg_sb = _hgrn2(zc, Lc, p['lb'], p['hg_norm_w'],
                                 zeros(HG_HEADS, HG_DK, HG_DV), zeros(HG_HEADS, HG_DK, HG_DV))
    yc_gl, gl_sf, gl_sb = _gla(zc, Lc, p['gl_w_a2'], p['gl_b_a'], p['gl_norm_w'],
                               zeros(GL_HEADS, GL_DK, GL_DV), zeros(GL_HEADS, GL_DK, GL_DV))
    y_hg, _, _ = _hgrn2(z, L, p['lb'], p['hg_norm_w'], hg_sf, hg_sb)
    y_gl, _, _ = _gla(z, L, p['gl_w_a2'], p['gl_b_a'], p['gl_norm_w'], gl_sf, gl_sb)
    hy = (p['hy_conv_w'], p['hy_conv_b'], p['hy_f'], p['hy_skip'])
    y_hy = _hyena(zb, L, *hy)
    h = _merge(zb, L, (y_hy, y_hg, y_gl), p['w_branch'], p['w_out'], h, gt)
    if need_ctx:
        yc_hy = _hyena(zcb, Lc, *hy)
        hc = _merge(zcb, Lc, (yc_hy, yc_hg, yc_gl), p['w_branch'], p['w_out'], hc, gtc)
    return h, hc


GATHER_UNROLL = 8
GATHER_AHEAD = 2
GATHER_SLOTS = GATHER_AHEAD + 1


def _ffn_kernel(blk_exp_ref, n_used_ref, tok_ref, x_hbm, wg_ref, wu_ref, wd_ref, o_ref, xbuf, sem, wg_s, wu_s, wd_s):
    i = pl.program_id(0)
    n_used = n_used_ref[0]
    rows = o_ref.shape[0]

    def issue(step, slot):
        def body(r, carry):
            src = tok_ref[step * rows + r]
            pltpu.make_async_copy(x_hbm.at[pl.ds(src, 1)], xbuf.at[slot, pl.ds(r, 1)], sem.at[slot]).start()
            return carry

        lax.fori_loop(0, rows, body, 0, unroll=GATHER_UNROLL)

    for s in range(GATHER_AHEAD):
        @pl.when(jnp.logical_and(i == 0, s < n_used))
        def _(s=s):
            issue(s, s % GATHER_SLOTS)

    @pl.when(i + GATHER_AHEAD < n_used)
    def _():
        issue(i + GATHER_AHEAD, (i + GATHER_AHEAD) % GATHER_SLOTS)

    new_expert = jnp.logical_or(i == 0, blk_exp_ref[i] != blk_exp_ref[jnp.maximum(i - 1, 0)])

    @pl.when(jnp.logical_and(i < n_used, new_expert))
    def _():
        wg_s[...] = wg_ref[0].astype(jnp.bfloat16)
        wu_s[...] = wu_ref[0].astype(jnp.bfloat16)
        wd_s[...] = wd_ref[0].astype(jnp.bfloat16)

    @pl.when(i < n_used)
    def _():
        slot = i % GATHER_SLOTS
        pltpu.make_async_copy(xbuf.at[slot], xbuf.at[slot], sem.at[slot]).wait()
        x = xbuf[slot].astype(jnp.bfloat16)
        hg = jnp.dot(x, wg_s[...], preferred_element_type=jnp.float32)
        hu = jnp.dot(x, wu_s[...], preferred_element_type=jnp.float32)
        act = (hg * jax.nn.sigmoid(hg) * hu).astype(jnp.bfloat16)
        o_ref[...] = jnp.dot(act, wd_s[...], preferred_element_type=jnp.float32)

    @pl.when(i >= n_used)
    def _():
        o_ref[...] = jnp.zeros_like(o_ref)


def _grouped_ffn(x, buf_tok, blk_exp, n_used, layer, w_gate, w_up, w_down):
    p_len = buf_tok.shape[0]
    d = x.shape[1]
    n_blk = p_len // MOE_BLOCK
    grid_spec = pltpu.PrefetchScalarGridSpec(
        num_scalar_prefetch=3,
        grid=(n_blk,),
        in_specs=[pl.BlockSpec(memory_space=pl.ANY),
                  pl.BlockSpec((None, 1, d, D_FF), lambda i, be, nu, tk: (layer, be[i], 0, 0)),
                  pl.BlockSpec((None, 1, d, D_FF), lambda i, be, nu, tk: (layer, be[i], 0, 0)),
                  pl.BlockSpec((None, 1, D_FF, d), lambda i, be, nu, tk: (layer, be[i], 0, 0))],
        out_specs=pl.BlockSpec((MOE_BLOCK, d), lambda i, be, nu, tk: (i, 0)),
        scratch_shapes=[pltpu.VMEM((GATHER_SLOTS, MOE_BLOCK, d), jnp.float32), pltpu.SemaphoreType.DMA((GATHER_SLOTS,)),
                        pltpu.VMEM((d, D_FF), jnp.bfloat16), pltpu.VMEM((d, D_FF), jnp.bfloat16),
                        pltpu.VMEM((D_FF, d), jnp.bfloat16)],
    )
    return pl.pallas_call(
        _ffn_kernel,
        grid_spec=grid_spec,
        out_shape=jax.ShapeDtypeStruct((p_len, d), jnp.float32),
        compiler_params=_cparams("arbitrary"),
        name="moe_grouped_ffn",
    )(blk_exp, n_used, buf_tok, x, w_gate, w_up, w_down)


ROUTER_COLS = 128


ROUTE_TOKENS = 256
ROUTE_OUT = (0, 1, 2, 3, 4, 5)


def _route_kernel(lg_ref, tril_ref, o_ref, cnt_ref, carry):
    i = pl.program_id(0)

    @pl.when(i == 0)
    def _():
        carry[...] = jnp.zeros_like(carry)

    x = lg_ref[...]
    lane = lax.broadcasted_iota(jnp.int32, x.shape, 1).astype(jnp.float32)
    neg = jnp.float32(-jnp.inf)
    far = jnp.float32(ROUTER_COLS)
    red_max = lambda a: jnp.max(a, axis=1, keepdims=True)
    red_min = lambda a: jnp.min(a, axis=1, keepdims=True)
    red_sum = lambda a: jnp.sum(a, axis=1, keepdims=True)
    gmask = lane < N_GROUPS
    gl = jnp.where(gmask, x, neg)
    gmax = red_max(gl)
    p_top = 1.0 / red_sum(jnp.where(gmask, jnp.exp(x - gmax), 0.0))
    grp = red_min(jnp.where(gl == gmax, lane, far))
    lo = N_GROUPS + EXP_PER_GROUP * grp
    emask = jnp.logical_and(lane >= lo, lane < lo + EXP_PER_GROUP)
    el = jnp.where(emask, x, neg)
    ee = jnp.where(emask, jnp.exp(x - red_max(el)), 0.0)
    prob = ee / red_sum(ee)
    p1 = red_max(prob)
    i1 = red_min(jnp.where(jnp.logical_and(emask, prob == p1), lane, far))
    rest = jnp.where(jnp.logical_and(emask, lane != i1), prob, -1.0)
    p2 = red_max(rest)
    i2 = red_min(jnp.where(rest == p2, lane, far))
    w1 = p_top * p1 / (p1 + p2)
    w2 = p_top * p2 / (p1 + p2)
    pick1 = (lane == i1).astype(jnp.float32)
    pick2 = (lane == i2).astype(jnp.float32)
    picks = pick1 + pick2
    before = jnp.dot(tril_ref[...], picks.astype(jnp.bfloat16), preferred_element_type=jnp.float32) + carry[...]
    r1 = red_sum(pick1 * before)
    r2 = red_sum(pick2 * before)
    carry[...] += jnp.sum(picks, axis=0, keepdims=True)
    cnt_ref[...] = carry[...]
    out = jnp.zeros_like(x)
    for col, val in zip(ROUTE_OUT, (i1 - N_GROUPS, i2 - N_GROUPS, w1, w2, r1, r2)):
        out = jnp.where(lane == col, val, out)
    o_ref[...] = out


def _route(logits):
    assert TOP_K == 2
    n = logits.shape[0]
    t = min(ROUTE_TOKENS, n)
    tril = jnp.asarray(np.tril(np.ones((t, t), np.float32), -1), jnp.bfloat16)
    out, cnt = pl.pallas_call(
        _route_kernel,
        grid=(n // t,),
        in_specs=[pl.BlockSpec((t, ROUTER_COLS), lambda i: (i, 0)), pl.BlockSpec((t, t), lambda i: (0, 0))],
        out_specs=[pl.BlockSpec((t, ROUTER_COLS), lambda i: (i, 0)), pl.BlockSpec((1, ROUTER_COLS), lambda i: (0, 0))],
        out_shape=[jax.ShapeDtypeStruct((n, ROUTER_COLS), jnp.float32),
                   jax.ShapeDtypeStruct((1, ROUTER_COLS), jnp.float32)],
        scratch_shapes=[pltpu.VMEM((1, ROUTER_COLS), jnp.float32)],
        compiler_params=_cparams("arbitrary"),
        name="moe_route",
    )(logits, tril)
    expert = out[:, 0:2].astype(jnp.int32)
    weight = out[:, 2:4]
    rank = out[:, 4:6].astype(jnp.int32)
    counts = cnt[0, N_GROUPS:N_GROUPS + N_EXPERTS].astype(jnp.int32)
    return expert, weight, rank, counts


def _hier_moe(h, norm_w, shift, scale, gt, p):
    n, d = h.shape
    pad = ROUTER_COLS - N_GROUPS - N_EXPERTS
    w_r = _bf(jnp.concatenate([p['w_rg'], p['w_re'], jnp.zeros((d, pad), jnp.float32)], axis=1))
    b_r = jnp.concatenate([p['b_rg'], p['b_re'], jnp.zeros((pad,), jnp.float32)]).reshape(1, ROUTER_COLS)
    xb, logits = _norm_mod(h, norm_w, shift, scale, router=(w_r, b_r), out_dtype=jnp.float32)
    expert, weight, rank, counts = _route(logits)
    a = n * TOP_K
    padded = (counts + MOE_BLOCK - 1) // MOE_BLOCK * MOE_BLOCK
    pad_end = jnp.cumsum(padded)
    pad_off = pad_end - padded
    pos = (pad_off[expert] + rank).reshape(a)
    p_len = (a + N_EXPERTS * MOE_BLOCK + MOE_BLOCK - 1) // MOE_BLOCK * MOE_BLOCK
    n_blk = p_len // MOE_BLOCK
    tok_flat = jnp.arange(a, dtype=jnp.int32) // TOP_K
    buf_tok = (jnp.arange(p_len, dtype=jnp.int32) % n).at[pos].set(tok_flat)
    blk_start = jnp.arange(n_blk, dtype=jnp.int32) * MOE_BLOCK
    blk_exp = jnp.minimum(jnp.sum(pad_end[None, :] <= blk_start[:, None], axis=1), N_EXPERTS - 1).astype(jnp.int32)
    n_used = (pad_end[-1:] // MOE_BLOCK).astype(jnp.int32)
    y = _grouped_ffn(xb, buf_tok, blk_exp, n_used, p['layer'], p['w_gate'], p['w_up'], p['w_down'])
    return _moe_combine(y, pos, weight, h, gt)


def _combine_kernel(pos_ref, y_hbm, wts_ref, h_ref, gt_ref, o_ref, buf, sem):
    i = pl.program_id(0)
    tokens = h_ref.shape[0]

    def row_copy(step, slot, r, k):
        src = pos_ref[(step * tokens + r) * TOP_K + k]
        return pltpu.make_async_copy(y_hbm.at[pl.ds(src, 1)], buf.at[slot, k, pl.ds(r, 1)], sem.at[slot])

    def issue(step, slot):
        def body(r, carry):
            for k in range(TOP_K):
                row_copy(step, slot, r, k).start()
            return carry

        lax.fori_loop(0, tokens, body, 0, unroll=GATHER_UNROLL // TOP_K)

    @pl.when(i == 0)
    def _():
        issue(0, 0)

    @pl.when(i + 1 < pl.num_programs(0))
    def _():
        issue(i + 1, (i + 1) % 2)

    slot = i % 2
    pltpu.make_async_copy(buf.at[slot], buf.at[slot], sem.at[slot]).wait()
    rows = buf[slot]
    wts = wts_ref[...]
    acc = rows[0] * wts[:, 0:1]
    for k in range(1, TOP_K):
        acc = acc + rows[k] * wts[:, k:k + 1]
    o_ref[...] = h_ref[...] + gt_ref[...] * acc


COMBINE_TOKENS = 128


def _moe_combine(y, pos, wts, h, gt):
    n, d = h.shape
    tokens = min(COMBINE_TOKENS, n)
    grid_spec = pltpu.PrefetchScalarGridSpec(
        num_scalar_prefetch=1,
        grid=(n // tokens,),
        in_specs=[pl.BlockSpec(memory_space=pl.ANY),
                  pl.BlockSpec((tokens, TOP_K), lambda i, pos: (i, 0)),
                  pl.BlockSpec((tokens, d), lambda i, pos: (i, 0)),
                  pl.BlockSpec((1, d), lambda i, pos: (0, 0))],
        out_specs=pl.BlockSpec((tokens, d), lambda i, pos: (i, 0)),
        scratch_shapes=[pltpu.VMEM((2, TOP_K, tokens, d), jnp.float32), pltpu.SemaphoreType.DMA((2,))],
    )
    return pl.pallas_call(
        _combine_kernel,
        grid_spec=grid_spec,
        out_shape=jax.ShapeDtypeStruct((n, d), jnp.float32),
        compiler_params=_cparams("arbitrary"),
        name="moe_combine",
    )(pos, y, wts, h, gt)


def _final_norm_kernel(x_ref, w_ref, o_ref):
    x = x_ref[...]
    o_ref[...] = x * lax.rsqrt(jnp.mean(x * x, axis=-1, keepdims=True) + NORM_EPS) * w_ref[...]


def _final_norm(x, w, tm=512):
    m, d = x.shape
    return pl.pallas_call(
        _final_norm_kernel,
        grid=(m // tm,),
        in_specs=[pl.BlockSpec((tm, d), lambda i: (i, 0)), pl.BlockSpec((1, d), lambda i: (0, 0))],
        out_specs=pl.BlockSpec((tm, d), lambda i: (i, 0)),
        out_shape=jax.ShapeDtypeStruct((m, d), jnp.float32),
        compiler_params=_cparams("parallel"),
        name="final_rmsnorm",
    )(x, w.reshape(1, d))


def kernel(x, c, ctx, c_ctx, w_mod, b_mod, norm_mix_w, norm_ffn_w, w_in, b_in, hy_conv_w, hy_conv_b, hy_f_w1, hy_f_b1, hy_f_w2, hy_f_b2, hy_f_w3, hy_f_b3, hy_f_freq, hy_skip, hg_lb_raw, hg_norm_w, gl_w_a2, gl_b_a, gl_norm_w, w_branch, w_out, w_rg, b_rg, w_re, b_re, w_gate, w_up, w_down, final_norm_w):
    assert x.shape[0] == 1
    depth = w_mod.shape[0]
    lb_all = jnp.cumsum(jax.nn.softmax(hg_lb_raw, axis=0), axis=0)
    lb_all = lb_all - lb_all[:1]
    h, hc = x[0], ctx[0]
    cc = jnp.concatenate([c, c_ctx[None, :]], axis=0)
    for l in range(depth):
        need_ctx = l < depth - 1
        mod = _matmul(_bf(jax.nn.silu(cc)), w_mod, b_mod[l], layer=l)
        sh1, sc1, gt1, sh2, sc2, gt2 = jnp.split(mod[0:1], 6, axis=-1)
        sh1c, sc1c, gt1c, sh2c, sc2c, gt2c = jnp.split(mod[1:2], 6, axis=-1)
        p = dict(w_in=w_in, b_in=b_in[l], hy_conv_w=hy_conv_w[l], hy_conv_b=hy_conv_b[l],
                 hy_f=(hy_f_w1[l], hy_f_b1[l], hy_f_w2[l], hy_f_b2[l], hy_f_w3[l], hy_f_b3[l], hy_f_freq[l]),
                 hy_skip=hy_skip[l], lb=lb_all[l], hg_norm_w=hg_norm_w[l], gl_w_a2=gl_w_a2[l], gl_b_a=gl_b_a[l],
                 gl_norm_w=gl_norm_w[l], w_branch=w_branch[l], w_out=w_out[l], w_rg=w_rg[l], b_rg=b_rg[l],
                 w_re=w_re[l], b_re=b_re[l], layer=l, w_gate=w_gate, w_up=w_up, w_down=w_down)
        u = _norm_mod(h, norm_mix_w[l], sh1, sc1)
        uc = _norm_mod(hc, norm_mix_w[l], sh1c, sc1c)
        h, hc = _mixer(h, hc, u, uc, gt1, gt1c, p, need_ctx)
        h = _hier_moe(h, norm_ffn_w[l], sh2, sc2, gt2, p)
        if need_ctx:
            hc = _hier_moe(hc, norm_ffn_w[l], sh2c, sc2c, gt2c, p)
    return _final_norm(h, final_norm_w)[None]
```

```python
import functools
import math

import jax
import jax.numpy as jnp
import numpy as np
from jax import lax
from jax.experimental import pallas as pl
from jax.experimental.pallas import tpu as pltpu

D_MODEL = 2048
NORM_EPS = 1e-6

HY_C = D_MODEL // 2
HY_EMB = 33
HY_BANDS = (HY_EMB - 1) // 2
HY_DECAY_TARGET = 1e-2
HY_FAST_PCT = 0.3
HY_SLOW_PCT = 1.5
HY_MOD_SHIFT = 0.05

HG_HEADS = 8
HG_DK = 128
HG_DV = 128
HG_K = HG_HEADS * HG_DK
HG_V = HG_HEADS * HG_DV

GL_HEADS = 4
GL_DK = 128
GL_DV = 256
GL_K = GL_HEADS * GL_DK
GL_V = GL_HEADS * GL_DV
GL_RANK = 16
GL_TAU = 16.0

N_BRANCH = 3
HG_COLS = 3 * HG_K + 2 * HG_V
GL_COLS = 2 * GL_K + 2 * GL_V + 2 * GL_RANK
REC_COLS = HG_COLS + GL_COLS
HY_COLS = 3 * HY_C
MERGE_COLS = N_BRANCH * D_MODEL

COL_TILE = 512
GL_Q_OFF = HG_COLS
GL_V_OFF = GL_Q_OFF + 2 * GL_K
GL_A_OFF = GL_V_OFF + 2 * GL_V
HY_OFF = -(-(GL_A_OFF + 2 * GL_RANK) // HY_C) * HY_C
MG_OFF = HY_OFF + HY_COLS
Z_COLS = MG_OFF + MERGE_COLS
assert GL_Q_OFF % GL_K == 0 and GL_V_OFF % GL_V == 0 and GL_A_OFF % 128 == 0 and Z_COLS % COL_TILE == 0
assert MG_OFF % D_MODEL == 0
ZB_MG_OFF = 0
ZB_HY_OFF = MERGE_COLS

N_GROUPS = 4
EXP_PER_GROUP = 8
N_EXPERTS = N_GROUPS * EXP_PER_GROUP
TOP_K = 2
D_FF = D_MODEL // 4
MOE_BLOCK = 256

SCAN_CHUNK = 128
SCAN_HEAD_GROUP = 8
SCAN_CHUNKS_PER_STEP = 4
LOG2_E = 1.4426950408889634

VMEM_LIMIT_BYTES = 56 * 1024 * 1024


def _cparams(*sem):
    return pltpu.CompilerParams(dimension_semantics=sem, vmem_limit_bytes=VMEM_LIMIT_BYTES)


def _bf(a):
    return a.astype(jnp.bfloat16)


def _mm_kernel(x_ref, w_ref, b_ref, o_ref):
    acc = jnp.dot(x_ref[...], w_ref[...].astype(jnp.bfloat16), preferred_element_type=jnp.float32) + b_ref[...]
    o_ref[...] = acc.astype(o_ref.dtype)


def _matmul(x, w, bias=None, tm=512, tn=COL_TILE, out_dtype=jnp.float32, n_cols=None, layer=None, col_off=0,
            out_map=None):
    m, k = x.shape
    n = w.shape[-1] if n_cols is None else n_cols
    assert n_cols is None or n_cols % tn == 0
    assert layer is None or n % tn == 0
    tm = min(tm, -(-m // 8) * 8)
    mp = -(-m // tm) * tm
    np_ = -(-n // tn) * tn
    if bias is None:
        bias = jnp.zeros((n,), jnp.float32)
    if mp != m:
        x = jnp.pad(x, ((0, mp - m), (0, 0)))
    if np_ != n:
        w = jnp.pad(w, ((0, 0), (0, np_ - n)))
        bias = jnp.pad(bias, (0, np_ - n))
    if out_map is None:
        out_map = lambda j: j
    if layer is None:
        w_spec = pl.BlockSpec((k, tn), lambda i, j: (0, j + col_off))
    else:
        assert col_off == 0
        w_spec = pl.BlockSpec((None, k, tn), lambda i, j: (layer, 0, j))
    out = pl.pallas_call(
        _mm_kernel,
        grid=(mp // tm, np_ // tn),
        in_specs=[pl.BlockSpec((tm, k), lambda i, j: (i, 0)),
                  w_spec,
                  pl.BlockSpec((1, tn), lambda i, j: (0, j + col_off))],
        out_specs=pl.BlockSpec((tm, tn), lambda i, j: (i, out_map(j))),
        out_shape=jax.ShapeDtypeStruct((mp, np_), out_dtype),
        compiler_params=_cparams("parallel", "arbitrary"),
        name="dense_matmul",
    )(x, w, bias.reshape(1, -1))
    if mp != m or np_ != n:
        out = out[:m, :n]
    return out


def _norm_mod_kernel(*refs, with_router):
    if with_router:
        h_ref, w_ref, sh_ref, sc_ref, wr_ref, br_ref, o_ref, lg_ref = refs
    else:
        h_ref, w_ref, sh_ref, sc_ref, o_ref = refs
    x = h_ref[...]
    y = x * lax.rsqrt(jnp.mean(x * x, axis=-1, keepdims=True) + NORM_EPS) * w_ref[...]
    u = (y * (1.0 + sc_ref[...]) + sh_ref[...]).astype(jnp.bfloat16)
    o_ref[...] = u.astype(o_ref.dtype)
    if with_router:
        lg_ref[...] = jnp.dot(u, wr_ref[...], preferred_element_type=jnp.float32) + br_ref[...]


def _norm_mod(h, w, shift, scale, router=None, out_dtype=jnp.bfloat16):
    m, d = h.shape
    tm = min(512, m)
    row = pl.BlockSpec((tm, d), lambda i: (i, 0))
    vec = pl.BlockSpec((1, d), lambda i: (0, 0))
    args = [h, w.reshape(1, d), shift, scale]
    specs = [row, vec, vec, vec]
    out_shape = [jax.ShapeDtypeStruct((m, d), out_dtype)]
    out_specs = [row]
    if router is not None:
        args += list(router)
        specs += [pl.BlockSpec(router[0].shape, lambda i: (0, 0)), pl.BlockSpec(router[1].shape, lambda i: (0, 0))]
        out_shape.append(jax.ShapeDtypeStruct((m, router[0].shape[1]), jnp.float32))
        out_specs.append(pl.BlockSpec((tm, router[0].shape[1]), lambda i: (i, 0)))
    out = pl.pallas_call(
        functools.partial(_norm_mod_kernel, with_router=router is not None),
        grid=(m // tm,),
        in_specs=specs,
        out_specs=out_specs,
        out_shape=out_shape,
        compiler_params=_cparams("parallel"),
        name="norm_modulate",
    )(*args)
    return out if router is not None else out[0]


def _scan_masks(c, reverse):
    t = np.arange(c)
    ms = [np.eye(c, dtype=np.float32)]
    for lvl in range(int(math.log2(c))):
        upper = ((t >> lvl) & 1).astype(bool)
        same = (t[:, None] >> (lvl + 1)) == (t[None, :] >> (lvl + 1))
        m = same & upper[:, None] & (~upper)[None, :]
        ms.append((m.T if reverse else m).astype(np.float32))
    tri = t[None, :] >= t[:, None] if reverse else t[None, :] <= t[:, None]
    return jnp.asarray(np.stack(ms)), jnp.asarray(tri.astype(np.float32), dtype=jnp.bfloat16)


def _level_arg(cum, lvl, reverse):
    c = cum.shape[0]
    blk = 1 << lvl
    if blk >= 8:
        pieces = []
        for gs in range(0, c, 2 * blk):
            ref = cum[gs + blk:gs + blk + 1, :]
            pieces.append(ref - cum[gs:gs + blk, :])
            pieces.append(cum[gs + blk:gs + 2 * blk, :] - ref)
        arg = jnp.concatenate(pieces, axis=0)
    else:
        c3 = cum.reshape(c // 8, 8, cum.shape[1])
        sub = lax.broadcasted_iota(jnp.int32, c3.shape, 1)
        ref_row = ((sub >> lvl) | 1) << lvl
        ref = None
        for r in range(blk, 8, 2 * blk):
            cand = jnp.broadcast_to(c3[:, r:r + 1, :], c3.shape)
            ref = cand if ref is None else jnp.where(ref_row == r, cand, ref)
        upper = ((sub >> lvl) & 1) == 1
        arg = jnp.where(upper, c3 - ref, ref - c3).reshape(cum.shape)
    return -arg if reverse else arg


def _dot_nt(a, b):
    return lax.dot_general(a, b, (((1,), (1,)), ((), ())), preferred_element_type=jnp.float32)


def _dot_tn(a, b):
    return lax.dot_general(a, b, (((0,), (0,)), ((), ())), preferred_element_type=jnp.float32)


def _sigmoid_parts(z):
    e = jnp.exp(-jnp.abs(z))
    r = 1.0 / (1.0 + e)
    return jnp.minimum(z, 0.0) - jnp.log(1.0 + e), jnp.where(z >= 0.0, e * r, r)


def _scan_kernel(*refs, mode, reverse, final, heads, dk, dv, c, cps):
    it = iter(refs)
    q_ref = next(it)
    k_ref = next(it)
    v_ref = next(it)
    if mode == "hg":
        lbp_ref = next(it)
    else:
        a_ref = next(it)
        wa_ref = next(it)
        ba_ref = next(it)
    s0_ref = next(it)
    masks_ref = next(it)
    tri_ref = next(it)
    if final:
        oprev_ref = next(it)
        gate_ref = next(it)
        nw_ref = next(it)
    o_ref = next(it)
    st_ref = next(it)

    @pl.when(pl.program_id(0) == 0)
    def _():
        st_ref[...] = s0_ref[...]

    for ci in (reversed(range(cps)) if reverse else range(cps)):
        rs = slice(ci * c, (ci + 1) * c)
        if mode == "gl":
            la_all = jnp.dot(a_ref[rs, :].astype(jnp.bfloat16), wa_ref[...],
                             preferred_element_type=jnp.float32) + ba_ref[...]
        tri = tri_ref[...]
        tot_row = 0 if reverse else c - 1
        n_lvl = int(math.log2(c))
        ksl = lambda h: slice(h * dk, (h + 1) * dk)
        vsl = lambda h: slice(h * dv, (h + 1) * dv)
        for h0 in range(0, heads, SCAN_HEAD_GROUP):
            group = range(h0, min(h0 + SCAN_HEAD_GROUP, heads))
            q_, k_, cum_, qb_, kb_, o_, sc_ = {}, {}, {}, {}, {}, {}, {}
            for h in group:
                q = q_ref[rs, ksl(h)]
                if mode == "hg":
                    log_sig, sig_neg = _sigmoid_parts(k_ref[rs, ksl(h)])
                    la = lbp_ref[0:1, ksl(h)]
                    lbb = lbp_ref[1:2, ksl(h)] + log_sig
                    g = jnp.maximum(la, lbb) + jnp.log(1.0 + jnp.exp(-jnp.abs(la - lbb)))
                    k = lbp_ref[2:3, ksl(h)] * sig_neg
                    q = q * jax.nn.sigmoid(q)
                else:
                    g = _sigmoid_parts(la_all[:, ksl(h)])[0] * (1.0 / GL_TAU)
                    k = k_ref[rs, ksl(h)]
                    q = q * (dk ** -0.5)
                g = g * LOG2_E
                g1 = g.astype(jnp.bfloat16)
                r1 = g - g1.astype(jnp.float32)
                g2 = r1.astype(jnp.bfloat16)
                g3 = (r1 - g2.astype(jnp.float32)).astype(jnp.bfloat16)
                cum_[h] = (jnp.dot(tri, g1, preferred_element_type=jnp.float32)
                           + jnp.dot(tri, g2, preferred_element_type=jnp.float32)
                           + jnp.dot(tri, g3, preferred_element_type=jnp.float32))
                q_[h], k_[h] = q, k
            for h in group:
                cum = cum_[h]
                tot = cum[tot_row:tot_row + 1, :]
                st = st_ref[h]
                v = v_ref[rs, vsl(h)].astype(jnp.bfloat16)
                o_[h] = _dot_nt((q_[h] * jnp.exp2(cum)).astype(jnp.bfloat16), st.astype(jnp.bfloat16))
                kt = (k_[h] * jnp.exp2(tot - cum)).astype(jnp.bfloat16)
                st_ref[h] = st * jnp.exp2(tot) + _dot_tn(v, kt)
                qb_[h] = q_[h].astype(jnp.bfloat16)
                kb_[h] = k_[h].astype(jnp.bfloat16)
                sc_[h] = masks_ref[0] * _dot_nt(qb_[h], kb_[h])
            for lvl in range(n_lvl):
                for h in group:
                    e = jnp.exp2(_level_arg(cum_[h], lvl, reverse)).astype(jnp.bfloat16)
                    sc_[h] = sc_[h] + masks_ref[1 + lvl] * _dot_nt(qb_[h] * e, kb_[h] * e)
            for h in group:
                v = v_ref[rs, vsl(h)].astype(jnp.bfloat16)
                o = o_[h] + jnp.dot(sc_[h].astype(jnp.bfloat16), v, preferred_element_type=jnp.float32)
                if final:
                    o = o + oprev_ref[rs, vsl(h)]
                    y = o * lax.rsqrt(jnp.mean(o * o, axis=-1, keepdims=True) + NORM_EPS) * nw_ref[...]
                    gt = gate_ref[rs, vsl(h)]
                    act = jax.nn.sigmoid(gt) if mode == "hg" else gt * jax.nn.sigmoid(gt)
                    o_ref[rs, vsl(h)] = (y * act).astype(o_ref.dtype)
                else:
                    o_ref[rs, vsl(h)] = o


def _scan_pass(mode, reverse, final, L, srcs, s0, params, final_srcs=(), norm_w=None):
    heads, dk, dv = (HG_HEADS, HG_DK, HG_DV) if mode == "hg" else (GL_HEADS, GL_DK, GL_DV)
    c = min(SCAN_CHUNK, L)
    cps = min(SCAN_CHUNKS_PER_STEP, L // c)
    rows = c * cps
    nb = L // rows
    row = (lambda i: nb - 1 - i) if reverse else (lambda i: i)
    masks, tri = _scan_masks(c, reverse)

    def const(shape):
        return pl.BlockSpec(shape, lambda i: (0,) * len(shape))

    def rowblock(width, cb):
        return pl.BlockSpec((rows, width), lambda i: (row(i), cb))

    args = [a for a, _, _ in srcs] + list(params) + [s0, masks, tri]
    specs = ([rowblock(w, cb) for _, w, cb in srcs] + [const(p.shape) for p in params]
             + [const(s0.shape), const(masks.shape), const(tri.shape)])
    if final:
        args += [a for a, _, _ in final_srcs] + [norm_w]
        specs += [rowblock(w, cb) for _, w, cb in final_srcs] + [const(norm_w.shape)]
    return pl.pallas_call(
        functools.partial(_scan_kernel, mode=mode, reverse=reverse, final=final, heads=heads, dk=dk, dv=dv, c=c,
                          cps=cps),
        grid=(nb,),
        in_specs=specs,
        out_specs=[pl.BlockSpec((rows, heads * dv), lambda i: (row(i), 0)), const((heads, dv, dk))],
        out_shape=[jax.ShapeDtypeStruct((L, heads * dv), jnp.bfloat16 if final else jnp.float32),
                   jax.ShapeDtypeStruct((heads, dv, dk), jnp.float32)],
        compiler_params=_cparams("arbitrary"),
        name=f"scan_{mode}_{'bwd' if reverse else 'fwd'}",
    )(*args)


def _hgrn2(z, L, lb, norm_w, s0_f, s0_b):
    lbp = lambda d: jnp.stack([jnp.log(lb[d]), jnp.log1p(-lb[d]), 1.0 - lb[d]])
    w = HG_K
    o_b, s_b = _scan_pass("hg", True, False, L, [(z, w, 0), (z, w, 2), (z, w, 3)], s0_b, [lbp(1)])
    y, s_f = _scan_pass("hg", False, True, L, [(z, w, 0), (z, w, 1), (z, w, 3)], s0_f, [lbp(0)],
                        final_srcs=[(o_b, HG_V, 0), (z, HG_V, 4)], norm_w=norm_w.reshape(1, HG_DV))
    return y, s_f, s_b


def _gla(z, L, w_a2, b_a, norm_w, s0_f, s0_b):
    def gate_params(d):
        wa = jnp.zeros((128, GL_K), jnp.float32).at[d * GL_RANK:(d + 1) * GL_RANK].set(w_a2[d])
        return [_bf(wa), b_a[d].reshape(1, GL_K)]

    srcs = [(z, GL_K, GL_Q_OFF // GL_K), (z, GL_K, GL_Q_OFF // GL_K + 1), (z, GL_V, GL_V_OFF // GL_V),
            (z, 128, GL_A_OFF // 128)]
    o_b, s_b = _scan_pass("gl", True, False, L, srcs, s0_b, gate_params(1))
    y, s_f = _scan_pass("gl", False, True, L, srcs, s0_f, gate_params(0),
                        final_srcs=[(o_b, GL_V, 0), (z, GL_V, GL_V_OFF // GL_V + 1)], norm_w=norm_w.reshape(1, GL_DV))
    return y, s_f, s_b


HY_FEAT_PAD = 128


def _hy_filter_kernel(f_ref, w1_ref, b1_ref, fq_ref, w2_ref, b2_ref, w3_ref, b3_ref, dl_ref, h_ref, s_ref):
    i = pl.program_id(0)
    f = f_ref[...]
    fq = fq_ref[...]
    a = jnp.sin(fq * (jnp.dot(f.astype(jnp.bfloat16), w1_ref[...], preferred_element_type=jnp.float32) + b1_ref[...]))
    a = jnp.sin(fq * (jnp.dot(a.astype(jnp.bfloat16), w2_ref[...], preferred_element_type=jnp.float32) + b2_ref[...]))
    hh = jnp.dot(a.astype(jnp.bfloat16), w3_ref[...], preferred_element_type=jnp.float32) + b3_ref[...]
    hh = hh * (jnp.exp(-f[:, 0:1] * dl_ref[...]) + HY_MOD_SHIFT)
    h_ref[...] = hh
    part = jnp.sum(jnp.abs(hh).reshape(hh.shape[0] // 8, 8, hh.shape[1]), axis=0)

    @pl.when(i == 0)
    def _():
        s_ref[...] = part

    @pl.when(i > 0)
    def _():
        s_ref[...] += part


def _hyena_filters(L, w1, b1, w2, b2, w3, b3, freq):
    t = jnp.linspace(0.0, 1.0, L, dtype=jnp.float32)[:, None]
    ang = 2.0 * math.pi * jnp.arange(L, dtype=jnp.float32)[:, None] / L
    bands = jnp.linspace(1e-4, HY_BANDS - 1, HY_BANDS, dtype=jnp.float32)[None, :]
    feats = jnp.concatenate([t, jnp.cos(bands * ang), -jnp.sin(bands * ang),
                             jnp.zeros((L, HY_FEAT_PAD - HY_EMB), jnp.float32)], axis=-1)
    deltas = jnp.abs(jnp.linspace(math.log(HY_DECAY_TARGET) / HY_SLOW_PCT, math.log(HY_DECAY_TARGET) / HY_FAST_PCT,
                                  HY_C, dtype=jnp.float32))
    fh = w1.shape[1]
    padm = lambda a, r, c: _bf(jnp.pad(a, ((0, r - a.shape[0]), (0, c - a.shape[1]))))
    padv = lambda a: jnp.pad(a, (0, HY_FEAT_PAD - a.shape[0])).reshape(1, HY_FEAT_PAD)
    tm = min(512, L)
    const = lambda r, c: pl.BlockSpec((r, c), lambda i: (0, 0))
    hfil, sums = pl.pallas_call(
        _hy_filter_kernel,
        grid=(L // tm,),
        in_specs=[pl.BlockSpec((tm, HY_FEAT_PAD), lambda i: (i, 0)),
                  const(HY_FEAT_PAD, HY_FEAT_PAD), const(1, HY_FEAT_PAD), const(1, HY_FEAT_PAD),
                  const(HY_FEAT_PAD, HY_FEAT_PAD), const(1, HY_FEAT_PAD),
                  const(HY_FEAT_PAD, 2 * HY_C), const(1, 2 * HY_C), const(1, 2 * HY_C)],
        out_specs=[pl.BlockSpec((tm, 2 * HY_C), lambda i: (i, 0)), const(8, 2 * HY_C)],
        out_shape=[jax.ShapeDtypeStruct((L, 2 * HY_C), jnp.float32), jax.ShapeDtypeStruct((8, 2 * HY_C), jnp.float32)],
        compiler_params=_cparams("arbitrary"),
        name="hyena_filters",
    )(feats, padm(w1, HY_FEAT_PAD, HY_FEAT_PAD), padv(b1), padv(freq), padm(w2, HY_FEAT_PAD, HY_FEAT_PAD), padv(b2),
      padm(w3, HY_FEAT_PAD, 2 * HY_C), b3.reshape(1, 2 * HY_C), jnp.tile(deltas, 2).reshape(1, 2 * HY_C))
    assert fh <= HY_FEAT_PAD
    inorm = 1.0 / jnp.sum(sums, axis=0)
    return hfil, inorm.reshape(2, HY_C)


def _hy_pre_kernel(x0_ref, x1_ref, v_ref, x0p_ref, x1p_ref, vp_ref, x0n_ref, x1n_ref, vn_ref, w_ref, b_ref,
                   vo_ref, x0o_ref):
    i = pl.program_id(0)
    first = i == 0
    last = i == pl.num_programs(0) - 1
    tm = x0_ref.shape[0]
    row = lax.broadcasted_iota(jnp.int32, x0_ref.shape, 0)

    def conv(x_ref, p_ref, n_ref, g):
        x = x_ref[...].astype(jnp.float32)
        cs = slice(g * HY_C, (g + 1) * HY_C)
        prev_row = jnp.where(first, 0.0, p_ref[...].astype(jnp.float32)[HALO_ROWS - 1:HALO_ROWS, :])
        next_row = jnp.where(last, 0.0, n_ref[...].astype(jnp.float32)[0:1, :])
        xp = jnp.where(row == 0, prev_row, pltpu.roll(x, 1, 0))
        xn = jnp.where(row == tm - 1, next_row, pltpu.roll(x, tm - 1, 0))
        return w_ref[0:1, cs] * xp + w_ref[1:2, cs] * x + w_ref[2:3, cs] * xn + b_ref[0:1, cs]

    x0 = conv(x0_ref, x0p_ref, x0n_ref, 0)
    x1 = conv(x1_ref, x1p_ref, x1n_ref, 1)
    v = conv(v_ref, vp_ref, vn_ref, 2)
    vo_ref[...] = (v * x1).astype(vo_ref.dtype)
    x0o_ref[...] = x0.astype(x0o_ref.dtype)


HALO_ROWS = 16


def _hy_pre(z, L, conv_w, conv_b):
    tm = min(256, L)
    nbh = L // HALO_ROWS
    cb = ZB_HY_OFF // HY_C
    main = lambda g: pl.BlockSpec((tm, HY_C), lambda i: (i, cb + g))
    prev = lambda g: pl.BlockSpec((HALO_ROWS, HY_C), lambda i: (jnp.maximum(i * (tm // HALO_ROWS) - 1, 0), cb + g))
    nxt = lambda g: pl.BlockSpec((HALO_ROWS, HY_C),
                                 lambda i: (jnp.minimum((i + 1) * (tm // HALO_ROWS), nbh - 1), cb + g))
    const = lambda a: pl.BlockSpec(a.shape, lambda i: (0, 0))
    cbias = conv_b.reshape(1, HY_COLS)
    return pl.pallas_call(
        _hy_pre_kernel,
        grid=(L // tm,),
        in_specs=[main(0), main(1), main(2), prev(0), prev(1), prev(2), nxt(0), nxt(1), nxt(2),
                  const(conv_w), const(cbias)],
        out_specs=[pl.BlockSpec((tm, HY_C), lambda i: (i, 0))] * 2,
        out_shape=[jax.ShapeDtypeStruct((L, HY_C), jnp.float32), jax.ShapeDtypeStruct((L, HY_C), jnp.bfloat16)],
        compiler_params=_cparams("parallel"),
        name="hyena_short_conv",
    )(z, z, z, z, z, z, z, z, z, conv_w, cbias)


HY_N1 = 128
HY_TWO_STAGE_MIN_L = 1024


def _dft_outer_table(n1, cols):
    ang = -2.0 * np.pi * np.outer(np.arange(n1 // 2) + 0.5, np.arange(cols)) / n1
    return jnp.asarray(np.concatenate([np.cos(ang), np.sin(ang)], axis=0), jnp.bfloat16)


def _dft_inner_table(n1, n2):
    j2 = np.arange(n2)
    f_ang = -2.0 * np.pi * np.outer(np.arange(n2), j2) / n2
    tw_ang = -2.0 * np.pi * np.outer(np.arange(n1 // 2) + 0.5, j2) / (n1 * n2)
    fr, fi = jnp.asarray(np.cos(f_ang), jnp.float32), jnp.asarray(np.sin(f_ang), jnp.float32)
    twr, twi = jnp.asarray(np.cos(tw_ang), jnp.float32), jnp.asarray(np.sin(tw_ang), jnp.float32)
    mr = fr[None] * twr[:, None, :] - fi[None] * twi[:, None, :]
    mi = fr[None] * twi[:, None, :] + fi[None] * twr[:, None, :]
    return _bf(jnp.concatenate([jnp.concatenate([mr, -mi], axis=2), jnp.concatenate([mi, mr], axis=2)], axis=1))


def _spectral_product(xv, xh, inorm, half):
    inf, inb = inorm[0:1, :], inorm[1:2, :]
    gr = xh[:half, :HY_C] * inf + xh[:half, HY_C:] * inb
    gi = xh[half:, :HY_C] * inf - xh[half:, HY_C:] * inb
    xr, xi = xv[:half], xv[half:]
    return jnp.concatenate([xr * gr - xi * gi, xr * gi + xi * gr], axis=0).astype(jnp.bfloat16)


HY_SLABS = 2


def _hy_spec_kernel(r_ref, avr_ref, avi_ref, ahr_ref, ahi_ref, inorm_ref, br_ref, bi_ref):
    n2 = avr_ref.shape[1]
    slabs = range(r_ref.shape[0])
    xv = [jnp.dot(r_ref[s], jnp.concatenate([avr_ref[s], avi_ref[s]], axis=0), preferred_element_type=jnp.float32)
          for s in slabs]
    xh = [jnp.dot(r_ref[s], jnp.concatenate([ahr_ref[s], ahi_ref[s]], axis=0), preferred_element_type=jnp.float32)
          for s in slabs]
    y = [_spectral_product(xv[s], xh[s], inorm_ref[...], n2) for s in slabs]
    for s in slabs:
        b = _dot_tn(r_ref[s], y[s])
        br_ref[s] = b[:n2].astype(br_ref.dtype)
        bi_ref[s] = b[n2:].astype(bi_ref.dtype)


def _hy_spec(r, av, ah, inorm, n1, n2):
    av3 = av.reshape(n1, n2, HY_C)
    ah3 = ah.reshape(n1, n2, 2 * HY_C)
    h1 = n1 // 2
    sl = HY_SLABS
    assert h1 % sl == 0
    out = jax.ShapeDtypeStruct((h1, n2, HY_C), jnp.bfloat16)
    return pl.pallas_call(
        _hy_spec_kernel,
        grid=(h1 // sl,),
        in_specs=[pl.BlockSpec((sl, 2 * n2, 2 * n2), lambda k: (k, 0, 0)),
                  pl.BlockSpec((sl, n2, HY_C), lambda k: (k, 0, 0)),
                  pl.BlockSpec((sl, n2, HY_C), lambda k: (k + h1 // sl, 0, 0)),
                  pl.BlockSpec((sl, n2, 2 * HY_C), lambda k: (k, 0, 0)),
                  pl.BlockSpec((sl, n2, 2 * HY_C), lambda k: (k + h1 // sl, 0, 0)),
                  pl.BlockSpec((2, HY_C), lambda k: (0, 0))],
        out_specs=[pl.BlockSpec((sl, n2, HY_C), lambda k: (k, 0, 0))] * 2,
        out_shape=[out, out],
        compiler_params=_cparams("parallel"),
        name="hyena_spectral",
    )(r, av3, av3, ah3, ah3, inorm)


def _hy_spec_direct_kernel(xv_ref, xh_ref, inorm_ref, yr_ref, yi_ref):
    half = yr_ref.shape[0]
    y = _spectral_product(xv_ref[...].astype(jnp.float32), xh_ref[...].astype(jnp.float32), inorm_ref[...], half)
    yr_ref[...] = y[:half]
    yi_ref[...] = y[half:]


def _hy_spec_direct(xv, xh, inorm, L):
    full = lambda a: pl.BlockSpec(a.shape, lambda i: (0, 0))
    out = jax.ShapeDtypeStruct((L, HY_C), jnp.bfloat16)
    return pl.pallas_call(
        _hy_spec_direct_kernel,
        grid=(1,),
        in_specs=[full(xv), full(xh), full(inorm)],
        out_specs=[pl.BlockSpec((L, HY_C), lambda i: (0, 0))] * 2,
        out_shape=[out, out],
        compiler_params=_cparams("arbitrary"),
        name="hyena_spectral_direct",
    )(xv, xh, inorm)


def _hy_post_kernel(tr_ref, ti_ref, br_ref, bi_ref, v_ref, x0_ref, skip_ref, o_ref, *, scale):
    acc = (jnp.dot(tr_ref[...], br_ref[...], preferred_element_type=jnp.float32)
           + jnp.dot(ti_ref[...], bi_ref[...], preferred_element_type=jnp.float32))
    y = (acc * scale + v_ref[...].astype(jnp.float32) * skip_ref[...]) * x0_ref[...].astype(jnp.float32)
    o_ref[...] = y.astype(o_ref.dtype)


def _hy_post(t_fwd, b_r, b_i, v, x0, skip, L, h1, n2):
    ncol = n2 * HY_C
    tn = min(4096, ncol)
    tr_t = t_fwd[:h1].T
    ti_t = t_fwd[h1:].T
    skip_t = jnp.tile(skip, tn // HY_C).reshape(1, tn)
    col = lambda rows: pl.BlockSpec((rows, tn), lambda j: (0, j))
    rows_out = tr_t.shape[0]
    y = pl.pallas_call(
        functools.partial(_hy_post_kernel, scale=1.0 / L),
        grid=(ncol // tn,),
        in_specs=[pl.BlockSpec(tr_t.shape, lambda j: (0, 0)), pl.BlockSpec(ti_t.shape, lambda j: (0, 0)),
                  col(h1), col(h1), col(rows_out), col(rows_out), pl.BlockSpec((1, tn), lambda j: (0, 0))],
        out_specs=col(rows_out),
        out_shape=jax.ShapeDtypeStruct((rows_out, ncol), jnp.bfloat16),
        compiler_params=_cparams("parallel"),
        name="hyena_inverse",
    )(tr_t, ti_t, b_r.reshape(h1, ncol), b_i.reshape(h1, ncol), v.reshape(rows_out, ncol),
      x0.reshape(rows_out, ncol), skip_t)
    return y.reshape(L, HY_C)


SUBLANE = 8
HY_COL_TILE = 512


def _dft_outer_kron(n1):
    h1 = n1 // 2
    ang = -2.0 * np.pi * np.outer(np.arange(h1) + 0.5, np.arange(h1)) / n1
    eye = np.eye(SUBLANE)
    t_r, t_i = np.cos(ang), np.sin(ang)
    fwd = np.kron(np.concatenate([t_r, t_i], axis=0), eye)
    inv = np.concatenate([np.kron(t_r.T, eye), np.kron(t_i.T, eye)], axis=1)
    return jnp.asarray(fwd, jnp.bfloat16), jnp.asarray(inv, jnp.bfloat16)


def _hy_outer_fwd_kernel(t_ref, x_ref, o_ref):
    x = x_ref[...]
    rows_in, rows_out = x.shape[0], t_ref.shape[0] // SUBLANE
    cw = x.shape[2]
    parts = []
    for s in range(0, x.shape[1], SUBLANE):
        xs = x[:, s:s + SUBLANE, :].reshape(rows_in * SUBLANE, cw).astype(jnp.bfloat16)
        r = jnp.dot(t_ref[...], xs, preferred_element_type=jnp.float32)
        parts.append(r.reshape(rows_out, SUBLANE, cw))
    o_ref[...] = jnp.concatenate(parts, axis=1).astype(o_ref.dtype)


def _hy_outer_fwd(t_kron, x3):
    h1, n2, w = x3.shape
    n1 = 2 * h1
    blk = 2 * SUBLANE
    return pl.pallas_call(
        _hy_outer_fwd_kernel,
        grid=(n2 // blk, w // HY_COL_TILE),
        in_specs=[pl.BlockSpec(t_kron.shape, lambda j, cc: (0, 0)),
                  pl.BlockSpec((h1, blk, HY_COL_TILE), lambda j, cc: (0, j, cc))],
        out_specs=pl.BlockSpec((n1, blk, HY_COL_TILE), lambda j, cc: (0, j, cc)),
        out_shape=jax.ShapeDtypeStruct((n1, n2, w), jnp.bfloat16),
        compiler_params=_cparams("parallel", "parallel"),
        name="hyena_outer_dft",
    )(t_kron, x3)


def _hy_outer_inv_kernel(t_ref, br_ref, bi_ref, v_ref, x0_ref, skip_ref, o_ref, *, scale):
    br = br_ref[...].astype(jnp.float32)
    bi = bi_ref[...].astype(jnp.float32)
    v = v_ref[...]
    x0 = x0_ref[...].astype(jnp.float32)
    h1, _, cw = br.shape
    parts = []
    for s in range(0, br.shape[1], SUBLANE):
        sl = slice(s, s + SUBLANE)
        b = jnp.concatenate([br[:, sl, :].reshape(h1 * SUBLANE, cw), bi[:, sl, :].reshape(h1 * SUBLANE, cw)], axis=0)
        r = jnp.dot(t_ref[...], b.astype(jnp.bfloat16), preferred_element_type=jnp.float32)
        parts.append((r.reshape(h1, SUBLANE, cw) * scale + v[:, sl, :] * skip_ref[...]) * x0[:, sl, :])
    o_ref[...] = jnp.concatenate(parts, axis=1).astype(o_ref.dtype)


def _hy_outer_inv(t_kron_inv, b_r, b_i, v3, x03, skip, L):
    h1, n2, w = b_r.shape
    blk = 2 * SUBLANE
    tile = pl.BlockSpec((h1, blk, HY_COL_TILE), lambda j, cc: (0, j, cc))
    return pl.pallas_call(
        functools.partial(_hy_outer_inv_kernel, scale=1.0 / L),
        grid=(n2 // blk, w // HY_COL_TILE),
        in_specs=[pl.BlockSpec(t_kron_inv.shape, lambda j, cc: (0, 0)), tile, tile, tile, tile,
                  pl.BlockSpec((1, 1, HY_COL_TILE), lambda j, cc: (0, 0, cc))],
        out_specs=tile,
        out_shape=jax.ShapeDtypeStruct((h1, n2, w), jnp.bfloat16),
        compiler_params=_cparams("parallel", "parallel"),
        name="hyena_outer_idft",
    )(t_kron_inv, b_r, b_i, v3, x03, skip.reshape(1, 1, w))


def _hyena(z, L, conv_w, conv_b, fparams, skip):
    v, x0 = _hy_pre(z, L, conv_w, conv_b)
    hfil, inorm = _hyena_filters(L, *fparams)
    if L >= HY_TWO_STAGE_MIN_L:
        n1 = HY_N1
        n2 = 2 * L // n1
        h1 = n1 // 2
        assert n2 % (2 * SUBLANE) == 0
        t_kron, t_kron_inv = _dft_outer_kron(n1)
        v3 = v.reshape(h1, n2, HY_C)
        av = _hy_outer_fwd(t_kron, v3)
        ah = _hy_outer_fwd(t_kron, hfil.reshape(h1, n2, 2 * HY_C))
        b_r, b_i = _hy_spec(_dft_inner_table(n1, n2), av, ah, inorm, n1, n2)
        return _hy_outer_inv(t_kron_inv, b_r, b_i, v3, x0.reshape(h1, n2, HY_C), skip, L).reshape(L, HY_C)
    t_fwd = _dft_outer_table(2 * L, L)
    xv = _matmul(t_fwd, _bf(v), tm=2 * L, tn=HY_C, out_dtype=jnp.bfloat16)
    xh = _matmul(t_fwd, _bf(hfil), tm=2 * L, tn=HY_C, out_dtype=jnp.bfloat16)
    y_r, y_i = _hy_spec_direct(xv, xh, inorm, L)
    return _hy_post(t_fwd, y_r, y_i, v, x0, skip, L, L, 1)


def _merge_kernel(yh_ref, yg_ref, yl_ref, gate_h_ref, gate_g_ref, gate_l_ref, wb_ref, o_ref):
    acc = None
    for br, (y_ref, g_ref) in enumerate(((yh_ref, gate_h_ref), (yg_ref, gate_g_ref), (yl_ref, gate_l_ref))):
        t = (jnp.dot(y_ref[...], wb_ref[br], preferred_element_type=jnp.float32)
             * jax.nn.sigmoid(g_ref[...].astype(jnp.float32)))
        acc = t if acc is None else acc + t
    o_ref[...] = acc.astype(o_ref.dtype)


def _proj_residual_kernel(m_ref, w_ref, h_ref, gt_ref, o_ref):
    o_ref[...] = h_ref[...] + gt_ref[...] * jnp.dot(m_ref[...], w_ref[...], preferred_element_type=jnp.float32)


MERGE_ROWS = 512


def _merge(z, L, ys, w_branch, w_out, h, gt):
    tm = min(MERGE_ROWS, L)
    gb = ZB_MG_OFF // D_MODEL
    ybs = pl.BlockSpec((tm, HY_C), lambda i: (i, 0))
    gate = lambda br: pl.BlockSpec((tm, D_MODEL), lambda i: (i, gb + br))
    row = pl.BlockSpec((tm, D_MODEL), lambda i: (i, 0))
    merged = pl.pallas_call(
        _merge_kernel,
        grid=(L // tm,),
        in_specs=[ybs, ybs, ybs, gate(0), gate(1), gate(2),
                  pl.BlockSpec((N_BRANCH, HY_C, D_MODEL), lambda i: (0, 0, 0), pipeline_mode=pl.Buffered(1))],
        out_specs=row,
        out_shape=jax.ShapeDtypeStruct((L, D_MODEL), jnp.bfloat16),
        compiler_params=_cparams("parallel"),
        name="branch_merge",
    )(ys[0], ys[1], ys[2], z, z, z, _bf(w_branch))
    return pl.pallas_call(
        _proj_residual_kernel,
        grid=(L // tm,),
        in_specs=[row, pl.BlockSpec((D_MODEL, D_MODEL), lambda i: (0, 0), pipeline_mode=pl.Buffered(1)), row,
                  pl.BlockSpec((1, D_MODEL), lambda i: (0, 0))],
        out_specs=row,
        out_shape=jax.ShapeDtypeStruct((L, D_MODEL), jnp.float32),
        compiler_params=_cparams("parallel"),
        name="out_proj_residual",
    )(merged, _bf(w_out), h, gt)


def _pad_cols(a):
    pad = lambda n: jnp.zeros(a.shape[:-1] + (n,), a.dtype)
    return jnp.concatenate([a[..., :REC_COLS], pad(HY_OFF - REC_COLS), a[..., REC_COLS:]], axis=-1)


W_TILE = 1024
W_SHIFT = HY_OFF - REC_COLS
W_ROW_OFF = W_TILE - W_SHIFT
assert 0 < W_SHIFT <= W_TILE and W_ROW_OFF % 8 == 0 and HY_OFF % W_TILE == 0 and Z_COLS % W_TILE == 0


def _w_in_prep_kernel(a_ref, b_ref, o_ref):
    j = pl.program_id(0)
    shifted = j >= HY_OFF // W_TILE

    @pl.when(jnp.logical_not(shifted))
    def _():
        o_ref[...] = a_ref[...].T.astype(o_ref.dtype)

    @pl.when(shifted)
    def _():
        window = jnp.concatenate([a_ref[...], b_ref[...]], axis=0)
        o_ref[...] = window[W_ROW_OFF:W_ROW_OFF + W_TILE].T.astype(o_ref.dtype)


def _w_in_prep(w_in, layer):
    w_t = jnp.swapaxes(w_in, 1, 2)
    _, n, k = w_t.shape
    first_shifted = HY_OFF // W_TILE
    a_idx = lambda j: jnp.where(j < first_shifted, j, j - 1)
    tail_blocks = W_TILE // W_ROW_OFF
    assert W_TILE % W_ROW_OFF == 0 and n % W_ROW_OFF == 0
    return pl.pallas_call(
        _w_in_prep_kernel,
        grid=(Z_COLS // W_TILE,),
        in_specs=[pl.BlockSpec((None, W_TILE, k), lambda j: (layer, a_idx(j), 0)),
                  pl.BlockSpec((None, W_ROW_OFF, k), lambda j: (layer, (a_idx(j) + 1) * tail_blocks, 0))],
        out_specs=pl.BlockSpec((k, W_TILE), lambda j: (0, j)),
        out_shape=jax.ShapeDtypeStruct((k, Z_COLS), jnp.bfloat16),
        compiler_params=_cparams("parallel"),
        name="w_in_relayout",
    )(w_t, w_t)


def _mixer(h, hc, u, uc, gt, gtc, p, need_ctx):
    L, Lc = u.shape[0], uc.shape[0]
    w_in = _w_in_prep(p['w_in'], p['layer'])
    b_in = _pad_cols(p['b_in'])
    rest = dict(tm=1024, tn=W_TILE, n_cols=Z_COLS - HY_OFF, col_off=HY_OFF // W_TILE, out_dtype=jnp.bfloat16,
                out_map=lambda j: jnp.where(j < HY_COLS // W_TILE, j + MERGE_COLS // W_TILE, j - HY_COLS // W_TILE))
    z = _matmul(u, w_in, b_in, tm=1024, tn=W_TILE, n_cols=HY_OFF)
    zb = _matmul(u, w_in, b_in, **rest)
    zc = _matmul(uc, w_in, b_in, tm=1024, tn=W_TILE, n_cols=HY_OFF)
    zcb = _matmul(uc, w_in, b_in, **rest) if need_ctx else None
    zeros = lambda hd, dk, dv: jnp.zeros((hd, dv, dk), jnp.float32)
    yc_hg, hg_sf, hg_sb = _hgrn2(zc, Lc, p['lb'], p['hg_norm_w'],
                                 zeros(HG_HEADS, HG_DK, HG_DV), zeros(HG_HEADS, HG_DK, HG_DV))
    yc_gl, gl_sf, gl_sb = _gla(zc, Lc, p['gl_w_a2'], p['gl_b_a'], p['gl_norm_w'],
                               zeros(GL_HEADS, GL_DK, GL_DV), zeros(GL_HEADS, GL_DK, GL_DV))
    y_hg, _, _ = _hgrn2(z, L, p['lb'], p['hg_norm_w'], hg_sf, hg_sb)
    y_gl, _, _ = _gla(z, L, p['gl_w_a2'], p['gl_b_a'], p['gl_norm_w'], gl_sf, gl_sb)
    hy = (p['hy_conv_w'], p['hy_conv_b'], p['hy_f'], p['hy_skip'])
    y_hy = _hyena(zb, L, *hy)
    h = _merge(zb, L, (y_hy, y_hg, y_gl), p['w_branch'], p['w_out'], h, gt)
    if need_ctx:
        yc_hy = _hyena(zcb, Lc, *hy)
        hc = _merge(zcb, Lc, (yc_hy, yc_hg, yc_gl), p['w_branch'], p['w_out'], hc, gtc)
    return h, hc


GATHER_UNROLL = 8
GATHER_AHEAD = 2
GATHER_SLOTS = GATHER_AHEAD + 1


def _ffn_kernel(blk_exp_ref, n_used_ref, tok_ref, x_hbm, wg_ref, wu_ref, wd_ref, o_ref, xbuf, sem, wg_s, wu_s, wd_s):
    i = pl.program_id(0)
    n_used = n_used_ref[0]
    rows = o_ref.shape[0]

    def issue(step, slot):
        def body(r, carry):
            src = tok_ref[step * rows + r]
            pltpu.make_async_copy(x_hbm.at[pl.ds(src, 1)], xbuf.at[slot, pl.ds(r, 1)], sem.at[slot]).start()
            return carry

        lax.fori_loop(0, rows, body, 0, unroll=GATHER_UNROLL)

    for s in range(GATHER_AHEAD):
        @pl.when(jnp.logical_and(i == 0, s < n_used))
        def _(s=s):
            issue(s, s % GATHER_SLOTS)

    @pl.when(i + GATHER_AHEAD < n_used)
    def _():
        issue(i + GATHER_AHEAD, (i + GATHER_AHEAD) % GATHER_SLOTS)

    new_expert = jnp.logical_or(i == 0, blk_exp_ref[i] != blk_exp_ref[jnp.maximum(i - 1, 0)])

    @pl.when(jnp.logical_and(i < n_used, new_expert))
    def _():
        wg_s[...] = wg_ref[0].astype(jnp.bfloat16)
        wu_s[...] = wu_ref[0].astype(jnp.bfloat16)
        wd_s[...] = wd_ref[0].astype(jnp.bfloat16)

    @pl.when(i < n_used)
    def _():
        slot = i % GATHER_SLOTS
        pltpu.make_async_copy(xbuf.at[slot], xbuf.at[slot], sem.at[slot]).wait()
        x = xbuf[slot].astype(jnp.bfloat16)
        hg = jnp.dot(x, wg_s[...], preferred_element_type=jnp.float32)
        hu = jnp.dot(x, wu_s[...], preferred_element_type=jnp.float32)
        act = (hg * jax.nn.sigmoid(hg) * hu).astype(jnp.bfloat16)
        o_ref[...] = jnp.dot(act, wd_s[...], preferred_element_type=jnp.float32)

    @pl.when(i >= n_used)
    def _():
        o_ref[...] = jnp.zeros_like(o_ref)


def _grouped_ffn(x, buf_tok, blk_exp, n_used, layer, w_gate, w_up, w_down):
    p_len = buf_tok.shape[0]
    d = x.shape[1]
    n_blk = p_len // MOE_BLOCK
    grid_spec = pltpu.PrefetchScalarGridSpec(
        num_scalar_prefetch=3,
        grid=(n_blk,),
        in_specs=[pl.BlockSpec(memory_space=pl.ANY),
                  pl.BlockSpec((None, 1, d, D_FF), lambda i, be, nu, tk: (layer, be[i], 0, 0)),
                  pl.BlockSpec((None, 1, d, D_FF), lambda i, be, nu, tk: (layer, be[i], 0, 0)),
                  pl.BlockSpec((None, 1, D_FF, d), lambda i, be, nu, tk: (layer, be[i], 0, 0))],
        out_specs=pl.BlockSpec((MOE_BLOCK, d), lambda i, be, nu, tk: (i, 0)),
        scratch_shapes=[pltpu.VMEM((GATHER_SLOTS, MOE_BLOCK, d), jnp.float32), pltpu.SemaphoreType.DMA((GATHER_SLOTS,)),
                        pltpu.VMEM((d, D_FF), jnp.bfloat16), pltpu.VMEM((d, D_FF), jnp.bfloat16),
                        pltpu.VMEM((D_FF, d), jnp.bfloat16)],
    )
    return pl.pallas_call(
        _ffn_kernel,
        grid_spec=grid_spec,
        out_shape=jax.ShapeDtypeStruct((p_len, d), jnp.float32),
        compiler_params=_cparams("arbitrary"),
        name="moe_grouped_ffn",
    )(blk_exp, n_used, buf_tok, x, w_gate, w_up, w_down)


ROUTER_COLS = 128


ROUTE_TOKENS = 256
ROUTE_OUT = (0, 1, 2, 3, 4, 5)


def _route_kernel(lg_ref, tril_ref, o_ref, cnt_ref, carry):
    i = pl.program_id(0)

    @pl.when(i == 0)
    def _():
        carry[...] = jnp.zeros_like(carry)

    x = lg_ref[...]
    lane = lax.broadcasted_iota(jnp.int32, x.shape, 1).astype(jnp.float32)
    neg = jnp.float32(-jnp.inf)
    far = jnp.float32(ROUTER_COLS)
    red_max = lambda a: jnp.max(a, axis=1, keepdims=True)
    red_min = lambda a: jnp.min(a, axis=1, keepdims=True)
    red_sum = lambda a: jnp.sum(a, axis=1, keepdims=True)
    gmask = lane < N_GROUPS
    gl = jnp.where(gmask, x, neg)
    gmax = red_max(gl)
    p_top = 1.0 / red_sum(jnp.where(gmask, jnp.exp(x - gmax), 0.0))
    grp = red_min(jnp.where(gl == gmax, lane, far))
    lo = N_GROUPS + EXP_PER_GROUP * grp
    emask = jnp.logical_and(lane >= lo, lane < lo + EXP_PER_GROUP)
    el = jnp.where(emask, x, neg)
    ee = jnp.where(emask, jnp.exp(x - red_max(el)), 0.0)
    prob = ee / red_sum(ee)
    p1 = red_max(prob)
    i1 = red_min(jnp.where(jnp.logical_and(emask, prob == p1), lane, far))
    rest = jnp.where(jnp.logical_and(emask, lane != i1), prob, -1.0)
    p2 = red_max(rest)
    i2 = red_min(jnp.where(rest == p2, lane, far))
    w1 = p_top * p1 / (p1 + p2)
    w2 = p_top * p2 / (p1 + p2)
    pick1 = (lane == i1).astype(jnp.float32)
    pick2 = (lane == i2).astype(jnp.float32)
    picks = pick1 + pick2
    before = jnp.dot(tril_ref[...], picks.astype(jnp.bfloat16), preferred_element_type=jnp.float32) + carry[...]
    r1 = red_sum(pick1 * before)
    r2 = red_sum(pick2 * before)
    carry[...] += jnp.sum(picks, axis=0, keepdims=True)
    cnt_ref[...] = carry[...]
    out = jnp.zeros_like(x)
    for col, val in zip(ROUTE_OUT, (i1 - N_GROUPS, i2 - N_GROUPS, w1, w2, r1, r2)):
        out = jnp.where(lane == col, val, out)
    o_ref[...] = out


def _route(logits):
    assert TOP_K == 2
    n = logits.shape[0]
    t = min(ROUTE_TOKENS, n)
    tril = jnp.asarray(np.tril(np.ones((t, t), np.float32), -1), jnp.bfloat16)
    out, cnt = pl.pallas_call(
        _route_kernel,
        grid=(n // t,),
        in_specs=[pl.BlockSpec((t, ROUTER_COLS), lambda i: (i, 0)), pl.BlockSpec((t, t), lambda i: (0, 0))],
        out_specs=[pl.BlockSpec((t, ROUTER_COLS), lambda i: (i, 0)), pl.BlockSpec((1, ROUTER_COLS), lambda i: (0, 0))],
        out_shape=[jax.ShapeDtypeStruct((n, ROUTER_COLS), jnp.float32),
                   jax.ShapeDtypeStruct((1, ROUTER_COLS), jnp.float32)],
        scratch_shapes=[pltpu.VMEM((1, ROUTER_COLS), jnp.float32)],
        compiler_params=_cparams("arbitrary"),
        name="moe_route",
    )(logits, tril)
    expert = out[:, 0:2].astype(jnp.int32)
    weight = out[:, 2:4]
    rank = out[:, 4:6].astype(jnp.int32)
    counts = cnt[0, N_GROUPS:N_GROUPS + N_EXPERTS].astype(jnp.int32)
    return expert, weight, rank, counts


def _hier_moe(h, norm_w, shift, scale, gt, p):
    n, d = h.shape
    pad = ROUTER_COLS - N_GROUPS - N_EXPERTS
    w_r = _bf(jnp.concatenate([p['w_rg'], p['w_re'], jnp.zeros((d, pad), jnp.float32)], axis=1))
    b_r = jnp.concatenate([p['b_rg'], p['b_re'], jnp.zeros((pad,), jnp.float32)]).reshape(1, ROUTER_COLS)
    xb, logits = _norm_mod(h, norm_w, shift, scale, router=(w_r, b_r), out_dtype=jnp.float32)
    expert, weight, rank, counts = _route(logits)
    a = n * TOP_K
    padded = (counts + MOE_BLOCK - 1) // MOE_BLOCK * MOE_BLOCK
    pad_end = jnp.cumsum(padded)
    pad_off = pad_end - padded
    pos = (pad_off[expert] + rank).reshape(a)
    p_len = (a + N_EXPERTS * MOE_BLOCK + MOE_BLOCK - 1) // MOE_BLOCK * MOE_BLOCK
    n_blk = p_len // MOE_BLOCK
    tok_flat = jnp.arange(a, dtype=jnp.int32) // TOP_K
    buf_tok = (jnp.arange(p_len, dtype=jnp.int32) % n).at[pos].set(tok_flat)
    blk_start = jnp.arange(n_blk, dtype=jnp.int32) * MOE_BLOCK
    blk_exp = jnp.minimum(jnp.sum(pad_end[None, :] <= blk_start[:, None], axis=1), N_EXPERTS - 1).astype(jnp.int32)
    n_used = (pad_end[-1:] // MOE_BLOCK).astype(jnp.int32)
    y = _grouped_ffn(xb, buf_tok, blk_exp, n_used, p['layer'], p['w_gate'], p['w_up'], p['w_down'])
    return _moe_combine(y, pos, weight, h, gt)


def _combine_kernel(pos_ref, y_hbm, wts_ref, h_ref, gt_ref, o_ref, buf, sem):
    i = pl.program_id(0)
    tokens = h_ref.shape[0]

    def row_copy(step, slot, r, k):
        src = pos_ref[(step * tokens + r) * TOP_K + k]
        return pltpu.make_async_copy(y_hbm.at[pl.ds(src, 1)], buf.at[slot, k, pl.ds(r, 1)], sem.at[slot])

    def issue(step, slot):
        def body(r, carry):
            for k in range(TOP_K):
                row_copy(step, slot, r, k).start()
            return carry

        lax.fori_loop(0, tokens, body, 0, unroll=GATHER_UNROLL // TOP_K)

    @pl.when(i == 0)
    def _():
        issue(0, 0)

    @pl.when(i + 1 < pl.num_programs(0))
    def _():
        issue(i + 1, (i + 1) % 2)

    slot = i % 2
    pltpu.make_async_copy(buf.at[slot], buf.at[slot], sem.at[slot]).wait()
    rows = buf[slot]
    wts = wts_ref[...]
    acc = rows[0] * wts[:, 0:1]
    for k in range(1, TOP_K):
        acc = acc + rows[k] * wts[:, k:k + 1]
    o_ref[...] = h_ref[...] + gt_ref[...] * acc


COMBINE_TOKENS = 256


def _moe_combine(y, pos, wts, h, gt):
    n, d = h.shape
    tokens = min(COMBINE_TOKENS, n)
    grid_spec = pltpu.PrefetchScalarGridSpec(
        num_scalar_prefetch=1,
        grid=(n // tokens,),
        in_specs=[pl.BlockSpec(memory_space=pl.ANY),
                  pl.BlockSpec((tokens, TOP_K), lambda i, pos: (i, 0)),
                  pl.BlockSpec((tokens, d), lambda i, pos: (i, 0)),
                  pl.BlockSpec((1, d), lambda i, pos: (0, 0))],
        out_specs=pl.BlockSpec((tokens, d), lambda i, pos: (i, 0)),
        scratch_shapes=[pltpu.VMEM((2, TOP_K, tokens, d), jnp.float32), pltpu.SemaphoreType.DMA((2,))],
    )
    return pl.pallas_call(
        _combine_kernel,
        grid_spec=grid_spec,
        out_shape=jax.ShapeDtypeStruct((n, d), jnp.float32),
        compiler_params=_cparams("arbitrary"),
        name="moe_combine",
    )(pos, y, wts, h, gt)


def _final_norm_kernel(x_ref, w_ref, o_ref):
    x = x_ref[...]
    o_ref[...] = x * lax.rsqrt(jnp.mean(x * x, axis=-1, keepdims=True) + NORM_EPS) * w_ref[...]


def _final_norm(x, w, tm=512):
    m, d = x.shape
    return pl.pallas_call(
        _final_norm_kernel,
        grid=(m // tm,),
        in_specs=[pl.BlockSpec((tm, d), lambda i: (i, 0)), pl.BlockSpec((1, d), lambda i: (0, 0))],
        out_specs=pl.BlockSpec((tm, d), lambda i: (i, 0)),
        out_shape=jax.ShapeDtypeStruct((m, d), jnp.float32),
        compiler_params=_cparams("parallel"),
        name="final_rmsnorm",
    )(x, w.reshape(1, d))


def kernel(x, c, ctx, c_ctx, w_mod, b_mod, norm_mix_w, norm_ffn_w, w_in, b_in, hy_conv_w, hy_conv_b, hy_f_w1, hy_f_b1, hy_f_w2, hy_f_b2, hy_f_w3, hy_f_b3, hy_f_freq, hy_skip, hg_lb_raw, hg_norm_w, gl_w_a2, gl_b_a, gl_norm_w, w_branch, w_out, w_rg, b_rg, w_re, b_re, w_gate, w_up, w_down, final_norm_w):
    assert x.shape[0] == 1
    depth = w_mod.shape[0]
    lb_all = jnp.cumsum(jax.nn.softmax(hg_lb_raw, axis=0), axis=0)
    lb_all = lb_all - lb_all[:1]
    h, hc = x[0], ctx[0]
    cc = jnp.concatenate([c, c_ctx[None, :]], axis=0)
    for l in range(depth):
        need_ctx = l < depth - 1
        mod = _matmul(_bf(jax.nn.silu(cc)), w_mod, b_mod[l], layer=l)
        sh1, sc1, gt1, sh2, sc2, gt2 = jnp.split(mod[0:1], 6, axis=-1)
        sh1c, sc1c, gt1c, sh2c, sc2c, gt2c = jnp.split(mod[1:2], 6, axis=-1)
        p = dict(w_in=w_in, b_in=b_in[l], hy_conv_w=hy_conv_w[l], hy_conv_b=hy_conv_b[l],
                 hy_f=(hy_f_w1[l], hy_f_b1[l], hy_f_w2[l], hy_f_b2[l], hy_f_w3[l], hy_f_b3[l], hy_f_freq[l]),
                 hy_skip=hy_skip[l], lb=lb_all[l], hg_norm_w=hg_norm_w[l], gl_w_a2=gl_w_a2[l], gl_b_a=gl_b_a[l],
                 gl_norm_w=gl_norm_w[l], w_branch=w_branch[l], w_out=w_out[l], w_rg=w_rg[l], b_rg=b_rg[l],
                 w_re=w_re[l], b_re=b_re[l], layer=l, w_gate=w_gate, w_up=w_up, w_down=w_down)
        u = _norm_mod(h, norm_mix_w[l], sh1, sc1)
        uc = _norm_mod(hc, norm_mix_w[l], sh1c, sc1c)
        h, hc = _mixer(h, hc, u, uc, gt1, gt1c, p, need_ctx)
        h = _hier_moe(h, norm_ffn_w[l], sh2, sc2, gt2, p)
        if need_ctx:
            hc = _hier_moe(hc, norm_ffn_w[l], sh2c, sc2c, gt2c, p)
    return _final_norm(h, final_norm_w)[None]
```

```python
import functools
import math

import jax
import jax.numpy as jnp
import numpy as np
from jax import lax
from jax.experimental import pallas as pl
from jax.experimental.pallas import tpu as pltpu

D_MODEL = 2048
NORM_EPS = 1e-6

HY_C = D_MODEL // 2
HY_EMB = 33
HY_BANDS = (HY_EMB - 1) // 2
HY_DECAY_TARGET = 1e-2
HY_FAST_PCT = 0.3
HY_SLOW_PCT = 1.5
HY_MOD_SHIFT = 0.05

HG_HEADS = 8
HG_DK = 128
HG_DV = 128
HG_K = HG_HEADS * HG_DK
HG_V = HG_HEADS * HG_DV

GL_HEADS = 4
GL_DK = 128
GL_DV = 256
GL_K = GL_HEADS * GL_DK
GL_V = GL_HEADS * GL_DV
GL_RANK = 16
GL_TAU = 16.0

N_BRANCH = 3
HG_COLS = 3 * HG_K + 2 * HG_V
GL_COLS = 2 * GL_K + 2 * GL_V + 2 * GL_RANK
REC_COLS = HG_COLS + GL_COLS
HY_COLS = 3 * HY_C
MERGE_COLS = N_BRANCH * D_MODEL

COL_TILE = 512
GL_Q_OFF = HG_COLS
GL_V_OFF = GL_Q_OFF + 2 * GL_K
GL_A_OFF = GL_V_OFF + 2 * GL_V
HY_OFF = -(-(GL_A_OFF + 2 * GL_RANK) // HY_C) * HY_C
MG_OFF = HY_OFF + HY_COLS
Z_COLS = MG_OFF + MERGE_COLS
assert GL_Q_OFF % GL_K == 0 and GL_V_OFF % GL_V == 0 and GL_A_OFF % 128 == 0 and Z_COLS % COL_TILE == 0
assert MG_OFF % D_MODEL == 0
ZB_MG_OFF = 0
ZB_HY_OFF = MERGE_COLS

N_GROUPS = 4
EXP_PER_GROUP = 8
N_EXPERTS = N_GROUPS * EXP_PER_GROUP
TOP_K = 2
D_FF = D_MODEL // 4
MOE_BLOCK = 256

SCAN_CHUNK = 128
SCAN_HEAD_GROUP = 8
SCAN_CHUNKS_PER_STEP = 4
LOG2_E = 1.4426950408889634

VMEM_LIMIT_BYTES = 56 * 1024 * 1024


def _cparams(*sem):
    return pltpu.CompilerParams(dimension_semantics=sem, vmem_limit_bytes=VMEM_LIMIT_BYTES)


def _bf(a):
    return a.astype(jnp.bfloat16)


def _mm_kernel(x_ref, w_ref, b_ref, o_ref):
    acc = jnp.dot(x_ref[...], w_ref[...].astype(jnp.bfloat16), preferred_element_type=jnp.float32) + b_ref[...]
    o_ref[...] = acc.astype(o_ref.dtype)


def _matmul(x, w, bias=None, tm=512, tn=COL_TILE, out_dtype=jnp.float32, n_cols=None, layer=None, col_off=0,
            out_map=None):
    m, k = x.shape
    n = w.shape[-1] if n_cols is None else n_cols
    assert n_cols is None or n_cols % tn == 0
    assert layer is None or n % tn == 0
    tm = min(tm, -(-m // 8) * 8)
    mp = -(-m // tm) * tm
    np_ = -(-n // tn) * tn
    if bias is None:
        bias = jnp.zeros((n,), jnp.float32)
    if mp != m:
        x = jnp.pad(x, ((0, mp - m), (0, 0)))
    if np_ != n:
        w = jnp.pad(w, ((0, 0), (0, np_ - n)))
        bias = jnp.pad(bias, (0, np_ - n))
    if out_map is None:
        out_map = lambda j: j
    if layer is None:
        w_spec = pl.BlockSpec((k, tn), lambda i, j: (0, j + col_off))
    else:
        assert col_off == 0
        w_spec = pl.BlockSpec((None, k, tn), lambda i, j: (layer, 0, j))
    out = pl.pallas_call(
        _mm_kernel,
        grid=(mp // tm, np_ // tn),
        in_specs=[pl.BlockSpec((tm, k), lambda i, j: (i, 0)),
                  w_spec,
                  pl.BlockSpec((1, tn), lambda i, j: (0, j + col_off))],
        out_specs=pl.BlockSpec((tm, tn), lambda i, j: (i, out_map(j))),
        out_shape=jax.ShapeDtypeStruct((mp, np_), out_dtype),
        compiler_params=_cparams("parallel", "arbitrary"),
        name="dense_matmul",
    )(x, w, bias.reshape(1, -1))
    if mp != m or np_ != n:
        out = out[:m, :n]
    return out


def _norm_mod_kernel(*refs, with_router):
    if with_router:
        h_ref, w_ref, sh_ref, sc_ref, wr_ref, br_ref, o_ref, lg_ref = refs
    else:
        h_ref, w_ref, sh_ref, sc_ref, o_ref = refs
    x = h_ref[...]
    y = x * lax.rsqrt(jnp.mean(x * x, axis=-1, keepdims=True) + NORM_EPS) * w_ref[...]
    u = (y * (1.0 + sc_ref[...]) + sh_ref[...]).astype(jnp.bfloat16)
    o_ref[...] = u.astype(o_ref.dtype)
    if with_router:
        lg_ref[...] = jnp.dot(u, wr_ref[...], preferred_element_type=jnp.float32) + br_ref[...]


def _norm_mod(h, w, shift, scale, router=None, out_dtype=jnp.bfloat16):
    m, d = h.shape
    tm = min(512, m)
    row = pl.BlockSpec((tm, d), lambda i: (i, 0))
    vec = pl.BlockSpec((1, d), lambda i: (0, 0))
    args = [h, w.reshape(1, d), shift, scale]
    specs = [row, vec, vec, vec]
    out_shape = [jax.ShapeDtypeStruct((m, d), out_dtype)]
    out_specs = [row]
    if router is not None:
        args += list(router)
        specs += [pl.BlockSpec(router[0].shape, lambda i: (0, 0)), pl.BlockSpec(router[1].shape, lambda i: (0, 0))]
        out_shape.append(jax.ShapeDtypeStruct((m, router[0].shape[1]), jnp.float32))
        out_specs.append(pl.BlockSpec((tm, router[0].shape[1]), lambda i: (i, 0)))
    out = pl.pallas_call(
        functools.partial(_norm_mod_kernel, with_router=router is not None),
        grid=(m // tm,),
        in_specs=specs,
        out_specs=out_specs,
        out_shape=out_shape,
        compiler_params=_cparams("parallel"),
        name="norm_modulate",
    )(*args)
    return out if router is not None else out[0]


def _scan_masks(c, reverse):
    t = np.arange(c)
    ms = [np.eye(c, dtype=np.float32)]
    for lvl in range(int(math.log2(c))):
        upper = ((t >> lvl) & 1).astype(bool)
        same = (t[:, None] >> (lvl + 1)) == (t[None, :] >> (lvl + 1))
        m = same & upper[:, None] & (~upper)[None, :]
        ms.append((m.T if reverse else m).astype(np.float32))
    tri = t[None, :] >= t[:, None] if reverse else t[None, :] <= t[:, None]
    return jnp.asarray(np.stack(ms)), jnp.asarray(tri.astype(np.float32), dtype=jnp.bfloat16)


def _level_arg(cum, lvl, reverse):
    c = cum.shape[0]
    blk = 1 << lvl
    if blk >= 8:
        pieces = []
        for gs in range(0, c, 2 * blk):
            ref = cum[gs + blk:gs + blk + 1, :]
            pieces.append(ref - cum[gs:gs + blk, :])
            pieces.append(cum[gs + blk:gs + 2 * blk, :] - ref)
        arg = jnp.concatenate(pieces, axis=0)
    else:
        c3 = cum.reshape(c // 8, 8, cum.shape[1])
        sub = lax.broadcasted_iota(jnp.int32, c3.shape, 1)
        ref_row = ((sub >> lvl) | 1) << lvl
        ref = None
        for r in range(blk, 8, 2 * blk):
            cand = jnp.broadcast_to(c3[:, r:r + 1, :], c3.shape)
            ref = cand if ref is None else jnp.where(ref_row == r, cand, ref)
        upper = ((sub >> lvl) & 1) == 1
        arg = jnp.where(upper, c3 - ref, ref - c3).reshape(cum.shape)
    return -arg if reverse else arg


def _dot_nt(a, b):
    return lax.dot_general(a, b, (((1,), (1,)), ((), ())), preferred_element_type=jnp.float32)


def _dot_tn(a, b):
    return lax.dot_general(a, b, (((0,), (0,)), ((), ())), preferred_element_type=jnp.float32)


def _sigmoid_parts(z):
    e = jnp.exp(-jnp.abs(z))
    r = 1.0 / (1.0 + e)
    return jnp.minimum(z, 0.0) - jnp.log(1.0 + e), jnp.where(z >= 0.0, e * r, r)


def _scan_kernel(*refs, mode, reverse, final, heads, dk, dv, c, cps):
    it = iter(refs)
    q_ref = next(it)
    k_ref = next(it)
    v_ref = next(it)
    if mode == "hg":
        lbp_ref = next(it)
    else:
        a_ref = next(it)
        wa_ref = next(it)
        ba_ref = next(it)
    s0_ref = next(it)
    masks_ref = next(it)
    tri_ref = next(it)
    if final:
        oprev_ref = next(it)
        gate_ref = next(it)
        nw_ref = next(it)
    o_ref = next(it)
    st_ref = next(it)

    @pl.when(pl.program_id(0) == 0)
    def _():
        st_ref[...] = s0_ref[...]

    for ci in (reversed(range(cps)) if reverse else range(cps)):
        rs = slice(ci * c, (ci + 1) * c)
        if mode == "gl":
            la_all = jnp.dot(a_ref[rs, :].astype(jnp.bfloat16), wa_ref[...],
                             preferred_element_type=jnp.float32) + ba_ref[...]
        tri = tri_ref[...]
        tot_row = 0 if reverse else c - 1
        n_lvl = int(math.log2(c))
        ksl = lambda h: slice(h * dk, (h + 1) * dk)
        vsl = lambda h: slice(h * dv, (h + 1) * dv)
        for h0 in range(0, heads, SCAN_HEAD_GROUP):
            group = range(h0, min(h0 + SCAN_HEAD_GROUP, heads))
            q_, k_, cum_, qb_, kb_, o_, sc_ = {}, {}, {}, {}, {}, {}, {}
            for h in group:
                q = q_ref[rs, ksl(h)]
                if mode == "hg":
                    log_sig, sig_neg = _sigmoid_parts(k_ref[rs, ksl(h)])
                    la = lbp_ref[0:1, ksl(h)]
                    lbb = lbp_ref[1:2, ksl(h)] + log_sig
                    g = jnp.maximum(la, lbb) + jnp.log(1.0 + jnp.exp(-jnp.abs(la - lbb)))
                    k = lbp_ref[2:3, ksl(h)] * sig_neg
                    q = q * jax.nn.sigmoid(q)
                else:
                    g = _sigmoid_parts(la_all[:, ksl(h)])[0] * (1.0 / GL_TAU)
                    k = k_ref[rs, ksl(h)]
                    q = q * (dk ** -0.5)
                g = g * LOG2_E
                g1 = g.astype(jnp.bfloat16)
                r1 = g - g1.astype(jnp.float32)
                g2 = r1.astype(jnp.bfloat16)
                g3 = (r1 - g2.astype(jnp.float32)).astype(jnp.bfloat16)
                cum_[h] = (jnp.dot(tri, g1, preferred_element_type=jnp.float32)
                           + jnp.dot(tri, g2, preferred_element_type=jnp.float32)
                           + jnp.dot(tri, g3, preferred_element_type=jnp.float32))
                q_[h], k_[h] = q, k
            for h in group:
                cum = cum_[h]
                tot = cum[tot_row:tot_row + 1, :]
                st = st_ref[h]
                v = v_ref[rs, vsl(h)].astype(jnp.bfloat16)
                o_[h] = _dot_nt((q_[h] * jnp.exp2(cum)).astype(jnp.bfloat16), st.astype(jnp.bfloat16))
                kt = (k_[h] * jnp.exp2(tot - cum)).astype(jnp.bfloat16)
                st_ref[h] = st * jnp.exp2(tot) + _dot_tn(v, kt)
                qb_[h] = q_[h].astype(jnp.bfloat16)
                kb_[h] = k_[h].astype(jnp.bfloat16)
                sc_[h] = masks_ref[0] * _dot_nt(qb_[h], kb_[h])
            for lvl in range(n_lvl):
                for h in group:
                    e = jnp.exp2(_level_arg(cum_[h], lvl, reverse)).astype(jnp.bfloat16)
                    sc_[h] = sc_[h] + masks_ref[1 + lvl] * _dot_nt(qb_[h] * e, kb_[h] * e)
            for h in group:
                v = v_ref[rs, vsl(h)].astype(jnp.bfloat16)
                o = o_[h] + jnp.dot(sc_[h].astype(jnp.bfloat16), v, preferred_element_type=jnp.float32)
                if final:
                    o = o + oprev_ref[rs, vsl(h)]
                    y = o * lax.rsqrt(jnp.mean(o * o, axis=-1, keepdims=True) + NORM_EPS) * nw_ref[...]
                    gt = gate_ref[rs, vsl(h)]
                    act = jax.nn.sigmoid(gt) if mode == "hg" else gt * jax.nn.sigmoid(gt)
                    o_ref[rs, vsl(h)] = (y * act).astype(o_ref.dtype)
                else:
                    o_ref[rs, vsl(h)] = o


def _scan_pass(mode, reverse, final, L, srcs, s0, params, final_srcs=(), norm_w=None):
    heads, dk, dv = (HG_HEADS, HG_DK, HG_DV) if mode == "hg" else (GL_HEADS, GL_DK, GL_DV)
    c = min(SCAN_CHUNK, L)
    cps = min(SCAN_CHUNKS_PER_STEP, L // c)
    rows = c * cps
    nb = L // rows
    row = (lambda i: nb - 1 - i) if reverse else (lambda i: i)
    masks, tri = _scan_masks(c, reverse)

    def const(shape):
        return pl.BlockSpec(shape, lambda i: (0,) * len(shape))

    def rowblock(width, cb):
        return pl.BlockSpec((rows, width), lambda i: (row(i), cb))

    args = [a for a, _, _ in srcs] + list(params) + [s0, masks, tri]
    specs = ([rowblock(w, cb) for _, w, cb in srcs] + [const(p.shape) for p in params]
             + [const(s0.shape), const(masks.shape), const(tri.shape)])
    if final:
        args += [a for a, _, _ in final_srcs] + [norm_w]
        specs += [rowblock(w, cb) for _, w, cb in final_srcs] + [const(norm_w.shape)]
    return pl.pallas_call(
        functools.partial(_scan_kernel, mode=mode, reverse=reverse, final=final, heads=heads, dk=dk, dv=dv, c=c,
                          cps=cps),
        grid=(nb,),
        in_specs=specs,
        out_specs=[pl.BlockSpec((rows, heads * dv), lambda i: (row(i), 0)), const((heads, dv, dk))],
        out_shape=[jax.ShapeDtypeStruct((L, heads * dv), jnp.bfloat16 if final else jnp.float32),
                   jax.ShapeDtypeStruct((heads, dv, dk), jnp.float32)],
        compiler_params=_cparams("arbitrary"),
        name=f"scan_{mode}_{'bwd' if reverse else 'fwd'}",
    )(*args)


def _hgrn2(z, L, lb, norm_w, s0_f, s0_b):
    lbp = lambda d: jnp.stack([jnp.log(lb[d]), jnp.log1p(-lb[d]), 1.0 - lb[d]])
    w = HG_K
    o_b, s_b = _scan_pass("hg", True, False, L, [(z, w, 0), (z, w, 2), (z, w, 3)], s0_b, [lbp(1)])
    y, s_f = _scan_pass("hg", False, True, L, [(z, w, 0), (z, w, 1), (z, w, 3)], s0_f, [lbp(0)],
                        final_srcs=[(o_b, HG_V, 0), (z, HG_V, 4)], norm_w=norm_w.reshape(1, HG_DV))
    return y, s_f, s_b


def _gla(z, za, L, w_a2, b_a, norm_w, s0_f, s0_b):
    def gate_params(d):
        wa = jnp.zeros((128, GL_K), jnp.float32).at[d * GL_RANK:(d + 1) * GL_RANK].set(w_a2[d])
        return [_bf(wa), b_a[d].reshape(1, GL_K)]

    srcs = [(z, GL_K, GL_Q_OFF // GL_K), (z, GL_K, GL_Q_OFF // GL_K + 1), (z, GL_V, GL_V_OFF // GL_V),
            (za, 128, 0)]
    o_b, s_b = _scan_pass("gl", True, False, L, srcs, s0_b, gate_params(1))
    y, s_f = _scan_pass("gl", False, True, L, srcs, s0_f, gate_params(0),
                        final_srcs=[(o_b, GL_V, 0), (z, GL_V, GL_V_OFF // GL_V + 1)], norm_w=norm_w.reshape(1, GL_DV))
    return y, s_f, s_b


HY_FEAT_PAD = 128


def _hy_filter_kernel(f_ref, w1_ref, b1_ref, fq_ref, w2_ref, b2_ref, w3_ref, b3_ref, dl_ref, h_ref, s_ref):
    i = pl.program_id(0)
    f = f_ref[...]
    fq = fq_ref[...]
    a = jnp.sin(fq * (jnp.dot(f.astype(jnp.bfloat16), w1_ref[...], preferred_element_type=jnp.float32) + b1_ref[...]))
    a = jnp.sin(fq * (jnp.dot(a.astype(jnp.bfloat16), w2_ref[...], preferred_element_type=jnp.float32) + b2_ref[...]))
    hh = jnp.dot(a.astype(jnp.bfloat16), w3_ref[...], preferred_element_type=jnp.float32) + b3_ref[...]
    hh = hh * (jnp.exp(-f[:, 0:1] * dl_ref[...]) + HY_MOD_SHIFT)
    h_ref[...] = hh
    part = jnp.sum(jnp.abs(hh).reshape(hh.shape[0] // 8, 8, hh.shape[1]), axis=0)

    @pl.when(i == 0)
    def _():
        s_ref[...] = part

    @pl.when(i > 0)
    def _():
        s_ref[...] += part


def _hyena_filters(L, w1, b1, w2, b2, w3, b3, freq):
    t = jnp.linspace(0.0, 1.0, L, dtype=jnp.float32)[:, None]
    ang = 2.0 * math.pi * jnp.arange(L, dtype=jnp.float32)[:, None] / L
    bands = jnp.linspace(1e-4, HY_BANDS - 1, HY_BANDS, dtype=jnp.float32)[None, :]
    feats = jnp.concatenate([t, jnp.cos(bands * ang), -jnp.sin(bands * ang),
                             jnp.zeros((L, HY_FEAT_PAD - HY_EMB), jnp.float32)], axis=-1)
    deltas = jnp.abs(jnp.linspace(math.log(HY_DECAY_TARGET) / HY_SLOW_PCT, math.log(HY_DECAY_TARGET) / HY_FAST_PCT,
                                  HY_C, dtype=jnp.float32))
    fh = w1.shape[1]
    padm = lambda a, r, c: _bf(jnp.pad(a, ((0, r - a.shape[0]), (0, c - a.shape[1]))))
    padv = lambda a: jnp.pad(a, (0, HY_FEAT_PAD - a.shape[0])).reshape(1, HY_FEAT_PAD)
    tm = min(512, L)
    const = lambda r, c: pl.BlockSpec((r, c), lambda i: (0, 0))
    hfil, sums = pl.pallas_call(
        _hy_filter_kernel,
        grid=(L // tm,),
        in_specs=[pl.BlockSpec((tm, HY_FEAT_PAD), lambda i: (i, 0)),
                  const(HY_FEAT_PAD, HY_FEAT_PAD), const(1, HY_FEAT_PAD), const(1, HY_FEAT_PAD),
                  const(HY_FEAT_PAD, HY_FEAT_PAD), const(1, HY_FEAT_PAD),
                  const(HY_FEAT_PAD, 2 * HY_C), const(1, 2 * HY_C), const(1, 2 * HY_C)],
        out_specs=[pl.BlockSpec((tm, 2 * HY_C), lambda i: (i, 0)), const(8, 2 * HY_C)],
        out_shape=[jax.ShapeDtypeStruct((L, 2 * HY_C), jnp.float32), jax.ShapeDtypeStruct((8, 2 * HY_C), jnp.float32)],
        compiler_params=_cparams("arbitrary"),
        name="hyena_filters",
    )(feats, padm(w1, HY_FEAT_PAD, HY_FEAT_PAD), padv(b1), padv(freq), padm(w2, HY_FEAT_PAD, HY_FEAT_PAD), padv(b2),
      padm(w3, HY_FEAT_PAD, 2 * HY_C), b3.reshape(1, 2 * HY_C), jnp.tile(deltas, 2).reshape(1, 2 * HY_C))
    assert fh <= HY_FEAT_PAD
    inorm = 1.0 / jnp.sum(sums, axis=0)
    return hfil, inorm.reshape(2, HY_C)


def _hy_pre_kernel(x0_ref, x1_ref, v_ref, x0p_ref, x1p_ref, vp_ref, x0n_ref, x1n_ref, vn_ref, w_ref, b_ref,
                   vo_ref, x0o_ref):
    i = pl.program_id(0)
    first = i == 0
    last = i == pl.num_programs(0) - 1
    tm = x0_ref.shape[0]
    row = lax.broadcasted_iota(jnp.int32, x0_ref.shape, 0)

    def conv(x_ref, p_ref, n_ref, g):
        x = x_ref[...].astype(jnp.float32)
        cs = slice(g * HY_C, (g + 1) * HY_C)
        prev_row = jnp.where(first, 0.0, p_ref[...].astype(jnp.float32)[HALO_ROWS - 1:HALO_ROWS, :])
        next_row = jnp.where(last, 0.0, n_ref[...].astype(jnp.float32)[0:1, :])
        xp = jnp.where(row == 0, prev_row, pltpu.roll(x, 1, 0))
        xn = jnp.where(row == tm - 1, next_row, pltpu.roll(x, tm - 1, 0))
        return w_ref[0:1, cs] * xp + w_ref[1:2, cs] * x + w_ref[2:3, cs] * xn + b_ref[0:1, cs]

    x0 = conv(x0_ref, x0p_ref, x0n_ref, 0)
    x1 = conv(x1_ref, x1p_ref, x1n_ref, 1)
    v = conv(v_ref, vp_ref, vn_ref, 2)
    vo_ref[...] = (v * x1).astype(vo_ref.dtype)
    x0o_ref[...] = x0.astype(x0o_ref.dtype)


HALO_ROWS = 16


def _hy_pre(z, L, conv_w, conv_b):
    tm = min(256, L)
    nbh = L // HALO_ROWS
    cb = ZB_HY_OFF // HY_C
    main = lambda g: pl.BlockSpec((tm, HY_C), lambda i: (i, cb + g))
    prev = lambda g: pl.BlockSpec((HALO_ROWS, HY_C), lambda i: (jnp.maximum(i * (tm // HALO_ROWS) - 1, 0), cb + g))
    nxt = lambda g: pl.BlockSpec((HALO_ROWS, HY_C),
                                 lambda i: (jnp.minimum((i + 1) * (tm // HALO_ROWS), nbh - 1), cb + g))
    const = lambda a: pl.BlockSpec(a.shape, lambda i: (0, 0))
    cbias = conv_b.reshape(1, HY_COLS)
    return pl.pallas_call(
        _hy_pre_kernel,
        grid=(L // tm,),
        in_specs=[main(0), main(1), main(2), prev(0), prev(1), prev(2), nxt(0), nxt(1), nxt(2),
                  const(conv_w), const(cbias)],
        out_specs=[pl.BlockSpec((tm, HY_C), lambda i: (i, 0))] * 2,
        out_shape=[jax.ShapeDtypeStruct((L, HY_C), jnp.float32), jax.ShapeDtypeStruct((L, HY_C), jnp.bfloat16)],
        compiler_params=_cparams("parallel"),
        name="hyena_short_conv",
    )(z, z, z, z, z, z, z, z, z, conv_w, cbias)


HY_N1 = 128
HY_TWO_STAGE_MIN_L = 1024


def _dft_outer_table(n1, cols):
    ang = -2.0 * np.pi * np.outer(np.arange(n1 // 2) + 0.5, np.arange(cols)) / n1
    return jnp.asarray(np.concatenate([np.cos(ang), np.sin(ang)], axis=0), jnp.bfloat16)


def _dft_inner_table(n1, n2):
    j2 = np.arange(n2)
    f_ang = -2.0 * np.pi * np.outer(np.arange(n2), j2) / n2
    tw_ang = -2.0 * np.pi * np.outer(np.arange(n1 // 2) + 0.5, j2) / (n1 * n2)
    fr, fi = jnp.asarray(np.cos(f_ang), jnp.float32), jnp.asarray(np.sin(f_ang), jnp.float32)
    twr, twi = jnp.asarray(np.cos(tw_ang), jnp.float32), jnp.asarray(np.sin(tw_ang), jnp.float32)
    mr = fr[None] * twr[:, None, :] - fi[None] * twi[:, None, :]
    mi = fr[None] * twi[:, None, :] + fi[None] * twr[:, None, :]
    return _bf(jnp.concatenate([jnp.concatenate([mr, -mi], axis=2), jnp.concatenate([mi, mr], axis=2)], axis=1))


def _spectral_product(xv, xh, inorm, half):
    inf, inb = inorm[0:1, :], inorm[1:2, :]
    gr = xh[:half, :HY_C] * inf + xh[:half, HY_C:] * inb
    gi = xh[half:, :HY_C] * inf - xh[half:, HY_C:] * inb
    xr, xi = xv[:half], xv[half:]
    return jnp.concatenate([xr * gr - xi * gi, xr * gi + xi * gr], axis=0).astype(jnp.bfloat16)


HY_SLABS = 2


def _hy_spec_kernel(r_ref, avr_ref, avi_ref, ahr_ref, ahi_ref, inorm_ref, br_ref, bi_ref):
    n2 = avr_ref.shape[1]
    slabs = range(r_ref.shape[0])
    xv = [jnp.dot(r_ref[s], jnp.concatenate([avr_ref[s], avi_ref[s]], axis=0), preferred_element_type=jnp.float32)
          for s in slabs]
    xh = [jnp.dot(r_ref[s], jnp.concatenate([ahr_ref[s], ahi_ref[s]], axis=0), preferred_element_type=jnp.float32)
          for s in slabs]
    y = [_spectral_product(xv[s], xh[s], inorm_ref[...], n2) for s in slabs]
    for s in slabs:
        b = _dot_tn(r_ref[s], y[s])
        br_ref[s] = b[:n2].astype(br_ref.dtype)
        bi_ref[s] = b[n2:].astype(bi_ref.dtype)


def _hy_spec(r, av, ah, inorm, n1, n2):
    av3 = av.reshape(n1, n2, HY_C)
    ah3 = ah.reshape(n1, n2, 2 * HY_C)
    h1 = n1 // 2
    sl = HY_SLABS
    assert h1 % sl == 0
    out = jax.ShapeDtypeStruct((h1, n2, HY_C), jnp.bfloat16)
    return pl.pallas_call(
        _hy_spec_kernel,
        grid=(h1 // sl,),
        in_specs=[pl.BlockSpec((sl, 2 * n2, 2 * n2), lambda k: (k, 0, 0)),
                  pl.BlockSpec((sl, n2, HY_C), lambda k: (k, 0, 0)),
                  pl.BlockSpec((sl, n2, HY_C), lambda k: (k + h1 // sl, 0, 0)),
                  pl.BlockSpec((sl, n2, 2 * HY_C), lambda k: (k, 0, 0)),
                  pl.BlockSpec((sl, n2, 2 * HY_C), lambda k: (k + h1 // sl, 0, 0)),
                  pl.BlockSpec((2, HY_C), lambda k: (0, 0))],
        out_specs=[pl.BlockSpec((sl, n2, HY_C), lambda k: (k, 0, 0))] * 2,
        out_shape=[out, out],
        compiler_params=_cparams("parallel"),
        name="hyena_spectral",
    )(r, av3, av3, ah3, ah3, inorm)


def _hy_spec_direct_kernel(xv_ref, xh_ref, inorm_ref, yr_ref, yi_ref):
    half = yr_ref.shape[0]
    y = _spectral_product(xv_ref[...].astype(jnp.float32), xh_ref[...].astype(jnp.float32), inorm_ref[...], half)
    yr_ref[...] = y[:half]
    yi_ref[...] = y[half:]


def _hy_spec_direct(xv, xh, inorm, L):
    full = lambda a: pl.BlockSpec(a.shape, lambda i: (0, 0))
    out = jax.ShapeDtypeStruct((L, HY_C), jnp.bfloat16)
    return pl.pallas_call(
        _hy_spec_direct_kernel,
        grid=(1,),
        in_specs=[full(xv), full(xh), full(inorm)],
        out_specs=[pl.BlockSpec((L, HY_C), lambda i: (0, 0))] * 2,
        out_shape=[out, out],
        compiler_params=_cparams("arbitrary"),
        name="hyena_spectral_direct",
    )(xv, xh, inorm)


def _hy_post_kernel(tr_ref, ti_ref, br_ref, bi_ref, v_ref, x0_ref, skip_ref, o_ref, *, scale):
    acc = (jnp.dot(tr_ref[...], br_ref[...], preferred_element_type=jnp.float32)
           + jnp.dot(ti_ref[...], bi_ref[...], preferred_element_type=jnp.float32))
    y = (acc * scale + v_ref[...].astype(jnp.float32) * skip_ref[...]) * x0_ref[...].astype(jnp.float32)
    o_ref[...] = y.astype(o_ref.dtype)


def _hy_post(t_fwd, b_r, b_i, v, x0, skip, L, h1, n2):
    ncol = n2 * HY_C
    tn = min(4096, ncol)
    tr_t = t_fwd[:h1].T
    ti_t = t_fwd[h1:].T
    skip_t = jnp.tile(skip, tn // HY_C).reshape(1, tn)
    col = lambda rows: pl.BlockSpec((rows, tn), lambda j: (0, j))
    rows_out = tr_t.shape[0]
    y = pl.pallas_call(
        functools.partial(_hy_post_kernel, scale=1.0 / L),
        grid=(ncol // tn,),
        in_specs=[pl.BlockSpec(tr_t.shape, lambda j: (0, 0)), pl.BlockSpec(ti_t.shape, lambda j: (0, 0)),
                  col(h1), col(h1), col(rows_out), col(rows_out), pl.BlockSpec((1, tn), lambda j: (0, 0))],
        out_specs=col(rows_out),
        out_shape=jax.ShapeDtypeStruct((rows_out, ncol), jnp.bfloat16),
        compiler_params=_cparams("parallel"),
        name="hyena_inverse",
    )(tr_t, ti_t, b_r.reshape(h1, ncol), b_i.reshape(h1, ncol), v.reshape(rows_out, ncol),
      x0.reshape(rows_out, ncol), skip_t)
    return y.reshape(L, HY_C)


SUBLANE = 8
HY_COL_TILE = 512


def _dft_outer_kron(n1):
    h1 = n1 // 2
    ang = -2.0 * np.pi * np.outer(np.arange(h1) + 0.5, np.arange(h1)) / n1
    eye = np.eye(SUBLANE)
    t_r, t_i = np.cos(ang), np.sin(ang)
    fwd = np.kron(np.concatenate([t_r, t_i], axis=0), eye)
    inv = np.concatenate([np.kron(t_r.T, eye), np.kron(t_i.T, eye)], axis=1)
    return jnp.asarray(fwd, jnp.bfloat16), jnp.asarray(inv, jnp.bfloat16)


def _hy_outer_fwd_kernel(t_ref, x_ref, o_ref):
    x = x_ref[...]
    rows_in, rows_out = x.shape[0], t_ref.shape[0] // SUBLANE
    cw = x.shape[2]
    parts = []
    for s in range(0, x.shape[1], SUBLANE):
        xs = x[:, s:s + SUBLANE, :].reshape(rows_in * SUBLANE, cw).astype(jnp.bfloat16)
        r = jnp.dot(t_ref[...], xs, preferred_element_type=jnp.float32)
        parts.append(r.reshape(rows_out, SUBLANE, cw))
    o_ref[...] = jnp.concatenate(parts, axis=1).astype(o_ref.dtype)


def _hy_outer_fwd(t_kron, x3):
    h1, n2, w = x3.shape
    n1 = 2 * h1
    blk = 2 * SUBLANE
    return pl.pallas_call(
        _hy_outer_fwd_kernel,
        grid=(n2 // blk, w // HY_COL_TILE),
        in_specs=[pl.BlockSpec(t_kron.shape, lambda j, cc: (0, 0)),
                  pl.BlockSpec((h1, blk, HY_COL_TILE), lambda j, cc: (0, j, cc))],
        out_specs=pl.BlockSpec((n1, blk, HY_COL_TILE), lambda j, cc: (0, j, cc)),
        out_shape=jax.ShapeDtypeStruct((n1, n2, w), jnp.bfloat16),
        compiler_params=_cparams("parallel", "parallel"),
        name="hyena_outer_dft",
    )(t_kron, x3)


def _hy_outer_inv_kernel(t_ref, br_ref, bi_ref, v_ref, x0_ref, skip_ref, o_ref, *, scale):
    br = br_ref[...].astype(jnp.float32)
    bi = bi_ref[...].astype(jnp.float32)
    v = v_ref[...]
    x0 = x0_ref[...].astype(jnp.float32)
    h1, _, cw = br.shape
    parts = []
    for s in range(0, br.shape[1], SUBLANE):
        sl = slice(s, s + SUBLANE)
        b = jnp.concatenate([br[:, sl, :].reshape(h1 * SUBLANE, cw), bi[:, sl, :].reshape(h1 * SUBLANE, cw)], axis=0)
        r = jnp.dot(t_ref[...], b.astype(jnp.bfloat16), preferred_element_type=jnp.float32)
        parts.append((r.reshape(h1, SUBLANE, cw) * scale + v[:, sl, :] * skip_ref[...]) * x0[:, sl, :])
    o_ref[...] = jnp.concatenate(parts, axis=1).astype(o_ref.dtype)


def _hy_outer_inv(t_kron_inv, b_r, b_i, v3, x03, skip, L):
    h1, n2, w = b_r.shape
    blk = 2 * SUBLANE
    tile = pl.BlockSpec((h1, blk, HY_COL_TILE), lambda j, cc: (0, j, cc))
    return pl.pallas_call(
        functools.partial(_hy_outer_inv_kernel, scale=1.0 / L),
        grid=(n2 // blk, w // HY_COL_TILE),
        in_specs=[pl.BlockSpec(t_kron_inv.shape, lambda j, cc: (0, 0)), tile, tile, tile, tile,
                  pl.BlockSpec((1, 1, HY_COL_TILE), lambda j, cc: (0, 0, cc))],
        out_specs=tile,
        out_shape=jax.ShapeDtypeStruct((h1, n2, w), jnp.bfloat16),
        compiler_params=_cparams("parallel", "parallel"),
        name="hyena_outer_idft",
    )(t_kron_inv, b_r, b_i, v3, x03, skip.reshape(1, 1, w))


def _hyena(z, L, conv_w, conv_b, fparams, skip):
    v, x0 = _hy_pre(z, L, conv_w, conv_b)
    hfil, inorm = _hyena_filters(L, *fparams)
    if L >= HY_TWO_STAGE_MIN_L:
        n1 = HY_N1
        n2 = 2 * L // n1
        h1 = n1 // 2
        assert n2 % (2 * SUBLANE) == 0
        t_kron, t_kron_inv = _dft_outer_kron(n1)
        v3 = v.reshape(h1, n2, HY_C)
        av = _hy_outer_fwd(t_kron, v3)
        ah = _hy_outer_fwd(t_kron, hfil.reshape(h1, n2, 2 * HY_C))
        b_r, b_i = _hy_spec(_dft_inner_table(n1, n2), av, ah, inorm, n1, n2)
        return _hy_outer_inv(t_kron_inv, b_r, b_i, v3, x0.reshape(h1, n2, HY_C), skip, L).reshape(L, HY_C)
    t_fwd = _dft_outer_table(2 * L, L)
    xv = _matmul(t_fwd, _bf(v), tm=2 * L, tn=HY_C, out_dtype=jnp.bfloat16)
    xh = _matmul(t_fwd, _bf(hfil), tm=2 * L, tn=HY_C, out_dtype=jnp.bfloat16)
    y_r, y_i = _hy_spec_direct(xv, xh, inorm, L)
    return _hy_post(t_fwd, y_r, y_i, v, x0, skip, L, L, 1)


def _merge_kernel(yh_ref, yg_ref, yl_ref, gate_h_ref, gate_g_ref, gate_l_ref, wb_ref, o_ref):
    acc = None
    for br, (y_ref, g_ref) in enumerate(((yh_ref, gate_h_ref), (yg_ref, gate_g_ref), (yl_ref, gate_l_ref))):
        t = (jnp.dot(y_ref[...], wb_ref[br], preferred_element_type=jnp.float32)
             * jax.nn.sigmoid(g_ref[...].astype(jnp.float32)))
        acc = t if acc is None else acc + t
    o_ref[...] = acc.astype(o_ref.dtype)


def _proj_residual_kernel(m_ref, w_ref, h_ref, gt_ref, o_ref):
    o_ref[...] = h_ref[...] + gt_ref[...] * jnp.dot(m_ref[...], w_ref[...], preferred_element_type=jnp.float32)


MERGE_ROWS = 512


def _merge(z, L, ys, w_branch, w_out, h, gt):
    tm = min(MERGE_ROWS, L)
    gb = ZB_MG_OFF // D_MODEL
    ybs = pl.BlockSpec((tm, HY_C), lambda i: (i, 0))
    gate = lambda br: pl.BlockSpec((tm, D_MODEL), lambda i: (i, gb + br))
    row = pl.BlockSpec((tm, D_MODEL), lambda i: (i, 0))
    merged = pl.pallas_call(
        _merge_kernel,
        grid=(L // tm,),
        in_specs=[ybs, ybs, ybs, gate(0), gate(1), gate(2),
                  pl.BlockSpec((N_BRANCH, HY_C, D_MODEL), lambda i: (0, 0, 0), pipeline_mode=pl.Buffered(1))],
        out_specs=row,
        out_shape=jax.ShapeDtypeStruct((L, D_MODEL), jnp.bfloat16),
        compiler_params=_cparams("parallel"),
        name="branch_merge",
    )(ys[0], ys[1], ys[2], z, z, z, _bf(w_branch))
    return pl.pallas_call(
        _proj_residual_kernel,
        grid=(L // tm,),
        in_specs=[row, pl.BlockSpec((D_MODEL, D_MODEL), lambda i: (0, 0), pipeline_mode=pl.Buffered(1)), row,
                  pl.BlockSpec((1, D_MODEL), lambda i: (0, 0))],
        out_specs=row,
        out_shape=jax.ShapeDtypeStruct((L, D_MODEL), jnp.float32),
        compiler_params=_cparams("parallel"),
        name="out_proj_residual",
    )(merged, _bf(w_out), h, gt)


def _pad_cols(a):
    pad = lambda n: jnp.zeros(a.shape[:-1] + (n,), a.dtype)
    return jnp.concatenate([a[..., :REC_COLS], pad(HY_OFF - REC_COLS), a[..., REC_COLS:]], axis=-1)


W_TILE = 1024
W_SHIFT = HY_OFF - REC_COLS
W_ROW_OFF = W_TILE - W_SHIFT
assert 0 < W_SHIFT <= W_TILE and W_ROW_OFF % 8 == 0 and HY_OFF % W_TILE == 0 and Z_COLS % W_TILE == 0


def _w_in_prep_kernel(a_ref, b_ref, o_ref):
    j = pl.program_id(0)
    shifted = j >= HY_OFF // W_TILE

    @pl.when(jnp.logical_not(shifted))
    def _():
        o_ref[...] = a_ref[...].T.astype(o_ref.dtype)

    @pl.when(shifted)
    def _():
        window = jnp.concatenate([a_ref[...], b_ref[...]], axis=0)
        o_ref[...] = window[W_ROW_OFF:W_ROW_OFF + W_TILE].T.astype(o_ref.dtype)


def _w_in_prep(w_in, layer):
    w_t = jnp.swapaxes(w_in, 1, 2)
    _, n, k = w_t.shape
    first_shifted = HY_OFF // W_TILE
    a_idx = lambda j: jnp.where(j < first_shifted, j, j - 1)
    tail_blocks = W_TILE // W_ROW_OFF
    assert W_TILE % W_ROW_OFF == 0 and n % W_ROW_OFF == 0
    return pl.pallas_call(
        _w_in_prep_kernel,
        grid=(Z_COLS // W_TILE,),
        in_specs=[pl.BlockSpec((None, W_TILE, k), lambda j: (layer, a_idx(j), 0)),
                  pl.BlockSpec((None, W_ROW_OFF, k), lambda j: (layer, (a_idx(j) + 1) * tail_blocks, 0))],
        out_specs=pl.BlockSpec((k, W_TILE), lambda j: (0, j)),
        out_shape=jax.ShapeDtypeStruct((k, Z_COLS), jnp.bfloat16),
        compiler_params=_cparams("parallel"),
        name="w_in_relayout",
    )(w_t, w_t)


def _mixer(h, hc, u, uc, gt, gtc, p, need_ctx):
    L, Lc = u.shape[0], uc.shape[0]
    w_in = _w_in_prep(p['w_in'], p['layer'])
    b_in = _pad_cols(p['b_in'])
    rest = dict(tm=1024, tn=W_TILE, n_cols=Z_COLS - HY_OFF, col_off=HY_OFF // W_TILE, out_dtype=jnp.bfloat16,
                out_map=lambda j: jnp.where(j < HY_COLS // W_TILE, j + MERGE_COLS // W_TILE, j - HY_COLS // W_TILE))
    gates = dict(tm=1024, tn=128, n_cols=128, col_off=GL_A_OFF // 128)
    z = _matmul(u, w_in, b_in, tm=1024, tn=W_TILE, n_cols=GL_A_OFF)
    za = _matmul(u, w_in, b_in, **gates)
    zb = _matmul(u, w_in, b_in, **rest)
    zc = _matmul(uc, w_in, b_in, tm=1024, tn=W_TILE, n_cols=GL_A_OFF)
    zca = _matmul(uc, w_in, b_in, **gates)
    zcb = _matmul(uc, w_in, b_in, **rest) if need_ctx else None
    zeros = lambda hd, dk, dv: jnp.zeros((hd, dv, dk), jnp.float32)
    yc_hg, hg_sf, hg_sb = _hgrn2(zc, Lc, p['lb'], p['hg_norm_w'],
                                 zeros(HG_HEADS, HG_DK, HG_DV), zeros(HG_HEADS, HG_DK, HG_DV))
    yc_gl, gl_sf, gl_sb = _gla(zc, zca, Lc, p['gl_w_a2'], p['gl_b_a'], p['gl_norm_w'],
                               zeros(GL_HEADS, GL_DK, GL_DV), zeros(GL_HEADS, GL_DK, GL_DV))
    y_hg, _, _ = _hgrn2(z, L, p['lb'], p['hg_norm_w'], hg_sf, hg_sb)
    y_gl, _, _ = _gla(z, za, L, p['gl_w_a2'], p['gl_b_a'], p['gl_norm_w'], gl_sf, gl_sb)
    hy = (p['hy_conv_w'], p['hy_conv_b'], p['hy_f'], p['hy_skip'])
    y_hy = _hyena(zb, L, *hy)
    h = _merge(zb, L, (y_hy, y_hg, y_gl), p['w_branch'], p['w_out'], h, gt)
    if need_ctx:
        yc_hy = _hyena(zcb, Lc, *hy)
        hc = _merge(zcb, Lc, (yc_hy, yc_hg, yc_gl), p['w_branch'], p['w_out'], hc, gtc)
    return h, hc


GATHER_UNROLL = 8
GATHER_AHEAD = 2
GATHER_SLOTS = GATHER_AHEAD + 1


def _ffn_kernel(blk_exp_ref, n_used_ref, tok_ref, x_hbm, wg_ref, wu_ref, wd_ref, o_ref, xbuf, sem, wg_s, wu_s, wd_s):
    i = pl.program_id(0)
    n_used = n_used_ref[0]
    rows = o_ref.shape[0]

    def issue(step, slot):
        def body(r, carry):
            src = tok_ref[step * rows + r]
            pltpu.make_async_copy(x_hbm.at[pl.ds(src, 1)], xbuf.at[slot, pl.ds(r, 1)], sem.at[slot]).start()
            return carry

        lax.fori_loop(0, rows, body, 0, unroll=GATHER_UNROLL)

    for s in range(GATHER_AHEAD):
        @pl.when(jnp.logical_and(i == 0, s < n_used))
        def _(s=s):
            issue(s, s % GATHER_SLOTS)

    @pl.when(i + GATHER_AHEAD < n_used)
    def _():
        issue(i + GATHER_AHEAD, (i + GATHER_AHEAD) % GATHER_SLOTS)

    new_expert = jnp.logical_or(i == 0, blk_exp_ref[i] != blk_exp_ref[jnp.maximum(i - 1, 0)])

    @pl.when(jnp.logical_and(i < n_used, new_expert))
    def _():
        wg_s[...] = wg_ref[0].astype(jnp.bfloat16)
        wu_s[...] = wu_ref[0].astype(jnp.bfloat16)
        wd_s[...] = wd_ref[0].astype(jnp.bfloat16)

    @pl.when(i < n_used)
    def _():
        slot = i % GATHER_SLOTS
        pltpu.make_async_copy(xbuf.at[slot], xbuf.at[slot], sem.at[slot]).wait()
        x = xbuf[slot].astype(jnp.bfloat16)
        hg = jnp.dot(x, wg_s[...], preferred_element_type=jnp.float32)
        hu = jnp.dot(x, wu_s[...], preferred_element_type=jnp.float32)
        act = (hg * jax.nn.sigmoid(hg) * hu).astype(jnp.bfloat16)
        o_ref[...] = jnp.dot(act, wd_s[...], preferred_element_type=jnp.float32)

    @pl.when(i >= n_used)
    def _():
        o_ref[...] = jnp.zeros_like(o_ref)


def _grouped_ffn(x, buf_tok, blk_exp, n_used, layer, w_gate, w_up, w_down):
    p_len = buf_tok.shape[0]
    d = x.shape[1]
    n_blk = p_len // MOE_BLOCK
    grid_spec = pltpu.PrefetchScalarGridSpec(
        num_scalar_prefetch=3,
        grid=(n_blk,),
        in_specs=[pl.BlockSpec(memory_space=pl.ANY),
                  pl.BlockSpec((None, 1, d, D_FF), lambda i, be, nu, tk: (layer, be[i], 0, 0)),
                  pl.BlockSpec((None, 1, d, D_FF), lambda i, be, nu, tk: (layer, be[i], 0, 0)),
                  pl.BlockSpec((None, 1, D_FF, d), lambda i, be, nu, tk: (layer, be[i], 0, 0))],
        out_specs=pl.BlockSpec((MOE_BLOCK, d), lambda i, be, nu, tk: (i, 0)),
        scratch_shapes=[pltpu.VMEM((GATHER_SLOTS, MOE_BLOCK, d), jnp.float32), pltpu.SemaphoreType.DMA((GATHER_SLOTS,)),
                        pltpu.VMEM((d, D_FF), jnp.bfloat16), pltpu.VMEM((d, D_FF), jnp.bfloat16),
                        pltpu.VMEM((D_FF, d), jnp.bfloat16)],
    )
    return pl.pallas_call(
        _ffn_kernel,
        grid_spec=grid_spec,
        out_shape=jax.ShapeDtypeStruct((p_len, d), jnp.float32),
        compiler_params=_cparams("arbitrary"),
        name="moe_grouped_ffn",
    )(blk_exp, n_used, buf_tok, x, w_gate, w_up, w_down)


ROUTER_COLS = 128


ROUTE_TOKENS = 256
ROUTE_OUT = (0, 1, 2, 3, 4, 5)


def _route_kernel(lg_ref, tril_ref, o_ref, cnt_ref, carry):
    i = pl.program_id(0)

    @pl.when(i == 0)
    def _():
        carry[...] = jnp.zeros_like(carry)

    x = lg_ref[...]
    lane = lax.broadcasted_iota(jnp.int32, x.shape, 1).astype(jnp.float32)
    neg = jnp.float32(-jnp.inf)
    far = jnp.float32(ROUTER_COLS)
    red_max = lambda a: jnp.max(a, axis=1, keepdims=True)
    red_min = lambda a: jnp.min(a, axis=1, keepdims=True)
    red_sum = lambda a: jnp.sum(a, axis=1, keepdims=True)
    gmask = lane < N_GROUPS
    gl = jnp.where(gmask, x, neg)
    gmax = red_max(gl)
    p_top = 1.0 / red_sum(jnp.where(gmask, jnp.exp(x - gmax), 0.0))
    grp = red_min(jnp.where(gl == gmax, lane, far))
    lo = N_GROUPS + EXP_PER_GROUP * grp
    emask = jnp.logical_and(lane >= lo, lane < lo + EXP_PER_GROUP)
    el = jnp.where(emask, x, neg)
    ee = jnp.where(emask, jnp.exp(x - red_max(el)), 0.0)
    prob = ee / red_sum(ee)
    p1 = red_max(prob)
    i1 = red_min(jnp.where(jnp.logical_and(emask, prob == p1), lane, far))
    rest = jnp.where(jnp.logical_and(emask, lane != i1), prob, -1.0)
    p2 = red_max(rest)
    i2 = red_min(jnp.where(rest == p2, lane, far))
    w1 = p_top * p1 / (p1 + p2)
    w2 = p_top * p2 / (p1 + p2)
    pick1 = (lane == i1).astype(jnp.float32)
    pick2 = (lane == i2).astype(jnp.float32)
    picks = pick1 + pick2
    before = jnp.dot(tril_ref[...], picks.astype(jnp.bfloat16), preferred_element_type=jnp.float32) + carry[...]
    r1 = red_sum(pick1 * before)
    r2 = red_sum(pick2 * before)
    carry[...] += jnp.sum(picks, axis=0, keepdims=True)
    cnt_ref[...] = carry[...]
    out = jnp.zeros_like(x)
    for col, val in zip(ROUTE_OUT, (i1 - N_GROUPS, i2 - N_GROUPS, w1, w2, r1, r2)):
        out = jnp.where(lane == col, val, out)
    o_ref[...] = out


def _route(logits):
    assert TOP_K == 2
    n = logits.shape[0]
    t = min(ROUTE_TOKENS, n)
    tril = jnp.asarray(np.tril(np.ones((t, t), np.float32), -1), jnp.bfloat16)
    out, cnt = pl.pallas_call(
        _route_kernel,
        grid=(n // t,),
        in_specs=[pl.BlockSpec((t, ROUTER_COLS), lambda i: (i, 0)), pl.BlockSpec((t, t), lambda i: (0, 0))],
        out_specs=[pl.BlockSpec((t, ROUTER_COLS), lambda i: (i, 0)), pl.BlockSpec((1, ROUTER_COLS), lambda i: (0, 0))],
        out_shape=[jax.ShapeDtypeStruct((n, ROUTER_COLS), jnp.float32),
                   jax.ShapeDtypeStruct((1, ROUTER_COLS), jnp.float32)],
        scratch_shapes=[pltpu.VMEM((1, ROUTER_COLS), jnp.float32)],
        compiler_params=_cparams("arbitrary"),
        name="moe_route",
    )(logits, tril)
    expert = out[:, 0:2].astype(jnp.int32)
    weight = out[:, 2:4]
    rank = out[:, 4:6].astype(jnp.int32)
    counts = cnt[0, N_GROUPS:N_GROUPS + N_EXPERTS].astype(jnp.int32)
    return expert, weight, rank, counts


def _hier_moe(h, norm_w, shift, scale, gt, p, final_w=None):
    n, d = h.shape
    pad = ROUTER_COLS - N_GROUPS - N_EXPERTS
    w_r = _bf(jnp.concatenate([p['w_rg'], p['w_re'], jnp.zeros((d, pad), jnp.float32)], axis=1))
    b_r = jnp.concatenate([p['b_rg'], p['b_re'], jnp.zeros((pad,), jnp.float32)]).reshape(1, ROUTER_COLS)
    xb, logits = _norm_mod(h, norm_w, shift, scale, router=(w_r, b_r), out_dtype=jnp.float32)
    expert, weight, rank, counts = _route(logits)
    a = n * TOP_K
    padded = (counts + MOE_BLOCK - 1) // MOE_BLOCK * MOE_BLOCK
    pad_end = jnp.cumsum(padded)
    pad_off = pad_end - padded
    pos = (pad_off[expert] + rank).reshape(a)
    p_len = (a + N_EXPERTS * MOE_BLOCK + MOE_BLOCK - 1) // MOE_BLOCK * MOE_BLOCK
    n_blk = p_len // MOE_BLOCK
    tok_flat = jnp.arange(a, dtype=jnp.int32) // TOP_K
    buf_tok = (jnp.arange(p_len, dtype=jnp.int32) % n).at[pos].set(tok_flat)
    blk_start = jnp.arange(n_blk, dtype=jnp.int32) * MOE_BLOCK
    blk_exp = jnp.minimum(jnp.sum(pad_end[None, :] <= blk_start[:, None], axis=1), N_EXPERTS - 1).astype(jnp.int32)
    n_used = (pad_end[-1:] // MOE_BLOCK).astype(jnp.int32)
    y = _grouped_ffn(xb, buf_tok, blk_exp, n_used, p['layer'], p['w_gate'], p['w_up'], p['w_down'])
    return _moe_combine(y, pos, weight, h, gt, final_w)


def _combine_kernel(pos_ref, y_hbm, wts_ref, h_ref, gt_ref, fw_ref, o_ref, buf, sem, *, final):
    i = pl.program_id(0)
    tokens = h_ref.shape[0]

    def row_copy(step, slot, r, k):
        src = pos_ref[(step * tokens + r) * TOP_K + k]
        return pltpu.make_async_copy(y_hbm.at[pl.ds(src, 1)], buf.at[slot, k, pl.ds(r, 1)], sem.at[slot])

    def issue(step, slot):
        def body(r, carry):
            for k in range(TOP_K):
                row_copy(step, slot, r, k).start()
            return carry

        lax.fori_loop(0, tokens, body, 0, unroll=GATHER_UNROLL // TOP_K)

    @pl.when(i == 0)
    def _():
        issue(0, 0)

    @pl.when(i + 1 < pl.num_programs(0))
    def _():
        issue(i + 1, (i + 1) % 2)

    slot = i % 2
    pltpu.make_async_copy(buf.at[slot], buf.at[slot], sem.at[slot]).wait()
    rows = buf[slot]
    wts = wts_ref[...]
    acc = rows[0] * wts[:, 0:1]
    for k in range(1, TOP_K):
        acc = acc + rows[k] * wts[:, k:k + 1]
    out = h_ref[...] + gt_ref[...] * acc
    if final:
        out = out * lax.rsqrt(jnp.mean(out * out, axis=-1, keepdims=True) + NORM_EPS) * fw_ref[...]
    o_ref[...] = out


COMBINE_TOKENS = 256


def _moe_combine(y, pos, wts, h, gt, final_w=None):
    n, d = h.shape
    tokens = min(COMBINE_TOKENS, n)
    grid_spec = pltpu.PrefetchScalarGridSpec(
        num_scalar_prefetch=1,
        grid=(n // tokens,),
        in_specs=[pl.BlockSpec(memory_space=pl.ANY),
                  pl.BlockSpec((tokens, TOP_K), lambda i, pos: (i, 0)),
                  pl.BlockSpec((tokens, d), lambda i, pos: (i, 0)),
                  pl.BlockSpec((1, d), lambda i, pos: (0, 0)),
                  pl.BlockSpec((1, d), lambda i, pos: (0, 0))],
        out_specs=pl.BlockSpec((tokens, d), lambda i, pos: (i, 0)),
        scratch_shapes=[pltpu.VMEM((2, TOP_K, tokens, d), jnp.float32), pltpu.SemaphoreType.DMA((2,))],
    )
    return pl.pallas_call(
        functools.partial(_combine_kernel, final=final_w is not None),
        grid_spec=grid_spec,
        out_shape=jax.ShapeDtypeStruct((n, d), jnp.float32),
        compiler_params=_cparams("arbitrary"),
        name="moe_combine",
    )(pos, y, wts, h, gt, (gt if final_w is None else final_w.reshape(1, d)))


def kernel(x, c, ctx, c_ctx, w_mod, b_mod, norm_mix_w, norm_ffn_w, w_in, b_in, hy_conv_w, hy_conv_b, hy_f_w1, hy_f_b1, hy_f_w2, hy_f_b2, hy_f_w3, hy_f_b3, hy_f_freq, hy_skip, hg_lb_raw, hg_norm_w, gl_w_a2, gl_b_a, gl_norm_w, w_branch, w_out, w_rg, b_rg, w_re, b_re, w_gate, w_up, w_down, final_norm_w):
    assert x.shape[0] == 1
    depth = w_mod.shape[0]
    lb_all = jnp.cumsum(jax.nn.softmax(hg_lb_raw, axis=0), axis=0)
    lb_all = lb_all - lb_all[:1]
    h, hc = x[0], ctx[0]
    cc = jnp.concatenate([c, c_ctx[None, :]], axis=0)
    for l in range(depth):
        need_ctx = l < depth - 1
        mod = _matmul(_bf(jax.nn.silu(cc)), w_mod, b_mod[l], layer=l)
        sh1, sc1, gt1, sh2, sc2, gt2 = jnp.split(mod[0:1], 6, axis=-1)
        sh1c, sc1c, gt1c, sh2c, sc2c, gt2c = jnp.split(mod[1:2], 6, axis=-1)
        p = dict(w_in=w_in, b_in=b_in[l], hy_conv_w=hy_conv_w[l], hy_conv_b=hy_conv_b[l],
                 hy_f=(hy_f_w1[l], hy_f_b1[l], hy_f_w2[l], hy_f_b2[l], hy_f_w3[l], hy_f_b3[l], hy_f_freq[l]),
                 hy_skip=hy_skip[l], lb=lb_all[l], hg_norm_w=hg_norm_w[l], gl_w_a2=gl_w_a2[l], gl_b_a=gl_b_a[l],
                 gl_norm_w=gl_norm_w[l], w_branch=w_branch[l], w_out=w_out[l], w_rg=w_rg[l], b_rg=b_rg[l],
                 w_re=w_re[l], b_re=b_re[l], layer=l, w_gate=w_gate, w_up=w_up, w_down=w_down)
        u = _norm_mod(h, norm_mix_w[l], sh1, sc1)
        uc = _norm_mod(hc, norm_mix_w[l], sh1c, sc1c)
        h, hc = _mixer(h, hc, u, uc, gt1, gt1c, p, need_ctx)
        h = _hier_moe(h, norm_ffn_w[l], sh2, sc2, gt2, p, final_w=None if need_ctx else final_norm_w)
        if need_ctx:
            hc = _hier_moe(hc, norm_ffn_w[l], sh2c, sc2c, gt2c, p)
    return h[None]
```

```python
import functools
import math

import jax
import jax.numpy as jnp
import numpy as np
from jax import lax
from jax.experimental import pallas as pl
from jax.experimental.pallas import tpu as pltpu

D_MODEL = 2048
NORM_EPS = 1e-6

HY_C = D_MODEL // 2
HY_EMB = 33
HY_BANDS = (HY_EMB - 1) // 2
HY_DECAY_TARGET = 1e-2
HY_FAST_PCT = 0.3
HY_SLOW_PCT = 1.5
HY_MOD_SHIFT = 0.05

HG_HEADS = 8
HG_DK = 128
HG_DV = 128
HG_K = HG_HEADS * HG_DK
HG_V = HG_HEADS * HG_DV

GL_HEADS = 4
GL_DK = 128
GL_DV = 256
GL_K = GL_HEADS * GL_DK
GL_V = GL_HEADS * GL_DV
GL_RANK = 16
GL_TAU = 16.0

N_BRANCH = 3
HG_COLS = 3 * HG_K + 2 * HG_V
GL_COLS = 2 * GL_K + 2 * GL_V + 2 * GL_RANK
REC_COLS = HG_COLS + GL_COLS
HY_COLS = 3 * HY_C
MERGE_COLS = N_BRANCH * D_MODEL

COL_TILE = 512
GL_Q_OFF = HG_COLS
GL_V_OFF = GL_Q_OFF + 2 * GL_K
GL_A_OFF = GL_V_OFF + 2 * GL_V
HY_OFF = -(-(GL_A_OFF + 2 * GL_RANK) // HY_C) * HY_C
MG_OFF = HY_OFF + HY_COLS
Z_COLS = MG_OFF + MERGE_COLS
assert GL_Q_OFF % GL_K == 0 and GL_V_OFF % GL_V == 0 and GL_A_OFF % 128 == 0 and Z_COLS % COL_TILE == 0
assert MG_OFF % D_MODEL == 0
ZB_MG_OFF = 0
ZB_HY_OFF = MERGE_COLS

N_GROUPS = 4
EXP_PER_GROUP = 8
N_EXPERTS = N_GROUPS * EXP_PER_GROUP
TOP_K = 2
D_FF = D_MODEL // 4
MOE_BLOCK = 256

SCAN_CHUNK = 128
SCAN_HEAD_GROUP = 8
SCAN_CHUNKS_PER_STEP = 4
LOG2_E = 1.4426950408889634

VMEM_LIMIT_BYTES = 56 * 1024 * 1024


def _cparams(*sem):
    return pltpu.CompilerParams(dimension_semantics=sem, vmem_limit_bytes=VMEM_LIMIT_BYTES)


def _bf(a):
    return a.astype(jnp.bfloat16)


def _mm_kernel(x_ref, w_ref, b_ref, o_ref):
    acc = jnp.dot(x_ref[...], w_ref[...].astype(jnp.bfloat16), preferred_element_type=jnp.float32) + b_ref[...]
    o_ref[...] = acc.astype(o_ref.dtype)


def _matmul(x, w, bias=None, tm=512, tn=COL_TILE, out_dtype=jnp.float32, n_cols=None, layer=None, col_off=0,
            out_map=None):
    m, k = x.shape
    n = w.shape[-1] if n_cols is None else n_cols
    assert n_cols is None or n_cols % tn == 0
    assert layer is None or n % tn == 0
    tm = min(tm, -(-m // 8) * 8)
    mp = -(-m // tm) * tm
    np_ = -(-n // tn) * tn
    if bias is None:
        bias = jnp.zeros((n,), jnp.float32)
    if mp != m:
        x = jnp.pad(x, ((0, mp - m), (0, 0)))
    if np_ != n:
        w = jnp.pad(w, ((0, 0), (0, np_ - n)))
        bias = jnp.pad(bias, (0, np_ - n))
    if out_map is None:
        out_map = lambda j: j
    if layer is None:
        w_spec = pl.BlockSpec((k, tn), lambda i, j: (0, j + col_off))
    else:
        assert col_off == 0
        w_spec = pl.BlockSpec((None, k, tn), lambda i, j: (layer, 0, j))
    out = pl.pallas_call(
        _mm_kernel,
        grid=(mp // tm, np_ // tn),
        in_specs=[pl.BlockSpec((tm, k), lambda i, j: (i, 0)),
                  w_spec,
                  pl.BlockSpec((1, tn), lambda i, j: (0, j + col_off))],
        out_specs=pl.BlockSpec((tm, tn), lambda i, j: (i, out_map(j))),
        out_shape=jax.ShapeDtypeStruct((mp, np_), out_dtype),
        compiler_params=_cparams("parallel", "arbitrary"),
        name="dense_matmul",
    )(x, w, bias.reshape(1, -1))
    if mp != m or np_ != n:
        out = out[:m, :n]
    return out


def _norm_mod_kernel(*refs, with_router):
    if with_router:
        h_ref, w_ref, sh_ref, sc_ref, wr_ref, br_ref, o_ref, lg_ref = refs
    else:
        h_ref, w_ref, sh_ref, sc_ref, o_ref = refs
    x = h_ref[...]
    y = x * lax.rsqrt(jnp.mean(x * x, axis=-1, keepdims=True) + NORM_EPS) * w_ref[...]
    u = (y * (1.0 + sc_ref[...]) + sh_ref[...]).astype(jnp.bfloat16)
    o_ref[...] = u.astype(o_ref.dtype)
    if with_router:
        lg_ref[...] = jnp.dot(u, wr_ref[...], preferred_element_type=jnp.float32) + br_ref[...]


def _norm_mod(h, w, shift, scale, router=None, out_dtype=jnp.bfloat16):
    m, d = h.shape
    tm = min(512, m)
    row = pl.BlockSpec((tm, d), lambda i: (i, 0))
    vec = pl.BlockSpec((1, d), lambda i: (0, 0))
    args = [h, w.reshape(1, d), shift, scale]
    specs = [row, vec, vec, vec]
    out_shape = [jax.ShapeDtypeStruct((m, d), out_dtype)]
    out_specs = [row]
    if router is not None:
        args += list(router)
        specs += [pl.BlockSpec(router[0].shape, lambda i: (0, 0)), pl.BlockSpec(router[1].shape, lambda i: (0, 0))]
        out_shape.append(jax.ShapeDtypeStruct((m, router[0].shape[1]), jnp.float32))
        out_specs.append(pl.BlockSpec((tm, router[0].shape[1]), lambda i: (i, 0)))
    out = pl.pallas_call(
        functools.partial(_norm_mod_kernel, with_router=router is not None),
        grid=(m // tm,),
        in_specs=specs,
        out_specs=out_specs,
        out_shape=out_shape,
        compiler_params=_cparams("parallel"),
        name="norm_modulate",
    )(*args)
    return out if router is not None else out[0]


def _scan_masks(c, reverse):
    t = np.arange(c)
    ms = [np.eye(c, dtype=np.float32)]
    for lvl in range(int(math.log2(c))):
        upper = ((t >> lvl) & 1).astype(bool)
        same = (t[:, None] >> (lvl + 1)) == (t[None, :] >> (lvl + 1))
        m = same & upper[:, None] & (~upper)[None, :]
        ms.append((m.T if reverse else m).astype(np.float32))
    tri = t[None, :] >= t[:, None] if reverse else t[None, :] <= t[:, None]
    return jnp.asarray(np.stack(ms)), jnp.asarray(tri.astype(np.float32), dtype=jnp.bfloat16)


def _level_arg(cum, lvl, reverse):
    c = cum.shape[0]
    blk = 1 << lvl
    if blk >= 8:
        pieces = []
        for gs in range(0, c, 2 * blk):
            ref = cum[gs + blk:gs + blk + 1, :]
            pieces.append(ref - cum[gs:gs + blk, :])
            pieces.append(cum[gs + blk:gs + 2 * blk, :] - ref)
        arg = jnp.concatenate(pieces, axis=0)
    else:
        c3 = cum.reshape(c // 8, 8, cum.shape[1])
        sub = lax.broadcasted_iota(jnp.int32, c3.shape, 1)
        ref_row = ((sub >> lvl) | 1) << lvl
        ref = None
        for r in range(blk, 8, 2 * blk):
            cand = jnp.broadcast_to(c3[:, r:r + 1, :], c3.shape)
            ref = cand if ref is None else jnp.where(ref_row == r, cand, ref)
        upper = ((sub >> lvl) & 1) == 1
        arg = jnp.where(upper, c3 - ref, ref - c3).reshape(cum.shape)
    return -arg if reverse else arg


def _dot_nt(a, b):
    return lax.dot_general(a, b, (((1,), (1,)), ((), ())), preferred_element_type=jnp.float32)


def _dot_tn(a, b):
    return lax.dot_general(a, b, (((0,), (0,)), ((), ())), preferred_element_type=jnp.float32)


def _sigmoid_parts(z):
    e = jnp.exp(-jnp.abs(z))
    r = 1.0 / (1.0 + e)
    return jnp.minimum(z, 0.0) - jnp.log(1.0 + e), jnp.where(z >= 0.0, e * r, r)


def _scan_kernel(*refs, mode, reverse, final, heads, dk, dv, c, cps):
    it = iter(refs)
    q_ref = next(it)
    k_ref = next(it)
    v_ref = next(it)
    if mode == "hg":
        lbp_ref = next(it)
    else:
        a_ref = next(it)
        wa_ref = next(it)
        ba_ref = next(it)
    s0_ref = next(it)
    masks_ref = next(it)
    tri_ref = next(it)
    if final:
        oprev_ref = next(it)
        gate_ref = next(it)
        nw_ref = next(it)
    o_ref = next(it)
    st_ref = next(it)

    @pl.when(pl.program_id(0) == 0)
    def _():
        st_ref[...] = s0_ref[...]

    for ci in (reversed(range(cps)) if reverse else range(cps)):
        rs = slice(ci * c, (ci + 1) * c)
        if mode == "gl":
            la_all = jnp.dot(a_ref[rs, :].astype(jnp.bfloat16), wa_ref[...],
                             preferred_element_type=jnp.float32) + ba_ref[...]
        tri = tri_ref[...]
        tot_row = 0 if reverse else c - 1
        n_lvl = int(math.log2(c))
        ksl = lambda h: slice(h * dk, (h + 1) * dk)
        vsl = lambda h: slice(h * dv, (h + 1) * dv)
        for h0 in range(0, heads, SCAN_HEAD_GROUP):
            group = range(h0, min(h0 + SCAN_HEAD_GROUP, heads))
            q_, k_, cum_, qb_, kb_, o_, sc_ = {}, {}, {}, {}, {}, {}, {}
            for h in group:
                q = q_ref[rs, ksl(h)]
                if mode == "hg":
                    log_sig, sig_neg = _sigmoid_parts(k_ref[rs, ksl(h)])
                    la = lbp_ref[0:1, ksl(h)]
                    lbb = lbp_ref[1:2, ksl(h)] + log_sig
                    g = jnp.maximum(la, lbb) + jnp.log(1.0 + jnp.exp(-jnp.abs(la - lbb)))
                    k = lbp_ref[2:3, ksl(h)] * sig_neg
                    q = q * jax.nn.sigmoid(q)
                else:
                    g = _sigmoid_parts(la_all[:, ksl(h)])[0] * (1.0 / GL_TAU)
                    k = k_ref[rs, ksl(h)]
                    q = q * (dk ** -0.5)
                g = g * LOG2_E
                g1 = g.astype(jnp.bfloat16)
                r1 = g - g1.astype(jnp.float32)
                g2 = r1.astype(jnp.bfloat16)
                g3 = (r1 - g2.astype(jnp.float32)).astype(jnp.bfloat16)
                cum_[h] = (jnp.dot(tri, g1, preferred_element_type=jnp.float32)
                           + jnp.dot(tri, g2, preferred_element_type=jnp.float32)
                           + jnp.dot(tri, g3, preferred_element_type=jnp.float32))
                q_[h], k_[h] = q, k
            for h in group:
                cum = cum_[h]
                tot = cum[tot_row:tot_row + 1, :]
                st = st_ref[h]
                v = v_ref[rs, vsl(h)].astype(jnp.bfloat16)
                o_[h] = _dot_nt((q_[h] * jnp.exp2(cum)).astype(jnp.bfloat16), st.astype(jnp.bfloat16))
                kt = (k_[h] * jnp.exp2(tot - cum)).astype(jnp.bfloat16)
                st_ref[h] = st * jnp.exp2(tot) + _dot_tn(v, kt)
                qb_[h] = q_[h].astype(jnp.bfloat16)
                kb_[h] = k_[h].astype(jnp.bfloat16)
                sc_[h] = masks_ref[0] * _dot_nt(qb_[h], kb_[h])
            for lvl in range(n_lvl):
                for h in group:
                    e = jnp.exp2(_level_arg(cum_[h], lvl, reverse)).astype(jnp.bfloat16)
                    sc_[h] = sc_[h] + masks_ref[1 + lvl] * _dot_nt(qb_[h] * e, kb_[h] * e)
            for h in group:
                v = v_ref[rs, vsl(h)].astype(jnp.bfloat16)
                o = o_[h] + jnp.dot(sc_[h].astype(jnp.bfloat16), v, preferred_element_type=jnp.float32)
                if final:
                    o = o + oprev_ref[rs, vsl(h)]
                    y = o * lax.rsqrt(jnp.mean(o * o, axis=-1, keepdims=True) + NORM_EPS) * nw_ref[...]
                    gt = gate_ref[rs, vsl(h)]
                    act = jax.nn.sigmoid(gt) if mode == "hg" else gt * jax.nn.sigmoid(gt)
                    o_ref[rs, vsl(h)] = (y * act).astype(o_ref.dtype)
                else:
                    o_ref[rs, vsl(h)] = o


def _scan_pass(mode, reverse, final, L, srcs, s0, params, final_srcs=(), norm_w=None):
    heads, dk, dv = (HG_HEADS, HG_DK, HG_DV) if mode == "hg" else (GL_HEADS, GL_DK, GL_DV)
    c = min(SCAN_CHUNK, L)
    cps = min(SCAN_CHUNKS_PER_STEP, L // c)
    rows = c * cps
    nb = L // rows
    row = (lambda i: nb - 1 - i) if reverse else (lambda i: i)
    masks, tri = _scan_masks(c, reverse)

    def const(shape):
        return pl.BlockSpec(shape, lambda i: (0,) * len(shape))

    def rowblock(width, cb):
        return pl.BlockSpec((rows, width), lambda i: (row(i), cb))

    args = [a for a, _, _ in srcs] + list(params) + [s0, masks, tri]
    specs = ([rowblock(w, cb) for _, w, cb in srcs] + [const(p.shape) for p in params]
             + [const(s0.shape), const(masks.shape), const(tri.shape)])
    if final:
        args += [a for a, _, _ in final_srcs] + [norm_w]
        specs += [rowblock(w, cb) for _, w, cb in final_srcs] + [const(norm_w.shape)]
    return pl.pallas_call(
        functools.partial(_scan_kernel, mode=mode, reverse=reverse, final=final, heads=heads, dk=dk, dv=dv, c=c,
                          cps=cps),
        grid=(nb,),
        in_specs=specs,
        out_specs=[pl.BlockSpec((rows, heads * dv), lambda i: (row(i), 0)), const((heads, dv, dk))],
        out_shape=[jax.ShapeDtypeStruct((L, heads * dv), jnp.bfloat16 if final else jnp.float32),
                   jax.ShapeDtypeStruct((heads, dv, dk), jnp.float32)],
        compiler_params=_cparams("arbitrary"),
        name=f"scan_{mode}_{'bwd' if reverse else 'fwd'}",
    )(*args)


def _hgrn2(z, L, lb, norm_w, s0_f, s0_b):
    lbp = lambda d: jnp.stack([jnp.log(lb[d]), jnp.log1p(-lb[d]), 1.0 - lb[d]])
    w = HG_K
    o_b, s_b = _scan_pass("hg", True, False, L, [(z, w, 0), (z, w, 2), (z, w, 3)], s0_b, [lbp(1)])
    y, s_f = _scan_pass("hg", False, True, L, [(z, w, 0), (z, w, 1), (z, w, 3)], s0_f, [lbp(0)],
                        final_srcs=[(o_b, HG_V, 0), (z, HG_V, 4)], norm_w=norm_w.reshape(1, HG_DV))
    return y, s_f, s_b


def _gla(z, za, L, w_a2, b_a, norm_w, s0_f, s0_b):
    def gate_params(d):
        wa = jnp.zeros((128, GL_K), jnp.float32).at[d * GL_RANK:(d + 1) * GL_RANK].set(w_a2[d])
        return [_bf(wa), b_a[d].reshape(1, GL_K)]

    srcs = [(z, GL_K, GL_Q_OFF // GL_K), (z, GL_K, GL_Q_OFF // GL_K + 1), (z, GL_V, GL_V_OFF // GL_V),
            (za, 128, 0)]
    o_b, s_b = _scan_pass("gl", True, False, L, srcs, s0_b, gate_params(1))
    y, s_f = _scan_pass("gl", False, True, L, srcs, s0_f, gate_params(0),
                        final_srcs=[(o_b, GL_V, 0), (z, GL_V, GL_V_OFF // GL_V + 1)], norm_w=norm_w.reshape(1, GL_DV))
    return y, s_f, s_b


HY_FEAT_PAD = 128


def _hy_filter_kernel(f_ref, w1_ref, b1_ref, fq_ref, w2_ref, b2_ref, w3_ref, b3_ref, dl_ref, h_ref, s_ref):
    i = pl.program_id(0)
    f = f_ref[...]
    fq = fq_ref[...]
    a = jnp.sin(fq * (jnp.dot(f.astype(jnp.bfloat16), w1_ref[...], preferred_element_type=jnp.float32) + b1_ref[...]))
    a = jnp.sin(fq * (jnp.dot(a.astype(jnp.bfloat16), w2_ref[...], preferred_element_type=jnp.float32) + b2_ref[...]))
    hh = jnp.dot(a.astype(jnp.bfloat16), w3_ref[...], preferred_element_type=jnp.float32) + b3_ref[...]
    hh = hh * (jnp.exp(-f[:, 0:1] * dl_ref[...]) + HY_MOD_SHIFT)
    h_ref[...] = hh
    part = jnp.sum(jnp.abs(hh).reshape(hh.shape[0] // 8, 8, hh.shape[1]), axis=0)

    @pl.when(i == 0)
    def _():
        s_ref[...] = part

    @pl.when(i > 0)
    def _():
        s_ref[...] += part


def _hyena_filters(L, w1, b1, w2, b2, w3, b3, freq):
    t = jnp.linspace(0.0, 1.0, L, dtype=jnp.float32)[:, None]
    ang = 2.0 * math.pi * jnp.arange(L, dtype=jnp.float32)[:, None] / L
    bands = jnp.linspace(1e-4, HY_BANDS - 1, HY_BANDS, dtype=jnp.float32)[None, :]
    feats = jnp.concatenate([t, jnp.cos(bands * ang), -jnp.sin(bands * ang),
                             jnp.zeros((L, HY_FEAT_PAD - HY_EMB), jnp.float32)], axis=-1)
    deltas = jnp.abs(jnp.linspace(math.log(HY_DECAY_TARGET) / HY_SLOW_PCT, math.log(HY_DECAY_TARGET) / HY_FAST_PCT,
                                  HY_C, dtype=jnp.float32))
    fh = w1.shape[1]
    padm = lambda a, r, c: _bf(jnp.pad(a, ((0, r - a.shape[0]), (0, c - a.shape[1]))))
    padv = lambda a: jnp.pad(a, (0, HY_FEAT_PAD - a.shape[0])).reshape(1, HY_FEAT_PAD)
    tm = min(512, L)
    const = lambda r, c: pl.BlockSpec((r, c), lambda i: (0, 0))
    hfil, sums = pl.pallas_call(
        _hy_filter_kernel,
        grid=(L // tm,),
        in_specs=[pl.BlockSpec((tm, HY_FEAT_PAD), lambda i: (i, 0)),
                  const(HY_FEAT_PAD, HY_FEAT_PAD), const(1, HY_FEAT_PAD), const(1, HY_FEAT_PAD),
                  const(HY_FEAT_PAD, HY_FEAT_PAD), const(1, HY_FEAT_PAD),
                  const(HY_FEAT_PAD, 2 * HY_C), const(1, 2 * HY_C), const(1, 2 * HY_C)],
        out_specs=[pl.BlockSpec((tm, 2 * HY_C), lambda i: (i, 0)), const(8, 2 * HY_C)],
        out_shape=[jax.ShapeDtypeStruct((L, 2 * HY_C), jnp.float32), jax.ShapeDtypeStruct((8, 2 * HY_C), jnp.float32)],
        compiler_params=_cparams("arbitrary"),
        name="hyena_filters",
    )(feats, padm(w1, HY_FEAT_PAD, HY_FEAT_PAD), padv(b1), padv(freq), padm(w2, HY_FEAT_PAD, HY_FEAT_PAD), padv(b2),
      padm(w3, HY_FEAT_PAD, 2 * HY_C), b3.reshape(1, 2 * HY_C), jnp.tile(deltas, 2).reshape(1, 2 * HY_C))
    assert fh <= HY_FEAT_PAD
    inorm = 1.0 / jnp.sum(sums, axis=0)
    return hfil, inorm.reshape(2, HY_C)


def _hy_pre_kernel(x0_ref, x1_ref, v_ref, x0p_ref, x1p_ref, vp_ref, x0n_ref, x1n_ref, vn_ref, w_ref, b_ref,
                   vo_ref, x0o_ref):
    i = pl.program_id(0)
    first = i == 0
    last = i == pl.num_programs(0) - 1
    tm = x0_ref.shape[0]
    row = lax.broadcasted_iota(jnp.int32, x0_ref.shape, 0)

    def conv(x_ref, p_ref, n_ref, g):
        x = x_ref[...].astype(jnp.float32)
        cs = slice(g * HY_C, (g + 1) * HY_C)
        prev_row = jnp.where(first, 0.0, p_ref[...].astype(jnp.float32)[HALO_ROWS - 1:HALO_ROWS, :])
        next_row = jnp.where(last, 0.0, n_ref[...].astype(jnp.float32)[0:1, :])
        xp = jnp.where(row == 0, prev_row, pltpu.roll(x, 1, 0))
        xn = jnp.where(row == tm - 1, next_row, pltpu.roll(x, tm - 1, 0))
        return w_ref[0:1, cs] * xp + w_ref[1:2, cs] * x + w_ref[2:3, cs] * xn + b_ref[0:1, cs]

    x0 = conv(x0_ref, x0p_ref, x0n_ref, 0)
    x1 = conv(x1_ref, x1p_ref, x1n_ref, 1)
    v = conv(v_ref, vp_ref, vn_ref, 2)
    vo_ref[...] = (v * x1).astype(vo_ref.dtype)
    x0o_ref[...] = x0.astype(x0o_ref.dtype)


HALO_ROWS = 16


def _hy_pre(z, L, conv_w, conv_b):
    tm = min(256, L)
    nbh = L // HALO_ROWS
    cb = ZB_HY_OFF // HY_C
    main = lambda g: pl.BlockSpec((tm, HY_C), lambda i: (i, cb + g))
    prev = lambda g: pl.BlockSpec((HALO_ROWS, HY_C), lambda i: (jnp.maximum(i * (tm // HALO_ROWS) - 1, 0), cb + g))
    nxt = lambda g: pl.BlockSpec((HALO_ROWS, HY_C),
                                 lambda i: (jnp.minimum((i + 1) * (tm // HALO_ROWS), nbh - 1), cb + g))
    const = lambda a: pl.BlockSpec(a.shape, lambda i: (0, 0))
    cbias = conv_b.reshape(1, HY_COLS)
    return pl.pallas_call(
        _hy_pre_kernel,
        grid=(L // tm,),
        in_specs=[main(0), main(1), main(2), prev(0), prev(1), prev(2), nxt(0), nxt(1), nxt(2),
                  const(conv_w), const(cbias)],
        out_specs=[pl.BlockSpec((tm, HY_C), lambda i: (i, 0))] * 2,
        out_shape=[jax.ShapeDtypeStruct((L, HY_C), jnp.float32), jax.ShapeDtypeStruct((L, HY_C), jnp.bfloat16)],
        compiler_params=_cparams("parallel"),
        name="hyena_short_conv",
    )(z, z, z, z, z, z, z, z, z, conv_w, cbias)


HY_N1 = 128
HY_TWO_STAGE_MIN_L = 1024


def _dft_outer_table(n1, cols):
    ang = -2.0 * np.pi * np.outer(np.arange(n1 // 2) + 0.5, np.arange(cols)) / n1
    return jnp.asarray(np.concatenate([np.cos(ang), np.sin(ang)], axis=0), jnp.bfloat16)


def _dft_inner_table(n1, n2):
    j2 = np.arange(n2)
    f_ang = -2.0 * np.pi * np.outer(np.arange(n2), j2) / n2
    tw_ang = -2.0 * np.pi * np.outer(np.arange(n1 // 2) + 0.5, j2) / (n1 * n2)
    fr, fi = jnp.asarray(np.cos(f_ang), jnp.float32), jnp.asarray(np.sin(f_ang), jnp.float32)
    twr, twi = jnp.asarray(np.cos(tw_ang), jnp.float32), jnp.asarray(np.sin(tw_ang), jnp.float32)
    mr = fr[None] * twr[:, None, :] - fi[None] * twi[:, None, :]
    mi = fr[None] * twi[:, None, :] + fi[None] * twr[:, None, :]
    return _bf(jnp.concatenate([jnp.concatenate([mr, -mi], axis=2), jnp.concatenate([mi, mr], axis=2)], axis=1))


def _spectral_product(xv, xh, inorm, half):
    inf, inb = inorm[0:1, :], inorm[1:2, :]
    gr = xh[:half, :HY_C] * inf + xh[:half, HY_C:] * inb
    gi = xh[half:, :HY_C] * inf - xh[half:, HY_C:] * inb
    xr, xi = xv[:half], xv[half:]
    return jnp.concatenate([xr * gr - xi * gi, xr * gi + xi * gr], axis=0).astype(jnp.bfloat16)


HY_SLABS = 2


def _hy_spec_kernel(r_ref, avr_ref, avi_ref, ahr_ref, ahi_ref, inorm_ref, br_ref, bi_ref):
    n2 = avr_ref.shape[1]
    slabs = range(r_ref.shape[0])
    xv = [jnp.dot(r_ref[s], jnp.concatenate([avr_ref[s], avi_ref[s]], axis=0), preferred_element_type=jnp.float32)
          for s in slabs]
    xh = [jnp.dot(r_ref[s], jnp.concatenate([ahr_ref[s], ahi_ref[s]], axis=0), preferred_element_type=jnp.float32)
          for s in slabs]
    y = [_spectral_product(xv[s], xh[s], inorm_ref[...], n2) for s in slabs]
    for s in slabs:
        b = _dot_tn(r_ref[s], y[s])
        br_ref[s] = b[:n2].astype(br_ref.dtype)
        bi_ref[s] = b[n2:].astype(bi_ref.dtype)


def _hy_spec(r, av, ah, inorm, n1, n2):
    av3 = av.reshape(n1, n2, HY_C)
    ah3 = ah.reshape(n1, n2, 2 * HY_C)
    h1 = n1 // 2
    sl = HY_SLABS
    assert h1 % sl == 0
    out = jax.ShapeDtypeStruct((h1, n2, HY_C), jnp.bfloat16)
    return pl.pallas_call(
        _hy_spec_kernel,
        grid=(h1 // sl,),
        in_specs=[pl.BlockSpec((sl, 2 * n2, 2 * n2), lambda k: (k, 0, 0)),
                  pl.BlockSpec((sl, n2, HY_C), lambda k: (k, 0, 0)),
                  pl.BlockSpec((sl, n2, HY_C), lambda k: (k + h1 // sl, 0, 0)),
                  pl.BlockSpec((sl, n2, 2 * HY_C), lambda k: (k, 0, 0)),
                  pl.BlockSpec((sl, n2, 2 * HY_C), lambda k: (k + h1 // sl, 0, 0)),
                  pl.BlockSpec((2, HY_C), lambda k: (0, 0))],
        out_specs=[pl.BlockSpec((sl, n2, HY_C), lambda k: (k, 0, 0))] * 2,
        out_shape=[out, out],
        compiler_params=_cparams("parallel"),
        name="hyena_spectral",
    )(r, av3, av3, ah3, ah3, inorm)


def _hy_spec_direct_kernel(xv_ref, xh_ref, inorm_ref, yr_ref, yi_ref):
    half = yr_ref.shape[0]
    y = _spectral_product(xv_ref[...].astype(jnp.float32), xh_ref[...].astype(jnp.float32), inorm_ref[...], half)
    yr_ref[...] = y[:half]
    yi_ref[...] = y[half:]


def _hy_spec_direct(xv, xh, inorm, L):
    full = lambda a: pl.BlockSpec(a.shape, lambda i: (0, 0))
    out = jax.ShapeDtypeStruct((L, HY_C), jnp.bfloat16)
    return pl.pallas_call(
        _hy_spec_direct_kernel,
        grid=(1,),
        in_specs=[full(xv), full(xh), full(inorm)],
        out_specs=[pl.BlockSpec((L, HY_C), lambda i: (0, 0))] * 2,
        out_shape=[out, out],
        compiler_params=_cparams("arbitrary"),
        name="hyena_spectral_direct",
    )(xv, xh, inorm)


def _hy_post_kernel(tr_ref, ti_ref, br_ref, bi_ref, v_ref, x0_ref, skip_ref, o_ref, *, scale):
    acc = (jnp.dot(tr_ref[...], br_ref[...], preferred_element_type=jnp.float32)
           + jnp.dot(ti_ref[...], bi_ref[...], preferred_element_type=jnp.float32))
    y = (acc * scale + v_ref[...].astype(jnp.float32) * skip_ref[...]) * x0_ref[...].astype(jnp.float32)
    o_ref[...] = y.astype(o_ref.dtype)


def _hy_post(t_fwd, b_r, b_i, v, x0, skip, L, h1, n2):
    ncol = n2 * HY_C
    tn = min(4096, ncol)
    tr_t = t_fwd[:h1].T
    ti_t = t_fwd[h1:].T
    skip_t = jnp.tile(skip, tn // HY_C).reshape(1, tn)
    col = lambda rows: pl.BlockSpec((rows, tn), lambda j: (0, j))
    rows_out = tr_t.shape[0]
    y = pl.pallas_call(
        functools.partial(_hy_post_kernel, scale=1.0 / L),
        grid=(ncol // tn,),
        in_specs=[pl.BlockSpec(tr_t.shape, lambda j: (0, 0)), pl.BlockSpec(ti_t.shape, lambda j: (0, 0)),
                  col(h1), col(h1), col(rows_out), col(rows_out), pl.BlockSpec((1, tn), lambda j: (0, 0))],
        out_specs=col(rows_out),
        out_shape=jax.ShapeDtypeStruct((rows_out, ncol), jnp.bfloat16),
        compiler_params=_cparams("parallel"),
        name="hyena_inverse",
    )(tr_t, ti_t, b_r.reshape(h1, ncol), b_i.reshape(h1, ncol), v.reshape(rows_out, ncol),
      x0.reshape(rows_out, ncol), skip_t)
    return y.reshape(L, HY_C)


SUBLANE = 8
HY_COL_TILE = 512


def _dft_outer_kron(n1):
    h1 = n1 // 2
    ang = -2.0 * np.pi * np.outer(np.arange(h1) + 0.5, np.arange(h1)) / n1
    eye = np.eye(SUBLANE)
    t_r, t_i = np.cos(ang), np.sin(ang)
    fwd = np.kron(np.concatenate([t_r, t_i], axis=0), eye)
    inv = np.concatenate([np.kron(t_r.T, eye), np.kron(t_i.T, eye)], axis=1)
    return jnp.asarray(fwd, jnp.bfloat16), jnp.asarray(inv, jnp.bfloat16)


def _hy_outer_fwd_kernel(t_ref, x_ref, o_ref):
    x = x_ref[...]
    rows_in, rows_out = x.shape[0], t_ref.shape[0] // SUBLANE
    cw = x.shape[2]
    parts = []
    for s in range(0, x.shape[1], SUBLANE):
        xs = x[:, s:s + SUBLANE, :].reshape(rows_in * SUBLANE, cw).astype(jnp.bfloat16)
        r = jnp.dot(t_ref[...], xs, preferred_element_type=jnp.float32)
        parts.append(r.reshape(rows_out, SUBLANE, cw))
    o_ref[...] = jnp.concatenate(parts, axis=1).astype(o_ref.dtype)


def _hy_outer_fwd(t_kron, x3):
    h1, n2, w = x3.shape
    n1 = 2 * h1
    blk = 2 * SUBLANE
    return pl.pallas_call(
        _hy_outer_fwd_kernel,
        grid=(n2 // blk, w // HY_COL_TILE),
        in_specs=[pl.BlockSpec(t_kron.shape, lambda j, cc: (0, 0)),
                  pl.BlockSpec((h1, blk, HY_COL_TILE), lambda j, cc: (0, j, cc))],
        out_specs=pl.BlockSpec((n1, blk, HY_COL_TILE), lambda j, cc: (0, j, cc)),
        out_shape=jax.ShapeDtypeStruct((n1, n2, w), jnp.bfloat16),
        compiler_params=_cparams("parallel", "parallel"),
        name="hyena_outer_dft",
    )(t_kron, x3)


def _hy_outer_inv_kernel(t_ref, br_ref, bi_ref, v_ref, x0_ref, skip_ref, o_ref, *, scale):
    br = br_ref[...].astype(jnp.float32)
    bi = bi_ref[...].astype(jnp.float32)
    v = v_ref[...]
    x0 = x0_ref[...].astype(jnp.float32)
    h1, _, cw = br.shape
    parts = []
    for s in range(0, br.shape[1], SUBLANE):
        sl = slice(s, s + SUBLANE)
        b = jnp.concatenate([br[:, sl, :].reshape(h1 * SUBLANE, cw), bi[:, sl, :].reshape(h1 * SUBLANE, cw)], axis=0)
        r = jnp.dot(t_ref[...], b.astype(jnp.bfloat16), preferred_element_type=jnp.float32)
        parts.append((r.reshape(h1, SUBLANE, cw) * scale + v[:, sl, :] * skip_ref[...]) * x0[:, sl, :])
    o_ref[...] = jnp.concatenate(parts, axis=1).astype(o_ref.dtype)


def _hy_outer_inv(t_kron_inv, b_r, b_i, v3, x03, skip, L):
    h1, n2, w = b_r.shape
    blk = 2 * SUBLANE
    tile = pl.BlockSpec((h1, blk, HY_COL_TILE), lambda j, cc: (0, j, cc))
    return pl.pallas_call(
        functools.partial(_hy_outer_inv_kernel, scale=1.0 / L),
        grid=(n2 // blk, w // HY_COL_TILE),
        in_specs=[pl.BlockSpec(t_kron_inv.shape, lambda j, cc: (0, 0)), tile, tile, tile, tile,
                  pl.BlockSpec((1, 1, HY_COL_TILE), lambda j, cc: (0, 0, cc))],
        out_specs=tile,
        out_shape=jax.ShapeDtypeStruct((h1, n2, w), jnp.bfloat16),
        compiler_params=_cparams("parallel", "parallel"),
        name="hyena_outer_idft",
    )(t_kron_inv, b_r, b_i, v3, x03, skip.reshape(1, 1, w))


def _hyena(z, L, conv_w, conv_b, fparams, skip):
    v, x0 = _hy_pre(z, L, conv_w, conv_b)
    hfil, inorm = _hyena_filters(L, *fparams)
    if L >= HY_TWO_STAGE_MIN_L:
        n1 = HY_N1
        n2 = 2 * L // n1
        h1 = n1 // 2
        assert n2 % (2 * SUBLANE) == 0
        t_kron, t_kron_inv = _dft_outer_kron(n1)
        v3 = v.reshape(h1, n2, HY_C)
        av = _hy_outer_fwd(t_kron, v3)
        ah = _hy_outer_fwd(t_kron, hfil.reshape(h1, n2, 2 * HY_C))
        b_r, b_i = _hy_spec(_dft_inner_table(n1, n2), av, ah, inorm, n1, n2)
        return _hy_outer_inv(t_kron_inv, b_r, b_i, v3, x0.reshape(h1, n2, HY_C), skip, L).reshape(L, HY_C)
    t_fwd = _dft_outer_table(2 * L, L)
    xv = _matmul(t_fwd, _bf(v), tm=2 * L, tn=HY_C, out_dtype=jnp.bfloat16)
    xh = _matmul(t_fwd, _bf(hfil), tm=2 * L, tn=HY_C, out_dtype=jnp.bfloat16)
    y_r, y_i = _hy_spec_direct(xv, xh, inorm, L)
    return _hy_post(t_fwd, y_r, y_i, v, x0, skip, L, L, 1)


def _merge_kernel(yh_ref, yg_ref, yl_ref, gate_h_ref, gate_g_ref, gate_l_ref, wb_ref, o_ref):
    acc = None
    for br, (y_ref, g_ref) in enumerate(((yh_ref, gate_h_ref), (yg_ref, gate_g_ref), (yl_ref, gate_l_ref))):
        t = (jnp.dot(y_ref[...], wb_ref[br], preferred_element_type=jnp.float32)
             * jax.nn.sigmoid(g_ref[...].astype(jnp.float32)))
        acc = t if acc is None else acc + t
    o_ref[...] = acc.astype(o_ref.dtype)


def _proj_residual_kernel(m_ref, w_ref, h_ref, gt_ref, o_ref):
    o_ref[...] = h_ref[...] + gt_ref[...] * jnp.dot(m_ref[...], w_ref[...], preferred_element_type=jnp.float32)


MERGE_ROWS = 512


def _merge(z, L, ys, w_branch, w_out, h, gt):
    tm = min(MERGE_ROWS, L)
    gb = ZB_MG_OFF // D_MODEL
    ybs = pl.BlockSpec((tm, HY_C), lambda i: (i, 0))
    gate = lambda br: pl.BlockSpec((tm, D_MODEL), lambda i: (i, gb + br))
    row = pl.BlockSpec((tm, D_MODEL), lambda i: (i, 0))
    merged = pl.pallas_call(
        _merge_kernel,
        grid=(L // tm,),
        in_specs=[ybs, ybs, ybs, gate(0), gate(1), gate(2),
                  pl.BlockSpec((N_BRANCH, HY_C, D_MODEL), lambda i: (0, 0, 0), pipeline_mode=pl.Buffered(1))],
        out_specs=row,
        out_shape=jax.ShapeDtypeStruct((L, D_MODEL), jnp.bfloat16),
        compiler_params=_cparams("parallel"),
        name="branch_merge",
    )(ys[0], ys[1], ys[2], z, z, z, _bf(w_branch))
    return pl.pallas_call(
        _proj_residual_kernel,
        grid=(L // tm,),
        in_specs=[row, pl.BlockSpec((D_MODEL, D_MODEL), lambda i: (0, 0), pipeline_mode=pl.Buffered(1)), row,
                  pl.BlockSpec((1, D_MODEL), lambda i: (0, 0))],
        out_specs=row,
        out_shape=jax.ShapeDtypeStruct((L, D_MODEL), jnp.float32),
        compiler_params=_cparams("parallel"),
        name="out_proj_residual",
    )(merged, _bf(w_out), h, gt)


def _pad_cols(a):
    pad = lambda n: jnp.zeros(a.shape[:-1] + (n,), a.dtype)
    return jnp.concatenate([a[..., :REC_COLS], pad(HY_OFF - REC_COLS), a[..., REC_COLS:]], axis=-1)


W_TILE = 1024
W_SHIFT = HY_OFF - REC_COLS
W_ROW_OFF = W_TILE - W_SHIFT
assert 0 < W_SHIFT <= W_TILE and W_ROW_OFF % 8 == 0 and HY_OFF % W_TILE == 0 and Z_COLS % W_TILE == 0


def _w_in_prep_kernel(a_ref, b_ref, o_ref):
    j = pl.program_id(0)
    shifted = j >= HY_OFF // W_TILE

    @pl.when(jnp.logical_not(shifted))
    def _():
        o_ref[...] = a_ref[...].T.astype(o_ref.dtype)

    @pl.when(shifted)
    def _():
        window = jnp.concatenate([a_ref[...], b_ref[...]], axis=0)
        o_ref[...] = window[W_ROW_OFF:W_ROW_OFF + W_TILE].T.astype(o_ref.dtype)


def _w_in_prep(w_in, layer):
    w_t = jnp.swapaxes(w_in, 1, 2)
    _, n, k = w_t.shape
    first_shifted = HY_OFF // W_TILE
    a_idx = lambda j: jnp.where(j < first_shifted, j, j - 1)
    tail_blocks = W_TILE // W_ROW_OFF
    assert W_TILE % W_ROW_OFF == 0 and n % W_ROW_OFF == 0
    return pl.pallas_call(
        _w_in_prep_kernel,
        grid=(Z_COLS // W_TILE,),
        in_specs=[pl.BlockSpec((None, W_TILE, k), lambda j: (layer, a_idx(j), 0)),
                  pl.BlockSpec((None, W_ROW_OFF, k), lambda j: (layer, (a_idx(j) + 1) * tail_blocks, 0))],
        out_specs=pl.BlockSpec((k, W_TILE), lambda j: (0, j)),
        out_shape=jax.ShapeDtypeStruct((k, Z_COLS), jnp.bfloat16),
        compiler_params=_cparams("parallel"),
        name="w_in_relayout",
    )(w_t, w_t)


def _mixer(h, hc, u, uc, gt, gtc, p, need_ctx):
    L, Lc = u.shape[0], uc.shape[0]
    w_in = _w_in_prep(p['w_in'], p['layer'])
    b_in = _pad_cols(p['b_in'])
    rest = dict(tm=1024, tn=W_TILE, n_cols=Z_COLS - HY_OFF, col_off=HY_OFF // W_TILE, out_dtype=jnp.bfloat16,
                out_map=lambda j: jnp.where(j < HY_COLS // W_TILE, j + MERGE_COLS // W_TILE, j - HY_COLS // W_TILE))
    gates = dict(tm=1024, tn=128, n_cols=128, col_off=GL_A_OFF // 128)
    z = _matmul(u, w_in, b_in, tm=1024, tn=W_TILE, n_cols=GL_A_OFF)
    za = _matmul(u, w_in, b_in, **gates)
    zb = _matmul(u, w_in, b_in, **rest)
    zc = _matmul(uc, w_in, b_in, tm=1024, tn=W_TILE, n_cols=GL_A_OFF)
    zca = _matmul(uc, w_in, b_in, **gates)
    zcb = _matmul(uc, w_in, b_in, **rest) if need_ctx else None
    zeros = lambda hd, dk, dv: jnp.zeros((hd, dv, dk), jnp.float32)
    yc_hg, hg_sf, hg_sb = _hgrn2(zc, Lc, p['lb'], p['hg_norm_w'],
                                 zeros(HG_HEADS, HG_DK, HG_DV), zeros(HG_HEADS, HG_DK, HG_DV))
    yc_gl, gl_sf, gl_sb = _gla(zc, zca, Lc, p['gl_w_a2'], p['gl_b_a'], p['gl_norm_w'],
                               zeros(GL_HEADS, GL_DK, GL_DV), zeros(GL_HEADS, GL_DK, GL_DV))
    y_hg, _, _ = _hgrn2(z, L, p['lb'], p['hg_norm_w'], hg_sf, hg_sb)
    y_gl, _, _ = _gla(z, za, L, p['gl_w_a2'], p['gl_b_a'], p['gl_norm_w'], gl_sf, gl_sb)
    hy = (p['hy_conv_w'], p['hy_conv_b'], p['hy_f'], p['hy_skip'])
    y_hy = _hyena(zb, L, *hy)
    h = _merge(zb, L, (y_hy, y_hg, y_gl), p['w_branch'], p['w_out'], h, gt)
    if need_ctx:
        yc_hy = _hyena(zcb, Lc, *hy)
        hc = _merge(zcb, Lc, (yc_hy, yc_hg, yc_gl), p['w_branch'], p['w_out'], hc, gtc)
    return h, hc


GATHER_UNROLL = 8
GATHER_AHEAD = 2
GATHER_SLOTS = GATHER_AHEAD + 1


def _ffn_kernel(blk_exp_ref, n_used_ref, tok_ref, x_hbm, wg_ref, wu_ref, wd_ref, o_ref, xbuf, sem, wg_s, wu_s, wd_s):
    i = pl.program_id(0)
    n_used = n_used_ref[0]
    rows = o_ref.shape[0]

    def issue(step, slot):
        def body(rt, carry):
            base = pl.multiple_of(rt * GATHER_UNROLL, GATHER_UNROLL)
            for u in range(GATHER_UNROLL):
                src = tok_ref[step * rows + base + u]
                pltpu.make_async_copy(x_hbm.at[pl.ds(src, 1)], xbuf.at[slot, pl.ds(base + u, 1)],
                                      sem.at[slot]).start(priority=u % 2)
            return carry

        lax.fori_loop(0, rows // GATHER_UNROLL, body, 0)

    for s in range(GATHER_AHEAD):
        @pl.when(jnp.logical_and(i == 0, s < n_used))
        def _(s=s):
            issue(s, s % GATHER_SLOTS)

    @pl.when(i + GATHER_AHEAD < n_used)
    def _():
        issue(i + GATHER_AHEAD, (i + GATHER_AHEAD) % GATHER_SLOTS)

    new_expert = jnp.logical_or(i == 0, blk_exp_ref[i] != blk_exp_ref[jnp.maximum(i - 1, 0)])

    @pl.when(jnp.logical_and(i < n_used, new_expert))
    def _():
        wg_s[...] = wg_ref[0].astype(jnp.bfloat16)
        wu_s[...] = wu_ref[0].astype(jnp.bfloat16)
        wd_s[...] = wd_ref[0].astype(jnp.bfloat16)

    @pl.when(i < n_used)
    def _():
        slot = i % GATHER_SLOTS
        pltpu.make_async_copy(xbuf.at[slot], xbuf.at[slot], sem.at[slot]).wait()
        x = xbuf[slot].astype(jnp.bfloat16)
        hg = jnp.dot(x, wg_s[...], preferred_element_type=jnp.float32)
        hu = jnp.dot(x, wu_s[...], preferred_element_type=jnp.float32)
        act = (hg * jax.nn.sigmoid(hg) * hu).astype(jnp.bfloat16)
        o_ref[...] = jnp.dot(act, wd_s[...], preferred_element_type=jnp.float32)

    @pl.when(i >= n_used)
    def _():
        o_ref[...] = jnp.zeros_like(o_ref)


def _grouped_ffn(x, buf_tok, blk_exp, n_used, layer, w_gate, w_up, w_down):
    p_len = buf_tok.shape[0]
    d = x.shape[1]
    n_blk = p_len // MOE_BLOCK
    grid_spec = pltpu.PrefetchScalarGridSpec(
        num_scalar_prefetch=3,
        grid=(n_blk,),
        in_specs=[pl.BlockSpec(memory_space=pl.ANY),
                  pl.BlockSpec((None, 1, d, D_FF), lambda i, be, nu, tk: (layer, be[i], 0, 0)),
                  pl.BlockSpec((None, 1, d, D_FF), lambda i, be, nu, tk: (layer, be[i], 0, 0)),
                  pl.BlockSpec((None, 1, D_FF, d), lambda i, be, nu, tk: (layer, be[i], 0, 0))],
        out_specs=pl.BlockSpec((MOE_BLOCK, d), lambda i, be, nu, tk: (i, 0)),
        scratch_shapes=[pltpu.VMEM((GATHER_SLOTS, MOE_BLOCK, d), jnp.float32), pltpu.SemaphoreType.DMA((GATHER_SLOTS,)),
                        pltpu.VMEM((d, D_FF), jnp.bfloat16), pltpu.VMEM((d, D_FF), jnp.bfloat16),
                        pltpu.VMEM((D_FF, d), jnp.bfloat16)],
    )
    return pl.pallas_call(
        _ffn_kernel,
        grid_spec=grid_spec,
        out_shape=jax.ShapeDtypeStruct((p_len, d), jnp.float32),
        compiler_params=_cparams("arbitrary"),
        name="moe_grouped_ffn",
    )(blk_exp, n_used, buf_tok, x, w_gate, w_up, w_down)


ROUTER_COLS = 128


ROUTE_TOKENS = 256
ROUTE_OUT = (0, 1, 2, 3, 4, 5)


def _route_kernel(lg_ref, tril_ref, o_ref, cnt_ref, carry):
    i = pl.program_id(0)

    @pl.when(i == 0)
    def _():
        carry[...] = jnp.zeros_like(carry)

    x = lg_ref[...]
    lane = lax.broadcasted_iota(jnp.int32, x.shape, 1).astype(jnp.float32)
    neg = jnp.float32(-jnp.inf)
    far = jnp.float32(ROUTER_COLS)
    red_max = lambda a: jnp.max(a, axis=1, keepdims=True)
    red_min = lambda a: jnp.min(a, axis=1, keepdims=True)
    red_sum = lambda a: jnp.sum(a, axis=1, keepdims=True)
    gmask = lane < N_GROUPS
    gl = jnp.where(gmask, x, neg)
    gmax = red_max(gl)
    p_top = 1.0 / red_sum(jnp.where(gmask, jnp.exp(x - gmax), 0.0))
    grp = red_min(jnp.where(gl == gmax, lane, far))
    lo = N_GROUPS + EXP_PER_GROUP * grp
    emask = jnp.logical_and(lane >= lo, lane < lo + EXP_PER_GROUP)
    el = jnp.where(emask, x, neg)
    ee = jnp.where(emask, jnp.exp(x - red_max(el)), 0.0)
    prob = ee / red_sum(ee)
    p1 = red_max(prob)
    i1 = red_min(jnp.where(jnp.logical_and(emask, prob == p1), lane, far))
    rest = jnp.where(jnp.logical_and(emask, lane != i1), prob, -1.0)
    p2 = red_max(rest)
    i2 = red_min(jnp.where(rest == p2, lane, far))
    w1 = p_top * p1 / (p1 + p2)
    w2 = p_top * p2 / (p1 + p2)
    pick1 = (lane == i1).astype(jnp.float32)
    pick2 = (lane == i2).astype(jnp.float32)
    picks = pick1 + pick2
    before = jnp.dot(tril_ref[...], picks.astype(jnp.bfloat16), preferred_element_type=jnp.float32) + carry[...]
    r1 = red_sum(pick1 * before)
    r2 = red_sum(pick2 * before)
    carry[...] += jnp.sum(picks, axis=0, keepdims=True)
    cnt_ref[...] = carry[...]
    out = jnp.zeros_like(x)
    for col, val in zip(ROUTE_OUT, (i1 - N_GROUPS, i2 - N_GROUPS, w1, w2, r1, r2)):
        out = jnp.where(lane == col, val, out)
    o_ref[...] = out


def _route(logits):
    assert TOP_K == 2
    n = logits.shape[0]
    t = min(ROUTE_TOKENS, n)
    tril = jnp.asarray(np.tril(np.ones((t, t), np.float32), -1), jnp.bfloat16)
    out, cnt = pl.pallas_call(
        _route_kernel,
        grid=(n // t,),
        in_specs=[pl.BlockSpec((t, ROUTER_COLS), lambda i: (i, 0)), pl.BlockSpec((t, t), lambda i: (0, 0))],
        out_specs=[pl.BlockSpec((t, ROUTER_COLS), lambda i: (i, 0)), pl.BlockSpec((1, ROUTER_COLS), lambda i: (0, 0))],
        out_shape=[jax.ShapeDtypeStruct((n, ROUTER_COLS), jnp.float32),
                   jax.ShapeDtypeStruct((1, ROUTER_COLS), jnp.float32)],
        scratch_shapes=[pltpu.VMEM((1, ROUTER_COLS), jnp.float32)],
        compiler_params=_cparams("arbitrary"),
        name="moe_route",
    )(logits, tril)
    expert = out[:, 0:2].astype(jnp.int32)
    weight = out[:, 2:4]
    rank = out[:, 4:6].astype(jnp.int32)
    counts = cnt[0, N_GROUPS:N_GROUPS + N_EXPERTS].astype(jnp.int32)
    return expert, weight, rank, counts


def _hier_moe(h, norm_w, shift, scale, gt, p, final_w=None):
    n, d = h.shape
    pad = ROUTER_COLS - N_GROUPS - N_EXPERTS
    w_r = _bf(jnp.concatenate([p['w_rg'], p['w_re'], jnp.zeros((d, pad), jnp.float32)], axis=1))
    b_r = jnp.concatenate([p['b_rg'], p['b_re'], jnp.zeros((pad,), jnp.float32)]).reshape(1, ROUTER_COLS)
    xb, logits = _norm_mod(h, norm_w, shift, scale, router=(w_r, b_r), out_dtype=jnp.float32)
    expert, weight, rank, counts = _route(logits)
    a = n * TOP_K
    padded = (counts + MOE_BLOCK - 1) // MOE_BLOCK * MOE_BLOCK
    pad_end = jnp.cumsum(padded)
    pad_off = pad_end - padded
    pos = (pad_off[expert] + rank).reshape(a)
    p_len = (a + N_EXPERTS * MOE_BLOCK + MOE_BLOCK - 1) // MOE_BLOCK * MOE_BLOCK
    n_blk = p_len // MOE_BLOCK
    tok_flat = jnp.arange(a, dtype=jnp.int32) // TOP_K
    buf_tok = (jnp.arange(p_len, dtype=jnp.int32) % n).at[pos].set(tok_flat)
    blk_start = jnp.arange(n_blk, dtype=jnp.int32) * MOE_BLOCK
    blk_exp = jnp.minimum(jnp.sum(pad_end[None, :] <= blk_start[:, None], axis=1), N_EXPERTS - 1).astype(jnp.int32)
    n_used = (pad_end[-1:] // MOE_BLOCK).astype(jnp.int32)
    y = _grouped_ffn(xb, buf_tok, blk_exp, n_used, p['layer'], p['w_gate'], p['w_up'], p['w_down'])
    return _moe_combine(y, pos, weight, h, gt, final_w)


def _combine_kernel(pos_ref, y_hbm, wts_ref, h_ref, gt_ref, fw_ref, o_ref, buf, sem, *, final):
    i = pl.program_id(0)
    tokens = h_ref.shape[0]

    def issue(step, slot):
        def body(rt, carry):
            base = pl.multiple_of(rt * GATHER_UNROLL, GATHER_UNROLL)
            for u in range(GATHER_UNROLL):
                for k in range(TOP_K):
                    src = pos_ref[(step * tokens + base + u) * TOP_K + k]
                    pltpu.make_async_copy(y_hbm.at[pl.ds(src, 1)], buf.at[slot, k, pl.ds(base + u, 1)],
                                          sem.at[slot]).start(priority=k % 2)
            return carry

        lax.fori_loop(0, tokens // GATHER_UNROLL, body, 0)

    @pl.when(i == 0)
    def _():
        issue(0, 0)

    @pl.when(i + 1 < pl.num_programs(0))
    def _():
        issue(i + 1, (i + 1) % 2)

    slot = i % 2
    pltpu.make_async_copy(buf.at[slot], buf.at[slot], sem.at[slot]).wait()
    rows = buf[slot]
    wts = wts_ref[...]
    acc = rows[0] * wts[:, 0:1]
    for k in range(1, TOP_K):
        acc = acc + rows[k] * wts[:, k:k + 1]
    out = h_ref[...] + gt_ref[...] * acc
    if final:
        out = out * lax.rsqrt(jnp.mean(out * out, axis=-1, keepdims=True) + NORM_EPS) * fw_ref[...]
    o_ref[...] = out


COMBINE_TOKENS = 256


def _moe_combine(y, pos, wts, h, gt, final_w=None):
    n, d = h.shape
    tokens = min(COMBINE_TOKENS, n)
    grid_spec = pltpu.PrefetchScalarGridSpec(
        num_scalar_prefetch=1,
        grid=(n // tokens,),
        in_specs=[pl.BlockSpec(memory_space=pl.ANY),
                  pl.BlockSpec((tokens, TOP_K), lambda i, pos: (i, 0)),
                  pl.BlockSpec((tokens, d), lambda i, pos: (i, 0)),
                  pl.BlockSpec((1, d), lambda i, pos: (0, 0)),
                  pl.BlockSpec((1, d), lambda i, pos: (0, 0))],
        out_specs=pl.BlockSpec((tokens, d), lambda i, pos: (i, 0)),
        scratch_shapes=[pltpu.VMEM((2, TOP_K, tokens, d), jnp.float32), pltpu.SemaphoreType.DMA((2,))],
    )
    return pl.pallas_call(
        functools.partial(_combine_kernel, final=final_w is not None),
        grid_spec=grid_spec,
        out_shape=jax.ShapeDtypeStruct((n, d), jnp.float32),
        compiler_params=_cparams("arbitrary"),
        name="moe_combine",
    )(pos, y, wts, h, gt, (gt if final_w is None else final_w.reshape(1, d)))


def kernel(x, c, ctx, c_ctx, w_mod, b_mod, norm_mix_w, norm_ffn_w, w_in, b_in, hy_conv_w, hy_conv_b, hy_f_w1, hy_f_b1, hy_f_w2, hy_f_b2, hy_f_w3, hy_f_b3, hy_f_freq, hy_skip, hg_lb_raw, hg_norm_w, gl_w_a2, gl_b_a, gl_norm_w, w_branch, w_out, w_rg, b_rg, w_re, b_re, w_gate, w_up, w_down, final_norm_w):
    assert x.shape[0] == 1
    depth = w_mod.shape[0]
    lb_all = jnp.cumsum(jax.nn.softmax(hg_lb_raw, axis=0), axis=0)
    lb_all = lb_all - lb_all[:1]
    h, hc = x[0], ctx[0]
    cc = jnp.concatenate([c, c_ctx[None, :]], axis=0)
    for l in range(depth):
        need_ctx = l < depth - 1
        mod = _matmul(_bf(jax.nn.silu(cc)), w_mod, b_mod[l], layer=l)
        sh1, sc1, gt1, sh2, sc2, gt2 = jnp.split(mod[0:1], 6, axis=-1)
        sh1c, sc1c, gt1c, sh2c, sc2c, gt2c = jnp.split(mod[1:2], 6, axis=-1)
        p = dict(w_in=w_in, b_in=b_in[l], hy_conv_w=hy_conv_w[l], hy_conv_b=hy_conv_b[l],
                 hy_f=(hy_f_w1[l], hy_f_b1[l], hy_f_w2[l], hy_f_b2[l], hy_f_w3[l], hy_f_b3[l], hy_f_freq[l]),
                 hy_skip=hy_skip[l], lb=lb_all[l], hg_norm_w=hg_norm_w[l], gl_w_a2=gl_w_a2[l], gl_b_a=gl_b_a[l],
                 gl_norm_w=gl_norm_w[l], w_branch=w_branch[l], w_out=w_out[l], w_rg=w_rg[l], b_rg=b_rg[l],
                 w_re=w_re[l], b_re=b_re[l], layer=l, w_gate=w_gate, w_up=w_up, w_down=w_down)
        u = _norm_mod(h, norm_mix_w[l], sh1, sc1)
        uc = _norm_mod(hc, norm_mix_w[l], sh1c, sc1c)
        h, hc = _mixer(h, hc, u, uc, gt1, gt1c, p, need_ctx)
        h = _hier_moe(h, norm_ffn_w[l], sh2, sc2, gt2, p, final_w=None if need_ctx else final_norm_w)
        if need_ctx:
            hc = _hier_moe(hc, norm_ffn_w[l], sh2c, sc2c, gt2c, p)
    return h[None]
```

```python
import functools
import math

import jax
import jax.numpy as jnp
import numpy as np
from jax import lax
from jax.experimental import pallas as pl
from jax.experimental.pallas import tpu as pltpu

D_MODEL = 2048
NORM_EPS = 1e-6

HY_C = D_MODEL // 2
HY_EMB = 33
HY_BANDS = (HY_EMB - 1) // 2
HY_DECAY_TARGET = 1e-2
HY_FAST_PCT = 0.3
HY_SLOW_PCT = 1.5
HY_MOD_SHIFT = 0.05

HG_HEADS = 8
HG_DK = 128
HG_DV = 128
HG_K = HG_HEADS * HG_DK
HG_V = HG_HEADS * HG_DV

GL_HEADS = 4
GL_DK = 128
GL_DV = 256
GL_K = GL_HEADS * GL_DK
GL_V = GL_HEADS * GL_DV
GL_RANK = 16
GL_TAU = 16.0

N_BRANCH = 3
HG_COLS = 3 * HG_K + 2 * HG_V
GL_COLS = 2 * GL_K + 2 * GL_V + 2 * GL_RANK
REC_COLS = HG_COLS + GL_COLS
HY_COLS = 3 * HY_C
MERGE_COLS = N_BRANCH * D_MODEL

COL_TILE = 512
GL_Q_OFF = HG_COLS
GL_V_OFF = GL_Q_OFF + 2 * GL_K
GL_A_OFF = GL_V_OFF + 2 * GL_V
HY_OFF = -(-(GL_A_OFF + 2 * GL_RANK) // HY_C) * HY_C
MG_OFF = HY_OFF + HY_COLS
Z_COLS = MG_OFF + MERGE_COLS
assert GL_Q_OFF % GL_K == 0 and GL_V_OFF % GL_V == 0 and GL_A_OFF % 128 == 0 and Z_COLS % COL_TILE == 0
assert MG_OFF % D_MODEL == 0
ZB_MG_OFF = 0
ZB_HY_OFF = MERGE_COLS

N_GROUPS = 4
EXP_PER_GROUP = 8
N_EXPERTS = N_GROUPS * EXP_PER_GROUP
TOP_K = 2
D_FF = D_MODEL // 4
MOE_BLOCK = 256

SCAN_CHUNK = 128
SCAN_HEAD_GROUP = 8
SCAN_CHUNKS_PER_STEP = 4
LOG2_E = 1.4426950408889634

VMEM_LIMIT_BYTES = 56 * 1024 * 1024


def _cparams(*sem):
    return pltpu.CompilerParams(dimension_semantics=sem, vmem_limit_bytes=VMEM_LIMIT_BYTES)


def _bf(a):
    return a.astype(jnp.bfloat16)


def _mm_kernel(x_ref, w_ref, b_ref, o_ref):
    acc = jnp.dot(x_ref[...], w_ref[...].astype(jnp.bfloat16), preferred_element_type=jnp.float32) + b_ref[...]
    o_ref[...] = acc.astype(o_ref.dtype)


def _matmul(x, w, bias=None, tm=512, tn=COL_TILE, out_dtype=jnp.float32, n_cols=None, layer=None, col_off=0,
            out_map=None):
    m, k = x.shape
    n = w.shape[-1] if n_cols is None else n_cols
    assert n_cols is None or n_cols % tn == 0
    assert layer is None or n % tn == 0
    tm = min(tm, -(-m // 8) * 8)
    mp = -(-m // tm) * tm
    np_ = -(-n // tn) * tn
    if bias is None:
        bias = jnp.zeros((n,), jnp.float32)
    if mp != m:
        x = jnp.pad(x, ((0, mp - m), (0, 0)))
    if np_ != n:
        w = jnp.pad(w, ((0, 0), (0, np_ - n)))
        bias = jnp.pad(bias, (0, np_ - n))
    if out_map is None:
        out_map = lambda j: j
    if layer is None:
        w_spec = pl.BlockSpec((k, tn), lambda i, j: (0, j + col_off))
    else:
        assert col_off == 0
        w_spec = pl.BlockSpec((None, k, tn), lambda i, j: (layer, 0, j))
    out = pl.pallas_call(
        _mm_kernel,
        grid=(mp // tm, np_ // tn),
        in_specs=[pl.BlockSpec((tm, k), lambda i, j: (i, 0)),
                  w_spec,
                  pl.BlockSpec((1, tn), lambda i, j: (0, j + col_off))],
        out_specs=pl.BlockSpec((tm, tn), lambda i, j: (i, out_map(j))),
        out_shape=jax.ShapeDtypeStruct((mp, np_), out_dtype),
        compiler_params=_cparams("parallel", "arbitrary"),
        name="dense_matmul",
    )(x, w, bias.reshape(1, -1))
    if mp != m or np_ != n:
        out = out[:m, :n]
    return out


def _norm_mod_kernel(*refs, with_router):
    if with_router:
        h_ref, w_ref, sh_ref, sc_ref, wr_ref, br_ref, o_ref, lg_ref = refs
    else:
        h_ref, w_ref, sh_ref, sc_ref, o_ref = refs
    x = h_ref[...]
    y = x * lax.rsqrt(jnp.mean(x * x, axis=-1, keepdims=True) + NORM_EPS) * w_ref[...]
    u = (y * (1.0 + sc_ref[...]) + sh_ref[...]).astype(jnp.bfloat16)
    o_ref[...] = u.astype(o_ref.dtype)
    if with_router:
        lg_ref[...] = jnp.dot(u, wr_ref[...], preferred_element_type=jnp.float32) + br_ref[...]


def _norm_mod(h, w, shift, scale, router=None, out_dtype=jnp.bfloat16):
    m, d = h.shape
    tm = min(1024, m)
    row = pl.BlockSpec((tm, d), lambda i: (i, 0))
    vec = pl.BlockSpec((1, d), lambda i: (0, 0))
    args = [h, w.reshape(1, d), shift, scale]
    specs = [row, vec, vec, vec]
    out_shape = [jax.ShapeDtypeStruct((m, d), out_dtype)]
    out_specs = [row]
    if router is not None:
        args += list(router)
        specs += [pl.BlockSpec(router[0].shape, lambda i: (0, 0)), pl.BlockSpec(router[1].shape, lambda i: (0, 0))]
        out_shape.append(jax.ShapeDtypeStruct((m, router[0].shape[1]), jnp.float32))
        out_specs.append(pl.BlockSpec((tm, router[0].shape[1]), lambda i: (i, 0)))
    out = pl.pallas_call(
        functools.partial(_norm_mod_kernel, with_router=router is not None),
        grid=(m // tm,),
        in_specs=specs,
        out_specs=out_specs,
        out_shape=out_shape,
        compiler_params=_cparams("parallel"),
        name="norm_modulate",
    )(*args)
    return out if router is not None else out[0]


def _scan_masks(c, reverse):
    t = np.arange(c)
    ms = [np.eye(c, dtype=np.float32)]
    for lvl in range(int(math.log2(c))):
        upper = ((t >> lvl) & 1).astype(bool)
        same = (t[:, None] >> (lvl + 1)) == (t[None, :] >> (lvl + 1))
        m = same & upper[:, None] & (~upper)[None, :]
        ms.append((m.T if reverse else m).astype(np.float32))
    tri = t[None, :] >= t[:, None] if reverse else t[None, :] <= t[:, None]
    return jnp.asarray(np.stack(ms)), jnp.asarray(tri.astype(np.float32), dtype=jnp.bfloat16)


def _level_arg(cum, lvl, reverse):
    c = cum.shape[0]
    blk = 1 << lvl
    if blk >= 8:
        pieces = []
        for gs in range(0, c, 2 * blk):
            ref = cum[gs + blk:gs + blk + 1, :]
            pieces.append(ref - cum[gs:gs + blk, :])
            pieces.append(cum[gs + blk:gs + 2 * blk, :] - ref)
        arg = jnp.concatenate(pieces, axis=0)
    else:
        c3 = cum.reshape(c // 8, 8, cum.shape[1])
        sub = lax.broadcasted_iota(jnp.int32, c3.shape, 1)
        ref_row = ((sub >> lvl) | 1) << lvl
        ref = None
        for r in range(blk, 8, 2 * blk):
            cand = jnp.broadcast_to(c3[:, r:r + 1, :], c3.shape)
            ref = cand if ref is None else jnp.where(ref_row == r, cand, ref)
        upper = ((sub >> lvl) & 1) == 1
        arg = jnp.where(upper, c3 - ref, ref - c3).reshape(cum.shape)
    return -arg if reverse else arg


def _dot_nt(a, b):
    return lax.dot_general(a, b, (((1,), (1,)), ((), ())), preferred_element_type=jnp.float32)


def _dot_tn(a, b):
    return lax.dot_general(a, b, (((0,), (0,)), ((), ())), preferred_element_type=jnp.float32)


def _sigmoid_parts(z):
    e = jnp.exp(-jnp.abs(z))
    r = 1.0 / (1.0 + e)
    return jnp.minimum(z, 0.0) - jnp.log(1.0 + e), jnp.where(z >= 0.0, e * r, r)


def _scan_kernel(*refs, mode, reverse, final, heads, dk, dv, c, cps):
    it = iter(refs)
    q_ref = next(it)
    k_ref = next(it)
    v_ref = next(it)
    if mode == "hg":
        lbp_ref = next(it)
    else:
        a_ref = next(it)
        wa_ref = next(it)
        ba_ref = next(it)
    s0_ref = next(it)
    masks_ref = next(it)
    tri_ref = next(it)
    if final:
        oprev_ref = next(it)
        gate_ref = next(it)
        nw_ref = next(it)
    o_ref = next(it)
    st_ref = next(it)

    @pl.when(pl.program_id(0) == 0)
    def _():
        st_ref[...] = s0_ref[...]

    for ci in (reversed(range(cps)) if reverse else range(cps)):
        rs = slice(ci * c, (ci + 1) * c)
        if mode == "gl":
            la_all = jnp.dot(a_ref[rs, :].astype(jnp.bfloat16), wa_ref[...],
                             preferred_element_type=jnp.float32) + ba_ref[...]
        tri = tri_ref[...]
        tot_row = 0 if reverse else c - 1
        n_lvl = int(math.log2(c))
        ksl = lambda h: slice(h * dk, (h + 1) * dk)
        vsl = lambda h: slice(h * dv, (h + 1) * dv)
        for h0 in range(0, heads, SCAN_HEAD_GROUP):
            group = range(h0, min(h0 + SCAN_HEAD_GROUP, heads))
            q_, k_, cum_, qb_, kb_, o_, sc_ = {}, {}, {}, {}, {}, {}, {}
            for h in group:
                q = q_ref[rs, ksl(h)]
                if mode == "hg":
                    log_sig, sig_neg = _sigmoid_parts(k_ref[rs, ksl(h)])
                    la = lbp_ref[0:1, ksl(h)]
                    lbb = lbp_ref[1:2, ksl(h)] + log_sig
                    g = jnp.maximum(la, lbb) + jnp.log(1.0 + jnp.exp(-jnp.abs(la - lbb)))
                    k = lbp_ref[2:3, ksl(h)] * sig_neg
                    q = q * jax.nn.sigmoid(q)
                else:
                    g = _sigmoid_parts(la_all[:, ksl(h)])[0] * (1.0 / GL_TAU)
                    k = k_ref[rs, ksl(h)]
                    q = q * (dk ** -0.5)
                g = g * LOG2_E
                g1 = g.astype(jnp.bfloat16)
                r1 = g - g1.astype(jnp.float32)
                g2 = r1.astype(jnp.bfloat16)
                g3 = (r1 - g2.astype(jnp.float32)).astype(jnp.bfloat16)
                cum_[h] = (jnp.dot(tri, g1, preferred_element_type=jnp.float32)
                           + jnp.dot(tri, g2, preferred_element_type=jnp.float32)
                           + jnp.dot(tri, g3, preferred_element_type=jnp.float32))
                q_[h], k_[h] = q, k
            for h in group:
                cum = cum_[h]
                tot = cum[tot_row:tot_row + 1, :]
                st = st_ref[h]
                v = v_ref[rs, vsl(h)].astype(jnp.bfloat16)
                o_[h] = _dot_nt((q_[h] * jnp.exp2(cum)).astype(jnp.bfloat16), st.astype(jnp.bfloat16))
                kt = (k_[h] * jnp.exp2(tot - cum)).astype(jnp.bfloat16)
                st_ref[h] = st * jnp.exp2(tot) + _dot_tn(v, kt)
                qb_[h] = q_[h].astype(jnp.bfloat16)
                kb_[h] = k_[h].astype(jnp.bfloat16)
                sc_[h] = masks_ref[0] * _dot_nt(qb_[h], kb_[h])
            for lvl in range(n_lvl):
                for h in group:
                    e = jnp.exp2(_level_arg(cum_[h], lvl, reverse)).astype(jnp.bfloat16)
                    sc_[h] = sc_[h] + masks_ref[1 + lvl] * _dot_nt(qb_[h] * e, kb_[h] * e)
            for h in group:
                v = v_ref[rs, vsl(h)].astype(jnp.bfloat16)
                o = o_[h] + jnp.dot(sc_[h].astype(jnp.bfloat16), v, preferred_element_type=jnp.float32)
                if final:
                    o = o + oprev_ref[rs, vsl(h)]
                    y = o * lax.rsqrt(jnp.mean(o * o, axis=-1, keepdims=True) + NORM_EPS) * nw_ref[...]
                    gt = gate_ref[rs, vsl(h)]
                    act = jax.nn.sigmoid(gt) if mode == "hg" else gt * jax.nn.sigmoid(gt)
                    o_ref[rs, vsl(h)] = (y * act).astype(o_ref.dtype)
                else:
                    o_ref[rs, vsl(h)] = o


def _scan_pass(mode, reverse, final, L, srcs, s0, params, final_srcs=(), norm_w=None):
    heads, dk, dv = (HG_HEADS, HG_DK, HG_DV) if mode == "hg" else (GL_HEADS, GL_DK, GL_DV)
    c = min(SCAN_CHUNK, L)
    cps = min(SCAN_CHUNKS_PER_STEP, L // c)
    rows = c * cps
    nb = L // rows
    row = (lambda i: nb - 1 - i) if reverse else (lambda i: i)
    masks, tri = _scan_masks(c, reverse)

    def const(shape):
        return pl.BlockSpec(shape, lambda i: (0,) * len(shape))

    def rowblock(width, cb):
        return pl.BlockSpec((rows, width), lambda i: (row(i), cb))

    args = [a for a, _, _ in srcs] + list(params) + [s0, masks, tri]
    specs = ([rowblock(w, cb) for _, w, cb in srcs] + [const(p.shape) for p in params]
             + [const(s0.shape), const(masks.shape), const(tri.shape)])
    if final:
        args += [a for a, _, _ in final_srcs] + [norm_w]
        specs += [rowblock(w, cb) for _, w, cb in final_srcs] + [const(norm_w.shape)]
    return pl.pallas_call(
        functools.partial(_scan_kernel, mode=mode, reverse=reverse, final=final, heads=heads, dk=dk, dv=dv, c=c,
                          cps=cps),
        grid=(nb,),
        in_specs=specs,
        out_specs=[pl.BlockSpec((rows, heads * dv), lambda i: (row(i), 0)), const((heads, dv, dk))],
        out_shape=[jax.ShapeDtypeStruct((L, heads * dv), jnp.bfloat16 if final else jnp.float32),
                   jax.ShapeDtypeStruct((heads, dv, dk), jnp.float32)],
        compiler_params=_cparams("arbitrary"),
        name=f"scan_{mode}_{'bwd' if reverse else 'fwd'}",
    )(*args)


def _hgrn2(z, L, lb, norm_w, s0_f, s0_b):
    lbp = lambda d: jnp.stack([jnp.log(lb[d]), jnp.log1p(-lb[d]), 1.0 - lb[d]])
    w = HG_K
    o_b, s_b = _scan_pass("hg", True, False, L, [(z, w, 0), (z, w, 2), (z, w, 3)], s0_b, [lbp(1)])
    y, s_f = _scan_pass("hg", False, True, L, [(z, w, 0), (z, w, 1), (z, w, 3)], s0_f, [lbp(0)],
                        final_srcs=[(o_b, HG_V, 0), (z, HG_V, 4)], norm_w=norm_w.reshape(1, HG_DV))
    return y, s_f, s_b


def _gla(z, za, L, w_a2, b_a, norm_w, s0_f, s0_b):
    def gate_params(d):
        wa = jnp.zeros((128, GL_K), jnp.float32).at[d * GL_RANK:(d + 1) * GL_RANK].set(w_a2[d])
        return [_bf(wa), b_a[d].reshape(1, GL_K)]

    srcs = [(z, GL_K, GL_Q_OFF // GL_K), (z, GL_K, GL_Q_OFF // GL_K + 1), (z, GL_V, GL_V_OFF // GL_V),
            (za, 128, 0)]
    o_b, s_b = _scan_pass("gl", True, False, L, srcs, s0_b, gate_params(1))
    y, s_f = _scan_pass("gl", False, True, L, srcs, s0_f, gate_params(0),
                        final_srcs=[(o_b, GL_V, 0), (z, GL_V, GL_V_OFF // GL_V + 1)], norm_w=norm_w.reshape(1, GL_DV))
    return y, s_f, s_b


HY_FEAT_PAD = 128


def _hy_filter_kernel(f_ref, w1_ref, b1_ref, fq_ref, w2_ref, b2_ref, w3_ref, b3_ref, dl_ref, h_ref, s_ref):
    i = pl.program_id(0)
    f = f_ref[...]
    fq = fq_ref[...]
    a = jnp.sin(fq * (jnp.dot(f.astype(jnp.bfloat16), w1_ref[...], preferred_element_type=jnp.float32) + b1_ref[...]))
    a = jnp.sin(fq * (jnp.dot(a.astype(jnp.bfloat16), w2_ref[...], preferred_element_type=jnp.float32) + b2_ref[...]))
    hh = jnp.dot(a.astype(jnp.bfloat16), w3_ref[...], preferred_element_type=jnp.float32) + b3_ref[...]
    hh = hh * (jnp.exp(-f[:, 0:1] * dl_ref[...]) + HY_MOD_SHIFT)
    h_ref[...] = hh
    part = jnp.sum(jnp.abs(hh).reshape(hh.shape[0] // 8, 8, hh.shape[1]), axis=0)

    @pl.when(i == 0)
    def _():
        s_ref[...] = part

    @pl.when(i > 0)
    def _():
        s_ref[...] += part


def _hyena_filters(L, w1, b1, w2, b2, w3, b3, freq):
    t = jnp.linspace(0.0, 1.0, L, dtype=jnp.float32)[:, None]
    ang = 2.0 * math.pi * jnp.arange(L, dtype=jnp.float32)[:, None] / L
    bands = jnp.linspace(1e-4, HY_BANDS - 1, HY_BANDS, dtype=jnp.float32)[None, :]
    feats = jnp.concatenate([t, jnp.cos(bands * ang), -jnp.sin(bands * ang),
                             jnp.zeros((L, HY_FEAT_PAD - HY_EMB), jnp.float32)], axis=-1)
    deltas = jnp.abs(jnp.linspace(math.log(HY_DECAY_TARGET) / HY_SLOW_PCT, math.log(HY_DECAY_TARGET) / HY_FAST_PCT,
                                  HY_C, dtype=jnp.float32))
    fh = w1.shape[1]
    padm = lambda a, r, c: _bf(jnp.pad(a, ((0, r - a.shape[0]), (0, c - a.shape[1]))))
    padv = lambda a: jnp.pad(a, (0, HY_FEAT_PAD - a.shape[0])).reshape(1, HY_FEAT_PAD)
    tm = min(512, L)
    const = lambda r, c: pl.BlockSpec((r, c), lambda i: (0, 0))
    hfil, sums = pl.pallas_call(
        _hy_filter_kernel,
        grid=(L // tm,),
        in_specs=[pl.BlockSpec((tm, HY_FEAT_PAD), lambda i: (i, 0)),
                  const(HY_FEAT_PAD, HY_FEAT_PAD), const(1, HY_FEAT_PAD), const(1, HY_FEAT_PAD),
                  const(HY_FEAT_PAD, HY_FEAT_PAD), const(1, HY_FEAT_PAD),
                  const(HY_FEAT_PAD, 2 * HY_C), const(1, 2 * HY_C), const(1, 2 * HY_C)],
        out_specs=[pl.BlockSpec((tm, 2 * HY_C), lambda i: (i, 0)), const(8, 2 * HY_C)],
        out_shape=[jax.ShapeDtypeStruct((L, 2 * HY_C), jnp.float32), jax.ShapeDtypeStruct((8, 2 * HY_C), jnp.float32)],
        compiler_params=_cparams("arbitrary"),
        name="hyena_filters",
    )(feats, padm(w1, HY_FEAT_PAD, HY_FEAT_PAD), padv(b1), padv(freq), padm(w2, HY_FEAT_PAD, HY_FEAT_PAD), padv(b2),
      padm(w3, HY_FEAT_PAD, 2 * HY_C), b3.reshape(1, 2 * HY_C), jnp.tile(deltas, 2).reshape(1, 2 * HY_C))
    assert fh <= HY_FEAT_PAD
    inorm = 1.0 / jnp.sum(sums, axis=0)
    return hfil, inorm.reshape(2, HY_C)


def _hy_pre_kernel(x0_ref, x1_ref, v_ref, x0p_ref, x1p_ref, vp_ref, x0n_ref, x1n_ref, vn_ref, w_ref, b_ref,
                   vo_ref, x0o_ref):
    i = pl.program_id(0)
    first = i == 0
    last = i == pl.num_programs(0) - 1
    tm = x0_ref.shape[0]
    row = lax.broadcasted_iota(jnp.int32, x0_ref.shape, 0)

    def conv(x_ref, p_ref, n_ref, g):
        x = x_ref[...].astype(jnp.float32)
        cs = slice(g * HY_C, (g + 1) * HY_C)
        prev_row = jnp.where(first, 0.0, p_ref[...].astype(jnp.float32)[HALO_ROWS - 1:HALO_ROWS, :])
        next_row = jnp.where(last, 0.0, n_ref[...].astype(jnp.float32)[0:1, :])
        xp = jnp.where(row == 0, prev_row, pltpu.roll(x, 1, 0))
        xn = jnp.where(row == tm - 1, next_row, pltpu.roll(x, tm - 1, 0))
        return w_ref[0:1, cs] * xp + w_ref[1:2, cs] * x + w_ref[2:3, cs] * xn + b_ref[0:1, cs]

    x0 = conv(x0_ref, x0p_ref, x0n_ref, 0)
    x1 = conv(x1_ref, x1p_ref, x1n_ref, 1)
    v = conv(v_ref, vp_ref, vn_ref, 2)
    vo_ref[...] = (v * x1).astype(vo_ref.dtype)
    x0o_ref[...] = x0.astype(x0o_ref.dtype)


HALO_ROWS = 16


def _hy_pre(z, L, conv_w, conv_b):
    tm = min(512, L)
    nbh = L // HALO_ROWS
    cb = ZB_HY_OFF // HY_C
    main = lambda g: pl.BlockSpec((tm, HY_C), lambda i: (i, cb + g))
    prev = lambda g: pl.BlockSpec((HALO_ROWS, HY_C), lambda i: (jnp.maximum(i * (tm // HALO_ROWS) - 1, 0), cb + g))
    nxt = lambda g: pl.BlockSpec((HALO_ROWS, HY_C),
                                 lambda i: (jnp.minimum((i + 1) * (tm // HALO_ROWS), nbh - 1), cb + g))
    const = lambda a: pl.BlockSpec(a.shape, lambda i: (0, 0))
    cbias = conv_b.reshape(1, HY_COLS)
    return pl.pallas_call(
        _hy_pre_kernel,
        grid=(L // tm,),
        in_specs=[main(0), main(1), main(2), prev(0), prev(1), prev(2), nxt(0), nxt(1), nxt(2),
                  const(conv_w), const(cbias)],
        out_specs=[pl.BlockSpec((tm, HY_C), lambda i: (i, 0))] * 2,
        out_shape=[jax.ShapeDtypeStruct((L, HY_C), jnp.float32), jax.ShapeDtypeStruct((L, HY_C), jnp.bfloat16)],
        compiler_params=_cparams("parallel"),
        name="hyena_short_conv",
    )(z, z, z, z, z, z, z, z, z, conv_w, cbias)


HY_N1 = 128
HY_TWO_STAGE_MIN_L = 1024


def _dft_outer_table(n1, cols):
    ang = -2.0 * np.pi * np.outer(np.arange(n1 // 2) + 0.5, np.arange(cols)) / n1
    return jnp.asarray(np.concatenate([np.cos(ang), np.sin(ang)], axis=0), jnp.bfloat16)


def _dft_inner_table(n1, n2):
    j2 = np.arange(n2)
    f_ang = -2.0 * np.pi * np.outer(np.arange(n2), j2) / n2
    tw_ang = -2.0 * np.pi * np.outer(np.arange(n1 // 2) + 0.5, j2) / (n1 * n2)
    fr, fi = jnp.asarray(np.cos(f_ang), jnp.float32), jnp.asarray(np.sin(f_ang), jnp.float32)
    twr, twi = jnp.asarray(np.cos(tw_ang), jnp.float32), jnp.asarray(np.sin(tw_ang), jnp.float32)
    mr = fr[None] * twr[:, None, :] - fi[None] * twi[:, None, :]
    mi = fr[None] * twi[:, None, :] + fi[None] * twr[:, None, :]
    return _bf(jnp.concatenate([jnp.concatenate([mr, -mi], axis=2), jnp.concatenate([mi, mr], axis=2)], axis=1))


def _spectral_product(xv, xh, inorm, half):
    inf, inb = inorm[0:1, :], inorm[1:2, :]
    gr = xh[:half, :HY_C] * inf + xh[:half, HY_C:] * inb
    gi = xh[half:, :HY_C] * inf - xh[half:, HY_C:] * inb
    xr, xi = xv[:half], xv[half:]
    return jnp.concatenate([xr * gr - xi * gi, xr * gi + xi * gr], axis=0).astype(jnp.bfloat16)


HY_SLABS = 4


def _hy_spec_kernel(r_ref, avr_ref, avi_ref, ahr_ref, ahi_ref, inorm_ref, br_ref, bi_ref):
    n2 = avr_ref.shape[1]
    slabs = range(r_ref.shape[0])
    xv = [jnp.dot(r_ref[s], jnp.concatenate([avr_ref[s], avi_ref[s]], axis=0), preferred_element_type=jnp.float32)
          for s in slabs]
    xh = [jnp.dot(r_ref[s], jnp.concatenate([ahr_ref[s], ahi_ref[s]], axis=0), preferred_element_type=jnp.float32)
          for s in slabs]
    y = [_spectral_product(xv[s], xh[s], inorm_ref[...], n2) for s in slabs]
    for s in slabs:
        b = _dot_tn(r_ref[s], y[s])
        br_ref[s] = b[:n2].astype(br_ref.dtype)
        bi_ref[s] = b[n2:].astype(bi_ref.dtype)


def _hy_spec(r, av, ah, inorm, n1, n2):
    av3 = av.reshape(n1, n2, HY_C)
    ah3 = ah.reshape(n1, n2, 2 * HY_C)
    h1 = n1 // 2
    sl = HY_SLABS
    assert h1 % sl == 0
    out = jax.ShapeDtypeStruct((h1, n2, HY_C), jnp.bfloat16)
    return pl.pallas_call(
        _hy_spec_kernel,
        grid=(h1 // sl,),
        in_specs=[pl.BlockSpec((sl, 2 * n2, 2 * n2), lambda k: (k, 0, 0)),
                  pl.BlockSpec((sl, n2, HY_C), lambda k: (k, 0, 0)),
                  pl.BlockSpec((sl, n2, HY_C), lambda k: (k + h1 // sl, 0, 0)),
                  pl.BlockSpec((sl, n2, 2 * HY_C), lambda k: (k, 0, 0)),
                  pl.BlockSpec((sl, n2, 2 * HY_C), lambda k: (k + h1 // sl, 0, 0)),
                  pl.BlockSpec((2, HY_C), lambda k: (0, 0))],
        out_specs=[pl.BlockSpec((sl, n2, HY_C), lambda k: (k, 0, 0))] * 2,
        out_shape=[out, out],
        compiler_params=_cparams("parallel"),
        name="hyena_spectral",
    )(r, av3, av3, ah3, ah3, inorm)


def _hy_spec_direct_kernel(xv_ref, xh_ref, inorm_ref, yr_ref, yi_ref):
    half = yr_ref.shape[0]
    y = _spectral_product(xv_ref[...].astype(jnp.float32), xh_ref[...].astype(jnp.float32), inorm_ref[...], half)
    yr_ref[...] = y[:half]
    yi_ref[...] = y[half:]


def _hy_spec_direct(xv, xh, inorm, L):
    full = lambda a: pl.BlockSpec(a.shape, lambda i: (0, 0))
    out = jax.ShapeDtypeStruct((L, HY_C), jnp.bfloat16)
    return pl.pallas_call(
        _hy_spec_direct_kernel,
        grid=(1,),
        in_specs=[full(xv), full(xh), full(inorm)],
        out_specs=[pl.BlockSpec((L, HY_C), lambda i: (0, 0))] * 2,
        out_shape=[out, out],
        compiler_params=_cparams("arbitrary"),
        name="hyena_spectral_direct",
    )(xv, xh, inorm)


def _hy_post_kernel(tr_ref, ti_ref, br_ref, bi_ref, v_ref, x0_ref, skip_ref, o_ref, *, scale):
    acc = (jnp.dot(tr_ref[...], br_ref[...], preferred_element_type=jnp.float32)
           + jnp.dot(ti_ref[...], bi_ref[...], preferred_element_type=jnp.float32))
    y = (acc * scale + v_ref[...].astype(jnp.float32) * skip_ref[...]) * x0_ref[...].astype(jnp.float32)
    o_ref[...] = y.astype(o_ref.dtype)


def _hy_post(t_fwd, b_r, b_i, v, x0, skip, L, h1, n2):
    ncol = n2 * HY_C
    tn = min(4096, ncol)
    tr_t = t_fwd[:h1].T
    ti_t = t_fwd[h1:].T
    skip_t = jnp.tile(skip, tn // HY_C).reshape(1, tn)
    col = lambda rows: pl.BlockSpec((rows, tn), lambda j: (0, j))
    rows_out = tr_t.shape[0]
    y = pl.pallas_call(
        functools.partial(_hy_post_kernel, scale=1.0 / L),
        grid=(ncol // tn,),
        in_specs=[pl.BlockSpec(tr_t.shape, lambda j: (0, 0)), pl.BlockSpec(ti_t.shape, lambda j: (0, 0)),
                  col(h1), col(h1), col(rows_out), col(rows_out), pl.BlockSpec((1, tn), lambda j: (0, 0))],
        out_specs=col(rows_out),
        out_shape=jax.ShapeDtypeStruct((rows_out, ncol), jnp.bfloat16),
        compiler_params=_cparams("parallel"),
        name="hyena_inverse",
    )(tr_t, ti_t, b_r.reshape(h1, ncol), b_i.reshape(h1, ncol), v.reshape(rows_out, ncol),
      x0.reshape(rows_out, ncol), skip_t)
    return y.reshape(L, HY_C)


SUBLANE = 8
HY_COL_TILE = 512


def _dft_outer_kron(n1):
    h1 = n1 // 2
    ang = -2.0 * np.pi * np.outer(np.arange(h1) + 0.5, np.arange(h1)) / n1
    eye = np.eye(SUBLANE)
    t_r, t_i = np.cos(ang), np.sin(ang)
    fwd = np.kron(np.concatenate([t_r, t_i], axis=0), eye)
    inv = np.concatenate([np.kron(t_r.T, eye), np.kron(t_i.T, eye)], axis=1)
    return jnp.asarray(fwd, jnp.bfloat16), jnp.asarray(inv, jnp.bfloat16)


def _hy_outer_fwd_kernel(t_ref, x_ref, o_ref):
    x = x_ref[...]
    rows_in, rows_out = x.shape[0], t_ref.shape[0] // SUBLANE
    cw = x.shape[2]
    parts = []
    for s in range(0, x.shape[1], SUBLANE):
        xs = x[:, s:s + SUBLANE, :].reshape(rows_in * SUBLANE, cw).astype(jnp.bfloat16)
        r = jnp.dot(t_ref[...], xs, preferred_element_type=jnp.float32)
        parts.append(r.reshape(rows_out, SUBLANE, cw))
    o_ref[...] = jnp.concatenate(parts, axis=1).astype(o_ref.dtype)


def _hy_outer_fwd(t_kron, x3):
    h1, n2, w = x3.shape
    n1 = 2 * h1
    blk = 2 * SUBLANE
    return pl.pallas_call(
        _hy_outer_fwd_kernel,
        grid=(n2 // blk, w // HY_COL_TILE),
        in_specs=[pl.BlockSpec(t_kron.shape, lambda j, cc: (0, 0)),
                  pl.BlockSpec((h1, blk, HY_COL_TILE), lambda j, cc: (0, j, cc))],
        out_specs=pl.BlockSpec((n1, blk, HY_COL_TILE), lambda j, cc: (0, j, cc)),
        out_shape=jax.ShapeDtypeStruct((n1, n2, w), jnp.bfloat16),
        compiler_params=_cparams("parallel", "parallel"),
        name="hyena_outer_dft",
    )(t_kron, x3)


def _hy_outer_inv_kernel(t_ref, br_ref, bi_ref, v_ref, x0_ref, skip_ref, o_ref, *, scale):
    br = br_ref[...].astype(jnp.float32)
    bi = bi_ref[...].astype(jnp.float32)
    v = v_ref[...]
    x0 = x0_ref[...].astype(jnp.float32)
    h1, _, cw = br.shape
    parts = []
    for s in range(0, br.shape[1], SUBLANE):
        sl = slice(s, s + SUBLANE)
        b = jnp.concatenate([br[:, sl, :].reshape(h1 * SUBLANE, cw), bi[:, sl, :].reshape(h1 * SUBLANE, cw)], axis=0)
        r = jnp.dot(t_ref[...], b.astype(jnp.bfloat16), preferred_element_type=jnp.float32)
        parts.append((r.reshape(h1, SUBLANE, cw) * scale + v[:, sl, :] * skip_ref[...]) * x0[:, sl, :])
    o_ref[...] = jnp.concatenate(parts, axis=1).astype(o_ref.dtype)


def _hy_outer_inv(t_kron_inv, b_r, b_i, v3, x03, skip, L):
    h1, n2, w = b_r.shape
    blk = 2 * SUBLANE
    tile = pl.BlockSpec((h1, blk, HY_COL_TILE), lambda j, cc: (0, j, cc))
    return pl.pallas_call(
        functools.partial(_hy_outer_inv_kernel, scale=1.0 / L),
        grid=(n2 // blk, w // HY_COL_TILE),
        in_specs=[pl.BlockSpec(t_kron_inv.shape, lambda j, cc: (0, 0)), tile, tile, tile, tile,
                  pl.BlockSpec((1, 1, HY_COL_TILE), lambda j, cc: (0, 0, cc))],
        out_specs=tile,
        out_shape=jax.ShapeDtypeStruct((h1, n2, w), jnp.bfloat16),
        compiler_params=_cparams("parallel", "parallel"),
        name="hyena_outer_idft",
    )(t_kron_inv, b_r, b_i, v3, x03, skip.reshape(1, 1, w))


def _hyena(z, L, conv_w, conv_b, fparams, skip):
    v, x0 = _hy_pre(z, L, conv_w, conv_b)
    hfil, inorm = _hyena_filters(L, *fparams)
    if L >= HY_TWO_STAGE_MIN_L:
        n1 = HY_N1
        n2 = 2 * L // n1
        h1 = n1 // 2
        assert n2 % (2 * SUBLANE) == 0
        t_kron, t_kron_inv = _dft_outer_kron(n1)
        v3 = v.reshape(h1, n2, HY_C)
        av = _hy_outer_fwd(t_kron, v3)
        ah = _hy_outer_fwd(t_kron, hfil.reshape(h1, n2, 2 * HY_C))
        b_r, b_i = _hy_spec(_dft_inner_table(n1, n2), av, ah, inorm, n1, n2)
        return _hy_outer_inv(t_kron_inv, b_r, b_i, v3, x0.reshape(h1, n2, HY_C), skip, L).reshape(L, HY_C)
    t_fwd = _dft_outer_table(2 * L, L)
    xv = _matmul(t_fwd, _bf(v), tm=2 * L, tn=HY_C, out_dtype=jnp.bfloat16)
    xh = _matmul(t_fwd, _bf(hfil), tm=2 * L, tn=HY_C, out_dtype=jnp.bfloat16)
    y_r, y_i = _hy_spec_direct(xv, xh, inorm, L)
    return _hy_post(t_fwd, y_r, y_i, v, x0, skip, L, L, 1)


def _merge_kernel(yh_ref, yg_ref, yl_ref, gate_h_ref, gate_g_ref, gate_l_ref, wb_ref, o_ref):
    acc = None
    for br, (y_ref, g_ref) in enumerate(((yh_ref, gate_h_ref), (yg_ref, gate_g_ref), (yl_ref, gate_l_ref))):
        t = (jnp.dot(y_ref[...], wb_ref[br], preferred_element_type=jnp.float32)
             * jax.nn.sigmoid(g_ref[...].astype(jnp.float32)))
        acc = t if acc is None else acc + t
    o_ref[...] = acc.astype(o_ref.dtype)


def _proj_residual_kernel(m_ref, w_ref, h_ref, gt_ref, o_ref):
    o_ref[...] = h_ref[...] + gt_ref[...] * jnp.dot(m_ref[...], w_ref[...], preferred_element_type=jnp.float32)


MERGE_ROWS = 512


def _merge(z, L, ys, w_branch, w_out, h, gt):
    tm = min(MERGE_ROWS, L)
    gb = ZB_MG_OFF // D_MODEL
    ybs = pl.BlockSpec((tm, HY_C), lambda i: (i, 0))
    gate = lambda br: pl.BlockSpec((tm, D_MODEL), lambda i: (i, gb + br))
    row = pl.BlockSpec((tm, D_MODEL), lambda i: (i, 0))
    merged = pl.pallas_call(
        _merge_kernel,
        grid=(L // tm,),
        in_specs=[ybs, ybs, ybs, gate(0), gate(1), gate(2),
                  pl.BlockSpec((N_BRANCH, HY_C, D_MODEL), lambda i: (0, 0, 0), pipeline_mode=pl.Buffered(1))],
        out_specs=row,
        out_shape=jax.ShapeDtypeStruct((L, D_MODEL), jnp.bfloat16),
        compiler_params=_cparams("parallel"),
        name="branch_merge",
    )(ys[0], ys[1], ys[2], z, z, z, _bf(w_branch))
    return pl.pallas_call(
        _proj_residual_kernel,
        grid=(L // tm,),
        in_specs=[row, pl.BlockSpec((D_MODEL, D_MODEL), lambda i: (0, 0), pipeline_mode=pl.Buffered(1)), row,
                  pl.BlockSpec((1, D_MODEL), lambda i: (0, 0))],
        out_specs=row,
        out_shape=jax.ShapeDtypeStruct((L, D_MODEL), jnp.float32),
        compiler_params=_cparams("parallel"),
        name="out_proj_residual",
    )(merged, _bf(w_out), h, gt)


def _pad_cols(a):
    pad = lambda n: jnp.zeros(a.shape[:-1] + (n,), a.dtype)
    return jnp.concatenate([a[..., :REC_COLS], pad(HY_OFF - REC_COLS), a[..., REC_COLS:]], axis=-1)


W_TILE = 1024
W_SHIFT = HY_OFF - REC_COLS
W_ROW_OFF = W_TILE - W_SHIFT
assert 0 < W_SHIFT <= W_TILE and W_ROW_OFF % 8 == 0 and HY_OFF % W_TILE == 0 and Z_COLS % W_TILE == 0


def _w_in_prep_kernel(a_ref, b_ref, o_ref):
    j = pl.program_id(0)
    shifted = j >= HY_OFF // W_TILE

    @pl.when(jnp.logical_not(shifted))
    def _():
        o_ref[...] = a_ref[...].T.astype(o_ref.dtype)

    @pl.when(shifted)
    def _():
        window = jnp.concatenate([a_ref[...], b_ref[...]], axis=0)
        o_ref[...] = window[W_ROW_OFF:W_ROW_OFF + W_TILE].T.astype(o_ref.dtype)


def _w_in_prep(w_in, layer):
    w_t = jnp.swapaxes(w_in, 1, 2)
    _, n, k = w_t.shape
    first_shifted = HY_OFF // W_TILE
    a_idx = lambda j: jnp.where(j < first_shifted, j, j - 1)
    tail_blocks = W_TILE // W_ROW_OFF
    assert W_TILE % W_ROW_OFF == 0 and n % W_ROW_OFF == 0
    return pl.pallas_call(
        _w_in_prep_kernel,
        grid=(Z_COLS // W_TILE,),
        in_specs=[pl.BlockSpec((None, W_TILE, k), lambda j: (layer, a_idx(j), 0)),
                  pl.BlockSpec((None, W_ROW_OFF, k), lambda j: (layer, (a_idx(j) + 1) * tail_blocks, 0))],
        out_specs=pl.BlockSpec((k, W_TILE), lambda j: (0, j)),
        out_shape=jax.ShapeDtypeStruct((k, Z_COLS), jnp.bfloat16),
        compiler_params=_cparams("parallel"),
        name="w_in_relayout",
    )(w_t, w_t)


def _mixer(h, hc, u, uc, gt, gtc, p, need_ctx):
    L, Lc = u.shape[0], uc.shape[0]
    w_in = _w_in_prep(p['w_in'], p['layer'])
    b_in = _pad_cols(p['b_in'])
    rest = dict(tm=1024, tn=W_TILE, n_cols=Z_COLS - HY_OFF, col_off=HY_OFF // W_TILE, out_dtype=jnp.bfloat16,
                out_map=lambda j: jnp.where(j < HY_COLS // W_TILE, j + MERGE_COLS // W_TILE, j - HY_COLS // W_TILE))
    gates = dict(tm=1024, tn=128, n_cols=128, col_off=GL_A_OFF // 128)
    z = _matmul(u, w_in, b_in, tm=1024, tn=W_TILE, n_cols=GL_A_OFF)
    za = _matmul(u, w_in, b_in, **gates)
    zb = _matmul(u, w_in, b_in, **rest)
    zc = _matmul(uc, w_in, b_in, tm=1024, tn=W_TILE, n_cols=GL_A_OFF)
    zca = _matmul(uc, w_in, b_in, **gates)
    zcb = _matmul(uc, w_in, b_in, **rest) if need_ctx else None
    zeros = lambda hd, dk, dv: jnp.zeros((hd, dv, dk), jnp.float32)
    yc_hg, hg_sf, hg_sb = _hgrn2(zc, Lc, p['lb'], p['hg_norm_w'],
                                 zeros(HG_HEADS, HG_DK, HG_DV), zeros(HG_HEADS, HG_DK, HG_DV))
    yc_gl, gl_sf, gl_sb = _gla(zc, zca, Lc, p['gl_w_a2'], p['gl_b_a'], p['gl_norm_w'],
                               zeros(GL_HEADS, GL_DK, GL_DV), zeros(GL_HEADS, GL_DK, GL_DV))
    y_hg, _, _ = _hgrn2(z, L, p['lb'], p['hg_norm_w'], hg_sf, hg_sb)
    y_gl, _, _ = _gla(z, za, L, p['gl_w_a2'], p['gl_b_a'], p['gl_norm_w'], gl_sf, gl_sb)
    hy = (p['hy_conv_w'], p['hy_conv_b'], p['hy_f'], p['hy_skip'])
    y_hy = _hyena(zb, L, *hy)
    h = _merge(zb, L, (y_hy, y_hg, y_gl), p['w_branch'], p['w_out'], h, gt)
    if need_ctx:
        yc_hy = _hyena(zcb, Lc, *hy)
        hc = _merge(zcb, Lc, (yc_hy, yc_hg, yc_gl), p['w_branch'], p['w_out'], hc, gtc)
    return h, hc


GATHER_UNROLL = 8
GATHER_AHEAD = 2
GATHER_SLOTS = GATHER_AHEAD + 1


def _ffn_kernel(blk_exp_ref, n_used_ref, tok_ref, x_hbm, wg_ref, wu_ref, wd_ref, o_ref, xbuf, sem, wg_s, wu_s, wd_s):
    i = pl.program_id(0)
    n_used = n_used_ref[0]
    rows = o_ref.shape[0]

    def issue(step, slot):
        def body(r, carry):
            src = tok_ref[step * rows + r]
            pltpu.make_async_copy(x_hbm.at[pl.ds(src, 1)], xbuf.at[slot, pl.ds(r, 1)], sem.at[slot]).start()
            return carry

        lax.fori_loop(0, rows, body, 0, unroll=GATHER_UNROLL)

    for s in range(GATHER_AHEAD):
        @pl.when(jnp.logical_and(i == 0, s < n_used))
        def _(s=s):
            issue(s, s % GATHER_SLOTS)

    @pl.when(i + GATHER_AHEAD < n_used)
    def _():
        issue(i + GATHER_AHEAD, (i + GATHER_AHEAD) % GATHER_SLOTS)

    new_expert = jnp.logical_or(i == 0, blk_exp_ref[i] != blk_exp_ref[jnp.maximum(i - 1, 0)])

    @pl.when(jnp.logical_and(i < n_used, new_expert))
    def _():
        wg_s[...] = wg_ref[0].astype(jnp.bfloat16)
        wu_s[...] = wu_ref[0].astype(jnp.bfloat16)
        wd_s[...] = wd_ref[0].astype(jnp.bfloat16)

    @pl.when(i < n_used)
    def _():
        slot = i % GATHER_SLOTS
        pltpu.make_async_copy(xbuf.at[slot], xbuf.at[slot], sem.at[slot]).wait()
        x = xbuf[slot].astype(jnp.bfloat16)
        hg = jnp.dot(x, wg_s[...], preferred_element_type=jnp.float32)
        hu = jnp.dot(x, wu_s[...], preferred_element_type=jnp.float32)
        act = (hg * jax.nn.sigmoid(hg) * hu).astype(jnp.bfloat16)
        o_ref[...] = jnp.dot(act, wd_s[...], preferred_element_type=jnp.float32)

    @pl.when(i >= n_used)
    def _():
        o_ref[...] = jnp.zeros_like(o_ref)


def _grouped_ffn(x, buf_tok, blk_exp, n_used, layer, w_gate, w_up, w_down):
    p_len = buf_tok.shape[0]
    d = x.shape[1]
    n_blk = p_len // MOE_BLOCK
    grid_spec = pltpu.PrefetchScalarGridSpec(
        num_scalar_prefetch=3,
        grid=(n_blk,),
        in_specs=[pl.BlockSpec(memory_space=pl.ANY),
                  pl.BlockSpec((None, 1, d, D_FF), lambda i, be, nu, tk: (layer, be[i], 0, 0)),
                  pl.BlockSpec((None, 1, d, D_FF), lambda i, be, nu, tk: (layer, be[i], 0, 0)),
                  pl.BlockSpec((None, 1, D_FF, d), lambda i, be, nu, tk: (layer, be[i], 0, 0))],
        out_specs=pl.BlockSpec((MOE_BLOCK, d), lambda i, be, nu, tk: (i, 0)),
        scratch_shapes=[pltpu.VMEM((GATHER_SLOTS, MOE_BLOCK, d), jnp.float32), pltpu.SemaphoreType.DMA((GATHER_SLOTS,)),
                        pltpu.VMEM((d, D_FF), jnp.bfloat16), pltpu.VMEM((d, D_FF), jnp.bfloat16),
                        pltpu.VMEM((D_FF, d), jnp.bfloat16)],
    )
    return pl.pallas_call(
        _ffn_kernel,
        grid_spec=grid_spec,
        out_shape=jax.ShapeDtypeStruct((p_len, d), jnp.float32),
        compiler_params=_cparams("arbitrary"),
        name="moe_grouped_ffn",
    )(blk_exp, n_used, buf_tok, x, w_gate, w_up, w_down)


ROUTER_COLS = 128


ROUTE_TOKENS = 512
ROUTE_OUT = (0, 1, 2, 3, 4, 5)


def _route_kernel(lg_ref, tril_ref, o_ref, cnt_ref, carry):
    i = pl.program_id(0)

    @pl.when(i == 0)
    def _():
        carry[...] = jnp.zeros_like(carry)

    x = lg_ref[...]
    lane = lax.broadcasted_iota(jnp.int32, x.shape, 1).astype(jnp.float32)
    neg = jnp.float32(-jnp.inf)
    far = jnp.float32(ROUTER_COLS)
    red_max = lambda a: jnp.max(a, axis=1, keepdims=True)
    red_min = lambda a: jnp.min(a, axis=1, keepdims=True)
    red_sum = lambda a: jnp.sum(a, axis=1, keepdims=True)
    gmask = lane < N_GROUPS
    gl = jnp.where(gmask, x, neg)
    gmax = red_max(gl)
    p_top = 1.0 / red_sum(jnp.where(gmask, jnp.exp(x - gmax), 0.0))
    grp = red_min(jnp.where(gl == gmax, lane, far))
    lo = N_GROUPS + EXP_PER_GROUP * grp
    emask = jnp.logical_and(lane >= lo, lane < lo + EXP_PER_GROUP)
    el = jnp.where(emask, x, neg)
    ee = jnp.where(emask, jnp.exp(x - red_max(el)), 0.0)
    prob = ee / red_sum(ee)
    p1 = red_max(prob)
    i1 = red_min(jnp.where(jnp.logical_and(emask, prob == p1), lane, far))
    rest = jnp.where(jnp.logical_and(emask, lane != i1), prob, -1.0)
    p2 = red_max(rest)
    i2 = red_min(jnp.where(rest == p2, lane, far))
    w1 = p_top * p1 / (p1 + p2)
    w2 = p_top * p2 / (p1 + p2)
    pick1 = (lane == i1).astype(jnp.float32)
    pick2 = (lane == i2).astype(jnp.float32)
    picks = pick1 + pick2
    before = jnp.dot(tril_ref[...], picks.astype(jnp.bfloat16), preferred_element_type=jnp.float32) + carry[...]
    r1 = red_sum(pick1 * before)
    r2 = red_sum(pick2 * before)
    carry[...] += jnp.sum(picks, axis=0, keepdims=True)
    cnt_ref[...] = carry[...]
    out = jnp.zeros_like(x)
    for col, val in zip(ROUTE_OUT, (i1 - N_GROUPS, i2 - N_GROUPS, w1, w2, r1, r2)):
        out = jnp.where(lane == col, val, out)
    o_ref[...] = out


def _route(logits):
    assert TOP_K == 2
    n = logits.shape[0]
    t = min(ROUTE_TOKENS, n)
    tril = jnp.asarray(np.tril(np.ones((t, t), np.float32), -1), jnp.bfloat16)
    out, cnt = pl.pallas_call(
        _route_kernel,
        grid=(n // t,),
        in_specs=[pl.BlockSpec((t, ROUTER_COLS), lambda i: (i, 0)), pl.BlockSpec((t, t), lambda i: (0, 0))],
        out_specs=[pl.BlockSpec((t, ROUTER_COLS), lambda i: (i, 0)), pl.BlockSpec((1, ROUTER_COLS), lambda i: (0, 0))],
        out_shape=[jax.ShapeDtypeStruct((n, ROUTER_COLS), jnp.float32),
                   jax.ShapeDtypeStruct((1, ROUTER_COLS), jnp.float32)],
        scratch_shapes=[pltpu.VMEM((1, ROUTER_COLS), jnp.float32)],
        compiler_params=_cparams("arbitrary"),
        name="moe_route",
    )(logits, tril)
    expert = out[:, 0:2].astype(jnp.int32)
    weight = out[:, 2:4]
    rank = out[:, 4:6].astype(jnp.int32)
    counts = cnt[0, N_GROUPS:N_GROUPS + N_EXPERTS].astype(jnp.int32)
    return expert, weight, rank, counts


def _hier_moe(h, norm_w, shift, scale, gt, p, final_w=None):
    n, d = h.shape
    pad = ROUTER_COLS - N_GROUPS - N_EXPERTS
    w_r = _bf(jnp.concatenate([p['w_rg'], p['w_re'], jnp.zeros((d, pad), jnp.float32)], axis=1))
    b_r = jnp.concatenate([p['b_rg'], p['b_re'], jnp.zeros((pad,), jnp.float32)]).reshape(1, ROUTER_COLS)
    xb, logits = _norm_mod(h, norm_w, shift, scale, router=(w_r, b_r), out_dtype=jnp.float32)
    expert, weight, rank, counts = _route(logits)
    a = n * TOP_K
    padded = (counts + MOE_BLOCK - 1) // MOE_BLOCK * MOE_BLOCK
    pad_end = jnp.cumsum(padded)
    pad_off = pad_end - padded
    pos = (pad_off[expert] + rank).reshape(a)
    p_len = (a + N_EXPERTS * MOE_BLOCK + MOE_BLOCK - 1) // MOE_BLOCK * MOE_BLOCK
    n_blk = p_len // MOE_BLOCK
    tok_flat = jnp.arange(a, dtype=jnp.int32) // TOP_K
    buf_tok = (jnp.arange(p_len, dtype=jnp.int32) % n).at[pos].set(tok_flat)
    blk_start = jnp.arange(n_blk, dtype=jnp.int32) * MOE_BLOCK
    blk_exp = jnp.minimum(jnp.sum(pad_end[None, :] <= blk_start[:, None], axis=1), N_EXPERTS - 1).astype(jnp.int32)
    n_used = (pad_end[-1:] // MOE_BLOCK).astype(jnp.int32)
    y = _grouped_ffn(xb, buf_tok, blk_exp, n_used, p['layer'], p['w_gate'], p['w_up'], p['w_down'])
    return _moe_combine(y, pos, weight, h, gt, final_w)


def _combine_kernel(pos_ref, y_hbm, wts_ref, h_ref, gt_ref, fw_ref, o_ref, buf, sem, *, final):
    i = pl.program_id(0)
    tokens = h_ref.shape[0]

    def row_copy(step, slot, r, k):
        src = pos_ref[(step * tokens + r) * TOP_K + k]
        return pltpu.make_async_copy(y_hbm.at[pl.ds(src, 1)], buf.at[slot, k, pl.ds(r, 1)], sem.at[slot])

    def issue(step, slot):
        def body(r, carry):
            for k in range(TOP_K):
                row_copy(step, slot, r, k).start()
            return carry

        lax.fori_loop(0, tokens, body, 0, unroll=GATHER_UNROLL // TOP_K)

    @pl.when(i == 0)
    def _():
        issue(0, 0)

    @pl.when(i + 1 < pl.num_programs(0))
    def _():
        issue(i + 1, (i + 1) % 2)

    slot = i % 2
    pltpu.make_async_copy(buf.at[slot], buf.at[slot], sem.at[slot]).wait()
    rows = buf[slot]
    wts = wts_ref[...]
    acc = rows[0] * wts[:, 0:1]
    for k in range(1, TOP_K):
        acc = acc + rows[k] * wts[:, k:k + 1]
    out = h_ref[...] + gt_ref[...] * acc
    if final:
        out = out * lax.rsqrt(jnp.mean(out * out, axis=-1, keepdims=True) + NORM_EPS) * fw_ref[...]
    o_ref[...] = out


COMBINE_TOKENS = 256


def _moe_combine(y, pos, wts, h, gt, final_w=None):
    n, d = h.shape
    tokens = min(COMBINE_TOKENS, n)
    grid_spec = pltpu.PrefetchScalarGridSpec(
        num_scalar_prefetch=1,
        grid=(n // tokens,),
        in_specs=[pl.BlockSpec(memory_space=pl.ANY),
                  pl.BlockSpec((tokens, TOP_K), lambda i, pos: (i, 0)),
                  pl.BlockSpec((tokens, d), lambda i, pos: (i, 0)),
                  pl.BlockSpec((1, d), lambda i, pos: (0, 0)),
                  pl.BlockSpec((1, d), lambda i, pos: (0, 0))],
        out_specs=pl.BlockSpec((tokens, d), lambda i, pos: (i, 0)),
        scratch_shapes=[pltpu.VMEM((2, TOP_K, tokens, d), jnp.float32), pltpu.SemaphoreType.DMA((2,))],
    )
    return pl.pallas_call(
        functools.partial(_combine_kernel, final=final_w is not None),
        grid_spec=grid_spec,
        out_shape=jax.ShapeDtypeStruct((n, d), jnp.float32),
        compiler_params=_cparams("arbitrary"),
        name="moe_combine",
    )(pos, y, wts, h, gt, (gt if final_w is None else final_w.reshape(1, d)))


def kernel(x, c, ctx, c_ctx, w_mod, b_mod, norm_mix_w, norm_ffn_w, w_in, b_in, hy_conv_w, hy_conv_b, hy_f_w1, hy_f_b1, hy_f_w2, hy_f_b2, hy_f_w3, hy_f_b3, hy_f_freq, hy_skip, hg_lb_raw, hg_norm_w, gl_w_a2, gl_b_a, gl_norm_w, w_branch, w_out, w_rg, b_rg, w_re, b_re, w_gate, w_up, w_down, final_norm_w):
    assert x.shape[0] == 1
    depth = w_mod.shape[0]
    lb_all = jnp.cumsum(jax.nn.softmax(hg_lb_raw, axis=0), axis=0)
    lb_all = lb_all - lb_all[:1]
    h, hc = x[0], ctx[0]
    cc = jnp.concatenate([c, c_ctx[None, :]], axis=0)
    for l in range(depth):
        need_ctx = l < depth - 1
        mod = _matmul(_bf(jax.nn.silu(cc)), w_mod, b_mod[l], layer=l)
        sh1, sc1, gt1, sh2, sc2, gt2 = jnp.split(mod[0:1], 6, axis=-1)
        sh1c, sc1c, gt1c, sh2c, sc2c, gt2c = jnp.split(mod[1:2], 6, axis=-1)
        p = dict(w_in=w_in, b_in=b_in[l], hy_conv_w=hy_conv_w[l], hy_conv_b=hy_conv_b[l],
                 hy_f=(hy_f_w1[l], hy_f_b1[l], hy_f_w2[l], hy_f_b2[l], hy_f_w3[l], hy_f_b3[l], hy_f_freq[l]),
                 hy_skip=hy_skip[l], lb=lb_all[l], hg_norm_w=hg_norm_w[l], gl_w_a2=gl_w_a2[l], gl_b_a=gl_b_a[l],
                 gl_norm_w=gl_norm_w[l], w_branch=w_branch[l], w_out=w_out[l], w_rg=w_rg[l], b_rg=b_rg[l],
                 w_re=w_re[l], b_re=b_re[l], layer=l, w_gate=w_gate, w_up=w_up, w_down=w_down)
        u = _norm_mod(h, norm_mix_w[l], sh1, sc1)
        uc = _norm_mod(hc, norm_mix_w[l], sh1c, sc1c)
        h, hc = _mixer(h, hc, u, uc, gt1, gt1c, p, need_ctx)
        h = _hier_moe(h, norm_ffn_w[l], sh2, sc2, gt2, p, final_w=None if need_ctx else final_norm_w)
        if need_ctx:
            hc = _hier_moe(hc, norm_ffn_w[l], sh2c, sc2c, gt2c, p)
    return h[None]
```
